```python
import math, functools
import jax, jax.numpy as jnp
from jax import lax
import numpy as np

D_MODEL = 1024
BATCH = 8
SEQ = 8192
DEPTH = 4

N_MIXERS = 3
N_LAYERS_CONV = len(range(0, DEPTH, N_MIXERS))
N_LAYERS_FOX = len(range(1, DEPTH, N_MIXERS))
N_LAYERS_SSD = len(range(2, DEPTH, N_MIXERS))

RMS_EPS = 1e-6

D_FF = -(-8 * D_MODEL // (3 * 256)) * 256

CONV_WIDTH = 3

ATTN_HEAD_DIM = 64
ATTN_HEADS = D_MODEL // ATTN_HEAD_DIM
ATTN_WIDTH = ATTN_HEADS * ATTN_HEAD_DIM
Q_BLOCK = 128
FOX_IN = 3 * ATTN_WIDTH + ATTN_HEADS

SSM_EXPAND = 2
SSM_D_INNER = SSM_EXPAND * D_MODEL
SSM_HEAD_DIM = 64
SSM_HEADS = SSM_D_INNER // SSM_HEAD_DIM
SSM_GROUPS = 8
SSM_HEADS_PER_GROUP = SSM_HEADS // SSM_GROUPS
SSM_STATE = 128
SSM_CONV = 4
SSM_CHUNK = 128
SSM_CONV_DIM = SSM_D_INNER + 2 * SSM_GROUPS * SSM_STATE
SSM_IN = SSM_D_INNER + SSM_CONV_DIM + SSM_HEADS

kernel_name = "hybrid_conv_fox_ssd_trunk"


def rms_norm(x, w, eps=RMS_EPS):
    xf = x.astype(jnp.float32)
    y = xf * lax.rsqrt(jnp.mean(xf * xf, axis=-1, keepdims=True) + eps)
    return (y * w.astype(jnp.float32)).astype(x.dtype)


def causal_depthwise_conv(u, w):
    k_w = w.shape[0]
    seq = u.shape[1]
    up = jnp.pad(u, ((0, 0), (k_w - 1, 0), (0, 0)))
    out = up[:, 0:seq] * w[0]
    for k in range(1, k_w):
        out = out + up[:, k:k + seq] * w[k]
    return out


def swiglu_ffn(h, w_gu, w_down):
    g, u = jnp.split(h @ w_gu, 2, axis=-1)
    return (jax.nn.silu(g) * u) @ w_down


def short_conv_mixer(h, w_in, conv_w, w_out):
    b_gate, c_gate, v = jnp.split(h @ w_in, 3, axis=-1)
    u = causal_depthwise_conv(c_gate * v, conv_w)
    return (b_gate * u) @ w_out


def forgetting_attention(h, w_in, b_f, q_gain, k_gain, w_out):
    bsz, seq, _ = h.shape
    proj = h @ w_in
    q, k, v, f_logit = jnp.split(proj, [ATTN_WIDTH, 2 * ATTN_WIDTH, 3 * ATTN_WIDTH], axis=-1)
    q = rms_norm(q.reshape(bsz, seq, ATTN_HEADS, ATTN_HEAD_DIM), q_gain).astype(jnp.float32)
    k = rms_norm(k.reshape(bsz, seq, ATTN_HEADS, ATTN_HEAD_DIM), k_gain).astype(jnp.float32)
    v = v.reshape(bsz, seq, ATTN_HEADS, ATTN_HEAD_DIM).astype(jnp.float32)
    log_f = jax.nn.log_sigmoid((f_logit + b_f).astype(jnp.float32))
    cum = jnp.cumsum(log_f, axis=1)
    cum_k = jnp.transpose(cum, (0, 2, 1))[:, :, None, :]
    n_blk = seq // Q_BLOCK
    q_blk = jnp.moveaxis(q.reshape(bsz, n_blk, Q_BLOCK, ATTN_HEADS, ATTN_HEAD_DIM), 1, 0)
    cum_q = jnp.moveaxis(cum.reshape(bsz, n_blk, Q_BLOCK, ATTN_HEADS), 1, 0)
    key_pos = jnp.arange(seq)
    scale = ATTN_HEAD_DIM ** -0.5

    def attend(args):
        qb, cq, bi = args
        logits = jnp.einsum('bqhd,bkhd->bhqk', qb, k) * scale
        logits = logits + jnp.transpose(cq, (0, 2, 1))[..., None] - cum_k
        q_pos = bi * Q_BLOCK + jnp.arange(Q_BLOCK)
        causal = q_pos[:, None] >= key_pos[None, :]
        logits = jnp.where(causal, logits, -jnp.inf)
        p = jax.nn.softmax(logits, axis=-1)
        return jnp.einsum('bhqk,bkhd->bqhd', p, v)

    out = lax.map(attend, (q_blk, cum_q, jnp.arange(n_blk)))
    out = jnp.moveaxis(out, 0, 1).reshape(bsz, seq, ATTN_WIDTH).astype(h.dtype)
    return out @ w_out


def ssd_chunked(xs, dt, a, b_m, c_m):
    bsz, seq = xs.shape[:2]
    nc = seq // SSM_CHUNK
    L, G, K, P, N = SSM_CHUNK, SSM_GROUPS, SSM_HEADS_PER_GROUP, SSM_HEAD_DIM, SSM_STATE
    x = xs.reshape(bsz, nc, L, G, K, P)
    dtc = dt.reshape(bsz, nc, L, G, K)
    bc = b_m.reshape(bsz, nc, L, G, N)
    cc = c_m.reshape(bsz, nc, L, G, N)
    acum = jnp.cumsum(dtc * a.reshape(G, K), axis=2)
    seg = acum[:, :, :, None] - acum[:, :, None]
    mask = jnp.tril(jnp.ones((L, L), dtype=bool))[:, :, None, None]
    decay = jnp.exp(jnp.where(mask, seg, -jnp.inf))
    cb = jnp.einsum('bclgn,bcsgn->bclsg', cc, bc)
    w = cb[..., None] * decay * dtc[:, :, None]
    y_diag = jnp.einsum('bclsgk,bcsgkp->bclgkp', w, x)
    decay_states = jnp.exp(acum[:, :, -1:] - acum)
    states = jnp.einsum('bclgn,bclgk,bclgkp->bcgkpn', bc, decay_states * dtc, x)
    chunk_decay = jnp.exp(acum[:, :, -1])

    def step(hst, inp):
        st, dec = inp
        return dec[..., None, None] * hst + st, hst

    h0 = jnp.zeros((bsz, G, K, P, N), jnp.float32)
    _, prev = lax.scan(step, h0, (jnp.moveaxis(states, 1, 0), jnp.moveaxis(chunk_decay, 1, 0)))
    prev = jnp.moveaxis(prev, 0, 1)
    y_off = jnp.einsum('bclgn,bcgkpn,bclgk->bclgkp', cc, prev, jnp.exp(acum))
    return (y_diag + y_off).reshape(bsz, seq, SSM_HEADS, P)


def mamba2_mixer(h, w_in, conv_w, conv_b, dt_bias, a_log, d_skip, norm_w, w_out):
    bsz, seq, _ = h.shape
    proj = h @ w_in
    z, xbc, dt = jnp.split(proj, [SSM_D_INNER, SSM_D_INNER + SSM_CONV_DIM], axis=-1)
    xbc = jax.nn.silu(causal_depthwise_conv(xbc, conv_w) + conv_b)
    xs, b_m, c_m = jnp.split(xbc, [SSM_D_INNER, SSM_D_INNER + SSM_GROUPS * SSM_STATE], axis=-1)
    xs = xs.reshape(bsz, seq, SSM_HEADS, SSM_HEAD_DIM).astype(jnp.float32)
    b_m = b_m.reshape(bsz, seq, SSM_GROUPS, SSM_STATE).astype(jnp.float32)
    c_m = c_m.reshape(bsz, seq, SSM_GROUPS, SSM_STATE).astype(jnp.float32)
    dt = jax.nn.softplus((dt + dt_bias).astype(jnp.float32))
    a = -jnp.exp(a_log.astype(jnp.float32))
    y = ssd_chunked(xs, dt, a, b_m, c_m) + d_skip.astype(jnp.float32)[:, None] * xs
    y = y.reshape(bsz, seq, SSM_D_INNER) * jax.nn.silu(z.astype(jnp.float32))
    yg = y.reshape(bsz, seq, SSM_GROUPS, SSM_D_INNER // SSM_GROUPS)
    yg = yg * lax.rsqrt(jnp.mean(yg * yg, axis=-1, keepdims=True) + RMS_EPS)
    y = (yg.reshape(bsz, seq, SSM_D_INNER) * norm_w.astype(jnp.float32)).astype(h.dtype)
    return y @ w_out


def _fwd_setup_inputs(seed: int = 0) -> dict:
    key = jax.random.key(seed)
    ks = iter(jax.random.split(key, 32))
    f32 = jnp.float32

    def normal(shape, scale):
        return jax.random.normal(next(ks), shape, f32) * scale

    def gains(shape):
        return 1.0 + normal(shape, 0.02)

    out_scale = (2.0 * DEPTH) ** -0.5
    nC, nF, nS = N_LAYERS_CONV, N_LAYERS_FOX, N_LAYERS_SSD
    dt0 = jnp.exp(jax.random.uniform(next(ks), (nS, SSM_HEADS), f32, math.log(1e-3), math.log(1e-1)))
    return {
        "x": normal((BATCH, SEQ, D_MODEL), 1.0),
        "mix_norm": gains((DEPTH, D_MODEL)),
        "ffn_norm": gains((DEPTH, D_MODEL)),
        "ffn_w_gu": normal((DEPTH, D_MODEL, 2 * D_FF), D_MODEL ** -0.5),
        "ffn_w_down": normal((DEPTH, D_FF, D_MODEL), D_FF ** -0.5 * out_scale),
        "conv_w_in": normal((nC, D_MODEL, 3 * D_MODEL), D_MODEL ** -0.5),
        "conv_w_dw": normal((nC, CONV_WIDTH, D_MODEL), CONV_WIDTH ** -0.5),
        "conv_w_out": normal((nC, D_MODEL, D_MODEL), D_MODEL ** -0.5 * out_scale),
        "fox_w_in": normal((nF, D_MODEL, FOX_IN), D_MODEL ** -0.5),
        "fox_b_f": 2.0 + normal((nF, ATTN_HEADS), 0.5),
        "fox_q_gain": gains((nF, ATTN_HEAD_DIM)),
        "fox_k_gain": gains((nF, ATTN_HEAD_DIM)),
        "fox_w_out": normal((nF, ATTN_WIDTH, D_MODEL), ATTN_WIDTH ** -0.5 * out_scale),
        "ssd_w_in": normal((nS, D_MODEL, SSM_IN), D_MODEL ** -0.5),
        "ssd_conv_w": normal((nS, SSM_CONV, SSM_CONV_DIM), SSM_CONV ** -0.5),
        "ssd_conv_b": normal((nS, SSM_CONV_DIM), 0.02),
        "ssd_dt_bias": dt0 + jnp.log(-jnp.expm1(-dt0)),
        "ssd_a_log": jnp.log(jax.random.uniform(next(ks), (nS, SSM_HEADS), f32, 1.0, 16.0)),
        "ssd_d": gains((nS, SSM_HEADS)),
        "ssd_norm_w": gains((nS, SSM_D_INNER)),
        "ssd_w_out": normal((nS, SSM_D_INNER, D_MODEL), SSM_D_INNER ** -0.5 * out_scale),
    }


def _fwd_reference(x, mix_norm, ffn_norm, ffn_w_gu, ffn_w_down,
              conv_w_in, conv_w_dw, conv_w_out,
              fox_w_in, fox_b_f, fox_q_gain, fox_k_gain, fox_w_out,
              ssd_w_in, ssd_conv_w, ssd_conv_b, ssd_dt_bias, ssd_a_log, ssd_d, ssd_norm_w, ssd_w_out):
    for i in range(DEPTH):
        kind, j = i % N_MIXERS, i // N_MIXERS
        h = rms_norm(x, mix_norm[i])
        if kind == 0:
            m = short_conv_mixer(h, conv_w_in[j], conv_w_dw[j], conv_w_out[j])
        elif kind == 1:
            m = forgetting_attention(h, fox_w_in[j], fox_b_f[j], fox_q_gain[j], fox_k_gain[j], fox_w_out[j])
        else:
            m = mamba2_mixer(h, ssd_w_in[j], ssd_conv_w[j], ssd_conv_b[j], ssd_dt_bias[j],
                             ssd_a_log[j], ssd_d[j], ssd_norm_w[j], ssd_w_out[j])
        x = x + m
        x = x + swiglu_ffn(rms_norm(x, ffn_norm[i]), ffn_w_gu[i], ffn_w_down[i])
    return x


import jax as _jax
import jax.numpy as _jnp

TWIN_FORMAT = 'train_step'
FWD_PARAMS = ['x', 'mix_norm', 'ffn_norm', 'ffn_w_gu', 'ffn_w_down', 'conv_w_in', 'conv_w_dw', 'conv_w_out', 'fox_w_in', 'fox_b_f', 'fox_q_gain', 'fox_k_gain', 'fox_w_out', 'ssd_w_in', 'ssd_conv_w', 'ssd_conv_b', 'ssd_dt_bias', 'ssd_a_log', 'ssd_d', 'ssd_norm_w', 'ssd_w_out']
TWIN_WEIGHTS = ['mix_norm', 'ffn_norm', 'ffn_w_gu', 'ffn_w_down', 'conv_w_in', 'conv_w_dw', 'conv_w_out', 'fox_w_in', 'fox_b_f', 'fox_q_gain', 'fox_k_gain', 'fox_w_out', 'ssd_w_in', 'ssd_conv_w', 'ssd_conv_b', 'ssd_dt_bias', 'ssd_a_log', 'ssd_d', 'ssd_norm_w', 'ssd_w_out']
TWIN_DIFF_INPUT = 'x'
TWIN_INPUTS = ['x', 'mix_norm', 'ffn_norm', 'ffn_w_gu', 'ffn_w_down', 'conv_w_in', 'conv_w_dw', 'conv_w_out', 'fox_w_in', 'fox_b_f', 'fox_q_gain', 'fox_k_gain', 'fox_w_out', 'ssd_w_in', 'ssd_conv_w', 'ssd_conv_b', 'ssd_dt_bias', 'ssd_a_log', 'ssd_d', 'ssd_norm_w', 'ssd_w_out', 'loss_target', 'm_mix_norm', 'm_ffn_norm', 'm_ffn_w_gu', 'm_ffn_w_down', 'm_conv_w_in', 'm_conv_w_dw', 'm_conv_w_out', 'm_fox_w_in', 'm_fox_b_f', 'm_fox_q_gain', 'm_fox_k_gain', 'm_fox_w_out', 'm_ssd_w_in', 'm_ssd_conv_w', 'm_ssd_conv_b', 'm_ssd_dt_bias', 'm_ssd_a_log', 'm_ssd_d', 'm_ssd_norm_w', 'm_ssd_w_out', 'v_mix_norm', 'v_ffn_norm', 'v_ffn_w_gu', 'v_ffn_w_down', 'v_conv_w_in', 'v_conv_w_dw', 'v_conv_w_out', 'v_fox_w_in', 'v_fox_b_f', 'v_fox_q_gain', 'v_fox_k_gain', 'v_fox_w_out', 'v_ssd_w_in', 'v_ssd_conv_w', 'v_ssd_conv_b', 'v_ssd_dt_bias', 'v_ssd_a_log', 'v_ssd_d', 'v_ssd_norm_w', 'v_ssd_w_out']
TWIN_OUTPUTS = ['loss', 'grad_x', 'grad_mix_norm', 'grad_ffn_norm', 'grad_ffn_w_gu', 'grad_ffn_w_down', 'grad_conv_w_in', 'grad_conv_w_dw', 'grad_conv_w_out', 'grad_fox_w_in', 'grad_fox_b_f', 'grad_fox_q_gain', 'grad_fox_k_gain', 'grad_fox_w_out', 'grad_ssd_w_in', 'grad_ssd_conv_w', 'grad_ssd_conv_b', 'grad_ssd_dt_bias', 'grad_ssd_a_log', 'grad_ssd_d', 'grad_ssd_norm_w', 'grad_ssd_w_out', 'delta_mix_norm', 'delta_ffn_norm', 'delta_ffn_w_gu', 'delta_ffn_w_down', 'delta_conv_w_in', 'delta_conv_w_dw', 'delta_conv_w_out', 'delta_fox_w_in', 'delta_fox_b_f', 'delta_fox_q_gain', 'delta_fox_k_gain', 'delta_fox_w_out', 'delta_ssd_w_in', 'delta_ssd_conv_w', 'delta_ssd_conv_b', 'delta_ssd_dt_bias', 'delta_ssd_a_log', 'delta_ssd_d', 'delta_ssd_norm_w', 'delta_ssd_w_out', 'new_m_mix_norm', 'new_m_ffn_norm', 'new_m_ffn_w_gu', 'new_m_ffn_w_down', 'new_m_conv_w_in', 'new_m_conv_w_dw', 'new_m_conv_w_out', 'new_m_fox_w_in', 'new_m_fox_b_f', 'new_m_fox_q_gain', 'new_m_fox_k_gain', 'new_m_fox_w_out', 'new_m_ssd_w_in', 'new_m_ssd_conv_w', 'new_m_ssd_conv_b', 'new_m_ssd_dt_bias', 'new_m_ssd_a_log', 'new_m_ssd_d', 'new_m_ssd_norm_w', 'new_m_ssd_w_out', 'new_v_mix_norm', 'new_v_ffn_norm', 'new_v_ffn_w_gu', 'new_v_ffn_w_down', 'new_v_conv_w_in', 'new_v_conv_w_dw', 'new_v_conv_w_out', 'new_v_fox_w_in', 'new_v_fox_b_f', 'new_v_fox_q_gain', 'new_v_fox_k_gain', 'new_v_fox_w_out', 'new_v_ssd_w_in', 'new_v_ssd_conv_w', 'new_v_ssd_conv_b', 'new_v_ssd_dt_bias', 'new_v_ssd_a_log', 'new_v_ssd_d', 'new_v_ssd_norm_w', 'new_v_ssd_w_out']
TWIN_LEAF_KINDS = {'loss': 'loss', 'grad_x': 'grad_x', 'grad_mix_norm': 'grad_w', 'grad_ffn_norm': 'grad_w', 'grad_ffn_w_gu': 'grad_w', 'grad_ffn_w_down': 'grad_w', 'grad_conv_w_in': 'grad_w', 'grad_conv_w_dw': 'grad_w', 'grad_conv_w_out': 'grad_w', 'grad_fox_w_in': 'grad_w', 'grad_fox_b_f': 'grad_w', 'grad_fox_q_gain': 'grad_w', 'grad_fox_k_gain': 'grad_w', 'grad_fox_w_out': 'grad_w', 'grad_ssd_w_in': 'grad_w', 'grad_ssd_conv_w': 'grad_w', 'grad_ssd_conv_b': 'grad_w', 'grad_ssd_dt_bias': 'grad_w', 'grad_ssd_a_log': 'grad_w', 'grad_ssd_d': 'grad_w', 'grad_ssd_norm_w': 'grad_w', 'grad_ssd_w_out': 'grad_w', 'delta_mix_norm': 'delta_w', 'delta_ffn_norm': 'delta_w', 'delta_ffn_w_gu': 'delta_w', 'delta_ffn_w_down': 'delta_w', 'delta_conv_w_in': 'delta_w', 'delta_conv_w_dw': 'delta_w', 'delta_conv_w_out': 'delta_w', 'delta_fox_w_in': 'delta_w', 'delta_fox_b_f': 'delta_w', 'delta_fox_q_gain': 'delta_w', 'delta_fox_k_gain': 'delta_w', 'delta_fox_w_out': 'delta_w', 'delta_ssd_w_in': 'delta_w', 'delta_ssd_conv_w': 'delta_w', 'delta_ssd_conv_b': 'delta_w', 'delta_ssd_dt_bias': 'delta_w', 'delta_ssd_a_log': 'delta_w', 'delta_ssd_d': 'delta_w', 'delta_ssd_norm_w': 'delta_w', 'delta_ssd_w_out': 'delta_w', 'new_m_mix_norm': 'new_m', 'new_m_ffn_norm': 'new_m', 'new_m_ffn_w_gu': 'new_m', 'new_m_ffn_w_down': 'new_m', 'new_m_conv_w_in': 'new_m', 'new_m_conv_w_dw': 'new_m', 'new_m_conv_w_out': 'new_m', 'new_m_fox_w_in': 'new_m', 'new_m_fox_b_f': 'new_m', 'new_m_fox_q_gain': 'new_m', 'new_m_fox_k_gain': 'new_m', 'new_m_fox_w_out': 'new_m', 'new_m_ssd_w_in': 'new_m', 'new_m_ssd_conv_w': 'new_m', 'new_m_ssd_conv_b': 'new_m', 'new_m_ssd_dt_bias': 'new_m', 'new_m_ssd_a_log': 'new_m', 'new_m_ssd_d': 'new_m', 'new_m_ssd_norm_w': 'new_m', 'new_m_ssd_w_out': 'new_m', 'new_v_mix_norm': 'new_v', 'new_v_ffn_norm': 'new_v', 'new_v_ffn_w_gu': 'new_v', 'new_v_ffn_w_down': 'new_v', 'new_v_conv_w_in': 'new_v', 'new_v_conv_w_dw': 'new_v', 'new_v_conv_w_out': 'new_v', 'new_v_fox_w_in': 'new_v', 'new_v_fox_b_f': 'new_v', 'new_v_fox_q_gain': 'new_v', 'new_v_fox_k_gain': 'new_v', 'new_v_fox_w_out': 'new_v', 'new_v_ssd_w_in': 'new_v', 'new_v_ssd_conv_w': 'new_v', 'new_v_ssd_conv_b': 'new_v', 'new_v_ssd_dt_bias': 'new_v', 'new_v_ssd_a_log': 'new_v', 'new_v_ssd_d': 'new_v', 'new_v_ssd_norm_w': 'new_v', 'new_v_ssd_w_out': 'new_v'}


def _forward(args):
    return _fwd_reference(*[args[k] for k in FWD_PARAMS])


def _output_shape():
    def fwd():
        inp = _fwd_setup_inputs(0)
        return _fwd_reference(*[inp[k] for k in FWD_PARAMS])
    out = _jax.eval_shape(fwd)
    return out.shape, out.dtype

N_MICROBATCH = 1
ADAM_LR = 0.001
ADAM_B1 = 0.9
ADAM_B2 = 0.999
ADAM_EPS = 1e-08
ADAM_WD = 0.01
ADAM_STEP = 10
PER_EXAMPLE_BATCH_AXIS = {'x': 0, 'loss_target': 0}
SHARED_INPUTS = []
_WEIGHT_DTYPES = {'mix_norm': _jnp.float32, 'ffn_norm': _jnp.float32, 'ffn_w_gu': _jnp.float32, 'ffn_w_down': _jnp.float32, 'conv_w_in': _jnp.float32, 'conv_w_dw': _jnp.float32, 'conv_w_out': _jnp.float32, 'fox_w_in': _jnp.float32, 'fox_b_f': _jnp.float32, 'fox_q_gain': _jnp.float32, 'fox_k_gain': _jnp.float32, 'fox_w_out': _jnp.float32, 'ssd_w_in': _jnp.float32, 'ssd_conv_w': _jnp.float32, 'ssd_conv_b': _jnp.float32, 'ssd_dt_bias': _jnp.float32, 'ssd_a_log': _jnp.float32, 'ssd_d': _jnp.float32, 'ssd_norm_w': _jnp.float32, 'ssd_w_out': _jnp.float32}
MOMENT_SCALE = {'mix_norm': 1.674715e+01, 'ffn_norm': 6.116272e+00, 'ffn_w_gu': 8.380890e-02, 'ffn_w_down': 4.311204e-01, 'conv_w_in': 2.818998e-01, 'conv_w_dw': 4.499741e+00, 'conv_w_out': 7.479366e-01, 'fox_w_in': 8.987557e-02, 'fox_b_f': 2.462489e+01, 'fox_q_gain': 7.997877e+00, 'fox_k_gain': 7.981683e+00, 'fox_w_out': 3.185168e-01, 'ssd_w_in': 1.399199e-01, 'ssd_conv_w': 1.440802e-01, 'ssd_conv_b': 4.231959e-01, 'ssd_dt_bias': 3.577285e-01, 'ssd_a_log': 1.296877e+00, 'ssd_d': 8.801215e-01, 'ssd_norm_w': 5.387427e+00, 'ssd_w_out': 1.114843e+00}


def _to_microbatches(a, axis):
    t = _jnp.moveaxis(a, axis, 0)
    t = t.reshape((N_MICROBATCH, t.shape[0] // N_MICROBATCH) + t.shape[1:])
    return _jnp.moveaxis(t, 1, axis + 1)


def setup_inputs(seed: int = 0) -> dict:
    inp = _fwd_setup_inputs(seed)
    key = _jax.random.fold_in(_jax.random.key(seed), 7919)
    shape, _ = _output_shape()
    out = dict(inp)
    out["loss_target"] = _jax.random.normal(_jax.random.fold_in(key, 0), shape, _jnp.float32)
    for i, name in enumerate(TWIN_WEIGHTS):
        w = inp[name].astype(_jnp.float32)
        if MOMENT_SCALE is None:
            s = _jnp.sqrt(_jnp.mean(_jnp.square(w)) + 1e-30)
        else:
            s = MOMENT_SCALE[name]
        km, kv = _jax.random.split(_jax.random.fold_in(key, i + 1))
        out[name] = w
        out["m_" + name] = s * _jax.random.normal(km, w.shape, _jnp.float32)
        out["v_" + name] = (s * s) * _jax.random.uniform(kv, w.shape, _jnp.float32, 0.5, 1.5)
    if N_MICROBATCH > 1:
        for name, axis in PER_EXAMPLE_BATCH_AXIS.items():
            out[name] = _to_microbatches(out[name], axis)
    return {'x': out['x'], 'mix_norm': out['mix_norm'], 'ffn_norm': out['ffn_norm'], 'ffn_w_gu': out['ffn_w_gu'], 'ffn_w_down': out['ffn_w_down'], 'conv_w_in': out['conv_w_in'], 'conv_w_dw': out['conv_w_dw'], 'conv_w_out': out['conv_w_out'], 'fox_w_in': out['fox_w_in'], 'fox_b_f': out['fox_b_f'], 'fox_q_gain': out['fox_q_gain'], 'fox_k_gain': out['fox_k_gain'], 'fox_w_out': out['fox_w_out'], 'ssd_w_in': out['ssd_w_in'], 'ssd_conv_w': out['ssd_conv_w'], 'ssd_conv_b': out['ssd_conv_b'], 'ssd_dt_bias': out['ssd_dt_bias'], 'ssd_a_log': out['ssd_a_log'], 'ssd_d': out['ssd_d'], 'ssd_norm_w': out['ssd_norm_w'], 'ssd_w_out': out['ssd_w_out'], 'loss_target': out['loss_target'], 'm_mix_norm': out['m_mix_norm'], 'm_ffn_norm': out['m_ffn_norm'], 'm_ffn_w_gu': out['m_ffn_w_gu'], 'm_ffn_w_down': out['m_ffn_w_down'], 'm_conv_w_in': out['m_conv_w_in'], 'm_conv_w_dw': out['m_conv_w_dw'], 'm_conv_w_out': out['m_conv_w_out'], 'm_fox_w_in': out['m_fox_w_in'], 'm_fox_b_f': out['m_fox_b_f'], 'm_fox_q_gain': out['m_fox_q_gain'], 'm_fox_k_gain': out['m_fox_k_gain'], 'm_fox_w_out': out['m_fox_w_out'], 'm_ssd_w_in': out['m_ssd_w_in'], 'm_ssd_conv_w': out['m_ssd_conv_w'], 'm_ssd_conv_b': out['m_ssd_conv_b'], 'm_ssd_dt_bias': out['m_ssd_dt_bias'], 'm_ssd_a_log': out['m_ssd_a_log'], 'm_ssd_d': out['m_ssd_d'], 'm_ssd_norm_w': out['m_ssd_norm_w'], 'm_ssd_w_out': out['m_ssd_w_out'], 'v_mix_norm': out['v_mix_norm'], 'v_ffn_norm': out['v_ffn_norm'], 'v_ffn_w_gu': out['v_ffn_w_gu'], 'v_ffn_w_down': out['v_ffn_w_down'], 'v_conv_w_in': out['v_conv_w_in'], 'v_conv_w_dw': out['v_conv_w_dw'], 'v_conv_w_out': out['v_conv_w_out'], 'v_fox_w_in': out['v_fox_w_in'], 'v_fox_b_f': out['v_fox_b_f'], 'v_fox_q_gain': out['v_fox_q_gain'], 'v_fox_k_gain': out['v_fox_k_gain'], 'v_fox_w_out': out['v_fox_w_out'], 'v_ssd_w_in': out['v_ssd_w_in'], 'v_ssd_conv_w': out['v_ssd_conv_w'], 'v_ssd_conv_b': out['v_ssd_conv_b'], 'v_ssd_dt_bias': out['v_ssd_dt_bias'], 'v_ssd_a_log': out['v_ssd_a_log'], 'v_ssd_d': out['v_ssd_d'], 'v_ssd_norm_w': out['v_ssd_norm_w'], 'v_ssd_w_out': out['v_ssd_w_out']}


def _loss(weights, diff, rest, loss_target):
    with _jax.named_scope("forward"):
        args = {**rest, TWIN_DIFF_INPUT: diff, **{k: w.astype(_WEIGHT_DTYPES[k]) for k, w in weights.items()}}
        y = _forward(args)
    with _jax.named_scope("loss_head"):
        err = _jnp.square(y.astype(_jnp.float32) - loss_target)
        return 0.5 * _jnp.sum(_jnp.mean(err, axis=-1)) if err.ndim else 0.5 * err


def _adamw(w, g, m, v):
    m = ADAM_B1 * m + (1.0 - ADAM_B1) * g
    v = ADAM_B2 * v + (1.0 - ADAM_B2) * _jnp.square(g)
    m_hat = m / (1.0 - ADAM_B1 ** ADAM_STEP)
    v_hat = v / (1.0 - ADAM_B2 ** ADAM_STEP)
    delta = -ADAM_LR * (m_hat / (_jnp.sqrt(v_hat) + ADAM_EPS) + ADAM_WD * w)
    return delta, m, v


def reference(x, mix_norm, ffn_norm, ffn_w_gu, ffn_w_down, conv_w_in, conv_w_dw, conv_w_out, fox_w_in, fox_b_f, fox_q_gain, fox_k_gain, fox_w_out, ssd_w_in, ssd_conv_w, ssd_conv_b, ssd_dt_bias, ssd_a_log, ssd_d, ssd_norm_w, ssd_w_out, loss_target, m_mix_norm, m_ffn_norm, m_ffn_w_gu, m_ffn_w_down, m_conv_w_in, m_conv_w_dw, m_conv_w_out, m_fox_w_in, m_fox_b_f, m_fox_q_gain, m_fox_k_gain, m_fox_w_out, m_ssd_w_in, m_ssd_conv_w, m_ssd_conv_b, m_ssd_dt_bias, m_ssd_a_log, m_ssd_d, m_ssd_norm_w, m_ssd_w_out, v_mix_norm, v_ffn_norm, v_ffn_w_gu, v_ffn_w_down, v_conv_w_in, v_conv_w_dw, v_conv_w_out, v_fox_w_in, v_fox_b_f, v_fox_q_gain, v_fox_k_gain, v_fox_w_out, v_ssd_w_in, v_ssd_conv_w, v_ssd_conv_b, v_ssd_dt_bias, v_ssd_a_log, v_ssd_d, v_ssd_norm_w, v_ssd_w_out):
    given = dict(x=x, mix_norm=mix_norm, ffn_norm=ffn_norm, ffn_w_gu=ffn_w_gu, ffn_w_down=ffn_w_down, conv_w_in=conv_w_in, conv_w_dw=conv_w_dw, conv_w_out=conv_w_out, fox_w_in=fox_w_in, fox_b_f=fox_b_f, fox_q_gain=fox_q_gain, fox_k_gain=fox_k_gain, fox_w_out=fox_w_out, ssd_w_in=ssd_w_in, ssd_conv_w=ssd_conv_w, ssd_conv_b=ssd_conv_b, ssd_dt_bias=ssd_dt_bias, ssd_a_log=ssd_a_log, ssd_d=ssd_d, ssd_norm_w=ssd_norm_w, ssd_w_out=ssd_w_out, loss_target=loss_target, m_mix_norm=m_mix_norm, m_ffn_norm=m_ffn_norm, m_ffn_w_gu=m_ffn_w_gu, m_ffn_w_down=m_ffn_w_down, m_conv_w_in=m_conv_w_in, m_conv_w_dw=m_conv_w_dw, m_conv_w_out=m_conv_w_out, m_fox_w_in=m_fox_w_in, m_fox_b_f=m_fox_b_f, m_fox_q_gain=m_fox_q_gain, m_fox_k_gain=m_fox_k_gain, m_fox_w_out=m_fox_w_out, m_ssd_w_in=m_ssd_w_in, m_ssd_conv_w=m_ssd_conv_w, m_ssd_conv_b=m_ssd_conv_b, m_ssd_dt_bias=m_ssd_dt_bias, m_ssd_a_log=m_ssd_a_log, m_ssd_d=m_ssd_d, m_ssd_norm_w=m_ssd_norm_w, m_ssd_w_out=m_ssd_w_out, v_mix_norm=v_mix_norm, v_ffn_norm=v_ffn_norm, v_ffn_w_gu=v_ffn_w_gu, v_ffn_w_down=v_ffn_w_down, v_conv_w_in=v_conv_w_in, v_conv_w_dw=v_conv_w_dw, v_conv_w_out=v_conv_w_out, v_fox_w_in=v_fox_w_in, v_fox_b_f=v_fox_b_f, v_fox_q_gain=v_fox_q_gain, v_fox_k_gain=v_fox_k_gain, v_fox_w_out=v_fox_w_out, v_ssd_w_in=v_ssd_w_in, v_ssd_conv_w=v_ssd_conv_w, v_ssd_conv_b=v_ssd_conv_b, v_ssd_dt_bias=v_ssd_dt_bias, v_ssd_a_log=v_ssd_a_log, v_ssd_d=v_ssd_d, v_ssd_norm_w=v_ssd_norm_w, v_ssd_w_out=v_ssd_w_out)
    weights = {n: given[n] for n in TWIN_WEIGHTS}
    shared = {n: given[n] for n in SHARED_INPUTS}
    per_example = {n: given[n] for n in ['x']}
    grad_fn = _jax.value_and_grad(_loss, argnums=(0, 1))

    def one_microbatch(ex, loss_target):
        ex = dict(ex)
        diff = ex.pop(TWIN_DIFF_INPUT)
        return grad_fn(weights, diff, {**shared, **ex}, loss_target)

    if N_MICROBATCH == 1:
        loss, (grad_w, grad_x) = one_microbatch(per_example, given["loss_target"])
    else:
        def body(carry, xs):
            loss_sum, grad_sum = carry
            l_k, (gw_k, gx_k) = one_microbatch(xs[0], xs[1])
            with _jax.named_scope("update"):
                return (loss_sum + l_k, _jax.tree.map(_jnp.add, grad_sum, gw_k)), gx_k

        init = (_jnp.zeros((), _jnp.float32), _jax.tree.map(_jnp.zeros_like, weights))
        (loss, grad_w), grad_x = _jax.lax.scan(body, init, (per_example, given["loss_target"]))
    with _jax.named_scope("update"):
        delta_w, new_m, new_v = {}, {}, {}
        for n in TWIN_WEIGHTS:
            delta_w[n], new_m[n], new_v[n] = _adamw(weights[n], grad_w[n], given["m_" + n], given["v_" + n])
    return (loss, grad_x, *[grad_w[n] for n in TWIN_WEIGHTS], *[delta_w[n] for n in TWIN_WEIGHTS],
            *[new_m[n] for n in TWIN_WEIGHTS], *[new_v[n] for n in TWIN_WEIGHTS])
```

```python
import functools
import math

import jax
import jax.numpy as jnp
from jax import lax
from jax.experimental import pallas as pl
from jax.experimental.pallas import tpu as pltpu

F32 = jnp.float32
MM_DTYPE = jnp.bfloat16
ACT_DTYPE = jnp.bfloat16
WIRE_DTYPE = jnp.bfloat16

RMS_EPS = 1e-6
HEAD = 64
SSM_STATE = 128
SSM_CHUNK = 128
LANES = 128
SUBLANES = 8
VMEM_LIMIT = 48 * 1024 * 1024

ADAM_LR, ADAM_B1, ADAM_B2, ADAM_EPS, ADAM_WD, ADAM_STEP = 0.001, 0.9, 0.999, 1e-08, 0.01, 10

HI = lax.Precision.HIGHEST
MESH = pl.DeviceIdType.MESH


def _tile(dim, prefs):
    for p in prefs:
        if dim % p == 0:
            return p
    return dim


def _params(sem):
    return pltpu.CompilerParams(dimension_semantics=sem, vmem_limit_bytes=VMEM_LIMIT)


def _sigmoid(x):
    return 1.0 / (1.0 + jnp.exp(-x))


def _softplus(x):
    return jnp.maximum(x, 0.0) + jnp.log(1.0 + jnp.exp(-jnp.abs(x)))


def _mm(a, b, *, ta=False, tb=False, add=None, out_dtype=F32, name):
    ka, m = (a.shape[0], a.shape[1]) if ta else (a.shape[1], a.shape[0])
    kb, n = (b.shape[1], b.shape[0]) if tb else (b.shape[0], b.shape[1])
    assert ka == kb, (a.shape, b.shape, ta, tb)
    k = ka
    tm = _tile(m, (512, 256, 128))
    tn = _tile(n, (1024, 1408, 512, 256, 128))
    tk = _tile(k, (1024, 1408, 512, 256, 128))
    nk = k // tk
    a_spec = pl.BlockSpec((tk, tm), lambda i, j, q: (q, i)) if ta else pl.BlockSpec((tm, tk), lambda i, j, q: (i, q))
    b_spec = pl.BlockSpec((tn, tk), lambda i, j, q: (j, q)) if tb else pl.BlockSpec((tk, tn), lambda i, j, q: (q, j))
    o_spec = pl.BlockSpec((tm, tn), lambda i, j, q: (i, j))
    dims = (((0 if ta else 1,), (1 if tb else 0,)), ((), ()))
    has_add = add is not None

    def body(*refs):
        a_ref, b_ref = refs[0], refs[1]
        o_ref = refs[2 + has_add]
        p = lax.dot_general(a_ref[...].astype(MM_DTYPE), b_ref[...].astype(MM_DTYPE), dims, preferred_element_type=F32)

        def finish(acc):
            if has_add:
                acc = acc + refs[2][...].astype(F32)
            o_ref[...] = acc.astype(out_dtype)

        if nk == 1:
            finish(p)
        else:
            acc_ref = refs[3 + has_add]
            q = pl.program_id(2)

            @pl.when(q == 0)
            def _():
                acc_ref[...] = p

            @pl.when(q > 0)
            def _():
                acc_ref[...] += p

            @pl.when(q == nk - 1)
            def _():
                finish(acc_ref[...])

    args = [a, b] + ([add] if has_add else [])
    in_specs = [a_spec, b_spec] + ([o_spec] if has_add else [])
    return pl.pallas_call(
        body, name=name, grid=(m // tm, n // tn, nk), in_specs=in_specs, out_specs=o_spec,
        out_shape=jax.ShapeDtypeStruct((m, n), out_dtype),
        scratch_shapes=[pltpu.VMEM((tm, tn), F32)] if nk > 1 else [],
        compiler_params=_params(("parallel", "parallel", "arbitrary")),
    )(*args)


def _rows(fn, *, name, s, tm, ncol=1, ins, outs, accs=()):
    nrt = s // tm
    hb = tm // SUBLANES
    in_specs, args = [], []
    for spec in ins:
        kind, arr = spec[0], spec[1]
        if kind == "full":
            in_specs.append(pl.BlockSpec(arr.shape, lambda j, i: (0, 0)))
        elif kind == "col":
            _, _, bw, cmap = spec
            in_specs.append(pl.BlockSpec((arr.shape[0], bw), lambda j, i, cmap=cmap: (0, cmap(j))))
        elif kind == "row":
            _, _, bw, cmap = spec
            in_specs.append(pl.BlockSpec((tm, bw), lambda j, i, cmap=cmap: (i, cmap(j))))
        elif kind == "prev":
            _, _, bw, cmap = spec
            in_specs.append(pl.BlockSpec((SUBLANES, bw), lambda j, i, cmap=cmap: (jnp.maximum(i * hb - 1, 0), cmap(j))))
        elif kind == "next":
            _, _, bw, cmap = spec
            in_specs.append(pl.BlockSpec((SUBLANES, bw), lambda j, i, cmap=cmap: (jnp.minimum((i + 1) * hb, s // SUBLANES - 1), cmap(j))))
        else:
            raise ValueError(kind)
        args.append(arr)
    out_specs, out_shape = [], []
    for w, bw, cmap, dt in outs:
        out_specs.append(pl.BlockSpec((tm, bw), lambda j, i, cmap=cmap: (i, cmap(j))))
        out_shape.append(jax.ShapeDtypeStruct((s, w), dt))
    for r, w, bw, cmap in accs:
        out_specs.append(pl.BlockSpec((r, bw), lambda j, i, cmap=cmap: (0, cmap(j))))
        out_shape.append(jax.ShapeDtypeStruct((r, w), F32))
    n_in, n_out, n_acc = len(ins), len(outs), len(accs)

    def body(*refs):
        i = pl.program_id(1)
        vals = [r[...] for r in refs[:n_in]]
        o_vals, a_vals = fn(i, nrt, *vals)
        assert len(o_vals) == n_out and len(a_vals) == n_acc
        for r, v in zip(refs[n_in:n_in + n_out], o_vals):
            r[...] = v.astype(r.dtype)
        for r, v in zip(refs[n_in + n_out:], a_vals):
            @pl.when(i == 0)
            def _(r=r, v=v):
                r[...] = v.astype(F32)

            @pl.when(i > 0)
            def _(r=r, v=v):
                r[...] += v.astype(F32)

    res = pl.pallas_call(
        body, name=name, grid=(ncol, nrt), in_specs=in_specs, out_specs=out_specs, out_shape=out_shape,
        compiler_params=_params(("parallel", "arbitrary" if accs else "parallel")),
    )(*args)
    return res


def _c0(j):
    return 0


def _cj(j):
    return j


def _gmean(v, gs):
    w = v.shape[-1]
    tile = max(gs, LANES)
    r = lax.broadcasted_iota(jnp.int32, (tile, tile), 0) // gs
    c = lax.broadcasted_iota(jnp.int32, (tile, tile), 1) // gs
    g = jnp.where(r == c, 1.0 / gs, 0.0).astype(F32)
    parts = [jnp.dot(v[:, t * tile:(t + 1) * tile], g, precision=HI, preferred_element_type=F32) for t in range(w // tile)]
    return parts[0] if len(parts) == 1 else jnp.concatenate(parts, axis=1)


def _sum_rows(v):
    return jnp.sum(v, axis=0, keepdims=True)


def _resid_rms(x, y, w, *, name):
    s, d = x.shape
    has_y = y is not None

    def fn(i, nrt, *v):
        xv = v[0] + (v[1] if has_y else 0.0)
        wv = v[-1]
        r = lax.rsqrt(jnp.mean(xv * xv, axis=-1, keepdims=True) + RMS_EPS)
        return (xv, xv * r * wv), ()

    ins = [("row", x, d, _c0)] + ([("row", y, d, _c0)] if has_y else []) + [("full", w.reshape(1, d))]
    xn, h = _rows(fn, name=name, s=s, tm=_tile(s, (512, 256, 128)), ins=ins, outs=[(d, d, _c0, F32), (d, d, _c0, ACT_DTYPE)])
    return xn, h


def _rms_bwd(x, w, dh, dx_in, *, name):
    s, d = x.shape

    def fn(i, nrt, xv, wv, dhv, dxi):
        r = lax.rsqrt(jnp.mean(xv * xv, axis=-1, keepdims=True) + RMS_EPS)
        xh = xv * r
        g = dhv * wv
        dx = dxi + r * (g - xh * jnp.mean(g * xh, axis=-1, keepdims=True))
        return (dx,), (_sum_rows(dhv * xh),)

    dx, dw = _rows(fn, name=name, s=s, tm=_tile(s, (512, 256, 128)),
                   ins=[("row", x, d, _c0), ("full", w.reshape(1, d)), ("row", dh, d, _c0), ("row", dx_in, d, _c0)],
                   outs=[(d, d, _c0, F32)], accs=[(1, d, d, _c0)])
    return dx, dw.reshape(d)


def _col_tile(w):
    return _tile(w, (1408, 1024, 512, 256, 128))


def _swiglu_fwd(g, u, *, name):
    s, f = g.shape
    bw = _col_tile(f)

    def fn(i, nrt, gv, uv):
        gv = gv.astype(F32)
        return (gv * _sigmoid(gv) * uv.astype(F32),), ()

    (a,) = _rows(fn, name=name, s=s, tm=_tile(s, (512, 256, 128)), ncol=f // bw,
                 ins=[("row", g, bw, _cj), ("row", u, bw, _cj)], outs=[(f, bw, _cj, ACT_DTYPE)])
    return a


def _swiglu_bwd(g, u, da, *, name):
    s, f = g.shape
    bw = _col_tile(f)

    def fn(i, nrt, gv, uv, dav):
        gv, uv, dav = gv.astype(F32), uv.astype(F32), dav.astype(F32)
        sg = _sigmoid(gv)
        dg = dav * uv * sg * (1.0 + gv * (1.0 - sg))
        du = dav * gv * sg
        return (dg, du), ()

    dg, du = _rows(fn, name=name, s=s, tm=_tile(s, (512, 256, 128)), ncol=f // bw,
                   ins=[("row", g, bw, _cj), ("row", u, bw, _cj), ("row", da, bw, _cj)],
                   outs=[(f, bw, _cj, ACT_DTYPE), (f, bw, _cj, ACT_DTYPE)])
    return dg, du


def _loss_head(x, y, tgt, *, name):
    s, d = x.shape

    def fn(i, nrt, xv, yv, tv):
        diff = xv + yv - tv
        part = 0.5 * jnp.sum(diff * diff) / d
        return (diff / d,), (jnp.full((1, LANES), part, F32),)

    dy, loss = _rows(fn, name=name, s=s, tm=_tile(s, (512, 256, 128)),
                     ins=[("row", x, d, _c0), ("row", y, d, _c0), ("row", tgt, d, _c0)],
                     outs=[(d, d, _c0, F32)], accs=[(1, LANES, LANES, _c0)])
    return loss[0, 0], dy


def _shift_down(ext, j, tm):
    src = pltpu.roll(ext, j, 0) if j else ext
    return src[SUBLANES:SUBLANES + tm]


def _shift_up(ext, j, tm):
    return ext[:tm] if j == 0 else pltpu.roll(ext, ext.shape[0] - j, 0)[:tm]


def _gconv_fwd(b, c, v, w, *, name):
    s, d = b.shape
    kw = w.shape[0]
    bw = _tile(d, (512, 256, 128))
    tm = _tile(s, (512, 256, 128))

    def fn(i, nrt, bv, cv_, vv, pc, pv, wv):
        cv = cv_.astype(F32) * vv.astype(F32)
        pcv = jnp.where(i == 0, 0.0, pc.astype(F32) * pv.astype(F32))
        ext = jnp.concatenate([pcv, cv], axis=0)
        u = sum(wv[k:k + 1, :] * _shift_down(ext, kw - 1 - k, tm) for k in range(kw))
        return (bv.astype(F32) * u,), ()

    (o,) = _rows(fn, name=name, s=s, tm=tm, ncol=d // bw,
                 ins=[("row", b, bw, _cj), ("row", c, bw, _cj), ("row", v, bw, _cj), ("prev", c, bw, _cj), ("prev", v, bw, _cj),
                      ("col", w, bw, _cj)],
                 outs=[(d, bw, _cj, ACT_DTYPE)])
    return o


def _gconv_bwd(b, c, v, w, do, *, name):
    s, d = b.shape
    kw = w.shape[0]
    bw = _tile(d, (512, 256, 128))
    tm = _tile(s, (512, 256, 128))

    def fn(i, nrt, bv, cv_, vv, pc, pv, dov, nb, ndo, wv):
        bv, cv_, vv, dov = bv.astype(F32), cv_.astype(F32), vv.astype(F32), dov.astype(F32)
        cv = cv_ * vv
        pcv = jnp.where(i == 0, 0.0, pc.astype(F32) * pv.astype(F32))
        ext = jnp.concatenate([pcv, cv], axis=0)
        shifted = [_shift_down(ext, kw - 1 - k, tm) for k in range(kw)]
        u = sum(wv[k:k + 1, :] * shifted[k] for k in range(kw))
        db = dov * u
        du = dov * bv
        ndu = jnp.where(i == nrt - 1, 0.0, ndo.astype(F32) * nb.astype(F32))
        ext2 = jnp.concatenate([du, ndu], axis=0)
        dcv = sum(wv[k:k + 1, :] * _shift_up(ext2, kw - 1 - k, tm) for k in range(kw))
        dw = jnp.concatenate([_sum_rows(du * shifted[k]) for k in range(kw)], axis=0)
        return (db, dcv * vv, dcv * cv_), (dw,)

    db, dc, dv, dw = _rows(fn, name=name, s=s, tm=tm, ncol=d // bw,
                           ins=[("row", b, bw, _cj), ("row", c, bw, _cj), ("row", v, bw, _cj), ("prev", c, bw, _cj),
                                ("prev", v, bw, _cj), ("row", do, bw, _cj), ("next", b, bw, _cj), ("next", do, bw, _cj),
                                ("col", w, bw, _cj)],
                           outs=[(d, bw, _cj, ACT_DTYPE)] * 3, accs=[(kw, d, bw, _cj)])
    return db, dc, dv, dw


def _sconv_fwd(x, w, bias, *, name):
    s, d = x.shape
    kw = w.shape[0]
    bw = _tile(d, (512, 256, 128))
    tm = _tile(s, (512, 256, 128))

    def fn(i, nrt, xv, px, wv, bsv):
        xv = xv.astype(F32)
        ext = jnp.concatenate([jnp.where(i == 0, 0.0, px.astype(F32)), xv], axis=0)
        pre = sum(wv[k:k + 1, :] * _shift_down(ext, kw - 1 - k, tm) for k in range(kw)) + bsv
        return (pre * _sigmoid(pre),), ()

    (o,) = _rows(fn, name=name, s=s, tm=tm, ncol=d // bw,
                 ins=[("row", x, bw, _cj), ("prev", x, bw, _cj), ("col", w, bw, _cj), ("col", bias.reshape(1, d), bw, _cj)],
                 outs=[(d, bw, _cj, ACT_DTYPE)])
    return o


def _sconv_bwd(x, w, bias, dact, *, name):
    s, d = x.shape
    kw = w.shape[0]
    bw = _tile(d, (512, 256, 128))
    tm = _tile(s, (512, 256, 128))

    def fn(i, nrt, xv, px, nx, dav, nda, wv, bsv):
        xv = xv.astype(F32)
        ext = jnp.concatenate([jnp.where(i == 0, 0.0, px.astype(F32)), xv, nx.astype(F32)], axis=0)
        rows_e = tm + SUBLANES
        pre_e = sum(wv[k:k + 1, :] * _shift_down(ext, kw - 1 - k, rows_e) for k in range(kw)) + bsv
        da_e = jnp.concatenate([dav.astype(F32), jnp.where(i == nrt - 1, 0.0, nda.astype(F32))], axis=0)
        sg = _sigmoid(pre_e)
        dpre_e = da_e * sg * (1.0 + pre_e * (1.0 - sg))
        dx = sum(wv[k:k + 1, :] * _shift_up(dpre_e, kw - 1 - k, tm) for k in range(kw))
        dpre = dpre_e[:tm]
        dw = jnp.concatenate([_sum_rows(dpre * _shift_down(ext, kw - 1 - k, tm)) for k in range(kw)], axis=0)
        return (dx,), (dw, _sum_rows(dpre))

    dx, dw, db = _rows(fn, name=name, s=s, tm=tm, ncol=d // bw,
                       ins=[("row", x, bw, _cj), ("prev", x, bw, _cj), ("next", x, bw, _cj), ("row", dact, bw, _cj),
                            ("next", dact, bw, _cj), ("col", w, bw, _cj), ("col", bias.reshape(1, d), bw, _cj)],
                       outs=[(d, bw, _cj, ACT_DTYPE)], accs=[(kw, d, bw, _cj), (1, d, bw, _cj)])
    return dx, dw, db.reshape(d)


def _tri(n, reverse):
    r = lax.broadcasted_iota(jnp.int32, (n, n), 0)
    c = lax.broadcasted_iota(jnp.int32, (n, n), 1)
    return jnp.where((c >= r) if reverse else (c <= r), 1.0, 0.0).astype(F32)


def _cumsum_rows(x, *, reverse, name):
    s, w = x.shape
    ch = _tile(s, (256, 128))
    n = s // ch

    def body(x_ref, o_ref, carry):
        i = pl.program_id(0)

        @pl.when(i == 0)
        def _():
            carry[...] = jnp.zeros_like(carry)

        out = jnp.dot(_tri(ch, reverse), x_ref[...], precision=HI, preferred_element_type=F32) + carry[...]
        o_ref[...] = out
        carry[...] = out[0:1, :] if reverse else out[ch - 1:ch, :]

    imap = (lambda i: (n - 1 - i, 0)) if reverse else (lambda i: (i, 0))
    return pl.pallas_call(
        body, name=name, grid=(n,), in_specs=[pl.BlockSpec((ch, w), imap)], out_specs=pl.BlockSpec((ch, w), imap),
        out_shape=jax.ShapeDtypeStruct((s, w), F32), scratch_shapes=[pltpu.VMEM((1, w), F32)],
        compiler_params=_params(("arbitrary",)),
    )(x)


def _fox_prep(q, k, f, gq, gk, bf, *, name):
    s, d = q.shape
    scale = HEAD ** -0.5

    def fn(i, nrt, qv, kv, fv, gqv, gkv, bfv):
        qv, kv = qv.astype(F32), kv.astype(F32)
        qn = qv * lax.rsqrt(_gmean(qv * qv, HEAD) + RMS_EPS) * gqv * scale
        kn = kv * lax.rsqrt(_gmean(kv * kv, HEAD) + RMS_EPS) * gkv
        z = fv + bfv
        logf = jnp.minimum(z, 0.0) - jnp.log(1.0 + jnp.exp(-jnp.abs(z)))
        return (qn, kn, logf), ()

    return _rows(fn, name=name, s=s, tm=_tile(s, (512, 256, 128)),
                 ins=[("row", q, d, _c0), ("row", k, d, _c0), ("row", f, LANES, _c0), ("full", gq), ("full", gk), ("full", bf)],
                 outs=[(d, d, _c0, ACT_DTYPE), (d, d, _c0, ACT_DTYPE), (LANES, LANES, _c0, F32)])


def _fox_prep_bwd(q, k, f, gq, gk, bf, dqs, dkn, dlogf, *, name):
    s, d = q.shape
    scale = HEAD ** -0.5

    def fn(i, nrt, qv, kv, fv, gqv, gkv, bfv, dqv, dkv, dlf):
        outs, accs = [], []
        for xv, gv, dv, sc in ((qv, gqv, dqv, scale), (kv, gkv, dkv, 1.0)):
            xv, dv = xv.astype(F32), dv.astype(F32) * sc
            r = lax.rsqrt(_gmean(xv * xv, HEAD) + RMS_EPS)
            xh = xv * r
            g = dv * gv
            outs.append(r * (g - xh * _gmean(g * xh, HEAD)))
            accs.append(_sum_rows(dv * xh))
        z = fv + bfv
        df = dlf * _sigmoid(-z)
        outs.append(df)
        accs.append(_sum_rows(df))
        return outs, accs

    return _rows(fn, name=name, s=s, tm=_tile(s, (512, 256, 128)),
                 ins=[("row", q, d, _c0), ("row", k, d, _c0), ("row", f, LANES, _c0), ("full", gq), ("full", gk), ("full", bf),
                      ("row", dqs, d, _c0), ("row", dkn, d, _c0), ("row", dlogf, LANES, _c0)],
                 outs=[(d, d, _c0, ACT_DTYPE), (d, d, _c0, ACT_DTYPE), (LANES, LANES, _c0, ACT_DTYPE)],
                 accs=[(1, d, d, _c0), (1, d, d, _c0), (1, LANES, LANES, _c0)])


def _head_masks(shape):
    lane = lax.broadcasted_iota(jnp.int32, shape, len(shape) - 1)
    return lane < HEAD, lane >= HEAD


def _pick_lane(blk, idx):
    lane = lax.broadcasted_iota(jnp.int32, blk.shape, 1)
    return jnp.sum(jnp.where(lane == idx, blk, 0.0), axis=1, keepdims=True)


def _pick_row(blk, idx):
    sub = lax.broadcasted_iota(jnp.int32, blk.shape, 0)
    return jnp.sum(jnp.where(sub == idx, blk, 0.0), axis=0, keepdims=True)


def _fox_scores(qh, k2, cq, ck, qi, kj, bq):
    sc = lax.dot_general(qh, k2, (((1,), (1,)), ((), ())), preferred_element_type=F32)
    sc = sc + cq - ck
    row = qi * bq + lax.broadcasted_iota(jnp.int32, sc.shape, 0)
    col = kj * bq + lax.broadcasted_iota(jnp.int32, sc.shape, 1)
    return jnp.where(row >= col, sc, -jnp.inf)


def _fox_attn_fwd(qs, kn, v, cum, cum_t, *, name):
    s, d = qs.shape
    bq = _tile(s, (512, 256, 128))
    nq = s // bq
    hp = d // LANES
    nh = cum_t.shape[0]

    def body(q_ref, k_ref, v_ref, cq_ref, ck_ref, o_ref, lse_ref, m_sc, l_sc, acc_sc):
        p_, qi, kj = pl.program_id(0), pl.program_id(1), pl.program_id(2)

        @pl.when(kj == 0)
        def _():
            m_sc[...] = jnp.full_like(m_sc, -jnp.inf)
            l_sc[...] = jnp.zeros_like(l_sc)
            acc_sc[...] = jnp.zeros_like(acc_sc)

        @pl.when(kj <= qi)
        def _():
            q2, k2, v2 = q_ref[...], k_ref[...], v_ref[...]
            masks = _head_masks(q2.shape)
            for h in range(2):
                qh = jnp.where(masks[h], q2, jnp.zeros_like(q2))
                cq = _pick_lane(cq_ref[...], 2 * p_ + h)
                ck = _pick_row(ck_ref[...], 2 * p_ + h)
                sc = _fox_scores(qh, k2, cq, ck, qi, kj, bq)
                m_prev = m_sc[h]
                m_new = jnp.maximum(m_prev, jnp.max(sc, axis=1, keepdims=True))
                p = jnp.exp(sc - m_new)
                alpha = jnp.exp(m_prev - m_new)
                l_sc[h] = alpha * l_sc[h] + jnp.sum(p, axis=1, keepdims=True)
                acc_sc[h] = alpha * acc_sc[h] + jnp.dot(p.astype(MM_DTYPE), v2, preferred_element_type=F32)
                m_sc[h] = m_new

        @pl.when(kj == nq - 1)
        def _():
            lo, _hi = _head_masks((bq, LANES))
            o_ref[...] = jnp.where(lo, acc_sc[0] / l_sc[0], acc_sc[1] / l_sc[1]).astype(o_ref.dtype)
            lse_ref[...] = jnp.where(lo, m_sc[0] + jnp.log(l_sc[0]), m_sc[1] + jnp.log(l_sc[1]))

    kmap = lambda p_, qi, kj: (jnp.minimum(kj, qi), p_)
    o, lse = pl.pallas_call(
        body, name=name, grid=(hp, nq, nq),
        in_specs=[pl.BlockSpec((bq, LANES), lambda p_, qi, kj: (qi, p_)), pl.BlockSpec((bq, LANES), kmap), pl.BlockSpec((bq, LANES), kmap),
                  pl.BlockSpec((bq, LANES), lambda p_, qi, kj: (qi, 0)), pl.BlockSpec((nh, bq), lambda p_, qi, kj: (0, jnp.minimum(kj, qi)))],
        out_specs=[pl.BlockSpec((bq, LANES), lambda p_, qi, kj: (qi, p_))] * 2,
        out_shape=[jax.ShapeDtypeStruct((s, d), ACT_DTYPE), jax.ShapeDtypeStruct((s, d), F32)],
        scratch_shapes=[pltpu.VMEM((2, bq, 1), F32), pltpu.VMEM((2, bq, 1), F32), pltpu.VMEM((2, bq, LANES), F32)],
        compiler_params=_params(("parallel", "parallel", "arbitrary")),
    )(qs, kn, v, cum, cum_t)
    return o, lse


def _fox_probs(q2, k2, v2, do2, lse2, dd2, cq_blk, ck_blk, p_, h, qi, kj, bq, masks):
    qh = jnp.where(masks[h], q2, jnp.zeros_like(q2))
    doh = jnp.where(masks[h], do2, jnp.zeros_like(do2))
    cq = _pick_lane(cq_blk, 2 * p_ + h)
    ck = _pick_row(ck_blk, 2 * p_ + h)
    sc = _fox_scores(qh, k2, cq, ck, qi, kj, bq)
    lse = jnp.max(jnp.where(masks[h], lse2, -jnp.inf), axis=1, keepdims=True)
    dd = jnp.sum(jnp.where(masks[h], dd2, 0.0), axis=1, keepdims=True)
    p = jnp.exp(sc - lse)
    dp = lax.dot_general(doh, v2, (((1,), (1,)), ((), ())), preferred_element_type=F32)
    return p, p * (dp - dd)


def _fox_attn_bwd(qs, kn, v, cum, cum_t, o, lse, do, *, name):
    s, d = qs.shape
    bq = _tile(s, (512, 256, 128))
    nq = s // bq
    hp = d // LANES
    nh = cum_t.shape[0]

    def dq_body(q_ref, k_ref, v_ref, cq_ref, ck_ref, o_ref, lse_ref, do_ref, dq_ref, dcq_ref, acc_sc, rs_sc):
        p_, qi, kj = pl.program_id(0), pl.program_id(1), pl.program_id(2)

        @pl.when(kj == 0)
        def _():
            acc_sc[...] = jnp.zeros_like(acc_sc)
            rs_sc[...] = jnp.zeros_like(rs_sc)

        @pl.when(kj <= qi)
        def _():
            q2, k2, v2, do2 = q_ref[...], k_ref[...], v_ref[...], do_ref[...]
            masks = _head_masks(q2.shape)
            dd2 = do2.astype(F32) * o_ref[...].astype(F32)
            for h in range(2):
                _p, ds = _fox_probs(q2, k2, v2, do2, lse_ref[...], dd2, cq_ref[...], ck_ref[...], p_, h, qi, kj, bq, masks)
                acc_sc[h] += jnp.dot(ds.astype(MM_DTYPE), k2, preferred_element_type=F32)
                rs_sc[h] += jnp.sum(ds, axis=1, keepdims=True)

        @pl.when(kj == nq - 1)
        def _():
            lo, _hi = _head_masks((bq, LANES))
            dq_ref[...] = jnp.where(lo, acc_sc[0], acc_sc[1]).astype(dq_ref.dtype)
            dcq_ref[...] = jnp.where(lo, rs_sc[0], rs_sc[1])

    kmap = lambda p_, qi, kj: (jnp.minimum(kj, qi), p_)
    qmap = lambda p_, qi, kj: (qi, p_)
    blk = (bq, LANES)
    dq, dcq = pl.pallas_call(
        dq_body, name=name + "_dq", grid=(hp, nq, nq),
        in_specs=[pl.BlockSpec(blk, qmap), pl.BlockSpec(blk, kmap), pl.BlockSpec(blk, kmap),
                  pl.BlockSpec(blk, lambda p_, qi, kj: (qi, 0)), pl.BlockSpec((nh, bq), lambda p_, qi, kj: (0, jnp.minimum(kj, qi))),
                  pl.BlockSpec(blk, qmap), pl.BlockSpec(blk, qmap), pl.BlockSpec(blk, qmap)],
        out_specs=[pl.BlockSpec(blk, qmap)] * 2,
        out_shape=[jax.ShapeDtypeStruct((s, d), ACT_DTYPE), jax.ShapeDtypeStruct((s, d), F32)],
        scratch_shapes=[pltpu.VMEM((2, bq, LANES), F32), pltpu.VMEM((2, bq, 1), F32)],
        compiler_params=_params(("parallel", "parallel", "arbitrary")),
    )(qs, kn, v, cum, cum_t, o, lse, do)

    def dkv_body(q_ref, k_ref, v_ref, cq_ref, ck_ref, o_ref, lse_ref, do_ref, dk_ref, dv_ref, dc_ref, dk_sc, dv_sc, dc_sc):
        p_, kj, qq = pl.program_id(0), pl.program_id(1), pl.program_id(2)
        qi = qq

        @pl.when(qq == 0)
        def _():
            dk_sc[...] = jnp.zeros_like(dk_sc)
            dv_sc[...] = jnp.zeros_like(dv_sc)
            dc_sc[...] = jnp.zeros_like(dc_sc)

        @pl.when(qi >= kj)
        def _():
            q2, k2, v2, do2 = q_ref[...], k_ref[...], v_ref[...], do_ref[...]
            masks = _head_masks(q2.shape)
            dd2 = do2.astype(F32) * o_ref[...].astype(F32)
            for h in range(2):
                p, ds = _fox_probs(q2, k2, v2, do2, lse_ref[...], dd2, cq_ref[...], ck_ref[...], p_, h, qi, kj, bq, masks)
                dv_sc[h] += lax.dot_general(p.astype(MM_DTYPE), do2, (((0,), (0,)), ((), ())), preferred_element_type=F32)
                dk_sc[h] += lax.dot_general(ds.astype(MM_DTYPE), q2, (((0,), (0,)), ((), ())), preferred_element_type=F32)
                dc_sc[h] += jnp.sum(ds, axis=0, keepdims=True)

        @pl.when(qq == nq - 1)
        def _():
            lo, _hi = _head_masks((bq, LANES))
            dk_ref[...] = jnp.where(lo, dk_sc[0], dk_sc[1]).astype(dk_ref.dtype)
            dv_ref[...] = jnp.where(lo, dv_sc[0], dv_sc[1]).astype(dv_ref.dtype)
            sub = lax.broadcasted_iota(jnp.int32, (SUBLANES, bq), 0)
            dc_ref[...] = -jnp.where(sub == 0, dc_sc[0], jnp.where(sub == 1, dc_sc[1], 0.0))

    qmap2 = lambda p_, kj, qq: (jnp.maximum(qq, kj), p_)
    kmap2 = lambda p_, kj, qq: (kj, p_)
    dk, dv, dcum_t = pl.pallas_call(
        dkv_body, name=name + "_dkv", grid=(hp, nq, nq),
        in_specs=[pl.BlockSpec(blk, qmap2), pl.BlockSpec(blk, kmap2), pl.BlockSpec(blk, kmap2),
                  pl.BlockSpec(blk, lambda p_, kj, qq: (jnp.maximum(qq, kj), 0)), pl.BlockSpec((nh, bq), lambda p_, kj, qq: (0, kj)),
                  pl.BlockSpec(blk, qmap2), pl.BlockSpec(blk, qmap2), pl.BlockSpec(blk, qmap2)],
        out_specs=[pl.BlockSpec(blk, kmap2), pl.BlockSpec(blk, kmap2), pl.BlockSpec((None, SUBLANES, bq), lambda p_, kj, qq: (p_, 0, kj))],
        out_shape=[jax.ShapeDtypeStruct((s, d), ACT_DTYPE), jax.ShapeDtypeStruct((s, d), ACT_DTYPE),
                   jax.ShapeDtypeStruct((hp, SUBLANES, s), F32)],
        scratch_shapes=[pltpu.VMEM((2, bq, LANES), F32), pltpu.VMEM((2, bq, LANES), F32), pltpu.VMEM((2, 1, bq), F32)],
        compiler_params=_params(("parallel", "parallel", "arbitrary")),
    )(qs, kn, v, cum, cum_t, o, lse, do)
    return dq, dk, dv, dcum_t, dcq


def _expand_heads(v, nh, hd):
    r = lax.broadcasted_iota(jnp.int32, (LANES, nh * hd), 0)
    c = lax.broadcasted_iota(jnp.int32, (LANES, nh * hd), 1) // hd
    e = jnp.where(r == c, 1.0, 0.0).astype(F32)
    return jnp.dot(v, e, precision=HI, preferred_element_type=F32)


def _reduce_heads(v, nh, hd):
    r = lax.broadcasted_iota(jnp.int32, (nh * hd, LANES), 0) // hd
    c = lax.broadcasted_iota(jnp.int32, (nh * hd, LANES), 1)
    e = jnp.where(r == c, 1.0, 0.0).astype(F32)
    return jnp.dot(v, e, precision=HI, preferred_element_type=F32)


def _ssd_prep(dt_raw, dt_bias, a_log, nh, *, name):
    s = dt_raw.shape[0]

    def fn(i, nrt, dtr, bsv, alv):
        dt = _softplus(dtr + bsv)
        acum = jnp.dot(_tri(SSM_CHUNK, False), dt * (-jnp.exp(alv)), precision=HI, preferred_element_type=F32)
        return (dt, acum, _expand_heads(dt, nh, HEAD), _expand_heads(acum, nh, HEAD)), ()

    w = nh * HEAD
    return _rows(fn, name=name, s=s, tm=SSM_CHUNK, ins=[("row", dt_raw, LANES, _c0), ("full", dt_bias), ("full", a_log)],
                 outs=[(LANES, LANES, _c0, F32), (LANES, LANES, _c0, F32), (w, w, _c0, F32), (w, w, _c0, F32)])


def _ssd_prep_bwd(dt_raw, dt_bias, a_log, ddtx, dacx, nh, *, name):
    s = dt_raw.shape[0]

    def fn(i, nrt, dtr, bsv, alv, ddx, dax):
        z = dtr + bsv
        dt = _softplus(z)
        a = -jnp.exp(alv)
        dda = jnp.dot(_tri(SSM_CHUNK, True), _reduce_heads(dax, nh, HEAD), precision=HI, preferred_element_type=F32)
        ddt = _reduce_heads(ddx, nh, HEAD) + dda * a
        dz = ddt * _sigmoid(z)
        lane = lax.broadcasted_iota(jnp.int32, dz.shape, 1)
        dz = jnp.where(lane < nh, dz, 0.0)
        return (dz,), (_sum_rows(dz), _sum_rows(dda * dt) * a)

    w = nh * HEAD
    return _rows(fn, name=name, s=s, tm=SSM_CHUNK,
                 ins=[("row", dt_raw, LANES, _c0), ("full", dt_bias), ("full", a_log), ("row", ddtx, w, _c0), ("row", dacx, w, _c0)],
                 outs=[(LANES, LANES, _c0, ACT_DTYPE)], accs=[(1, LANES, LANES, _c0), (1, LANES, LANES, _c0)])


def _ssd_decay(ac_blk, act_blk, head):
    col = _pick_lane(ac_blk, head)
    row = _pick_row(act_blk, head)
    r = lax.broadcasted_iota(jnp.int32, (SSM_CHUNK, SSM_CHUNK), 0)
    c = lax.broadcasted_iota(jnp.int32, (SSM_CHUNK, SSM_CHUNK), 1)
    return jnp.exp(jnp.where(r >= c, col - row, -jnp.inf))


def _group_masks(shape, hpg):
    lane = lax.broadcasted_iota(jnp.int32, shape, len(shape) - 1) // HEAD
    return [lane == k for k in range(hpg)]


def _ssd_scan_fwd(xs, bm, cm, dtx, acx, acum, acum_t, d_x, *, name):
    s, di = xs.shape
    ng = bm.shape[1] // SSM_STATE
    gw = di // ng
    hpg = gw // HEAD
    nc = s // SSM_CHUNK
    L = SSM_CHUNK
    nh_pad = acum_t.shape[0]

    def body(x_ref, b_ref, c_ref, dt_ref, ax_ref, ac_ref, act_ref, d_ref, y_ref, st_ref, state):
        g, c = pl.program_id(0), pl.program_id(1)

        @pl.when(c == 0)
        def _():
            state[...] = jnp.zeros_like(state)

        x4, bv, cv = x_ref[...].astype(F32), b_ref[...], c_ref[...]
        ax = ax_ref[...]
        tx = (x4 * dt_ref[...])
        cb = lax.dot_general(cv, bv, (((1,), (1,)), ((), ())), preferred_element_type=F32)
        masks = _group_masks((L, gw), hpg)
        y = jnp.zeros((L, gw), F32)
        txb = tx.astype(MM_DTYPE)
        for k in range(hpg):
            wk = (cb * _ssd_decay(ac_ref[...], act_ref[...], g * hpg + k)).astype(MM_DTYPE)
            y = y + jnp.where(masks[k], jnp.dot(wk, txb, preferred_element_type=F32), 0.0)
        prev = state[...]
        st_ref[...] = prev
        y = y + jnp.dot(cv, prev.astype(MM_DTYPE), preferred_element_type=F32) * jnp.exp(ax)
        y = y + d_ref[...] * x4
        y_ref[...] = y.astype(y_ref.dtype)
        a_last = ax[L - 1:L, :]
        sx = (tx * jnp.exp(a_last - ax)).astype(MM_DTYPE)
        state[...] = prev * jnp.exp(a_last) + lax.dot_general(bv, sx, (((0,), (0,)), ((), ())), preferred_element_type=F32)

    y, states = pl.pallas_call(
        body, name=name, grid=(ng, nc),
        in_specs=[pl.BlockSpec((L, gw), lambda g, c: (c, g)), pl.BlockSpec((L, SSM_STATE), lambda g, c: (c, g)),
                  pl.BlockSpec((L, SSM_STATE), lambda g, c: (c, g)), pl.BlockSpec((L, gw), lambda g, c: (c, g)),
                  pl.BlockSpec((L, gw), lambda g, c: (c, g)), pl.BlockSpec((L, LANES), lambda g, c: (c, 0)),
                  pl.BlockSpec((nh_pad, L), lambda g, c: (0, c)), pl.BlockSpec((1, gw), lambda g, c: (0, g))],
        out_specs=[pl.BlockSpec((L, gw), lambda g, c: (c, g)), pl.BlockSpec((None, None, SSM_STATE, gw), lambda g, c: (g, c, 0, 0))],
        out_shape=[jax.ShapeDtypeStruct((s, di), ACT_DTYPE), jax.ShapeDtypeStruct((ng, nc, SSM_STATE, gw), F32)],
        scratch_shapes=[pltpu.VMEM((SSM_STATE, gw), F32)],
        compiler_params=_params(("parallel", "arbitrary")),
    )(xs, bm, cm, dtx, acx, acum, acum_t, d_x)
    return y, states


def _ssd_scan_bwd(xs, bm, cm, dtx, acx, acum, acum_t, d_x, states, dy, *, name):
    s, di = xs.shape
    ng = bm.shape[1] // SSM_STATE
    gw = di // ng
    hpg = gw // HEAD
    nc = s // SSM_CHUNK
    L = SSM_CHUNK
    nh_pad = acum_t.shape[0]

    def body(x_ref, b_ref, c_ref, dt_ref, ax_ref, ac_ref, act_ref, d_ref, st_ref, dy_ref,
             dx_ref, db_ref, dc_ref, ddt_ref, dax_ref, dd_ref, dstate):
        g, cc = pl.program_id(0), pl.program_id(1)

        @pl.when(cc == 0)
        def _():
            dstate[...] = jnp.zeros_like(dstate)
            dd_ref[...] = jnp.zeros_like(dd_ref)

        x4, bv, cv = x_ref[...].astype(F32), b_ref[...], c_ref[...]
        tv, ax, dyv = dt_ref[...], ax_ref[...], dy_ref[...].astype(F32)
        prev, dn = st_ref[...], dstate[...]
        dnb = dn.astype(MM_DTYPE)
        masks = _group_masks((L, gw), hpg)
        tx = x4 * tv
        txb = tx.astype(MM_DTYPE)
        e_ax = jnp.exp(ax)
        a_last = ax[L - 1:L, :]
        e_last = jnp.exp(a_last)
        ed = jnp.exp(a_last - ax)

        dx = d_ref[...] * dyv
        dd_ref[...] += _sum_rows(dyv * x4)
        dye = (dyv * e_ax).astype(MM_DTYPE)
        yo = jnp.dot(cv, prev.astype(MM_DTYPE), preferred_element_type=F32) * e_ax
        dc = lax.dot_general(dye, prev.astype(MM_DTYPE), (((1,), (1,)), ((), ())), preferred_element_type=F32)
        dprev = lax.dot_general(cv, dye, (((0,), (0,)), ((), ())), preferred_element_type=F32)
        dax = dyv * yo
        sx = tx * ed
        dsx = jnp.dot(bv, dnb, preferred_element_type=F32)
        db = lax.dot_general(sx.astype(MM_DTYPE), dnb, (((1,), (1,)), ((), ())), preferred_element_type=F32)
        dtx_ = dsx * ed
        dsx_sx = dsx * sx
        dax = dax - dsx_sx
        dlast = _sum_rows(dsx_sx) + _sum_rows(dn * prev) * e_last
        dprev = dprev + dn * e_last
        cb = lax.dot_general(cv, bv, (((1,), (1,)), ((), ())), preferred_element_type=F32)
        dcb = jnp.zeros((L, L), F32)
        lane = lax.broadcasted_iota(jnp.int32, (L, gw), 1)
        for k in range(hpg):
            dec = _ssd_decay(ac_ref[...], act_ref[...], g * hpg + k)
            wk = (cb * dec).astype(MM_DTYPE)
            dyk = jnp.where(masks[k], dyv, 0.0).astype(MM_DTYPE)
            dtx_ = dtx_ + jnp.where(masks[k], lax.dot_general(wk, dyk, (((0,), (0,)), ((), ())), preferred_element_type=F32), 0.0)
            dwk = lax.dot_general(dyk, txb, (((1,), (1,)), ((), ())), preferred_element_type=F32)
            dcb = dcb + dwk * dec
            mk = dwk * cb * dec
            da_k = jnp.sum(mk, axis=1, keepdims=True) - jnp.sum(mk.T, axis=1, keepdims=True)
            dax = dax + jnp.where(lane == k * HEAD, da_k, 0.0)
        dcbb = dcb.astype(MM_DTYPE)
        dc = dc + jnp.dot(dcbb, bv, preferred_element_type=F32)
        db = db + lax.dot_general(dcbb, cv, (((0,), (0,)), ((), ())), preferred_element_type=F32)
        sub = lax.broadcasted_iota(jnp.int32, (L, gw), 0)
        dax = dax + jnp.where(sub == L - 1, dlast, 0.0)
        dx_ref[...] = (dx + dtx_ * tv).astype(dx_ref.dtype)
        ddt_ref[...] = dtx_ * x4
        dax_ref[...] = dax
        db_ref[...] = db.astype(db_ref.dtype)
        dc_ref[...] = dc.astype(dc_ref.dtype)
        dstate[...] = dprev

    rev = lambda g, c: (nc - 1 - c, g)
    rev0 = lambda g, c: (nc - 1 - c, 0)
    outs = pl.pallas_call(
        body, name=name, grid=(ng, nc),
        in_specs=[pl.BlockSpec((L, gw), rev), pl.BlockSpec((L, SSM_STATE), rev), pl.BlockSpec((L, SSM_STATE), rev),
                  pl.BlockSpec((L, gw), rev), pl.BlockSpec((L, gw), rev), pl.BlockSpec((L, LANES), rev0),
                  pl.BlockSpec((nh_pad, L), lambda g, c: (0, nc - 1 - c)), pl.BlockSpec((1, gw), lambda g, c: (0, g)),
                  pl.BlockSpec((None, None, SSM_STATE, gw), lambda g, c: (g, nc - 1 - c, 0, 0)), pl.BlockSpec((L, gw), rev)],
        out_specs=[pl.BlockSpec((L, gw), rev), pl.BlockSpec((L, SSM_STATE), rev), pl.BlockSpec((L, SSM_STATE), rev),
                   pl.BlockSpec((L, gw), rev), pl.BlockSpec((L, gw), rev), pl.BlockSpec((1, gw), lambda g, c: (0, g))],
        out_shape=[jax.ShapeDtypeStruct((s, di), ACT_DTYPE), jax.ShapeDtypeStruct(bm.shape, ACT_DTYPE), jax.ShapeDtypeStruct(cm.shape, ACT_DTYPE),
                   jax.ShapeDtypeStruct((s, di), F32), jax.ShapeDtypeStruct((s, di), F32), jax.ShapeDtypeStruct((1, di), F32)],
        scratch_shapes=[pltpu.VMEM((SSM_STATE, gw), F32)],
        compiler_params=_params(("parallel", "arbitrary")),
    )(xs, bm, cm, dtx, acx, acum, acum_t, d_x, states, dy)
    return outs


def _ssd_gate(y, z, w, gs, *, name):
    s, d = y.shape

    def fn(i, nrt, yv, zv, wv):
        zv = zv.astype(F32)
        u = yv.astype(F32) * zv * _sigmoid(zv)
        return (u * lax.rsqrt(_gmean(u * u, gs) + RMS_EPS) * wv,), ()

    (o,) = _rows(fn, name=name, s=s, tm=_tile(s, (256, 128)), ins=[("row", y, d, _c0), ("row", z, d, _c0), ("full", w.reshape(1, d))],
                 outs=[(d, d, _c0, ACT_DTYPE)])
    return o


def _ssd_gate_bwd(y, z, w, do, gs, *, name):
    s, d = y.shape

    def fn(i, nrt, yv, zv, wv, dov):
        yv, zv, dov = yv.astype(F32), zv.astype(F32), dov.astype(F32)
        sg = _sigmoid(zv)
        sl = zv * sg
        u = yv * sl
        r = lax.rsqrt(_gmean(u * u, gs) + RMS_EPS)
        uh = u * r
        g = dov * wv
        du = r * (g - uh * _gmean(g * uh, gs))
        return (du * sl, du * yv * sg * (1.0 + zv * (1.0 - sg))), (_sum_rows(dov * uh),)

    dy, dz, dw = _rows(fn, name=name, s=s, tm=_tile(s, (256, 128)),
                       ins=[("row", y, d, _c0), ("row", z, d, _c0), ("full", w.reshape(1, d)), ("row", do, d, _c0)],
                       outs=[(d, d, _c0, ACT_DTYPE), (d, d, _c0, ACT_DTYPE)], accs=[(1, d, d, _c0)])
    return dy, dz, dw.reshape(d)


def _pad_lanes(w):
    return jnp.pad(w, ((0, 0), (0, LANES - w.shape[1])))


def _nt_sum(pairs, name):
    acc = None
    for a, b in pairs:
        acc = _mm(a, b, tb=True, add=acc, name=name)
    return acc


def _conv_mixer_fwd(h, w_in, w_dw, tag):
    d = h.shape[1]
    ws = [w_in[:, k * d:(k + 1) * d] for k in range(3)]
    b, c, v = [_mm(h, w, out_dtype=ACT_DTYPE, name=f"{tag}_in") for w in ws]
    return _gconv_fwd(b, c, v, w_dw, name=f"{tag}_gate"), (h, ws, b, c, v, w_dw)


def _conv_mixer_bwd(cache, do, tag):
    h, ws, b, c, v, w_dw = cache
    db, dc, dv, dw_dw = _gconv_bwd(b, c, v, w_dw, do, name=f"{tag}_gate_bwd")
    dps = (db, dc, dv)
    dw_in = jnp.concatenate([_mm(h, dp, ta=True, name=f"{tag}_dw_in") for dp in dps], axis=1)
    dh = _nt_sum(list(zip(dps, ws)), f"{tag}_dh")
    return dh, {"w_in": dw_in, "w_dw": dw_dw}


def _fox_mixer_fwd(h, w_in, b_f, q_gain, k_gain, tag):
    d = h.shape[1]
    nh = d // HEAD
    ws = [w_in[:, k * d:(k + 1) * d] for k in range(3)] + [_pad_lanes(w_in[:, 3 * d:])]
    q, k, v = [_mm(h, w, out_dtype=ACT_DTYPE, name=f"{tag}_in") for w in ws[:3]]
    f = _mm(h, ws[3], name=f"{tag}_in_f")
    gq = jnp.tile(q_gain, nh).reshape(1, d)
    gk = jnp.tile(k_gain, nh).reshape(1, d)
    bf = _pad_lanes(b_f.reshape(1, nh))
    qs, kn, logf = _fox_prep(q, k, f, gq, gk, bf, name=f"{tag}_prep")
    cum = _cumsum_rows(logf, reverse=False, name=f"{tag}_cum")
    cum_t = cum[:, :nh].T
    o, lse = _fox_attn_fwd(qs, kn, v, cum, cum_t, name=f"{tag}_attn")
    return o, (h, ws, q, k, v, f, gq, gk, bf, qs, kn, cum, cum_t, o, lse)


def _fox_mixer_bwd(cache, do, tag):
    h, ws, q, k, v, f, gq, gk, bf, qs, kn, cum, cum_t, o, lse = cache
    s, d = q.shape
    nh = d // HEAD
    dqs, dkn, dv, dcum_t, dcq = _fox_attn_bwd(qs, kn, v, cum, cum_t, o, lse, do, name=f"{tag}_attn_bwd")
    dcum = _pad_lanes(dcum_t[:, :2, :].reshape(nh, s).T + dcq[:, ::HEAD])
    dlogf = _cumsum_rows(dcum, reverse=True, name=f"{tag}_cum_bwd")
    dq, dk, df, dgq, dgk, dbf = _fox_prep_bwd(q, k, f, gq, gk, bf, dqs, dkn, dlogf, name=f"{tag}_prep_bwd")
    dps = (dq, dk, dv, df)
    dws = [_mm(h, dp, ta=True, name=f"{tag}_dw_in") for dp in dps]
    dw_in = jnp.concatenate(dws[:3] + [dws[3][:, :nh]], axis=1)
    dh = _nt_sum(list(zip(dps, ws)), f"{tag}_dh")
    return dh, {"w_in": dw_in, "b_f": dbf[0, :nh], "q_gain": dgq.reshape(nh, HEAD).sum(0), "k_gain": dgk.reshape(nh, HEAD).sum(0)}


def _ssd_mixer_fwd(h, w_in, conv_w, conv_b, dt_bias, a_log, d_skip, norm_w, tag):
    di = norm_w.shape[0]
    nh = di // HEAD
    gn = (conv_w.shape[1] - di) // 2
    cuts = [0, di, 2 * di, 2 * di + gn, 2 * di + 2 * gn]
    ws = [w_in[:, cuts[k]:cuts[k + 1]] for k in range(4)] + [_pad_lanes(w_in[:, cuts[4]:])]
    z, xr, br, cr = [_mm(h, w, out_dtype=ACT_DTYPE, name=f"{tag}_in") for w in ws[:4]]
    dtr = _mm(h, ws[4], name=f"{tag}_in_dt")
    ccuts = [0, di, di + gn, di + 2 * gn]
    cws = [conv_w[:, ccuts[k]:ccuts[k + 1]] for k in range(3)]
    cbs = [conv_b[ccuts[k]:ccuts[k + 1]] for k in range(3)]
    xs, bm, cm = [_sconv_fwd(r, w, b, name=f"{tag}_conv") for r, w, b in zip((xr, br, cr), cws, cbs)]
    dtb = _pad_lanes(dt_bias.reshape(1, nh))
    alg = _pad_lanes(a_log.reshape(1, nh))
    _dt, acum, dtx, acx = _ssd_prep(dtr, dtb, alg, nh, name=f"{tag}_prep")
    acum_t = acum[:, :nh].T
    d_x = jnp.repeat(d_skip, HEAD).reshape(1, di)
    y, states = _ssd_scan_fwd(xs, bm, cm, dtx, acx, acum, acum_t, d_x, name=f"{tag}_scan")
    gs = di // (gn // SSM_STATE)
    o = _ssd_gate(y, z, norm_w, gs, name=f"{tag}_gate")
    return o, (h, ws, z, (xr, br, cr), dtr, cws, cbs, xs, bm, cm, dtb, alg, dtx, acx, acum, acum_t, d_x, states, y, norm_w, gs, nh)


def _ssd_mixer_bwd(cache, do, tag):
    h, ws, z, raws, dtr, cws, cbs, xs, bm, cm, dtb, alg, dtx, acx, acum, acum_t, d_x, states, y, norm_w, gs, nh = cache
    dy, dz, dnorm = _ssd_gate_bwd(y, z, norm_w, do, gs, name=f"{tag}_gate_bwd")
    dxs, dbm, dcm, ddtx, dacx, dd_x = _ssd_scan_bwd(xs, bm, cm, dtx, acx, acum, acum_t, d_x, states, dy, name=f"{tag}_scan_bwd")
    ddtr, ddtb, dalg = _ssd_prep_bwd(dtr, dtb, alg, ddtx, dacx, nh, name=f"{tag}_prep_bwd")
    conv = [_sconv_bwd(r, w, b, da, name=f"{tag}_conv_bwd") for r, w, b, da in zip(raws, cws, cbs, (dxs, dbm, dcm))]
    dps = (dz, conv[0][0], conv[1][0], conv[2][0], ddtr)
    dws = [_mm(h, dp, ta=True, name=f"{tag}_dw_in") for dp in dps]
    dw_in = jnp.concatenate(dws[:4] + [dws[4][:, :nh]], axis=1)
    dh = _nt_sum(list(zip(dps, ws)), f"{tag}_dh")
    return dh, {"w_in": dw_in, "conv_w": jnp.concatenate([c[1] for c in conv], axis=1), "conv_b": jnp.concatenate([c[2] for c in conv]),
                "dt_bias": ddtb[0, :nh], "a_log": dalg[0, :nh], "d": dd_x.reshape(nh, HEAD).sum(1), "norm_w": dnorm}


def _local_step(x, tgt, fw):
    depth = fw["mix_norm"].shape[0]
    layers = []
    xc, y_prev = x, None
    for i in range(depth):
        kind, j = i % 3, i // 3
        tag = f"l{i}"
        xin, h = _resid_rms(xc, y_prev, fw["mix_norm"][i], name=f"{tag}_norm1")
        if kind == 0:
            o, mc = _conv_mixer_fwd(h, fw["conv_w_in"][j], fw["conv_w_dw"][j], tag + "_conv")
            w_out = fw["conv_w_out"][j]
        elif kind == 1:
            o, mc = _fox_mixer_fwd(h, fw["fox_w_in"][j], fw["fox_b_f"][j], fw["fox_q_gain"][j], fw["fox_k_gain"][j], tag + "_fox")
            w_out = fw["fox_w_out"][j]
        else:
            o, mc = _ssd_mixer_fwd(h, fw["ssd_w_in"][j], fw["ssd_conv_w"][j], fw["ssd_conv_b"][j], fw["ssd_dt_bias"][j], fw["ssd_a_log"][j],
                                   fw["ssd_d"][j], fw["ssd_norm_w"][j], tag + "_ssd")
            w_out = fw["ssd_w_out"][j]
        ym = _mm(o, w_out, name=f"{tag}_mix_out")
        x1, h2 = _resid_rms(xin, ym, fw["ffn_norm"][i], name=f"{tag}_norm2")
        f = fw["ffn_w_down"].shape[1]
        wg, wu = fw["ffn_w_gu"][i][:, :f], fw["ffn_w_gu"][i][:, f:]
        g = _mm(h2, wg, out_dtype=ACT_DTYPE, name=f"{tag}_ffn_g")
        u = _mm(h2, wu, out_dtype=ACT_DTYPE, name=f"{tag}_ffn_u")
        a = _swiglu_fwd(g, u, name=f"{tag}_swiglu")
        yf = _mm(a, fw["ffn_w_down"][i], name=f"{tag}_ffn_down")
        layers.append((xin, o, mc, w_out, x1, h2, wg, wu, g, u, a))
        xc, y_prev = x1, yf
    loss, dx = _loss_head(xc, y_prev, tgt, name="loss_head")

    names = ("conv", "fox", "ssd")
    grads = {k: [None] * v.shape[0] for k, v in fw.items()}
    for i in reversed(range(depth)):
        kind, j = i % 3, i // 3
        tag = f"l{i}"
        xin, o, mc, w_out, x1, h2, wg, wu, g, u, a = layers[i]
        grads["ffn_w_down"][i] = _mm(a, dx, ta=True, name=f"{tag}_dw_down")
        da = _mm(dx, fw["ffn_w_down"][i], tb=True, out_dtype=ACT_DTYPE, name=f"{tag}_da")
        dg, du = _swiglu_bwd(g, u, da, name=f"{tag}_swiglu_bwd")
        grads["ffn_w_gu"][i] = jnp.concatenate([_mm(h2, dg, ta=True, name=f"{tag}_dw_g"), _mm(h2, du, ta=True, name=f"{tag}_dw_u")], axis=1)
        dh2 = _nt_sum([(dg, wg), (du, wu)], f"{tag}_dh2")
        dx1, grads["ffn_norm"][i] = _rms_bwd(x1, fw["ffn_norm"][i], dh2, dx, name=f"{tag}_norm2_bwd")
        grads[names[kind] + "_w_out"][j] = _mm(o, dx1, ta=True, name=f"{tag}_dw_out")
        do = _mm(dx1, w_out, tb=True, out_dtype=ACT_DTYPE, name=f"{tag}_do")
        if kind == 0:
            dh, mg = _conv_mixer_bwd(mc, do, tag + "_conv")
        elif kind == 1:
            dh, mg = _fox_mixer_bwd(mc, do, tag + "_fox")
        else:
            dh, mg = _ssd_mixer_bwd(mc, do, tag + "_ssd")
        for k, v in mg.items():
            grads[f"{names[kind]}_{k}"][j] = v
        dx, grads["mix_norm"][i] = _rms_bwd(xin, fw["mix_norm"][i], dh, dx1, name=f"{tag}_norm1_bwd")
    return loss, dx, {k: jnp.stack(v) for k, v in grads.items()}


ANY = pl.BlockSpec(memory_space=pl.ANY)
VMEM_SPEC = pl.BlockSpec(memory_space=pltpu.VMEM)


def _place():
    return lax.axis_index("x"), lax.axis_index("y"), lax.axis_index("c")


def _remote(src, dst, send_sems, recv_sems, k, to):
    return pltpu.make_async_remote_copy(src_ref=src, dst_ref=dst, send_sem=send_sems.at[k], recv_sem=recv_sems.at[k],
                                        device_id=to, device_id_type=MESH)


def _gather_big(wp, *, name):
    r, w = wp.shape
    rh = r // 2

    def body(w_ref, o_ref, send_sems, recv_sems, local_sem):
        x, y, c = _place()
        me, sibling, m = (x, y, c), (x, y, 1 - c), 2 * x + y
        chips = [(1 - x, y), (x, 1 - y), (1 - x, 1 - y)]
        mine = w_ref.at[pl.ds(pl.multiple_of(c * rh, 16), rh)]
        local = pltpu.make_async_copy(mine, o_ref.at[c, m], local_sem)
        local.start()
        first = [_remote(mine, o_ref.at[c, m], send_sems, recv_sems, 0, sibling)]
        first += [_remote(mine, o_ref.at[c, m], send_sems, recv_sems, 1 + j, (px, py, c)) for j, (px, py) in enumerate(chips)]
        for cp in first:
            cp.start()
        passed = []
        for j, (px, py) in enumerate(chips):
            blk = o_ref.at[c, 2 * px + py]
            _remote(blk, blk, send_sems, recv_sems, 1 + j, me).wait_recv()
            fwd = _remote(blk, blk, send_sems, recv_sems, 4 + j, sibling)
            fwd.start()
            passed.append(fwd)
        _remote(o_ref.at[1 - c, m], o_ref.at[1 - c, m], send_sems, recv_sems, 0, me).wait_recv()
        for j, (px, py) in enumerate(chips):
            blk = o_ref.at[1 - c, 2 * px + py]
            _remote(blk, blk, send_sems, recv_sems, 4 + j, me).wait_recv()
        for cp in first + passed:
            cp.wait_send()
        local.wait()

    return pl.pallas_call(
        body, name=name, in_specs=[ANY], out_specs=ANY, out_shape=jax.ShapeDtypeStruct((2, 4, rh, w), wp.dtype),
        scratch_shapes=[pltpu.SemaphoreType.DMA((7,)), pltpu.SemaphoreType.DMA((7,)), pltpu.SemaphoreType.DMA],
    )(wp)


def _gather_small(v, *, name):
    r, w = v.shape

    def body(v_ref, o_ref, send_sems, recv_sems):
        x, y, c = _place()
        m = 2 * x + y
        chips = [(1 - x, y), (x, 1 - y), (1 - x, 1 - y)]
        o_ref[m] = v_ref[...]
        sends = [_remote(v_ref, o_ref.at[m], send_sems, recv_sems, j, (px, py, c)) for j, (px, py) in enumerate(chips)]
        for cp in sends:
            cp.start()
        for j, (px, py) in enumerate(chips):
            blk = o_ref.at[2 * px + py]
            _remote(blk, blk, send_sems, recv_sems, j, (x, y, c)).wait_recv()
        for cp in sends:
            cp.wait_send()

    return pl.pallas_call(
        body, name=name, in_specs=[VMEM_SPEC], out_specs=VMEM_SPEC, out_shape=jax.ShapeDtypeStruct((4, r, w), v.dtype),
        scratch_shapes=[pltpu.SemaphoreType.DMA((3,)), pltpu.SemaphoreType.DMA((3,))],
    )(v)


def _swap_sibling(a, *, pick_other_half, name):
    shape = a.shape[1:] if pick_other_half else a.shape

    def body(a_ref, o_ref, send_sem, recv_sem):
        x, y, c = _place()
        src = a_ref.at[1 - c] if pick_other_half else a_ref
        cp = pltpu.make_async_remote_copy(src_ref=src, dst_ref=o_ref, send_sem=send_sem, recv_sem=recv_sem,
                                          device_id=(x, y, 1 - c), device_id_type=MESH)
        cp.start()
        cp.wait()

    return pl.pallas_call(
        body, name=name, in_specs=[ANY], out_specs=ANY, out_shape=jax.ShapeDtypeStruct(shape, a.dtype),
        scratch_shapes=[pltpu.SemaphoreType.DMA, pltpu.SemaphoreType.DMA],
    )(a)


def _scatter_chips(a, *, name):
    def body(a_ref, o_ref, send_sems, recv_sems, local_sem):
        x, y, c = _place()
        m = 2 * x + y
        chips = [(1 - x, y), (x, 1 - y), (1 - x, 1 - y)]
        local = pltpu.make_async_copy(a_ref.at[m], o_ref.at[m], local_sem)
        local.start()
        sends = [_remote(a_ref.at[2 * px + py], o_ref.at[m], send_sems, recv_sems, j, (px, py, c)) for j, (px, py) in enumerate(chips)]
        for cp in sends:
            cp.start()
        for j, (px, py) in enumerate(chips):
            blk = o_ref.at[2 * px + py]
            _remote(blk, blk, send_sems, recv_sems, j, (x, y, c)).wait_recv()
        for cp in sends:
            cp.wait_send()
        local.wait()

    return pl.pallas_call(
        body, name=name, in_specs=[ANY], out_specs=ANY, out_shape=jax.ShapeDtypeStruct(a.shape, a.dtype),
        scratch_shapes=[pltpu.SemaphoreType.DMA((3,)), pltpu.SemaphoreType.DMA((3,)), pltpu.SemaphoreType.DMA],
    )(a)


def _allreduce_small(v, *, name):
    r, w = v.shape

    def body(v_ref, o_ref, slots, send_sems, recv_sems):
        x, y, c = _place()
        me = 4 * x + 2 * y + c
        slots[me] = v_ref[...]
        peers = [((1 - x) if k & 4 else x, (1 - y) if k & 2 else y, (1 - c) if k & 1 else c) for k in range(1, 8)]
        sends = [_remote(v_ref, slots.at[me], send_sems, recv_sems, k, p) for k, p in enumerate(peers)]
        for cp in sends:
            cp.start()
        for k, (px, py, pc) in enumerate(peers):
            blk = slots.at[4 * px + 2 * py + pc]
            _remote(blk, blk, send_sems, recv_sems, k, (x, y, c)).wait_recv()
        for cp in sends:
            cp.wait_send()
        acc = slots[0]
        for k in range(1, 8):
            acc = acc + slots[k]
        o_ref[...] = acc

    return pl.pallas_call(
        body, name=name, in_specs=[VMEM_SPEC], out_specs=VMEM_SPEC, out_shape=jax.ShapeDtypeStruct(v.shape, v.dtype),
        scratch_shapes=[pltpu.VMEM((8, r, w), F32), pltpu.SemaphoreType.DMA((7,)), pltpu.SemaphoreType.DMA((7,))],
    )(v)


def _add2(a, b, *, name):
    s, w = a.shape

    def fn(i, nrt, av, bv):
        return (av.astype(F32) + bv.astype(F32),), ()

    (o,) = _rows(fn, name=name, s=s, tm=_tile(s, (512, 256, 128, 64, 32, 16)), ins=[("row", a, w, _c0), ("row", b, w, _c0)],
                 outs=[(w, w, _c0, a.dtype)])
    return o


def _sum_blocks(a, *, name):
    n, r, w = a.shape
    tm = _tile(r, (512, 256, 128, 64, 32, 16))

    def body(a_ref, o_ref):
        acc = a_ref[0].astype(F32)
        for k in range(1, n):
            acc = acc + a_ref[k].astype(F32)
        o_ref[...] = acc

    return pl.pallas_call(
        body, name=name, grid=(r // tm,), in_specs=[pl.BlockSpec((n, tm, w), lambda i: (0, i, 0))],
        out_specs=pl.BlockSpec((tm, w), lambda i: (i, 0)), out_shape=jax.ShapeDtypeStruct((r, w), F32),
        compiler_params=_params(("parallel",)),
    )(a)


def _adamw(w, g, m, v, *, name):
    shape = w.shape
    cols = shape[-1]
    rows = math.prod(shape[:-1])
    tm = _tile(rows, (256, 128, 64, 32, 16, 8))
    c1 = 1.0 - ADAM_B1 ** ADAM_STEP
    c2 = 1.0 - ADAM_B2 ** ADAM_STEP

    def fn(i, nrt, wv, gv, mv, vv):
        mn = ADAM_B1 * mv + (1.0 - ADAM_B1) * gv
        vn = ADAM_B2 * vv + (1.0 - ADAM_B2) * (gv * gv)
        delta = -ADAM_LR * ((mn / c1) / (jnp.sqrt(vn / c2) + ADAM_EPS) + ADAM_WD * wv)
        return (delta, mn, vn), ()

    outs = _rows(fn, name=name, s=rows, tm=tm, ins=[("row", t.reshape(rows, cols), cols, _c0) for t in (w, g, m, v)],
                 outs=[(cols, cols, _c0, F32)] * 3)
    return [o.reshape(shape) for o in outs]


WEIGHTS = ["mix_norm", "ffn_norm", "ffn_w_gu", "ffn_w_down", "conv_w_in", "conv_w_dw", "conv_w_out", "fox_w_in", "fox_b_f", "fox_q_gain",
           "fox_k_gain", "fox_w_out", "ssd_w_in", "ssd_conv_w", "ssd_conv_b", "ssd_dt_bias", "ssd_a_log", "ssd_d", "ssd_norm_w", "ssd_w_out"]
SHARD_AXIS = {"ffn_w_gu": 2, "ffn_w_down": 1, "conv_w_in": 2, "conv_w_dw": 2, "conv_w_out": 1, "fox_w_in": 2, "fox_w_out": 1, "ssd_w_in": 2,
              "ssd_conv_w": 2, "ssd_conv_b": 1, "ssd_norm_w": 1, "ssd_w_out": 1}
BIG = ["ffn_w_gu", "ffn_w_down", "conv_w_in", "conv_w_out", "fox_w_in", "fox_w_out", "ssd_w_in", "ssd_w_out"]
SMALL_SHARDED = ["conv_w_dw", "ssd_conv_w", "ssd_conv_b", "ssd_norm_w"]
ROW_PAD = 1024
N_CHIPS = 4


def _pack_rows(parts, width, pad_to):
    mats = [p.reshape(-1, width) for p in parts]
    offs, n = [], 0
    for mt in mats:
        offs.append(n)
        n += mt.shape[0]
    total = -(-n // pad_to) * pad_to
    if total > n:
        mats.append(jnp.zeros((total - n, width), mats[0].dtype))
    return jnp.concatenate(mats, axis=0), offs


def _pack_flat(parts, pad_to):
    flat = [p.reshape(-1) for p in parts]
    offs, n = [], 0
    for f in flat:
        offs.append(n)
        n += f.shape[0]
    total = -(-n // pad_to) * pad_to
    if total > n:
        flat.append(jnp.zeros((total - n,), flat[0].dtype))
    return jnp.concatenate(flat).reshape(-1, LANES), offs


def kernel(x, mix_norm, ffn_norm, ffn_w_gu, ffn_w_down, conv_w_in, conv_w_dw, conv_w_out, fox_w_in, fox_b_f, fox_q_gain, fox_k_gain, fox_w_out, ssd_w_in, ssd_conv_w, ssd_conv_b, ssd_dt_bias, ssd_a_log, ssd_d, ssd_norm_w, ssd_w_out, loss_target, m_mix_norm, m_ffn_norm, m_ffn_w_gu, m_ffn_w_down, m_conv_w_in, m_conv_w_dw, m_conv_w_out, m_fox_w_in, m_fox_b_f, m_fox_q_gain, m_fox_k_gain, m_fox_w_out, m_ssd_w_in, m_ssd_conv_w, m_ssd_conv_b, m_ssd_dt_bias, m_ssd_a_log, m_ssd_d, m_ssd_norm_w, m_ssd_w_out, v_mix_norm, v_ffn_norm, v_ffn_w_gu, v_ffn_w_down, v_conv_w_in, v_conv_w_dw, v_conv_w_out, v_fox_w_in, v_fox_b_f, v_fox_q_gain, v_fox_k_gain, v_fox_w_out, v_ssd_w_in, v_ssd_conv_w, v_ssd_conv_b, v_ssd_dt_bias, v_ssd_a_log, v_ssd_d, v_ssd_norm_w, v_ssd_w_out):
    w = dict(zip(WEIGHTS, (mix_norm, ffn_norm, ffn_w_gu, ffn_w_down, conv_w_in, conv_w_dw, conv_w_out, fox_w_in, fox_b_f, fox_q_gain, fox_k_gain,
                           fox_w_out, ssd_w_in, ssd_conv_w, ssd_conv_b, ssd_dt_bias, ssd_a_log, ssd_d, ssd_norm_w, ssd_w_out)))
    m1 = dict(zip(WEIGHTS, (m_mix_norm, m_ffn_norm, m_ffn_w_gu, m_ffn_w_down, m_conv_w_in, m_conv_w_dw, m_conv_w_out, m_fox_w_in, m_fox_b_f,
                            m_fox_q_gain, m_fox_k_gain, m_fox_w_out, m_ssd_w_in, m_ssd_conv_w, m_ssd_conv_b, m_ssd_dt_bias, m_ssd_a_log, m_ssd_d,
                            m_ssd_norm_w, m_ssd_w_out)))
    m2 = dict(zip(WEIGHTS, (v_mix_norm, v_ffn_norm, v_ffn_w_gu, v_ffn_w_down, v_conv_w_in, v_conv_w_dw, v_conv_w_out, v_fox_w_in, v_fox_b_f,
                            v_fox_q_gain, v_fox_k_gain, v_fox_w_out, v_ssd_w_in, v_ssd_conv_w, v_ssd_conv_b, v_ssd_dt_bias, v_ssd_a_log, v_ssd_d,
                            v_ssd_norm_w, v_ssd_w_out)))
    d = x.shape[-1]
    cx, cy, cc = _place()
    chip = 2 * cx + cy

    wp, offs = _pack_rows([w[n].astype(WIRE_DTYPE) for n in BIG], d, ROW_PAD)
    gath = _gather_big(wp, name="gather_weights")
    gath = jnp.swapaxes(gath, 0, 1).reshape(N_CHIPS, wp.shape[0], d)
    sp, soffs = _pack_flat([w[n] for n in SMALL_SHARDED], SUBLANES * LANES)
    sgath = _gather_small(sp, name="gather_small").reshape(N_CHIPS, -1)
    full = {n: w[n] for n in WEIGHTS if n not in SHARD_AXIS}
    for n, off in zip(BIG, offs):
        rows = w[n].size // d
        full[n] = jnp.concatenate([gath[j, off:off + rows].reshape(w[n].shape) for j in range(N_CHIPS)], axis=SHARD_AXIS[n])
    for n, off in zip(SMALL_SHARDED, soffs):
        full[n] = jnp.concatenate([sgath[j, off:off + w[n].size].reshape(w[n].shape) for j in range(N_CHIPS)], axis=SHARD_AXIS[n])

    loss, gx, grads = _local_step(x[0], loss_target[0], full)
    loss = lax.psum(loss, ("x", "y", "c"))

    per_chip = []
    for j in range(N_CHIPS):
        parts = [jnp.split(grads[n], N_CHIPS, axis=SHARD_AXIS[n])[j].astype(WIRE_DTYPE) for n in BIG]
        per_chip.append(_pack_rows(parts, d, ROW_PAD)[0])
    r = per_chip[0].shape[0]
    rh = r // 2
    gp = jnp.stack([jnp.stack([pc[h * rh:(h + 1) * rh] for pc in per_chip]) for h in range(2)])
    from_sibling = _swap_sibling(gp, pick_other_half=True, name="reduce_halves")
    mine = lax.dynamic_index_in_dim(gp, cc, 0, keepdims=False)
    chip_sum = _add2(mine.reshape(N_CHIPS * rh, d), from_sibling.reshape(N_CHIPS * rh, d), name="reduce_add_sibling").reshape(N_CHIPS, rh, d)
    by_chip = _scatter_chips(chip_sum, name="reduce_chips")
    red_half = _sum_blocks(by_chip, name="reduce_sum_chips")
    other_half = _swap_sibling(red_half, pick_other_half=False, name="reduce_share")
    red = jnp.where(cc == 0, jnp.concatenate([red_half, other_half]), jnp.concatenate([other_half, red_half]))

    small_names = [n for n in WEIGHTS if n not in BIG]
    sm, smoffs = _pack_flat([grads[n] for n in small_names], SUBLANES * LANES)
    sred = _allreduce_small(sm, name="allreduce_small").reshape(-1)

    g = {}
    for n, off in zip(BIG, offs):
        g[n] = red[off:off + w[n].size // d].reshape(w[n].shape)
    for n, off in zip(small_names, smoffs):
        fullg = sred[off:off + grads[n].size].reshape(grads[n].shape)
        if n in SHARD_AXIS:
            ax = SHARD_AXIS[n]
            fullg = lax.dynamic_slice_in_dim(fullg, chip * w[n].shape[ax], w[n].shape[ax], axis=ax)
        g[n] = fullg

    deltas, new_m, new_v = [], [], []
    for n in WEIGHTS:
        dl, mn, vn = _adamw(w[n], g[n], m1[n], m2[n], name=f"adamw_{n}")
        deltas.append(dl)
        new_m.append(mn)
        new_v.append(vn)
    return (loss, gx[None], *[g[n] for n in WEIGHTS], *deltas, *new_m, *new_v)
```

```python
import functools
import math

import jax
import jax.numpy as jnp
from jax import lax
from jax.experimental import pallas as pl
from jax.experimental.pallas import tpu as pltpu

F32 = jnp.float32
MM_DTYPE = jnp.bfloat16
ACT_DTYPE = jnp.bfloat16
WIRE_DTYPE = jnp.bfloat16

RMS_EPS = 1e-6
HEAD = 64
SSM_STATE = 128
SSM_CHUNK = 128
LANES = 128
SUBLANES = 8
VMEM_LIMIT = 48 * 1024 * 1024

ADAM_LR, ADAM_B1, ADAM_B2, ADAM_EPS, ADAM_WD, ADAM_STEP = 0.001, 0.9, 0.999, 1e-08, 0.01, 10

HI = lax.Precision.HIGHEST
MESH = pl.DeviceIdType.MESH


def _tile(dim, prefs):
    for p in prefs:
        if dim % p == 0:
            return p
    return dim


def _params(sem):
    return pltpu.CompilerParams(dimension_semantics=sem, vmem_limit_bytes=VMEM_LIMIT)


def _sigmoid(x):
    return 1.0 / (1.0 + jnp.exp(-x))


def _softplus(x):
    return jnp.maximum(x, 0.0) + jnp.log(1.0 + jnp.exp(-jnp.abs(x)))


def _mm(a, b, *, ta=False, tb=False, add=None, out_dtype=F32, name):
    ka, m = (a.shape[0], a.shape[1]) if ta else (a.shape[1], a.shape[0])
    kb, n = (b.shape[1], b.shape[0]) if tb else (b.shape[0], b.shape[1])
    assert ka == kb, (a.shape, b.shape, ta, tb)
    k = ka
    tm = _tile(m, (1408, 1024, 512, 256, 128) if ta else (512, 256, 128))
    tn = _tile(n, (1024, 1408, 512, 256, 128))
    tk = _tile(k, (1024, 1408, 512, 256, 128))
    nk = k // tk
    a_spec = pl.BlockSpec((tk, tm), lambda i, j, q: (q, i)) if ta else pl.BlockSpec((tm, tk), lambda i, j, q: (i, q))
    b_spec = pl.BlockSpec((tn, tk), lambda i, j, q: (j, q)) if tb else pl.BlockSpec((tk, tn), lambda i, j, q: (q, j))
    o_spec = pl.BlockSpec((tm, tn), lambda i, j, q: (i, j))
    dims = (((0 if ta else 1,), (1 if tb else 0,)), ((), ()))
    has_add = add is not None

    def body(*refs):
        a_ref, b_ref = refs[0], refs[1]
        o_ref = refs[2 + has_add]
        p = lax.dot_general(a_ref[...].astype(MM_DTYPE), b_ref[...].astype(MM_DTYPE), dims, preferred_element_type=F32)

        def finish(acc):
            if has_add:
                acc = acc + refs[2][...].astype(F32)
            o_ref[...] = acc.astype(out_dtype)

        if nk == 1:
            finish(p)
        else:
            acc_ref = refs[3 + has_add]
            q = pl.program_id(2)

            @pl.when(q == 0)
            def _():
                acc_ref[...] = p

            @pl.when(q > 0)
            def _():
                acc_ref[...] += p

            @pl.when(q == nk - 1)
            def _():
                finish(acc_ref[...])

    args = [a, b] + ([add] if has_add else [])
    in_specs = [a_spec, b_spec] + ([o_spec] if has_add else [])
    return pl.pallas_call(
        body, name=name, grid=(m // tm, n // tn, nk), in_specs=in_specs, out_specs=o_spec,
        out_shape=jax.ShapeDtypeStruct((m, n), out_dtype),
        scratch_shapes=[pltpu.VMEM((tm, tn), F32)] if nk > 1 else [],
        compiler_params=_params(("parallel", "parallel", "arbitrary")),
    )(*args)


def _rows(fn, *, name, s, tm, ncol=1, ins, outs, accs=()):
    nrt = s // tm
    hb = tm // SUBLANES
    in_specs, args = [], []
    for spec in ins:
        kind, arr = spec[0], spec[1]
        if kind == "full":
            in_specs.append(pl.BlockSpec(arr.shape, lambda j, i: (0, 0)))
        elif kind == "col":
            _, _, bw, cmap = spec
            in_specs.append(pl.BlockSpec((arr.shape[0], bw), lambda j, i, cmap=cmap: (0, cmap(j))))
        elif kind == "row":
            _, _, bw, cmap = spec
            in_specs.append(pl.BlockSpec((tm, bw), lambda j, i, cmap=cmap: (i, cmap(j))))
        elif kind == "prev":
            _, _, bw, cmap = spec
            in_specs.append(pl.BlockSpec((SUBLANES, bw), lambda j, i, cmap=cmap: (jnp.maximum(i * hb - 1, 0), cmap(j))))
        elif kind == "next":
            _, _, bw, cmap = spec
            in_specs.append(pl.BlockSpec((SUBLANES, bw), lambda j, i, cmap=cmap: (jnp.minimum((i + 1) * hb, s // SUBLANES - 1), cmap(j))))
        else:
            raise ValueError(kind)
        args.append(arr)
    out_specs, out_shape = [], []
    for w, bw, cmap, dt in outs:
        out_specs.append(pl.BlockSpec((tm, bw), lambda j, i, cmap=cmap: (i, cmap(j))))
        out_shape.append(jax.ShapeDtypeStruct((s, w), dt))
    for r, w, bw, cmap in accs:
        out_specs.append(pl.BlockSpec((r, bw), lambda j, i, cmap=cmap: (0, cmap(j))))
        out_shape.append(jax.ShapeDtypeStruct((r, w), F32))
    n_in, n_out, n_acc = len(ins), len(outs), len(accs)

    def body(*refs):
        i = pl.program_id(1)
        vals = [r[...] for r in refs[:n_in]]
        o_vals, a_vals = fn(i, nrt, *vals)
        assert len(o_vals) == n_out and len(a_vals) == n_acc
        for r, v in zip(refs[n_in:n_in + n_out], o_vals):
            r[...] = v.astype(r.dtype)
        for r, v in zip(refs[n_in + n_out:], a_vals):
            @pl.when(i == 0)
            def _(r=r, v=v):
                r[...] = v.astype(F32)

            @pl.when(i > 0)
            def _(r=r, v=v):
                r[...] += v.astype(F32)

    res = pl.pallas_call(
        body, name=name, grid=(ncol, nrt), in_specs=in_specs, out_specs=out_specs, out_shape=out_shape,
        compiler_params=_params(("parallel", "arbitrary" if accs else "parallel")),
    )(*args)
    return res


def _c0(j):
    return 0


def _cj(j):
    return j


def _gmean(v, gs):
    w = v.shape[-1]
    tile = max(gs, LANES)
    r = lax.broadcasted_iota(jnp.int32, (tile, tile), 0) // gs
    c = lax.broadcasted_iota(jnp.int32, (tile, tile), 1) // gs
    g = jnp.where(r == c, 1.0 / gs, 0.0).astype(F32)
    parts = [jnp.dot(v[:, t * tile:(t + 1) * tile], g, precision=HI, preferred_element_type=F32) for t in range(w // tile)]
    return parts[0] if len(parts) == 1 else jnp.concatenate(parts, axis=1)


def _sum_rows(v):
    return jnp.sum(v, axis=0, keepdims=True)


def _resid_rms(x, y, w, *, name):
    s, d = x.shape
    has_y = y is not None

    def fn(i, nrt, *v):
        xv = v[0] + (v[1] if has_y else 0.0)
        wv = v[-1]
        r = lax.rsqrt(jnp.mean(xv * xv, axis=-1, keepdims=True) + RMS_EPS)
        return (xv, xv * r * wv), ()

    ins = [("row", x, d, _c0)] + ([("row", y, d, _c0)] if has_y else []) + [("full", w.reshape(1, d))]
    xn, h = _rows(fn, name=name, s=s, tm=_tile(s, (512, 256, 128)), ins=ins, outs=[(d, d, _c0, F32), (d, d, _c0, ACT_DTYPE)])
    return xn, h


def _rms_bwd(x, w, dh, dx_in, *, name):
    s, d = x.shape

    def fn(i, nrt, xv, wv, dhv, dxi):
        r = lax.rsqrt(jnp.mean(xv * xv, axis=-1, keepdims=True) + RMS_EPS)
        xh = xv * r
        g = dhv * wv
        dx = dxi + r * (g - xh * jnp.mean(g * xh, axis=-1, keepdims=True))
        return (dx, dx), (_sum_rows(dhv * xh),)

    dx, dxb, dw = _rows(fn, name=name, s=s, tm=_tile(s, (512, 256, 128)),
                        ins=[("row", x, d, _c0), ("full", w.reshape(1, d)), ("row", dh, d, _c0), ("row", dx_in, d, _c0)],
                        outs=[(d, d, _c0, F32), (d, d, _c0, MM_DTYPE)], accs=[(1, d, d, _c0)])
    return dx, dxb, dw.reshape(d)


def _col_tile(w):
    return _tile(w, (1408, 1024, 512, 256, 128))


def _swiglu_fwd(g, u, *, name):
    s, f = g.shape
    bw = _col_tile(f)

    def fn(i, nrt, gv, uv):
        gv = gv.astype(F32)
        return (gv * _sigmoid(gv) * uv.astype(F32),), ()

    (a,) = _rows(fn, name=name, s=s, tm=_tile(s, (512, 256, 128)), ncol=f // bw,
                 ins=[("row", g, bw, _cj), ("row", u, bw, _cj)], outs=[(f, bw, _cj, ACT_DTYPE)])
    return a


def _swiglu_bwd(g, u, da, *, name):
    s, f = g.shape
    bw = _col_tile(f)

    def fn(i, nrt, gv, uv, dav):
        gv, uv, dav = gv.astype(F32), uv.astype(F32), dav.astype(F32)
        sg = _sigmoid(gv)
        dg = dav * uv * sg * (1.0 + gv * (1.0 - sg))
        du = dav * gv * sg
        return (dg, du), ()

    dg, du = _rows(fn, name=name, s=s, tm=_tile(s, (512, 256, 128)), ncol=f // bw,
                   ins=[("row", g, bw, _cj), ("row", u, bw, _cj), ("row", da, bw, _cj)],
                   outs=[(f, bw, _cj, ACT_DTYPE), (f, bw, _cj, ACT_DTYPE)])
    return dg, du


def _loss_head(x, y, tgt, *, name):
    s, d = x.shape

    def fn(i, nrt, xv, yv, tv):
        diff = xv + yv - tv
        part = 0.5 * jnp.sum(diff * diff) / d
        return (diff / d, diff / d), (jnp.full((1, LANES), part, F32),)

    dy, dyb, loss = _rows(fn, name=name, s=s, tm=_tile(s, (512, 256, 128)),
                          ins=[("row", x, d, _c0), ("row", y, d, _c0), ("row", tgt, d, _c0)],
                          outs=[(d, d, _c0, F32), (d, d, _c0, MM_DTYPE)], accs=[(1, LANES, LANES, _c0)])
    return loss[0, 0], dy, dyb


def _shift_down(ext, j, tm):
    src = pltpu.roll(ext, j, 0) if j else ext
    return src[SUBLANES:SUBLANES + tm]


def _shift_up(ext, j, tm):
    return ext[:tm] if j == 0 else pltpu.roll(ext, ext.shape[0] - j, 0)[:tm]


def _gconv_fwd(b, c, v, w, *, name):
    s, d = b.shape
    kw = w.shape[0]
    bw = _tile(d, (512, 256, 128))
    tm = _tile(s, (512, 256, 128))

    def fn(i, nrt, bv, cv_, vv, pc, pv, wv):
        cv = cv_.astype(F32) * vv.astype(F32)
        pcv = jnp.where(i == 0, 0.0, pc.astype(F32) * pv.astype(F32))
        ext = jnp.concatenate([pcv, cv], axis=0)
        u = sum(wv[k:k + 1, :] * _shift_down(ext, kw - 1 - k, tm) for k in range(kw))
        return (bv.astype(F32) * u,), ()

    (o,) = _rows(fn, name=name, s=s, tm=tm, ncol=d // bw,
                 ins=[("row", b, bw, _cj), ("row", c, bw, _cj), ("row", v, bw, _cj), ("prev", c, bw, _cj), ("prev", v, bw, _cj),
                      ("col", w, bw, _cj)],
                 outs=[(d, bw, _cj, ACT_DTYPE)])
    return o


def _gconv_bwd(b, c, v, w, do, *, name):
    s, d = b.shape
    kw = w.shape[0]
    bw = _tile(d, (512, 256, 128))
    tm = _tile(s, (512, 256, 128))

    def fn(i, nrt, bv, cv_, vv, pc, pv, dov, nb, ndo, wv):
        bv, cv_, vv, dov = bv.astype(F32), cv_.astype(F32), vv.astype(F32), dov.astype(F32)
        cv = cv_ * vv
        pcv = jnp.where(i == 0, 0.0, pc.astype(F32) * pv.astype(F32))
        ext = jnp.concatenate([pcv, cv], axis=0)
        shifted = [_shift_down(ext, kw - 1 - k, tm) for k in range(kw)]
        u = sum(wv[k:k + 1, :] * shifted[k] for k in range(kw))
        db = dov * u
        du = dov * bv
        ndu = jnp.where(i == nrt - 1, 0.0, ndo.astype(F32) * nb.astype(F32))
        ext2 = jnp.concatenate([du, ndu], axis=0)
        dcv = sum(wv[k:k + 1, :] * _shift_up(ext2, kw - 1 - k, tm) for k in range(kw))
        dw = jnp.concatenate([_sum_rows(du * shifted[k]) for k in range(kw)], axis=0)
        return (db, dcv * vv, dcv * cv_), (dw,)

    db, dc, dv, dw = _rows(fn, name=name, s=s, tm=tm, ncol=d // bw,
                           ins=[("row", b, bw, _cj), ("row", c, bw, _cj), ("row", v, bw, _cj), ("prev", c, bw, _cj),
                                ("prev", v, bw, _cj), ("row", do, bw, _cj), ("next", b, bw, _cj), ("next", do, bw, _cj),
                                ("col", w, bw, _cj)],
                           outs=[(d, bw, _cj, ACT_DTYPE)] * 3, accs=[(kw, d, bw, _cj)])
    return db, dc, dv, dw


def _sconv_fwd(x, w, bias, *, name):
    s, d = x.shape
    kw = w.shape[0]
    bw = _tile(d, (512, 256, 128))
    tm = _tile(s, (512, 256, 128))

    def fn(i, nrt, xv, px, wv, bsv):
        xv = xv.astype(F32)
        ext = jnp.concatenate([jnp.where(i == 0, 0.0, px.astype(F32)), xv], axis=0)
        pre = sum(wv[k:k + 1, :] * _shift_down(ext, kw - 1 - k, tm) for k in range(kw)) + bsv
        return (pre * _sigmoid(pre),), ()

    (o,) = _rows(fn, name=name, s=s, tm=tm, ncol=d // bw,
                 ins=[("row", x, bw, _cj), ("prev", x, bw, _cj), ("col", w, bw, _cj), ("col", bias.reshape(1, d), bw, _cj)],
                 outs=[(d, bw, _cj, ACT_DTYPE)])
    return o


def _sconv_bwd(x, w, bias, dact, *, name):
    s, d = x.shape
    kw = w.shape[0]
    bw = _tile(d, (512, 256, 128))
    tm = _tile(s, (512, 256, 128))

    def fn(i, nrt, xv, px, nx, dav, nda, wv, bsv):
        xv = xv.astype(F32)
        ext = jnp.concatenate([jnp.where(i == 0, 0.0, px.astype(F32)), xv, nx.astype(F32)], axis=0)
        rows_e = tm + SUBLANES
        pre_e = sum(wv[k:k + 1, :] * _shift_down(ext, kw - 1 - k, rows_e) for k in range(kw)) + bsv
        da_e = jnp.concatenate([dav.astype(F32), jnp.where(i == nrt - 1, 0.0, nda.astype(F32))], axis=0)
        sg = _sigmoid(pre_e)
        dpre_e = da_e * sg * (1.0 + pre_e * (1.0 - sg))
        dx = sum(wv[k:k + 1, :] * _shift_up(dpre_e, kw - 1 - k, tm) for k in range(kw))
        dpre = dpre_e[:tm]
        dw = jnp.concatenate([_sum_rows(dpre * _shift_down(ext, kw - 1 - k, tm)) for k in range(kw)], axis=0)
        return (dx,), (dw, _sum_rows(dpre))

    dx, dw, db = _rows(fn, name=name, s=s, tm=tm, ncol=d // bw,
                       ins=[("row", x, bw, _cj), ("prev", x, bw, _cj), ("next", x, bw, _cj), ("row", dact, bw, _cj),
                            ("next", dact, bw, _cj), ("col", w, bw, _cj), ("col", bias.reshape(1, d), bw, _cj)],
                       outs=[(d, bw, _cj, ACT_DTYPE)], accs=[(kw, d, bw, _cj), (1, d, bw, _cj)])
    return dx, dw, db.reshape(d)


def _tri(n, reverse):
    r = lax.broadcasted_iota(jnp.int32, (n, n), 0)
    c = lax.broadcasted_iota(jnp.int32, (n, n), 1)
    return jnp.where((c >= r) if reverse else (c <= r), 1.0, 0.0).astype(F32)


def _cumsum_rows(x, *, reverse, name):
    s, w = x.shape
    ch = _tile(s, (256, 128))
    n = s // ch

    def body(x_ref, o_ref, carry):
        i = pl.program_id(0)

        @pl.when(i == 0)
        def _():
            carry[...] = jnp.zeros_like(carry)

        out = jnp.dot(_tri(ch, reverse), x_ref[...], precision=HI, preferred_element_type=F32) + carry[...]
        o_ref[...] = out
        carry[...] = out[0:1, :] if reverse else out[ch - 1:ch, :]

    imap = (lambda i: (n - 1 - i, 0)) if reverse else (lambda i: (i, 0))
    return pl.pallas_call(
        body, name=name, grid=(n,), in_specs=[pl.BlockSpec((ch, w), imap)], out_specs=pl.BlockSpec((ch, w), imap),
        out_shape=jax.ShapeDtypeStruct((s, w), F32), scratch_shapes=[pltpu.VMEM((1, w), F32)],
        compiler_params=_params(("arbitrary",)),
    )(x)


def _fox_prep(q, k, f, gq, gk, bf, *, name):
    s, d = q.shape
    scale = HEAD ** -0.5

    def fn(i, nrt, qv, kv, fv, gqv, gkv, bfv):
        qv, kv = qv.astype(F32), kv.astype(F32)
        qn = qv * lax.rsqrt(_gmean(qv * qv, HEAD) + RMS_EPS) * gqv * scale
        kn = kv * lax.rsqrt(_gmean(kv * kv, HEAD) + RMS_EPS) * gkv
        z = fv + bfv
        logf = jnp.minimum(z, 0.0) - jnp.log(1.0 + jnp.exp(-jnp.abs(z)))
        return (qn, kn, logf), ()

    return _rows(fn, name=name, s=s, tm=_tile(s, (512, 256, 128)),
                 ins=[("row", q, d, _c0), ("row", k, d, _c0), ("row", f, LANES, _c0), ("full", gq), ("full", gk), ("full", bf)],
                 outs=[(d, d, _c0, ACT_DTYPE), (d, d, _c0, ACT_DTYPE), (LANES, LANES, _c0, F32)])


def _fox_prep_bwd(q, k, f, gq, gk, bf, dqs, dkn, dlogf, *, name):
    s, d = q.shape
    scale = HEAD ** -0.5

    def fn(i, nrt, qv, kv, fv, gqv, gkv, bfv, dqv, dkv, dlf):
        outs, accs = [], []
        for xv, gv, dv, sc in ((qv, gqv, dqv, scale), (kv, gkv, dkv, 1.0)):
            xv, dv = xv.astype(F32), dv.astype(F32) * sc
            r = lax.rsqrt(_gmean(xv * xv, HEAD) + RMS_EPS)
            xh = xv * r
            g = dv * gv
            outs.append(r * (g - xh * _gmean(g * xh, HEAD)))
            accs.append(_sum_rows(dv * xh))
        z = fv + bfv
        df = dlf * _sigmoid(-z)
        outs.append(df)
        accs.append(_sum_rows(df))
        return outs, accs

    return _rows(fn, name=name, s=s, tm=_tile(s, (512, 256, 128)),
                 ins=[("row", q, d, _c0), ("row", k, d, _c0), ("row", f, LANES, _c0), ("full", gq), ("full", gk), ("full", bf),
                      ("row", dqs, d, _c0), ("row", dkn, d, _c0), ("row", dlogf, LANES, _c0)],
                 outs=[(d, d, _c0, ACT_DTYPE), (d, d, _c0, ACT_DTYPE), (LANES, LANES, _c0, ACT_DTYPE)],
                 accs=[(1, d, d, _c0), (1, d, d, _c0), (1, LANES, LANES, _c0)])


def _head_masks(shape):
    lane = lax.broadcasted_iota(jnp.int32, shape, len(shape) - 1)
    return lane < HEAD, lane >= HEAD


def _pick_lane(blk, idx):
    lane = lax.broadcasted_iota(jnp.int32, blk.shape, 1)
    return jnp.sum(jnp.where(lane == idx, blk, 0.0), axis=1, keepdims=True)


def _pick_row(blk, idx):
    sub = lax.broadcasted_iota(jnp.int32, blk.shape, 0)
    return jnp.sum(jnp.where(sub == idx, blk, 0.0), axis=0, keepdims=True)


def _fox_aug(qs, kn, cum, *, name):
    s, d = qs.shape
    hp = d // LANES

    def fn(i, nrt, qv, kv, cv):
        lane = lax.broadcasted_iota(jnp.int32, (qv.shape[0], LANES), 1)
        outs = [[], [], [], []]
        for p in range(hp):
            qt, kt = qv[:, p * LANES:(p + 1) * LANES], kv[:, p * LANES:(p + 1) * LANES]
            for h in range(2):
                mine = (lane < HEAD) if h == 0 else (lane >= HEAD)
                a0 = HEAD if h == 0 else 0
                c = cv[:, 2 * p + h:2 * p + h + 1]
                hi = c.astype(ACT_DTYPE).astype(F32)
                mid = (c - hi).astype(ACT_DTYPE).astype(F32)
                lo = (c - hi - mid).astype(ACT_DTYPE).astype(F32)
                ones = jnp.where((lane >= a0) & (lane < a0 + 3), 1.0, 0.0)
                kx = jnp.where(lane == a0, -hi, jnp.where(lane == a0 + 1, -mid, jnp.where(lane == a0 + 2, -lo, 0.0)))
                outs[h].append(jnp.where(mine, qt.astype(F32), ones))
                outs[2 + h].append(jnp.where(mine, kt.astype(F32), kx))
        return [jnp.concatenate(o, axis=1) for o in outs], ()

    return _rows(fn, name=name, s=s, tm=_tile(s, (512, 256, 128)), ins=[("row", qs, d, _c0), ("row", kn, d, _c0), ("row", cum, LANES, _c0)],
                 outs=[(d, d, _c0, ACT_DTYPE)] * 4)


def _tri_tables(nq, by_key):
    import numpy as np
    pairs = [(qi, kj) for kj in range(nq) for qi in range(kj, nq)] if by_key else [(qi, kj) for qi in range(nq) for kj in range(qi + 1)]
    return jnp.asarray(np.array([p[0] for p in pairs], np.int32)), jnp.asarray(np.array([p[1] for p in pairs], np.int32))


def _nt(a, b):
    return lax.dot_general(a, b, (((1,), (1,)), ((), ())), preferred_element_type=F32)


def _tn(a, b):
    return lax.dot_general(a, b, (((0,), (0,)), ((), ())), preferred_element_type=F32)


def _fox_dd(do, o, *, name):
    s, d = do.shape

    def fn(i, nrt, dov, ov):
        return (_reduce_heads(dov.astype(F32) * ov.astype(F32), d // HEAD, HEAD),), ()

    (dd,) = _rows(fn, name=name, s=s, tm=_tile(s, (512, 256, 128)), ins=[("row", do, d, _c0), ("row", o, d, _c0)],
                  outs=[(LANES, LANES, _c0, F32)])
    return dd


def _pair_rows(a, nh):
    s = a.shape[0]
    t = a[:, :nh].T.reshape(nh // 2, 2, s)
    return jnp.pad(t, ((0, 0), (0, SUBLANES - 2), (0, 0)))


def _rows01(r0, r1):
    sub = lax.broadcasted_iota(jnp.int32, (SUBLANES, r0.shape[1]), 0)
    return jnp.where(sub == 0, r0, jnp.where(sub == 1, r1, 0.0))


def _fox_fwd_t(q_aug, k_aug, v, *, name):
    s, d = v.shape
    bq = _tile(s, (512, 256, 128))
    nq = s // bq
    hp = d // LANES
    qtab, ktab = _tri_tables(nq, by_key=False)

    def body(qt, kt, q0_ref, q1_ref, k0_ref, k1_ref, v_ref, o_ref, lse_ref, m_sc, l_sc, acc_sc):
        t = pl.program_id(1)
        qi, kj = qt[t], kt[t]

        @pl.when(kj == 0)
        def _():
            m_sc[...] = jnp.full_like(m_sc, -jnp.inf)
            l_sc[...] = jnp.zeros_like(l_sc)
            acc_sc[...] = jnp.zeros_like(acc_sc)

        def update(diagonal):
            v2 = v_ref[...]
            for h, (q_ref, k_ref) in enumerate(((q0_ref, k0_ref), (q1_ref, k1_ref))):
                st = _nt(k_ref[...], q_ref[...])
                if diagonal:
                    st = _diag_mask_t(st)
                m_prev = m_sc[h]
                m_new = jnp.maximum(m_prev, jnp.max(st, axis=0, keepdims=True))
                p = jnp.exp(st - m_new)
                alpha = jnp.exp(m_prev - m_new)
                l_sc[h] = alpha * l_sc[h] + jnp.sum(p, axis=0, keepdims=True)
                acc_sc[h] = alpha * acc_sc[h] + _tn(v2, p.astype(MM_DTYPE))
                m_sc[h] = m_new

        @pl.when(kj < qi)
        def _():
            update(False)

        @pl.when(kj == qi)
        def _():
            update(True)
            row = lax.broadcasted_iota(jnp.int32, (LANES, bq), 0)
            ot = jnp.where(row < HEAD, acc_sc[0] / l_sc[0], acc_sc[1] / l_sc[1])
            o_ref[...] = ot.T.astype(o_ref.dtype)
            lse_ref[...] = _rows01(m_sc[0] + jnp.log(l_sc[0]), m_sc[1] + jnp.log(l_sc[1]))

    blk = (bq, LANES)
    qmap = lambda p_, t, qt, kt: (qt[t], p_)
    kmap = lambda p_, t, qt, kt: (kt[t], p_)
    grid_spec = pltpu.PrefetchScalarGridSpec(
        num_scalar_prefetch=2, grid=(hp, qtab.shape[0]),
        in_specs=[pl.BlockSpec(blk, qmap), pl.BlockSpec(blk, qmap), pl.BlockSpec(blk, kmap), pl.BlockSpec(blk, kmap), pl.BlockSpec(blk, kmap)],
        out_specs=[pl.BlockSpec(blk, qmap), pl.BlockSpec((None, SUBLANES, bq), lambda p_, t, qt, kt: (p_, 0, qt[t]))],
        scratch_shapes=[pltpu.VMEM((2, 1, bq), F32), pltpu.VMEM((2, 1, bq), F32), pltpu.VMEM((2, LANES, bq), F32)])
    o, lse = pl.pallas_call(
        body, name=name, grid_spec=grid_spec,
        out_shape=[jax.ShapeDtypeStruct((s, d), ACT_DTYPE), jax.ShapeDtypeStruct((hp, SUBLANES, s), F32)],
        compiler_params=_params(("parallel", "arbitrary")),
    )(qtab, ktab, q_aug[0], q_aug[1], k_aug[0], k_aug[1], v)
    return o, lse


def _diag_mask_t(st):
    key = lax.broadcasted_iota(jnp.int32, st.shape, 0)
    qry = lax.broadcasted_iota(jnp.int32, st.shape, 1)
    return jnp.where(qry >= key, st, -jnp.inf)


def _fox_bwd_t(q_aug, k_aug, v, lse, dd, do, *, name):
    s, d = v.shape
    bq = _tile(s, (512, 256, 128))
    nq = s // bq
    hp = d // LANES
    blk = (bq, LANES)
    qtab, ktab = _tri_tables(nq, by_key=True)
    n_steps = qtab.shape[0]

    def body(qt, kt, q0_ref, q1_ref, k0_ref, k1_ref, v_ref, lse_ref, dd_ref, do_ref,
             dq_ref, dk_ref, dv_ref, dcol_ref, drow_ref, dq_sc, rs_sc, dk_sc, dv_sc, cs_sc):
        t = pl.program_id(1)
        qi, kj = qt[t], kt[t]

        @pl.when(t == 0)
        def _():
            dq_sc[...] = jnp.zeros_like(dq_sc)
            rs_sc[...] = jnp.zeros_like(rs_sc)

        def update(diagonal):
            v2, do2 = v_ref[...], do_ref[...]
            masks = _head_masks(blk)
            row = lax.broadcasted_iota(jnp.int32, (LANES, bq), 0)
            off = pl.multiple_of(qi * bq, bq)
            for h, (q_ref, k_ref) in enumerate(((q0_ref, k0_ref), (q1_ref, k1_ref))):
                st = _nt(k_ref[...], q_ref[...])
                if diagonal:
                    st = _diag_mask_t(st)
                p = jnp.exp(st - lse_ref[h:h + 1, :])
                dp = _nt(v2, jnp.where(masks[h], do2, jnp.zeros_like(do2)))
                ds = p * (dp - dd_ref[h:h + 1, :])
                dsb = ds.astype(MM_DTYPE)
                dv_sc[h] += jnp.dot(p.astype(MM_DTYPE), do2, preferred_element_type=F32)
                dk_sc[h] += jnp.dot(dsb, q_ref[...], preferred_element_type=F32)
                cs_sc[h] += jnp.sum(ds, axis=1, keepdims=True)
                mine = (row < HEAD) if h == 0 else (row >= HEAD)
                dq_sc[:, pl.ds(off, bq)] += jnp.where(mine, _tn(k_ref[...], dsb), 0.0)
                rs_sc[h:h + 1, pl.ds(off, bq)] += jnp.sum(ds, axis=0, keepdims=True)

        @pl.when(qi == kj)
        def _():
            dk_sc[...] = jnp.zeros_like(dk_sc)
            dv_sc[...] = jnp.zeros_like(dv_sc)
            cs_sc[...] = jnp.zeros_like(cs_sc)
            update(True)

        @pl.when(qi > kj)
        def _():
            update(False)

        @pl.when(qi == nq - 1)
        def _():
            lo, _hi = _head_masks(blk)
            dk_ref[...] = jnp.where(lo, dk_sc[0], dk_sc[1]).astype(dk_ref.dtype)
            dv_ref[...] = jnp.where(lo, dv_sc[0], dv_sc[1]).astype(dv_ref.dtype)
            dcol_ref[...] = jnp.where(lo, cs_sc[0], cs_sc[1])

        @pl.when(t == n_steps - 1)
        def _():
            for c in range(nq):
                dq_ref[c * bq:(c + 1) * bq, :] = dq_sc[:, c * bq:(c + 1) * bq].T.astype(dq_ref.dtype)
            drow_ref[...] = rs_sc[...]

    qmap = lambda p_, t, qt, kt: (qt[t], p_)
    kmap = lambda p_, t, qt, kt: (kt[t], p_)
    rmap = lambda p_, t, qt, kt: (p_, 0, qt[t])
    return pl.pallas_call(
        body, name=name,
        grid_spec=pltpu.PrefetchScalarGridSpec(
            num_scalar_prefetch=2, grid=(hp, n_steps),
            in_specs=[pl.BlockSpec(blk, qmap), pl.BlockSpec(blk, qmap), pl.BlockSpec(blk, kmap), pl.BlockSpec(blk, kmap), pl.BlockSpec(blk, kmap),
                      pl.BlockSpec((None, SUBLANES, bq), rmap), pl.BlockSpec((None, SUBLANES, bq), rmap), pl.BlockSpec(blk, qmap)],
            out_specs=[pl.BlockSpec((s, LANES), lambda p_, t, qt, kt: (0, p_)), pl.BlockSpec(blk, kmap), pl.BlockSpec(blk, kmap),
                       pl.BlockSpec(blk, kmap), pl.BlockSpec((None, SUBLANES, s), lambda p_, t, qt, kt: (p_, 0, 0))],
            scratch_shapes=[pltpu.VMEM((LANES, s), F32), pltpu.VMEM((SUBLANES, s), F32), pltpu.VMEM((2, bq, LANES), F32),
                            pltpu.VMEM((2, bq, LANES), F32), pltpu.VMEM((2, bq, 1), F32)]),
        out_shape=[jax.ShapeDtypeStruct((s, d), ACT_DTYPE), jax.ShapeDtypeStruct((s, d), ACT_DTYPE), jax.ShapeDtypeStruct((s, d), ACT_DTYPE),
                   jax.ShapeDtypeStruct((s, d), F32), jax.ShapeDtypeStruct((hp, SUBLANES, s), F32)],
        compiler_params=_params(("parallel", "arbitrary")),
    )(qtab, ktab, q_aug[0], q_aug[1], k_aug[0], k_aug[1], v, lse, dd, do)


def _expand_heads(v, nh, hd):
    r = lax.broadcasted_iota(jnp.int32, (LANES, nh * hd), 0)
    c = lax.broadcasted_iota(jnp.int32, (LANES, nh * hd), 1) // hd
    e = jnp.where(r == c, 1.0, 0.0).astype(F32)
    return jnp.dot(v, e, precision=HI, preferred_element_type=F32)


def _reduce_heads(v, nh, hd):
    r = lax.broadcasted_iota(jnp.int32, (nh * hd, LANES), 0) // hd
    c = lax.broadcasted_iota(jnp.int32, (nh * hd, LANES), 1)
    e = jnp.where(r == c, 1.0, 0.0).astype(F32)
    return jnp.dot(v, e, precision=HI, preferred_element_type=F32)


def _ssd_prep(dt_raw, dt_bias, a_log, nh, *, name):
    s = dt_raw.shape[0]

    def fn(i, nrt, dtr, bsv, alv):
        dt = _softplus(dtr + bsv)
        acum = jnp.dot(_tri(SSM_CHUNK, False), dt * (-jnp.exp(alv)), precision=HI, preferred_element_type=F32)
        return (dt, acum, _expand_heads(dt, nh, HEAD), _expand_heads(acum, nh, HEAD)), ()

    w = nh * HEAD
    return _rows(fn, name=name, s=s, tm=SSM_CHUNK, ins=[("row", dt_raw, LANES, _c0), ("full", dt_bias), ("full", a_log)],
                 outs=[(LANES, LANES, _c0, F32), (LANES, LANES, _c0, F32), (w, w, _c0, F32), (w, w, _c0, F32)])


def _ssd_prep_bwd(dt_raw, dt_bias, a_log, ddtx, dacx, nh, *, name):
    s = dt_raw.shape[0]

    def fn(i, nrt, dtr, bsv, alv, ddx, dax):
        z = dtr + bsv
        dt = _softplus(z)
        a = -jnp.exp(alv)
        dda = jnp.dot(_tri(SSM_CHUNK, True), _reduce_heads(dax, nh, HEAD), precision=HI, preferred_element_type=F32)
        ddt = _reduce_heads(ddx, nh, HEAD) + dda * a
        dz = ddt * _sigmoid(z)
        lane = lax.broadcasted_iota(jnp.int32, dz.shape, 1)
        dz = jnp.where(lane < nh, dz, 0.0)
        return (dz,), (_sum_rows(dz), _sum_rows(dda * dt) * a)

    w = nh * HEAD
    return _rows(fn, name=name, s=s, tm=SSM_CHUNK,
                 ins=[("row", dt_raw, LANES, _c0), ("full", dt_bias), ("full", a_log), ("row", ddtx, w, _c0), ("row", dacx, w, _c0)],
                 outs=[(LANES, LANES, _c0, ACT_DTYPE)], accs=[(1, LANES, LANES, _c0), (1, LANES, LANES, _c0)])


def _ssd_decay(ac_blk, act_blk, head):
    col = _pick_lane(ac_blk, head)
    row = _pick_row(act_blk, head)
    r = lax.broadcasted_iota(jnp.int32, (SSM_CHUNK, SSM_CHUNK), 0)
    c = lax.broadcasted_iota(jnp.int32, (SSM_CHUNK, SSM_CHUNK), 1)
    return jnp.exp(jnp.where(r >= c, col - row, -jnp.inf))


def _group_masks(shape, hpg):
    lane = lax.broadcasted_iota(jnp.int32, shape, len(shape) - 1) // HEAD
    return [lane == k for k in range(hpg)]


def _ssd_scan_fwd(xs, bm, cm, dtx, acx, acum, acum_t, d_x, *, name):
    s, di = xs.shape
    ng = bm.shape[1] // SSM_STATE
    gw = di // ng
    hpg = gw // HEAD
    nc = s // SSM_CHUNK
    L = SSM_CHUNK
    nh_pad = acum_t.shape[0]

    def body(x_ref, b_ref, c_ref, dt_ref, ax_ref, ac_ref, act_ref, d_ref, y_ref, st_ref, state):
        g, c = pl.program_id(0), pl.program_id(1)

        @pl.when(c == 0)
        def _():
            state[...] = jnp.zeros_like(state)

        x4, bv, cv = x_ref[...].astype(F32), b_ref[...], c_ref[...]
        ax = ax_ref[...]
        tx = (x4 * dt_ref[...])
        cb = lax.dot_general(cv, bv, (((1,), (1,)), ((), ())), preferred_element_type=F32)
        masks = _group_masks((L, gw), hpg)
        y = jnp.zeros((L, gw), F32)
        txb = tx.astype(MM_DTYPE)
        for k in range(hpg):
            wk = (cb * _ssd_decay(ac_ref[...], act_ref[...], g * hpg + k)).astype(MM_DTYPE)
            y = y + jnp.where(masks[k], jnp.dot(wk, txb, preferred_element_type=F32), 0.0)
        prev = state[...]
        st_ref[...] = prev
        y = y + jnp.dot(cv, prev.astype(MM_DTYPE), preferred_element_type=F32) * jnp.exp(ax)
        y = y + d_ref[...] * x4
        y_ref[...] = y.astype(y_ref.dtype)
        a_last = ax[L - 1:L, :]
        sx = (tx * jnp.exp(a_last - ax)).astype(MM_DTYPE)
        state[...] = prev * jnp.exp(a_last) + lax.dot_general(bv, sx, (((0,), (0,)), ((), ())), preferred_element_type=F32)

    y, states = pl.pallas_call(
        body, name=name, grid=(ng, nc),
        in_specs=[pl.BlockSpec((L, gw), lambda g, c: (c, g)), pl.BlockSpec((L, SSM_STATE), lambda g, c: (c, g)),
                  pl.BlockSpec((L, SSM_STATE), lambda g, c: (c, g)), pl.BlockSpec((L, gw), lambda g, c: (c, g)),
                  pl.BlockSpec((L, gw), lambda g, c: (c, g)), pl.BlockSpec((L, LANES), lambda g, c: (c, 0)),
                  pl.BlockSpec((nh_pad, L), lambda g, c: (0, c)), pl.BlockSpec((1, gw), lambda g, c: (0, g))],
        out_specs=[pl.BlockSpec((L, gw), lambda g, c: (c, g)), pl.BlockSpec((None, None, SSM_STATE, gw), lambda g, c: (g, c, 0, 0))],
        out_shape=[jax.ShapeDtypeStruct((s, di), ACT_DTYPE), jax.ShapeDtypeStruct((ng, nc, SSM_STATE, gw), F32)],
        scratch_shapes=[pltpu.VMEM((SSM_STATE, gw), F32)],
        compiler_params=_params(("parallel", "arbitrary")),
    )(xs, bm, cm, dtx, acx, acum, acum_t, d_x)
    return y, states


def _ssd_scan_bwd(xs, bm, cm, dtx, acx, acum, acum_t, d_x, states, dy, *, name):
    s, di = xs.shape
    ng = bm.shape[1] // SSM_STATE
    gw = di // ng
    hpg = gw // HEAD
    nc = s // SSM_CHUNK
    L = SSM_CHUNK
    nh_pad = acum_t.shape[0]

    def body(x_ref, b_ref, c_ref, dt_ref, ax_ref, ac_ref, act_ref, d_ref, st_ref, dy_ref,
             dx_ref, db_ref, dc_ref, ddt_ref, dax_ref, dd_ref, dstate):
        g, cc = pl.program_id(0), pl.program_id(1)

        @pl.when(cc == 0)
        def _():
            dstate[...] = jnp.zeros_like(dstate)
            dd_ref[...] = jnp.zeros_like(dd_ref)

        x4, bv, cv = x_ref[...].astype(F32), b_ref[...], c_ref[...]
        tv, ax, dyv = dt_ref[...], ax_ref[...], dy_ref[...].astype(F32)
        prev, dn = st_ref[...], dstate[...]
        dnb = dn.astype(MM_DTYPE)
        masks = _group_masks((L, gw), hpg)
        tx = x4 * tv
        txb = tx.astype(MM_DTYPE)
        e_ax = jnp.exp(ax)
        a_last = ax[L - 1:L, :]
        e_last = jnp.exp(a_last)
        ed = jnp.exp(a_last - ax)

        dx = d_ref[...] * dyv
        dd_ref[...] += _sum_rows(dyv * x4)
        dye = (dyv * e_ax).astype(MM_DTYPE)
        yo = jnp.dot(cv, prev.astype(MM_DTYPE), preferred_element_type=F32) * e_ax
        dc = lax.dot_general(dye, prev.astype(MM_DTYPE), (((1,), (1,)), ((), ())), preferred_element_type=F32)
        dprev = lax.dot_general(cv, dye, (((0,), (0,)), ((), ())), preferred_element_type=F32)
        dax = dyv * yo
        sx = tx * ed
        dsx = jnp.dot(bv, dnb, preferred_element_type=F32)
        db = lax.dot_general(sx.astype(MM_DTYPE), dnb, (((1,), (1,)), ((), ())), preferred_element_type=F32)
        dtx_ = dsx * ed
        dsx_sx = dsx * sx
        dax = dax - dsx_sx
        dlast = _sum_rows(dsx_sx) + _sum_rows(dn * prev) * e_last
        dprev = dprev + dn * e_last
        cb = lax.dot_general(cv, bv, (((1,), (1,)), ((), ())), preferred_element_type=F32)
        dcb = jnp.zeros((L, L), F32)
        lane = lax.broadcasted_iota(jnp.int32, (L, gw), 1)
        for k in range(hpg):
            dec = _ssd_decay(ac_ref[...], act_ref[...], g * hpg + k)
            wk = (cb * dec).astype(MM_DTYPE)
            dyk = jnp.where(masks[k], dyv, 0.0).astype(MM_DTYPE)
            dtx_ = dtx_ + jnp.where(masks[k], lax.dot_general(wk, dyk, (((0,), (0,)), ((), ())), preferred_element_type=F32), 0.0)
            dwk = lax.dot_general(dyk, txb, (((1,), (1,)), ((), ())), preferred_element_type=F32)
            dcb = dcb + dwk * dec
            mk = dwk * cb * dec
            da_k = jnp.sum(mk, axis=1, keepdims=True) - jnp.sum(mk.T, axis=1, keepdims=True)
            dax = dax + jnp.where(lane == k * HEAD, da_k, 0.0)
        dcbb = dcb.astype(MM_DTYPE)
        dc = dc + jnp.dot(dcbb, bv, preferred_element_type=F32)
        db = db + lax.dot_general(dcbb, cv, (((0,), (0,)), ((), ())), preferred_element_type=F32)
        sub = lax.broadcasted_iota(jnp.int32, (L, gw), 0)
        dax = dax + jnp.where(sub == L - 1, dlast, 0.0)
        dx_ref[...] = (dx + dtx_ * tv).astype(dx_ref.dtype)
        ddt_ref[...] = dtx_ * x4
        dax_ref[...] = dax
        db_ref[...] = db.astype(db_ref.dtype)
        dc_ref[...] = dc.astype(dc_ref.dtype)
        dstate[...] = dprev

    rev = lambda g, c: (nc - 1 - c, g)
    rev0 = lambda g, c: (nc - 1 - c, 0)
    outs = pl.pallas_call(
        body, name=name, grid=(ng, nc),
        in_specs=[pl.BlockSpec((L, gw), rev), pl.BlockSpec((L, SSM_STATE), rev), pl.BlockSpec((L, SSM_STATE), rev),
                  pl.BlockSpec((L, gw), rev), pl.BlockSpec((L, gw), rev), pl.BlockSpec((L, LANES), rev0),
                  pl.BlockSpec((nh_pad, L), lambda g, c: (0, nc - 1 - c)), pl.BlockSpec((1, gw), lambda g, c: (0, g)),
                  pl.BlockSpec((None, None, SSM_STATE, gw), lambda g, c: (g, nc - 1 - c, 0, 0)), pl.BlockSpec((L, gw), rev)],
        out_specs=[pl.BlockSpec((L, gw), rev), pl.BlockSpec((L, SSM_STATE), rev), pl.BlockSpec((L, SSM_STATE), rev),
                   pl.BlockSpec((L, gw), rev), pl.BlockSpec((L, gw), rev), pl.BlockSpec((1, gw), lambda g, c: (0, g))],
        out_shape=[jax.ShapeDtypeStruct((s, di), ACT_DTYPE), jax.ShapeDtypeStruct(bm.shape, ACT_DTYPE), jax.ShapeDtypeStruct(cm.shape, ACT_DTYPE),
                   jax.ShapeDtypeStruct((s, di), F32), jax.ShapeDtypeStruct((s, di), F32), jax.ShapeDtypeStruct((1, di), F32)],
        scratch_shapes=[pltpu.VMEM((SSM_STATE, gw), F32)],
        compiler_params=_params(("parallel", "arbitrary")),
    )(xs, bm, cm, dtx, acx, acum, acum_t, d_x, states, dy)
    return outs


def _ssd_gate(y, z, w, gs, *, name):
    s, d = y.shape

    def fn(i, nrt, yv, zv, wv):
        zv = zv.astype(F32)
        u = yv.astype(F32) * zv * _sigmoid(zv)
        return (u * lax.rsqrt(_gmean(u * u, gs) + RMS_EPS) * wv,), ()

    (o,) = _rows(fn, name=name, s=s, tm=_tile(s, (256, 128)), ins=[("row", y, d, _c0), ("row", z, d, _c0), ("full", w.reshape(1, d))],
                 outs=[(d, d, _c0, ACT_DTYPE)])
    return o


def _ssd_gate_bwd(y, z, w, do, gs, *, name):
    s, d = y.shape

    def fn(i, nrt, yv, zv, wv, dov):
        yv, zv, dov = yv.astype(F32), zv.astype(F32), dov.astype(F32)
        sg = _sigmoid(zv)
        sl = zv * sg
        u = yv * sl
        r = lax.rsqrt(_gmean(u * u, gs) + RMS_EPS)
        uh = u * r
        g = dov * wv
        du = r * (g - uh * _gmean(g * uh, gs))
        return (du * sl, du * yv * sg * (1.0 + zv * (1.0 - sg))), (_sum_rows(dov * uh),)

    dy, dz, dw = _rows(fn, name=name, s=s, tm=_tile(s, (256, 128)),
                       ins=[("row", y, d, _c0), ("row", z, d, _c0), ("full", w.reshape(1, d)), ("row", do, d, _c0)],
                       outs=[(d, d, _c0, ACT_DTYPE), (d, d, _c0, ACT_DTYPE)], accs=[(1, d, d, _c0)])
    return dy, dz, dw.reshape(d)


def _pad_lanes(w):
    return jnp.pad(w, ((0, 0), (0, LANES - w.shape[1])))


def _nt_sum(pairs, name):
    acc = None
    for a, b in pairs:
        acc = _mm(a, b, tb=True, add=acc, name=name)
    return acc


def _conv_mixer_fwd(h, w_in, w_dw, tag):
    d = h.shape[1]
    ws = [w_in[:, k * d:(k + 1) * d] for k in range(3)]
    b, c, v = [_mm(h, w, out_dtype=ACT_DTYPE, name=f"{tag}_in") for w in ws]
    return _gconv_fwd(b, c, v, w_dw, name=f"{tag}_gate"), (h, ws, b, c, v, w_dw)


def _conv_mixer_bwd(cache, do, tag):
    h, ws, b, c, v, w_dw = cache
    db, dc, dv, dw_dw = _gconv_bwd(b, c, v, w_dw, do, name=f"{tag}_gate_bwd")
    dps = (db, dc, dv)
    dw_in = jnp.concatenate([_mm(h, dp, ta=True, name=f"{tag}_dw_in") for dp in dps], axis=1)
    dh = _nt_sum(list(zip(dps, ws)), f"{tag}_dh")
    return dh, {"w_in": dw_in, "w_dw": dw_dw}


def _fox_mixer_fwd(h, w_in, b_f, q_gain, k_gain, tag):
    d = h.shape[1]
    nh = d // HEAD
    ws = [w_in[:, k * d:(k + 1) * d] for k in range(3)] + [_pad_lanes(w_in[:, 3 * d:])]
    q, k, v = [_mm(h, w, out_dtype=ACT_DTYPE, name=f"{tag}_in") for w in ws[:3]]
    f = _mm(h, ws[3], name=f"{tag}_in_f")
    gq = jnp.tile(q_gain, nh).reshape(1, d)
    gk = jnp.tile(k_gain, nh).reshape(1, d)
    bf = _pad_lanes(b_f.reshape(1, nh))
    qs, kn, logf = _fox_prep(q, k, f, gq, gk, bf, name=f"{tag}_prep")
    cum = _cumsum_rows(logf, reverse=False, name=f"{tag}_cum")
    aug = _fox_aug(qs, kn, cum, name=f"{tag}_aug")
    q_aug, k_aug = aug[:2], aug[2:]
    o, lse = _fox_fwd_t(q_aug, k_aug, v, name=f"{tag}_attn")
    return o, (h, ws, q, k, v, f, gq, gk, bf, q_aug, k_aug, o, lse)


def _fox_mixer_bwd(cache, do, tag):
    h, ws, q, k, v, f, gq, gk, bf, q_aug, k_aug, o, lse = cache
    s, d = q.shape
    nh = d // HEAD
    dd = _pair_rows(_fox_dd(do, o, name=f"{tag}_attn_dd"), nh)
    dqs, dkn, dv, dcol, drow = _fox_bwd_t(q_aug, k_aug, v, lse, dd, do, name=f"{tag}_attn_bwd")
    dcum = _pad_lanes(drow[:, :2, :].reshape(nh, s).T - dcol[:, ::HEAD])
    dlogf = _cumsum_rows(dcum, reverse=True, name=f"{tag}_cum_bwd")
    dq, dk, df, dgq, dgk, dbf = _fox_prep_bwd(q, k, f, gq, gk, bf, dqs, dkn, dlogf, name=f"{tag}_prep_bwd")
    dps = (dq, dk, dv, df)
    dws = [_mm(h, dp, ta=True, name=f"{tag}_dw_in") for dp in dps]
    dw_in = jnp.concatenate(dws[:3] + [dws[3][:, :nh]], axis=1)
    dh = _nt_sum(list(zip(dps, ws)), f"{tag}_dh")
    return dh, {"w_in": dw_in, "b_f": dbf[0, :nh], "q_gain": dgq.reshape(nh, HEAD).sum(0), "k_gain": dgk.reshape(nh, HEAD).sum(0)}


def _ssd_mixer_fwd(h, w_in, conv_w, conv_b, dt_bias, a_log, d_skip, norm_w, tag):
    di = norm_w.shape[0]
    nh = di // HEAD
    gn = (conv_w.shape[1] - di) // 2
    cuts = [0, di, 2 * di, 2 * di + gn, 2 * di + 2 * gn]
    ws = [w_in[:, cuts[k]:cuts[k + 1]] for k in range(4)] + [_pad_lanes(w_in[:, cuts[4]:])]
    z, xr, br, cr = [_mm(h, w, out_dtype=ACT_DTYPE, name=f"{tag}_in") for w in ws[:4]]
    dtr = _mm(h, ws[4], name=f"{tag}_in_dt")
    ccuts = [0, di, di + gn, di + 2 * gn]
    cws = [conv_w[:, ccuts[k]:ccuts[k + 1]] for k in range(3)]
    cbs = [conv_b[ccuts[k]:ccuts[k + 1]] for k in range(3)]
    xs, bm, cm = [_sconv_fwd(r, w, b, name=f"{tag}_conv") for r, w, b in zip((xr, br, cr), cws, cbs)]
    dtb = _pad_lanes(dt_bias.reshape(1, nh))
    alg = _pad_lanes(a_log.reshape(1, nh))
    _dt, acum, dtx, acx = _ssd_prep(dtr, dtb, alg, nh, name=f"{tag}_prep")
    acum_t = acum[:, :nh].T
    d_x = jnp.repeat(d_skip, HEAD).reshape(1, di)
    y, states = _ssd_scan_fwd(xs, bm, cm, dtx, acx, acum, acum_t, d_x, name=f"{tag}_scan")
    gs = di // (gn // SSM_STATE)
    o = _ssd_gate(y, z, norm_w, gs, name=f"{tag}_gate")
    return o, (h, ws, z, (xr, br, cr), dtr, cws, cbs, xs, bm, cm, dtb, alg, dtx, acx, acum, acum_t, d_x, states, y, norm_w, gs, nh)


def _ssd_mixer_bwd(cache, do, tag):
    h, ws, z, raws, dtr, cws, cbs, xs, bm, cm, dtb, alg, dtx, acx, acum, acum_t, d_x, states, y, norm_w, gs, nh = cache
    dy, dz, dnorm = _ssd_gate_bwd(y, z, norm_w, do, gs, name=f"{tag}_gate_bwd")
    dxs, dbm, dcm, ddtx, dacx, dd_x = _ssd_scan_bwd(xs, bm, cm, dtx, acx, acum, acum_t, d_x, states, dy, name=f"{tag}_scan_bwd")
    ddtr, ddtb, dalg = _ssd_prep_bwd(dtr, dtb, alg, ddtx, dacx, nh, name=f"{tag}_prep_bwd")
    conv = [_sconv_bwd(r, w, b, da, name=f"{tag}_conv_bwd") for r, w, b, da in zip(raws, cws, cbs, (dxs, dbm, dcm))]
    dps = (dz, conv[0][0], conv[1][0], conv[2][0], ddtr)
    dws = [_mm(h, dp, ta=True, name=f"{tag}_dw_in") for dp in dps]
    dw_in = jnp.concatenate(dws[:4] + [dws[4][:, :nh]], axis=1)
    dh = _nt_sum(list(zip(dps, ws)), f"{tag}_dh")
    return dh, {"w_in": dw_in, "conv_w": jnp.concatenate([c[1] for c in conv], axis=1), "conv_b": jnp.concatenate([c[2] for c in conv]),
                "dt_bias": ddtb[0, :nh], "a_log": dalg[0, :nh], "d": dd_x.reshape(nh, HEAD).sum(1), "norm_w": dnorm}


def _local_step(x, tgt, fw):
    depth = fw["mix_norm"].shape[0]
    layers = []
    xc, y_prev = x, None
    for i in range(depth):
        kind, j = i % 3, i // 3
        tag = f"l{i}"
        xin, h = _resid_rms(xc, y_prev, fw["mix_norm"][i], name=f"{tag}_norm1")
        if kind == 0:
            o, mc = _conv_mixer_fwd(h, fw["conv_w_in"][j], fw["conv_w_dw"][j], tag + "_conv")
            w_out = fw["conv_w_out"][j]
        elif kind == 1:
            o, mc = _fox_mixer_fwd(h, fw["fox_w_in"][j], fw["fox_b_f"][j], fw["fox_q_gain"][j], fw["fox_k_gain"][j], tag + "_fox")
            w_out = fw["fox_w_out"][j]
        else:
            o, mc = _ssd_mixer_fwd(h, fw["ssd_w_in"][j], fw["ssd_conv_w"][j], fw["ssd_conv_b"][j], fw["ssd_dt_bias"][j], fw["ssd_a_log"][j],
                                   fw["ssd_d"][j], fw["ssd_norm_w"][j], tag + "_ssd")
            w_out = fw["ssd_w_out"][j]
        ym = _mm(o, w_out, name=f"{tag}_mix_out")
        x1, h2 = _resid_rms(xin, ym, fw["ffn_norm"][i], name=f"{tag}_norm2")
        f = fw["ffn_w_down"].shape[1]
        wg, wu = fw["ffn_w_gu"][i][:, :f], fw["ffn_w_gu"][i][:, f:]
        g = _mm(h2, wg, out_dtype=ACT_DTYPE, name=f"{tag}_ffn_g")
        u = _mm(h2, wu, out_dtype=ACT_DTYPE, name=f"{tag}_ffn_u")
        a = _swiglu_fwd(g, u, name=f"{tag}_swiglu")
        yf = _mm(a, fw["ffn_w_down"][i], name=f"{tag}_ffn_down")
        layers.append((xin, o, mc, w_out, x1, h2, wg, wu, g, u, a))
        xc, y_prev = x1, yf
    loss, dx, dxb = _loss_head(xc, y_prev, tgt, name="loss_head")

    names = ("conv", "fox", "ssd")
    grads = {k: [None] * v.shape[0] for k, v in fw.items()}
    for i in reversed(range(depth)):
        kind, j = i % 3, i // 3
        tag = f"l{i}"
        xin, o, mc, w_out, x1, h2, wg, wu, g, u, a = layers[i]
        grads["ffn_w_down"][i] = _mm(a, dxb, ta=True, name=f"{tag}_dw_down")
        da = _mm(dxb, fw["ffn_w_down"][i], tb=True, out_dtype=ACT_DTYPE, name=f"{tag}_da")
        dg, du = _swiglu_bwd(g, u, da, name=f"{tag}_swiglu_bwd")
        grads["ffn_w_gu"][i] = jnp.concatenate([_mm(h2, dg, ta=True, name=f"{tag}_dw_g"), _mm(h2, du, ta=True, name=f"{tag}_dw_u")], axis=1)
        dh2 = _nt_sum([(dg, wg), (du, wu)], f"{tag}_dh2")
        dx1, dx1b, grads["ffn_norm"][i] = _rms_bwd(x1, fw["ffn_norm"][i], dh2, dx, name=f"{tag}_norm2_bwd")
        grads[names[kind] + "_w_out"][j] = _mm(o, dx1b, ta=True, name=f"{tag}_dw_out")
        do = _mm(dx1b, w_out, tb=True, out_dtype=ACT_DTYPE, name=f"{tag}_do")
        if kind == 0:
            dh, mg = _conv_mixer_bwd(mc, do, tag + "_conv")
        elif kind == 1:
            dh, mg = _fox_mixer_bwd(mc, do, tag + "_fox")
        else:
            dh, mg = _ssd_mixer_bwd(mc, do, tag + "_ssd")
        for k, v in mg.items():
            grads[f"{names[kind]}_{k}"][j] = v
        dx, dxb, grads["mix_norm"][i] = _rms_bwd(xin, fw["mix_norm"][i], dh, dx1, name=f"{tag}_norm1_bwd")
    return loss, dx, {k: jnp.stack(v) for k, v in grads.items()}


ANY = pl.BlockSpec(memory_space=pl.ANY)
VMEM_SPEC = pl.BlockSpec(memory_space=pltpu.VMEM)


def _place():
    return lax.axis_index("x"), lax.axis_index("y"), lax.axis_index("c")


def _remote(src, dst, send_sems, recv_sems, k, to):
    return pltpu.make_async_remote_copy(src_ref=src, dst_ref=dst, send_sem=send_sems.at[k], recv_sem=recv_sems.at[k],
                                        device_id=to, device_id_type=MESH)


def _gather_big(wp, *, name):
    r, w = wp.shape
    rh = r // 2

    def body(w_ref, o_ref, send_sems, recv_sems, local_sem):
        x, y, c = _place()
        me, sibling, m = (x, y, c), (x, y, 1 - c), 2 * x + y
        chips = [(1 - x, y), (x, 1 - y), (1 - x, 1 - y)]
        mine = w_ref.at[pl.ds(pl.multiple_of(c * rh, 16), rh)]
        local = pltpu.make_async_copy(mine, o_ref.at[c, m], local_sem)
        local.start()
        first = [_remote(mine, o_ref.at[c, m], send_sems, recv_sems, 0, sibling)]
        first += [_remote(mine, o_ref.at[c, m], send_sems, recv_sems, 1 + j, (px, py, c)) for j, (px, py) in enumerate(chips)]
        for cp in first:
            cp.start()
        passed = []
        for j, (px, py) in enumerate(chips):
            blk = o_ref.at[c, 2 * px + py]
            _remote(blk, blk, send_sems, recv_sems, 1 + j, me).wait_recv()
            fwd = _remote(blk, blk, send_sems, recv_sems, 4 + j, sibling)
            fwd.start()
            passed.append(fwd)
        _remote(o_ref.at[1 - c, m], o_ref.at[1 - c, m], send_sems, recv_sems, 0, me).wait_recv()
        for j, (px, py) in enumerate(chips):
            blk = o_ref.at[1 - c, 2 * px + py]
            _remote(blk, blk, send_sems, recv_sems, 4 + j, me).wait_recv()
        for cp in first + passed:
            cp.wait_send()
        local.wait()

    return pl.pallas_call(
        body, name=name, in_specs=[ANY], out_specs=ANY, out_shape=jax.ShapeDtypeStruct((2, 4, rh, w), wp.dtype),
        scratch_shapes=[pltpu.SemaphoreType.DMA((7,)), pltpu.SemaphoreType.DMA((7,)), pltpu.SemaphoreType.DMA],
    )(wp)


def _gather_small(v, *, name):
    r, w = v.shape

    def body(v_ref, o_ref, send_sems, recv_sems):
        x, y, c = _place()
        m = 2 * x + y
        chips = [(1 - x, y), (x, 1 - y), (1 - x, 1 - y)]
        o_ref[m] = v_ref[...]
        sends = [_remote(v_ref, o_ref.at[m], send_sems, recv_sems, j, (px, py, c)) for j, (px, py) in enumerate(chips)]
        for cp in sends:
            cp.start()
        for j, (px, py) in enumerate(chips):
            blk = o_ref.at[2 * px + py]
            _remote(blk, blk, send_sems, recv_sems, j, (x, y, c)).wait_recv()
        for cp in sends:
            cp.wait_send()

    return pl.pallas_call(
        body, name=name, in_specs=[VMEM_SPEC], out_specs=VMEM_SPEC, out_shape=jax.ShapeDtypeStruct((4, r, w), v.dtype),
        scratch_shapes=[pltpu.SemaphoreType.DMA((3,)), pltpu.SemaphoreType.DMA((3,))],
    )(v)


def _swap_sibling(a, *, name):
    def body(a_ref, o_ref, send_sem, recv_sem):
        x, y, c = _place()
        cp = pltpu.make_async_remote_copy(src_ref=a_ref.at[1 - c], dst_ref=o_ref, send_sem=send_sem, recv_sem=recv_sem,
                                          device_id=(x, y, 1 - c), device_id_type=MESH)
        cp.start()
        cp.wait()

    return pl.pallas_call(
        body, name=name, in_specs=[ANY], out_specs=ANY, out_shape=jax.ShapeDtypeStruct(a.shape[1:], a.dtype),
        scratch_shapes=[pltpu.SemaphoreType.DMA, pltpu.SemaphoreType.DMA],
    )(a)


def _join_halves(a, *, name):
    r, w = a.shape

    def body(a_ref, o_ref, send_sem, recv_sem, local_sem):
        x, y, c = _place()
        mine = o_ref.at[pl.ds(pl.multiple_of(c * r, SUBLANES), r)]
        local = pltpu.make_async_copy(a_ref, mine, local_sem)
        local.start()
        cp = pltpu.make_async_remote_copy(src_ref=a_ref, dst_ref=mine, send_sem=send_sem, recv_sem=recv_sem,
                                          device_id=(x, y, 1 - c), device_id_type=MESH)
        cp.start()
        cp.wait_send()
        other = o_ref.at[pl.ds(pl.multiple_of((1 - c) * r, SUBLANES), r)]
        pltpu.make_async_remote_copy(src_ref=other, dst_ref=other, send_sem=send_sem, recv_sem=recv_sem,
                                     device_id=(x, y, c), device_id_type=MESH).wait_recv()
        local.wait()

    return pl.pallas_call(
        body, name=name, in_specs=[ANY], out_specs=ANY, out_shape=jax.ShapeDtypeStruct((2 * r, w), a.dtype),
        scratch_shapes=[pltpu.SemaphoreType.DMA, pltpu.SemaphoreType.DMA, pltpu.SemaphoreType.DMA],
    )(a)


def _scatter_chips(a, *, name):
    def body(a_ref, o_ref, send_sems, recv_sems, local_sem):
        x, y, c = _place()
        m = 2 * x + y
        chips = [(1 - x, y), (x, 1 - y), (1 - x, 1 - y)]
        local = pltpu.make_async_copy(a_ref.at[m], o_ref.at[m], local_sem)
        local.start()
        sends = [_remote(a_ref.at[2 * px + py], o_ref.at[m], send_sems, recv_sems, j, (px, py, c)) for j, (px, py) in enumerate(chips)]
        for cp in sends:
            cp.start()
        for j, (px, py) in enumerate(chips):
            blk = o_ref.at[2 * px + py]
            _remote(blk, blk, send_sems, recv_sems, j, (x, y, c)).wait_recv()
        for cp in sends:
            cp.wait_send()
        local.wait()

    return pl.pallas_call(
        body, name=name, in_specs=[ANY], out_specs=ANY, out_shape=jax.ShapeDtypeStruct(a.shape, a.dtype),
        scratch_shapes=[pltpu.SemaphoreType.DMA((3,)), pltpu.SemaphoreType.DMA((3,)), pltpu.SemaphoreType.DMA],
    )(a)


def _allreduce_small(v, *, name):
    r, w = v.shape

    def body(v_ref, o_ref, slots, send_sems, recv_sems):
        x, y, c = _place()
        me = 4 * x + 2 * y + c
        slots[me] = v_ref[...]
        peers = [((1 - x) if k & 4 else x, (1 - y) if k & 2 else y, (1 - c) if k & 1 else c) for k in range(1, 8)]
        sends = [_remote(v_ref, slots.at[me], send_sems, recv_sems, k, p) for k, p in enumerate(peers)]
        for cp in sends:
            cp.start()
        for k, (px, py, pc) in enumerate(peers):
            blk = slots.at[4 * px + 2 * py + pc]
            _remote(blk, blk, send_sems, recv_sems, k, (x, y, c)).wait_recv()
        for cp in sends:
            cp.wait_send()
        acc = slots[0]
        for k in range(1, 8):
            acc = acc + slots[k]
        o_ref[...] = acc

    return pl.pallas_call(
        body, name=name, in_specs=[VMEM_SPEC], out_specs=VMEM_SPEC, out_shape=jax.ShapeDtypeStruct(v.shape, v.dtype),
        scratch_shapes=[pltpu.VMEM((8, r, w), F32), pltpu.SemaphoreType.DMA((7,)), pltpu.SemaphoreType.DMA((7,))],
    )(v)


def _add2(a, b, *, name):
    s, w = a.shape

    def fn(i, nrt, av, bv):
        return (av.astype(F32) + bv.astype(F32),), ()

    (o,) = _rows(fn, name=name, s=s, tm=_tile(s, (512, 256, 128, 64, 32, 16)), ins=[("row", a, w, _c0), ("row", b, w, _c0)],
                 outs=[(w, w, _c0, a.dtype)])
    return o


def _sum_blocks(a, *, name):
    n, r, w = a.shape
    tm = _tile(r, (512, 256, 128, 64, 32, 16))

    def body(a_ref, o_ref):
        acc = a_ref[0].astype(F32)
        for k in range(1, n):
            acc = acc + a_ref[k].astype(F32)
        o_ref[...] = acc

    return pl.pallas_call(
        body, name=name, grid=(r // tm,), in_specs=[pl.BlockSpec((n, tm, w), lambda i: (0, i, 0))],
        out_specs=pl.BlockSpec((tm, w), lambda i: (i, 0)), out_shape=jax.ShapeDtypeStruct((r, w), F32),
        compiler_params=_params(("parallel",)),
    )(a)


def _adamw(w, g, m, v, *, name):
    shape = w.shape
    cols = shape[-1]
    rows = math.prod(shape[:-1])
    tm = _tile(rows, (256, 128, 64, 32, 16, 8))
    c1 = 1.0 - ADAM_B1 ** ADAM_STEP
    c2 = 1.0 - ADAM_B2 ** ADAM_STEP

    def fn(i, nrt, wv, gv, mv, vv):
        mn = ADAM_B1 * mv + (1.0 - ADAM_B1) * gv
        vn = ADAM_B2 * vv + (1.0 - ADAM_B2) * (gv * gv)
        delta = -ADAM_LR * ((mn / c1) / (jnp.sqrt(vn / c2) + ADAM_EPS) + ADAM_WD * wv)
        return (delta, mn, vn), ()

    outs = _rows(fn, name=name, s=rows, tm=tm, ins=[("row", t.reshape(rows, cols), cols, _c0) for t in (w, g, m, v)],
                 outs=[(cols, cols, _c0, F32)] * 3)
    return [o.reshape(shape) for o in outs]


WEIGHTS = ["mix_norm", "ffn_norm", "ffn_w_gu", "ffn_w_down", "conv_w_in", "conv_w_dw", "conv_w_out", "fox_w_in", "fox_b_f", "fox_q_gain",
           "fox_k_gain", "fox_w_out", "ssd_w_in", "ssd_conv_w", "ssd_conv_b", "ssd_dt_bias", "ssd_a_log", "ssd_d", "ssd_norm_w", "ssd_w_out"]
SHARD_AXIS = {"ffn_w_gu": 2, "ffn_w_down": 1, "conv_w_in": 2, "conv_w_dw": 2, "conv_w_out": 1, "fox_w_in": 2, "fox_w_out": 1, "ssd_w_in": 2,
              "ssd_conv_w": 2, "ssd_conv_b": 1, "ssd_norm_w": 1, "ssd_w_out": 1}
BIG = ["ffn_w_gu", "ffn_w_down", "conv_w_in", "conv_w_out", "fox_w_in", "fox_w_out", "ssd_w_in", "ssd_w_out"]
SMALL_SHARDED = ["conv_w_dw", "ssd_conv_w", "ssd_conv_b", "ssd_norm_w"]
ROW_PAD = 1024
N_CHIPS = 4


def _pack_rows(parts, width, pad_to):
    mats = [p.reshape(-1, width) for p in parts]
    offs, n = [], 0
    for mt in mats:
        offs.append(n)
        n += mt.shape[0]
    total = -(-n // pad_to) * pad_to
    if total > n:
        mats.append(jnp.zeros((total - n, width), mats[0].dtype))
    return jnp.concatenate(mats, axis=0), offs


def _pack_flat(parts, pad_to):
    flat = [p.reshape(-1) for p in parts]
    offs, n = [], 0
    for f in flat:
        offs.append(n)
        n += f.shape[0]
    total = -(-n // pad_to) * pad_to
    if total > n:
        flat.append(jnp.zeros((total - n,), flat[0].dtype))
    return jnp.concatenate(flat).reshape(-1, LANES), offs


def kernel(x, mix_norm, ffn_norm, ffn_w_gu, ffn_w_down, conv_w_in, conv_w_dw, conv_w_out, fox_w_in, fox_b_f, fox_q_gain, fox_k_gain, fox_w_out, ssd_w_in, ssd_conv_w, ssd_conv_b, ssd_dt_bias, ssd_a_log, ssd_d, ssd_norm_w, ssd_w_out, loss_target, m_mix_norm, m_ffn_norm, m_ffn_w_gu, m_ffn_w_down, m_conv_w_in, m_conv_w_dw, m_conv_w_out, m_fox_w_in, m_fox_b_f, m_fox_q_gain, m_fox_k_gain, m_fox_w_out, m_ssd_w_in, m_ssd_conv_w, m_ssd_conv_b, m_ssd_dt_bias, m_ssd_a_log, m_ssd_d, m_ssd_norm_w, m_ssd_w_out, v_mix_norm, v_ffn_norm, v_ffn_w_gu, v_ffn_w_down, v_conv_w_in, v_conv_w_dw, v_conv_w_out, v_fox_w_in, v_fox_b_f, v_fox_q_gain, v_fox_k_gain, v_fox_w_out, v_ssd_w_in, v_ssd_conv_w, v_ssd_conv_b, v_ssd_dt_bias, v_ssd_a_log, v_ssd_d, v_ssd_norm_w, v_ssd_w_out):
    w = dict(zip(WEIGHTS, (mix_norm, ffn_norm, ffn_w_gu, ffn_w_down, conv_w_in, conv_w_dw, conv_w_out, fox_w_in, fox_b_f, fox_q_gain, fox_k_gain,
                           fox_w_out, ssd_w_in, ssd_conv_w, ssd_conv_b, ssd_dt_bias, ssd_a_log, ssd_d, ssd_norm_w, ssd_w_out)))
    m1 = dict(zip(WEIGHTS, (m_mix_norm, m_ffn_norm, m_ffn_w_gu, m_ffn_w_down, m_conv_w_in, m_conv_w_dw, m_conv_w_out, m_fox_w_in, m_fox_b_f,
                            m_fox_q_gain, m_fox_k_gain, m_fox_w_out, m_ssd_w_in, m_ssd_conv_w, m_ssd_conv_b, m_ssd_dt_bias, m_ssd_a_log, m_ssd_d,
                            m_ssd_norm_w, m_ssd_w_out)))
    m2 = dict(zip(WEIGHTS, (v_mix_norm, v_ffn_norm, v_ffn_w_gu, v_ffn_w_down, v_conv_w_in, v_conv_w_dw, v_conv_w_out, v_fox_w_in, v_fox_b_f,
                            v_fox_q_gain, v_fox_k_gain, v_fox_w_out, v_ssd_w_in, v_ssd_conv_w, v_ssd_conv_b, v_ssd_dt_bias, v_ssd_a_log, v_ssd_d,
                            v_ssd_norm_w, v_ssd_w_out)))
    d = x.shape[-1]
    cx, cy, cc = _place()
    chip = 2 * cx + cy

    wp, offs = _pack_rows([w[n].astype(WIRE_DTYPE) for n in BIG], d, ROW_PAD)
    gath = _gather_big(wp, name="gather_weights")
    gath = jnp.swapaxes(gath, 0, 1).reshape(N_CHIPS, wp.shape[0], d)
    sp, soffs = _pack_flat([w[n] for n in SMALL_SHARDED], SUBLANES * LANES)
    sgath = _gather_small(sp, name="gather_small").reshape(N_CHIPS, -1)
    full = {n: w[n] for n in WEIGHTS if n not in SHARD_AXIS}
    for n, off in zip(BIG, offs):
        rows = w[n].size // d
        full[n] = jnp.concatenate([gath[j, off:off + rows].reshape(w[n].shape) for j in range(N_CHIPS)], axis=SHARD_AXIS[n])
    for n, off in zip(SMALL_SHARDED, soffs):
        full[n] = jnp.concatenate([sgath[j, off:off + w[n].size].reshape(w[n].shape) for j in range(N_CHIPS)], axis=SHARD_AXIS[n])

    loss, gx, grads = _local_step(x[0], loss_target[0], full)
    loss = lax.psum(loss, ("x", "y", "c"))

    per_chip = []
    for j in range(N_CHIPS):
        parts = [jnp.split(grads[n], N_CHIPS, axis=SHARD_AXIS[n])[j].astype(WIRE_DTYPE) for n in BIG]
        per_chip.append(_pack_rows(parts, d, ROW_PAD)[0])
    r = per_chip[0].shape[0]
    rh = r // 2
    gp = jnp.stack([jnp.stack([pc[h * rh:(h + 1) * rh] for pc in per_chip]) for h in range(2)])
    from_sibling = _swap_sibling(gp, name="reduce_halves")
    mine = lax.dynamic_index_in_dim(gp, cc, 0, keepdims=False)
    chip_sum = _add2(mine.reshape(N_CHIPS * rh, d), from_sibling.reshape(N_CHIPS * rh, d), name="reduce_add_sibling").reshape(N_CHIPS, rh, d)
    by_chip = _scatter_chips(chip_sum, name="reduce_chips")
    red_half = _sum_blocks(by_chip, name="reduce_sum_chips")
    red = _join_halves(red_half, name="reduce_share")

    small_names = [n for n in WEIGHTS if n not in BIG]
    sm, smoffs = _pack_flat([grads[n] for n in small_names], SUBLANES * LANES)
    sred = _allreduce_small(sm, name="allreduce_small").reshape(-1)

    g = {}
    for n, off in zip(BIG, offs):
        g[n] = red[off:off + w[n].size // d].reshape(w[n].shape)
    for n, off in zip(small_names, smoffs):
        fullg = sred[off:off + grads[n].size].reshape(grads[n].shape)
        if n in SHARD_AXIS:
            ax = SHARD_AXIS[n]
            fullg = lax.dynamic_slice_in_dim(fullg, chip * w[n].shape[ax], w[n].shape[ax], axis=ax)
        g[n] = fullg

    deltas, new_m, new_v = [], [], []
    for n in WEIGHTS:
        dl, mn, vn = _adamw(w[n], g[n], m1[n], m2[n], name=f"adamw_{n}")
        deltas.append(dl)
        new_m.append(mn)
        new_v.append(vn)
    return (loss, gx[None], *[g[n] for n in WEIGHTS], *deltas, *new_m, *new_v)
```

```python
import functools
import math

import jax
import jax.numpy as jnp
from jax import lax
from jax.experimental import pallas as pl
from jax.experimental.pallas import tpu as pltpu

F32 = jnp.float32
MM_DTYPE = jnp.bfloat16
ACT_DTYPE = jnp.bfloat16
WIRE_DTYPE = jnp.bfloat16

RMS_EPS = 1e-6
HEAD = 64
SSM_STATE = 128
SSM_CHUNK = 128
LANES = 128
SUBLANES = 8
VMEM_LIMIT = 48 * 1024 * 1024

ADAM_LR, ADAM_B1, ADAM_B2, ADAM_EPS, ADAM_WD, ADAM_STEP = 0.001, 0.9, 0.999, 1e-08, 0.01, 10

HI = lax.Precision.HIGHEST
MESH = pl.DeviceIdType.MESH


def _tile(dim, prefs):
    for p in prefs:
        if dim % p == 0:
            return p
    return dim


def _params(sem):
    return pltpu.CompilerParams(dimension_semantics=sem, vmem_limit_bytes=VMEM_LIMIT)


def _sigmoid(x):
    return 1.0 / (1.0 + jnp.exp(-x))


def _softplus(x):
    return jnp.maximum(x, 0.0) + jnp.log(1.0 + jnp.exp(-jnp.abs(x)))


def _mm(a, b, *, ta=False, tb=False, add=None, out_dtype=F32, name):
    ka, m = (a.shape[0], a.shape[1]) if ta else (a.shape[1], a.shape[0])
    kb, n = (b.shape[1], b.shape[0]) if tb else (b.shape[0], b.shape[1])
    assert ka == kb, (a.shape, b.shape, ta, tb)
    k = ka
    tm = _tile(m, (1408, 1024, 512, 256, 128) if ta else (512, 256, 128))
    tn = _tile(n, (1024, 1408, 512, 256, 128))
    tk = _tile(k, (1024, 1408, 512, 256, 128))
    nk = k // tk
    a_spec = pl.BlockSpec((tk, tm), lambda i, j, q: (q, i)) if ta else pl.BlockSpec((tm, tk), lambda i, j, q: (i, q))
    b_spec = pl.BlockSpec((tn, tk), lambda i, j, q: (j, q)) if tb else pl.BlockSpec((tk, tn), lambda i, j, q: (q, j))
    o_spec = pl.BlockSpec((tm, tn), lambda i, j, q: (i, j))
    dims = (((0 if ta else 1,), (1 if tb else 0,)), ((), ()))
    has_add = add is not None

    def body(*refs):
        a_ref, b_ref = refs[0], refs[1]
        o_ref = refs[2 + has_add]
        p = lax.dot_general(a_ref[...].astype(MM_DTYPE), b_ref[...].astype(MM_DTYPE), dims, preferred_element_type=F32)

        def finish(acc):
            if has_add:
                acc = acc + refs[2][...].astype(F32)
            o_ref[...] = acc.astype(out_dtype)

        if nk == 1:
            finish(p)
        else:
            acc_ref = refs[3 + has_add]
            q = pl.program_id(2)

            @pl.when(q == 0)
            def _():
                acc_ref[...] = p

            @pl.when(q > 0)
            def _():
                acc_ref[...] += p

            @pl.when(q == nk - 1)
            def _():
                finish(acc_ref[...])

    args = [a, b] + ([add] if has_add else [])
    in_specs = [a_spec, b_spec] + ([o_spec] if has_add else [])
    return pl.pallas_call(
        body, name=name, grid=(m // tm, n // tn, nk), in_specs=in_specs, out_specs=o_spec,
        out_shape=jax.ShapeDtypeStruct((m, n), out_dtype),
        scratch_shapes=[pltpu.VMEM((tm, tn), F32)] if nk > 1 else [],
        compiler_params=_params(("parallel", "parallel", "arbitrary")),
    )(*args)


def _rows(fn, *, name, s, tm, ncol=1, ins, outs, accs=()):
    nrt = s // tm
    hb = tm // SUBLANES
    in_specs, args = [], []
    for spec in ins:
        kind, arr = spec[0], spec[1]
        if kind == "full":
            in_specs.append(pl.BlockSpec(arr.shape, lambda j, i: (0, 0)))
        elif kind == "col":
            _, _, bw, cmap = spec
            in_specs.append(pl.BlockSpec((arr.shape[0], bw), lambda j, i, cmap=cmap: (0, cmap(j))))
        elif kind == "row":
            _, _, bw, cmap = spec
            in_specs.append(pl.BlockSpec((tm, bw), lambda j, i, cmap=cmap: (i, cmap(j))))
        elif kind == "prev":
            _, _, bw, cmap = spec
            in_specs.append(pl.BlockSpec((SUBLANES, bw), lambda j, i, cmap=cmap: (jnp.maximum(i * hb - 1, 0), cmap(j))))
        elif kind == "next":
            _, _, bw, cmap = spec
            in_specs.append(pl.BlockSpec((SUBLANES, bw), lambda j, i, cmap=cmap: (jnp.minimum((i + 1) * hb, s // SUBLANES - 1), cmap(j))))
        else:
            raise ValueError(kind)
        args.append(arr)
    out_specs, out_shape = [], []
    for w, bw, cmap, dt in outs:
        out_specs.append(pl.BlockSpec((tm, bw), lambda j, i, cmap=cmap: (i, cmap(j))))
        out_shape.append(jax.ShapeDtypeStruct((s, w), dt))
    for r, w, bw, cmap in accs:
        out_specs.append(pl.BlockSpec((r, bw), lambda j, i, cmap=cmap: (0, cmap(j))))
        out_shape.append(jax.ShapeDtypeStruct((r, w), F32))
    n_in, n_out, n_acc = len(ins), len(outs), len(accs)

    def body(*refs):
        i = pl.program_id(1)
        vals = [r[...] for r in refs[:n_in]]
        o_vals, a_vals = fn(i, nrt, *vals)
        assert len(o_vals) == n_out and len(a_vals) == n_acc
        for r, v in zip(refs[n_in:n_in + n_out], o_vals):
            r[...] = v.astype(r.dtype)
        for r, v in zip(refs[n_in + n_out:], a_vals):
            @pl.when(i == 0)
            def _(r=r, v=v):
                r[...] = v.astype(F32)

            @pl.when(i > 0)
            def _(r=r, v=v):
                r[...] += v.astype(F32)

    res = pl.pallas_call(
        body, name=name, grid=(ncol, nrt), in_specs=in_specs, out_specs=out_specs, out_shape=out_shape,
        compiler_params=_params(("parallel", "arbitrary" if accs else "parallel")),
    )(*args)
    return res


def _c0(j):
    return 0


def _cj(j):
    return j


def _gmean(v, gs):
    w = v.shape[-1]
    tile = max(gs, LANES)
    r = lax.broadcasted_iota(jnp.int32, (tile, tile), 0) // gs
    c = lax.broadcasted_iota(jnp.int32, (tile, tile), 1) // gs
    g = jnp.where(r == c, 1.0 / gs, 0.0).astype(F32)
    parts = [jnp.dot(v[:, t * tile:(t + 1) * tile], g, precision=HI, preferred_element_type=F32) for t in range(w // tile)]
    return parts[0] if len(parts) == 1 else jnp.concatenate(parts, axis=1)


def _sum_rows(v):
    return jnp.sum(v, axis=0, keepdims=True)


def _resid_rms(x, y, w, *, name):
    s, d = x.shape
    has_y = y is not None

    def fn(i, nrt, *v):
        xv = v[0] + (v[1] if has_y else 0.0)
        wv = v[-1]
        r = lax.rsqrt(jnp.mean(xv * xv, axis=-1, keepdims=True) + RMS_EPS)
        return (xv, xv * r * wv), ()

    ins = [("row", x, d, _c0)] + ([("row", y, d, _c0)] if has_y else []) + [("full", w.reshape(1, d))]
    xn, h = _rows(fn, name=name, s=s, tm=_tile(s, (512, 256, 128)), ins=ins, outs=[(d, d, _c0, F32), (d, d, _c0, ACT_DTYPE)])
    return xn, h


def _rms_bwd(x, w, dh, dx_in, *, name):
    s, d = x.shape

    def fn(i, nrt, xv, wv, dhv, dxi):
        r = lax.rsqrt(jnp.mean(xv * xv, axis=-1, keepdims=True) + RMS_EPS)
        xh = xv * r
        g = dhv * wv
        dx = dxi + r * (g - xh * jnp.mean(g * xh, axis=-1, keepdims=True))
        return (dx, dx), (_sum_rows(dhv * xh),)

    dx, dxb, dw = _rows(fn, name=name, s=s, tm=_tile(s, (512, 256, 128)),
                        ins=[("row", x, d, _c0), ("full", w.reshape(1, d)), ("row", dh, d, _c0), ("row", dx_in, d, _c0)],
                        outs=[(d, d, _c0, F32), (d, d, _c0, MM_DTYPE)], accs=[(1, d, d, _c0)])
    return dx, dxb, dw.reshape(d)


def _col_tile(w):
    return _tile(w, (1408, 1024, 512, 256, 128))


def _swiglu_fwd(g, u, *, name):
    s, f = g.shape
    bw = _col_tile(f)

    def fn(i, nrt, gv, uv):
        gv = gv.astype(F32)
        return (gv * _sigmoid(gv) * uv.astype(F32),), ()

    (a,) = _rows(fn, name=name, s=s, tm=_tile(s, (512, 256, 128)), ncol=f // bw,
                 ins=[("row", g, bw, _cj), ("row", u, bw, _cj)], outs=[(f, bw, _cj, ACT_DTYPE)])
    return a


def _swiglu_bwd(g, u, da, *, name):
    s, f = g.shape
    bw = _col_tile(f)

    def fn(i, nrt, gv, uv, dav):
        gv, uv, dav = gv.astype(F32), uv.astype(F32), dav.astype(F32)
        sg = _sigmoid(gv)
        dg = dav * uv * sg * (1.0 + gv * (1.0 - sg))
        du = dav * gv * sg
        return (dg, du), ()

    dg, du = _rows(fn, name=name, s=s, tm=_tile(s, (512, 256, 128)), ncol=f // bw,
                   ins=[("row", g, bw, _cj), ("row", u, bw, _cj), ("row", da, bw, _cj)],
                   outs=[(f, bw, _cj, ACT_DTYPE), (f, bw, _cj, ACT_DTYPE)])
    return dg, du


def _loss_head(x, y, tgt, *, name):
    s, d = x.shape

    def fn(i, nrt, xv, yv, tv):
        diff = xv + yv - tv
        part = 0.5 * jnp.sum(diff * diff) / d
        return (diff / d, diff / d), (jnp.full((1, LANES), part, F32),)

    dy, dyb, loss = _rows(fn, name=name, s=s, tm=_tile(s, (512, 256, 128)),
                          ins=[("row", x, d, _c0), ("row", y, d, _c0), ("row", tgt, d, _c0)],
                          outs=[(d, d, _c0, F32), (d, d, _c0, MM_DTYPE)], accs=[(1, LANES, LANES, _c0)])
    return loss[0, 0], dy, dyb


def _shift_down(ext, j, tm):
    src = pltpu.roll(ext, j, 0) if j else ext
    return src[SUBLANES:SUBLANES + tm]


def _shift_up(ext, j, tm):
    return ext[:tm] if j == 0 else pltpu.roll(ext, ext.shape[0] - j, 0)[:tm]


def _gconv_fwd(b, c, v, w, *, name):
    s, d = b.shape
    kw = w.shape[0]
    bw = _tile(d, (512, 256, 128))
    tm = _tile(s, (512, 256, 128))

    def fn(i, nrt, bv, cv_, vv, pc, pv, wv):
        cv = cv_.astype(F32) * vv.astype(F32)
        pcv = jnp.where(i == 0, 0.0, pc.astype(F32) * pv.astype(F32))
        ext = jnp.concatenate([pcv, cv], axis=0)
        u = sum(wv[k:k + 1, :] * _shift_down(ext, kw - 1 - k, tm) for k in range(kw))
        return (bv.astype(F32) * u,), ()

    (o,) = _rows(fn, name=name, s=s, tm=tm, ncol=d // bw,
                 ins=[("row", b, bw, _cj), ("row", c, bw, _cj), ("row", v, bw, _cj), ("prev", c, bw, _cj), ("prev", v, bw, _cj),
                      ("col", w, bw, _cj)],
                 outs=[(d, bw, _cj, ACT_DTYPE)])
    return o


def _gconv_bwd(b, c, v, w, do, *, name):
    s, d = b.shape
    kw = w.shape[0]
    bw = _tile(d, (512, 256, 128))
    tm = _tile(s, (512, 256, 128))

    def fn(i, nrt, bv, cv_, vv, pc, pv, dov, nb, ndo, wv):
        bv, cv_, vv, dov = bv.astype(F32), cv_.astype(F32), vv.astype(F32), dov.astype(F32)
        cv = cv_ * vv
        pcv = jnp.where(i == 0, 0.0, pc.astype(F32) * pv.astype(F32))
        ext = jnp.concatenate([pcv, cv], axis=0)
        shifted = [_shift_down(ext, kw - 1 - k, tm) for k in range(kw)]
        u = sum(wv[k:k + 1, :] * shifted[k] for k in range(kw))
        db = dov * u
        du = dov * bv
        ndu = jnp.where(i == nrt - 1, 0.0, ndo.astype(F32) * nb.astype(F32))
        ext2 = jnp.concatenate([du, ndu], axis=0)
        dcv = sum(wv[k:k + 1, :] * _shift_up(ext2, kw - 1 - k, tm) for k in range(kw))
        dw = jnp.concatenate([_sum_rows(du * shifted[k]) for k in range(kw)], axis=0)
        return (db, dcv * vv, dcv * cv_), (dw,)

    db, dc, dv, dw = _rows(fn, name=name, s=s, tm=tm, ncol=d // bw,
                           ins=[("row", b, bw, _cj), ("row", c, bw, _cj), ("row", v, bw, _cj), ("prev", c, bw, _cj),
                                ("prev", v, bw, _cj), ("row", do, bw, _cj), ("next", b, bw, _cj), ("next", do, bw, _cj),
                                ("col", w, bw, _cj)],
                           outs=[(d, bw, _cj, ACT_DTYPE)] * 3, accs=[(kw, d, bw, _cj)])
    return db, dc, dv, dw


def _sconv_fwd(x, w, bias, *, name):
    s, d = x.shape
    kw = w.shape[0]
    bw = _tile(d, (512, 256, 128))
    tm = _tile(s, (512, 256, 128))

    def fn(i, nrt, xv, px, wv, bsv):
        xv = xv.astype(F32)
        ext = jnp.concatenate([jnp.where(i == 0, 0.0, px.astype(F32)), xv], axis=0)
        pre = sum(wv[k:k + 1, :] * _shift_down(ext, kw - 1 - k, tm) for k in range(kw)) + bsv
        return (pre * _sigmoid(pre),), ()

    (o,) = _rows(fn, name=name, s=s, tm=tm, ncol=d // bw,
                 ins=[("row", x, bw, _cj), ("prev", x, bw, _cj), ("col", w, bw, _cj), ("col", bias.reshape(1, d), bw, _cj)],
                 outs=[(d, bw, _cj, ACT_DTYPE)])
    return o


def _sconv_bwd(x, w, bias, dact, *, name):
    s, d = x.shape
    kw = w.shape[0]
    bw = _tile(d, (512, 256, 128))
    tm = _tile(s, (512, 256, 128))

    def fn(i, nrt, xv, px, nx, dav, nda, wv, bsv):
        xv = xv.astype(F32)
        ext = jnp.concatenate([jnp.where(i == 0, 0.0, px.astype(F32)), xv, nx.astype(F32)], axis=0)
        rows_e = tm + SUBLANES
        pre_e = sum(wv[k:k + 1, :] * _shift_down(ext, kw - 1 - k, rows_e) for k in range(kw)) + bsv
        da_e = jnp.concatenate([dav.astype(F32), jnp.where(i == nrt - 1, 0.0, nda.astype(F32))], axis=0)
        sg = _sigmoid(pre_e)
        dpre_e = da_e * sg * (1.0 + pre_e * (1.0 - sg))
        dx = sum(wv[k:k + 1, :] * _shift_up(dpre_e, kw - 1 - k, tm) for k in range(kw))
        dpre = dpre_e[:tm]
        dw = jnp.concatenate([_sum_rows(dpre * _shift_down(ext, kw - 1 - k, tm)) for k in range(kw)], axis=0)
        return (dx,), (dw, _sum_rows(dpre))

    dx, dw, db = _rows(fn, name=name, s=s, tm=tm, ncol=d // bw,
                       ins=[("row", x, bw, _cj), ("prev", x, bw, _cj), ("next", x, bw, _cj), ("row", dact, bw, _cj),
                            ("next", dact, bw, _cj), ("col", w, bw, _cj), ("col", bias.reshape(1, d), bw, _cj)],
                       outs=[(d, bw, _cj, ACT_DTYPE)], accs=[(kw, d, bw, _cj), (1, d, bw, _cj)])
    return dx, dw, db.reshape(d)


def _tri(n, reverse):
    r = lax.broadcasted_iota(jnp.int32, (n, n), 0)
    c = lax.broadcasted_iota(jnp.int32, (n, n), 1)
    return jnp.where((c >= r) if reverse else (c <= r), 1.0, 0.0).astype(F32)


def _cumsum_rows(x, *, reverse, name):
    s, w = x.shape
    ch = _tile(s, (256, 128))
    n = s // ch

    def body(x_ref, o_ref, carry):
        i = pl.program_id(0)

        @pl.when(i == 0)
        def _():
            carry[...] = jnp.zeros_like(carry)

        out = jnp.dot(_tri(ch, reverse), x_ref[...], precision=HI, preferred_element_type=F32) + carry[...]
        o_ref[...] = out
        carry[...] = out[0:1, :] if reverse else out[ch - 1:ch, :]

    imap = (lambda i: (n - 1 - i, 0)) if reverse else (lambda i: (i, 0))
    return pl.pallas_call(
        body, name=name, grid=(n,), in_specs=[pl.BlockSpec((ch, w), imap)], out_specs=pl.BlockSpec((ch, w), imap),
        out_shape=jax.ShapeDtypeStruct((s, w), F32), scratch_shapes=[pltpu.VMEM((1, w), F32)],
        compiler_params=_params(("arbitrary",)),
    )(x)


def _fox_prep(q, k, f, gq, gk, bf, *, name):
    s, d = q.shape
    scale = HEAD ** -0.5

    def fn(i, nrt, qv, kv, fv, gqv, gkv, bfv):
        qv, kv = qv.astype(F32), kv.astype(F32)
        qn = qv * lax.rsqrt(_gmean(qv * qv, HEAD) + RMS_EPS) * gqv * scale
        kn = kv * lax.rsqrt(_gmean(kv * kv, HEAD) + RMS_EPS) * gkv
        z = fv + bfv
        logf = jnp.minimum(z, 0.0) - jnp.log(1.0 + jnp.exp(-jnp.abs(z)))
        return (qn, kn, logf), ()

    return _rows(fn, name=name, s=s, tm=_tile(s, (512, 256, 128)),
                 ins=[("row", q, d, _c0), ("row", k, d, _c0), ("row", f, LANES, _c0), ("full", gq), ("full", gk), ("full", bf)],
                 outs=[(d, d, _c0, ACT_DTYPE), (d, d, _c0, ACT_DTYPE), (LANES, LANES, _c0, F32)])


def _fox_prep_bwd(q, k, f, gq, gk, bf, dqs, dkn, dlogf, *, name):
    s, d = q.shape
    scale = HEAD ** -0.5

    def fn(i, nrt, qv, kv, fv, gqv, gkv, bfv, dqv, dkv, dlf):
        outs, accs = [], []
        for xv, gv, dv, sc in ((qv, gqv, dqv, scale), (kv, gkv, dkv, 1.0)):
            xv, dv = xv.astype(F32), dv.astype(F32) * sc
            r = lax.rsqrt(_gmean(xv * xv, HEAD) + RMS_EPS)
            xh = xv * r
            g = dv * gv
            outs.append(r * (g - xh * _gmean(g * xh, HEAD)))
            accs.append(_sum_rows(dv * xh))
        z = fv + bfv
        df = dlf * _sigmoid(-z)
        outs.append(df)
        accs.append(_sum_rows(df))
        return outs, accs

    return _rows(fn, name=name, s=s, tm=_tile(s, (512, 256, 128)),
                 ins=[("row", q, d, _c0), ("row", k, d, _c0), ("row", f, LANES, _c0), ("full", gq), ("full", gk), ("full", bf),
                      ("row", dqs, d, _c0), ("row", dkn, d, _c0), ("row", dlogf, LANES, _c0)],
                 outs=[(d, d, _c0, ACT_DTYPE), (d, d, _c0, ACT_DTYPE), (LANES, LANES, _c0, ACT_DTYPE)],
                 accs=[(1, d, d, _c0), (1, d, d, _c0), (1, LANES, LANES, _c0)])


def _head_masks(shape):
    lane = lax.broadcasted_iota(jnp.int32, shape, len(shape) - 1)
    return lane < HEAD, lane >= HEAD


def _pick_lane(blk, idx):
    lane = lax.broadcasted_iota(jnp.int32, blk.shape, 1)
    return jnp.sum(jnp.where(lane == idx, blk, 0.0), axis=1, keepdims=True)


def _pick_row(blk, idx):
    sub = lax.broadcasted_iota(jnp.int32, blk.shape, 0)
    return jnp.sum(jnp.where(sub == idx, blk, 0.0), axis=0, keepdims=True)


def _fox_aug(qs, kn, cum, *, name):
    s, d = qs.shape
    hp = d // LANES

    def fn(i, nrt, qv, kv, cv):
        lane = lax.broadcasted_iota(jnp.int32, (qv.shape[0], LANES), 1)
        outs = [[], [], [], []]
        for p in range(hp):
            qt, kt = qv[:, p * LANES:(p + 1) * LANES], kv[:, p * LANES:(p + 1) * LANES]
            for h in range(2):
                mine = (lane < HEAD) if h == 0 else (lane >= HEAD)
                a0 = HEAD if h == 0 else 0
                c = cv[:, 2 * p + h:2 * p + h + 1]
                hi = c.astype(ACT_DTYPE).astype(F32)
                mid = (c - hi).astype(ACT_DTYPE).astype(F32)
                lo = (c - hi - mid).astype(ACT_DTYPE).astype(F32)
                ones = jnp.where((lane >= a0) & (lane < a0 + 3), 1.0, 0.0)
                kx = jnp.where(lane == a0, -hi, jnp.where(lane == a0 + 1, -mid, jnp.where(lane == a0 + 2, -lo, 0.0)))
                outs[h].append(jnp.where(mine, qt.astype(F32), ones))
                outs[2 + h].append(jnp.where(mine, kt.astype(F32), kx))
        return [jnp.concatenate(o, axis=1) for o in outs], ()

    return _rows(fn, name=name, s=s, tm=_tile(s, (512, 256, 128)), ins=[("row", qs, d, _c0), ("row", kn, d, _c0), ("row", cum, LANES, _c0)],
                 outs=[(d, d, _c0, ACT_DTYPE)] * 4)


def _tri_tables(nq, by_key):
    import numpy as np
    pairs = [(qi, kj) for kj in range(nq) for qi in range(kj, nq)] if by_key else [(qi, kj) for qi in range(nq) for kj in range(qi + 1)]
    return jnp.asarray(np.array([p[0] for p in pairs], np.int32)), jnp.asarray(np.array([p[1] for p in pairs], np.int32))


def _nt(a, b):
    return lax.dot_general(a, b, (((1,), (1,)), ((), ())), preferred_element_type=F32)


def _tn(a, b):
    return lax.dot_general(a, b, (((0,), (0,)), ((), ())), preferred_element_type=F32)


def _fox_dd(do, o, *, name):
    s, d = do.shape

    def fn(i, nrt, dov, ov):
        return (_reduce_heads(dov.astype(F32) * ov.astype(F32), d // HEAD, HEAD),), ()

    (dd,) = _rows(fn, name=name, s=s, tm=_tile(s, (512, 256, 128)), ins=[("row", do, d, _c0), ("row", o, d, _c0)],
                  outs=[(LANES, LANES, _c0, F32)])
    return dd


def _pair_rows(a, nh):
    s = a.shape[0]
    t = a[:, :nh].T.reshape(nh // 2, 2, s)
    return jnp.pad(t, ((0, 0), (0, SUBLANES - 2), (0, 0)))


def _rows01(r0, r1):
    sub = lax.broadcasted_iota(jnp.int32, (SUBLANES, r0.shape[1]), 0)
    return jnp.where(sub == 0, r0, jnp.where(sub == 1, r1, 0.0))


def _fox_fwd_t(q_aug, k_aug, v, *, name):
    s, d = v.shape
    bq = _tile(s, (512, 256, 128))
    nq = s // bq
    hp = d // LANES
    qtab, ktab = _tri_tables(nq, by_key=False)

    def body(qt, kt, q0_ref, q1_ref, k0_ref, k1_ref, v_ref, o_ref, lse_ref, m_sc, l_sc, acc_sc):
        t = pl.program_id(1)
        qi, kj = qt[t], kt[t]

        @pl.when(kj == 0)
        def _():
            m_sc[...] = jnp.full_like(m_sc, -jnp.inf)
            l_sc[...] = jnp.zeros_like(l_sc)
            acc_sc[...] = jnp.zeros_like(acc_sc)

        def update(diagonal):
            v2 = v_ref[...]
            for h, (q_ref, k_ref) in enumerate(((q0_ref, k0_ref), (q1_ref, k1_ref))):
                st = _nt(k_ref[...], q_ref[...])
                if diagonal:
                    st = _diag_mask_t(st)
                m_prev = m_sc[h]
                m_new = jnp.maximum(m_prev, jnp.max(st, axis=0, keepdims=True))
                p = jnp.exp(st - m_new)
                alpha = jnp.exp(m_prev - m_new)
                l_sc[h] = alpha * l_sc[h] + jnp.sum(p, axis=0, keepdims=True)
                acc_sc[h] = alpha * acc_sc[h] + _tn(v2, p.astype(MM_DTYPE))
                m_sc[h] = m_new

        @pl.when(kj < qi)
        def _():
            update(False)

        @pl.when(kj == qi)
        def _():
            update(True)
            row = lax.broadcasted_iota(jnp.int32, (LANES, bq), 0)
            ot = jnp.where(row < HEAD, acc_sc[0] / l_sc[0], acc_sc[1] / l_sc[1])
            o_ref[...] = ot.T.astype(o_ref.dtype)
            lse_ref[...] = _rows01(m_sc[0] + jnp.log(l_sc[0]), m_sc[1] + jnp.log(l_sc[1]))

    blk = (bq, LANES)
    qmap = lambda p_, t, qt, kt: (qt[t], p_)
    kmap = lambda p_, t, qt, kt: (kt[t], p_)
    grid_spec = pltpu.PrefetchScalarGridSpec(
        num_scalar_prefetch=2, grid=(hp, qtab.shape[0]),
        in_specs=[pl.BlockSpec(blk, qmap), pl.BlockSpec(blk, qmap), pl.BlockSpec(blk, kmap), pl.BlockSpec(blk, kmap), pl.BlockSpec(blk, kmap)],
        out_specs=[pl.BlockSpec(blk, qmap), pl.BlockSpec((None, SUBLANES, bq), lambda p_, t, qt, kt: (p_, 0, qt[t]))],
        scratch_shapes=[pltpu.VMEM((2, 1, bq), F32), pltpu.VMEM((2, 1, bq), F32), pltpu.VMEM((2, LANES, bq), F32)])
    o, lse = pl.pallas_call(
        body, name=name, grid_spec=grid_spec,
        out_shape=[jax.ShapeDtypeStruct((s, d), ACT_DTYPE), jax.ShapeDtypeStruct((hp, SUBLANES, s), F32)],
        compiler_params=_params(("parallel", "arbitrary")),
    )(qtab, ktab, q_aug[0], q_aug[1], k_aug[0], k_aug[1], v)
    return o, lse


def _diag_mask_t(st):
    key = lax.broadcasted_iota(jnp.int32, st.shape, 0)
    qry = lax.broadcasted_iota(jnp.int32, st.shape, 1)
    return jnp.where(qry >= key, st, -jnp.inf)


def _fox_bwd_t(q_aug, k_aug, v, lse, dd, do, *, name):
    s, d = v.shape
    bq = _tile(s, (512, 256, 128))
    nq = s // bq
    hp = d // LANES
    blk = (bq, LANES)
    qtab, ktab = _tri_tables(nq, by_key=True)
    n_steps = qtab.shape[0]

    def body(qt, kt, q0_ref, q1_ref, k0_ref, k1_ref, v_ref, lse_ref, dd_ref, do_ref,
             dq_ref, dk_ref, dv_ref, dcol_ref, drow_ref, dq_sc, rs_sc, dk_sc, dv_sc, cs_sc):
        t = pl.program_id(1)
        qi, kj = qt[t], kt[t]

        @pl.when(t == 0)
        def _():
            dq_sc[...] = jnp.zeros_like(dq_sc)
            rs_sc[...] = jnp.zeros_like(rs_sc)

        def update(diagonal):
            v2, do2 = v_ref[...], do_ref[...]
            masks = _head_masks(blk)
            row = lax.broadcasted_iota(jnp.int32, (LANES, bq), 0)
            off = pl.multiple_of(qi * bq, bq)
            for h, (q_ref, k_ref) in enumerate(((q0_ref, k0_ref), (q1_ref, k1_ref))):
                st = _nt(k_ref[...], q_ref[...])
                if diagonal:
                    st = _diag_mask_t(st)
                p = jnp.exp(st - lse_ref[h:h + 1, :])
                dp = _nt(v2, jnp.where(masks[h], do2, jnp.zeros_like(do2)))
                ds = p * (dp - dd_ref[h:h + 1, :])
                dsb = ds.astype(MM_DTYPE)
                dv_sc[h] += jnp.dot(p.astype(MM_DTYPE), do2, preferred_element_type=F32)
                dk_sc[h] += jnp.dot(dsb, q_ref[...], preferred_element_type=F32)
                cs_sc[h] += jnp.sum(ds, axis=1, keepdims=True)
                mine = (row < HEAD) if h == 0 else (row >= HEAD)
                dq_sc[:, pl.ds(off, bq)] += jnp.where(mine, _tn(k_ref[...], dsb), 0.0)
                rs_sc[h:h + 1, pl.ds(off, bq)] += jnp.sum(ds, axis=0, keepdims=True)

        @pl.when(qi == kj)
        def _():
            dk_sc[...] = jnp.zeros_like(dk_sc)
            dv_sc[...] = jnp.zeros_like(dv_sc)
            cs_sc[...] = jnp.zeros_like(cs_sc)
            update(True)

        @pl.when(qi > kj)
        def _():
            update(False)

        @pl.when(qi == nq - 1)
        def _():
            lo, _hi = _head_masks(blk)
            dk_ref[...] = jnp.where(lo, dk_sc[0], dk_sc[1]).astype(dk_ref.dtype)
            dv_ref[...] = jnp.where(lo, dv_sc[0], dv_sc[1]).astype(dv_ref.dtype)
            dcol_ref[...] = jnp.where(lo, cs_sc[0], cs_sc[1])

        @pl.when(t == n_steps - 1)
        def _():
            for c in range(nq):
                dq_ref[c * bq:(c + 1) * bq, :] = dq_sc[:, c * bq:(c + 1) * bq].T.astype(dq_ref.dtype)
            drow_ref[...] = rs_sc[...]

    qmap = lambda p_, t, qt, kt: (qt[t], p_)
    kmap = lambda p_, t, qt, kt: (kt[t], p_)
    rmap = lambda p_, t, qt, kt: (p_, 0, qt[t])
    return pl.pallas_call(
        body, name=name,
        grid_spec=pltpu.PrefetchScalarGridSpec(
            num_scalar_prefetch=2, grid=(hp, n_steps),
            in_specs=[pl.BlockSpec(blk, qmap), pl.BlockSpec(blk, qmap), pl.BlockSpec(blk, kmap), pl.BlockSpec(blk, kmap), pl.BlockSpec(blk, kmap),
                      pl.BlockSpec((None, SUBLANES, bq), rmap), pl.BlockSpec((None, SUBLANES, bq), rmap), pl.BlockSpec(blk, qmap)],
            out_specs=[pl.BlockSpec((s, LANES), lambda p_, t, qt, kt: (0, p_)), pl.BlockSpec(blk, kmap), pl.BlockSpec(blk, kmap),
                       pl.BlockSpec(blk, kmap), pl.BlockSpec((None, SUBLANES, s), lambda p_, t, qt, kt: (p_, 0, 0))],
            scratch_shapes=[pltpu.VMEM((LANES, s), F32), pltpu.VMEM((SUBLANES, s), F32), pltpu.VMEM((2, bq, LANES), F32),
                            pltpu.VMEM((2, bq, LANES), F32), pltpu.VMEM((2, bq, 1), F32)]),
        out_shape=[jax.ShapeDtypeStruct((s, d), ACT_DTYPE), jax.ShapeDtypeStruct((s, d), ACT_DTYPE), jax.ShapeDtypeStruct((s, d), ACT_DTYPE),
                   jax.ShapeDtypeStruct((s, d), F32), jax.ShapeDtypeStruct((hp, SUBLANES, s), F32)],
        compiler_params=_params(("parallel", "arbitrary")),
    )(qtab, ktab, q_aug[0], q_aug[1], k_aug[0], k_aug[1], v, lse, dd, do)


def _expand_heads(v, nh, hd):
    r = lax.broadcasted_iota(jnp.int32, (LANES, nh * hd), 0)
    c = lax.broadcasted_iota(jnp.int32, (LANES, nh * hd), 1) // hd
    e = jnp.where(r == c, 1.0, 0.0).astype(F32)
    return jnp.dot(v, e, precision=HI, preferred_element_type=F32)


def _reduce_heads(v, nh, hd):
    r = lax.broadcasted_iota(jnp.int32, (nh * hd, LANES), 0) // hd
    c = lax.broadcasted_iota(jnp.int32, (nh * hd, LANES), 1)
    e = jnp.where(r == c, 1.0, 0.0).astype(F32)
    return jnp.dot(v, e, precision=HI, preferred_element_type=F32)


def _ssd_prep(dt_raw, dt_bias, a_log, nh, *, name):
    s = dt_raw.shape[0]

    def fn(i, nrt, dtr, bsv, alv):
        dt = _softplus(dtr + bsv)
        acum = jnp.dot(_tri(SSM_CHUNK, False), dt * (-jnp.exp(alv)), precision=HI, preferred_element_type=F32)
        return (dt, acum, _expand_heads(dt, nh, HEAD), _expand_heads(acum, nh, HEAD)), ()

    w = nh * HEAD
    return _rows(fn, name=name, s=s, tm=SSM_CHUNK, ins=[("row", dt_raw, LANES, _c0), ("full", dt_bias), ("full", a_log)],
                 outs=[(LANES, LANES, _c0, F32), (LANES, LANES, _c0, F32), (w, w, _c0, F32), (w, w, _c0, F32)])


def _ssd_prep_bwd(dt_raw, dt_bias, a_log, ddtx, dacx, nh, *, name):
    s = dt_raw.shape[0]

    def fn(i, nrt, dtr, bsv, alv, ddx, dax):
        z = dtr + bsv
        dt = _softplus(z)
        a = -jnp.exp(alv)
        dda = jnp.dot(_tri(SSM_CHUNK, True), _reduce_heads(dax, nh, HEAD), precision=HI, preferred_element_type=F32)
        ddt = _reduce_heads(ddx, nh, HEAD) + dda * a
        dz = ddt * _sigmoid(z)
        lane = lax.broadcasted_iota(jnp.int32, dz.shape, 1)
        dz = jnp.where(lane < nh, dz, 0.0)
        return (dz,), (_sum_rows(dz), _sum_rows(dda * dt) * a)

    w = nh * HEAD
    return _rows(fn, name=name, s=s, tm=SSM_CHUNK,
                 ins=[("row", dt_raw, LANES, _c0), ("full", dt_bias), ("full", a_log), ("row", ddtx, w, _c0), ("row", dacx, w, _c0)],
                 outs=[(LANES, LANES, _c0, ACT_DTYPE)], accs=[(1, LANES, LANES, _c0), (1, LANES, LANES, _c0)])


def _ssd_decay(ac_blk, act_blk, head):
    col = _pick_lane(ac_blk, head)
    row = _pick_row(act_blk, head)
    r = lax.broadcasted_iota(jnp.int32, (SSM_CHUNK, SSM_CHUNK), 0)
    c = lax.broadcasted_iota(jnp.int32, (SSM_CHUNK, SSM_CHUNK), 1)
    return jnp.exp(jnp.where(r >= c, col - row, -jnp.inf))


def _group_masks(shape, hpg):
    lane = lax.broadcasted_iota(jnp.int32, shape, len(shape) - 1) // HEAD
    return [lane == k for k in range(hpg)]


def _ssd_scan_fwd(xs, bm, cm, dtx, acx, acum, acum_t, d_x, *, name):
    s, di = xs.shape
    ng = bm.shape[1] // SSM_STATE
    gw = di // ng
    hpg = gw // HEAD
    nc = s // SSM_CHUNK
    L = SSM_CHUNK
    nh_pad = acum_t.shape[0]

    def body(x_ref, b_ref, c_ref, dt_ref, ax_ref, ac_ref, act_ref, d_ref, y_ref, st_ref, state):
        g, c = pl.program_id(0), pl.program_id(1)

        @pl.when(c == 0)
        def _():
            state[...] = jnp.zeros_like(state)

        x4, bv, cv = x_ref[...].astype(F32), b_ref[...], c_ref[...]
        ax = ax_ref[...]
        tx = (x4 * dt_ref[...])
        cb = lax.dot_general(cv, bv, (((1,), (1,)), ((), ())), preferred_element_type=F32)
        masks = _group_masks((L, gw), hpg)
        y = jnp.zeros((L, gw), F32)
        txb = tx.astype(MM_DTYPE)
        for k in range(hpg):
            wk = (cb * _ssd_decay(ac_ref[...], act_ref[...], g * hpg + k)).astype(MM_DTYPE)
            y = y + jnp.where(masks[k], jnp.dot(wk, txb, preferred_element_type=F32), 0.0)
        prev = state[...]
        st_ref[...] = prev
        y = y + jnp.dot(cv, prev.astype(MM_DTYPE), preferred_element_type=F32) * jnp.exp(ax)
        y = y + d_ref[...] * x4
        y_ref[...] = y.astype(y_ref.dtype)
        a_last = ax[L - 1:L, :]
        sx = (tx * jnp.exp(a_last - ax)).astype(MM_DTYPE)
        state[...] = prev * jnp.exp(a_last) + lax.dot_general(bv, sx, (((0,), (0,)), ((), ())), preferred_element_type=F32)

    y, states = pl.pallas_call(
        body, name=name, grid=(ng, nc),
        in_specs=[pl.BlockSpec((L, gw), lambda g, c: (c, g)), pl.BlockSpec((L, SSM_STATE), lambda g, c: (c, g)),
                  pl.BlockSpec((L, SSM_STATE), lambda g, c: (c, g)), pl.BlockSpec((L, gw), lambda g, c: (c, g)),
                  pl.BlockSpec((L, gw), lambda g, c: (c, g)), pl.BlockSpec((L, LANES), lambda g, c: (c, 0)),
                  pl.BlockSpec((nh_pad, L), lambda g, c: (0, c)), pl.BlockSpec((1, gw), lambda g, c: (0, g))],
        out_specs=[pl.BlockSpec((L, gw), lambda g, c: (c, g)), pl.BlockSpec((None, None, SSM_STATE, gw), lambda g, c: (g, c, 0, 0))],
        out_shape=[jax.ShapeDtypeStruct((s, di), ACT_DTYPE), jax.ShapeDtypeStruct((ng, nc, SSM_STATE, gw), F32)],
        scratch_shapes=[pltpu.VMEM((SSM_STATE, gw), F32)],
        compiler_params=_params(("parallel", "arbitrary")),
    )(xs, bm, cm, dtx, acx, acum, acum_t, d_x)
    return y, states


def _ssd_scan_bwd(xs, bm, cm, dtx, acx, acum, acum_t, d_x, states, dy, *, name):
    s, di = xs.shape
    ng = bm.shape[1] // SSM_STATE
    gw = di // ng
    hpg = gw // HEAD
    nc = s // SSM_CHUNK
    L = SSM_CHUNK
    nh_pad = acum_t.shape[0]

    def body(x_ref, b_ref, c_ref, dt_ref, ax_ref, ac_ref, act_ref, d_ref, st_ref, dy_ref,
             dx_ref, db_ref, dc_ref, ddt_ref, dax_ref, dd_ref, dstate):
        g, cc = pl.program_id(0), pl.program_id(1)

        @pl.when(cc == 0)
        def _():
            dstate[...] = jnp.zeros_like(dstate)
            dd_ref[...] = jnp.zeros_like(dd_ref)

        x4, bv, cv = x_ref[...].astype(F32), b_ref[...], c_ref[...]
        tv, ax, dyv = dt_ref[...], ax_ref[...], dy_ref[...].astype(F32)
        prev, dn = st_ref[...], dstate[...]
        dnb = dn.astype(MM_DTYPE)
        masks = _group_masks((L, gw), hpg)
        tx = x4 * tv
        txb = tx.astype(MM_DTYPE)
        e_ax = jnp.exp(ax)
        a_last = ax[L - 1:L, :]
        e_last = jnp.exp(a_last)
        ed = jnp.exp(a_last - ax)

        dx = d_ref[...] * dyv
        dd_ref[...] += _sum_rows(dyv * x4)
        dye = (dyv * e_ax).astype(MM_DTYPE)
        yo = jnp.dot(cv, prev.astype(MM_DTYPE), preferred_element_type=F32) * e_ax
        dc = lax.dot_general(dye, prev.astype(MM_DTYPE), (((1,), (1,)), ((), ())), preferred_element_type=F32)
        dprev = lax.dot_general(cv, dye, (((0,), (0,)), ((), ())), preferred_element_type=F32)
        dax = dyv * yo
        sx = tx * ed
        dsx = jnp.dot(bv, dnb, preferred_element_type=F32)
        db = lax.dot_general(sx.astype(MM_DTYPE), dnb, (((1,), (1,)), ((), ())), preferred_element_type=F32)
        dtx_ = dsx * ed
        dsx_sx = dsx * sx
        dax = dax - dsx_sx
        dlast = _sum_rows(dsx_sx) + _sum_rows(dn * prev) * e_last
        dprev = dprev + dn * e_last
        cb = lax.dot_general(cv, bv, (((1,), (1,)), ((), ())), preferred_element_type=F32)
        dcb = jnp.zeros((L, L), F32)
        lane = lax.broadcasted_iota(jnp.int32, (L, gw), 1)
        for k in range(hpg):
            dec = _ssd_decay(ac_ref[...], act_ref[...], g * hpg + k)
            wk = (cb * dec).astype(MM_DTYPE)
            dyk = jnp.where(masks[k], dyv, 0.0).astype(MM_DTYPE)
            dtx_ = dtx_ + jnp.where(masks[k], lax.dot_general(wk, dyk, (((0,), (0,)), ((), ())), preferred_element_type=F32), 0.0)
            dwk = lax.dot_general(dyk, txb, (((1,), (1,)), ((), ())), preferred_element_type=F32)
            dcb = dcb + dwk * dec
            mk = dwk * cb * dec
            da_k = jnp.sum(mk, axis=1, keepdims=True) - jnp.sum(mk.T, axis=1, keepdims=True)
            dax = dax + jnp.where(lane == k * HEAD, da_k, 0.0)
        dcbb = dcb.astype(MM_DTYPE)
        dc = dc + jnp.dot(dcbb, bv, preferred_element_type=F32)
        db = db + lax.dot_general(dcbb, cv, (((0,), (0,)), ((), ())), preferred_element_type=F32)
        sub = lax.broadcasted_iota(jnp.int32, (L, gw), 0)
        dax = dax + jnp.where(sub == L - 1, dlast, 0.0)
        dx_ref[...] = (dx + dtx_ * tv).astype(dx_ref.dtype)
        ddt_ref[...] = dtx_ * x4
        dax_ref[...] = dax
        db_ref[...] = db.astype(db_ref.dtype)
        dc_ref[...] = dc.astype(dc_ref.dtype)
        dstate[...] = dprev

    rev = lambda g, c: (nc - 1 - c, g)
    rev0 = lambda g, c: (nc - 1 - c, 0)
    outs = pl.pallas_call(
        body, name=name, grid=(ng, nc),
        in_specs=[pl.BlockSpec((L, gw), rev), pl.BlockSpec((L, SSM_STATE), rev), pl.BlockSpec((L, SSM_STATE), rev),
                  pl.BlockSpec((L, gw), rev), pl.BlockSpec((L, gw), rev), pl.BlockSpec((L, LANES), rev0),
                  pl.BlockSpec((nh_pad, L), lambda g, c: (0, nc - 1 - c)), pl.BlockSpec((1, gw), lambda g, c: (0, g)),
                  pl.BlockSpec((None, None, SSM_STATE, gw), lambda g, c: (g, nc - 1 - c, 0, 0)), pl.BlockSpec((L, gw), rev)],
        out_specs=[pl.BlockSpec((L, gw), rev), pl.BlockSpec((L, SSM_STATE), rev), pl.BlockSpec((L, SSM_STATE), rev),
                   pl.BlockSpec((L, gw), rev), pl.BlockSpec((L, gw), rev), pl.BlockSpec((1, gw), lambda g, c: (0, g))],
        out_shape=[jax.ShapeDtypeStruct((s, di), ACT_DTYPE), jax.ShapeDtypeStruct(bm.shape, ACT_DTYPE), jax.ShapeDtypeStruct(cm.shape, ACT_DTYPE),
                   jax.ShapeDtypeStruct((s, di), F32), jax.ShapeDtypeStruct((s, di), F32), jax.ShapeDtypeStruct((1, di), F32)],
        scratch_shapes=[pltpu.VMEM((SSM_STATE, gw), F32)],
        compiler_params=_params(("parallel", "arbitrary")),
    )(xs, bm, cm, dtx, acx, acum, acum_t, d_x, states, dy)
    return outs


def _ssd_gate(y, z, w, gs, *, name):
    s, d = y.shape

    def fn(i, nrt, yv, zv, wv):
        zv = zv.astype(F32)
        u = yv.astype(F32) * zv * _sigmoid(zv)
        return (u * lax.rsqrt(_gmean(u * u, gs) + RMS_EPS) * wv,), ()

    (o,) = _rows(fn, name=name, s=s, tm=_tile(s, (256, 128)), ins=[("row", y, d, _c0), ("row", z, d, _c0), ("full", w.reshape(1, d))],
                 outs=[(d, d, _c0, ACT_DTYPE)])
    return o


def _ssd_gate_bwd(y, z, w, do, gs, *, name):
    s, d = y.shape

    def fn(i, nrt, yv, zv, wv, dov):
        yv, zv, dov = yv.astype(F32), zv.astype(F32), dov.astype(F32)
        sg = _sigmoid(zv)
        sl = zv * sg
        u = yv * sl
        r = lax.rsqrt(_gmean(u * u, gs) + RMS_EPS)
        uh = u * r
        g = dov * wv
        du = r * (g - uh * _gmean(g * uh, gs))
        return (du * sl, du * yv * sg * (1.0 + zv * (1.0 - sg))), (_sum_rows(dov * uh),)

    dy, dz, dw = _rows(fn, name=name, s=s, tm=_tile(s, (256, 128)),
                       ins=[("row", y, d, _c0), ("row", z, d, _c0), ("full", w.reshape(1, d)), ("row", do, d, _c0)],
                       outs=[(d, d, _c0, ACT_DTYPE), (d, d, _c0, ACT_DTYPE)], accs=[(1, d, d, _c0)])
    return dy, dz, dw.reshape(d)


def _pad_lanes(w):
    return jnp.pad(w, ((0, 0), (0, LANES - w.shape[1])))


def _nt_sum(pairs, name):
    acc = None
    for a, b in pairs:
        acc = _mm(a, b, tb=True, add=acc, name=name)
    return acc


def _conv_mixer_fwd(h, w_in, w_dw, tag):
    d = h.shape[1]
    ws = [w_in[:, k * d:(k + 1) * d] for k in range(3)]
    b, c, v = [_mm(h, w, out_dtype=ACT_DTYPE, name=f"{tag}_in") for w in ws]
    return _gconv_fwd(b, c, v, w_dw, name=f"{tag}_gate"), (h, ws, b, c, v, w_dw)


def _conv_mixer_bwd(cache, do, tag):
    h, ws, b, c, v, w_dw = cache
    db, dc, dv, dw_dw = _gconv_bwd(b, c, v, w_dw, do, name=f"{tag}_gate_bwd")
    dps = (db, dc, dv)
    dw_in = jnp.concatenate([_mm(h, dp, ta=True, name=f"{tag}_dw_in") for dp in dps], axis=1)
    dh = _nt_sum(list(zip(dps, ws)), f"{tag}_dh")
    return dh, {"w_in": dw_in, "w_dw": dw_dw}


def _fox_mixer_fwd(h, w_in, b_f, q_gain, k_gain, tag):
    d = h.shape[1]
    nh = d // HEAD
    ws = [w_in[:, k * d:(k + 1) * d] for k in range(3)] + [_pad_lanes(w_in[:, 3 * d:])]
    q, k, v = [_mm(h, w, out_dtype=ACT_DTYPE, name=f"{tag}_in") for w in ws[:3]]
    f = _mm(h, ws[3], name=f"{tag}_in_f")
    gq = jnp.tile(q_gain, nh).reshape(1, d)
    gk = jnp.tile(k_gain, nh).reshape(1, d)
    bf = _pad_lanes(b_f.reshape(1, nh))
    qs, kn, logf = _fox_prep(q, k, f, gq, gk, bf, name=f"{tag}_prep")
    cum = _cumsum_rows(logf, reverse=False, name=f"{tag}_cum")
    aug = _fox_aug(qs, kn, cum, name=f"{tag}_aug")
    q_aug, k_aug = aug[:2], aug[2:]
    o, lse = _fox_fwd_t(q_aug, k_aug, v, name=f"{tag}_attn")
    return o, (h, ws, q, k, v, f, gq, gk, bf, q_aug, k_aug, o, lse)


def _fox_mixer_bwd(cache, do, tag):
    h, ws, q, k, v, f, gq, gk, bf, q_aug, k_aug, o, lse = cache
    s, d = q.shape
    nh = d // HEAD
    dd = _pair_rows(_fox_dd(do, o, name=f"{tag}_attn_dd"), nh)
    dqs, dkn, dv, dcol, drow = _fox_bwd_t(q_aug, k_aug, v, lse, dd, do, name=f"{tag}_attn_bwd")
    dcum = _pad_lanes(drow[:, :2, :].reshape(nh, s).T - dcol[:, ::HEAD])
    dlogf = _cumsum_rows(dcum, reverse=True, name=f"{tag}_cum_bwd")
    dq, dk, df, dgq, dgk, dbf = _fox_prep_bwd(q, k, f, gq, gk, bf, dqs, dkn, dlogf, name=f"{tag}_prep_bwd")
    dps = (dq, dk, dv, df)
    dws = [_mm(h, dp, ta=True, name=f"{tag}_dw_in") for dp in dps]
    dw_in = jnp.concatenate(dws[:3] + [dws[3][:, :nh]], axis=1)
    dh = _nt_sum(list(zip(dps, ws)), f"{tag}_dh")
    return dh, {"w_in": dw_in, "b_f": dbf[0, :nh], "q_gain": dgq.reshape(nh, HEAD).sum(0), "k_gain": dgk.reshape(nh, HEAD).sum(0)}


def _ssd_mixer_fwd(h, w_in, conv_w, conv_b, dt_bias, a_log, d_skip, norm_w, tag):
    di = norm_w.shape[0]
    nh = di // HEAD
    gn = (conv_w.shape[1] - di) // 2
    cuts = [0, di, 2 * di, 2 * di + gn, 2 * di + 2 * gn]
    ws = [w_in[:, cuts[k]:cuts[k + 1]] for k in range(4)] + [_pad_lanes(w_in[:, cuts[4]:])]
    z, xr, br, cr = [_mm(h, w, out_dtype=ACT_DTYPE, name=f"{tag}_in") for w in ws[:4]]
    dtr = _mm(h, ws[4], name=f"{tag}_in_dt")
    ccuts = [0, di, di + gn, di + 2 * gn]
    cws = [conv_w[:, ccuts[k]:ccuts[k + 1]] for k in range(3)]
    cbs = [conv_b[ccuts[k]:ccuts[k + 1]] for k in range(3)]
    xs, bm, cm = [_sconv_fwd(r, w, b, name=f"{tag}_conv") for r, w, b in zip((xr, br, cr), cws, cbs)]
    dtb = _pad_lanes(dt_bias.reshape(1, nh))
    alg = _pad_lanes(a_log.reshape(1, nh))
    _dt, acum, dtx, acx = _ssd_prep(dtr, dtb, alg, nh, name=f"{tag}_prep")
    acum_t = acum[:, :nh].T
    d_x = jnp.repeat(d_skip, HEAD).reshape(1, di)
    y, states = _ssd_scan_fwd(xs, bm, cm, dtx, acx, acum, acum_t, d_x, name=f"{tag}_scan")
    gs = di // (gn // SSM_STATE)
    o = _ssd_gate(y, z, norm_w, gs, name=f"{tag}_gate")
    return o, (h, ws, z, (xr, br, cr), dtr, cws, cbs, xs, bm, cm, dtb, alg, dtx, acx, acum, acum_t, d_x, states, y, norm_w, gs, nh)


def _ssd_mixer_bwd(cache, do, tag):
    h, ws, z, raws, dtr, cws, cbs, xs, bm, cm, dtb, alg, dtx, acx, acum, acum_t, d_x, states, y, norm_w, gs, nh = cache
    dy, dz, dnorm = _ssd_gate_bwd(y, z, norm_w, do, gs, name=f"{tag}_gate_bwd")
    dxs, dbm, dcm, ddtx, dacx, dd_x = _ssd_scan_bwd(xs, bm, cm, dtx, acx, acum, acum_t, d_x, states, dy, name=f"{tag}_scan_bwd")
    ddtr, ddtb, dalg = _ssd_prep_bwd(dtr, dtb, alg, ddtx, dacx, nh, name=f"{tag}_prep_bwd")
    conv = [_sconv_bwd(r, w, b, da, name=f"{tag}_conv_bwd") for r, w, b, da in zip(raws, cws, cbs, (dxs, dbm, dcm))]
    dps = (dz, conv[0][0], conv[1][0], conv[2][0], ddtr)
    dws = [_mm(h, dp, ta=True, name=f"{tag}_dw_in") for dp in dps]
    dw_in = jnp.concatenate(dws[:4] + [dws[4][:, :nh]], axis=1)
    dh = _nt_sum(list(zip(dps, ws)), f"{tag}_dh")
    return dh, {"w_in": dw_in, "conv_w": jnp.concatenate([c[1] for c in conv], axis=1), "conv_b": jnp.concatenate([c[2] for c in conv]),
                "dt_bias": ddtb[0, :nh], "a_log": dalg[0, :nh], "d": dd_x.reshape(nh, HEAD).sum(1), "norm_w": dnorm}


def _local_step(x, tgt, fw):
    depth = fw["mix_norm"].shape[0]
    layers = []
    xc, y_prev = x, None
    for i in range(depth):
        kind, j = i % 3, i // 3
        tag = f"l{i}"
        xin, h = _resid_rms(xc, y_prev, fw["mix_norm"][i], name=f"{tag}_norm1")
        if kind == 0:
            o, mc = _conv_mixer_fwd(h, fw["conv_w_in"][j], fw["conv_w_dw"][j], tag + "_conv")
            w_out = fw["conv_w_out"][j]
        elif kind == 1:
            o, mc = _fox_mixer_fwd(h, fw["fox_w_in"][j], fw["fox_b_f"][j], fw["fox_q_gain"][j], fw["fox_k_gain"][j], tag + "_fox")
            w_out = fw["fox_w_out"][j]
        else:
            o, mc = _ssd_mixer_fwd(h, fw["ssd_w_in"][j], fw["ssd_conv_w"][j], fw["ssd_conv_b"][j], fw["ssd_dt_bias"][j], fw["ssd_a_log"][j],
                                   fw["ssd_d"][j], fw["ssd_norm_w"][j], tag + "_ssd")
            w_out = fw["ssd_w_out"][j]
        ym = _mm(o, w_out, name=f"{tag}_mix_out")
        x1, h2 = _resid_rms(xin, ym, fw["ffn_norm"][i], name=f"{tag}_norm2")
        f = fw["ffn_w_down"].shape[1]
        wg, wu = fw["ffn_w_gu"][i][:, :f], fw["ffn_w_gu"][i][:, f:]
        g = _mm(h2, wg, out_dtype=ACT_DTYPE, name=f"{tag}_ffn_g")
        u = _mm(h2, wu, out_dtype=ACT_DTYPE, name=f"{tag}_ffn_u")
        a = _swiglu_fwd(g, u, name=f"{tag}_swiglu")
        yf = _mm(a, fw["ffn_w_down"][i], name=f"{tag}_ffn_down")
        layers.append((xin, o, mc, w_out, x1, h2, wg, wu, g, u, a))
        xc, y_prev = x1, yf
    loss, dx, dxb = _loss_head(xc, y_prev, tgt, name="loss_head")

    names = ("conv", "fox", "ssd")
    grads = {k: [None] * v.shape[0] for k, v in fw.items()}
    for i in reversed(range(depth)):
        kind, j = i % 3, i // 3
        tag = f"l{i}"
        xin, o, mc, w_out, x1, h2, wg, wu, g, u, a = layers[i]
        grads["ffn_w_down"][i] = _mm(a, dxb, ta=True, name=f"{tag}_dw_down")
        da = _mm(dxb, fw["ffn_w_down"][i], tb=True, out_dtype=ACT_DTYPE, name=f"{tag}_da")
        dg, du = _swiglu_bwd(g, u, da, name=f"{tag}_swiglu_bwd")
        grads["ffn_w_gu"][i] = jnp.concatenate([_mm(h2, dg, ta=True, name=f"{tag}_dw_g"), _mm(h2, du, ta=True, name=f"{tag}_dw_u")], axis=1)
        dh2 = _nt_sum([(dg, wg), (du, wu)], f"{tag}_dh2")
        dx1, dx1b, grads["ffn_norm"][i] = _rms_bwd(x1, fw["ffn_norm"][i], dh2, dx, name=f"{tag}_norm2_bwd")
        grads[names[kind] + "_w_out"][j] = _mm(o, dx1b, ta=True, name=f"{tag}_dw_out")
        do = _mm(dx1b, w_out, tb=True, out_dtype=ACT_DTYPE, name=f"{tag}_do")
        if kind == 0:
            dh, mg = _conv_mixer_bwd(mc, do, tag + "_conv")
        elif kind == 1:
            dh, mg = _fox_mixer_bwd(mc, do, tag + "_fox")
        else:
            dh, mg = _ssd_mixer_bwd(mc, do, tag + "_ssd")
        for k, v in mg.items():
            grads[f"{names[kind]}_{k}"][j] = v
        dx, dxb, grads["mix_norm"][i] = _rms_bwd(xin, fw["mix_norm"][i], dh, dx1, name=f"{tag}_norm1_bwd")
    return loss, dx, {k: jnp.stack(v) for k, v in grads.items()}


ANY = pl.BlockSpec(memory_space=pl.ANY)
VMEM_SPEC = pl.BlockSpec(memory_space=pltpu.VMEM)


def _place():
    return lax.axis_index("x"), lax.axis_index("y"), lax.axis_index("c")


def _remote(src, dst, send_sems, recv_sems, k, to):
    return pltpu.make_async_remote_copy(src_ref=src, dst_ref=dst, send_sem=send_sems.at[k], recv_sem=recv_sems.at[k],
                                        device_id=to, device_id_type=MESH)


def _gather_big(wp, *, name):
    r, w = wp.shape
    rh = r // 2

    def body(w_ref, o_ref, send_sems, recv_sems):
        x, y, c = _place()
        me, sibling, m = (x, y, c), (x, y, 1 - c), 2 * x + y
        chips = [(1 - x, y), (x, 1 - y), (1 - x, 1 - y)]
        mine = w_ref.at[pl.ds(pl.multiple_of(c * rh, 16), rh)]
        first = [_remote(mine, o_ref.at[c, m], send_sems, recv_sems, j, (px, py, c)) for j, (px, py) in enumerate(chips)]
        for cp in first:
            cp.start()
        passed = []
        for j, (px, py) in enumerate(chips):
            blk = o_ref.at[c, 2 * px + py]
            _remote(blk, blk, send_sems, recv_sems, j, me).wait_recv()
            fwd = _remote(blk, blk, send_sems, recv_sems, 3 + j, sibling)
            fwd.start()
            passed.append(fwd)
        for j, (px, py) in enumerate(chips):
            blk = o_ref.at[1 - c, 2 * px + py]
            _remote(blk, blk, send_sems, recv_sems, 3 + j, me).wait_recv()
        for cp in first + passed:
            cp.wait_send()

    return pl.pallas_call(
        body, name=name, in_specs=[ANY], out_specs=ANY, out_shape=jax.ShapeDtypeStruct((2, 4, rh, w), wp.dtype),
        scratch_shapes=[pltpu.SemaphoreType.DMA((6,)), pltpu.SemaphoreType.DMA((6,))],
    )(wp)


def _gather_small(v, *, name):
    r, w = v.shape

    def body(v_ref, o_ref, send_sems, recv_sems):
        x, y, c = _place()
        m = 2 * x + y
        chips = [(1 - x, y), (x, 1 - y), (1 - x, 1 - y)]
        o_ref[m] = v_ref[...]
        sends = [_remote(v_ref, o_ref.at[m], send_sems, recv_sems, j, (px, py, c)) for j, (px, py) in enumerate(chips)]
        for cp in sends:
            cp.start()
        for j, (px, py) in enumerate(chips):
            blk = o_ref.at[2 * px + py]
            _remote(blk, blk, send_sems, recv_sems, j, (x, y, c)).wait_recv()
        for cp in sends:
            cp.wait_send()

    return pl.pallas_call(
        body, name=name, in_specs=[VMEM_SPEC], out_specs=VMEM_SPEC, out_shape=jax.ShapeDtypeStruct((4, r, w), v.dtype),
        scratch_shapes=[pltpu.SemaphoreType.DMA((3,)), pltpu.SemaphoreType.DMA((3,))],
    )(v)


def _swap_sibling(a, *, name):
    def body(a_ref, o_ref, send_sem, recv_sem):
        x, y, c = _place()
        cp = pltpu.make_async_remote_copy(src_ref=a_ref.at[1 - c], dst_ref=o_ref, send_sem=send_sem, recv_sem=recv_sem,
                                          device_id=(x, y, 1 - c), device_id_type=MESH)
        cp.start()
        cp.wait()

    return pl.pallas_call(
        body, name=name, in_specs=[ANY], out_specs=ANY, out_shape=jax.ShapeDtypeStruct(a.shape[1:], a.dtype),
        scratch_shapes=[pltpu.SemaphoreType.DMA, pltpu.SemaphoreType.DMA],
    )(a)


def _join_halves(a, *, name):
    r, w = a.shape[0] // 2, a.shape[1]

    def body(a_ref, o_ref, send_sem, recv_sem):
        x, y, c = _place()
        mine = o_ref.at[pl.ds(pl.multiple_of(c * r, SUBLANES), r)]
        cp = pltpu.make_async_remote_copy(src_ref=mine, dst_ref=mine, send_sem=send_sem, recv_sem=recv_sem,
                                          device_id=(x, y, 1 - c), device_id_type=MESH)
        cp.start()
        cp.wait_send()
        other = o_ref.at[pl.ds(pl.multiple_of((1 - c) * r, SUBLANES), r)]
        pltpu.make_async_remote_copy(src_ref=other, dst_ref=other, send_sem=send_sem, recv_sem=recv_sem,
                                     device_id=(x, y, c), device_id_type=MESH).wait_recv()

    return pl.pallas_call(
        body, name=name, in_specs=[ANY], out_specs=ANY, out_shape=jax.ShapeDtypeStruct(a.shape, a.dtype),
        input_output_aliases={0: 0}, scratch_shapes=[pltpu.SemaphoreType.DMA, pltpu.SemaphoreType.DMA],
    )(a)


def _scatter_chips(a, *, name):
    def body(a_ref, o_ref, send_sems, recv_sems):
        x, y, c = _place()
        m = 2 * x + y
        chips = [(1 - x, y), (x, 1 - y), (1 - x, 1 - y)]
        sends = [_remote(a_ref.at[2 * px + py], o_ref.at[m], send_sems, recv_sems, j, (px, py, c)) for j, (px, py) in enumerate(chips)]
        for cp in sends:
            cp.start()
        for j, (px, py) in enumerate(chips):
            blk = o_ref.at[2 * px + py]
            _remote(blk, blk, send_sems, recv_sems, j, (x, y, c)).wait_recv()
        for cp in sends:
            cp.wait_send()

    return pl.pallas_call(
        body, name=name, in_specs=[ANY], out_specs=ANY, out_shape=jax.ShapeDtypeStruct(a.shape, a.dtype),
        scratch_shapes=[pltpu.SemaphoreType.DMA((3,)), pltpu.SemaphoreType.DMA((3,))],
    )(a)


def _allreduce_small(v, *, name):
    r, w = v.shape

    def body(v_ref, o_ref, slots, send_sems, recv_sems):
        x, y, c = _place()
        me = 4 * x + 2 * y + c
        slots[me] = v_ref[...]
        peers = [((1 - x) if k & 4 else x, (1 - y) if k & 2 else y, (1 - c) if k & 1 else c) for k in range(1, 8)]
        sends = [_remote(v_ref, slots.at[me], send_sems, recv_sems, k, p) for k, p in enumerate(peers)]
        for cp in sends:
            cp.start()
        for k, (px, py, pc) in enumerate(peers):
            blk = slots.at[4 * px + 2 * py + pc]
            _remote(blk, blk, send_sems, recv_sems, k, (x, y, c)).wait_recv()
        for cp in sends:
            cp.wait_send()
        acc = slots[0]
        for k in range(1, 8):
            acc = acc + slots[k]
        o_ref[...] = acc

    return pl.pallas_call(
        body, name=name, in_specs=[VMEM_SPEC], out_specs=VMEM_SPEC, out_shape=jax.ShapeDtypeStruct(v.shape, v.dtype),
        scratch_shapes=[pltpu.VMEM((8, r, w), F32), pltpu.SemaphoreType.DMA((7,)), pltpu.SemaphoreType.DMA((7,))],
    )(v)


def _add2(a, b, *, name):
    s, w = a.shape

    def fn(i, nrt, av, bv):
        return (av.astype(F32) + bv.astype(F32),), ()

    (o,) = _rows(fn, name=name, s=s, tm=_tile(s, (512, 256, 128, 64, 32, 16)), ins=[("row", a, w, _c0), ("row", b, w, _c0)],
                 outs=[(w, w, _c0, a.dtype)])
    return o


def _sum_chips(recv, own, place, *, name):
    n, r, w = recv.shape
    tm = _tile(r, (512, 256, 128, 64, 32, 16))
    nb = r // tm

    def body(place_ref, *refs):
        own_ref, o_ref = refs[n], refs[n + 1]
        acc = None
        for k in range(n):
            term = jnp.where(place_ref[0] == k, own_ref[...], refs[k][...]).astype(F32)
            acc = term if acc is None else acc + term
        o_ref[...] = acc

    recv_specs = [pl.BlockSpec((None, tm, w), lambda i, p, k=k: (jnp.where(p[0] == k, (k + 1) % n, k), i, 0)) for k in range(n)]
    return pl.pallas_call(
        body, name=name,
        grid_spec=pltpu.PrefetchScalarGridSpec(
            num_scalar_prefetch=1, grid=(nb,),
            in_specs=recv_specs + [pl.BlockSpec((None, tm, w), lambda i, p: (p[0], i, 0))],
            out_specs=pl.BlockSpec((tm, w), lambda i, p: (p[1] * nb + i, 0))),
        out_shape=jax.ShapeDtypeStruct((2 * r, w), F32), compiler_params=_params(("parallel",)),
    )(place, *([recv] * n), own)


def _adamw(w, g, m, v, *, name):
    shape = w.shape
    cols = shape[-1]
    rows = math.prod(shape[:-1])
    tm = _tile(rows, (256, 128, 64, 32, 16, 8))
    c1 = 1.0 - ADAM_B1 ** ADAM_STEP
    c2 = 1.0 - ADAM_B2 ** ADAM_STEP

    def fn(i, nrt, wv, gv, mv, vv):
        mn = ADAM_B1 * mv + (1.0 - ADAM_B1) * gv
        vn = ADAM_B2 * vv + (1.0 - ADAM_B2) * (gv * gv)
        delta = -ADAM_LR * ((mn / c1) / (jnp.sqrt(vn / c2) + ADAM_EPS) + ADAM_WD * wv)
        return (delta, mn, vn), ()

    outs = _rows(fn, name=name, s=rows, tm=tm, ins=[("row", t.reshape(rows, cols), cols, _c0) for t in (w, g, m, v)],
                 outs=[(cols, cols, _c0, F32)] * 3)
    return [o.reshape(shape) for o in outs]


WEIGHTS = ["mix_norm", "ffn_norm", "ffn_w_gu", "ffn_w_down", "conv_w_in", "conv_w_dw", "conv_w_out", "fox_w_in", "fox_b_f", "fox_q_gain",
           "fox_k_gain", "fox_w_out", "ssd_w_in", "ssd_conv_w", "ssd_conv_b", "ssd_dt_bias", "ssd_a_log", "ssd_d", "ssd_norm_w", "ssd_w_out"]
SHARD_AXIS = {"ffn_w_gu": 2, "ffn_w_down": 1, "conv_w_in": 2, "conv_w_dw": 2, "conv_w_out": 1, "fox_w_in": 2, "fox_w_out": 1, "ssd_w_in": 2,
              "ssd_conv_w": 2, "ssd_conv_b": 1, "ssd_norm_w": 1, "ssd_w_out": 1}
BIG = ["ffn_w_gu", "ffn_w_down", "conv_w_in", "conv_w_out", "fox_w_in", "fox_w_out", "ssd_w_in", "ssd_w_out"]
SMALL_SHARDED = ["conv_w_dw", "ssd_conv_w", "ssd_conv_b", "ssd_norm_w"]
ROW_PAD = 1024
N_CHIPS = 4


def _pack_rows(parts, width, pad_to):
    mats = [p.reshape(-1, width) for p in parts]
    offs, n = [], 0
    for mt in mats:
        offs.append(n)
        n += mt.shape[0]
    total = -(-n // pad_to) * pad_to
    if total > n:
        mats.append(jnp.zeros((total - n, width), mats[0].dtype))
    return jnp.concatenate(mats, axis=0), offs


def _pack_flat(parts, pad_to):
    flat = [p.reshape(-1) for p in parts]
    offs, n = [], 0
    for f in flat:
        offs.append(n)
        n += f.shape[0]
    total = -(-n // pad_to) * pad_to
    if total > n:
        flat.append(jnp.zeros((total - n,), flat[0].dtype))
    return jnp.concatenate(flat).reshape(-1, LANES), offs


def kernel(x, mix_norm, ffn_norm, ffn_w_gu, ffn_w_down, conv_w_in, conv_w_dw, conv_w_out, fox_w_in, fox_b_f, fox_q_gain, fox_k_gain, fox_w_out, ssd_w_in, ssd_conv_w, ssd_conv_b, ssd_dt_bias, ssd_a_log, ssd_d, ssd_norm_w, ssd_w_out, loss_target, m_mix_norm, m_ffn_norm, m_ffn_w_gu, m_ffn_w_down, m_conv_w_in, m_conv_w_dw, m_conv_w_out, m_fox_w_in, m_fox_b_f, m_fox_q_gain, m_fox_k_gain, m_fox_w_out, m_ssd_w_in, m_ssd_conv_w, m_ssd_conv_b, m_ssd_dt_bias, m_ssd_a_log, m_ssd_d, m_ssd_norm_w, m_ssd_w_out, v_mix_norm, v_ffn_norm, v_ffn_w_gu, v_ffn_w_down, v_conv_w_in, v_conv_w_dw, v_conv_w_out, v_fox_w_in, v_fox_b_f, v_fox_q_gain, v_fox_k_gain, v_fox_w_out, v_ssd_w_in, v_ssd_conv_w, v_ssd_conv_b, v_ssd_dt_bias, v_ssd_a_log, v_ssd_d, v_ssd_norm_w, v_ssd_w_out):
    w = dict(zip(WEIGHTS, (mix_norm, ffn_norm, ffn_w_gu, ffn_w_down, conv_w_in, conv_w_dw, conv_w_out, fox_w_in, fox_b_f, fox_q_gain, fox_k_gain,
                           fox_w_out, ssd_w_in, ssd_conv_w, ssd_conv_b, ssd_dt_bias, ssd_a_log, ssd_d, ssd_norm_w, ssd_w_out)))
    m1 = dict(zip(WEIGHTS, (m_mix_norm, m_ffn_norm, m_ffn_w_gu, m_ffn_w_down, m_conv_w_in, m_conv_w_dw, m_conv_w_out, m_fox_w_in, m_fox_b_f,
                            m_fox_q_gain, m_fox_k_gain, m_fox_w_out, m_ssd_w_in, m_ssd_conv_w, m_ssd_conv_b, m_ssd_dt_bias, m_ssd_a_log, m_ssd_d,
                            m_ssd_norm_w, m_ssd_w_out)))
    m2 = dict(zip(WEIGHTS, (v_mix_norm, v_ffn_norm, v_ffn_w_gu, v_ffn_w_down, v_conv_w_in, v_conv_w_dw, v_conv_w_out, v_fox_w_in, v_fox_b_f,
                            v_fox_q_gain, v_fox_k_gain, v_fox_w_out, v_ssd_w_in, v_ssd_conv_w, v_ssd_conv_b, v_ssd_dt_bias, v_ssd_a_log, v_ssd_d,
                            v_ssd_norm_w, v_ssd_w_out)))
    d = x.shape[-1]
    cx, cy, cc = _place()
    chip = 2 * cx + cy

    wp, offs = _pack_rows([w[n].astype(WIRE_DTYPE) for n in BIG], d, ROW_PAD)
    gath = _gather_big(wp, name="gather_weights")
    gath = jnp.swapaxes(gath, 0, 1).reshape(N_CHIPS, wp.shape[0], d)
    gath = lax.dynamic_update_slice(gath, wp[None], (chip, 0, 0))
    sp, soffs = _pack_flat([w[n] for n in SMALL_SHARDED], SUBLANES * LANES)
    sgath = _gather_small(sp, name="gather_small").reshape(N_CHIPS, -1)
    full = {n: w[n] for n in WEIGHTS if n not in SHARD_AXIS}
    for n, off in zip(BIG, offs):
        rows = w[n].size // d
        full[n] = jnp.concatenate([gath[j, off:off + rows].reshape(w[n].shape) for j in range(N_CHIPS)], axis=SHARD_AXIS[n])
    for n, off in zip(SMALL_SHARDED, soffs):
        full[n] = jnp.concatenate([sgath[j, off:off + w[n].size].reshape(w[n].shape) for j in range(N_CHIPS)], axis=SHARD_AXIS[n])

    loss, gx, grads = _local_step(x[0], loss_target[0], full)
    loss = lax.psum(loss, ("x", "y", "c"))

    per_chip = []
    for j in range(N_CHIPS):
        parts = [jnp.split(grads[n], N_CHIPS, axis=SHARD_AXIS[n])[j].astype(WIRE_DTYPE) for n in BIG]
        per_chip.append(_pack_rows(parts, d, ROW_PAD)[0])
    r = per_chip[0].shape[0]
    rh = r // 2
    gp = jnp.stack([jnp.stack([pc[h * rh:(h + 1) * rh] for pc in per_chip]) for h in range(2)])
    from_sibling = _swap_sibling(gp, name="reduce_halves")
    mine = lax.dynamic_index_in_dim(gp, cc, 0, keepdims=False)
    chip_sum = _add2(mine.reshape(N_CHIPS * rh, d), from_sibling.reshape(N_CHIPS * rh, d), name="reduce_add_sibling").reshape(N_CHIPS, rh, d)
    by_chip = _scatter_chips(chip_sum, name="reduce_chips")
    place = jnp.stack([chip, cc]).astype(jnp.int32)
    red = _sum_chips(by_chip, chip_sum, place, name="reduce_sum_chips")
    red = _join_halves(red, name="reduce_share")

    small_names = [n for n in WEIGHTS if n not in BIG]
    sm, smoffs = _pack_flat([grads[n] for n in small_names], SUBLANES * LANES)
    sred = _allreduce_small(sm, name="allreduce_small").reshape(-1)

    g = {}
    for n, off in zip(BIG, offs):
        g[n] = red[off:off + w[n].size // d].reshape(w[n].shape)
    for n, off in zip(small_names, smoffs):
        fullg = sred[off:off + grads[n].size].reshape(grads[n].shape)
        if n in SHARD_AXIS:
            ax = SHARD_AXIS[n]
            fullg = lax.dynamic_slice_in_dim(fullg, chip * w[n].shape[ax], w[n].shape[ax], axis=ax)
        g[n] = fullg

    deltas, new_m, new_v = [], [], []
    for n in WEIGHTS:
        dl, mn, vn = _adamw(w[n], g[n], m1[n], m2[n], name=f"adamw_{n}")
        deltas.append(dl)
        new_m.append(mn)
        new_v.append(vn)
    return (loss, gx[None], *[g[n] for n in WEIGHTS], *deltas, *new_m, *new_v)
```

```python
import functools
import math

import jax
import jax.numpy as jnp
from jax import lax
from jax.experimental import pallas as pl
from jax.experimental.pallas import tpu as pltpu

F32 = jnp.float32
MM_DTYPE = jnp.bfloat16
ACT_DTYPE = jnp.bfloat16
WIRE_DTYPE = jnp.bfloat16

RMS_EPS = 1e-6
HEAD = 64
SSM_STATE = 128
SSM_CHUNK = 128
LANES = 128
SUBLANES = 8
VMEM_LIMIT = 48 * 1024 * 1024

ADAM_LR, ADAM_B1, ADAM_B2, ADAM_EPS, ADAM_WD, ADAM_STEP = 0.001, 0.9, 0.999, 1e-08, 0.01, 10

HI = lax.Precision.HIGHEST
MESH = pl.DeviceIdType.MESH


def _tile(dim, prefs):
    for p in prefs:
        if dim % p == 0:
            return p
    return dim


def _params(sem):
    return pltpu.CompilerParams(dimension_semantics=sem, vmem_limit_bytes=VMEM_LIMIT)


def _sigmoid(x):
    return 1.0 / (1.0 + jnp.exp(-x))


def _softplus(x):
    return jnp.maximum(x, 0.0) + jnp.log(1.0 + jnp.exp(-jnp.abs(x)))


TILES = (1024, 1408, 768, 512, 256, 128)


def _mm(a, b, *, ta=False, tb=False, add=None, out_dtype=F32, name, b_layer=None, out_shard=None, inter=None):
    ka, m = (a.shape[0], a.shape[1]) if ta else (a.shape[1], a.shape[0])
    if b_layer is None:
        kb, n = (b.shape[1], b.shape[0]) if tb else (b.shape[0], b.shape[1])
        ns = None
    else:
        ns = b.shape[3]
        kb, n = (b.shape[0] * ns, b.shape[2]) if tb else (b.shape[2], b.shape[0] * ns)
    assert ka == kb, (a.shape, b.shape, ta, tb)
    k = ka
    tm = _tile(m, (1408, 1024, 512, 256, 128) if ta else (512, 256, 128))
    tn = _tile(n, TILES)
    tk = _tile(k, TILES)
    if inter:
        segs, bw = inter
        if tb:
            tk = bw
        else:
            tn = bw
        tps = ((k if tb else n) // bw) // segs
        col = lambda q: ((q % segs) * tps + q // segs) * bw
    else:
        col = lambda q: q * (tk if tb else tn)
    nk = k // tk
    a_spec = pl.BlockSpec((tk, tm), lambda i, j, q: (q, i)) if ta else pl.BlockSpec((tm, tk), lambda i, j, q: (i, q))
    if b_layer is None:
        b_spec = pl.BlockSpec((tn, tk), lambda i, j, q: (j, q)) if tb else pl.BlockSpec((tk, tn), lambda i, j, q: (q, j))
    elif tb:
        b_spec = pl.BlockSpec((None, None, tn, tk), lambda i, j, q: (col(q) // ns, b_layer, j, (col(q) % ns) // tk))
    else:
        b_spec = pl.BlockSpec((None, None, tk, tn), lambda i, j, q: (col(j) // ns, b_layer, q, (col(j) % ns) // tn))
    if out_shard:
        assert ta and add is None
        o_spec = pl.BlockSpec((None, tm, tn), lambda i, j, q: (col(j) // out_shard, i, (col(j) % out_shard) // tn))
        o_shape = jax.ShapeDtypeStruct((N_CHIPS, m, out_shard), out_dtype)
    else:
        o_spec = pl.BlockSpec((tm, tn), lambda i, j, q: (i, j))
        o_shape = jax.ShapeDtypeStruct((m, n), out_dtype)
    dims = (((0 if ta else 1,), (1 if tb else 0,)), ((), ()))
    has_add = add is not None

    def body(*refs):
        a_ref, b_ref = refs[0], refs[1]
        o_ref = refs[2 + has_add]
        p = lax.dot_general(a_ref[...].astype(MM_DTYPE), b_ref[...].astype(MM_DTYPE), dims, preferred_element_type=F32)

        def finish(acc):
            if has_add:
                acc = acc + refs[2][...].astype(F32)
            o_ref[...] = acc.astype(out_dtype)

        if nk == 1:
            finish(p)
        else:
            acc_ref = refs[3 + has_add]
            q = pl.program_id(2)

            @pl.when(q == 0)
            def _():
                acc_ref[...] = p

            @pl.when(q > 0)
            def _():
                acc_ref[...] += p

            @pl.when(q == nk - 1)
            def _():
                finish(acc_ref[...])

    args = [a, b] + ([add] if has_add else [])
    in_specs = [a_spec, b_spec] + ([o_spec] if has_add else [])
    return pl.pallas_call(
        body, name=name, grid=(m // tm, n // tn, nk), in_specs=in_specs, out_specs=o_spec, out_shape=o_shape,
        scratch_shapes=[pltpu.VMEM((tm, tn), F32)] if nk > 1 else [],
        compiler_params=_params(("parallel", "parallel", "arbitrary")),
    )(*args)


def _rows(fn, *, name, s, tm, ncol=1, ins, outs, accs=()):
    nrt = s // tm
    hb = tm // SUBLANES
    in_specs, args = [], []
    for spec in ins:
        kind, arr = spec[0], spec[1]
        if kind == "full":
            in_specs.append(pl.BlockSpec(arr.shape, lambda j, i: (0, 0)))
        elif kind == "col":
            _, _, bw, cmap = spec
            in_specs.append(pl.BlockSpec((arr.shape[0], bw), lambda j, i, cmap=cmap: (0, cmap(j))))
        elif kind == "row":
            _, _, bw, cmap = spec
            in_specs.append(pl.BlockSpec((tm, bw), lambda j, i, cmap=cmap: (i, cmap(j))))
        elif kind == "prev":
            _, _, bw, cmap = spec
            in_specs.append(pl.BlockSpec((SUBLANES, bw), lambda j, i, cmap=cmap: (jnp.maximum(i * hb - 1, 0), cmap(j))))
        elif kind == "next":
            _, _, bw, cmap = spec
            in_specs.append(pl.BlockSpec((SUBLANES, bw), lambda j, i, cmap=cmap: (jnp.minimum((i + 1) * hb, s // SUBLANES - 1), cmap(j))))
        else:
            raise ValueError(kind)
        args.append(arr)
    out_specs, out_shape = [], []
    for w, bw, cmap, dt in outs:
        out_specs.append(pl.BlockSpec((tm, bw), lambda j, i, cmap=cmap: (i, cmap(j))))
        out_shape.append(jax.ShapeDtypeStruct((s, w), dt))
    for r, w, bw, cmap in accs:
        out_specs.append(pl.BlockSpec((r, bw), lambda j, i, cmap=cmap: (0, cmap(j))))
        out_shape.append(jax.ShapeDtypeStruct((r, w), F32))
    n_in, n_out, n_acc = len(ins), len(outs), len(accs)

    def body(*refs):
        i = pl.program_id(1)
        vals = [r[...] for r in refs[:n_in]]
        o_vals, a_vals = fn(i, nrt, *vals)
        assert len(o_vals) == n_out and len(a_vals) == n_acc
        for r, v in zip(refs[n_in:n_in + n_out], o_vals):
            r[...] = v.astype(r.dtype)
        for r, v in zip(refs[n_in + n_out:], a_vals):
            @pl.when(i == 0)
            def _(r=r, v=v):
                r[...] = v.astype(F32)

            @pl.when(i > 0)
            def _(r=r, v=v):
                r[...] += v.astype(F32)

    res = pl.pallas_call(
        body, name=name, grid=(ncol, nrt), in_specs=in_specs, out_specs=out_specs, out_shape=out_shape,
        compiler_params=_params(("parallel", "arbitrary" if accs else "parallel")),
    )(*args)
    return res


def _c0(j):
    return 0


def _cj(j):
    return j


def _gmean(v, gs):
    w = v.shape[-1]
    tile = max(gs, LANES)
    r = lax.broadcasted_iota(jnp.int32, (tile, tile), 0) // gs
    c = lax.broadcasted_iota(jnp.int32, (tile, tile), 1) // gs
    g = jnp.where(r == c, 1.0 / gs, 0.0).astype(F32)
    parts = [jnp.dot(v[:, t * tile:(t + 1) * tile], g, precision=HI, preferred_element_type=F32) for t in range(w // tile)]
    return parts[0] if len(parts) == 1 else jnp.concatenate(parts, axis=1)


def _sum_rows(v):
    return jnp.sum(v, axis=0, keepdims=True)


def _resid_rms(x, y, w, *, name):
    s, d = x.shape
    has_y = y is not None

    def fn(i, nrt, *v):
        xv = v[0] + (v[1] if has_y else 0.0)
        wv = v[-1]
        r = lax.rsqrt(jnp.mean(xv * xv, axis=-1, keepdims=True) + RMS_EPS)
        return (xv, xv * r * wv), ()

    ins = [("row", x, d, _c0)] + ([("row", y, d, _c0)] if has_y else []) + [("full", w.reshape(1, d))]
    xn, h = _rows(fn, name=name, s=s, tm=_tile(s, (512, 256, 128)), ins=ins, outs=[(d, d, _c0, F32), (d, d, _c0, ACT_DTYPE)])
    return xn, h


def _rms_bwd(x, w, dh, dx_in, *, name):
    s, d = x.shape

    def fn(i, nrt, xv, wv, dhv, dxi):
        r = lax.rsqrt(jnp.mean(xv * xv, axis=-1, keepdims=True) + RMS_EPS)
        xh = xv * r
        g = dhv * wv
        dx = dxi + r * (g - xh * jnp.mean(g * xh, axis=-1, keepdims=True))
        return (dx, dx), (_sum_rows(dhv * xh),)

    dx, dxb, dw = _rows(fn, name=name, s=s, tm=_tile(s, (512, 256, 128)),
                        ins=[("row", x, d, _c0), ("full", w.reshape(1, d)), ("row", dh, d, _c0), ("row", dx_in, d, _c0)],
                        outs=[(d, d, _c0, F32), (d, d, _c0, MM_DTYPE)], accs=[(1, d, d, _c0)])
    return dx, dxb, dw.reshape(d)


def _swiglu_fwd(p, bw, *, name):
    s, f = p.shape[0], p.shape[1] // 2

    def fn(i, nrt, pv):
        gv, uv = pv[:, :bw].astype(F32), pv[:, bw:].astype(F32)
        return (gv * _sigmoid(gv) * uv,), ()

    (a,) = _rows(fn, name=name, s=s, tm=_tile(s, (512, 256, 128)), ncol=f // bw,
                 ins=[("row", p, 2 * bw, _cj)], outs=[(f, bw, _cj, ACT_DTYPE)])
    return a


def _swiglu_bwd(p, da, bw, *, name):
    s, f = da.shape

    def fn(i, nrt, pv, dav):
        gv, uv, dav = pv[:, :bw].astype(F32), pv[:, bw:].astype(F32), dav.astype(F32)
        sg = _sigmoid(gv)
        dg = dav * uv * sg * (1.0 + gv * (1.0 - sg))
        du = dav * gv * sg
        return (jnp.concatenate([dg, du], axis=1),), ()

    (dp,) = _rows(fn, name=name, s=s, tm=_tile(s, (512, 256, 128)), ncol=f // bw,
                  ins=[("row", p, 2 * bw, _cj), ("row", da, bw, _cj)], outs=[(2 * f, 2 * bw, _cj, ACT_DTYPE)])
    return dp


def _loss_head(x, y, tgt, *, name):
    s, d = x.shape

    def fn(i, nrt, xv, yv, tv):
        diff = xv + yv - tv
        part = 0.5 * jnp.sum(diff * diff) / d
        return (diff / d, diff / d), (jnp.full((1, LANES), part, F32),)

    dy, dyb, loss = _rows(fn, name=name, s=s, tm=_tile(s, (512, 256, 128)),
                          ins=[("row", x, d, _c0), ("row", y, d, _c0), ("row", tgt, d, _c0)],
                          outs=[(d, d, _c0, F32), (d, d, _c0, MM_DTYPE)], accs=[(1, LANES, LANES, _c0)])
    return loss[0, 0], dy, dyb


def _shift_down(ext, j, tm):
    src = pltpu.roll(ext, j, 0) if j else ext
    return src[SUBLANES:SUBLANES + tm]


def _shift_up(ext, j, tm):
    return ext[:tm] if j == 0 else pltpu.roll(ext, ext.shape[0] - j, 0)[:tm]


def _gconv_fwd(p, w, bw, *, name):
    s, d = p.shape[0], p.shape[1] // 3
    kw = w.shape[0]
    tm = _tile(s, (512, 256, 128))

    def fn(i, nrt, pv, pp, wv):
        pv, pp = pv.astype(F32), pp.astype(F32)
        cv = pv[:, bw:2 * bw] * pv[:, 2 * bw:]
        pcv = jnp.where(i == 0, 0.0, pp[:, bw:2 * bw] * pp[:, 2 * bw:])
        ext = jnp.concatenate([pcv, cv], axis=0)
        u = sum(wv[k:k + 1, :] * _shift_down(ext, kw - 1 - k, tm) for k in range(kw))
        return (pv[:, :bw] * u,), ()

    (o,) = _rows(fn, name=name, s=s, tm=tm, ncol=d // bw, ins=[("row", p, 3 * bw, _cj), ("prev", p, 3 * bw, _cj), ("col", w, bw, _cj)],
                 outs=[(d, bw, _cj, ACT_DTYPE)])
    return o


def _gconv_bwd(p, w, do, bw, *, name):
    s, d = do.shape
    kw = w.shape[0]
    tm = _tile(s, (512, 256, 128))

    def fn(i, nrt, pv, pp, pn, dov, ndo, wv):
        pv, pp, dov = pv.astype(F32), pp.astype(F32), dov.astype(F32)
        bv, cv_, vv = pv[:, :bw], pv[:, bw:2 * bw], pv[:, 2 * bw:]
        cv = cv_ * vv
        pcv = jnp.where(i == 0, 0.0, pp[:, bw:2 * bw] * pp[:, 2 * bw:])
        ext = jnp.concatenate([pcv, cv], axis=0)
        shifted = [_shift_down(ext, kw - 1 - k, tm) for k in range(kw)]
        u = sum(wv[k:k + 1, :] * shifted[k] for k in range(kw))
        db = dov * u
        du = dov * bv
        ndu = jnp.where(i == nrt - 1, 0.0, ndo.astype(F32) * pn[:, :bw].astype(F32))
        ext2 = jnp.concatenate([du, ndu], axis=0)
        dcv = sum(wv[k:k + 1, :] * _shift_up(ext2, kw - 1 - k, tm) for k in range(kw))
        dw = jnp.concatenate([_sum_rows(du * shifted[k]) for k in range(kw)], axis=0)
        return (jnp.concatenate([db, dcv * vv, dcv * cv_], axis=1),), (dw,)

    dp, dw = _rows(fn, name=name, s=s, tm=tm, ncol=d // bw,
                   ins=[("row", p, 3 * bw, _cj), ("prev", p, 3 * bw, _cj), ("next", p, 3 * bw, _cj), ("row", do, bw, _cj),
                        ("next", do, bw, _cj), ("col", w, bw, _cj)],
                   outs=[(3 * d, 3 * bw, _cj, ACT_DTYPE)], accs=[(kw, d, bw, _cj)])
    return dp, dw


def _sconv_fwd(x, w, bias, *, name):
    s, d = x.shape
    kw = w.shape[0]
    bw = _tile(d, (512, 256, 128))
    tm = _tile(s, (512, 256, 128))

    def fn(i, nrt, xv, px, wv, bsv):
        xv = xv.astype(F32)
        ext = jnp.concatenate([jnp.where(i == 0, 0.0, px.astype(F32)), xv], axis=0)
        pre = sum(wv[k:k + 1, :] * _shift_down(ext, kw - 1 - k, tm) for k in range(kw)) + bsv
        return (pre * _sigmoid(pre),), ()

    (o,) = _rows(fn, name=name, s=s, tm=tm, ncol=d // bw,
                 ins=[("row", x, bw, _cj), ("prev", x, bw, _cj), ("col", w, bw, _cj), ("col", bias.reshape(1, d), bw, _cj)],
                 outs=[(d, bw, _cj, ACT_DTYPE)])
    return o


def _sconv_bwd(x, w, bias, dact, *, name):
    s, d = x.shape
    kw = w.shape[0]
    bw = _tile(d, (512, 256, 128))
    tm = _tile(s, (512, 256, 128))

    def fn(i, nrt, xv, px, nx, dav, nda, wv, bsv):
        xv = xv.astype(F32)
        ext = jnp.concatenate([jnp.where(i == 0, 0.0, px.astype(F32)), xv, nx.astype(F32)], axis=0)
        rows_e = tm + SUBLANES
        pre_e = sum(wv[k:k + 1, :] * _shift_down(ext, kw - 1 - k, rows_e) for k in range(kw)) + bsv
        da_e = jnp.concatenate([dav.astype(F32), jnp.where(i == nrt - 1, 0.0, nda.astype(F32))], axis=0)
        sg = _sigmoid(pre_e)
        dpre_e = da_e * sg * (1.0 + pre_e * (1.0 - sg))
        dx = sum(wv[k:k + 1, :] * _shift_up(dpre_e, kw - 1 - k, tm) for k in range(kw))
        dpre = dpre_e[:tm]
        dw = jnp.concatenate([_sum_rows(dpre * _shift_down(ext, kw - 1 - k, tm)) for k in range(kw)], axis=0)
        return (dx,), (dw, _sum_rows(dpre))

    dx, dw, db = _rows(fn, name=name, s=s, tm=tm, ncol=d // bw,
                       ins=[("row", x, bw, _cj), ("prev", x, bw, _cj), ("next", x, bw, _cj), ("row", dact, bw, _cj),
                            ("next", dact, bw, _cj), ("col", w, bw, _cj), ("col", bias.reshape(1, d), bw, _cj)],
                       outs=[(d, bw, _cj, ACT_DTYPE)], accs=[(kw, d, bw, _cj), (1, d, bw, _cj)])
    return dx, dw, db.reshape(d)


def _tri(n, reverse):
    r = lax.broadcasted_iota(jnp.int32, (n, n), 0)
    c = lax.broadcasted_iota(jnp.int32, (n, n), 1)
    return jnp.where((c >= r) if reverse else (c <= r), 1.0, 0.0).astype(F32)


def _cumsum_rows(x, *, reverse, name):
    s, w = x.shape
    ch = _tile(s, (256, 128))
    n = s // ch

    def body(x_ref, o_ref, carry):
        i = pl.program_id(0)

        @pl.when(i == 0)
        def _():
            carry[...] = jnp.zeros_like(carry)

        out = jnp.dot(_tri(ch, reverse), x_ref[...], precision=HI, preferred_element_type=F32) + carry[...]
        o_ref[...] = out
        carry[...] = out[0:1, :] if reverse else out[ch - 1:ch, :]

    imap = (lambda i: (n - 1 - i, 0)) if reverse else (lambda i: (i, 0))
    return pl.pallas_call(
        body, name=name, grid=(n,), in_specs=[pl.BlockSpec((ch, w), imap)], out_specs=pl.BlockSpec((ch, w), imap),
        out_shape=jax.ShapeDtypeStruct((s, w), F32), scratch_shapes=[pltpu.VMEM((1, w), F32)],
        compiler_params=_params(("arbitrary",)),
    )(x)


def _fox_prep(q, k, f, gq, gk, bf, *, name):
    s, d = q.shape
    scale = HEAD ** -0.5

    def fn(i, nrt, qv, kv, fv, gqv, gkv, bfv):
        qv, kv = qv.astype(F32), kv.astype(F32)
        qn = qv * lax.rsqrt(_gmean(qv * qv, HEAD) + RMS_EPS) * gqv * scale
        kn = kv * lax.rsqrt(_gmean(kv * kv, HEAD) + RMS_EPS) * gkv
        z = fv + bfv
        logf = jnp.minimum(z, 0.0) - jnp.log(1.0 + jnp.exp(-jnp.abs(z)))
        return (qn, kn, logf), ()

    return _rows(fn, name=name, s=s, tm=_tile(s, (512, 256, 128)),
                 ins=[("row", q, d, _c0), ("row", k, d, _c0), ("row", f, LANES, _c0), ("full", gq), ("full", gk), ("full", bf)],
                 outs=[(d, d, _c0, ACT_DTYPE), (d, d, _c0, ACT_DTYPE), (LANES, LANES, _c0, F32)])


def _fox_prep_bwd(q, k, f, gq, gk, bf, dqs, dkn, dlogf, *, name):
    s, d = q.shape
    scale = HEAD ** -0.5

    def fn(i, nrt, qv, kv, fv, gqv, gkv, bfv, dqv, dkv, dlf):
        outs, accs = [], []
        for xv, gv, dv, sc in ((qv, gqv, dqv, scale), (kv, gkv, dkv, 1.0)):
            xv, dv = xv.astype(F32), dv.astype(F32) * sc
            r = lax.rsqrt(_gmean(xv * xv, HEAD) + RMS_EPS)
            xh = xv * r
            g = dv * gv
            outs.append(r * (g - xh * _gmean(g * xh, HEAD)))
            accs.append(_sum_rows(dv * xh))
        z = fv + bfv
        df = dlf * _sigmoid(-z)
        outs.append(df)
        accs.append(_sum_rows(df))
        return outs, accs

    return _rows(fn, name=name, s=s, tm=_tile(s, (512, 256, 128)),
                 ins=[("row", q, d, _c0), ("row", k, d, _c0), ("row", f, LANES, _c0), ("full", gq), ("full", gk), ("full", bf),
                      ("row", dqs, d, _c0), ("row", dkn, d, _c0), ("row", dlogf, LANES, _c0)],
                 outs=[(d, d, _c0, ACT_DTYPE), (d, d, _c0, ACT_DTYPE), (LANES, LANES, _c0, ACT_DTYPE)],
                 accs=[(1, d, d, _c0), (1, d, d, _c0), (1, LANES, LANES, _c0)])


def _head_masks(shape):
    lane = lax.broadcasted_iota(jnp.int32, shape, len(shape) - 1)
    return lane < HEAD, lane >= HEAD


def _pick_lane(blk, idx):
    lane = lax.broadcasted_iota(jnp.int32, blk.shape, 1)
    return jnp.sum(jnp.where(lane == idx, blk, 0.0), axis=1, keepdims=True)


def _pick_row(blk, idx):
    sub = lax.broadcasted_iota(jnp.int32, blk.shape, 0)
    return jnp.sum(jnp.where(sub == idx, blk, 0.0), axis=0, keepdims=True)


def _fox_aug(qs, kn, cum, *, name):
    s, d = qs.shape
    hp = d // LANES

    def fn(i, nrt, qv, kv, cv):
        lane = lax.broadcasted_iota(jnp.int32, (qv.shape[0], LANES), 1)
        outs = [[], [], [], []]
        for p in range(hp):
            qt, kt = qv[:, p * LANES:(p + 1) * LANES], kv[:, p * LANES:(p + 1) * LANES]
            for h in range(2):
                mine = (lane < HEAD) if h == 0 else (lane >= HEAD)
                a0 = HEAD if h == 0 else 0
                c = cv[:, 2 * p + h:2 * p + h + 1]
                hi = c.astype(ACT_DTYPE).astype(F32)
                mid = (c - hi).astype(ACT_DTYPE).astype(F32)
                lo = (c - hi - mid).astype(ACT_DTYPE).astype(F32)
                ones = jnp.where((lane >= a0) & (lane < a0 + 3), 1.0, 0.0)
                kx = jnp.where(lane == a0, -hi, jnp.where(lane == a0 + 1, -mid, jnp.where(lane == a0 + 2, -lo, 0.0)))
                outs[h].append(jnp.where(mine, qt.astype(F32), ones))
                outs[2 + h].append(jnp.where(mine, kt.astype(F32), kx))
        return [jnp.concatenate(o, axis=1) for o in outs], ()

    return _rows(fn, name=name, s=s, tm=_tile(s, (512, 256, 128)), ins=[("row", qs, d, _c0), ("row", kn, d, _c0), ("row", cum, LANES, _c0)],
                 outs=[(d, d, _c0, ACT_DTYPE)] * 4)


def _tri_tables(nq, by_key):
    import numpy as np
    pairs = [(qi, kj) for kj in range(nq) for qi in range(kj, nq)] if by_key else [(qi, kj) for qi in range(nq) for kj in range(qi + 1)]
    return jnp.asarray(np.array([p[0] for p in pairs], np.int32)), jnp.asarray(np.array([p[1] for p in pairs], np.int32))


def _nt(a, b):
    return lax.dot_general(a, b, (((1,), (1,)), ((), ())), preferred_element_type=F32)


def _tn(a, b):
    return lax.dot_general(a, b, (((0,), (0,)), ((), ())), preferred_element_type=F32)


def _fox_dd(do, o, *, name):
    s, d = do.shape

    def fn(i, nrt, dov, ov):
        return (_reduce_heads(dov.astype(F32) * ov.astype(F32), d // HEAD, HEAD),), ()

    (dd,) = _rows(fn, name=name, s=s, tm=_tile(s, (512, 256, 128)), ins=[("row", do, d, _c0), ("row", o, d, _c0)],
                  outs=[(LANES, LANES, _c0, F32)])
    return dd


def _pair_rows(a, nh):
    s = a.shape[0]
    t = a[:, :nh].T.reshape(nh // 2, 2, s)
    return jnp.pad(t, ((0, 0), (0, SUBLANES - 2), (0, 0)))


def _rows01(r0, r1):
    sub = lax.broadcasted_iota(jnp.int32, (SUBLANES, r0.shape[1]), 0)
    return jnp.where(sub == 0, r0, jnp.where(sub == 1, r1, 0.0))


def _fox_fwd_t(q_aug, k_aug, v, *, name):
    s, d = v.shape
    bq = _tile(s, (512, 256, 128))
    nq = s // bq
    hp = d // LANES
    qtab, ktab = _tri_tables(nq, by_key=False)

    def body(qt, kt, q0_ref, q1_ref, k0_ref, k1_ref, v_ref, o_ref, lse_ref, m_sc, l_sc, acc_sc):
        t = pl.program_id(1)
        qi, kj = qt[t], kt[t]

        @pl.when(kj == 0)
        def _():
            m_sc[...] = jnp.full_like(m_sc, -jnp.inf)
            l_sc[...] = jnp.zeros_like(l_sc)
            acc_sc[...] = jnp.zeros_like(acc_sc)

        def update(diagonal):
            v2 = v_ref[...]
            for h, (q_ref, k_ref) in enumerate(((q0_ref, k0_ref), (q1_ref, k1_ref))):
                st = _nt(k_ref[...], q_ref[...])
                if diagonal:
                    st = _diag_mask_t(st)
                m_prev = m_sc[h]
                m_new = jnp.maximum(m_prev, jnp.max(st, axis=0, keepdims=True))
                p = jnp.exp(st - m_new)
                alpha = jnp.exp(m_prev - m_new)
                l_sc[h] = alpha * l_sc[h] + jnp.sum(p, axis=0, keepdims=True)
                acc_sc[h] = alpha * acc_sc[h] + _tn(v2, p.astype(MM_DTYPE))
                m_sc[h] = m_new

        @pl.when(kj < qi)
        def _():
            update(False)

        @pl.when(kj == qi)
        def _():
            update(True)
            row = lax.broadcasted_iota(jnp.int32, (LANES, bq), 0)
            ot = jnp.where(row < HEAD, acc_sc[0] / l_sc[0], acc_sc[1] / l_sc[1])
            o_ref[...] = ot.T.astype(o_ref.dtype)
            lse_ref[...] = _rows01(m_sc[0] + jnp.log(l_sc[0]), m_sc[1] + jnp.log(l_sc[1]))

    blk = (bq, LANES)
    qmap = lambda p_, t, qt, kt: (qt[t], p_)
    kmap = lambda p_, t, qt, kt: (kt[t], p_)
    grid_spec = pltpu.PrefetchScalarGridSpec(
        num_scalar_prefetch=2, grid=(hp, qtab.shape[0]),
        in_specs=[pl.BlockSpec(blk, qmap), pl.BlockSpec(blk, qmap), pl.BlockSpec(blk, kmap), pl.BlockSpec(blk, kmap), pl.BlockSpec(blk, kmap)],
        out_specs=[pl.BlockSpec(blk, qmap), pl.BlockSpec((None, SUBLANES, bq), lambda p_, t, qt, kt: (p_, 0, qt[t]))],
        scratch_shapes=[pltpu.VMEM((2, 1, bq), F32), pltpu.VMEM((2, 1, bq), F32), pltpu.VMEM((2, LANES, bq), F32)])
    o, lse = pl.pallas_call(
        body, name=name, grid_spec=grid_spec,
        out_shape=[jax.ShapeDtypeStruct((s, d), ACT_DTYPE), jax.ShapeDtypeStruct((hp, SUBLANES, s), F32)],
        compiler_params=_params(("parallel", "arbitrary")),
    )(qtab, ktab, q_aug[0], q_aug[1], k_aug[0], k_aug[1], v)
    return o, lse


def _diag_mask_t(st):
    key = lax.broadcasted_iota(jnp.int32, st.shape, 0)
    qry = lax.broadcasted_iota(jnp.int32, st.shape, 1)
    return jnp.where(qry >= key, st, -jnp.inf)


def _fox_bwd_t(q_aug, k_aug, v, lse, dd, do, *, name):
    s, d = v.shape
    bq = _tile(s, (512, 256, 128))
    nq = s // bq
    hp = d // LANES
    blk = (bq, LANES)
    qtab, ktab = _tri_tables(nq, by_key=True)
    n_steps = qtab.shape[0]

    def body(qt, kt, q0_ref, q1_ref, k0_ref, k1_ref, v_ref, lse_ref, dd_ref, do_ref,
             dq_ref, dk_ref, dv_ref, dcol_ref, drow_ref, dq_sc, rs_sc, dk_sc, dv_sc, cs_sc):
        t = pl.program_id(1)
        qi, kj = qt[t], kt[t]

        @pl.when(t == 0)
        def _():
            dq_sc[...] = jnp.zeros_like(dq_sc)
            rs_sc[...] = jnp.zeros_like(rs_sc)

        def update(diagonal):
            v2, do2 = v_ref[...], do_ref[...]
            masks = _head_masks(blk)
            row = lax.broadcasted_iota(jnp.int32, (LANES, bq), 0)
            off = pl.multiple_of(qi * bq, bq)
            for h, (q_ref, k_ref) in enumerate(((q0_ref, k0_ref), (q1_ref, k1_ref))):
                st = _nt(k_ref[...], q_ref[...])
                if diagonal:
                    st = _diag_mask_t(st)
                p = jnp.exp(st - lse_ref[h:h + 1, :])
                dp = _nt(v2, jnp.where(masks[h], do2, jnp.zeros_like(do2)))
                ds = p * (dp - dd_ref[h:h + 1, :])
                dsb = ds.astype(MM_DTYPE)
                dv_sc[h] += jnp.dot(p.astype(MM_DTYPE), do2, preferred_element_type=F32)
                dk_sc[h] += jnp.dot(dsb, q_ref[...], preferred_element_type=F32)
                cs_sc[h] += jnp.sum(ds, axis=1, keepdims=True)
                mine = (row < HEAD) if h == 0 else (row >= HEAD)
                dq_sc[:, pl.ds(off, bq)] += jnp.where(mine, _tn(k_ref[...], dsb), 0.0)
                rs_sc[h:h + 1, pl.ds(off, bq)] += jnp.sum(ds, axis=0, keepdims=True)

        @pl.when(qi == kj)
        def _():
            dk_sc[...] = jnp.zeros_like(dk_sc)
            dv_sc[...] = jnp.zeros_like(dv_sc)
            cs_sc[...] = jnp.zeros_like(cs_sc)
            update(True)

        @pl.when(qi > kj)
        def _():
            update(False)

        @pl.when(qi == nq - 1)
        def _():
            lo, _hi = _head_masks(blk)
            dk_ref[...] = jnp.where(lo, dk_sc[0], dk_sc[1]).astype(dk_ref.dtype)
            dv_ref[...] = jnp.where(lo, dv_sc[0], dv_sc[1]).astype(dv_ref.dtype)
            dcol_ref[...] = jnp.where(lo, cs_sc[0], cs_sc[1])

        @pl.when(t == n_steps - 1)
        def _():
            for c in range(nq):
                dq_ref[c * bq:(c + 1) * bq, :] = dq_sc[:, c * bq:(c + 1) * bq].T.astype(dq_ref.dtype)
            drow_ref[...] = rs_sc[...]

    qmap = lambda p_, t, qt, kt: (qt[t], p_)
    kmap = lambda p_, t, qt, kt: (kt[t], p_)
    rmap = lambda p_, t, qt, kt: (p_, 0, qt[t])
    return pl.pallas_call(
        body, name=name,
        grid_spec=pltpu.PrefetchScalarGridSpec(
            num_scalar_prefetch=2, grid=(hp, n_steps),
            in_specs=[pl.BlockSpec(blk, qmap), pl.BlockSpec(blk, qmap), pl.BlockSpec(blk, kmap), pl.BlockSpec(blk, kmap), pl.BlockSpec(blk, kmap),
                      pl.BlockSpec((None, SUBLANES, bq), rmap), pl.BlockSpec((None, SUBLANES, bq), rmap), pl.BlockSpec(blk, qmap)],
            out_specs=[pl.BlockSpec((s, LANES), lambda p_, t, qt, kt: (0, p_)), pl.BlockSpec(blk, kmap), pl.BlockSpec(blk, kmap),
                       pl.BlockSpec(blk, kmap), pl.BlockSpec((None, SUBLANES, s), lambda p_, t, qt, kt: (p_, 0, 0))],
            scratch_shapes=[pltpu.VMEM((LANES, s), F32), pltpu.VMEM((SUBLANES, s), F32), pltpu.VMEM((2, bq, LANES), F32),
                            pltpu.VMEM((2, bq, LANES), F32), pltpu.VMEM((2, bq, 1), F32)]),
        out_shape=[jax.ShapeDtypeStruct((s, d), ACT_DTYPE), jax.ShapeDtypeStruct((s, d), ACT_DTYPE), jax.ShapeDtypeStruct((s, d), ACT_DTYPE),
                   jax.ShapeDtypeStruct((s, d), F32), jax.ShapeDtypeStruct((hp, SUBLANES, s), F32)],
        compiler_params=_params(("parallel", "arbitrary")),
    )(qtab, ktab, q_aug[0], q_aug[1], k_aug[0], k_aug[1], v, lse, dd, do)


def _expand_heads(v, nh, hd):
    r = lax.broadcasted_iota(jnp.int32, (LANES, nh * hd), 0)
    c = lax.broadcasted_iota(jnp.int32, (LANES, nh * hd), 1) // hd
    e = jnp.where(r == c, 1.0, 0.0).astype(F32)
    return jnp.dot(v, e, precision=HI, preferred_element_type=F32)


def _reduce_heads(v, nh, hd):
    r = lax.broadcasted_iota(jnp.int32, (nh * hd, LANES), 0) // hd
    c = lax.broadcasted_iota(jnp.int32, (nh * hd, LANES), 1)
    e = jnp.where(r == c, 1.0, 0.0).astype(F32)
    return jnp.dot(v, e, precision=HI, preferred_element_type=F32)


def _ssd_prep(dt_raw, dt_bias, a_log, nh, *, name):
    s = dt_raw.shape[0]

    def fn(i, nrt, dtr, bsv, alv):
        dt = _softplus(dtr + bsv)
        acum = jnp.dot(_tri(SSM_CHUNK, False), dt * (-jnp.exp(alv)), precision=HI, preferred_element_type=F32)
        return (dt, acum, _expand_heads(dt, nh, HEAD), _expand_heads(acum, nh, HEAD)), ()

    w = nh * HEAD
    return _rows(fn, name=name, s=s, tm=SSM_CHUNK, ins=[("row", dt_raw, LANES, _c0), ("full", dt_bias), ("full", a_log)],
                 outs=[(LANES, LANES, _c0, F32), (LANES, LANES, _c0, F32), (w, w, _c0, F32), (w, w, _c0, F32)])


def _ssd_prep_bwd(dt_raw, dt_bias, a_log, ddtx, dacx, nh, *, name):
    s = dt_raw.shape[0]

    def fn(i, nrt, dtr, bsv, alv, ddx, dax):
        z = dtr + bsv
        dt = _softplus(z)
        a = -jnp.exp(alv)
        dda = jnp.dot(_tri(SSM_CHUNK, True), _reduce_heads(dax, nh, HEAD), precision=HI, preferred_element_type=F32)
        ddt = _reduce_heads(ddx, nh, HEAD) + dda * a
        dz = ddt * _sigmoid(z)
        lane = lax.broadcasted_iota(jnp.int32, dz.shape, 1)
        dz = jnp.where(lane < nh, dz, 0.0)
        return (dz,), (_sum_rows(dz), _sum_rows(dda * dt) * a)

    w = nh * HEAD
    return _rows(fn, name=name, s=s, tm=SSM_CHUNK,
                 ins=[("row", dt_raw, LANES, _c0), ("full", dt_bias), ("full", a_log), ("row", ddtx, w, _c0), ("row", dacx, w, _c0)],
                 outs=[(LANES, LANES, _c0, ACT_DTYPE)], accs=[(1, LANES, LANES, _c0), (1, LANES, LANES, _c0)])


def _ssd_decay(ac_blk, act_blk, head):
    col = _pick_lane(ac_blk, head)
    row = _pick_row(act_blk, head)
    r = lax.broadcasted_iota(jnp.int32, (SSM_CHUNK, SSM_CHUNK), 0)
    c = lax.broadcasted_iota(jnp.int32, (SSM_CHUNK, SSM_CHUNK), 1)
    return jnp.exp(jnp.where(r >= c, col - row, -jnp.inf))


def _group_masks(shape, hpg):
    lane = lax.broadcasted_iota(jnp.int32, shape, len(shape) - 1) // HEAD
    return [lane == k for k in range(hpg)]


def _ssd_scan_fwd(xs, bm, cm, dtx, acx, acum, acum_t, d_x, *, name):
    s, di = xs.shape
    ng = bm.shape[1] // SSM_STATE
    gw = di // ng
    hpg = gw // HEAD
    nc = s // SSM_CHUNK
    L = SSM_CHUNK
    nh_pad = acum_t.shape[0]

    def body(x_ref, b_ref, c_ref, dt_ref, ax_ref, ac_ref, act_ref, d_ref, y_ref, st_ref, state):
        g, c = pl.program_id(0), pl.program_id(1)

        @pl.when(c == 0)
        def _():
            state[...] = jnp.zeros_like(state)

        x4, bv, cv = x_ref[...].astype(F32), b_ref[...], c_ref[...]
        ax = ax_ref[...]
        tx = (x4 * dt_ref[...])
        cb = lax.dot_general(cv, bv, (((1,), (1,)), ((), ())), preferred_element_type=F32)
        masks = _group_masks((L, gw), hpg)
        y = jnp.zeros((L, gw), F32)
        txb = tx.astype(MM_DTYPE)
        for k in range(hpg):
            wk = (cb * _ssd_decay(ac_ref[...], act_ref[...], g * hpg + k)).astype(MM_DTYPE)
            y = y + jnp.where(masks[k], jnp.dot(wk, txb, preferred_element_type=F32), 0.0)
        prev = state[...]
        st_ref[...] = prev
        y = y + jnp.dot(cv, prev.astype(MM_DTYPE), preferred_element_type=F32) * jnp.exp(ax)
        y = y + d_ref[...] * x4
        y_ref[...] = y.astype(y_ref.dtype)
        a_last = ax[L - 1:L, :]
        sx = (tx * jnp.exp(a_last - ax)).astype(MM_DTYPE)
        state[...] = prev * jnp.exp(a_last) + lax.dot_general(bv, sx, (((0,), (0,)), ((), ())), preferred_element_type=F32)

    y, states = pl.pallas_call(
        body, name=name, grid=(ng, nc),
        in_specs=[pl.BlockSpec((L, gw), lambda g, c: (c, g)), pl.BlockSpec((L, SSM_STATE), lambda g, c: (c, g)),
                  pl.BlockSpec((L, SSM_STATE), lambda g, c: (c, g)), pl.BlockSpec((L, gw), lambda g, c: (c, g)),
                  pl.BlockSpec((L, gw), lambda g, c: (c, g)), pl.BlockSpec((L, LANES), lambda g, c: (c, 0)),
                  pl.BlockSpec((nh_pad, L), lambda g, c: (0, c)), pl.BlockSpec((1, gw), lambda g, c: (0, g))],
        out_specs=[pl.BlockSpec((L, gw), lambda g, c: (c, g)), pl.BlockSpec((None, None, SSM_STATE, gw), lambda g, c: (g, c, 0, 0))],
        out_shape=[jax.ShapeDtypeStruct((s, di), ACT_DTYPE), jax.ShapeDtypeStruct((ng, nc, SSM_STATE, gw), F32)],
        scratch_shapes=[pltpu.VMEM((SSM_STATE, gw), F32)],
        compiler_params=_params(("parallel", "arbitrary")),
    )(xs, bm, cm, dtx, acx, acum, acum_t, d_x)
    return y, states


def _ssd_scan_bwd(xs, bm, cm, dtx, acx, acum, acum_t, d_x, states, dy, *, name):
    s, di = xs.shape
    ng = bm.shape[1] // SSM_STATE
    gw = di // ng
    hpg = gw // HEAD
    nc = s // SSM_CHUNK
    L = SSM_CHUNK
    nh_pad = acum_t.shape[0]

    def body(x_ref, b_ref, c_ref, dt_ref, ax_ref, ac_ref, act_ref, d_ref, st_ref, dy_ref,
             dx_ref, db_ref, dc_ref, ddt_ref, dax_ref, dd_ref, dstate):
        g, cc = pl.program_id(0), pl.program_id(1)

        @pl.when(cc == 0)
        def _():
            dstate[...] = jnp.zeros_like(dstate)
            dd_ref[...] = jnp.zeros_like(dd_ref)

        x4, bv, cv = x_ref[...].astype(F32), b_ref[...], c_ref[...]
        tv, ax, dyv = dt_ref[...], ax_ref[...], dy_ref[...].astype(F32)
        prev, dn = st_ref[...], dstate[...]
        dnb = dn.astype(MM_DTYPE)
        masks = _group_masks((L, gw), hpg)
        tx = x4 * tv
        txb = tx.astype(MM_DTYPE)
        e_ax = jnp.exp(ax)
        a_last = ax[L - 1:L, :]
        e_last = jnp.exp(a_last)
        ed = jnp.exp(a_last - ax)

        dx = d_ref[...] * dyv
        dd_ref[...] += _sum_rows(dyv * x4)
        dye = (dyv * e_ax).astype(MM_DTYPE)
        yo = jnp.dot(cv, prev.astype(MM_DTYPE), preferred_element_type=F32) * e_ax
        dc = lax.dot_general(dye, prev.astype(MM_DTYPE), (((1,), (1,)), ((), ())), preferred_element_type=F32)
        dprev = lax.dot_general(cv, dye, (((0,), (0,)), ((), ())), preferred_element_type=F32)
        dax = dyv * yo
        sx = tx * ed
        dsx = jnp.dot(bv, dnb, preferred_element_type=F32)
        db = lax.dot_general(sx.astype(MM_DTYPE), dnb, (((1,), (1,)), ((), ())), preferred_element_type=F32)
        dtx_ = dsx * ed
        dsx_sx = dsx * sx
        dax = dax - dsx_sx
        dlast = _sum_rows(dsx_sx) + _sum_rows(dn * prev) * e_last
        dprev = dprev + dn * e_last
        cb = lax.dot_general(cv, bv, (((1,), (1,)), ((), ())), preferred_element_type=F32)
        dcb = jnp.zeros((L, L), F32)
        lane = lax.broadcasted_iota(jnp.int32, (L, gw), 1)
        for k in range(hpg):
            dec = _ssd_decay(ac_ref[...], act_ref[...], g * hpg + k)
            wk = (cb * dec).astype(MM_DTYPE)
            dyk = jnp.where(masks[k], dyv, 0.0).astype(MM_DTYPE)
            dtx_ = dtx_ + jnp.where(masks[k], lax.dot_general(wk, dyk, (((0,), (0,)), ((), ())), preferred_element_type=F32), 0.0)
            dwk = lax.dot_general(dyk, txb, (((1,), (1,)), ((), ())), preferred_element_type=F32)
            dcb = dcb + dwk * dec
            mk = dwk * cb * dec
            da_k = jnp.sum(mk, axis=1, keepdims=True) - jnp.sum(mk.T, axis=1, keepdims=True)
            dax = dax + jnp.where(lane == k * HEAD, da_k, 0.0)
        dcbb = dcb.astype(MM_DTYPE)
        dc = dc + jnp.dot(dcbb, bv, preferred_element_type=F32)
        db = db + lax.dot_general(dcbb, cv, (((0,), (0,)), ((), ())), preferred_element_type=F32)
        sub = lax.broadcasted_iota(jnp.int32, (L, gw), 0)
        dax = dax + jnp.where(sub == L - 1, dlast, 0.0)
        dx_ref[...] = (dx + dtx_ * tv).astype(dx_ref.dtype)
        ddt_ref[...] = dtx_ * x4
        dax_ref[...] = dax
        db_ref[...] = db.astype(db_ref.dtype)
        dc_ref[...] = dc.astype(dc_ref.dtype)
        dstate[...] = dprev

    rev = lambda g, c: (nc - 1 - c, g)
    rev0 = lambda g, c: (nc - 1 - c, 0)
    outs = pl.pallas_call(
        body, name=name, grid=(ng, nc),
        in_specs=[pl.BlockSpec((L, gw), rev), pl.BlockSpec((L, SSM_STATE), rev), pl.BlockSpec((L, SSM_STATE), rev),
                  pl.BlockSpec((L, gw), rev), pl.BlockSpec((L, gw), rev), pl.BlockSpec((L, LANES), rev0),
                  pl.BlockSpec((nh_pad, L), lambda g, c: (0, nc - 1 - c)), pl.BlockSpec((1, gw), lambda g, c: (0, g)),
                  pl.BlockSpec((None, None, SSM_STATE, gw), lambda g, c: (g, nc - 1 - c, 0, 0)), pl.BlockSpec((L, gw), rev)],
        out_specs=[pl.BlockSpec((L, gw), rev), pl.BlockSpec((L, SSM_STATE), rev), pl.BlockSpec((L, SSM_STATE), rev),
                   pl.BlockSpec((L, gw), rev), pl.BlockSpec((L, gw), rev), pl.BlockSpec((1, gw), lambda g, c: (0, g))],
        out_shape=[jax.ShapeDtypeStruct((s, di), ACT_DTYPE), jax.ShapeDtypeStruct(bm.shape, ACT_DTYPE), jax.ShapeDtypeStruct(cm.shape, ACT_DTYPE),
                   jax.ShapeDtypeStruct((s, di), F32), jax.ShapeDtypeStruct((s, di), F32), jax.ShapeDtypeStruct((1, di), F32)],
        scratch_shapes=[pltpu.VMEM((SSM_STATE, gw), F32)],
        compiler_params=_params(("parallel", "arbitrary")),
    )(xs, bm, cm, dtx, acx, acum, acum_t, d_x, states, dy)
    return outs


def _ssd_gate(y, z, w, gs, *, name):
    s, d = y.shape

    def fn(i, nrt, yv, zv, wv):
        zv = zv.astype(F32)
        u = yv.astype(F32) * zv * _sigmoid(zv)
        return (u * lax.rsqrt(_gmean(u * u, gs) + RMS_EPS) * wv,), ()

    (o,) = _rows(fn, name=name, s=s, tm=_tile(s, (256, 128)), ins=[("row", y, d, _c0), ("row", z, d, _c0), ("full", w.reshape(1, d))],
                 outs=[(d, d, _c0, ACT_DTYPE)])
    return o


def _ssd_gate_bwd(y, z, w, do, gs, *, name):
    s, d = y.shape

    def fn(i, nrt, yv, zv, wv, dov):
        yv, zv, dov = yv.astype(F32), zv.astype(F32), dov.astype(F32)
        sg = _sigmoid(zv)
        sl = zv * sg
        u = yv * sl
        r = lax.rsqrt(_gmean(u * u, gs) + RMS_EPS)
        uh = u * r
        g = dov * wv
        du = r * (g - uh * _gmean(g * uh, gs))
        return (du * sl, du * yv * sg * (1.0 + zv * (1.0 - sg))), (_sum_rows(dov * uh),)

    dy, dz, dw = _rows(fn, name=name, s=s, tm=_tile(s, (256, 128)),
                       ins=[("row", y, d, _c0), ("row", z, d, _c0), ("full", w.reshape(1, d)), ("row", do, d, _c0)],
                       outs=[(d, d, _c0, ACT_DTYPE), (d, d, _c0, ACT_DTYPE)], accs=[(1, d, d, _c0)])
    return dy, dz, dw.reshape(d)


def _pad_lanes(w):
    return jnp.pad(w, ((0, 0), (0, LANES - w.shape[1])))


def _nt_sum(pairs, name):
    acc = None
    for a, b in pairs:
        acc = _mm(a, b, tb=True, add=acc, name=name)
    return acc


def _conv_mixer_fwd(h, w_in, layer, w_dw, tag):
    d, ns = h.shape[1], w_in.shape[3]
    inter = (3, _tile(math.gcd(d, ns), TILES))
    p = _mm(h, w_in, b_layer=layer, inter=inter, out_dtype=ACT_DTYPE, name=f"{tag}_in")
    return _gconv_fwd(p, w_dw, inter[1], name=f"{tag}_gate"), (h, w_in, layer, inter, p, w_dw)


def _conv_mixer_bwd(cache, do, tag):
    h, w_in, layer, inter, p, w_dw = cache
    dp, dw_dw = _gconv_bwd(p, w_dw, do, inter[1], name=f"{tag}_gate_bwd")
    dw_in = _mm(h, dp, ta=True, out_shard=w_in.shape[3], inter=inter, out_dtype=WIRE_DTYPE, name=f"{tag}_dw_in")
    dh = _mm(dp, w_in, tb=True, b_layer=layer, inter=inter, name=f"{tag}_dh")
    return dh, {"w_in": dw_in, "w_dw": dw_dw}


def _fox_mixer_fwd(h, w_in, b_f, q_gain, k_gain, tag):
    d = h.shape[1]
    nh = d // HEAD
    ws = [w_in[:, k * d:(k + 1) * d] for k in range(3)] + [_pad_lanes(w_in[:, 3 * d:])]
    q, k, v = [_mm(h, w, out_dtype=ACT_DTYPE, name=f"{tag}_in") for w in ws[:3]]
    f = _mm(h, ws[3], name=f"{tag}_in_f")
    gq = jnp.tile(q_gain, nh).reshape(1, d)
    gk = jnp.tile(k_gain, nh).reshape(1, d)
    bf = _pad_lanes(b_f.reshape(1, nh))
    qs, kn, logf = _fox_prep(q, k, f, gq, gk, bf, name=f"{tag}_prep")
    cum = _cumsum_rows(logf, reverse=False, name=f"{tag}_cum")
    aug = _fox_aug(qs, kn, cum, name=f"{tag}_aug")
    q_aug, k_aug = aug[:2], aug[2:]
    o, lse = _fox_fwd_t(q_aug, k_aug, v, name=f"{tag}_attn")
    return o, (h, ws, q, k, v, f, gq, gk, bf, q_aug, k_aug, o, lse)


def _fox_mixer_bwd(cache, do, tag):
    h, ws, q, k, v, f, gq, gk, bf, q_aug, k_aug, o, lse = cache
    s, d = q.shape
    nh = d // HEAD
    dd = _pair_rows(_fox_dd(do, o, name=f"{tag}_attn_dd"), nh)
    dqs, dkn, dv, dcol, drow = _fox_bwd_t(q_aug, k_aug, v, lse, dd, do, name=f"{tag}_attn_bwd")
    dcum = _pad_lanes(drow[:, :2, :].reshape(nh, s).T - dcol[:, ::HEAD])
    dlogf = _cumsum_rows(dcum, reverse=True, name=f"{tag}_cum_bwd")
    dq, dk, df, dgq, dgk, dbf = _fox_prep_bwd(q, k, f, gq, gk, bf, dqs, dkn, dlogf, name=f"{tag}_prep_bwd")
    dps = (dq, dk, dv, df)
    dws = [_mm(h, dp, ta=True, name=f"{tag}_dw_in") for dp in dps]
    dw_in = jnp.concatenate(dws[:3] + [dws[3][:, :nh]], axis=1)
    dh = _nt_sum(list(zip(dps, ws)), f"{tag}_dh")
    return dh, {"w_in": dw_in, "b_f": dbf[0, :nh], "q_gain": dgq.reshape(nh, HEAD).sum(0), "k_gain": dgk.reshape(nh, HEAD).sum(0)}


def _ssd_mixer_fwd(h, w_in, conv_w, conv_b, dt_bias, a_log, d_skip, norm_w, tag):
    di = norm_w.shape[0]
    nh = di // HEAD
    gn = (conv_w.shape[1] - di) // 2
    cuts = [0, di, 2 * di, 2 * di + gn, 2 * di + 2 * gn]
    ws = [w_in[:, cuts[k]:cuts[k + 1]] for k in range(4)] + [_pad_lanes(w_in[:, cuts[4]:])]
    z, xr, br, cr = [_mm(h, w, out_dtype=ACT_DTYPE, name=f"{tag}_in") for w in ws[:4]]
    dtr = _mm(h, ws[4], name=f"{tag}_in_dt")
    ccuts = [0, di, di + gn, di + 2 * gn]
    cws = [conv_w[:, ccuts[k]:ccuts[k + 1]] for k in range(3)]
    cbs = [conv_b[ccuts[k]:ccuts[k + 1]] for k in range(3)]
    xs, bm, cm = [_sconv_fwd(r, w, b, name=f"{tag}_conv") for r, w, b in zip((xr, br, cr), cws, cbs)]
    dtb = _pad_lanes(dt_bias.reshape(1, nh))
    alg = _pad_lanes(a_log.reshape(1, nh))
    _dt, acum, dtx, acx = _ssd_prep(dtr, dtb, alg, nh, name=f"{tag}_prep")
    acum_t = acum[:, :nh].T
    d_x = jnp.repeat(d_skip, HEAD).reshape(1, di)
    y, states = _ssd_scan_fwd(xs, bm, cm, dtx, acx, acum, acum_t, d_x, name=f"{tag}_scan")
    gs = di // (gn // SSM_STATE)
    o = _ssd_gate(y, z, norm_w, gs, name=f"{tag}_gate")
    return o, (h, ws, z, (xr, br, cr), dtr, cws, cbs, xs, bm, cm, dtb, alg, dtx, acx, acum, acum_t, d_x, states, y, norm_w, gs, nh)


def _ssd_mixer_bwd(cache, do, tag):
    h, ws, z, raws, dtr, cws, cbs, xs, bm, cm, dtb, alg, dtx, acx, acum, acum_t, d_x, states, y, norm_w, gs, nh = cache
    dy, dz, dnorm = _ssd_gate_bwd(y, z, norm_w, do, gs, name=f"{tag}_gate_bwd")
    dxs, dbm, dcm, ddtx, dacx, dd_x = _ssd_scan_bwd(xs, bm, cm, dtx, acx, acum, acum_t, d_x, states, dy, name=f"{tag}_scan_bwd")
    ddtr, ddtb, dalg = _ssd_prep_bwd(dtr, dtb, alg, ddtx, dacx, nh, name=f"{tag}_prep_bwd")
    conv = [_sconv_bwd(r, w, b, da, name=f"{tag}_conv_bwd") for r, w, b, da in zip(raws, cws, cbs, (dxs, dbm, dcm))]
    dps = (dz, conv[0][0], conv[1][0], conv[2][0], ddtr)
    dws = [_mm(h, dp, ta=True, name=f"{tag}_dw_in") for dp in dps]
    dw_in = jnp.concatenate(dws[:4] + [dws[4][:, :nh]], axis=1)
    dh = _nt_sum(list(zip(dps, ws)), f"{tag}_dh")
    return dh, {"w_in": dw_in, "conv_w": jnp.concatenate([c[1] for c in conv], axis=1), "conv_b": jnp.concatenate([c[2] for c in conv]),
                "dt_bias": ddtb[0, :nh], "a_log": dalg[0, :nh], "d": dd_x.reshape(nh, HEAD).sum(1), "norm_w": dnorm}


def _rows_natural(cm, layer):
    return cm[:, layer].reshape(-1, cm.shape[3])


def _cols_natural(cm, layer):
    return jnp.moveaxis(cm[:, layer], 0, 1).reshape(cm.shape[2], -1)


def _cols_chip_major(g):
    return jnp.moveaxis(g.reshape(g.shape[0], N_CHIPS, -1), 1, 0).astype(WIRE_DTYPE)


def _local_step(x, tgt, fw, cm):
    depth = fw["mix_norm"].shape[0]
    layers = []
    xc, y_prev = x, None
    for i in range(depth):
        kind, j = i % 3, i // 3
        tag = f"l{i}"
        xin, h = _resid_rms(xc, y_prev, fw["mix_norm"][i], name=f"{tag}_norm1")
        if kind == 0:
            o, mc = _conv_mixer_fwd(h, cm["conv_w_in"], j, fw["conv_w_dw"][j], tag + "_conv")
            w_out = _rows_natural(cm["conv_w_out"], j)
        elif kind == 1:
            o, mc = _fox_mixer_fwd(h, _cols_natural(cm["fox_w_in"], j), fw["fox_b_f"][j], fw["fox_q_gain"][j], fw["fox_k_gain"][j], tag + "_fox")
            w_out = _rows_natural(cm["fox_w_out"], j)
        else:
            o, mc = _ssd_mixer_fwd(h, _cols_natural(cm["ssd_w_in"], j), fw["ssd_conv_w"][j], fw["ssd_conv_b"][j], fw["ssd_dt_bias"][j],
                                   fw["ssd_a_log"][j], fw["ssd_d"][j], fw["ssd_norm_w"][j], tag + "_ssd")
            w_out = _rows_natural(cm["ssd_w_out"], j)
        ym = _mm(o, w_out, name=f"{tag}_mix_out")
        x1, h2 = _resid_rms(xin, ym, fw["ffn_norm"][i], name=f"{tag}_norm2")
        w_gu, w_down = cm["ffn_w_gu"], _rows_natural(cm["ffn_w_down"], i)
        inter = (2, _tile(math.gcd(w_down.shape[0], w_gu.shape[3]), TILES))
        gu = _mm(h2, w_gu, b_layer=i, inter=inter, out_dtype=ACT_DTYPE, name=f"{tag}_ffn_gu")
        a = _swiglu_fwd(gu, inter[1], name=f"{tag}_swiglu")
        yf = _mm(a, w_down, name=f"{tag}_ffn_down")
        layers.append((xin, o, mc, w_out, x1, h2, w_down, inter, gu, a))
        xc, y_prev = x1, yf
    loss, dx, dxb = _loss_head(xc, y_prev, tgt, name="loss_head")

    names = ("conv", "fox", "ssd")
    small = {k: [None] * v.shape[0] for k, v in fw.items()}
    big = {k: [None] * v.shape[1] for k, v in cm.items()}
    for i in reversed(range(depth)):
        kind, j = i % 3, i // 3
        tag = f"l{i}"
        xin, o, mc, w_out, x1, h2, w_down, inter, gu, a = layers[i]
        big["ffn_w_down"][i] = _mm(a, dxb, ta=True, out_dtype=WIRE_DTYPE, name=f"{tag}_dw_down").reshape(N_CHIPS, -1, w_down.shape[1])
        da = _mm(dxb, w_down, tb=True, out_dtype=ACT_DTYPE, name=f"{tag}_da")
        dgu = _swiglu_bwd(gu, da, inter[1], name=f"{tag}_swiglu_bwd")
        big["ffn_w_gu"][i] = _mm(h2, dgu, ta=True, out_shard=cm["ffn_w_gu"].shape[3], inter=inter, out_dtype=WIRE_DTYPE, name=f"{tag}_dw_gu")
        dh2 = _mm(dgu, cm["ffn_w_gu"], tb=True, b_layer=i, inter=inter, name=f"{tag}_dh2")
        dx1, dx1b, small["ffn_norm"][i] = _rms_bwd(x1, fw["ffn_norm"][i], dh2, dx, name=f"{tag}_norm2_bwd")
        big[names[kind] + "_w_out"][j] = _mm(o, dx1b, ta=True, out_dtype=WIRE_DTYPE, name=f"{tag}_dw_out").reshape(N_CHIPS, -1, w_out.shape[1])
        do = _mm(dx1b, w_out, tb=True, out_dtype=ACT_DTYPE, name=f"{tag}_do")
        if kind == 0:
            dh, mg = _conv_mixer_bwd(mc, do, tag + "_conv")
        elif kind == 1:
            dh, mg = _fox_mixer_bwd(mc, do, tag + "_fox")
        else:
            dh, mg = _ssd_mixer_bwd(mc, do, tag + "_ssd")
        for k, v in mg.items():
            if k == "w_in":
                big[f"{names[kind]}_w_in"][j] = v if kind == 0 else _cols_chip_major(v)
            else:
                small[f"{names[kind]}_{k}"][j] = v
        dx, dxb, small["mix_norm"][i] = _rms_bwd(xin, fw["mix_norm"][i], dh, dx1, name=f"{tag}_norm1_bwd")
    return loss, dx, {k: jnp.stack(v) for k, v in small.items()}, big


ANY = pl.BlockSpec(memory_space=pl.ANY)
VMEM_SPEC = pl.BlockSpec(memory_space=pltpu.VMEM)


def _place():
    return lax.axis_index("x"), lax.axis_index("y"), lax.axis_index("c")


def _remote(src, dst, send_sems, recv_sems, k, to):
    return pltpu.make_async_remote_copy(src_ref=src, dst_ref=dst, send_sem=send_sems.at[k], recv_sem=recv_sems.at[k],
                                        device_id=to, device_id_type=MESH)


def _half_of(ref, h, shape):
    layers, rows, _ = shape
    if layers % 2 == 0:
        return ref.at[pl.ds(h * (layers // 2), layers // 2)]
    return ref.at[:, pl.ds(pl.multiple_of(h * (rows // 2), 16), rows // 2)]


def _row_half(ref, h, rows):
    return ref.at[:, pl.ds(pl.multiple_of(h * (rows // 2), 16), rows // 2)]


def _gather_list(ws, *, name):
    n = len(ws)

    def body(*refs):
        w_refs, o_refs, send_sems, recv_sems = refs[:n], refs[n:2 * n], refs[2 * n], refs[2 * n + 1]
        x, y, c = _place()
        me, sibling, m = (x, y, c), (x, y, 1 - c), 2 * x + y
        chips = [(1 - x, y), (x, 1 - y), (1 - x, 1 - y)]
        first, passed = [], []
        for p, (w_ref, o_ref) in enumerate(zip(w_refs, o_refs)):
            for j, (px, py) in enumerate(chips):
                cp = _remote(_half_of(w_ref, c, ws[p].shape), _half_of(o_ref.at[m], c, ws[p].shape), send_sems, recv_sems, 6 * p + j, (px, py, c))
                cp.start()
                first.append(cp)
        for p, o_ref in enumerate(o_refs):
            for j, (px, py) in enumerate(chips):
                blk = _half_of(o_ref.at[2 * px + py], c, ws[p].shape)
                _remote(blk, blk, send_sems, recv_sems, 6 * p + j, me).wait_recv()
                fwd = _remote(blk, blk, send_sems, recv_sems, 6 * p + 3 + j, sibling)
                fwd.start()
                passed.append(fwd)
        for p, o_ref in enumerate(o_refs):
            for j, (px, py) in enumerate(chips):
                blk = _half_of(o_ref.at[2 * px + py], 1 - c, ws[p].shape)
                _remote(blk, blk, send_sems, recv_sems, 6 * p + 3 + j, me).wait_recv()
        for cp in first + passed:
            cp.wait_send()

    return pl.pallas_call(
        body, name=name, in_specs=[ANY] * n, out_specs=[ANY] * n,
        out_shape=[jax.ShapeDtypeStruct((N_CHIPS,) + w.shape, w.dtype) for w in ws],
        scratch_shapes=[pltpu.SemaphoreType.DMA((6 * n,)), pltpu.SemaphoreType.DMA((6 * n,))],
    )(*ws)


def _gather_small(v, *, name):
    r, w = v.shape

    def body(v_ref, o_ref, send_sems, recv_sems):
        x, y, c = _place()
        m = 2 * x + y
        chips = [(1 - x, y), (x, 1 - y), (1 - x, 1 - y)]
        o_ref[m] = v_ref[...]
        sends = [_remote(v_ref, o_ref.at[m], send_sems, recv_sems, j, (px, py, c)) for j, (px, py) in enumerate(chips)]
        for cp in sends:
            cp.start()
        for j, (px, py) in enumerate(chips):
            blk = o_ref.at[2 * px + py]
            _remote(blk, blk, send_sems, recv_sems, j, (x, y, c)).wait_recv()
        for cp in sends:
            cp.wait_send()

    return pl.pallas_call(
        body, name=name, in_specs=[VMEM_SPEC], out_specs=VMEM_SPEC, out_shape=jax.ShapeDtypeStruct((4, r, w), v.dtype),
        scratch_shapes=[pltpu.SemaphoreType.DMA((3,)), pltpu.SemaphoreType.DMA((3,))],
    )(v)


def _swap_halves_list(gs, *, name):
    n = len(gs)

    def body(*refs):
        g_refs, o_refs, send_sems, recv_sems = refs[:n], refs[n:2 * n], refs[2 * n], refs[2 * n + 1]
        x, y, c = _place()
        cps = [_remote(_row_half(g_ref, 1 - c, gs[p].shape[1]), o_ref, send_sems, recv_sems, p, (x, y, 1 - c))
               for p, (g_ref, o_ref) in enumerate(zip(g_refs, o_refs))]
        for cp in cps:
            cp.start()
        for cp in cps:
            cp.wait()

    return pl.pallas_call(
        body, name=name, in_specs=[ANY] * n, out_specs=[ANY] * n,
        out_shape=[jax.ShapeDtypeStruct((g.shape[0], g.shape[1] // 2, g.shape[2]), g.dtype) for g in gs],
        scratch_shapes=[pltpu.SemaphoreType.DMA((n,)), pltpu.SemaphoreType.DMA((n,))],
    )(*gs)


def _join_halves_list(reds, *, name):
    n = len(reds)

    def body(*refs):
        o_refs, send_sems, recv_sems = refs[n:2 * n], refs[2 * n], refs[2 * n + 1]
        x, y, c = _place()
        cps = []
        for p, o_ref in enumerate(o_refs):
            rh = reds[p].shape[0] // 2
            mine = o_ref.at[pl.ds(pl.multiple_of(c * rh, SUBLANES), rh)]
            cp = _remote(mine, mine, send_sems, recv_sems, p, (x, y, 1 - c))
            cp.start()
            cps.append(cp)
        for p, o_ref in enumerate(o_refs):
            rh = reds[p].shape[0] // 2
            other = o_ref.at[pl.ds(pl.multiple_of((1 - c) * rh, SUBLANES), rh)]
            _remote(other, other, send_sems, recv_sems, p, (x, y, c)).wait_recv()
        for cp in cps:
            cp.wait_send()

    return pl.pallas_call(
        body, name=name, in_specs=[ANY] * n, out_specs=[ANY] * n, out_shape=[jax.ShapeDtypeStruct(r.shape, r.dtype) for r in reds],
        input_output_aliases={p: p for p in range(n)},
        scratch_shapes=[pltpu.SemaphoreType.DMA((n,)), pltpu.SemaphoreType.DMA((n,))],
    )(*reds)


def _scatter_chips_list(sums, *, name):
    n = len(sums)

    def body(*refs):
        a_refs, o_refs, send_sems, recv_sems = refs[:n], refs[n:2 * n], refs[2 * n], refs[2 * n + 1]
        x, y, c = _place()
        m = 2 * x + y
        chips = [(1 - x, y), (x, 1 - y), (1 - x, 1 - y)]
        sends = []
        for p, (a_ref, o_ref) in enumerate(zip(a_refs, o_refs)):
            for j, (px, py) in enumerate(chips):
                cp = _remote(a_ref.at[2 * px + py], o_ref.at[m], send_sems, recv_sems, 3 * p + j, (px, py, c))
                cp.start()
                sends.append(cp)
        for p, o_ref in enumerate(o_refs):
            for j, (px, py) in enumerate(chips):
                blk = o_ref.at[2 * px + py]
                _remote(blk, blk, send_sems, recv_sems, 3 * p + j, (x, y, c)).wait_recv()
        for cp in sends:
            cp.wait_send()

    return pl.pallas_call(
        body, name=name, in_specs=[ANY] * n, out_specs=[ANY] * n, out_shape=[jax.ShapeDtypeStruct(a.shape, a.dtype) for a in sums],
        scratch_shapes=[pltpu.SemaphoreType.DMA((3 * n,)), pltpu.SemaphoreType.DMA((3 * n,))],
    )(*sums)


def _allreduce_small(v, *, name):
    r, w = v.shape

    def body(v_ref, o_ref, slots, send_sems, recv_sems):
        x, y, c = _place()
        me = 4 * x + 2 * y + c
        slots[me] = v_ref[...]
        peers = [((1 - x) if k & 4 else x, (1 - y) if k & 2 else y, (1 - c) if k & 1 else c) for k in range(1, 8)]
        sends = [_remote(v_ref, slots.at[me], send_sems, recv_sems, k, p) for k, p in enumerate(peers)]
        for cp in sends:
            cp.start()
        for k, (px, py, pc) in enumerate(peers):
            blk = slots.at[4 * px + 2 * py + pc]
            _remote(blk, blk, send_sems, recv_sems, k, (x, y, c)).wait_recv()
        for cp in sends:
            cp.wait_send()
        acc = slots[0]
        for k in range(1, 8):
            acc = acc + slots[k]
        o_ref[...] = acc

    return pl.pallas_call(
        body, name=name, in_specs=[VMEM_SPEC], out_specs=VMEM_SPEC, out_shape=jax.ShapeDtypeStruct(v.shape, v.dtype),
        scratch_shapes=[pltpu.VMEM((8, r, w), F32), pltpu.SemaphoreType.DMA((7,)), pltpu.SemaphoreType.DMA((7,))],
    )(v)


def _row_tile(r):
    return r if r <= 512 else _tile(r, (512, 256, 128, 64, 32, 16))


def _add_half(g, recv, place, *, name):
    n, r, w = g.shape
    tm = _row_tile(r // 2)
    nb = (r // 2) // tm

    def body(place_ref, g_ref, r_ref, o_ref):
        o_ref[...] = (g_ref[...].astype(F32) + r_ref[...].astype(F32)).astype(o_ref.dtype)

    return pl.pallas_call(
        body, name=name,
        grid_spec=pltpu.PrefetchScalarGridSpec(
            num_scalar_prefetch=1, grid=(n, nb),
            in_specs=[pl.BlockSpec((None, tm, w), lambda k, i, p: (k, p[1] * nb + i, 0)), pl.BlockSpec((None, tm, w), lambda k, i, p: (k, i, 0))],
            out_specs=pl.BlockSpec((None, tm, w), lambda k, i, p: (k, i, 0))),
        out_shape=jax.ShapeDtypeStruct(recv.shape, g.dtype), compiler_params=_params(("parallel", "parallel")),
    )(place, g, recv)


def _sum_chips(recv, own, place, *, name):
    n, r, w = recv.shape
    tm = _row_tile(r)
    nb = r // tm

    def body(place_ref, *refs):
        own_ref, o_ref = refs[n], refs[n + 1]
        acc = None
        for k in range(n):
            term = jnp.where(place_ref[0] == k, own_ref[...], refs[k][...]).astype(F32)
            acc = term if acc is None else acc + term
        o_ref[...] = acc

    recv_specs = [pl.BlockSpec((None, tm, w), lambda i, p, k=k: (jnp.where(p[0] == k, (k + 1) % n, k), i, 0)) for k in range(n)]
    return pl.pallas_call(
        body, name=name,
        grid_spec=pltpu.PrefetchScalarGridSpec(
            num_scalar_prefetch=1, grid=(nb,),
            in_specs=recv_specs + [pl.BlockSpec((None, tm, w), lambda i, p: (p[0], i, 0))],
            out_specs=pl.BlockSpec((tm, w), lambda i, p: (p[1] * nb + i, 0))),
        out_shape=jax.ShapeDtypeStruct((2 * r, w), F32), compiler_params=_params(("parallel",)),
    )(place, *([recv] * n), own)


def _adamw(w, g, m, v, *, name):
    shape = w.shape
    cols = shape[-1]
    rows = math.prod(shape[:-1])
    tm = _tile(rows, (256, 128, 64, 32, 16, 8))
    c1 = 1.0 - ADAM_B1 ** ADAM_STEP
    c2 = 1.0 - ADAM_B2 ** ADAM_STEP

    def fn(i, nrt, wv, gv, mv, vv):
        mn = ADAM_B1 * mv + (1.0 - ADAM_B1) * gv
        vn = ADAM_B2 * vv + (1.0 - ADAM_B2) * (gv * gv)
        delta = -ADAM_LR * ((mn / c1) / (jnp.sqrt(vn / c2) + ADAM_EPS) + ADAM_WD * wv)
        return (delta, mn, vn), ()

    outs = _rows(fn, name=name, s=rows, tm=tm, ins=[("row", t.reshape(rows, cols), cols, _c0) for t in (w, g, m, v)],
                 outs=[(cols, cols, _c0, F32)] * 3)
    return [o.reshape(shape) for o in outs]


WEIGHTS = ["mix_norm", "ffn_norm", "ffn_w_gu", "ffn_w_down", "conv_w_in", "conv_w_dw", "conv_w_out", "fox_w_in", "fox_b_f", "fox_q_gain",
           "fox_k_gain", "fox_w_out", "ssd_w_in", "ssd_conv_w", "ssd_conv_b", "ssd_dt_bias", "ssd_a_log", "ssd_d", "ssd_norm_w", "ssd_w_out"]
SHARD_AXIS = {"ffn_w_gu": 2, "ffn_w_down": 1, "conv_w_in": 2, "conv_w_dw": 2, "conv_w_out": 1, "fox_w_in": 2, "fox_w_out": 1, "ssd_w_in": 2,
              "ssd_conv_w": 2, "ssd_conv_b": 1, "ssd_norm_w": 1, "ssd_w_out": 1}
BIG = ["ffn_w_gu", "ffn_w_down", "conv_w_in", "conv_w_out", "fox_w_in", "fox_w_out", "ssd_w_in", "ssd_w_out"]
SMALL_SHARDED = ["conv_w_dw", "ssd_conv_w", "ssd_conv_b", "ssd_norm_w"]
N_CHIPS = 4


def _pack_flat(parts, pad_to):
    flat = [p.reshape(-1) for p in parts]
    offs, n = [], 0
    for f in flat:
        offs.append(n)
        n += f.shape[0]
    total = -(-n // pad_to) * pad_to
    if total > n:
        flat.append(jnp.zeros((total - n,), flat[0].dtype))
    return jnp.concatenate(flat).reshape(-1, LANES), offs


def kernel(x, mix_norm, ffn_norm, ffn_w_gu, ffn_w_down, conv_w_in, conv_w_dw, conv_w_out, fox_w_in, fox_b_f, fox_q_gain, fox_k_gain, fox_w_out, ssd_w_in, ssd_conv_w, ssd_conv_b, ssd_dt_bias, ssd_a_log, ssd_d, ssd_norm_w, ssd_w_out, loss_target, m_mix_norm, m_ffn_norm, m_ffn_w_gu, m_ffn_w_down, m_conv_w_in, m_conv_w_dw, m_conv_w_out, m_fox_w_in, m_fox_b_f, m_fox_q_gain, m_fox_k_gain, m_fox_w_out, m_ssd_w_in, m_ssd_conv_w, m_ssd_conv_b, m_ssd_dt_bias, m_ssd_a_log, m_ssd_d, m_ssd_norm_w, m_ssd_w_out, v_mix_norm, v_ffn_norm, v_ffn_w_gu, v_ffn_w_down, v_conv_w_in, v_conv_w_dw, v_conv_w_out, v_fox_w_in, v_fox_b_f, v_fox_q_gain, v_fox_k_gain, v_fox_w_out, v_ssd_w_in, v_ssd_conv_w, v_ssd_conv_b, v_ssd_dt_bias, v_ssd_a_log, v_ssd_d, v_ssd_norm_w, v_ssd_w_out):
    w = dict(zip(WEIGHTS, (mix_norm, ffn_norm, ffn_w_gu, ffn_w_down, conv_w_in, conv_w_dw, conv_w_out, fox_w_in, fox_b_f, fox_q_gain, fox_k_gain,
                           fox_w_out, ssd_w_in, ssd_conv_w, ssd_conv_b, ssd_dt_bias, ssd_a_log, ssd_d, ssd_norm_w, ssd_w_out)))
    m1 = dict(zip(WEIGHTS, (m_mix_norm, m_ffn_norm, m_ffn_w_gu, m_ffn_w_down, m_conv_w_in, m_conv_w_dw, m_conv_w_out, m_fox_w_in, m_fox_b_f,
                            m_fox_q_gain, m_fox_k_gain, m_fox_w_out, m_ssd_w_in, m_ssd_conv_w, m_ssd_conv_b, m_ssd_dt_bias, m_ssd_a_log, m_ssd_d,
                            m_ssd_norm_w, m_ssd_w_out)))
    m2 = dict(zip(WEIGHTS, (v_mix_norm, v_ffn_norm, v_ffn_w_gu, v_ffn_w_down, v_conv_w_in, v_conv_w_dw, v_conv_w_out, v_fox_w_in, v_fox_b_f,
                            v_fox_q_gain, v_fox_k_gain, v_fox_w_out, v_ssd_w_in, v_ssd_conv_w, v_ssd_conv_b, v_ssd_dt_bias, v_ssd_a_log, v_ssd_d,
                            v_ssd_norm_w, v_ssd_w_out)))
    cx, cy, cc = _place()
    chip = 2 * cx + cy

    wb = [w[n].astype(WIRE_DTYPE) for n in BIG]
    gathered = _gather_list(wb, name="gather_weights")
    cm = {n: lax.dynamic_update_slice(g_, wn[None], (chip, 0, 0, 0)) for n, g_, wn in zip(BIG, gathered, wb)}
    sp, soffs = _pack_flat([w[n] for n in SMALL_SHARDED], SUBLANES * LANES)
    sgath = _gather_small(sp, name="gather_small").reshape(N_CHIPS, -1)
    full = {n: w[n] for n in WEIGHTS if n not in SHARD_AXIS}
    for n, off in zip(SMALL_SHARDED, soffs):
        full[n] = jnp.concatenate([sgath[j, off:off + w[n].size].reshape(w[n].shape) for j in range(N_CHIPS)], axis=SHARD_AXIS[n])

    loss, gx, grads, big = _local_step(x[0], loss_target[0], full, cm)
    loss = lax.psum(loss, ("x", "y", "c"))

    where = [(n, l) for n in BIG for l in range(w[n].shape[0])]
    gs = [big[n][l] for n, l in where]
    place = jnp.stack([chip, cc]).astype(jnp.int32)
    from_sibling = _swap_halves_list(gs, name="reduce_halves")
    chip_sums = [_add_half(g_, r_, place, name="reduce_add_sibling") for g_, r_ in zip(gs, from_sibling)]
    by_chip = _scatter_chips_list(chip_sums, name="reduce_chips")
    reds = [_sum_chips(b_, s_, place, name="reduce_sum_chips") for b_, s_ in zip(by_chip, chip_sums)]
    reds = dict(zip(where, _join_halves_list(reds, name="reduce_share")))

    small_names = [n for n in WEIGHTS if n not in BIG]
    sm, smoffs = _pack_flat([grads[n] for n in small_names], SUBLANES * LANES)
    sred = _allreduce_small(sm, name="allreduce_small").reshape(-1)

    g = {n: jnp.stack([reds[(n, l)] for l in range(w[n].shape[0])]).reshape(w[n].shape) for n in BIG}
    for n, off in zip(small_names, smoffs):
        fullg = sred[off:off + grads[n].size].reshape(grads[n].shape)
        if n in SHARD_AXIS:
            ax = SHARD_AXIS[n]
            fullg = lax.dynamic_slice_in_dim(fullg, chip * w[n].shape[ax], w[n].shape[ax], axis=ax)
        g[n] = fullg

    deltas, new_m, new_v = [], [], []
    for n in WEIGHTS:
        dl, mn, vn = _adamw(w[n], g[n], m1[n], m2[n], name=f"adamw_{n}")
        deltas.append(dl)
        new_m.append(mn)
        new_v.append(vn)
    return (loss, gx[None], *[g[n] for n in WEIGHTS], *deltas, *new_m, *new_v)
```

```python
import functools
import math

import jax
import jax.numpy as jnp
from jax import lax
from jax.experimental import pallas as pl
from jax.experimental.pallas import tpu as pltpu

F32 = jnp.float32
MM_DTYPE = jnp.bfloat16
ACT_DTYPE = jnp.bfloat16
WIRE_DTYPE = jnp.bfloat16

RMS_EPS = 1e-6
HEAD = 64
SSM_STATE = 128
SSM_CHUNK = 128
LANES = 128
SUBLANES = 8
VMEM_LIMIT = 48 * 1024 * 1024

ADAM_LR, ADAM_B1, ADAM_B2, ADAM_EPS, ADAM_WD, ADAM_STEP = 0.001, 0.9, 0.999, 1e-08, 0.01, 10

HI = lax.Precision.HIGHEST
MESH = pl.DeviceIdType.MESH


def _tile(dim, prefs):
    for p in prefs:
        if dim % p == 0:
            return p
    return dim


def _params(sem):
    return pltpu.CompilerParams(dimension_semantics=sem, vmem_limit_bytes=VMEM_LIMIT)


def _sigmoid(x):
    return 1.0 / (1.0 + jnp.exp(-x))


def _softplus(x):
    return jnp.maximum(x, 0.0) + jnp.log(1.0 + jnp.exp(-jnp.abs(x)))


TILES = (1024, 1408, 768, 512, 256, 128)
MIN_STEP_WORK = 1 << 29


def _mm(a, b, *, ta=False, tb=False, add=None, out_dtype=F32, name, b_layer=None, out_shard=None, inter=None):
    ka, m = (a.shape[0], a.shape[1]) if ta else (a.shape[1], a.shape[0])
    if b_layer is None:
        kb, n = (b.shape[1], b.shape[0]) if tb else (b.shape[0], b.shape[1])
        ns = None
    else:
        ns = b.shape[3]
        kb, n = (b.shape[0] * ns, b.shape[2]) if tb else (b.shape[2], b.shape[0] * ns)
    assert ka == kb, (a.shape, b.shape, ta, tb)
    k = ka
    tm = _tile(m, (1408, 1024, 512, 256, 128) if ta else (512, 256, 128))
    tn = _tile(n, TILES)
    tk = _tile(k, TILES)
    if inter:
        segs, bw = inter
        if tb:
            tk = bw
        else:
            tn = bw
        tps = ((k if tb else n) // bw) // segs
        col = lambda q: ((q % segs) * tps + q // segs) * bw
    else:
        col = lambda q: q * (tk if tb else tn)
    def vmem(tm_, tk_):
        out_b = jnp.dtype(out_dtype).itemsize * 2 + (8 if add is not None else 0) + 4
        return 2 * tk_ * (tm_ * a.dtype.itemsize + tn * b.dtype.itemsize) + tm_ * tn * out_b

    if ta:
        while tk * 2 <= k and k % (tk * 2) == 0 and tm * tn * tk < MIN_STEP_WORK and vmem(tm, tk * 2) < VMEM_LIMIT * 3 // 4:
            tk *= 2
    else:
        while tm * 2 <= m and m % (tm * 2) == 0 and tm * tn * tk < MIN_STEP_WORK and vmem(tm * 2, tk) < VMEM_LIMIT * 3 // 4:
            tm *= 2
    nk = k // tk
    a_spec = pl.BlockSpec((tk, tm), lambda i, j, q: (q, i)) if ta else pl.BlockSpec((tm, tk), lambda i, j, q: (i, q))
    if b_layer is None:
        b_spec = pl.BlockSpec((tn, tk), lambda i, j, q: (j, q)) if tb else pl.BlockSpec((tk, tn), lambda i, j, q: (q, j))
    elif tb:
        b_spec = pl.BlockSpec((None, None, tn, tk), lambda i, j, q: (col(q) // ns, b_layer, j, (col(q) % ns) // tk))
    else:
        b_spec = pl.BlockSpec((None, None, tk, tn), lambda i, j, q: (col(j) // ns, b_layer, q, (col(j) % ns) // tn))
    if out_shard:
        assert ta and add is None
        o_spec = pl.BlockSpec((None, tm, tn), lambda i, j, q: (col(j) // out_shard, i, (col(j) % out_shard) // tn))
        o_shape = jax.ShapeDtypeStruct((N_CHIPS, m, out_shard), out_dtype)
    else:
        o_spec = pl.BlockSpec((tm, tn), lambda i, j, q: (i, j))
        o_shape = jax.ShapeDtypeStruct((m, n), out_dtype)
    dims = (((0 if ta else 1,), (1 if tb else 0,)), ((), ()))
    has_add = add is not None

    def body(*refs):
        a_ref, b_ref = refs[0], refs[1]
        o_ref = refs[2 + has_add]
        p = lax.dot_general(a_ref[...].astype(MM_DTYPE), b_ref[...].astype(MM_DTYPE), dims, preferred_element_type=F32)

        def finish(acc):
            if has_add:
                acc = acc + refs[2][...].astype(F32)
            o_ref[...] = acc.astype(out_dtype)

        if nk == 1:
            finish(p)
        else:
            acc_ref = refs[3 + has_add]
            q = pl.program_id(2)

            @pl.when(q == 0)
            def _():
                acc_ref[...] = p

            @pl.when(q > 0)
            def _():
                acc_ref[...] += p

            @pl.when(q == nk - 1)
            def _():
                finish(acc_ref[...])

    args = [a, b] + ([add] if has_add else [])
    in_specs = [a_spec, b_spec] + ([o_spec] if has_add else [])
    return pl.pallas_call(
        body, name=name, grid=(m // tm, n // tn, nk), in_specs=in_specs, out_specs=o_spec, out_shape=o_shape,
        scratch_shapes=[pltpu.VMEM((tm, tn), F32)] if nk > 1 else [],
        compiler_params=_params(("parallel", "parallel", "arbitrary")),
    )(*args)


def _rows(fn, *, name, s, tm, ncol=1, ins, outs, accs=()):
    nrt = s // tm
    hb = tm // SUBLANES
    in_specs, args = [], []
    for spec in ins:
        kind, arr = spec[0], spec[1]
        if kind == "full":
            in_specs.append(pl.BlockSpec(arr.shape, lambda j, i: (0, 0)))
        elif kind == "col":
            _, _, bw, cmap = spec
            in_specs.append(pl.BlockSpec((arr.shape[0], bw), lambda j, i, cmap=cmap: (0, cmap(j))))
        elif kind == "row":
            _, _, bw, cmap = spec
            in_specs.append(pl.BlockSpec((tm, bw), lambda j, i, cmap=cmap: (i, cmap(j))))
        elif kind == "prev":
            _, _, bw, cmap = spec
            in_specs.append(pl.BlockSpec((SUBLANES, bw), lambda j, i, cmap=cmap: (jnp.maximum(i * hb - 1, 0), cmap(j))))
        elif kind == "next":
            _, _, bw, cmap = spec
            in_specs.append(pl.BlockSpec((SUBLANES, bw), lambda j, i, cmap=cmap: (jnp.minimum((i + 1) * hb, s // SUBLANES - 1), cmap(j))))
        else:
            raise ValueError(kind)
        args.append(arr)
    out_specs, out_shape = [], []
    for w, bw, cmap, dt in outs:
        out_specs.append(pl.BlockSpec((tm, bw), lambda j, i, cmap=cmap: (i, cmap(j))))
        out_shape.append(jax.ShapeDtypeStruct((s, w), dt))
    for r, w, bw, cmap in accs:
        out_specs.append(pl.BlockSpec((r, bw), lambda j, i, cmap=cmap: (0, cmap(j))))
        out_shape.append(jax.ShapeDtypeStruct((r, w), F32))
    n_in, n_out, n_acc = len(ins), len(outs), len(accs)

    def body(*refs):
        i = pl.program_id(1)
        vals = [r[...] for r in refs[:n_in]]
        o_vals, a_vals = fn(i, nrt, *vals)
        assert len(o_vals) == n_out and len(a_vals) == n_acc
        for r, v in zip(refs[n_in:n_in + n_out], o_vals):
            r[...] = v.astype(r.dtype)
        for r, v in zip(refs[n_in + n_out:], a_vals):
            @pl.when(i == 0)
            def _(r=r, v=v):
                r[...] = v.astype(F32)

            @pl.when(i > 0)
            def _(r=r, v=v):
                r[...] += v.astype(F32)

    res = pl.pallas_call(
        body, name=name, grid=(ncol, nrt), in_specs=in_specs, out_specs=out_specs, out_shape=out_shape,
        compiler_params=_params(("parallel", "arbitrary" if accs else "parallel")),
    )(*args)
    return res


def _c0(j):
    return 0


def _cj(j):
    return j


def _gmean(v, gs):
    w = v.shape[-1]
    tile = max(gs, LANES)
    r = lax.broadcasted_iota(jnp.int32, (tile, tile), 0) // gs
    c = lax.broadcasted_iota(jnp.int32, (tile, tile), 1) // gs
    g = jnp.where(r == c, 1.0 / gs, 0.0).astype(F32)
    parts = [jnp.dot(v[:, t * tile:(t + 1) * tile], g, precision=HI, preferred_element_type=F32) for t in range(w // tile)]
    return parts[0] if len(parts) == 1 else jnp.concatenate(parts, axis=1)


def _sum_rows(v):
    return jnp.sum(v, axis=0, keepdims=True)


def _resid_rms(x, y, w, *, name):
    s, d = x.shape
    has_y = y is not None

    def fn(i, nrt, *v):
        xv = v[0] + (v[1] if has_y else 0.0)
        wv = v[-1]
        r = lax.rsqrt(jnp.mean(xv * xv, axis=-1, keepdims=True) + RMS_EPS)
        return (xv, xv * r * wv), ()

    ins = [("row", x, d, _c0)] + ([("row", y, d, _c0)] if has_y else []) + [("full", w.reshape(1, d))]
    xn, h = _rows(fn, name=name, s=s, tm=_tile(s, (512, 256, 128)), ins=ins, outs=[(d, d, _c0, F32), (d, d, _c0, ACT_DTYPE)])
    return xn, h


def _rms_bwd(x, w, dh, dx_in, *, name):
    s, d = x.shape

    def fn(i, nrt, xv, wv, dhv, dxi):
        r = lax.rsqrt(jnp.mean(xv * xv, axis=-1, keepdims=True) + RMS_EPS)
        xh = xv * r
        g = dhv * wv
        dx = dxi + r * (g - xh * jnp.mean(g * xh, axis=-1, keepdims=True))
        return (dx, dx), (_sum_rows(dhv * xh),)

    dx, dxb, dw = _rows(fn, name=name, s=s, tm=_tile(s, (512, 256, 128)),
                        ins=[("row", x, d, _c0), ("full", w.reshape(1, d)), ("row", dh, d, _c0), ("row", dx_in, d, _c0)],
                        outs=[(d, d, _c0, F32), (d, d, _c0, MM_DTYPE)], accs=[(1, d, d, _c0)])
    return dx, dxb, dw.reshape(d)


def _swiglu_fwd(p, bw, *, name):
    s, f = p.shape[0], p.shape[1] // 2

    def fn(i, nrt, pv):
        gv, uv = pv[:, :bw].astype(F32), pv[:, bw:].astype(F32)
        return (gv * _sigmoid(gv) * uv,), ()

    (a,) = _rows(fn, name=name, s=s, tm=_tile(s, (512, 256, 128)), ncol=f // bw,
                 ins=[("row", p, 2 * bw, _cj)], outs=[(f, bw, _cj, ACT_DTYPE)])
    return a


def _swiglu_bwd(p, da, bw, *, name):
    s, f = da.shape

    def fn(i, nrt, pv, dav):
        gv, uv, dav = pv[:, :bw].astype(F32), pv[:, bw:].astype(F32), dav.astype(F32)
        sg = _sigmoid(gv)
        dg = dav * uv * sg * (1.0 + gv * (1.0 - sg))
        du = dav * gv * sg
        return (jnp.concatenate([dg, du], axis=1),), ()

    (dp,) = _rows(fn, name=name, s=s, tm=_tile(s, (512, 256, 128)), ncol=f // bw,
                  ins=[("row", p, 2 * bw, _cj), ("row", da, bw, _cj)], outs=[(2 * f, 2 * bw, _cj, ACT_DTYPE)])
    return dp


def _loss_head(x, y, tgt, *, name):
    s, d = x.shape

    def fn(i, nrt, xv, yv, tv):
        diff = xv + yv - tv
        part = 0.5 * jnp.sum(diff * diff) / d
        return (diff / d, diff / d), (jnp.full((1, LANES), part, F32),)

    dy, dyb, loss = _rows(fn, name=name, s=s, tm=_tile(s, (512, 256, 128)),
                          ins=[("row", x, d, _c0), ("row", y, d, _c0), ("row", tgt, d, _c0)],
                          outs=[(d, d, _c0, F32), (d, d, _c0, MM_DTYPE)], accs=[(1, LANES, LANES, _c0)])
    return loss[0, 0], dy, dyb


def _shift_down(ext, j, tm):
    src = pltpu.roll(ext, j, 0) if j else ext
    return src[SUBLANES:SUBLANES + tm]


def _shift_up(ext, j, tm):
    return ext[:tm] if j == 0 else pltpu.roll(ext, ext.shape[0] - j, 0)[:tm]


def _gconv_fwd(p, w, bw, *, name):
    s, d = p.shape[0], p.shape[1] // 3
    kw = w.shape[0]
    tm = _tile(s, (512, 256, 128))

    def fn(i, nrt, pv, pp, wv):
        pv, pp = pv.astype(F32), pp.astype(F32)
        cv = pv[:, bw:2 * bw] * pv[:, 2 * bw:]
        pcv = jnp.where(i == 0, 0.0, pp[:, bw:2 * bw] * pp[:, 2 * bw:])
        ext = jnp.concatenate([pcv, cv], axis=0)
        u = sum(wv[k:k + 1, :] * _shift_down(ext, kw - 1 - k, tm) for k in range(kw))
        return (pv[:, :bw] * u,), ()

    (o,) = _rows(fn, name=name, s=s, tm=tm, ncol=d // bw, ins=[("row", p, 3 * bw, _cj), ("prev", p, 3 * bw, _cj), ("col", w, bw, _cj)],
                 outs=[(d, bw, _cj, ACT_DTYPE)])
    return o


def _gconv_bwd(p, w, do, bw, *, name):
    s, d = do.shape
    kw = w.shape[0]
    tm = _tile(s, (512, 256, 128))

    def fn(i, nrt, pv, pp, pn, dov, ndo, wv):
        pv, pp, dov = pv.astype(F32), pp.astype(F32), dov.astype(F32)
        bv, cv_, vv = pv[:, :bw], pv[:, bw:2 * bw], pv[:, 2 * bw:]
        cv = cv_ * vv
        pcv = jnp.where(i == 0, 0.0, pp[:, bw:2 * bw] * pp[:, 2 * bw:])
        ext = jnp.concatenate([pcv, cv], axis=0)
        shifted = [_shift_down(ext, kw - 1 - k, tm) for k in range(kw)]
        u = sum(wv[k:k + 1, :] * shifted[k] for k in range(kw))
        db = dov * u
        du = dov * bv
        ndu = jnp.where(i == nrt - 1, 0.0, ndo.astype(F32) * pn[:, :bw].astype(F32))
        ext2 = jnp.concatenate([du, ndu], axis=0)
        dcv = sum(wv[k:k + 1, :] * _shift_up(ext2, kw - 1 - k, tm) for k in range(kw))
        dw = jnp.concatenate([_sum_rows(du * shifted[k]) for k in range(kw)], axis=0)
        return (jnp.concatenate([db, dcv * vv, dcv * cv_], axis=1),), (dw,)

    dp, dw = _rows(fn, name=name, s=s, tm=tm, ncol=d // bw,
                   ins=[("row", p, 3 * bw, _cj), ("prev", p, 3 * bw, _cj), ("next", p, 3 * bw, _cj), ("row", do, bw, _cj),
                        ("next", do, bw, _cj), ("col", w, bw, _cj)],
                   outs=[(3 * d, 3 * bw, _cj, ACT_DTYPE)], accs=[(kw, d, bw, _cj)])
    return dp, dw


def _sconv_fwd(x, w, bias, *, name):
    s, d = x.shape
    kw = w.shape[0]
    bw = _tile(d, (512, 256, 128))
    tm = _tile(s, (512, 256, 128))

    def fn(i, nrt, xv, px, wv, bsv):
        xv = xv.astype(F32)
        ext = jnp.concatenate([jnp.where(i == 0, 0.0, px.astype(F32)), xv], axis=0)
        pre = sum(wv[k:k + 1, :] * _shift_down(ext, kw - 1 - k, tm) for k in range(kw)) + bsv
        return (pre * _sigmoid(pre),), ()

    (o,) = _rows(fn, name=name, s=s, tm=tm, ncol=d // bw,
                 ins=[("row", x, bw, _cj), ("prev", x, bw, _cj), ("col", w, bw, _cj), ("col", bias.reshape(1, d), bw, _cj)],
                 outs=[(d, bw, _cj, ACT_DTYPE)])
    return o


def _sconv_bwd(x, w, bias, dact, *, name):
    s, d = x.shape
    kw = w.shape[0]
    bw = _tile(d, (512, 256, 128))
    tm = _tile(s, (512, 256, 128))

    def fn(i, nrt, xv, px, nx, dav, nda, wv, bsv):
        xv = xv.astype(F32)
        ext = jnp.concatenate([jnp.where(i == 0, 0.0, px.astype(F32)), xv, nx.astype(F32)], axis=0)
        rows_e = tm + SUBLANES
        pre_e = sum(wv[k:k + 1, :] * _shift_down(ext, kw - 1 - k, rows_e) for k in range(kw)) + bsv
        da_e = jnp.concatenate([dav.astype(F32), jnp.where(i == nrt - 1, 0.0, nda.astype(F32))], axis=0)
        sg = _sigmoid(pre_e)
        dpre_e = da_e * sg * (1.0 + pre_e * (1.0 - sg))
        dx = sum(wv[k:k + 1, :] * _shift_up(dpre_e, kw - 1 - k, tm) for k in range(kw))
        dpre = dpre_e[:tm]
        dw = jnp.concatenate([_sum_rows(dpre * _shift_down(ext, kw - 1 - k, tm)) for k in range(kw)], axis=0)
        return (dx,), (dw, _sum_rows(dpre))

    dx, dw, db = _rows(fn, name=name, s=s, tm=tm, ncol=d // bw,
                       ins=[("row", x, bw, _cj), ("prev", x, bw, _cj), ("next", x, bw, _cj), ("row", dact, bw, _cj),
                            ("next", dact, bw, _cj), ("col", w, bw, _cj), ("col", bias.reshape(1, d), bw, _cj)],
                       outs=[(d, bw, _cj, ACT_DTYPE)], accs=[(kw, d, bw, _cj), (1, d, bw, _cj)])
    return dx, dw, db.reshape(d)


def _tri(n, reverse):
    r = lax.broadcasted_iota(jnp.int32, (n, n), 0)
    c = lax.broadcasted_iota(jnp.int32, (n, n), 1)
    return jnp.where((c >= r) if reverse else (c <= r), 1.0, 0.0).astype(F32)


def _cumsum_rows(x, *, reverse, name):
    s, w = x.shape
    ch = _tile(s, (256, 128))
    n = s // ch

    def body(x_ref, o_ref, carry):
        i = pl.program_id(0)

        @pl.when(i == 0)
        def _():
            carry[...] = jnp.zeros_like(carry)

        out = jnp.dot(_tri(ch, reverse), x_ref[...], precision=HI, preferred_element_type=F32) + carry[...]
        o_ref[...] = out
        carry[...] = out[0:1, :] if reverse else out[ch - 1:ch, :]

    imap = (lambda i: (n - 1 - i, 0)) if reverse else (lambda i: (i, 0))
    return pl.pallas_call(
        body, name=name, grid=(n,), in_specs=[pl.BlockSpec((ch, w), imap)], out_specs=pl.BlockSpec((ch, w), imap),
        out_shape=jax.ShapeDtypeStruct((s, w), F32), scratch_shapes=[pltpu.VMEM((1, w), F32)],
        compiler_params=_params(("arbitrary",)),
    )(x)


def _fox_prep(q, k, f, gq, gk, bf, *, name):
    s, d = q.shape
    scale = HEAD ** -0.5

    def fn(i, nrt, qv, kv, fv, gqv, gkv, bfv):
        qv, kv = qv.astype(F32), kv.astype(F32)
        qn = qv * lax.rsqrt(_gmean(qv * qv, HEAD) + RMS_EPS) * gqv * scale
        kn = kv * lax.rsqrt(_gmean(kv * kv, HEAD) + RMS_EPS) * gkv
        z = fv + bfv
        logf = jnp.minimum(z, 0.0) - jnp.log(1.0 + jnp.exp(-jnp.abs(z)))
        return (qn, kn, logf), ()

    return _rows(fn, name=name, s=s, tm=_tile(s, (512, 256, 128)),
                 ins=[("row", q, d, _c0), ("row", k, d, _c0), ("row", f, LANES, _c0), ("full", gq), ("full", gk), ("full", bf)],
                 outs=[(d, d, _c0, ACT_DTYPE), (d, d, _c0, ACT_DTYPE), (LANES, LANES, _c0, F32)])


def _fox_prep_bwd(q, k, f, gq, gk, bf, dqs, dkn, dlogf, *, name):
    s, d = q.shape
    scale = HEAD ** -0.5

    def fn(i, nrt, qv, kv, fv, gqv, gkv, bfv, dqv, dkv, dlf):
        outs, accs = [], []
        for xv, gv, dv, sc in ((qv, gqv, dqv, scale), (kv, gkv, dkv, 1.0)):
            xv, dv = xv.astype(F32), dv.astype(F32) * sc
            r = lax.rsqrt(_gmean(xv * xv, HEAD) + RMS_EPS)
            xh = xv * r
            g = dv * gv
            outs.append(r * (g - xh * _gmean(g * xh, HEAD)))
            accs.append(_sum_rows(dv * xh))
        z = fv + bfv
        df = dlf * _sigmoid(-z)
        outs.append(df)
        accs.append(_sum_rows(df))
        return outs, accs

    return _rows(fn, name=name, s=s, tm=_tile(s, (512, 256, 128)),
                 ins=[("row", q, d, _c0), ("row", k, d, _c0), ("row", f, LANES, _c0), ("full", gq), ("full", gk), ("full", bf),
                      ("row", dqs, d, _c0), ("row", dkn, d, _c0), ("row", dlogf, LANES, _c0)],
                 outs=[(d, d, _c0, ACT_DTYPE), (d, d, _c0, ACT_DTYPE), (LANES, LANES, _c0, ACT_DTYPE)],
                 accs=[(1, d, d, _c0), (1, d, d, _c0), (1, LANES, LANES, _c0)])


def _head_masks(shape):
    lane = lax.broadcasted_iota(jnp.int32, shape, len(shape) - 1)
    return lane < HEAD, lane >= HEAD


def _pick_lane(blk, idx):
    lane = lax.broadcasted_iota(jnp.int32, blk.shape, 1)
    return jnp.sum(jnp.where(lane == idx, blk, 0.0), axis=1, keepdims=True)


def _pick_row(blk, idx):
    sub = lax.broadcasted_iota(jnp.int32, blk.shape, 0)
    return jnp.sum(jnp.where(sub == idx, blk, 0.0), axis=0, keepdims=True)


def _fox_aug(qs, kn, cum, *, name):
    s, d = qs.shape
    hp = d // LANES

    def fn(i, nrt, qv, kv, cv):
        lane = lax.broadcasted_iota(jnp.int32, (qv.shape[0], LANES), 1)
        outs = [[], [], [], []]
        for p in range(hp):
            qt, kt = qv[:, p * LANES:(p + 1) * LANES], kv[:, p * LANES:(p + 1) * LANES]
            for h in range(2):
                mine = (lane < HEAD) if h == 0 else (lane >= HEAD)
                a0 = HEAD if h == 0 else 0
                c = cv[:, 2 * p + h:2 * p + h + 1]
                hi = c.astype(ACT_DTYPE).astype(F32)
                mid = (c - hi).astype(ACT_DTYPE).astype(F32)
                lo = (c - hi - mid).astype(ACT_DTYPE).astype(F32)
                ones = jnp.where((lane >= a0) & (lane < a0 + 3), 1.0, 0.0)
                kx = jnp.where(lane == a0, -hi, jnp.where(lane == a0 + 1, -mid, jnp.where(lane == a0 + 2, -lo, 0.0)))
                outs[h].append(jnp.where(mine, qt.astype(F32), ones))
                outs[2 + h].append(jnp.where(mine, kt.astype(F32), kx))
        return [jnp.concatenate(o, axis=1) for o in outs], ()

    return _rows(fn, name=name, s=s, tm=_tile(s, (512, 256, 128)), ins=[("row", qs, d, _c0), ("row", kn, d, _c0), ("row", cum, LANES, _c0)],
                 outs=[(d, d, _c0, ACT_DTYPE)] * 4)


def _tri_tables(nq, by_key):
    import numpy as np
    pairs = [(qi, kj) for kj in range(nq) for qi in range(kj, nq)] if by_key else [(qi, kj) for qi in range(nq) for kj in range(qi + 1)]
    return jnp.asarray(np.array([p[0] for p in pairs], np.int32)), jnp.asarray(np.array([p[1] for p in pairs], np.int32))


def _nt(a, b):
    return lax.dot_general(a, b, (((1,), (1,)), ((), ())), preferred_element_type=F32)


def _tn(a, b):
    return lax.dot_general(a, b, (((0,), (0,)), ((), ())), preferred_element_type=F32)


ATTN_BLOCKS = (1024, 512, 256, 128)


def _fox_dd(do, o, *, name):
    s, d = do.shape

    def fn(i, nrt, dov, ov):
        return (_reduce_heads(dov.astype(F32) * ov.astype(F32), d // HEAD, HEAD),), ()

    (dd,) = _rows(fn, name=name, s=s, tm=_tile(s, (512, 256, 128)), ins=[("row", do, d, _c0), ("row", o, d, _c0)],
                  outs=[(LANES, LANES, _c0, F32)])
    return dd


def _pair_rows(a, nh):
    s = a.shape[0]
    t = a[:, :nh].T.reshape(nh // 2, 2, s)
    return jnp.pad(t, ((0, 0), (0, SUBLANES - 2), (0, 0)))


def _rows01(r0, r1):
    sub = lax.broadcasted_iota(jnp.int32, (SUBLANES, r0.shape[1]), 0)
    return jnp.where(sub == 0, r0, jnp.where(sub == 1, r1, 0.0))


def _fox_fwd_t(q_aug, k_aug, v, *, name):
    s, d = v.shape
    bq = _tile(s, ATTN_BLOCKS)
    nq = s // bq
    hp = d // LANES
    qtab, ktab = _tri_tables(nq, by_key=False)

    def body(qt, kt, q0_ref, q1_ref, k0_ref, k1_ref, v_ref, o_ref, lse_ref, m0, m1, l0, l1, acc0, acc1):
        t = pl.program_id(1)
        qi, kj = qt[t], kt[t]
        ms, ls, accs = (m0, m1), (l0, l1), (acc0, acc1)

        @pl.when(kj == 0)
        def _():
            for h in range(2):
                ms[h][...] = jnp.full_like(ms[h], -jnp.inf)
                ls[h][...] = jnp.zeros_like(ls[h])
                accs[h][...] = jnp.zeros_like(accs[h])

        def update(diagonal):
            v2 = v_ref[...]
            qk = ((q0_ref, k0_ref), (q1_ref, k1_ref))
            sts = [_nt(qk[h][1][...], qk[h][0][...]) for h in range(2)]
            if diagonal:
                sts = [_diag_mask_t(st) for st in sts]
            m_prev = [ms[h][...] for h in range(2)]
            m_new = [jnp.maximum(m_prev[h], jnp.max(sts[h], axis=0, keepdims=True)) for h in range(2)]
            ps = [jnp.exp(sts[h] - m_new[h]) for h in range(2)]
            alpha = [jnp.exp(m_prev[h] - m_new[h]) for h in range(2)]
            for h in range(2):
                ls[h][...] = alpha[h] * ls[h][...] + jnp.sum(ps[h], axis=0, keepdims=True)
                accs[h][...] = alpha[h] * accs[h][...] + _tn(v2, ps[h].astype(MM_DTYPE))
                ms[h][...] = m_new[h]

        @pl.when(kj < qi)
        def _():
            update(False)

        @pl.when(kj == qi)
        def _():
            update(True)
            row = lax.broadcasted_iota(jnp.int32, (LANES, bq), 0)
            ot = jnp.where(row < HEAD, acc0[...] / l0[...], acc1[...] / l1[...])
            o_ref[...] = ot.T.astype(o_ref.dtype)
            lse_ref[...] = _rows01(m0[...] + jnp.log(l0[...]), m1[...] + jnp.log(l1[...]))

    blk = (bq, LANES)
    qmap = lambda p_, t, qt, kt: (qt[t], p_)
    kmap = lambda p_, t, qt, kt: (kt[t], p_)
    grid_spec = pltpu.PrefetchScalarGridSpec(
        num_scalar_prefetch=2, grid=(hp, qtab.shape[0]),
        in_specs=[pl.BlockSpec(blk, qmap), pl.BlockSpec(blk, qmap), pl.BlockSpec(blk, kmap), pl.BlockSpec(blk, kmap), pl.BlockSpec(blk, kmap)],
        out_specs=[pl.BlockSpec(blk, qmap), pl.BlockSpec((None, SUBLANES, bq), lambda p_, t, qt, kt: (p_, 0, qt[t]))],
        scratch_shapes=[pltpu.VMEM((1, bq), F32)] * 4 + [pltpu.VMEM((LANES, bq), F32)] * 2)
    o, lse = pl.pallas_call(
        body, name=name, grid_spec=grid_spec,
        out_shape=[jax.ShapeDtypeStruct((s, d), ACT_DTYPE), jax.ShapeDtypeStruct((hp, SUBLANES, s), F32)],
        compiler_params=_params(("parallel", "arbitrary")),
    )(qtab, ktab, q_aug[0], q_aug[1], k_aug[0], k_aug[1], v)
    return o, lse


def _diag_mask_t(st):
    key = lax.broadcasted_iota(jnp.int32, st.shape, 0)
    qry = lax.broadcasted_iota(jnp.int32, st.shape, 1)
    return jnp.where(qry >= key, st, -jnp.inf)


def _fox_bwd_t(q_aug, k_aug, v, lse, dd, do, *, name):
    s, d = v.shape
    bq = _tile(s, ATTN_BLOCKS)
    nq = s // bq
    hp = d // LANES
    blk = (bq, LANES)
    qtab, ktab = _tri_tables(nq, by_key=True)
    n_steps = qtab.shape[0]

    def body(qt, kt, q0_ref, q1_ref, k0_ref, k1_ref, v_ref, lse_ref, dd_ref, do_ref,
             dq_ref, dk_ref, dv_ref, dcol_ref, drow_ref, dq_sc, rs_sc, dk_sc, dv_sc, cs_sc):
        t = pl.program_id(1)
        qi, kj = qt[t], kt[t]

        @pl.when(t == 0)
        def _():
            dq_sc[...] = jnp.zeros_like(dq_sc)
            rs_sc[...] = jnp.zeros_like(rs_sc)

        def update(diagonal):
            v2, do2 = v_ref[...], do_ref[...]
            masks = _head_masks(blk)
            row = lax.broadcasted_iota(jnp.int32, (LANES, bq), 0)
            off = pl.multiple_of(qi * bq, bq)
            for h, (q_ref, k_ref) in enumerate(((q0_ref, k0_ref), (q1_ref, k1_ref))):
                st = _nt(k_ref[...], q_ref[...])
                if diagonal:
                    st = _diag_mask_t(st)
                p = jnp.exp(st - lse_ref[h:h + 1, :])
                dp = _nt(v2, jnp.where(masks[h], do2, jnp.zeros_like(do2)))
                ds = p * (dp - dd_ref[h:h + 1, :])
                dsb = ds.astype(MM_DTYPE)
                dv_sc[h] += jnp.dot(p.astype(MM_DTYPE), do2, preferred_element_type=F32)
                dk_sc[h] += jnp.dot(dsb, q_ref[...], preferred_element_type=F32)
                cs_sc[h] += jnp.sum(ds, axis=1, keepdims=True)
                mine = (row < HEAD) if h == 0 else (row >= HEAD)
                dq_sc[:, pl.ds(off, bq)] += jnp.where(mine, _tn(k_ref[...], dsb), 0.0)
                rs_sc[h:h + 1, pl.ds(off, bq)] += jnp.sum(ds, axis=0, keepdims=True)

        @pl.when(qi == kj)
        def _():
            dk_sc[...] = jnp.zeros_like(dk_sc)
            dv_sc[...] = jnp.zeros_like(dv_sc)
            cs_sc[...] = jnp.zeros_like(cs_sc)
            update(True)

        @pl.when(qi > kj)
        def _():
            update(False)

        @pl.when(qi == nq - 1)
        def _():
            lo, _hi = _head_masks(blk)
            dk_ref[...] = jnp.where(lo, dk_sc[0], dk_sc[1]).astype(dk_ref.dtype)
            dv_ref[...] = jnp.where(lo, dv_sc[0], dv_sc[1]).astype(dv_ref.dtype)
            dcol_ref[...] = jnp.where(lo, cs_sc[0], cs_sc[1])

        @pl.when(t == n_steps - 1)
        def _():
            for c in range(nq):
                dq_ref[c * bq:(c + 1) * bq, :] = dq_sc[:, c * bq:(c + 1) * bq].T.astype(dq_ref.dtype)
            drow_ref[...] = rs_sc[...]

    qmap = lambda p_, t, qt, kt: (qt[t], p_)
    kmap = lambda p_, t, qt, kt: (kt[t], p_)
    rmap = lambda p_, t, qt, kt: (p_, 0, qt[t])
    return pl.pallas_call(
        body, name=name,
        grid_spec=pltpu.PrefetchScalarGridSpec(
            num_scalar_prefetch=2, grid=(hp, n_steps),
            in_specs=[pl.BlockSpec(blk, qmap), pl.BlockSpec(blk, qmap), pl.BlockSpec(blk, kmap), pl.BlockSpec(blk, kmap), pl.BlockSpec(blk, kmap),
                      pl.BlockSpec((None, SUBLANES, bq), rmap), pl.BlockSpec((None, SUBLANES, bq), rmap), pl.BlockSpec(blk, qmap)],
            out_specs=[pl.BlockSpec((s, LANES), lambda p_, t, qt, kt: (0, p_)), pl.BlockSpec(blk, kmap), pl.BlockSpec(blk, kmap),
                       pl.BlockSpec(blk, kmap), pl.BlockSpec((None, SUBLANES, s), lambda p_, t, qt, kt: (p_, 0, 0))],
            scratch_shapes=[pltpu.VMEM((LANES, s), F32), pltpu.VMEM((SUBLANES, s), F32), pltpu.VMEM((2, bq, LANES), F32),
                            pltpu.VMEM((2, bq, LANES), F32), pltpu.VMEM((2, bq, 1), F32)]),
        out_shape=[jax.ShapeDtypeStruct((s, d), ACT_DTYPE), jax.ShapeDtypeStruct((s, d), ACT_DTYPE), jax.ShapeDtypeStruct((s, d), ACT_DTYPE),
                   jax.ShapeDtypeStruct((s, d), F32), jax.ShapeDtypeStruct((hp, SUBLANES, s), F32)],
        compiler_params=_params(("parallel", "arbitrary")),
    )(qtab, ktab, q_aug[0], q_aug[1], k_aug[0], k_aug[1], v, lse, dd, do)


def _expand_heads(v, nh, hd):
    r = lax.broadcasted_iota(jnp.int32, (LANES, nh * hd), 0)
    c = lax.broadcasted_iota(jnp.int32, (LANES, nh * hd), 1) // hd
    e = jnp.where(r == c, 1.0, 0.0).astype(F32)
    return jnp.dot(v, e, precision=HI, preferred_element_type=F32)


def _reduce_heads(v, nh, hd):
    r = lax.broadcasted_iota(jnp.int32, (nh * hd, LANES), 0) // hd
    c = lax.broadcasted_iota(jnp.int32, (nh * hd, LANES), 1)
    e = jnp.where(r == c, 1.0, 0.0).astype(F32)
    return jnp.dot(v, e, precision=HI, preferred_element_type=F32)


def _ssd_prep(dt_raw, dt_bias, a_log, nh, *, name):
    s = dt_raw.shape[0]

    def fn(i, nrt, dtr, bsv, alv):
        dt = _softplus(dtr + bsv)
        acum = jnp.dot(_tri(SSM_CHUNK, False), dt * (-jnp.exp(alv)), precision=HI, preferred_element_type=F32)
        return (dt, acum, _expand_heads(dt, nh, HEAD), _expand_heads(acum, nh, HEAD)), ()

    w = nh * HEAD
    return _rows(fn, name=name, s=s, tm=SSM_CHUNK, ins=[("row", dt_raw, LANES, _c0), ("full", dt_bias), ("full", a_log)],
                 outs=[(LANES, LANES, _c0, F32), (LANES, LANES, _c0, F32), (w, w, _c0, F32), (w, w, _c0, F32)])


def _ssd_prep_bwd(dt_raw, dt_bias, a_log, ddtx, dacx, nh, *, name):
    s = dt_raw.shape[0]

    def fn(i, nrt, dtr, bsv, alv, ddx, dax):
        z = dtr + bsv
        dt = _softplus(z)
        a = -jnp.exp(alv)
        dda = jnp.dot(_tri(SSM_CHUNK, True), _reduce_heads(dax, nh, HEAD), precision=HI, preferred_element_type=F32)
        ddt = _reduce_heads(ddx, nh, HEAD) + dda * a
        dz = ddt * _sigmoid(z)
        lane = lax.broadcasted_iota(jnp.int32, dz.shape, 1)
        dz = jnp.where(lane < nh, dz, 0.0)
        return (dz,), (_sum_rows(dz), _sum_rows(dda * dt) * a)

    w = nh * HEAD
    return _rows(fn, name=name, s=s, tm=SSM_CHUNK,
                 ins=[("row", dt_raw, LANES, _c0), ("full", dt_bias), ("full", a_log), ("row", ddtx, w, _c0), ("row", dacx, w, _c0)],
                 outs=[(LANES, LANES, _c0, ACT_DTYPE)], accs=[(1, LANES, LANES, _c0), (1, LANES, LANES, _c0)])


def _ssd_decay(ac_blk, act_blk, head):
    col = _pick_lane(ac_blk, head)
    row = _pick_row(act_blk, head)
    r = lax.broadcasted_iota(jnp.int32, (SSM_CHUNK, SSM_CHUNK), 0)
    c = lax.broadcasted_iota(jnp.int32, (SSM_CHUNK, SSM_CHUNK), 1)
    return jnp.exp(jnp.where(r >= c, col - row, -jnp.inf))


def _group_masks(shape, hpg):
    lane = lax.broadcasted_iota(jnp.int32, shape, len(shape) - 1) // HEAD
    return [lane == k for k in range(hpg)]


def _ssd_scan_fwd(xs, bm, cm, dtx, acx, acum, acum_t, d_x, *, name):
    s, di = xs.shape
    ng = bm.shape[1] // SSM_STATE
    gw = di // ng
    hpg = gw // HEAD
    nc = s // SSM_CHUNK
    L = SSM_CHUNK
    nh_pad = acum_t.shape[0]

    def body(x_ref, b_ref, c_ref, dt_ref, ax_ref, ac_ref, act_ref, d_ref, y_ref, st_ref, state):
        g, c = pl.program_id(0), pl.program_id(1)

        @pl.when(c == 0)
        def _():
            state[...] = jnp.zeros_like(state)

        x4, bv, cv = x_ref[...].astype(F32), b_ref[...], c_ref[...]
        ax = ax_ref[...]
        tx = (x4 * dt_ref[...])
        cb = lax.dot_general(cv, bv, (((1,), (1,)), ((), ())), preferred_element_type=F32)
        masks = _group_masks((L, gw), hpg)
        y = jnp.zeros((L, gw), F32)
        txb = tx.astype(MM_DTYPE)
        for k in range(hpg):
            wk = (cb * _ssd_decay(ac_ref[...], act_ref[...], g * hpg + k)).astype(MM_DTYPE)
            y = y + jnp.where(masks[k], jnp.dot(wk, txb, preferred_element_type=F32), 0.0)
        prev = state[...]
        st_ref[...] = prev
        y = y + jnp.dot(cv, prev.astype(MM_DTYPE), preferred_element_type=F32) * jnp.exp(ax)
        y = y + d_ref[...] * x4
        y_ref[...] = y.astype(y_ref.dtype)
        a_last = ax[L - 1:L, :]
        sx = (tx * jnp.exp(a_last - ax)).astype(MM_DTYPE)
        state[...] = prev * jnp.exp(a_last) + lax.dot_general(bv, sx, (((0,), (0,)), ((), ())), preferred_element_type=F32)

    y, states = pl.pallas_call(
        body, name=name, grid=(ng, nc),
        in_specs=[pl.BlockSpec((L, gw), lambda g, c: (c, g)), pl.BlockSpec((L, SSM_STATE), lambda g, c: (c, g)),
                  pl.BlockSpec((L, SSM_STATE), lambda g, c: (c, g)), pl.BlockSpec((L, gw), lambda g, c: (c, g)),
                  pl.BlockSpec((L, gw), lambda g, c: (c, g)), pl.BlockSpec((L, LANES), lambda g, c: (c, 0)),
                  pl.BlockSpec((nh_pad, L), lambda g, c: (0, c)), pl.BlockSpec((1, gw), lambda g, c: (0, g))],
        out_specs=[pl.BlockSpec((L, gw), lambda g, c: (c, g)), pl.BlockSpec((None, None, SSM_STATE, gw), lambda g, c: (g, c, 0, 0))],
        out_shape=[jax.ShapeDtypeStruct((s, di), ACT_DTYPE), jax.ShapeDtypeStruct((ng, nc, SSM_STATE, gw), F32)],
        scratch_shapes=[pltpu.VMEM((SSM_STATE, gw), F32)],
        compiler_params=_params(("parallel", "arbitrary")),
    )(xs, bm, cm, dtx, acx, acum, acum_t, d_x)
    return y, states


def _ssd_scan_bwd(xs, bm, cm, dtx, acx, acum, acum_t, d_x, states, dy, *, name):
    s, di = xs.shape
    ng = bm.shape[1] // SSM_STATE
    gw = di // ng
    hpg = gw // HEAD
    nc = s // SSM_CHUNK
    L = SSM_CHUNK
    nh_pad = acum_t.shape[0]

    def body(x_ref, b_ref, c_ref, dt_ref, ax_ref, ac_ref, act_ref, d_ref, st_ref, dy_ref,
             dx_ref, db_ref, dc_ref, ddt_ref, dax_ref, dd_ref, dstate):
        g, cc = pl.program_id(0), pl.program_id(1)

        @pl.when(cc == 0)
        def _():
            dstate[...] = jnp.zeros_like(dstate)
            dd_ref[...] = jnp.zeros_like(dd_ref)

        x4, bv, cv = x_ref[...].astype(F32), b_ref[...], c_ref[...]
        tv, ax, dyv = dt_ref[...], ax_ref[...], dy_ref[...].astype(F32)
        prev, dn = st_ref[...], dstate[...]
        dnb = dn.astype(MM_DTYPE)
        masks = _group_masks((L, gw), hpg)
        tx = x4 * tv
        txb = tx.astype(MM_DTYPE)
        e_ax = jnp.exp(ax)
        a_last = ax[L - 1:L, :]
        e_last = jnp.exp(a_last)
        ed = jnp.exp(a_last - ax)

        dx = d_ref[...] * dyv
        dd_ref[...] += _sum_rows(dyv * x4)
        dye = (dyv * e_ax).astype(MM_DTYPE)
        yo = jnp.dot(cv, prev.astype(MM_DTYPE), preferred_element_type=F32) * e_ax
        dc = lax.dot_general(dye, prev.astype(MM_DTYPE), (((1,), (1,)), ((), ())), preferred_element_type=F32)
        dprev = lax.dot_general(cv, dye, (((0,), (0,)), ((), ())), preferred_element_type=F32)
        dax = dyv * yo
        sx = tx * ed
        dsx = jnp.dot(bv, dnb, preferred_element_type=F32)
        db = lax.dot_general(sx.astype(MM_DTYPE), dnb, (((1,), (1,)), ((), ())), preferred_element_type=F32)
        dtx_ = dsx * ed
        dsx_sx = dsx * sx
        dax = dax - dsx_sx
        dlast = _sum_rows(dsx_sx) + _sum_rows(dn * prev) * e_last
        dprev = dprev + dn * e_last
        cb = lax.dot_general(cv, bv, (((1,), (1,)), ((), ())), preferred_element_type=F32)
        dcb = jnp.zeros((L, L), F32)
        lane = lax.broadcasted_iota(jnp.int32, (L, gw), 1)
        for k in range(hpg):
            dec = _ssd_decay(ac_ref[...], act_ref[...], g * hpg + k)
            wk = (cb * dec).astype(MM_DTYPE)
            dyk = jnp.where(masks[k], dyv, 0.0).astype(MM_DTYPE)
            dtx_ = dtx_ + jnp.where(masks[k], lax.dot_general(wk, dyk, (((0,), (0,)), ((), ())), preferred_element_type=F32), 0.0)
            dwk = lax.dot_general(dyk, txb, (((1,), (1,)), ((), ())), preferred_element_type=F32)
            dcb = dcb + dwk * dec
            mk = dwk * cb * dec
            da_k = jnp.sum(mk, axis=1, keepdims=True) - jnp.sum(mk.T, axis=1, keepdims=True)
            dax = dax + jnp.where(lane == k * HEAD, da_k, 0.0)
        dcbb = dcb.astype(MM_DTYPE)
        dc = dc + jnp.dot(dcbb, bv, preferred_element_type=F32)
        db = db + lax.dot_general(dcbb, cv, (((0,), (0,)), ((), ())), preferred_element_type=F32)
        sub = lax.broadcasted_iota(jnp.int32, (L, gw), 0)
        dax = dax + jnp.where(sub == L - 1, dlast, 0.0)
        dx_ref[...] = (dx + dtx_ * tv).astype(dx_ref.dtype)
        ddt_ref[...] = dtx_ * x4
        dax_ref[...] = dax
        db_ref[...] = db.astype(db_ref.dtype)
        dc_ref[...] = dc.astype(dc_ref.dtype)
        dstate[...] = dprev

    rev = lambda g, c: (nc - 1 - c, g)
    rev0 = lambda g, c: (nc - 1 - c, 0)
    outs = pl.pallas_call(
        body, name=name, grid=(ng, nc),
        in_specs=[pl.BlockSpec((L, gw), rev), pl.BlockSpec((L, SSM_STATE), rev), pl.BlockSpec((L, SSM_STATE), rev),
                  pl.BlockSpec((L, gw), rev), pl.BlockSpec((L, gw), rev), pl.BlockSpec((L, LANES), rev0),
                  pl.BlockSpec((nh_pad, L), lambda g, c: (0, nc - 1 - c)), pl.BlockSpec((1, gw), lambda g, c: (0, g)),
                  pl.BlockSpec((None, None, SSM_STATE, gw), lambda g, c: (g, nc - 1 - c, 0, 0)), pl.BlockSpec((L, gw), rev)],
        out_specs=[pl.BlockSpec((L, gw), rev), pl.BlockSpec((L, SSM_STATE), rev), pl.BlockSpec((L, SSM_STATE), rev),
                   pl.BlockSpec((L, gw), rev), pl.BlockSpec((L, gw), rev), pl.BlockSpec((1, gw), lambda g, c: (0, g))],
        out_shape=[jax.ShapeDtypeStruct((s, di), ACT_DTYPE), jax.ShapeDtypeStruct(bm.shape, ACT_DTYPE), jax.ShapeDtypeStruct(cm.shape, ACT_DTYPE),
                   jax.ShapeDtypeStruct((s, di), F32), jax.ShapeDtypeStruct((s, di), F32), jax.ShapeDtypeStruct((1, di), F32)],
        scratch_shapes=[pltpu.VMEM((SSM_STATE, gw), F32)],
        compiler_params=_params(("parallel", "arbitrary")),
    )(xs, bm, cm, dtx, acx, acum, acum_t, d_x, states, dy)
    return outs


def _ssd_gate(y, z, w, gs, *, name):
    s, d = y.shape

    def fn(i, nrt, yv, zv, wv):
        zv = zv.astype(F32)
        u = yv.astype(F32) * zv * _sigmoid(zv)
        return (u * lax.rsqrt(_gmean(u * u, gs) + RMS_EPS) * wv,), ()

    (o,) = _rows(fn, name=name, s=s, tm=_tile(s, (256, 128)), ins=[("row", y, d, _c0), ("row", z, d, _c0), ("full", w.reshape(1, d))],
                 outs=[(d, d, _c0, ACT_DTYPE)])
    return o


def _ssd_gate_bwd(y, z, w, do, gs, *, name):
    s, d = y.shape

    def fn(i, nrt, yv, zv, wv, dov):
        yv, zv, dov = yv.astype(F32), zv.astype(F32), dov.astype(F32)
        sg = _sigmoid(zv)
        sl = zv * sg
        u = yv * sl
        r = lax.rsqrt(_gmean(u * u, gs) + RMS_EPS)
        uh = u * r
        g = dov * wv
        du = r * (g - uh * _gmean(g * uh, gs))
        return (du * sl, du * yv * sg * (1.0 + zv * (1.0 - sg))), (_sum_rows(dov * uh),)

    dy, dz, dw = _rows(fn, name=name, s=s, tm=_tile(s, (256, 128)),
                       ins=[("row", y, d, _c0), ("row", z, d, _c0), ("full", w.reshape(1, d)), ("row", do, d, _c0)],
                       outs=[(d, d, _c0, ACT_DTYPE), (d, d, _c0, ACT_DTYPE)], accs=[(1, d, d, _c0)])
    return dy, dz, dw.reshape(d)


def _pad_lanes(w):
    return jnp.pad(w, ((0, 0), (0, LANES - w.shape[1])))


def _nt_sum(pairs, name):
    acc = None
    for a, b in pairs:
        acc = _mm(a, b, tb=True, add=acc, name=name)
    return acc


def _conv_mixer_fwd(h, w_in, layer, w_dw, tag):
    d, ns = h.shape[1], w_in.shape[3]
    inter = (3, _tile(math.gcd(d, ns), TILES))
    p = _mm(h, w_in, b_layer=layer, inter=inter, out_dtype=ACT_DTYPE, name=f"{tag}_in")
    return _gconv_fwd(p, w_dw, inter[1], name=f"{tag}_gate"), (h, w_in, layer, inter, p, w_dw)


def _conv_mixer_bwd(cache, do, tag):
    h, w_in, layer, inter, p, w_dw = cache
    dp, dw_dw = _gconv_bwd(p, w_dw, do, inter[1], name=f"{tag}_gate_bwd")
    dw_in = _mm(h, dp, ta=True, out_shard=w_in.shape[3], inter=inter, out_dtype=WIRE_DTYPE, name=f"{tag}_dw_in")
    dh = _mm(dp, w_in, tb=True, b_layer=layer, inter=inter, name=f"{tag}_dh")
    return dh, {"w_in": dw_in, "w_dw": dw_dw}


def _fox_mixer_fwd(h, w_in, b_f, q_gain, k_gain, tag):
    d = h.shape[1]
    nh = d // HEAD
    ws = [w_in[:, k * d:(k + 1) * d] for k in range(3)] + [_pad_lanes(w_in[:, 3 * d:])]
    q, k, v = [_mm(h, w, out_dtype=ACT_DTYPE, name=f"{tag}_in") for w in ws[:3]]
    f = _mm(h, ws[3], name=f"{tag}_in_f")
    gq = jnp.tile(q_gain, nh).reshape(1, d)
    gk = jnp.tile(k_gain, nh).reshape(1, d)
    bf = _pad_lanes(b_f.reshape(1, nh))
    qs, kn, logf = _fox_prep(q, k, f, gq, gk, bf, name=f"{tag}_prep")
    cum = _cumsum_rows(logf, reverse=False, name=f"{tag}_cum")
    aug = _fox_aug(qs, kn, cum, name=f"{tag}_aug")
    q_aug, k_aug = aug[:2], aug[2:]
    o, lse = _fox_fwd_t(q_aug, k_aug, v, name=f"{tag}_attn")
    return o, (h, ws, q, k, v, f, gq, gk, bf, q_aug, k_aug, o, lse)


def _fox_mixer_bwd(cache, do, tag):
    h, ws, q, k, v, f, gq, gk, bf, q_aug, k_aug, o, lse = cache
    s, d = q.shape
    nh = d // HEAD
    dd = _pair_rows(_fox_dd(do, o, name=f"{tag}_attn_dd"), nh)
    dqs, dkn, dv, dcol, drow = _fox_bwd_t(q_aug, k_aug, v, lse, dd, do, name=f"{tag}_attn_bwd")
    dcum = _pad_lanes(drow[:, :2, :].reshape(nh, s).T - dcol[:, ::HEAD])
    dlogf = _cumsum_rows(dcum, reverse=True, name=f"{tag}_cum_bwd")
    dq, dk, df, dgq, dgk, dbf = _fox_prep_bwd(q, k, f, gq, gk, bf, dqs, dkn, dlogf, name=f"{tag}_prep_bwd")
    dps = (dq, dk, dv, df)
    dws = [_mm(h, dp, ta=True, name=f"{tag}_dw_in") for dp in dps]
    dw_in = jnp.concatenate(dws[:3] + [dws[3][:, :nh]], axis=1)
    dh = _nt_sum(list(zip(dps, ws)), f"{tag}_dh")
    return dh, {"w_in": dw_in, "b_f": dbf[0, :nh], "q_gain": dgq.reshape(nh, HEAD).sum(0), "k_gain": dgk.reshape(nh, HEAD).sum(0)}


def _ssd_mixer_fwd(h, w_in, conv_w, conv_b, dt_bias, a_log, d_skip, norm_w, tag):
    di = norm_w.shape[0]
    nh = di // HEAD
    gn = (conv_w.shape[1] - di) // 2
    cuts = [0, di, 2 * di, 2 * di + gn, 2 * di + 2 * gn]
    ws = [w_in[:, cuts[k]:cuts[k + 1]] for k in range(4)] + [_pad_lanes(w_in[:, cuts[4]:])]
    z, xr, br, cr = [_mm(h, w, out_dtype=ACT_DTYPE, name=f"{tag}_in") for w in ws[:4]]
    dtr = _mm(h, ws[4], name=f"{tag}_in_dt")
    ccuts = [0, di, di + gn, di + 2 * gn]
    cws = [conv_w[:, ccuts[k]:ccuts[k + 1]] for k in range(3)]
    cbs = [conv_b[ccuts[k]:ccuts[k + 1]] for k in range(3)]
    xs, bm, cm = [_sconv_fwd(r, w, b, name=f"{tag}_conv") for r, w, b in zip((xr, br, cr), cws, cbs)]
    dtb = _pad_lanes(dt_bias.reshape(1, nh))
    alg = _pad_lanes(a_log.reshape(1, nh))
    _dt, acum, dtx, acx = _ssd_prep(dtr, dtb, alg, nh, name=f"{tag}_prep")
    acum_t = acum[:, :nh].T
    d_x = jnp.repeat(d_skip, HEAD).reshape(1, di)
    y, states = _ssd_scan_fwd(xs, bm, cm, dtx, acx, acum, acum_t, d_x, name=f"{tag}_scan")
    gs = di // (gn // SSM_STATE)
    o = _ssd_gate(y, z, norm_w, gs, name=f"{tag}_gate")
    return o, (h, ws, z, (xr, br, cr), dtr, cws, cbs, xs, bm, cm, dtb, alg, dtx, acx, acum, acum_t, d_x, states, y, norm_w, gs, nh)


def _ssd_mixer_bwd(cache, do, tag):
    h, ws, z, raws, dtr, cws, cbs, xs, bm, cm, dtb, alg, dtx, acx, acum, acum_t, d_x, states, y, norm_w, gs, nh = cache
    dy, dz, dnorm = _ssd_gate_bwd(y, z, norm_w, do, gs, name=f"{tag}_gate_bwd")
    dxs, dbm, dcm, ddtx, dacx, dd_x = _ssd_scan_bwd(xs, bm, cm, dtx, acx, acum, acum_t, d_x, states, dy, name=f"{tag}_scan_bwd")
    ddtr, ddtb, dalg = _ssd_prep_bwd(dtr, dtb, alg, ddtx, dacx, nh, name=f"{tag}_prep_bwd")
    conv = [_sconv_bwd(r, w, b, da, name=f"{tag}_conv_bwd") for r, w, b, da in zip(raws, cws, cbs, (dxs, dbm, dcm))]
    dps = (dz, conv[0][0], conv[1][0], conv[2][0], ddtr)
    dws = [_mm(h, dp, ta=True, name=f"{tag}_dw_in") for dp in dps]
    dw_in = jnp.concatenate(dws[:4] + [dws[4][:, :nh]], axis=1)
    dh = _nt_sum(list(zip(dps, ws)), f"{tag}_dh")
    return dh, {"w_in": dw_in, "conv_w": jnp.concatenate([c[1] for c in conv], axis=1), "conv_b": jnp.concatenate([c[2] for c in conv]),
                "dt_bias": ddtb[0, :nh], "a_log": dalg[0, :nh], "d": dd_x.reshape(nh, HEAD).sum(1), "norm_w": dnorm}


def _rows_natural(cm, layer):
    return cm[:, layer].reshape(-1, cm.shape[3])


def _cols_natural(cm, layer):
    return jnp.moveaxis(cm[:, layer], 0, 1).reshape(cm.shape[2], -1)


def _cols_chip_major(g):
    return jnp.moveaxis(g.reshape(g.shape[0], N_CHIPS, -1), 1, 0).astype(WIRE_DTYPE)


def _local_step(x, tgt, fw, cm):
    depth = fw["mix_norm"].shape[0]
    layers = []
    xc, y_prev = x, None
    for i in range(depth):
        kind, j = i % 3, i // 3
        tag = f"l{i}"
        xin, h = _resid_rms(xc, y_prev, fw["mix_norm"][i], name=f"{tag}_norm1")
        if kind == 0:
            o, mc = _conv_mixer_fwd(h, cm["conv_w_in"], j, fw["conv_w_dw"][j], tag + "_conv")
            w_out = _rows_natural(cm["conv_w_out"], j)
        elif kind == 1:
            o, mc = _fox_mixer_fwd(h, _cols_natural(cm["fox_w_in"], j), fw["fox_b_f"][j], fw["fox_q_gain"][j], fw["fox_k_gain"][j], tag + "_fox")
            w_out = _rows_natural(cm["fox_w_out"], j)
        else:
            o, mc = _ssd_mixer_fwd(h, _cols_natural(cm["ssd_w_in"], j), fw["ssd_conv_w"][j], fw["ssd_conv_b"][j], fw["ssd_dt_bias"][j],
                                   fw["ssd_a_log"][j], fw["ssd_d"][j], fw["ssd_norm_w"][j], tag + "_ssd")
            w_out = _rows_natural(cm["ssd_w_out"], j)
        ym = _mm(o, w_out, name=f"{tag}_mix_out")
        x1, h2 = _resid_rms(xin, ym, fw["ffn_norm"][i], name=f"{tag}_norm2")
        w_gu, w_down = cm["ffn_w_gu"], _rows_natural(cm["ffn_w_down"], i)
        inter = (2, _tile(math.gcd(w_down.shape[0], w_gu.shape[3]), TILES))
        gu = _mm(h2, w_gu, b_layer=i, inter=inter, out_dtype=ACT_DTYPE, name=f"{tag}_ffn_gu")
        a = _swiglu_fwd(gu, inter[1], name=f"{tag}_swiglu")
        yf = _mm(a, w_down, name=f"{tag}_ffn_down")
        layers.append((xin, o, mc, w_out, x1, h2, w_down, inter, gu, a))
        xc, y_prev = x1, yf
    loss, dx, dxb = _loss_head(xc, y_prev, tgt, name="loss_head")

    names = ("conv", "fox", "ssd")
    small = {k: [None] * v.shape[0] for k, v in fw.items()}
    big = {k: [None] * v.shape[1] for k, v in cm.items()}
    for i in reversed(range(depth)):
        kind, j = i % 3, i // 3
        tag = f"l{i}"
        xin, o, mc, w_out, x1, h2, w_down, inter, gu, a = layers[i]
        big["ffn_w_down"][i] = _mm(a, dxb, ta=True, out_dtype=WIRE_DTYPE, name=f"{tag}_dw_down").reshape(N_CHIPS, -1, w_down.shape[1])
        da = _mm(dxb, w_down, tb=True, out_dtype=ACT_DTYPE, name=f"{tag}_da")
        dgu = _swiglu_bwd(gu, da, inter[1], name=f"{tag}_swiglu_bwd")
        big["ffn_w_gu"][i] = _mm(h2, dgu, ta=True, out_shard=cm["ffn_w_gu"].shape[3], inter=inter, out_dtype=WIRE_DTYPE, name=f"{tag}_dw_gu")
        dh2 = _mm(dgu, cm["ffn_w_gu"], tb=True, b_layer=i, inter=inter, name=f"{tag}_dh2")
        dx1, dx1b, small["ffn_norm"][i] = _rms_bwd(x1, fw["ffn_norm"][i], dh2, dx, name=f"{tag}_norm2_bwd")
        big[names[kind] + "_w_out"][j] = _mm(o, dx1b, ta=True, out_dtype=WIRE_DTYPE, name=f"{tag}_dw_out").reshape(N_CHIPS, -1, w_out.shape[1])
        do = _mm(dx1b, w_out, tb=True, out_dtype=ACT_DTYPE, name=f"{tag}_do")
        if kind == 0:
            dh, mg = _conv_mixer_bwd(mc, do, tag + "_conv")
        elif kind == 1:
            dh, mg = _fox_mixer_bwd(mc, do, tag + "_fox")
        else:
            dh, mg = _ssd_mixer_bwd(mc, do, tag + "_ssd")
        for k, v in mg.items():
            if k == "w_in":
                big[f"{names[kind]}_w_in"][j] = v if kind == 0 else _cols_chip_major(v)
            else:
                small[f"{names[kind]}_{k}"][j] = v
        dx, dxb, small["mix_norm"][i] = _rms_bwd(xin, fw["mix_norm"][i], dh, dx1, name=f"{tag}_norm1_bwd")
    return loss, dx, {k: jnp.stack(v) for k, v in small.items()}, big


ANY = pl.BlockSpec(memory_space=pl.ANY)
VMEM_SPEC = pl.BlockSpec(memory_space=pltpu.VMEM)


def _place():
    return lax.axis_index("x"), lax.axis_index("y"), lax.axis_index("c")


def _remote(src, dst, send_sems, recv_sems, k, to):
    return pltpu.make_async_remote_copy(src_ref=src, dst_ref=dst, send_sem=send_sems.at[k], recv_sem=recv_sems.at[k],
                                        device_id=to, device_id_type=MESH)


def _half_of(ref, h, shape):
    layers, rows, _ = shape
    if layers % 2 == 0:
        return ref.at[pl.ds(h * (layers // 2), layers // 2)]
    return ref.at[:, pl.ds(pl.multiple_of(h * (rows // 2), 16), rows // 2)]


def _row_half(ref, h, rows):
    return ref.at[:, pl.ds(pl.multiple_of(h * (rows // 2), 16), rows // 2)]


def _gather_list(ws, *, name):
    n = len(ws)

    def body(*refs):
        w_refs, o_refs, send_sems, recv_sems = refs[:n], refs[n:2 * n], refs[2 * n], refs[2 * n + 1]
        x, y, c = _place()
        me, sibling, m = (x, y, c), (x, y, 1 - c), 2 * x + y
        chips = [(1 - x, y), (x, 1 - y), (1 - x, 1 - y)]
        first, passed = [], []
        for p, (w_ref, o_ref) in enumerate(zip(w_refs, o_refs)):
            for j, (px, py) in enumerate(chips):
                cp = _remote(_half_of(w_ref, c, ws[p].shape), _half_of(o_ref.at[m], c, ws[p].shape), send_sems, recv_sems, 6 * p + j, (px, py, c))
                cp.start()
                first.append(cp)
        for p, o_ref in enumerate(o_refs):
            for j, (px, py) in enumerate(chips):
                blk = _half_of(o_ref.at[2 * px + py], c, ws[p].shape)
                _remote(blk, blk, send_sems, recv_sems, 6 * p + j, me).wait_recv()
                fwd = _remote(blk, blk, send_sems, recv_sems, 6 * p + 3 + j, sibling)
                fwd.start()
                passed.append(fwd)
        for p, o_ref in enumerate(o_refs):
            for j, (px, py) in enumerate(chips):
                blk = _half_of(o_ref.at[2 * px + py], 1 - c, ws[p].shape)
                _remote(blk, blk, send_sems, recv_sems, 6 * p + 3 + j, me).wait_recv()
        for cp in first + passed:
            cp.wait_send()

    return pl.pallas_call(
        body, name=name, in_specs=[ANY] * n, out_specs=[ANY] * n,
        out_shape=[jax.ShapeDtypeStruct((N_CHIPS,) + w.shape, w.dtype) for w in ws],
        scratch_shapes=[pltpu.SemaphoreType.DMA((6 * n,)), pltpu.SemaphoreType.DMA((6 * n,))],
    )(*ws)


def _gather_small(v, *, name):
    r, w = v.shape

    def body(v_ref, o_ref, send_sems, recv_sems):
        x, y, c = _place()
        m = 2 * x + y
        chips = [(1 - x, y), (x, 1 - y), (1 - x, 1 - y)]
        o_ref[m] = v_ref[...]
        sends = [_remote(v_ref, o_ref.at[m], send_sems, recv_sems, j, (px, py, c)) for j, (px, py) in enumerate(chips)]
        for cp in sends:
            cp.start()
        for j, (px, py) in enumerate(chips):
            blk = o_ref.at[2 * px + py]
            _remote(blk, blk, send_sems, recv_sems, j, (x, y, c)).wait_recv()
        for cp in sends:
            cp.wait_send()

    return pl.pallas_call(
        body, name=name, in_specs=[VMEM_SPEC], out_specs=VMEM_SPEC, out_shape=jax.ShapeDtypeStruct((4, r, w), v.dtype),
        scratch_shapes=[pltpu.SemaphoreType.DMA((3,)), pltpu.SemaphoreType.DMA((3,))],
    )(v)


def _swap_halves_list(gs, *, name):
    n = len(gs)

    def body(*refs):
        g_refs, o_refs, send_sems, recv_sems = refs[:n], refs[n:2 * n], refs[2 * n], refs[2 * n + 1]
        x, y, c = _place()
        cps = [_remote(_row_half(g_ref, 1 - c, gs[p].shape[1]), o_ref, send_sems, recv_sems, p, (x, y, 1 - c))
               for p, (g_ref, o_ref) in enumerate(zip(g_refs, o_refs))]
        for cp in cps:
            cp.start()
        for cp in cps:
            cp.wait()

    return pl.pallas_call(
        body, name=name, in_specs=[ANY] * n, out_specs=[ANY] * n,
        out_shape=[jax.ShapeDtypeStruct((g.shape[0], g.shape[1] // 2, g.shape[2]), g.dtype) for g in gs],
        scratch_shapes=[pltpu.SemaphoreType.DMA((n,)), pltpu.SemaphoreType.DMA((n,))],
    )(*gs)


def _join_halves_list(reds, *, name):
    n = len(reds)

    def body(*refs):
        o_refs, send_sems, recv_sems = refs[n:2 * n], refs[2 * n], refs[2 * n + 1]
        x, y, c = _place()
        cps = []
        for p, o_ref in enumerate(o_refs):
            rh = reds[p].shape[0] // 2
            mine = o_ref.at[pl.ds(pl.multiple_of(c * rh, SUBLANES), rh)]
            cp = _remote(mine, mine, send_sems, recv_sems, p, (x, y, 1 - c))
            cp.start()
            cps.append(cp)
        for p, o_ref in enumerate(o_refs):
            rh = reds[p].shape[0] // 2
            other = o_ref.at[pl.ds(pl.multiple_of((1 - c) * rh, SUBLANES), rh)]
            _remote(other, other, send_sems, recv_sems, p, (x, y, c)).wait_recv()
        for cp in cps:
            cp.wait_send()

    return pl.pallas_call(
        body, name=name, in_specs=[ANY] * n, out_specs=[ANY] * n, out_shape=[jax.ShapeDtypeStruct(r.shape, r.dtype) for r in reds],
        input_output_aliases={p: p for p in range(n)},
        scratch_shapes=[pltpu.SemaphoreType.DMA((n,)), pltpu.SemaphoreType.DMA((n,))],
    )(*reds)


def _scatter_chips_list(sums, *, name):
    n = len(sums)

    def body(*refs):
        a_refs, o_refs, send_sems, recv_sems = refs[:n], refs[n:2 * n], refs[2 * n], refs[2 * n + 1]
        x, y, c = _place()
        m = 2 * x + y
        chips = [(1 - x, y), (x, 1 - y), (1 - x, 1 - y)]
        sends = []
        for p, (a_ref, o_ref) in enumerate(zip(a_refs, o_refs)):
            for j, (px, py) in enumerate(chips):
                cp = _remote(a_ref.at[2 * px + py], o_ref.at[m], send_sems, recv_sems, 3 * p + j, (px, py, c))
                cp.start()
                sends.append(cp)
        for p, o_ref in enumerate(o_refs):
            for j, (px, py) in enumerate(chips):
                blk = o_ref.at[2 * px + py]
                _remote(blk, blk, send_sems, recv_sems, 3 * p + j, (x, y, c)).wait_recv()
        for cp in sends:
            cp.wait_send()

    return pl.pallas_call(
        body, name=name, in_specs=[ANY] * n, out_specs=[ANY] * n, out_shape=[jax.ShapeDtypeStruct(a.shape, a.dtype) for a in sums],
        scratch_shapes=[pltpu.SemaphoreType.DMA((3 * n,)), pltpu.SemaphoreType.DMA((3 * n,))],
    )(*sums)


def _allreduce_small(v, *, name):
    r, w = v.shape

    def body(v_ref, o_ref, slots, send_sems, recv_sems):
        x, y, c = _place()
        me = 4 * x + 2 * y + c
        slots[me] = v_ref[...]
        peers = [((1 - x) if k & 4 else x, (1 - y) if k & 2 else y, (1 - c) if k & 1 else c) for k in range(1, 8)]
        sends = [_remote(v_ref, slots.at[me], send_sems, recv_sems, k, p) for k, p in enumerate(peers)]
        for cp in sends:
            cp.start()
        for k, (px, py, pc) in enumerate(peers):
            blk = slots.at[4 * px + 2 * py + pc]
            _remote(blk, blk, send_sems, recv_sems, k, (x, y, c)).wait_recv()
        for cp in sends:
            cp.wait_send()
        acc = slots[0]
        for k in range(1, 8):
            acc = acc + slots[k]
        o_ref[...] = acc

    return pl.pallas_call(
        body, name=name, in_specs=[VMEM_SPEC], out_specs=VMEM_SPEC, out_shape=jax.ShapeDtypeStruct(v.shape, v.dtype),
        scratch_shapes=[pltpu.VMEM((8, r, w), F32), pltpu.SemaphoreType.DMA((7,)), pltpu.SemaphoreType.DMA((7,))],
    )(v)


def _row_tile(r):
    return r if r <= 512 else _tile(r, (512, 256, 128, 64, 32, 16))


def _add_half(g, recv, place, *, name):
    n, r, w = g.shape
    tm = _row_tile(r // 2)
    nb = (r // 2) // tm

    def body(place_ref, g_ref, r_ref, o_ref):
        o_ref[...] = (g_ref[...].astype(F32) + r_ref[...].astype(F32)).astype(o_ref.dtype)

    return pl.pallas_call(
        body, name=name,
        grid_spec=pltpu.PrefetchScalarGridSpec(
            num_scalar_prefetch=1, grid=(n, nb),
            in_specs=[pl.BlockSpec((None, tm, w), lambda k, i, p: (k, p[1] * nb + i, 0)), pl.BlockSpec((None, tm, w), lambda k, i, p: (k, i, 0))],
            out_specs=pl.BlockSpec((None, tm, w), lambda k, i, p: (k, i, 0))),
        out_shape=jax.ShapeDtypeStruct(recv.shape, g.dtype), compiler_params=_params(("parallel", "parallel")),
    )(place, g, recv)


def _sum_chips(recv, own, place, *, name):
    n, r, w = recv.shape
    tm = _row_tile(r)
    nb = r // tm

    def body(place_ref, *refs):
        own_ref, o_ref = refs[n], refs[n + 1]
        acc = None
        for k in range(n):
            term = jnp.where(place_ref[0] == k, own_ref[...], refs[k][...]).astype(F32)
            acc = term if acc is None else acc + term
        o_ref[...] = acc

    recv_specs = [pl.BlockSpec((None, tm, w), lambda i, p, k=k: (jnp.where(p[0] == k, (k + 1) % n, k), i, 0)) for k in range(n)]
    return pl.pallas_call(
        body, name=name,
        grid_spec=pltpu.PrefetchScalarGridSpec(
            num_scalar_prefetch=1, grid=(nb,),
            in_specs=recv_specs + [pl.BlockSpec((None, tm, w), lambda i, p: (p[0], i, 0))],
            out_specs=pl.BlockSpec((tm, w), lambda i, p: (p[1] * nb + i, 0))),
        out_shape=jax.ShapeDtypeStruct((2 * r, w), F32), compiler_params=_params(("parallel",)),
    )(place, *([recv] * n), own)


def _adamw(w, g, m, v, *, name):
    shape = w.shape
    cols = shape[-1]
    rows = math.prod(shape[:-1])
    tm = _tile(rows, (256, 128, 64, 32, 16, 8))
    c1 = 1.0 - ADAM_B1 ** ADAM_STEP
    c2 = 1.0 - ADAM_B2 ** ADAM_STEP

    def fn(i, nrt, wv, gv, mv, vv):
        mn = ADAM_B1 * mv + (1.0 - ADAM_B1) * gv
        vn = ADAM_B2 * vv + (1.0 - ADAM_B2) * (gv * gv)
        delta = -ADAM_LR * ((mn / c1) / (jnp.sqrt(vn / c2) + ADAM_EPS) + ADAM_WD * wv)
        return (delta, mn, vn), ()

    outs = _rows(fn, name=name, s=rows, tm=tm, ins=[("row", t.reshape(rows, cols), cols, _c0) for t in (w, g, m, v)],
                 outs=[(cols, cols, _c0, F32)] * 3)
    return [o.reshape(shape) for o in outs]


WEIGHTS = ["mix_norm", "ffn_norm", "ffn_w_gu", "ffn_w_down", "conv_w_in", "conv_w_dw", "conv_w_out", "fox_w_in", "fox_b_f", "fox_q_gain",
           "fox_k_gain", "fox_w_out", "ssd_w_in", "ssd_conv_w", "ssd_conv_b", "ssd_dt_bias", "ssd_a_log", "ssd_d", "ssd_norm_w", "ssd_w_out"]
SHARD_AXIS = {"ffn_w_gu": 2, "ffn_w_down": 1, "conv_w_in": 2, "conv_w_dw": 2, "conv_w_out": 1, "fox_w_in": 2, "fox_w_out": 1, "ssd_w_in": 2,
              "ssd_conv_w": 2, "ssd_conv_b": 1, "ssd_norm_w": 1, "ssd_w_out": 1}
BIG = ["ffn_w_gu", "ffn_w_down", "conv_w_in", "conv_w_out", "fox_w_in", "fox_w_out", "ssd_w_in", "ssd_w_out"]
SMALL_SHARDED = ["conv_w_dw", "ssd_conv_w", "ssd_conv_b", "ssd_norm_w"]
N_CHIPS = 4


def _pack_flat(parts, pad_to):
    flat = [p.reshape(-1) for p in parts]
    offs, n = [], 0
    for f in flat:
        offs.append(n)
        n += f.shape[0]
    total = -(-n // pad_to) * pad_to
    if total > n:
        flat.append(jnp.zeros((total - n,), flat[0].dtype))
    return jnp.concatenate(flat).reshape(-1, LANES), offs


def kernel(x, mix_norm, ffn_norm, ffn_w_gu, ffn_w_down, conv_w_in, conv_w_dw, conv_w_out, fox_w_in, fox_b_f, fox_q_gain, fox_k_gain, fox_w_out, ssd_w_in, ssd_conv_w, ssd_conv_b, ssd_dt_bias, ssd_a_log, ssd_d, ssd_norm_w, ssd_w_out, loss_target, m_mix_norm, m_ffn_norm, m_ffn_w_gu, m_ffn_w_down, m_conv_w_in, m_conv_w_dw, m_conv_w_out, m_fox_w_in, m_fox_b_f, m_fox_q_gain, m_fox_k_gain, m_fox_w_out, m_ssd_w_in, m_ssd_conv_w, m_ssd_conv_b, m_ssd_dt_bias, m_ssd_a_log, m_ssd_d, m_ssd_norm_w, m_ssd_w_out, v_mix_norm, v_ffn_norm, v_ffn_w_gu, v_ffn_w_down, v_conv_w_in, v_conv_w_dw, v_conv_w_out, v_fox_w_in, v_fox_b_f, v_fox_q_gain, v_fox_k_gain, v_fox_w_out, v_ssd_w_in, v_ssd_conv_w, v_ssd_conv_b, v_ssd_dt_bias, v_ssd_a_log, v_ssd_d, v_ssd_norm_w, v_ssd_w_out):
    w = dict(zip(WEIGHTS, (mix_norm, ffn_norm, ffn_w_gu, ffn_w_down, conv_w_in, conv_w_dw, conv_w_out, fox_w_in, fox_b_f, fox_q_gain, fox_k_gain,
                           fox_w_out, ssd_w_in, ssd_conv_w, ssd_conv_b, ssd_dt_bias, ssd_a_log, ssd_d, ssd_norm_w, ssd_w_out)))
    m1 = dict(zip(WEIGHTS, (m_mix_norm, m_ffn_norm, m_ffn_w_gu, m_ffn_w_down, m_conv_w_in, m_conv_w_dw, m_conv_w_out, m_fox_w_in, m_fox_b_f,
                            m_fox_q_gain, m_fox_k_gain, m_fox_w_out, m_ssd_w_in, m_ssd_conv_w, m_ssd_conv_b, m_ssd_dt_bias, m_ssd_a_log, m_ssd_d,
                            m_ssd_norm_w, m_ssd_w_out)))
    m2 = dict(zip(WEIGHTS, (v_mix_norm, v_ffn_norm, v_ffn_w_gu, v_ffn_w_down, v_conv_w_in, v_conv_w_dw, v_conv_w_out, v_fox_w_in, v_fox_b_f,
                            v_fox_q_gain, v_fox_k_gain, v_fox_w_out, v_ssd_w_in, v_ssd_conv_w, v_ssd_conv_b, v_ssd_dt_bias, v_ssd_a_log, v_ssd_d,
                            v_ssd_norm_w, v_ssd_w_out)))
    cx, cy, cc = _place()
    chip = 2 * cx + cy

    wb = [w[n].astype(WIRE_DTYPE) for n in BIG]
    gathered = _gather_list(wb, name="gather_weights")
    cm = {n: lax.dynamic_update_slice(g_, wn[None], (chip, 0, 0, 0)) for n, g_, wn in zip(BIG, gathered, wb)}
    sp, soffs = _pack_flat([w[n] for n in SMALL_SHARDED], SUBLANES * LANES)
    sgath = _gather_small(sp, name="gather_small").reshape(N_CHIPS, -1)
    full = {n: w[n] for n in WEIGHTS if n not in SHARD_AXIS}
    for n, off in zip(SMALL_SHARDED, soffs):
        full[n] = jnp.concatenate([sgath[j, off:off + w[n].size].reshape(w[n].shape) for j in range(N_CHIPS)], axis=SHARD_AXIS[n])

    loss, gx, grads, big = _local_step(x[0], loss_target[0], full, cm)
    loss = lax.psum(loss, ("x", "y", "c"))

    where = [(n, l) for n in BIG for l in range(w[n].shape[0])]
    gs = [big[n][l] for n, l in where]
    place = jnp.stack([chip, cc]).astype(jnp.int32)
    from_sibling = _swap_halves_list(gs, name="reduce_halves")
    chip_sums = [_add_half(g_, r_, place, name="reduce_add_sibling") for g_, r_ in zip(gs, from_sibling)]
    by_chip = _scatter_chips_list(chip_sums, name="reduce_chips")
    reds = [_sum_chips(b_, s_, place, name="reduce_sum_chips") for b_, s_ in zip(by_chip, chip_sums)]
    reds = dict(zip(where, _join_halves_list(reds, name="reduce_share")))

    small_names = [n for n in WEIGHTS if n not in BIG]
    sm, smoffs = _pack_flat([grads[n] for n in small_names], SUBLANES * LANES)
    sred = _allreduce_small(sm, name="allreduce_small").reshape(-1)

    g = {n: jnp.stack([reds[(n, l)] for l in range(w[n].shape[0])]).reshape(w[n].shape) for n in BIG}
    for n, off in zip(small_names, smoffs):
        fullg = sred[off:off + grads[n].size].reshape(grads[n].shape)
        if n in SHARD_AXIS:
            ax = SHARD_AXIS[n]
            fullg = lax.dynamic_slice_in_dim(fullg, chip * w[n].shape[ax], w[n].shape[ax], axis=ax)
        g[n] = fullg

    deltas, new_m, new_v = [], [], []
    for n in WEIGHTS:
        dl, mn, vn = _adamw(w[n], g[n], m1[n], m2[n], name=f"adamw_{n}")
        deltas.append(dl)
        new_m.append(mn)
        new_v.append(vn)
    return (loss, gx[None], *[g[n] for n in WEIGHTS], *deltas, *new_m, *new_v)
```

```python
import functools
import math

import jax
import jax.numpy as jnp
from jax import lax
from jax.experimental import pallas as pl
from jax.experimental.pallas import tpu as pltpu

F32 = jnp.float32
MM_DTYPE = jnp.bfloat16
ACT_DTYPE = jnp.bfloat16
WIRE_DTYPE = jnp.bfloat16

RMS_EPS = 1e-6
HEAD = 64
SSM_STATE = 128
SSM_CHUNK = 128
LANES = 128
SUBLANES = 8
VMEM_LIMIT = 48 * 1024 * 1024

ADAM_LR, ADAM_B1, ADAM_B2, ADAM_EPS, ADAM_WD, ADAM_STEP = 0.001, 0.9, 0.999, 1e-08, 0.01, 10

HI = lax.Precision.HIGHEST
MESH = pl.DeviceIdType.MESH


def _tile(dim, prefs):
    for p in prefs:
        if dim % p == 0:
            return p
    return dim


def _params(sem):
    return pltpu.CompilerParams(dimension_semantics=sem, vmem_limit_bytes=VMEM_LIMIT)


def _sigmoid(x):
    return 1.0 / (1.0 + jnp.exp(-x))


def _softplus(x):
    return jnp.maximum(x, 0.0) + jnp.log(1.0 + jnp.exp(-jnp.abs(x)))


TILES = (1024, 1408, 768, 512, 256, 128)
MIN_STEP_WORK = 1 << 29


def _mm(a, b, *, ta=False, tb=False, add=None, out_dtype=F32, name, b_layer=None, out_shard=None, inter=None):
    ka, m = (a.shape[0], a.shape[1]) if ta else (a.shape[1], a.shape[0])
    if b_layer is None:
        kb, n = (b.shape[1], b.shape[0]) if tb else (b.shape[0], b.shape[1])
        ns = None
    else:
        ns = b.shape[3]
        kb, n = (b.shape[0] * ns, b.shape[2]) if tb else (b.shape[2], b.shape[0] * ns)
    assert ka == kb, (a.shape, b.shape, ta, tb)
    k = ka
    tm = _tile(m, (1408, 1024, 512, 256, 128) if ta else (512, 256, 128))
    tn = _tile(n, TILES)
    tk = _tile(k, TILES)
    if inter:
        segs, bw = inter
        if tb:
            tk = bw
        else:
            tn = bw
        tps = ((k if tb else n) // bw) // segs
        col = lambda q: ((q % segs) * tps + q // segs) * bw
    else:
        col = lambda q: q * (tk if tb else tn)
    def vmem(tm_, tk_):
        out_b = jnp.dtype(out_dtype).itemsize * 2 + (8 if add is not None else 0) + 4
        return 2 * tk_ * (tm_ * a.dtype.itemsize + tn * b.dtype.itemsize) + tm_ * tn * out_b

    if ta:
        while tk * 2 <= k and k % (tk * 2) == 0 and tm * tn * tk < MIN_STEP_WORK and vmem(tm, tk * 2) < VMEM_LIMIT * 3 // 4:
            tk *= 2
    else:
        while tm * 2 <= m and m % (tm * 2) == 0 and tm * tn * tk < MIN_STEP_WORK and vmem(tm * 2, tk) < VMEM_LIMIT * 3 // 4:
            tm *= 2
    nk = k // tk
    a_spec = pl.BlockSpec((tk, tm), lambda i, j, q: (q, i)) if ta else pl.BlockSpec((tm, tk), lambda i, j, q: (i, q))
    if b_layer is None:
        b_spec = pl.BlockSpec((tn, tk), lambda i, j, q: (j, q)) if tb else pl.BlockSpec((tk, tn), lambda i, j, q: (q, j))
    elif tb:
        b_spec = pl.BlockSpec((None, None, tn, tk), lambda i, j, q: (col(q) // ns, b_layer, j, (col(q) % ns) // tk))
    else:
        b_spec = pl.BlockSpec((None, None, tk, tn), lambda i, j, q: (col(j) // ns, b_layer, q, (col(j) % ns) // tn))
    if out_shard:
        assert ta and add is None
        o_spec = pl.BlockSpec((None, tm, tn), lambda i, j, q: (col(j) // out_shard, i, (col(j) % out_shard) // tn))
        o_shape = jax.ShapeDtypeStruct((N_CHIPS, m, out_shard), out_dtype)
    else:
        o_spec = pl.BlockSpec((tm, tn), lambda i, j, q: (i, j))
        o_shape = jax.ShapeDtypeStruct((m, n), out_dtype)
    dims = (((0 if ta else 1,), (1 if tb else 0,)), ((), ()))
    has_add = add is not None

    def body(*refs):
        a_ref, b_ref = refs[0], refs[1]
        o_ref = refs[2 + has_add]
        p = lax.dot_general(a_ref[...].astype(MM_DTYPE), b_ref[...].astype(MM_DTYPE), dims, preferred_element_type=F32)

        def finish(acc):
            if has_add:
                acc = acc + refs[2][...].astype(F32)
            o_ref[...] = acc.astype(out_dtype)

        if nk == 1:
            finish(p)
        else:
            acc_ref = refs[3 + has_add]
            q = pl.program_id(2)

            @pl.when(q == 0)
            def _():
                acc_ref[...] = p

            @pl.when(q > 0)
            def _():
                acc_ref[...] += p

            @pl.when(q == nk - 1)
            def _():
                finish(acc_ref[...])

    args = [a, b] + ([add] if has_add else [])
    in_specs = [a_spec, b_spec] + ([o_spec] if has_add else [])
    return pl.pallas_call(
        body, name=name, grid=(m // tm, n // tn, nk), in_specs=in_specs, out_specs=o_spec, out_shape=o_shape,
        scratch_shapes=[pltpu.VMEM((tm, tn), F32)] if nk > 1 else [],
        compiler_params=_params(("parallel", "parallel", "arbitrary")),
    )(*args)


def _rows(fn, *, name, s, tm, ncol=1, ins, outs, accs=()):
    nrt = s // tm
    hb = tm // SUBLANES
    in_specs, args = [], []
    for spec in ins:
        kind, arr = spec[0], spec[1]
        if kind == "full":
            in_specs.append(pl.BlockSpec(arr.shape, lambda j, i: (0, 0)))
        elif kind == "col":
            _, _, bw, cmap = spec
            in_specs.append(pl.BlockSpec((arr.shape[0], bw), lambda j, i, cmap=cmap: (0, cmap(j))))
        elif kind == "row":
            _, _, bw, cmap = spec
            in_specs.append(pl.BlockSpec((tm, bw), lambda j, i, cmap=cmap: (i, cmap(j))))
        elif kind == "prev":
            _, _, bw, cmap = spec
            in_specs.append(pl.BlockSpec((SUBLANES, bw), lambda j, i, cmap=cmap: (jnp.maximum(i * hb - 1, 0), cmap(j))))
        elif kind == "next":
            _, _, bw, cmap = spec
            in_specs.append(pl.BlockSpec((SUBLANES, bw), lambda j, i, cmap=cmap: (jnp.minimum((i + 1) * hb, s // SUBLANES - 1), cmap(j))))
        else:
            raise ValueError(kind)
        args.append(arr)
    out_specs, out_shape = [], []
    for w, bw, cmap, dt in outs:
        out_specs.append(pl.BlockSpec((tm, bw), lambda j, i, cmap=cmap: (i, cmap(j))))
        out_shape.append(jax.ShapeDtypeStruct((s, w), dt))
    for r, w, bw, cmap in accs:
        out_specs.append(pl.BlockSpec((r, bw), lambda j, i, cmap=cmap: (0, cmap(j))))
        out_shape.append(jax.ShapeDtypeStruct((r, w), F32))
    n_in, n_out, n_acc = len(ins), len(outs), len(accs)

    def body(*refs):
        i = pl.program_id(1)
        vals = [r[...] for r in refs[:n_in]]
        o_vals, a_vals = fn(i, nrt, *vals)
        assert len(o_vals) == n_out and len(a_vals) == n_acc
        for r, v in zip(refs[n_in:n_in + n_out], o_vals):
            r[...] = v.astype(r.dtype)
        for r, v in zip(refs[n_in + n_out:], a_vals):
            @pl.when(i == 0)
            def _(r=r, v=v):
                r[...] = v.astype(F32)

            @pl.when(i > 0)
            def _(r=r, v=v):
                r[...] += v.astype(F32)

    res = pl.pallas_call(
        body, name=name, grid=(ncol, nrt), in_specs=in_specs, out_specs=out_specs, out_shape=out_shape,
        compiler_params=_params(("parallel", "arbitrary" if accs else "parallel")),
    )(*args)
    return res


def _c0(j):
    return 0


def _cj(j):
    return j


def _gmean(v, gs):
    w = v.shape[-1]
    tile = max(gs, LANES)
    r = lax.broadcasted_iota(jnp.int32, (tile, tile), 0) // gs
    c = lax.broadcasted_iota(jnp.int32, (tile, tile), 1) // gs
    g = jnp.where(r == c, 1.0 / gs, 0.0).astype(F32)
    parts = [jnp.dot(v[:, t * tile:(t + 1) * tile], g, precision=HI, preferred_element_type=F32) for t in range(w // tile)]
    return parts[0] if len(parts) == 1 else jnp.concatenate(parts, axis=1)


def _sum_rows(v):
    return jnp.sum(v, axis=0, keepdims=True)


def _resid_rms(x, y, w, *, name):
    s, d = x.shape
    has_y = y is not None

    def fn(i, nrt, *v):
        xv = v[0] + (v[1] if has_y else 0.0)
        wv = v[-1]
        r = lax.rsqrt(jnp.mean(xv * xv, axis=-1, keepdims=True) + RMS_EPS)
        return (xv, xv * r * wv), ()

    ins = [("row", x, d, _c0)] + ([("row", y, d, _c0)] if has_y else []) + [("full", w.reshape(1, d))]
    xn, h = _rows(fn, name=name, s=s, tm=_tile(s, (512, 256, 128)), ins=ins, outs=[(d, d, _c0, F32), (d, d, _c0, ACT_DTYPE)])
    return xn, h


def _rms_bwd(x, w, dh, dx_in, *, name):
    s, d = x.shape

    def fn(i, nrt, xv, wv, dhv, dxi):
        r = lax.rsqrt(jnp.mean(xv * xv, axis=-1, keepdims=True) + RMS_EPS)
        xh = xv * r
        g = dhv * wv
        dx = dxi + r * (g - xh * jnp.mean(g * xh, axis=-1, keepdims=True))
        return (dx, dx), (_sum_rows(dhv * xh),)

    dx, dxb, dw = _rows(fn, name=name, s=s, tm=_tile(s, (512, 256, 128)),
                        ins=[("row", x, d, _c0), ("full", w.reshape(1, d)), ("row", dh, d, _c0), ("row", dx_in, d, _c0)],
                        outs=[(d, d, _c0, F32), (d, d, _c0, MM_DTYPE)], accs=[(1, d, d, _c0)])
    return dx, dxb, dw.reshape(d)


def _swiglu_fwd(p, bw, *, name):
    s, f = p.shape[0], p.shape[1] // 2

    def fn(i, nrt, pv):
        gv, uv = pv[:, :bw].astype(F32), pv[:, bw:].astype(F32)
        return (gv * _sigmoid(gv) * uv,), ()

    (a,) = _rows(fn, name=name, s=s, tm=_tile(s, (512, 256, 128)), ncol=f // bw,
                 ins=[("row", p, 2 * bw, _cj)], outs=[(f, bw, _cj, ACT_DTYPE)])
    return a


def _swiglu_bwd(p, da, bw, *, name):
    s, f = da.shape

    def fn(i, nrt, pv, dav):
        gv, uv, dav = pv[:, :bw].astype(F32), pv[:, bw:].astype(F32), dav.astype(F32)
        sg = _sigmoid(gv)
        dg = dav * uv * sg * (1.0 + gv * (1.0 - sg))
        du = dav * gv * sg
        return (jnp.concatenate([dg, du], axis=1),), ()

    (dp,) = _rows(fn, name=name, s=s, tm=_tile(s, (512, 256, 128)), ncol=f // bw,
                  ins=[("row", p, 2 * bw, _cj), ("row", da, bw, _cj)], outs=[(2 * f, 2 * bw, _cj, ACT_DTYPE)])
    return dp


def _loss_head(x, y, tgt, *, name):
    s, d = x.shape

    def fn(i, nrt, xv, yv, tv):
        diff = xv + yv - tv
        part = 0.5 * jnp.sum(diff * diff) / d
        return (diff / d, diff / d), (jnp.full((1, LANES), part, F32),)

    dy, dyb, loss = _rows(fn, name=name, s=s, tm=_tile(s, (512, 256, 128)),
                          ins=[("row", x, d, _c0), ("row", y, d, _c0), ("row", tgt, d, _c0)],
                          outs=[(d, d, _c0, F32), (d, d, _c0, MM_DTYPE)], accs=[(1, LANES, LANES, _c0)])
    return loss[0, 0], dy, dyb


def _shift_down(ext, j, tm):
    src = pltpu.roll(ext, j, 0) if j else ext
    return src[SUBLANES:SUBLANES + tm]


def _shift_up(ext, j, tm):
    return ext[:tm] if j == 0 else pltpu.roll(ext, ext.shape[0] - j, 0)[:tm]


def _gconv_fwd(p, w, bw, *, name):
    s, d = p.shape[0], p.shape[1] // 3
    kw = w.shape[0]
    tm = _tile(s, (512, 256, 128))

    def fn(i, nrt, pv, pp, wv):
        pv, pp = pv.astype(F32), pp.astype(F32)
        cv = pv[:, bw:2 * bw] * pv[:, 2 * bw:]
        pcv = jnp.where(i == 0, 0.0, pp[:, bw:2 * bw] * pp[:, 2 * bw:])
        ext = jnp.concatenate([pcv, cv], axis=0)
        u = sum(wv[k:k + 1, :] * _shift_down(ext, kw - 1 - k, tm) for k in range(kw))
        return (pv[:, :bw] * u,), ()

    (o,) = _rows(fn, name=name, s=s, tm=tm, ncol=d // bw, ins=[("row", p, 3 * bw, _cj), ("prev", p, 3 * bw, _cj), ("col", w, bw, _cj)],
                 outs=[(d, bw, _cj, ACT_DTYPE)])
    return o


def _gconv_bwd(p, w, do, bw, *, name):
    s, d = do.shape
    kw = w.shape[0]
    tm = _tile(s, (512, 256, 128))

    def fn(i, nrt, pv, pp, pn, dov, ndo, wv):
        pv, pp, dov = pv.astype(F32), pp.astype(F32), dov.astype(F32)
        bv, cv_, vv = pv[:, :bw], pv[:, bw:2 * bw], pv[:, 2 * bw:]
        cv = cv_ * vv
        pcv = jnp.where(i == 0, 0.0, pp[:, bw:2 * bw] * pp[:, 2 * bw:])
        ext = jnp.concatenate([pcv, cv], axis=0)
        shifted = [_shift_down(ext, kw - 1 - k, tm) for k in range(kw)]
        u = sum(wv[k:k + 1, :] * shifted[k] for k in range(kw))
        db = dov * u
        du = dov * bv
        ndu = jnp.where(i == nrt - 1, 0.0, ndo.astype(F32) * pn[:, :bw].astype(F32))
        ext2 = jnp.concatenate([du, ndu], axis=0)
        dcv = sum(wv[k:k + 1, :] * _shift_up(ext2, kw - 1 - k, tm) for k in range(kw))
        dw = jnp.concatenate([_sum_rows(du * shifted[k]) for k in range(kw)], axis=0)
        return (jnp.concatenate([db, dcv * vv, dcv * cv_], axis=1),), (dw,)

    dp, dw = _rows(fn, name=name, s=s, tm=tm, ncol=d // bw,
                   ins=[("row", p, 3 * bw, _cj), ("prev", p, 3 * bw, _cj), ("next", p, 3 * bw, _cj), ("row", do, bw, _cj),
                        ("next", do, bw, _cj), ("col", w, bw, _cj)],
                   outs=[(3 * d, 3 * bw, _cj, ACT_DTYPE)], accs=[(kw, d, bw, _cj)])
    return dp, dw


def _sconv_fwd(x, w, bias, *, name):
    s, d = x.shape
    kw = w.shape[0]
    bw = _tile(d, (512, 256, 128))
    tm = _tile(s, (512, 256, 128))

    def fn(i, nrt, xv, px, wv, bsv):
        xv = xv.astype(F32)
        ext = jnp.concatenate([jnp.where(i == 0, 0.0, px.astype(F32)), xv], axis=0)
        pre = sum(wv[k:k + 1, :] * _shift_down(ext, kw - 1 - k, tm) for k in range(kw)) + bsv
        return (pre * _sigmoid(pre),), ()

    (o,) = _rows(fn, name=name, s=s, tm=tm, ncol=d // bw,
                 ins=[("row", x, bw, _cj), ("prev", x, bw, _cj), ("col", w, bw, _cj), ("col", bias.reshape(1, d), bw, _cj)],
                 outs=[(d, bw, _cj, ACT_DTYPE)])
    return o


def _sconv_bwd(x, w, bias, dact, *, name):
    s, d = x.shape
    kw = w.shape[0]
    bw = _tile(d, (512, 256, 128))
    tm = _tile(s, (512, 256, 128))

    def fn(i, nrt, xv, px, nx, dav, nda, wv, bsv):
        xv = xv.astype(F32)
        ext = jnp.concatenate([jnp.where(i == 0, 0.0, px.astype(F32)), xv, nx.astype(F32)], axis=0)
        rows_e = tm + SUBLANES
        pre_e = sum(wv[k:k + 1, :] * _shift_down(ext, kw - 1 - k, rows_e) for k in range(kw)) + bsv
        da_e = jnp.concatenate([dav.astype(F32), jnp.where(i == nrt - 1, 0.0, nda.astype(F32))], axis=0)
        sg = _sigmoid(pre_e)
        dpre_e = da_e * sg * (1.0 + pre_e * (1.0 - sg))
        dx = sum(wv[k:k + 1, :] * _shift_up(dpre_e, kw - 1 - k, tm) for k in range(kw))
        dpre = dpre_e[:tm]
        dw = jnp.concatenate([_sum_rows(dpre * _shift_down(ext, kw - 1 - k, tm)) for k in range(kw)], axis=0)
        return (dx,), (dw, _sum_rows(dpre))

    dx, dw, db = _rows(fn, name=name, s=s, tm=tm, ncol=d // bw,
                       ins=[("row", x, bw, _cj), ("prev", x, bw, _cj), ("next", x, bw, _cj), ("row", dact, bw, _cj),
                            ("next", dact, bw, _cj), ("col", w, bw, _cj), ("col", bias.reshape(1, d), bw, _cj)],
                       outs=[(d, bw, _cj, ACT_DTYPE)], accs=[(kw, d, bw, _cj), (1, d, bw, _cj)])
    return dx, dw, db.reshape(d)


def _tri(n, reverse):
    r = lax.broadcasted_iota(jnp.int32, (n, n), 0)
    c = lax.broadcasted_iota(jnp.int32, (n, n), 1)
    return jnp.where((c >= r) if reverse else (c <= r), 1.0, 0.0).astype(F32)


def _cumsum_rows(x, *, reverse, name):
    s, w = x.shape
    ch = _tile(s, (256, 128))
    n = s // ch

    def body(x_ref, o_ref, carry):
        i = pl.program_id(0)

        @pl.when(i == 0)
        def _():
            carry[...] = jnp.zeros_like(carry)

        out = jnp.dot(_tri(ch, reverse), x_ref[...], precision=HI, preferred_element_type=F32) + carry[...]
        o_ref[...] = out
        carry[...] = out[0:1, :] if reverse else out[ch - 1:ch, :]

    imap = (lambda i: (n - 1 - i, 0)) if reverse else (lambda i: (i, 0))
    return pl.pallas_call(
        body, name=name, grid=(n,), in_specs=[pl.BlockSpec((ch, w), imap)], out_specs=pl.BlockSpec((ch, w), imap),
        out_shape=jax.ShapeDtypeStruct((s, w), F32), scratch_shapes=[pltpu.VMEM((1, w), F32)],
        compiler_params=_params(("arbitrary",)),
    )(x)


def _fox_prep(q, k, f, gq, gk, bf, *, name):
    s, d = q.shape
    scale = HEAD ** -0.5

    def fn(i, nrt, qv, kv, fv, gqv, gkv, bfv):
        qv, kv = qv.astype(F32), kv.astype(F32)
        qn = qv * lax.rsqrt(_gmean(qv * qv, HEAD) + RMS_EPS) * gqv * scale
        kn = kv * lax.rsqrt(_gmean(kv * kv, HEAD) + RMS_EPS) * gkv
        z = fv + bfv
        logf = jnp.minimum(z, 0.0) - jnp.log(1.0 + jnp.exp(-jnp.abs(z)))
        return (qn, kn, logf), ()

    return _rows(fn, name=name, s=s, tm=_tile(s, (512, 256, 128)),
                 ins=[("row", q, d, _c0), ("row", k, d, _c0), ("row", f, LANES, _c0), ("full", gq), ("full", gk), ("full", bf)],
                 outs=[(d, d, _c0, ACT_DTYPE), (d, d, _c0, ACT_DTYPE), (LANES, LANES, _c0, F32)])


def _fox_prep_bwd(q, k, f, gq, gk, bf, dqs, dkn, dlogf, *, name):
    s, d = q.shape
    scale = HEAD ** -0.5

    def fn(i, nrt, qv, kv, fv, gqv, gkv, bfv, dqv, dkv, dlf):
        outs, accs = [], []
        for xv, gv, dv, sc in ((qv, gqv, dqv, scale), (kv, gkv, dkv, 1.0)):
            xv, dv = xv.astype(F32), dv.astype(F32) * sc
            r = lax.rsqrt(_gmean(xv * xv, HEAD) + RMS_EPS)
            xh = xv * r
            g = dv * gv
            outs.append(r * (g - xh * _gmean(g * xh, HEAD)))
            accs.append(_sum_rows(dv * xh))
        z = fv + bfv
        df = dlf * _sigmoid(-z)
        outs.append(df)
        accs.append(_sum_rows(df))
        return outs, accs

    return _rows(fn, name=name, s=s, tm=_tile(s, (512, 256, 128)),
                 ins=[("row", q, d, _c0), ("row", k, d, _c0), ("row", f, LANES, _c0), ("full", gq), ("full", gk), ("full", bf),
                      ("row", dqs, d, _c0), ("row", dkn, d, _c0), ("row", dlogf, LANES, _c0)],
                 outs=[(d, d, _c0, ACT_DTYPE), (d, d, _c0, ACT_DTYPE), (LANES, LANES, _c0, ACT_DTYPE)],
                 accs=[(1, d, d, _c0), (1, d, d, _c0), (1, LANES, LANES, _c0)])


def _head_masks(shape):
    lane = lax.broadcasted_iota(jnp.int32, shape, len(shape) - 1)
    return lane < HEAD, lane >= HEAD


def _pick_lane(blk, idx):
    lane = lax.broadcasted_iota(jnp.int32, blk.shape, 1)
    return jnp.sum(jnp.where(lane == idx, blk, 0.0), axis=1, keepdims=True)


def _pick_row(blk, idx):
    sub = lax.broadcasted_iota(jnp.int32, blk.shape, 0)
    return jnp.sum(jnp.where(sub == idx, blk, 0.0), axis=0, keepdims=True)


def _fox_aug(qs, kn, cum, *, name):
    s, d = qs.shape
    hp = d // LANES

    def fn(i, nrt, qv, kv, cv):
        lane = lax.broadcasted_iota(jnp.int32, (qv.shape[0], LANES), 1)
        outs = [[], [], [], []]
        for p in range(hp):
            qt, kt = qv[:, p * LANES:(p + 1) * LANES], kv[:, p * LANES:(p + 1) * LANES]
            for h in range(2):
                mine = (lane < HEAD) if h == 0 else (lane >= HEAD)
                a0 = HEAD if h == 0 else 0
                c = cv[:, 2 * p + h:2 * p + h + 1]
                hi = c.astype(ACT_DTYPE).astype(F32)
                mid = (c - hi).astype(ACT_DTYPE).astype(F32)
                lo = (c - hi - mid).astype(ACT_DTYPE).astype(F32)
                ones = jnp.where((lane >= a0) & (lane < a0 + 3), 1.0, 0.0)
                kx = jnp.where(lane == a0, -hi, jnp.where(lane == a0 + 1, -mid, jnp.where(lane == a0 + 2, -lo, 0.0)))
                outs[h].append(jnp.where(mine, qt.astype(F32), ones))
                outs[2 + h].append(jnp.where(mine, kt.astype(F32), kx))
        return [jnp.concatenate(o, axis=1) for o in outs], ()

    return _rows(fn, name=name, s=s, tm=_tile(s, (512, 256, 128)), ins=[("row", qs, d, _c0), ("row", kn, d, _c0), ("row", cum, LANES, _c0)],
                 outs=[(d, d, _c0, ACT_DTYPE)] * 4)


def _tri_tables(nq, by_key):
    import numpy as np
    pairs = [(qi, kj) for kj in range(nq) for qi in range(kj, nq)] if by_key else [(qi, kj) for qi in range(nq) for kj in range(qi + 1)]
    return jnp.asarray(np.array([p[0] for p in pairs], np.int32)), jnp.asarray(np.array([p[1] for p in pairs], np.int32))


def _nt(a, b):
    return lax.dot_general(a, b, (((1,), (1,)), ((), ())), preferred_element_type=F32)


def _tn(a, b):
    return lax.dot_general(a, b, (((0,), (0,)), ((), ())), preferred_element_type=F32)


ATTN_BLOCKS = (1024, 512, 256, 128)


def _fox_dd(do, o, *, name):
    s, d = do.shape

    def fn(i, nrt, dov, ov):
        return (_reduce_heads(dov.astype(F32) * ov.astype(F32), d // HEAD, HEAD),), ()

    (dd,) = _rows(fn, name=name, s=s, tm=_tile(s, (512, 256, 128)), ins=[("row", do, d, _c0), ("row", o, d, _c0)],
                  outs=[(LANES, LANES, _c0, F32)])
    return dd


def _pair_rows(a, nh):
    s = a.shape[0]
    t = a[:, :nh].T.reshape(nh // 2, 2, s)
    return jnp.pad(t, ((0, 0), (0, SUBLANES - 2), (0, 0)))


def _rows01(r0, r1):
    sub = lax.broadcasted_iota(jnp.int32, (SUBLANES, r0.shape[1]), 0)
    return jnp.where(sub == 0, r0, jnp.where(sub == 1, r1, 0.0))


def _rider_parts(rider):
    if rider is None:
        return [], [], [], [], []
    return [ANY] * len(rider.arrays), [ANY] * len(rider.out_shapes), rider.out_shapes, rider.scratch, rider.arrays


def _fox_fwd_t(q_aug, k_aug, v, *, name, rider=None):
    s, d = v.shape
    bq = _tile(s, ATTN_BLOCKS)
    nq = s // bq
    hp = d // LANES
    qtab, ktab = _tri_tables(nq, by_key=False)

    def body(qt, kt, q0_ref, q1_ref, k0_ref, k1_ref, v_ref, o_ref, lse_ref, m0, m1, l0, l1, acc0, acc1):
        t = pl.program_id(1)
        qi, kj = qt[t], kt[t]
        ms, ls, accs = (m0, m1), (l0, l1), (acc0, acc1)

        @pl.when(kj == 0)
        def _():
            for h in range(2):
                ms[h][...] = jnp.full_like(ms[h], -jnp.inf)
                ls[h][...] = jnp.zeros_like(ls[h])
                accs[h][...] = jnp.zeros_like(accs[h])

        def update(diagonal):
            v2 = v_ref[...]
            qk = ((q0_ref, k0_ref), (q1_ref, k1_ref))
            sts = [_nt(qk[h][1][...], qk[h][0][...]) for h in range(2)]
            if diagonal:
                sts = [_diag_mask_t(st) for st in sts]
            m_prev = [ms[h][...] for h in range(2)]
            m_new = [jnp.maximum(m_prev[h], jnp.max(sts[h], axis=0, keepdims=True)) for h in range(2)]
            ps = [jnp.exp(sts[h] - m_new[h]) for h in range(2)]
            alpha = [jnp.exp(m_prev[h] - m_new[h]) for h in range(2)]
            for h in range(2):
                ls[h][...] = alpha[h] * ls[h][...] + jnp.sum(ps[h], axis=0, keepdims=True)
                accs[h][...] = alpha[h] * accs[h][...] + _tn(v2, ps[h].astype(MM_DTYPE))
                ms[h][...] = m_new[h]

        @pl.when(kj < qi)
        def _():
            update(False)

        @pl.when(kj == qi)
        def _():
            update(True)
            row = lax.broadcasted_iota(jnp.int32, (LANES, bq), 0)
            ot = jnp.where(row < HEAD, acc0[...] / l0[...], acc1[...] / l1[...])
            o_ref[...] = ot.T.astype(o_ref.dtype)
            lse_ref[...] = _rows01(m0[...] + jnp.log(l0[...]), m1[...] + jnp.log(l1[...]))

    blk = (bq, LANES)
    qmap = lambda p_, t, qt, kt: (qt[t], p_)
    kmap = lambda p_, t, qt, kt: (kt[t], p_)
    grid = (hp, qtab.shape[0])
    r_in, r_out, r_shapes, r_scratch, r_args = _rider_parts(rider)
    grid_spec = pltpu.PrefetchScalarGridSpec(
        num_scalar_prefetch=2, grid=grid,
        in_specs=[pl.BlockSpec(blk, qmap), pl.BlockSpec(blk, qmap), pl.BlockSpec(blk, kmap), pl.BlockSpec(blk, kmap), pl.BlockSpec(blk, kmap)] + r_in,
        out_specs=[pl.BlockSpec(blk, qmap), pl.BlockSpec((None, SUBLANES, bq), lambda p_, t, qt, kt: (p_, 0, qt[t]))] + r_out,
        scratch_shapes=[pltpu.VMEM((1, bq), F32)] * 4 + [pltpu.VMEM((LANES, bq), F32)] * 2 + r_scratch)
    outs = pl.pallas_call(
        _carry(body, rider, 2, 5, 2, grid), name=name, grid_spec=grid_spec,
        out_shape=[jax.ShapeDtypeStruct((s, d), ACT_DTYPE), jax.ShapeDtypeStruct((hp, SUBLANES, s), F32)] + r_shapes,
        compiler_params=_params(("arbitrary", "arbitrary") if rider else ("parallel", "arbitrary")),
    )(qtab, ktab, q_aug[0], q_aug[1], k_aug[0], k_aug[1], v, *r_args)
    return outs[0], outs[1], list(outs[2:])


def _diag_mask_t(st):
    key = lax.broadcasted_iota(jnp.int32, st.shape, 0)
    qry = lax.broadcasted_iota(jnp.int32, st.shape, 1)
    return jnp.where(qry >= key, st, -jnp.inf)


def _fox_bwd_t(q_aug, k_aug, v, lse, dd, do, *, name, rider=None):
    s, d = v.shape
    bq = _tile(s, ATTN_BLOCKS)
    nq = s // bq
    hp = d // LANES
    blk = (bq, LANES)
    qtab, ktab = _tri_tables(nq, by_key=True)
    n_steps = qtab.shape[0]

    def body(qt, kt, q0_ref, q1_ref, k0_ref, k1_ref, v_ref, lse_ref, dd_ref, do_ref,
             dq_ref, dk_ref, dv_ref, dcol_ref, drow_ref, dq_sc, rs_sc, dk_sc, dv_sc, cs_sc):
        t = pl.program_id(1)
        qi, kj = qt[t], kt[t]

        @pl.when(t == 0)
        def _():
            dq_sc[...] = jnp.zeros_like(dq_sc)
            rs_sc[...] = jnp.zeros_like(rs_sc)

        def update(diagonal):
            v2, do2 = v_ref[...], do_ref[...]
            masks = _head_masks(blk)
            row = lax.broadcasted_iota(jnp.int32, (LANES, bq), 0)
            off = pl.multiple_of(qi * bq, bq)
            for h, (q_ref, k_ref) in enumerate(((q0_ref, k0_ref), (q1_ref, k1_ref))):
                st = _nt(k_ref[...], q_ref[...])
                if diagonal:
                    st = _diag_mask_t(st)
                p = jnp.exp(st - lse_ref[h:h + 1, :])
                dp = _nt(v2, jnp.where(masks[h], do2, jnp.zeros_like(do2)))
                ds = p * (dp - dd_ref[h:h + 1, :])
                dsb = ds.astype(MM_DTYPE)
                dv_sc[h] += jnp.dot(p.astype(MM_DTYPE), do2, preferred_element_type=F32)
                dk_sc[h] += jnp.dot(dsb, q_ref[...], preferred_element_type=F32)
                cs_sc[h] += jnp.sum(ds, axis=1, keepdims=True)
                mine = (row < HEAD) if h == 0 else (row >= HEAD)
                dq_sc[:, pl.ds(off, bq)] += jnp.where(mine, _tn(k_ref[...], dsb), 0.0)
                rs_sc[h:h + 1, pl.ds(off, bq)] += jnp.sum(ds, axis=0, keepdims=True)

        @pl.when(qi == kj)
        def _():
            dk_sc[...] = jnp.zeros_like(dk_sc)
            dv_sc[...] = jnp.zeros_like(dv_sc)
            cs_sc[...] = jnp.zeros_like(cs_sc)
            update(True)

        @pl.when(qi > kj)
        def _():
            update(False)

        @pl.when(qi == nq - 1)
        def _():
            lo, _hi = _head_masks(blk)
            dk_ref[...] = jnp.where(lo, dk_sc[0], dk_sc[1]).astype(dk_ref.dtype)
            dv_ref[...] = jnp.where(lo, dv_sc[0], dv_sc[1]).astype(dv_ref.dtype)
            dcol_ref[...] = jnp.where(lo, cs_sc[0], cs_sc[1])

        @pl.when(t == n_steps - 1)
        def _():
            for c in range(nq):
                dq_ref[c * bq:(c + 1) * bq, :] = dq_sc[:, c * bq:(c + 1) * bq].T.astype(dq_ref.dtype)
            drow_ref[...] = rs_sc[...]

    qmap = lambda p_, t, qt, kt: (qt[t], p_)
    kmap = lambda p_, t, qt, kt: (kt[t], p_)
    rmap = lambda p_, t, qt, kt: (p_, 0, qt[t])
    grid = (hp, n_steps)
    r_in, r_out, r_shapes, r_scratch, r_args = _rider_parts(rider)
    outs = pl.pallas_call(
        _carry(body, rider, 2, 8, 5, grid), name=name,
        grid_spec=pltpu.PrefetchScalarGridSpec(
            num_scalar_prefetch=2, grid=grid,
            in_specs=[pl.BlockSpec(blk, qmap), pl.BlockSpec(blk, qmap), pl.BlockSpec(blk, kmap), pl.BlockSpec(blk, kmap), pl.BlockSpec(blk, kmap),
                      pl.BlockSpec((None, SUBLANES, bq), rmap), pl.BlockSpec((None, SUBLANES, bq), rmap), pl.BlockSpec(blk, qmap)] + r_in,
            out_specs=[pl.BlockSpec((s, LANES), lambda p_, t, qt, kt: (0, p_)), pl.BlockSpec(blk, kmap), pl.BlockSpec(blk, kmap),
                       pl.BlockSpec(blk, kmap), pl.BlockSpec((None, SUBLANES, s), lambda p_, t, qt, kt: (p_, 0, 0))] + r_out,
            scratch_shapes=[pltpu.VMEM((LANES, s), F32), pltpu.VMEM((SUBLANES, s), F32), pltpu.VMEM((2, bq, LANES), F32),
                            pltpu.VMEM((2, bq, LANES), F32), pltpu.VMEM((2, bq, 1), F32)] + r_scratch),
        out_shape=[jax.ShapeDtypeStruct((s, d), ACT_DTYPE), jax.ShapeDtypeStruct((s, d), ACT_DTYPE), jax.ShapeDtypeStruct((s, d), ACT_DTYPE),
                   jax.ShapeDtypeStruct((s, d), F32), jax.ShapeDtypeStruct((hp, SUBLANES, s), F32)] + r_shapes,
        compiler_params=_params(("arbitrary", "arbitrary") if rider else ("parallel", "arbitrary")),
    )(qtab, ktab, q_aug[0], q_aug[1], k_aug[0], k_aug[1], v, lse, dd, do, *r_args)
    return list(outs[:5]), list(outs[5:])


def _expand_heads(v, nh, hd):
    r = lax.broadcasted_iota(jnp.int32, (LANES, nh * hd), 0)
    c = lax.broadcasted_iota(jnp.int32, (LANES, nh * hd), 1) // hd
    e = jnp.where(r == c, 1.0, 0.0).astype(F32)
    return jnp.dot(v, e, precision=HI, preferred_element_type=F32)


def _reduce_heads(v, nh, hd):
    r = lax.broadcasted_iota(jnp.int32, (nh * hd, LANES), 0) // hd
    c = lax.broadcasted_iota(jnp.int32, (nh * hd, LANES), 1)
    e = jnp.where(r == c, 1.0, 0.0).astype(F32)
    return jnp.dot(v, e, precision=HI, preferred_element_type=F32)


def _ssd_prep(dt_raw, dt_bias, a_log, nh, *, name):
    s = dt_raw.shape[0]

    def fn(i, nrt, dtr, bsv, alv):
        dt = _softplus(dtr + bsv)
        acum = jnp.dot(_tri(SSM_CHUNK, False), dt * (-jnp.exp(alv)), precision=HI, preferred_element_type=F32)
        return (dt, acum, _expand_heads(dt, nh, HEAD), _expand_heads(acum, nh, HEAD)), ()

    w = nh * HEAD
    return _rows(fn, name=name, s=s, tm=SSM_CHUNK, ins=[("row", dt_raw, LANES, _c0), ("full", dt_bias), ("full", a_log)],
                 outs=[(LANES, LANES, _c0, F32), (LANES, LANES, _c0, F32), (w, w, _c0, F32), (w, w, _c0, F32)])


def _ssd_prep_bwd(dt_raw, dt_bias, a_log, ddtx, dacx, nh, *, name):
    s = dt_raw.shape[0]

    def fn(i, nrt, dtr, bsv, alv, ddx, dax):
        z = dtr + bsv
        dt = _softplus(z)
        a = -jnp.exp(alv)
        dda = jnp.dot(_tri(SSM_CHUNK, True), _reduce_heads(dax, nh, HEAD), precision=HI, preferred_element_type=F32)
        ddt = _reduce_heads(ddx, nh, HEAD) + dda * a
        dz = ddt * _sigmoid(z)
        lane = lax.broadcasted_iota(jnp.int32, dz.shape, 1)
        dz = jnp.where(lane < nh, dz, 0.0)
        return (dz,), (_sum_rows(dz), _sum_rows(dda * dt) * a)

    w = nh * HEAD
    return _rows(fn, name=name, s=s, tm=SSM_CHUNK,
                 ins=[("row", dt_raw, LANES, _c0), ("full", dt_bias), ("full", a_log), ("row", ddtx, w, _c0), ("row", dacx, w, _c0)],
                 outs=[(LANES, LANES, _c0, ACT_DTYPE)], accs=[(1, LANES, LANES, _c0), (1, LANES, LANES, _c0)])


def _ssd_decay(ac_blk, act_blk, head):
    col = _pick_lane(ac_blk, head)
    row = _pick_row(act_blk, head)
    r = lax.broadcasted_iota(jnp.int32, (SSM_CHUNK, SSM_CHUNK), 0)
    c = lax.broadcasted_iota(jnp.int32, (SSM_CHUNK, SSM_CHUNK), 1)
    return jnp.exp(jnp.where(r >= c, col - row, -jnp.inf))


def _group_masks(shape, hpg):
    lane = lax.broadcasted_iota(jnp.int32, shape, len(shape) - 1) // HEAD
    return [lane == k for k in range(hpg)]


def _ssd_scan_fwd(xs, bm, cm, dtx, acx, acum, acum_t, d_x, *, name):
    s, di = xs.shape
    ng = bm.shape[1] // SSM_STATE
    gw = di // ng
    hpg = gw // HEAD
    nc = s // SSM_CHUNK
    L = SSM_CHUNK
    nh_pad = acum_t.shape[0]

    def body(x_ref, b_ref, c_ref, dt_ref, ax_ref, ac_ref, act_ref, d_ref, y_ref, st_ref, state):
        g, c = pl.program_id(0), pl.program_id(1)

        @pl.when(c == 0)
        def _():
            state[...] = jnp.zeros_like(state)

        x4, bv, cv = x_ref[...].astype(F32), b_ref[...], c_ref[...]
        ax = ax_ref[...]
        tx = (x4 * dt_ref[...])
        cb = lax.dot_general(cv, bv, (((1,), (1,)), ((), ())), preferred_element_type=F32)
        masks = _group_masks((L, gw), hpg)
        y = jnp.zeros((L, gw), F32)
        txb = tx.astype(MM_DTYPE)
        for k in range(hpg):
            wk = (cb * _ssd_decay(ac_ref[...], act_ref[...], g * hpg + k)).astype(MM_DTYPE)
            y = y + jnp.where(masks[k], jnp.dot(wk, txb, preferred_element_type=F32), 0.0)
        prev = state[...]
        st_ref[...] = prev
        y = y + jnp.dot(cv, prev.astype(MM_DTYPE), preferred_element_type=F32) * jnp.exp(ax)
        y = y + d_ref[...] * x4
        y_ref[...] = y.astype(y_ref.dtype)
        a_last = ax[L - 1:L, :]
        sx = (tx * jnp.exp(a_last - ax)).astype(MM_DTYPE)
        state[...] = prev * jnp.exp(a_last) + lax.dot_general(bv, sx, (((0,), (0,)), ((), ())), preferred_element_type=F32)

    y, states = pl.pallas_call(
        body, name=name, grid=(ng, nc),
        in_specs=[pl.BlockSpec((L, gw), lambda g, c: (c, g)), pl.BlockSpec((L, SSM_STATE), lambda g, c: (c, g)),
                  pl.BlockSpec((L, SSM_STATE), lambda g, c: (c, g)), pl.BlockSpec((L, gw), lambda g, c: (c, g)),
                  pl.BlockSpec((L, gw), lambda g, c: (c, g)), pl.BlockSpec((L, LANES), lambda g, c: (c, 0)),
                  pl.BlockSpec((nh_pad, L), lambda g, c: (0, c)), pl.BlockSpec((1, gw), lambda g, c: (0, g))],
        out_specs=[pl.BlockSpec((L, gw), lambda g, c: (c, g)), pl.BlockSpec((None, None, SSM_STATE, gw), lambda g, c: (g, c, 0, 0))],
        out_shape=[jax.ShapeDtypeStruct((s, di), ACT_DTYPE), jax.ShapeDtypeStruct((ng, nc, SSM_STATE, gw), F32)],
        scratch_shapes=[pltpu.VMEM((SSM_STATE, gw), F32)],
        compiler_params=_params(("parallel", "arbitrary")),
    )(xs, bm, cm, dtx, acx, acum, acum_t, d_x)
    return y, states


def _ssd_scan_bwd(xs, bm, cm, dtx, acx, acum, acum_t, d_x, states, dy, *, name, rider=None):
    s, di = xs.shape
    ng = bm.shape[1] // SSM_STATE
    gw = di // ng
    hpg = gw // HEAD
    nc = s // SSM_CHUNK
    L = SSM_CHUNK
    nh_pad = acum_t.shape[0]

    def body(x_ref, b_ref, c_ref, dt_ref, ax_ref, ac_ref, act_ref, d_ref, st_ref, dy_ref,
             dx_ref, db_ref, dc_ref, ddt_ref, dax_ref, dd_ref, dstate):
        g, cc = pl.program_id(0), pl.program_id(1)

        @pl.when(cc == 0)
        def _():
            dstate[...] = jnp.zeros_like(dstate)
            dd_ref[...] = jnp.zeros_like(dd_ref)

        x4, bv, cv = x_ref[...].astype(F32), b_ref[...], c_ref[...]
        tv, ax, dyv = dt_ref[...], ax_ref[...], dy_ref[...].astype(F32)
        prev, dn = st_ref[...], dstate[...]
        dnb = dn.astype(MM_DTYPE)
        masks = _group_masks((L, gw), hpg)
        tx = x4 * tv
        txb = tx.astype(MM_DTYPE)
        e_ax = jnp.exp(ax)
        a_last = ax[L - 1:L, :]
        e_last = jnp.exp(a_last)
        ed = jnp.exp(a_last - ax)

        dx = d_ref[...] * dyv
        dd_ref[...] += _sum_rows(dyv * x4)
        dye = (dyv * e_ax).astype(MM_DTYPE)
        yo = jnp.dot(cv, prev.astype(MM_DTYPE), preferred_element_type=F32) * e_ax
        dc = lax.dot_general(dye, prev.astype(MM_DTYPE), (((1,), (1,)), ((), ())), preferred_element_type=F32)
        dprev = lax.dot_general(cv, dye, (((0,), (0,)), ((), ())), preferred_element_type=F32)
        dax = dyv * yo
        sx = tx * ed
        dsx = jnp.dot(bv, dnb, preferred_element_type=F32)
        db = lax.dot_general(sx.astype(MM_DTYPE), dnb, (((1,), (1,)), ((), ())), preferred_element_type=F32)
        dtx_ = dsx * ed
        dsx_sx = dsx * sx
        dax = dax - dsx_sx
        dlast = _sum_rows(dsx_sx) + _sum_rows(dn * prev) * e_last
        dprev = dprev + dn * e_last
        cb = lax.dot_general(cv, bv, (((1,), (1,)), ((), ())), preferred_element_type=F32)
        dcb = jnp.zeros((L, L), F32)
        lane = lax.broadcasted_iota(jnp.int32, (L, gw), 1)
        for k in range(hpg):
            dec = _ssd_decay(ac_ref[...], act_ref[...], g * hpg + k)
            wk = (cb * dec).astype(MM_DTYPE)
            dyk = jnp.where(masks[k], dyv, 0.0).astype(MM_DTYPE)
            dtx_ = dtx_ + jnp.where(masks[k], lax.dot_general(wk, dyk, (((0,), (0,)), ((), ())), preferred_element_type=F32), 0.0)
            dwk = lax.dot_general(dyk, txb, (((1,), (1,)), ((), ())), preferred_element_type=F32)
            dcb = dcb + dwk * dec
            mk = dwk * cb * dec
            da_k = jnp.sum(mk, axis=1, keepdims=True) - jnp.sum(mk.T, axis=1, keepdims=True)
            dax = dax + jnp.where(lane == k * HEAD, da_k, 0.0)
        dcbb = dcb.astype(MM_DTYPE)
        dc = dc + jnp.dot(dcbb, bv, preferred_element_type=F32)
        db = db + lax.dot_general(dcbb, cv, (((0,), (0,)), ((), ())), preferred_element_type=F32)
        sub = lax.broadcasted_iota(jnp.int32, (L, gw), 0)
        dax = dax + jnp.where(sub == L - 1, dlast, 0.0)
        dx_ref[...] = (dx + dtx_ * tv).astype(dx_ref.dtype)
        ddt_ref[...] = dtx_ * x4
        dax_ref[...] = dax
        db_ref[...] = db.astype(db_ref.dtype)
        dc_ref[...] = dc.astype(dc_ref.dtype)
        dstate[...] = dprev

    rev = lambda g, c: (nc - 1 - c, g)
    rev0 = lambda g, c: (nc - 1 - c, 0)
    grid = (ng, nc)
    r_in, r_out, r_shapes, r_scratch, r_args = _rider_parts(rider)
    outs = pl.pallas_call(
        _carry(body, rider, 0, 10, 6, grid), name=name, grid=grid,
        in_specs=[pl.BlockSpec((L, gw), rev), pl.BlockSpec((L, SSM_STATE), rev), pl.BlockSpec((L, SSM_STATE), rev),
                  pl.BlockSpec((L, gw), rev), pl.BlockSpec((L, gw), rev), pl.BlockSpec((L, LANES), rev0),
                  pl.BlockSpec((nh_pad, L), lambda g, c: (0, nc - 1 - c)), pl.BlockSpec((1, gw), lambda g, c: (0, g)),
                  pl.BlockSpec((None, None, SSM_STATE, gw), lambda g, c: (g, nc - 1 - c, 0, 0)), pl.BlockSpec((L, gw), rev)] + r_in,
        out_specs=[pl.BlockSpec((L, gw), rev), pl.BlockSpec((L, SSM_STATE), rev), pl.BlockSpec((L, SSM_STATE), rev),
                   pl.BlockSpec((L, gw), rev), pl.BlockSpec((L, gw), rev), pl.BlockSpec((1, gw), lambda g, c: (0, g))] + r_out,
        out_shape=[jax.ShapeDtypeStruct((s, di), ACT_DTYPE), jax.ShapeDtypeStruct(bm.shape, ACT_DTYPE), jax.ShapeDtypeStruct(cm.shape, ACT_DTYPE),
                   jax.ShapeDtypeStruct((s, di), F32), jax.ShapeDtypeStruct((s, di), F32), jax.ShapeDtypeStruct((1, di), F32)] + r_shapes,
        scratch_shapes=[pltpu.VMEM((SSM_STATE, gw), F32)] + r_scratch,
        compiler_params=_params(("arbitrary", "arbitrary") if rider else ("parallel", "arbitrary")),
    )(xs, bm, cm, dtx, acx, acum, acum_t, d_x, states, dy, *r_args)
    return list(outs[:6]), list(outs[6:])


def _ssd_gate(y, z, w, gs, *, name):
    s, d = y.shape

    def fn(i, nrt, yv, zv, wv):
        zv = zv.astype(F32)
        u = yv.astype(F32) * zv * _sigmoid(zv)
        return (u * lax.rsqrt(_gmean(u * u, gs) + RMS_EPS) * wv,), ()

    (o,) = _rows(fn, name=name, s=s, tm=_tile(s, (256, 128)), ins=[("row", y, d, _c0), ("row", z, d, _c0), ("full", w.reshape(1, d))],
                 outs=[(d, d, _c0, ACT_DTYPE)])
    return o


def _ssd_gate_bwd(y, z, w, do, gs, *, name):
    s, d = y.shape

    def fn(i, nrt, yv, zv, wv, dov):
        yv, zv, dov = yv.astype(F32), zv.astype(F32), dov.astype(F32)
        sg = _sigmoid(zv)
        sl = zv * sg
        u = yv * sl
        r = lax.rsqrt(_gmean(u * u, gs) + RMS_EPS)
        uh = u * r
        g = dov * wv
        du = r * (g - uh * _gmean(g * uh, gs))
        return (du * sl, du * yv * sg * (1.0 + zv * (1.0 - sg))), (_sum_rows(dov * uh),)

    dy, dz, dw = _rows(fn, name=name, s=s, tm=_tile(s, (256, 128)),
                       ins=[("row", y, d, _c0), ("row", z, d, _c0), ("full", w.reshape(1, d)), ("row", do, d, _c0)],
                       outs=[(d, d, _c0, ACT_DTYPE), (d, d, _c0, ACT_DTYPE)], accs=[(1, d, d, _c0)])
    return dy, dz, dw.reshape(d)


def _pad_lanes(w):
    return jnp.pad(w, ((0, 0), (0, LANES - w.shape[1])))


def _nt_sum(pairs, name):
    acc = None
    for a, b in pairs:
        acc = _mm(a, b, tb=True, add=acc, name=name)
    return acc


def _conv_mixer_fwd(h, w_in, layer, w_dw, tag):
    d, ns = h.shape[1], w_in.shape[3]
    inter = (3, _tile(math.gcd(d, ns), TILES))
    p = _mm(h, w_in, b_layer=layer, inter=inter, out_dtype=ACT_DTYPE, name=f"{tag}_in")
    return _gconv_fwd(p, w_dw, inter[1], name=f"{tag}_gate"), (h, w_in, layer, inter, p, w_dw)


def _conv_mixer_bwd(cache, do, tag):
    h, w_in, layer, inter, p, w_dw = cache
    dp, dw_dw = _gconv_bwd(p, w_dw, do, inter[1], name=f"{tag}_gate_bwd")
    dw_in = _mm(h, dp, ta=True, out_shard=w_in.shape[3], inter=inter, out_dtype=WIRE_DTYPE, name=f"{tag}_dw_in")
    dh = _mm(dp, w_in, tb=True, b_layer=layer, inter=inter, name=f"{tag}_dh")
    return dh, {"w_in": dw_in, "w_dw": dw_dw}


def _fox_mixer_fwd(h, w_in, b_f, q_gain, k_gain, tag, rider=None):
    d = h.shape[1]
    nh = d // HEAD
    ws = [w_in[:, k * d:(k + 1) * d] for k in range(3)] + [_pad_lanes(w_in[:, 3 * d:])]
    q, k, v = [_mm(h, w, out_dtype=ACT_DTYPE, name=f"{tag}_in") for w in ws[:3]]
    f = _mm(h, ws[3], name=f"{tag}_in_f")
    gq = jnp.tile(q_gain, nh).reshape(1, d)
    gk = jnp.tile(k_gain, nh).reshape(1, d)
    bf = _pad_lanes(b_f.reshape(1, nh))
    qs, kn, logf = _fox_prep(q, k, f, gq, gk, bf, name=f"{tag}_prep")
    cum = _cumsum_rows(logf, reverse=False, name=f"{tag}_cum")
    aug = _fox_aug(qs, kn, cum, name=f"{tag}_aug")
    q_aug, k_aug = aug[:2], aug[2:]
    o, lse, landed = _fox_fwd_t(q_aug, k_aug, v, name=f"{tag}_attn", rider=rider)
    return o, (h, ws, q, k, v, f, gq, gk, bf, q_aug, k_aug, o, lse), landed


def _fox_mixer_bwd(cache, do, tag, rider=None):
    h, ws, q, k, v, f, gq, gk, bf, q_aug, k_aug, o, lse = cache
    s, d = q.shape
    nh = d // HEAD
    dd = _pair_rows(_fox_dd(do, o, name=f"{tag}_attn_dd"), nh)
    (dqs, dkn, dv, dcol, drow), landed = _fox_bwd_t(q_aug, k_aug, v, lse, dd, do, name=f"{tag}_attn_bwd", rider=rider)
    dcum = _pad_lanes(drow[:, :2, :].reshape(nh, s).T - dcol[:, ::HEAD])
    dlogf = _cumsum_rows(dcum, reverse=True, name=f"{tag}_cum_bwd")
    dq, dk, df, dgq, dgk, dbf = _fox_prep_bwd(q, k, f, gq, gk, bf, dqs, dkn, dlogf, name=f"{tag}_prep_bwd")
    dps = (dq, dk, dv, df)
    dws = [_mm(h, dp, ta=True, name=f"{tag}_dw_in") for dp in dps]
    dw_in = jnp.concatenate(dws[:3] + [dws[3][:, :nh]], axis=1)
    dh = _nt_sum(list(zip(dps, ws)), f"{tag}_dh")
    return dh, {"w_in": dw_in, "b_f": dbf[0, :nh], "q_gain": dgq.reshape(nh, HEAD).sum(0), "k_gain": dgk.reshape(nh, HEAD).sum(0)}, landed


def _ssd_mixer_fwd(h, w_in, conv_w, conv_b, dt_bias, a_log, d_skip, norm_w, tag):
    di = norm_w.shape[0]
    nh = di // HEAD
    gn = (conv_w.shape[1] - di) // 2
    cuts = [0, di, 2 * di, 2 * di + gn, 2 * di + 2 * gn]
    ws = [w_in[:, cuts[k]:cuts[k + 1]] for k in range(4)] + [_pad_lanes(w_in[:, cuts[4]:])]
    z, xr, br, cr = [_mm(h, w, out_dtype=ACT_DTYPE, name=f"{tag}_in") for w in ws[:4]]
    dtr = _mm(h, ws[4], name=f"{tag}_in_dt")
    ccuts = [0, di, di + gn, di + 2 * gn]
    cws = [conv_w[:, ccuts[k]:ccuts[k + 1]] for k in range(3)]
    cbs = [conv_b[ccuts[k]:ccuts[k + 1]] for k in range(3)]
    xs, bm, cm = [_sconv_fwd(r, w, b, name=f"{tag}_conv") for r, w, b in zip((xr, br, cr), cws, cbs)]
    dtb = _pad_lanes(dt_bias.reshape(1, nh))
    alg = _pad_lanes(a_log.reshape(1, nh))
    _dt, acum, dtx, acx = _ssd_prep(dtr, dtb, alg, nh, name=f"{tag}_prep")
    acum_t = acum[:, :nh].T
    d_x = jnp.repeat(d_skip, HEAD).reshape(1, di)
    y, states = _ssd_scan_fwd(xs, bm, cm, dtx, acx, acum, acum_t, d_x, name=f"{tag}_scan")
    gs = di // (gn // SSM_STATE)
    o = _ssd_gate(y, z, norm_w, gs, name=f"{tag}_gate")
    return o, (h, ws, z, (xr, br, cr), dtr, cws, cbs, xs, bm, cm, dtb, alg, dtx, acx, acum, acum_t, d_x, states, y, norm_w, gs, nh)


def _ssd_mixer_bwd(cache, do, tag, rider=None):
    h, ws, z, raws, dtr, cws, cbs, xs, bm, cm, dtb, alg, dtx, acx, acum, acum_t, d_x, states, y, norm_w, gs, nh = cache
    dy, dz, dnorm = _ssd_gate_bwd(y, z, norm_w, do, gs, name=f"{tag}_gate_bwd")
    (dxs, dbm, dcm, ddtx, dacx, dd_x), landed = _ssd_scan_bwd(xs, bm, cm, dtx, acx, acum, acum_t, d_x, states, dy, name=f"{tag}_scan_bwd",
                                                             rider=rider)
    ddtr, ddtb, dalg = _ssd_prep_bwd(dtr, dtb, alg, ddtx, dacx, nh, name=f"{tag}_prep_bwd")
    conv = [_sconv_bwd(r, w, b, da, name=f"{tag}_conv_bwd") for r, w, b, da in zip(raws, cws, cbs, (dxs, dbm, dcm))]
    dps = (dz, conv[0][0], conv[1][0], conv[2][0], ddtr)
    dws = [_mm(h, dp, ta=True, name=f"{tag}_dw_in") for dp in dps]
    dw_in = jnp.concatenate(dws[:4] + [dws[4][:, :nh]], axis=1)
    dh = _nt_sum(list(zip(dps, ws)), f"{tag}_dh")
    return dh, {"w_in": dw_in, "conv_w": jnp.concatenate([c[1] for c in conv], axis=1), "conv_b": jnp.concatenate([c[2] for c in conv]),
                "dt_bias": ddtb[0, :nh], "a_log": dalg[0, :nh], "d": dd_x.reshape(nh, HEAD).sum(1), "norm_w": dnorm}, landed


def _rows_natural(cm, layer):
    return cm[:, layer].reshape(-1, cm.shape[3])


def _cols_natural(cm, layer):
    return jnp.moveaxis(cm[:, layer], 0, 1).reshape(cm.shape[2], -1)


def _cols_chip_major(g):
    return jnp.moveaxis(g.reshape(g.shape[0], N_CHIPS, -1), 1, 0).astype(WIRE_DTYPE)


MIXERS = ("conv", "fox", "ssd")


def _model_layer(name, l):
    return l if name.startswith("ffn") else 3 * l + MIXERS.index(name.split("_")[0])


def _piece(pieces, layer):
    for arr, start in pieces:
        if start <= layer < start + arr.shape[1]:
            return arr, layer - start
    raise KeyError(layer)


def _with_own(landed, shards, chip):
    return [lax.dynamic_update_slice(g, w[None], (chip, 0, 0, 0)) for g, w in zip(landed, shards)]


def _reduce_begin(gs, place):
    from_sibling = _swap_halves_list(gs, name="reduce_halves")
    return [_add_half(g, r, place, name="reduce_add_sibling") for g, r in zip(gs, from_sibling)]


def _reduce_end(by_chip, chip_sums, place):
    reds = [_sum_chips(b, s, place, name="reduce_sum_chips") for b, s in zip(by_chip, chip_sums)]
    return _join_halves_list(reds, name="reduce_share")


def _local_step(x, tgt, fw, cm, late, place):
    depth = fw["mix_norm"].shape[0]
    chip = place[0]
    cm = {n: list(p) for n, p in cm.items()}
    layers = []
    xc, y_prev = x, None
    for i in range(depth):
        kind, j = i % 3, i // 3
        tag = f"l{i}"
        xin, h = _resid_rms(xc, y_prev, fw["mix_norm"][i], name=f"{tag}_norm1")
        if kind == 0:
            w_in, jl = _piece(cm["conv_w_in"], j)
            o, mc = _conv_mixer_fwd(h, w_in, jl, fw["conv_w_dw"][j], tag + "_conv")
        elif kind == 1:
            rider = _gather_ici_rider([late[n][0] for n in late]) if late else None
            o, mc, landed = _fox_mixer_fwd(h, _cols_natural(*_piece(cm["fox_w_in"], j)), fw["fox_b_f"][j], fw["fox_q_gain"][j],
                                           fw["fox_k_gain"][j], tag + "_fox", rider=rider)
            if late:
                landed = _with_own(_forward_halves_list(landed, name="gather_late_forward"), [late[n][0] for n in late], chip)
                for n, arr in zip(late, landed):
                    cm[n].append((arr, late[n][1]))
                late = {}
        else:
            o, mc = _ssd_mixer_fwd(h, _cols_natural(*_piece(cm["ssd_w_in"], j)), fw["ssd_conv_w"][j], fw["ssd_conv_b"][j], fw["ssd_dt_bias"][j],
                                   fw["ssd_a_log"][j], fw["ssd_d"][j], fw["ssd_norm_w"][j], tag + "_ssd")
        w_out = _rows_natural(*_piece(cm[MIXERS[kind] + "_w_out"], j))
        ym = _mm(o, w_out, name=f"{tag}_mix_out")
        x1, h2 = _resid_rms(xin, ym, fw["ffn_norm"][i], name=f"{tag}_norm2")
        w_gu, il = _piece(cm["ffn_w_gu"], i)
        w_down = _rows_natural(*_piece(cm["ffn_w_down"], i))
        inter = (2, _tile(math.gcd(w_down.shape[0], w_gu.shape[3]), TILES))
        gu = _mm(h2, w_gu, b_layer=il, inter=inter, out_dtype=ACT_DTYPE, name=f"{tag}_ffn_gu")
        a = _swiglu_fwd(gu, inter[1], name=f"{tag}_swiglu")
        yf = _mm(a, w_down, name=f"{tag}_ffn_down")
        layers.append((xin, o, mc, w_out, x1, h2, w_gu, il, w_down, inter, gu, a))
        xc, y_prev = x1, yf
    loss, dx, dxb = _loss_head(xc, y_prev, tgt, name="loss_head")

    small = {k: [None] * v.shape[0] for k, v in fw.items()}
    reds = {}
    riding = None
    at_end = []
    for i in reversed(range(depth)):
        kind, j = i % 3, i // 3
        tag = f"l{i}"
        xin, o, mc, w_out, x1, h2, w_gu, il, w_down, inter, gu, a = layers[i]
        mine = [(("ffn_w_down", i), _mm(a, dxb, ta=True, out_dtype=WIRE_DTYPE, name=f"{tag}_dw_down").reshape(N_CHIPS, -1, w_down.shape[1]))]
        da = _mm(dxb, w_down, tb=True, out_dtype=ACT_DTYPE, name=f"{tag}_da")
        dgu = _swiglu_bwd(gu, da, inter[1], name=f"{tag}_swiglu_bwd")
        mine.append((("ffn_w_gu", i), _mm(h2, dgu, ta=True, out_shard=w_gu.shape[3], inter=inter, out_dtype=WIRE_DTYPE, name=f"{tag}_dw_gu")))
        dh2 = _mm(dgu, w_gu, tb=True, b_layer=il, inter=inter, name=f"{tag}_dh2")
        dx1, dx1b, small["ffn_norm"][i] = _rms_bwd(x1, fw["ffn_norm"][i], dh2, dx, name=f"{tag}_norm2_bwd")
        mine.append(((MIXERS[kind] + "_w_out", j),
                     _mm(o, dx1b, ta=True, out_dtype=WIRE_DTYPE, name=f"{tag}_dw_out").reshape(N_CHIPS, -1, w_out.shape[1])))
        do = _mm(dx1b, w_out, tb=True, out_dtype=ACT_DTYPE, name=f"{tag}_do")
        rider = riding[2] if riding and kind != 0 else None
        if kind == 0:
            dh, mg = _conv_mixer_bwd(mc, do, tag + "_conv")
        elif kind == 1:
            dh, mg, landed = _fox_mixer_bwd(mc, do, tag + "_fox", rider=rider)
        else:
            dh, mg, landed = _ssd_mixer_bwd(mc, do, tag + "_ssd", rider=rider)
        if rider is not None:
            reds.update(zip(riding[0], _reduce_end(landed, riding[1], place)))
            riding = None
        for k, v in mg.items():
            if k == "w_in":
                mine.append(((f"{MIXERS[kind]}_w_in", j), v if kind == 0 else _cols_chip_major(v)))
            else:
                small[f"{MIXERS[kind]}_{k}"][j] = v
        dx, dxb, small["mix_norm"][i] = _rms_bwd(xin, fw["mix_norm"][i], dh, dx1, name=f"{tag}_norm1_bwd")
        if i > 0 and (i - 1) % 3 != 0 and riding is None:
            chip_sums = _reduce_begin([g for _, g in mine], place)
            riding = ([k for k, _ in mine], chip_sums, _scatter_rider(chip_sums))
        else:
            at_end += mine
    assert riding is None
    chip_sums = _reduce_begin([g for _, g in at_end], place)
    by_chip = _run_rider(_scatter_rider(chip_sums), name="reduce_chips")
    reds.update(zip([k for k, _ in at_end], _reduce_end(by_chip, chip_sums, place)))
    return loss, dx, {k: jnp.stack(v) for k, v in small.items()}, reds


ANY = pl.BlockSpec(memory_space=pl.ANY)
VMEM_SPEC = pl.BlockSpec(memory_space=pltpu.VMEM)


def _place():
    return lax.axis_index("x"), lax.axis_index("y"), lax.axis_index("c")


def _remote(src, dst, send_sems, recv_sems, k, to):
    return pltpu.make_async_remote_copy(src_ref=src, dst_ref=dst, send_sem=send_sems.at[k], recv_sem=recv_sems.at[k],
                                        device_id=to, device_id_type=MESH)


def _half_of(ref, h, shape):
    layers, rows, _ = shape
    if layers % 2 == 0:
        return ref.at[pl.ds(h * (layers // 2), layers // 2)]
    return ref.at[:, pl.ds(pl.multiple_of(h * (rows // 2), 16), rows // 2)]


def _row_half(ref, h, rows):
    return ref.at[:, pl.ds(pl.multiple_of(h * (rows // 2), 16), rows // 2)]


def _gather_list(ws, *, name):
    n = len(ws)

    def body(*refs):
        w_refs, o_refs, send_sems, recv_sems = refs[:n], refs[n:2 * n], refs[2 * n], refs[2 * n + 1]
        x, y, c = _place()
        me, sibling, m = (x, y, c), (x, y, 1 - c), 2 * x + y
        chips = [(1 - x, y), (x, 1 - y), (1 - x, 1 - y)]
        first, passed = [], []
        for p, (w_ref, o_ref) in enumerate(zip(w_refs, o_refs)):
            for j, (px, py) in enumerate(chips):
                cp = _remote(_half_of(w_ref, c, ws[p].shape), _half_of(o_ref.at[m], c, ws[p].shape), send_sems, recv_sems, 6 * p + j, (px, py, c))
                cp.start()
                first.append(cp)
        for p, o_ref in enumerate(o_refs):
            for j, (px, py) in enumerate(chips):
                blk = _half_of(o_ref.at[2 * px + py], c, ws[p].shape)
                _remote(blk, blk, send_sems, recv_sems, 6 * p + j, me).wait_recv()
                fwd = _remote(blk, blk, send_sems, recv_sems, 6 * p + 3 + j, sibling)
                fwd.start()
                passed.append(fwd)
        for p, o_ref in enumerate(o_refs):
            for j, (px, py) in enumerate(chips):
                blk = _half_of(o_ref.at[2 * px + py], 1 - c, ws[p].shape)
                _remote(blk, blk, send_sems, recv_sems, 6 * p + 3 + j, me).wait_recv()
        for cp in first + passed:
            cp.wait_send()

    return pl.pallas_call(
        body, name=name, in_specs=[ANY] * n, out_specs=[ANY] * n,
        out_shape=[jax.ShapeDtypeStruct((N_CHIPS,) + w.shape, w.dtype) for w in ws],
        scratch_shapes=[pltpu.SemaphoreType.DMA((6 * n,)), pltpu.SemaphoreType.DMA((6 * n,))],
    )(*ws)


class _Rider:
    def __init__(self, arrays, out_shapes, n_sems, start, finish):
        self.arrays, self.out_shapes, self.n_sems, self.start, self.finish = list(arrays), list(out_shapes), n_sems, start, finish

    @property
    def scratch(self):
        return [pltpu.SemaphoreType.DMA((self.n_sems,)), pltpu.SemaphoreType.DMA((self.n_sems,))]


def _carry(body, rider, n_prefetch, n_in, n_out, grid):
    if rider is None:
        return body
    ri, ro = len(rider.arrays), len(rider.out_shapes)

    def wrapped(*refs):
        pre, rest = refs[:n_prefetch], refs[n_prefetch:]
        ins, r_in = rest[:n_in], rest[n_in:n_in + ri]
        outs, r_out = rest[n_in + ri:n_in + ri + n_out], rest[n_in + ri + n_out:n_in + ri + n_out + ro]
        scratch = rest[n_in + ri + n_out + ro:]
        first = functools.reduce(jnp.logical_and, [pl.program_id(a) == 0 for a in range(len(grid))])
        last = functools.reduce(jnp.logical_and, [pl.program_id(a) == g - 1 for a, g in enumerate(grid)])

        @pl.when(first)
        def _():
            rider.start(r_in, r_out, scratch[-2], scratch[-1])

        body(*pre, *ins, *outs, *scratch[:-2])

        @pl.when(last)
        def _():
            rider.finish(r_in, r_out, scratch[-2], scratch[-1])

    return wrapped


def _run_rider(rider, *, name):
    n = len(rider.arrays)

    def body(*refs):
        rider.start(refs[:n], refs[n:2 * n], refs[-2], refs[-1])
        rider.finish(refs[:n], refs[n:2 * n], refs[-2], refs[-1])

    return pl.pallas_call(body, name=name, in_specs=[ANY] * n, out_specs=[ANY] * len(rider.out_shapes), out_shape=rider.out_shapes,
                          scratch_shapes=rider.scratch)(*rider.arrays)


def _chips_of(x, y):
    return [(1 - x, y), (x, 1 - y), (1 - x, 1 - y)]


def _gather_ici_rider(ws):
    def copies(w_refs, o_refs, send_sems, recv_sems):
        x, y, c = _place()
        m = 2 * x + y
        return [_remote(_half_of(w_ref, c, ws[p].shape), _half_of(o_ref.at[m], c, ws[p].shape), send_sems, recv_sems, 3 * p + j, (px, py, c))
                for p, (w_ref, o_ref) in enumerate(zip(w_refs, o_refs)) for j, (px, py) in enumerate(_chips_of(x, y))]

    def start(w_refs, o_refs, send_sems, recv_sems):
        for cp in copies(w_refs, o_refs, send_sems, recv_sems):
            cp.start()

    def finish(w_refs, o_refs, send_sems, recv_sems):
        x, y, c = _place()
        for p, o_ref in enumerate(o_refs):
            for j, (px, py) in enumerate(_chips_of(x, y)):
                blk = _half_of(o_ref.at[2 * px + py], c, ws[p].shape)
                _remote(blk, blk, send_sems, recv_sems, 3 * p + j, (x, y, c)).wait_recv()
        for cp in copies(w_refs, o_refs, send_sems, recv_sems):
            cp.wait_send()

    return _Rider(ws, [jax.ShapeDtypeStruct((N_CHIPS,) + w.shape, w.dtype) for w in ws], 3 * len(ws), start, finish)


def _forward_halves_list(gathered, *, name):
    n = len(gathered)
    shapes = [g.shape[1:] for g in gathered]

    def body(*refs):
        o_refs, send_sems, recv_sems = refs[n:2 * n], refs[2 * n], refs[2 * n + 1]
        x, y, c = _place()
        cps = []
        for p, o_ref in enumerate(o_refs):
            for j, (px, py) in enumerate(_chips_of(x, y)):
                blk = _half_of(o_ref.at[2 * px + py], c, shapes[p])
                cp = _remote(blk, blk, send_sems, recv_sems, 3 * p + j, (x, y, 1 - c))
                cp.start()
                cps.append(cp)
        for p, o_ref in enumerate(o_refs):
            for j, (px, py) in enumerate(_chips_of(x, y)):
                blk = _half_of(o_ref.at[2 * px + py], 1 - c, shapes[p])
                _remote(blk, blk, send_sems, recv_sems, 3 * p + j, (x, y, c)).wait_recv()
        for cp in cps:
            cp.wait_send()

    return pl.pallas_call(
        body, name=name, in_specs=[ANY] * n, out_specs=[ANY] * n, out_shape=[jax.ShapeDtypeStruct(g.shape, g.dtype) for g in gathered],
        input_output_aliases={p: p for p in range(n)},
        scratch_shapes=[pltpu.SemaphoreType.DMA((3 * n,)), pltpu.SemaphoreType.DMA((3 * n,))],
    )(*gathered)


def _scatter_rider(sums):
    def copies(a_refs, o_refs, send_sems, recv_sems):
        x, y, c = _place()
        m = 2 * x + y
        return [_remote(a_ref.at[2 * px + py], o_ref.at[m], send_sems, recv_sems, 3 * p + j, (px, py, c))
                for p, (a_ref, o_ref) in enumerate(zip(a_refs, o_refs)) for j, (px, py) in enumerate(_chips_of(x, y))]

    def start(a_refs, o_refs, send_sems, recv_sems):
        for cp in copies(a_refs, o_refs, send_sems, recv_sems):
            cp.start()

    def finish(a_refs, o_refs, send_sems, recv_sems):
        x, y, c = _place()
        for p, o_ref in enumerate(o_refs):
            for j, (px, py) in enumerate(_chips_of(x, y)):
                blk = o_ref.at[2 * px + py]
                _remote(blk, blk, send_sems, recv_sems, 3 * p + j, (x, y, c)).wait_recv()
        for cp in copies(a_refs, o_refs, send_sems, recv_sems):
            cp.wait_send()

    return _Rider(sums, [jax.ShapeDtypeStruct(a.shape, a.dtype) for a in sums], 3 * len(sums), start, finish)


def _gather_small(v, *, name):
    r, w = v.shape

    def body(v_ref, o_ref, send_sems, recv_sems):
        x, y, c = _place()
        m = 2 * x + y
        chips = [(1 - x, y), (x, 1 - y), (1 - x, 1 - y)]
        o_ref[m] = v_ref[...]
        sends = [_remote(v_ref, o_ref.at[m], send_sems, recv_sems, j, (px, py, c)) for j, (px, py) in enumerate(chips)]
        for cp in sends:
            cp.start()
        for j, (px, py) in enumerate(chips):
            blk = o_ref.at[2 * px + py]
            _remote(blk, blk, send_sems, recv_sems, j, (x, y, c)).wait_recv()
        for cp in sends:
            cp.wait_send()

    return pl.pallas_call(
        body, name=name, in_specs=[VMEM_SPEC], out_specs=VMEM_SPEC, out_shape=jax.ShapeDtypeStruct((4, r, w), v.dtype),
        scratch_shapes=[pltpu.SemaphoreType.DMA((3,)), pltpu.SemaphoreType.DMA((3,))],
    )(v)


def _swap_halves_list(gs, *, name):
    n = len(gs)

    def body(*refs):
        g_refs, o_refs, send_sems, recv_sems = refs[:n], refs[n:2 * n], refs[2 * n], refs[2 * n + 1]
        x, y, c = _place()
        cps = [_remote(_row_half(g_ref, 1 - c, gs[p].shape[1]), o_ref, send_sems, recv_sems, p, (x, y, 1 - c))
               for p, (g_ref, o_ref) in enumerate(zip(g_refs, o_refs))]
        for cp in cps:
            cp.start()
        for cp in cps:
            cp.wait()

    return pl.pallas_call(
        body, name=name, in_specs=[ANY] * n, out_specs=[ANY] * n,
        out_shape=[jax.ShapeDtypeStruct((g.shape[0], g.shape[1] // 2, g.shape[2]), g.dtype) for g in gs],
        scratch_shapes=[pltpu.SemaphoreType.DMA((n,)), pltpu.SemaphoreType.DMA((n,))],
    )(*gs)


def _join_halves_list(reds, *, name):
    n = len(reds)

    def body(*refs):
        o_refs, send_sems, recv_sems = refs[n:2 * n], refs[2 * n], refs[2 * n + 1]
        x, y, c = _place()
        cps = []
        for p, o_ref in enumerate(o_refs):
            rh = reds[p].shape[0] // 2
            mine = o_ref.at[pl.ds(pl.multiple_of(c * rh, SUBLANES), rh)]
            cp = _remote(mine, mine, send_sems, recv_sems, p, (x, y, 1 - c))
            cp.start()
            cps.append(cp)
        for p, o_ref in enumerate(o_refs):
            rh = reds[p].shape[0] // 2
            other = o_ref.at[pl.ds(pl.multiple_of((1 - c) * rh, SUBLANES), rh)]
            _remote(other, other, send_sems, recv_sems, p, (x, y, c)).wait_recv()
        for cp in cps:
            cp.wait_send()

    return pl.pallas_call(
        body, name=name, in_specs=[ANY] * n, out_specs=[ANY] * n, out_shape=[jax.ShapeDtypeStruct(r.shape, r.dtype) for r in reds],
        input_output_aliases={p: p for p in range(n)},
        scratch_shapes=[pltpu.SemaphoreType.DMA((n,)), pltpu.SemaphoreType.DMA((n,))],
    )(*reds)


def _allreduce_small(v, *, name):
    r, w = v.shape

    def body(v_ref, o_ref, slots, send_sems, recv_sems):
        x, y, c = _place()
        me = 4 * x + 2 * y + c
        slots[me] = v_ref[...]
        peers = [((1 - x) if k & 4 else x, (1 - y) if k & 2 else y, (1 - c) if k & 1 else c) for k in range(1, 8)]
        sends = [_remote(v_ref, slots.at[me], send_sems, recv_sems, k, p) for k, p in enumerate(peers)]
        for cp in sends:
            cp.start()
        for k, (px, py, pc) in enumerate(peers):
            blk = slots.at[4 * px + 2 * py + pc]
            _remote(blk, blk, send_sems, recv_sems, k, (x, y, c)).wait_recv()
        for cp in sends:
            cp.wait_send()
        acc = slots[0]
        for k in range(1, 8):
            acc = acc + slots[k]
        o_ref[...] = acc

    return pl.pallas_call(
        body, name=name, in_specs=[VMEM_SPEC], out_specs=VMEM_SPEC, out_shape=jax.ShapeDtypeStruct(v.shape, v.dtype),
        scratch_shapes=[pltpu.VMEM((8, r, w), F32), pltpu.SemaphoreType.DMA((7,)), pltpu.SemaphoreType.DMA((7,))],
    )(v)


def _row_tile(r):
    return r if r <= 512 else _tile(r, (512, 256, 128, 64, 32, 16))


def _add_half(g, recv, place, *, name):
    n, r, w = g.shape
    tm = _row_tile(r // 2)
    nb = (r // 2) // tm

    def body(place_ref, g_ref, r_ref, o_ref):
        o_ref[...] = (g_ref[...].astype(F32) + r_ref[...].astype(F32)).astype(o_ref.dtype)

    return pl.pallas_call(
        body, name=name,
        grid_spec=pltpu.PrefetchScalarGridSpec(
            num_scalar_prefetch=1, grid=(n, nb),
            in_specs=[pl.BlockSpec((None, tm, w), lambda k, i, p: (k, p[1] * nb + i, 0)), pl.BlockSpec((None, tm, w), lambda k, i, p: (k, i, 0))],
            out_specs=pl.BlockSpec((None, tm, w), lambda k, i, p: (k, i, 0))),
        out_shape=jax.ShapeDtypeStruct(recv.shape, g.dtype), compiler_params=_params(("parallel", "parallel")),
    )(place, g, recv)


def _sum_chips(recv, own, place, *, name):
    n, r, w = recv.shape
    tm = _row_tile(r)
    nb = r // tm

    def body(place_ref, *refs):
        own_ref, o_ref = refs[n], refs[n + 1]
        acc = None
        for k in range(n):
            term = jnp.where(place_ref[0] == k, own_ref[...], refs[k][...]).astype(F32)
            acc = term if acc is None else acc + term
        o_ref[...] = acc

    recv_specs = [pl.BlockSpec((None, tm, w), lambda i, p, k=k: (jnp.where(p[0] == k, (k + 1) % n, k), i, 0)) for k in range(n)]
    return pl.pallas_call(
        body, name=name,
        grid_spec=pltpu.PrefetchScalarGridSpec(
            num_scalar_prefetch=1, grid=(nb,),
            in_specs=recv_specs + [pl.BlockSpec((None, tm, w), lambda i, p: (p[0], i, 0))],
            out_specs=pl.BlockSpec((tm, w), lambda i, p: (p[1] * nb + i, 0))),
        out_shape=jax.ShapeDtypeStruct((2 * r, w), F32), compiler_params=_params(("parallel",)),
    )(place, *([recv] * n), own)


def _adamw(w, g, m, v, *, name):
    shape = w.shape
    cols = shape[-1]
    rows = math.prod(shape[:-1])
    tm = _tile(rows, (256, 128, 64, 32, 16, 8))
    c1 = 1.0 - ADAM_B1 ** ADAM_STEP
    c2 = 1.0 - ADAM_B2 ** ADAM_STEP

    def fn(i, nrt, wv, gv, mv, vv):
        mn = ADAM_B1 * mv + (1.0 - ADAM_B1) * gv
        vn = ADAM_B2 * vv + (1.0 - ADAM_B2) * (gv * gv)
        delta = -ADAM_LR * ((mn / c1) / (jnp.sqrt(vn / c2) + ADAM_EPS) + ADAM_WD * wv)
        return (delta, mn, vn), ()

    outs = _rows(fn, name=name, s=rows, tm=tm, ins=[("row", t.reshape(rows, cols), cols, _c0) for t in (w, g, m, v)],
                 outs=[(cols, cols, _c0, F32)] * 3)
    return [o.reshape(shape) for o in outs]


WEIGHTS = ["mix_norm", "ffn_norm", "ffn_w_gu", "ffn_w_down", "conv_w_in", "conv_w_dw", "conv_w_out", "fox_w_in", "fox_b_f", "fox_q_gain",
           "fox_k_gain", "fox_w_out", "ssd_w_in", "ssd_conv_w", "ssd_conv_b", "ssd_dt_bias", "ssd_a_log", "ssd_d", "ssd_norm_w", "ssd_w_out"]
SHARD_AXIS = {"ffn_w_gu": 2, "ffn_w_down": 1, "conv_w_in": 2, "conv_w_dw": 2, "conv_w_out": 1, "fox_w_in": 2, "fox_w_out": 1, "ssd_w_in": 2,
              "ssd_conv_w": 2, "ssd_conv_b": 1, "ssd_norm_w": 1, "ssd_w_out": 1}
BIG = ["ffn_w_gu", "ffn_w_down", "conv_w_in", "conv_w_out", "fox_w_in", "fox_w_out", "ssd_w_in", "ssd_w_out"]
SMALL_SHARDED = ["conv_w_dw", "ssd_conv_w", "ssd_conv_b", "ssd_norm_w"]
N_CHIPS = 4


def _pack_flat(parts, pad_to):
    flat = [p.reshape(-1) for p in parts]
    offs, n = [], 0
    for f in flat:
        offs.append(n)
        n += f.shape[0]
    total = -(-n // pad_to) * pad_to
    if total > n:
        flat.append(jnp.zeros((total - n,), flat[0].dtype))
    return jnp.concatenate(flat).reshape(-1, LANES), offs


def kernel(x, mix_norm, ffn_norm, ffn_w_gu, ffn_w_down, conv_w_in, conv_w_dw, conv_w_out, fox_w_in, fox_b_f, fox_q_gain, fox_k_gain, fox_w_out, ssd_w_in, ssd_conv_w, ssd_conv_b, ssd_dt_bias, ssd_a_log, ssd_d, ssd_norm_w, ssd_w_out, loss_target, m_mix_norm, m_ffn_norm, m_ffn_w_gu, m_ffn_w_down, m_conv_w_in, m_conv_w_dw, m_conv_w_out, m_fox_w_in, m_fox_b_f, m_fox_q_gain, m_fox_k_gain, m_fox_w_out, m_ssd_w_in, m_ssd_conv_w, m_ssd_conv_b, m_ssd_dt_bias, m_ssd_a_log, m_ssd_d, m_ssd_norm_w, m_ssd_w_out, v_mix_norm, v_ffn_norm, v_ffn_w_gu, v_ffn_w_down, v_conv_w_in, v_conv_w_dw, v_conv_w_out, v_fox_w_in, v_fox_b_f, v_fox_q_gain, v_fox_k_gain, v_fox_w_out, v_ssd_w_in, v_ssd_conv_w, v_ssd_conv_b, v_ssd_dt_bias, v_ssd_a_log, v_ssd_d, v_ssd_norm_w, v_ssd_w_out):
    w = dict(zip(WEIGHTS, (mix_norm, ffn_norm, ffn_w_gu, ffn_w_down, conv_w_in, conv_w_dw, conv_w_out, fox_w_in, fox_b_f, fox_q_gain, fox_k_gain,
                           fox_w_out, ssd_w_in, ssd_conv_w, ssd_conv_b, ssd_dt_bias, ssd_a_log, ssd_d, ssd_norm_w, ssd_w_out)))
    m1 = dict(zip(WEIGHTS, (m_mix_norm, m_ffn_norm, m_ffn_w_gu, m_ffn_w_down, m_conv_w_in, m_conv_w_dw, m_conv_w_out, m_fox_w_in, m_fox_b_f,
                            m_fox_q_gain, m_fox_k_gain, m_fox_w_out, m_ssd_w_in, m_ssd_conv_w, m_ssd_conv_b, m_ssd_dt_bias, m_ssd_a_log, m_ssd_d,
                            m_ssd_norm_w, m_ssd_w_out)))
    m2 = dict(zip(WEIGHTS, (v_mix_norm, v_ffn_norm, v_ffn_w_gu, v_ffn_w_down, v_conv_w_in, v_conv_w_dw, v_conv_w_out, v_fox_w_in, v_fox_b_f,
                            v_fox_q_gain, v_fox_k_gain, v_fox_w_out, v_ssd_w_in, v_ssd_conv_w, v_ssd_conv_b, v_ssd_dt_bias, v_ssd_a_log, v_ssd_d,
                            v_ssd_norm_w, v_ssd_w_out)))
    cx, cy, cc = _place()
    chip = 2 * cx + cy

    place = jnp.stack([chip, cc]).astype(jnp.int32)
    depth = mix_norm.shape[0]
    carrier = next((i for i in range(depth) if i % 3 == 1), depth)
    early, late = {}, {}
    for n in BIG:
        wb = w[n].astype(WIRE_DTYPE)
        cut = sum(_model_layer(n, l) <= carrier for l in range(wb.shape[0]))
        if cut:
            early[n] = wb[:cut]
        if cut < wb.shape[0]:
            late[n] = (wb[cut:], cut)
    gathered = _with_own(_gather_list(list(early.values()), name="gather_weights"), list(early.values()), chip)
    cm = {n: [(g_, 0)] for n, g_ in zip(early, gathered)}
    for n in late:
        cm.setdefault(n, [])
    sp, soffs = _pack_flat([w[n] for n in SMALL_SHARDED], SUBLANES * LANES)
    sgath = _gather_small(sp, name="gather_small").reshape(N_CHIPS, -1)
    full = {n: w[n] for n in WEIGHTS if n not in SHARD_AXIS}
    for n, off in zip(SMALL_SHARDED, soffs):
        full[n] = jnp.concatenate([sgath[j, off:off + w[n].size].reshape(w[n].shape) for j in range(N_CHIPS)], axis=SHARD_AXIS[n])

    loss, gx, grads, reds = _local_step(x[0], loss_target[0], full, cm, late, place)
    loss = lax.psum(loss, ("x", "y", "c"))

    small_names = [n for n in WEIGHTS if n not in BIG]
    sm, smoffs = _pack_flat([grads[n] for n in small_names], SUBLANES * LANES)
    sred = _allreduce_small(sm, name="allreduce_small").reshape(-1)

    g = {n: jnp.stack([reds[(n, l)] for l in range(w[n].shape[0])]).reshape(w[n].shape) for n in BIG}
    for n, off in zip(small_names, smoffs):
        fullg = sred[off:off + grads[n].size].reshape(grads[n].shape)
        if n in SHARD_AXIS:
            ax = SHARD_AXIS[n]
            fullg = lax.dynamic_slice_in_dim(fullg, chip * w[n].shape[ax], w[n].shape[ax], axis=ax)
        g[n] = fullg

    deltas, new_m, new_v = [], [], []
    for n in WEIGHTS:
        dl, mn, vn = _adamw(w[n], g[n], m1[n], m2[n], name=f"adamw_{n}")
        deltas.append(dl)
        new_m.append(mn)
        new_v.append(vn)
    return (loss, gx[None], *[g[n] for n in WEIGHTS], *deltas, *new_m, *new_v)
```

```python
import functools
import math

import jax
import jax.numpy as jnp
from jax import lax
from jax.experimental import pallas as pl
from jax.experimental.pallas import tpu as pltpu

F32 = jnp.float32
MM_DTYPE = jnp.bfloat16
ACT_DTYPE = jnp.bfloat16
WIRE_DTYPE = jnp.bfloat16

RMS_EPS = 1e-6
HEAD = 64
SSM_STATE = 128
SSM_CHUNK = 128
LANES = 128
SUBLANES = 8
VMEM_LIMIT = 48 * 1024 * 1024

ADAM_LR, ADAM_B1, ADAM_B2, ADAM_EPS, ADAM_WD, ADAM_STEP = 0.001, 0.9, 0.999, 1e-08, 0.01, 10

HI = lax.Precision.HIGHEST
MESH = pl.DeviceIdType.MESH


def _tile(dim, prefs):
    for p in prefs:
        if dim % p == 0:
            return p
    return dim


def _params(sem):
    return pltpu.CompilerParams(dimension_semantics=sem, vmem_limit_bytes=VMEM_LIMIT)


def _sigmoid(x):
    return 1.0 / (1.0 + jnp.exp(-x))


def _softplus(x):
    return jnp.maximum(x, 0.0) + jnp.log(1.0 + jnp.exp(-jnp.abs(x)))


TILES = (1024, 1408, 768, 512, 256, 128)
MIN_STEP_WORK = 1 << 30


def _mm(a, b, *, ta=False, tb=False, add=None, out_dtype=F32, name, b_layer=None, out_shard=None, inter=None, rider=None):
    ka, m = (a.shape[0], a.shape[1]) if ta else (a.shape[1], a.shape[0])
    if b_layer is None:
        kb, n = (b.shape[1], b.shape[0]) if tb else (b.shape[0], b.shape[1])
        ns = None
    else:
        ns = b.shape[3]
        kb, n = (b.shape[0] * ns, b.shape[2]) if tb else (b.shape[2], b.shape[0] * ns)
    assert ka == kb, (a.shape, b.shape, ta, tb)
    k = ka
    tm = _tile(m, (1408, 1024, 512, 256, 128) if ta else (512, 256, 128))
    tn = _tile(n, TILES)
    tk = _tile(k, TILES)
    if inter:
        segs, bw = inter
        if tb:
            tk = bw
        else:
            tn = bw
        tps = ((k if tb else n) // bw) // segs
        col = lambda q: ((q % segs) * tps + q // segs) * bw
    else:
        col = lambda q: q * (tk if tb else tn)
    def vmem(tm_, tk_):
        out_b = jnp.dtype(out_dtype).itemsize * 2 + (8 if add is not None else 0) + 4
        return 2 * tk_ * (tm_ * a.dtype.itemsize + tn * b.dtype.itemsize) + tm_ * tn * out_b

    if ta:
        while tk * 2 <= k and k % (tk * 2) == 0 and tm * tn * tk < MIN_STEP_WORK and vmem(tm, tk * 2) < VMEM_LIMIT * 3 // 4:
            tk *= 2
    else:
        while tm * 2 <= m and m % (tm * 2) == 0 and tm * tn * tk < MIN_STEP_WORK and vmem(tm * 2, tk) < VMEM_LIMIT * 3 // 4:
            tm *= 2
    nk = k // tk
    a_spec = pl.BlockSpec((tk, tm), lambda i, j, q: (q, i)) if ta else pl.BlockSpec((tm, tk), lambda i, j, q: (i, q))
    if b_layer is None:
        b_spec = pl.BlockSpec((tn, tk), lambda i, j, q: (j, q)) if tb else pl.BlockSpec((tk, tn), lambda i, j, q: (q, j))
    elif tb:
        b_spec = pl.BlockSpec((None, None, tn, tk), lambda i, j, q: (col(q) // ns, b_layer, j, (col(q) % ns) // tk))
    else:
        b_spec = pl.BlockSpec((None, None, tk, tn), lambda i, j, q: (col(j) // ns, b_layer, q, (col(j) % ns) // tn))
    if out_shard:
        assert ta and add is None
        o_spec = pl.BlockSpec((None, tm, tn), lambda i, j, q: (col(j) // out_shard, i, (col(j) % out_shard) // tn))
        o_shape = jax.ShapeDtypeStruct((N_CHIPS, m, out_shard), out_dtype)
    else:
        o_spec = pl.BlockSpec((tm, tn), lambda i, j, q: (i, j))
        o_shape = jax.ShapeDtypeStruct((m, n), out_dtype)
    dims = (((0 if ta else 1,), (1 if tb else 0,)), ((), ()))
    has_add = add is not None

    def body(*refs):
        a_ref, b_ref = refs[0], refs[1]
        o_ref = refs[2 + has_add]
        p = lax.dot_general(a_ref[...].astype(MM_DTYPE), b_ref[...].astype(MM_DTYPE), dims, preferred_element_type=F32)

        def finish(acc):
            if has_add:
                acc = acc + refs[2][...].astype(F32)
            o_ref[...] = acc.astype(out_dtype)

        if nk == 1:
            finish(p)
        else:
            acc_ref = refs[3 + has_add]
            q = pl.program_id(2)

            @pl.when(q == 0)
            def _():
                acc_ref[...] = p

            @pl.when(q > 0)
            def _():
                acc_ref[...] += p

            @pl.when(q == nk - 1)
            def _():
                finish(acc_ref[...])

    args = [a, b] + ([add] if has_add else [])
    in_specs = [a_spec, b_spec] + ([o_spec] if has_add else [])
    grid = (m // tm, n // tn, nk)
    r_in, r_out, r_shapes, r_scratch, r_args = _rider_parts(rider)
    outs = pl.pallas_call(
        _carry(body, rider, 0, len(args), 1, grid), name=name, grid=grid, in_specs=in_specs + r_in, out_specs=[o_spec] + r_out,
        out_shape=[o_shape] + r_shapes, scratch_shapes=([pltpu.VMEM((tm, tn), F32)] if nk > 1 else []) + r_scratch,
        compiler_params=_params(("arbitrary",) * 3 if rider else ("parallel", "parallel", "arbitrary")),
    )(*args, *r_args)
    return (outs[0], list(outs[1:])) if rider else outs[0]


def _rows(fn, *, name, s, tm, ncol=1, ins, outs, accs=()):
    nrt = s // tm
    hb = tm // SUBLANES
    in_specs, args = [], []
    for spec in ins:
        kind, arr = spec[0], spec[1]
        if kind == "full":
            in_specs.append(pl.BlockSpec(arr.shape, lambda j, i: (0, 0)))
        elif kind == "col":
            _, _, bw, cmap = spec
            in_specs.append(pl.BlockSpec((arr.shape[0], bw), lambda j, i, cmap=cmap: (0, cmap(j))))
        elif kind == "row":
            _, _, bw, cmap = spec
            in_specs.append(pl.BlockSpec((tm, bw), lambda j, i, cmap=cmap: (i, cmap(j))))
        elif kind == "prev":
            _, _, bw, cmap = spec
            in_specs.append(pl.BlockSpec((SUBLANES, bw), lambda j, i, cmap=cmap: (jnp.maximum(i * hb - 1, 0), cmap(j))))
        elif kind == "next":
            _, _, bw, cmap = spec
            in_specs.append(pl.BlockSpec((SUBLANES, bw), lambda j, i, cmap=cmap: (jnp.minimum((i + 1) * hb, s // SUBLANES - 1), cmap(j))))
        else:
            raise ValueError(kind)
        args.append(arr)
    out_specs, out_shape = [], []
    for w, bw, cmap, dt in outs:
        out_specs.append(pl.BlockSpec((tm, bw), lambda j, i, cmap=cmap: (i, cmap(j))))
        out_shape.append(jax.ShapeDtypeStruct((s, w), dt))
    for r, w, bw, cmap in accs:
        out_specs.append(pl.BlockSpec((r, bw), lambda j, i, cmap=cmap: (0, cmap(j))))
        out_shape.append(jax.ShapeDtypeStruct((r, w), F32))
    n_in, n_out, n_acc = len(ins), len(outs), len(accs)

    def body(*refs):
        i = pl.program_id(1)
        vals = [r[...] for r in refs[:n_in]]
        o_vals, a_vals = fn(i, nrt, *vals)
        assert len(o_vals) == n_out and len(a_vals) == n_acc
        for r, v in zip(refs[n_in:n_in + n_out], o_vals):
            r[...] = v.astype(r.dtype)
        for r, v in zip(refs[n_in + n_out:], a_vals):
            @pl.when(i == 0)
            def _(r=r, v=v):
                r[...] = v.astype(F32)

            @pl.when(i > 0)
            def _(r=r, v=v):
                r[...] += v.astype(F32)

    res = pl.pallas_call(
        body, name=name, grid=(ncol, nrt), in_specs=in_specs, out_specs=out_specs, out_shape=out_shape,
        compiler_params=_params(("parallel", "arbitrary" if accs else "parallel")),
    )(*args)
    return res


def _c0(j):
    return 0


def _cj(j):
    return j


def _gmean(v, gs):
    w = v.shape[-1]
    tile = max(gs, LANES)
    r = lax.broadcasted_iota(jnp.int32, (tile, tile), 0) // gs
    c = lax.broadcasted_iota(jnp.int32, (tile, tile), 1) // gs
    g = jnp.where(r == c, 1.0 / gs, 0.0).astype(F32)
    parts = [jnp.dot(v[:, t * tile:(t + 1) * tile], g, precision=HI, preferred_element_type=F32) for t in range(w // tile)]
    return parts[0] if len(parts) == 1 else jnp.concatenate(parts, axis=1)


def _sum_rows(v):
    return jnp.sum(v, axis=0, keepdims=True)


def _resid_rms(x, y, w, *, name):
    s, d = x.shape
    has_y = y is not None

    def fn(i, nrt, *v):
        xv = v[0] + (v[1] if has_y else 0.0)
        wv = v[-1]
        r = lax.rsqrt(jnp.mean(xv * xv, axis=-1, keepdims=True) + RMS_EPS)
        return (xv, xv * r * wv), ()

    ins = [("row", x, d, _c0)] + ([("row", y, d, _c0)] if has_y else []) + [("full", w.reshape(1, d))]
    xn, h = _rows(fn, name=name, s=s, tm=_tile(s, (512, 256, 128)), ins=ins, outs=[(d, d, _c0, F32), (d, d, _c0, ACT_DTYPE)])
    return xn, h


def _rms_bwd(x, w, dh, dx_in, *, name):
    s, d = x.shape

    def fn(i, nrt, xv, wv, dhv, dxi):
        r = lax.rsqrt(jnp.mean(xv * xv, axis=-1, keepdims=True) + RMS_EPS)
        xh = xv * r
        g = dhv * wv
        dx = dxi + r * (g - xh * jnp.mean(g * xh, axis=-1, keepdims=True))
        return (dx, dx), (_sum_rows(dhv * xh),)

    dx, dxb, dw = _rows(fn, name=name, s=s, tm=_tile(s, (512, 256, 128)),
                        ins=[("row", x, d, _c0), ("full", w.reshape(1, d)), ("row", dh, d, _c0), ("row", dx_in, d, _c0)],
                        outs=[(d, d, _c0, F32), (d, d, _c0, MM_DTYPE)], accs=[(1, d, d, _c0)])
    return dx, dxb, dw.reshape(d)


def _swiglu_fwd(p, bw, *, name):
    s, f = p.shape[0], p.shape[1] // 2

    def fn(i, nrt, pv):
        gv, uv = pv[:, :bw].astype(F32), pv[:, bw:].astype(F32)
        return (gv * _sigmoid(gv) * uv,), ()

    (a,) = _rows(fn, name=name, s=s, tm=_tile(s, (512, 256, 128)), ncol=f // bw,
                 ins=[("row", p, 2 * bw, _cj)], outs=[(f, bw, _cj, ACT_DTYPE)])
    return a


def _swiglu_bwd(p, da, bw, *, name):
    s, f = da.shape

    def fn(i, nrt, pv, dav):
        gv, uv, dav = pv[:, :bw].astype(F32), pv[:, bw:].astype(F32), dav.astype(F32)
        sg = _sigmoid(gv)
        dg = dav * uv * sg * (1.0 + gv * (1.0 - sg))
        du = dav * gv * sg
        return (jnp.concatenate([dg, du], axis=1),), ()

    (dp,) = _rows(fn, name=name, s=s, tm=_tile(s, (512, 256, 128)), ncol=f // bw,
                  ins=[("row", p, 2 * bw, _cj), ("row", da, bw, _cj)], outs=[(2 * f, 2 * bw, _cj, ACT_DTYPE)])
    return dp


def _loss_head(x, y, tgt, *, name):
    s, d = x.shape

    def fn(i, nrt, xv, yv, tv):
        diff = xv + yv - tv
        part = 0.5 * jnp.sum(diff * diff) / d
        return (diff / d, diff / d), (jnp.full((1, LANES), part, F32),)

    dy, dyb, loss = _rows(fn, name=name, s=s, tm=_tile(s, (512, 256, 128)),
                          ins=[("row", x, d, _c0), ("row", y, d, _c0), ("row", tgt, d, _c0)],
                          outs=[(d, d, _c0, F32), (d, d, _c0, MM_DTYPE)], accs=[(1, LANES, LANES, _c0)])
    return loss[0, 0], dy, dyb


def _shift_down(ext, j, tm):
    src = pltpu.roll(ext, j, 0) if j else ext
    return src[SUBLANES:SUBLANES + tm]


def _shift_up(ext, j, tm):
    return ext[:tm] if j == 0 else pltpu.roll(ext, ext.shape[0] - j, 0)[:tm]


def _gconv_fwd(p, w, bw, *, name):
    s, d = p.shape[0], p.shape[1] // 3
    kw = w.shape[0]
    tm = _tile(s, (512, 256, 128))

    def fn(i, nrt, pv, pp, wv):
        pv, pp = pv.astype(F32), pp.astype(F32)
        cv = pv[:, bw:2 * bw] * pv[:, 2 * bw:]
        pcv = jnp.where(i == 0, 0.0, pp[:, bw:2 * bw] * pp[:, 2 * bw:])
        ext = jnp.concatenate([pcv, cv], axis=0)
        u = sum(wv[k:k + 1, :] * _shift_down(ext, kw - 1 - k, tm) for k in range(kw))
        return (pv[:, :bw] * u,), ()

    (o,) = _rows(fn, name=name, s=s, tm=tm, ncol=d // bw, ins=[("row", p, 3 * bw, _cj), ("prev", p, 3 * bw, _cj), ("col", w, bw, _cj)],
                 outs=[(d, bw, _cj, ACT_DTYPE)])
    return o


def _gconv_bwd(p, w, do, bw, *, name):
    s, d = do.shape
    kw = w.shape[0]
    tm = _tile(s, (512, 256, 128))

    def fn(i, nrt, pv, pp, pn, dov, ndo, wv):
        pv, pp, dov = pv.astype(F32), pp.astype(F32), dov.astype(F32)
        bv, cv_, vv = pv[:, :bw], pv[:, bw:2 * bw], pv[:, 2 * bw:]
        cv = cv_ * vv
        pcv = jnp.where(i == 0, 0.0, pp[:, bw:2 * bw] * pp[:, 2 * bw:])
        ext = jnp.concatenate([pcv, cv], axis=0)
        shifted = [_shift_down(ext, kw - 1 - k, tm) for k in range(kw)]
        u = sum(wv[k:k + 1, :] * shifted[k] for k in range(kw))
        db = dov * u
        du = dov * bv
        ndu = jnp.where(i == nrt - 1, 0.0, ndo.astype(F32) * pn[:, :bw].astype(F32))
        ext2 = jnp.concatenate([du, ndu], axis=0)
        dcv = sum(wv[k:k + 1, :] * _shift_up(ext2, kw - 1 - k, tm) for k in range(kw))
        dw = jnp.concatenate([_sum_rows(du * shifted[k]) for k in range(kw)], axis=0)
        return (jnp.concatenate([db, dcv * vv, dcv * cv_], axis=1),), (dw,)

    dp, dw = _rows(fn, name=name, s=s, tm=tm, ncol=d // bw,
                   ins=[("row", p, 3 * bw, _cj), ("prev", p, 3 * bw, _cj), ("next", p, 3 * bw, _cj), ("row", do, bw, _cj),
                        ("next", do, bw, _cj), ("col", w, bw, _cj)],
                   outs=[(3 * d, 3 * bw, _cj, ACT_DTYPE)], accs=[(kw, d, bw, _cj)])
    return dp, dw


def _sconv_fwd(x, w, bias, *, name):
    s, d = x.shape
    kw = w.shape[0]
    bw = _tile(d, (512, 256, 128))
    tm = _tile(s, (512, 256, 128))

    def fn(i, nrt, xv, px, wv, bsv):
        xv = xv.astype(F32)
        ext = jnp.concatenate([jnp.where(i == 0, 0.0, px.astype(F32)), xv], axis=0)
        pre = sum(wv[k:k + 1, :] * _shift_down(ext, kw - 1 - k, tm) for k in range(kw)) + bsv
        return (pre * _sigmoid(pre),), ()

    (o,) = _rows(fn, name=name, s=s, tm=tm, ncol=d // bw,
                 ins=[("row", x, bw, _cj), ("prev", x, bw, _cj), ("col", w, bw, _cj), ("col", bias.reshape(1, d), bw, _cj)],
                 outs=[(d, bw, _cj, ACT_DTYPE)])
    return o


def _sconv_bwd(x, w, bias, dact, *, name):
    s, d = x.shape
    kw = w.shape[0]
    bw = _tile(d, (512, 256, 128))
    tm = _tile(s, (512, 256, 128))

    def fn(i, nrt, xv, px, nx, dav, nda, wv, bsv):
        xv = xv.astype(F32)
        ext = jnp.concatenate([jnp.where(i == 0, 0.0, px.astype(F32)), xv, nx.astype(F32)], axis=0)
        rows_e = tm + SUBLANES
        pre_e = sum(wv[k:k + 1, :] * _shift_down(ext, kw - 1 - k, rows_e) for k in range(kw)) + bsv
        da_e = jnp.concatenate([dav.astype(F32), jnp.where(i == nrt - 1, 0.0, nda.astype(F32))], axis=0)
        sg = _sigmoid(pre_e)
        dpre_e = da_e * sg * (1.0 + pre_e * (1.0 - sg))
        dx = sum(wv[k:k + 1, :] * _shift_up(dpre_e, kw - 1 - k, tm) for k in range(kw))
        dpre = dpre_e[:tm]
        dw = jnp.concatenate([_sum_rows(dpre * _shift_down(ext, kw - 1 - k, tm)) for k in range(kw)], axis=0)
        return (dx,), (dw, _sum_rows(dpre))

    dx, dw, db = _rows(fn, name=name, s=s, tm=tm, ncol=d // bw,
                       ins=[("row", x, bw, _cj), ("prev", x, bw, _cj), ("next", x, bw, _cj), ("row", dact, bw, _cj),
                            ("next", dact, bw, _cj), ("col", w, bw, _cj), ("col", bias.reshape(1, d), bw, _cj)],
                       outs=[(d, bw, _cj, ACT_DTYPE)], accs=[(kw, d, bw, _cj), (1, d, bw, _cj)])
    return dx, dw, db.reshape(d)


def _tri(n, reverse):
    r = lax.broadcasted_iota(jnp.int32, (n, n), 0)
    c = lax.broadcasted_iota(jnp.int32, (n, n), 1)
    return jnp.where((c >= r) if reverse else (c <= r), 1.0, 0.0).astype(F32)


def _cumsum_rows(x, *, reverse, name):
    s, w = x.shape
    ch = _tile(s, (256, 128))
    n = s // ch

    def body(x_ref, o_ref, carry):
        i = pl.program_id(0)

        @pl.when(i == 0)
        def _():
            carry[...] = jnp.zeros_like(carry)

        out = jnp.dot(_tri(ch, reverse), x_ref[...], precision=HI, preferred_element_type=F32) + carry[...]
        o_ref[...] = out
        carry[...] = out[0:1, :] if reverse else out[ch - 1:ch, :]

    imap = (lambda i: (n - 1 - i, 0)) if reverse else (lambda i: (i, 0))
    return pl.pallas_call(
        body, name=name, grid=(n,), in_specs=[pl.BlockSpec((ch, w), imap)], out_specs=pl.BlockSpec((ch, w), imap),
        out_shape=jax.ShapeDtypeStruct((s, w), F32), scratch_shapes=[pltpu.VMEM((1, w), F32)],
        compiler_params=_params(("arbitrary",)),
    )(x)


def _fox_prep(q, k, f, gq, gk, bf, *, name):
    s, d = q.shape
    scale = HEAD ** -0.5

    def fn(i, nrt, qv, kv, fv, gqv, gkv, bfv):
        qv, kv = qv.astype(F32), kv.astype(F32)
        qn = qv * lax.rsqrt(_gmean(qv * qv, HEAD) + RMS_EPS) * gqv * scale
        kn = kv * lax.rsqrt(_gmean(kv * kv, HEAD) + RMS_EPS) * gkv
        z = fv + bfv
        logf = jnp.minimum(z, 0.0) - jnp.log(1.0 + jnp.exp(-jnp.abs(z)))
        return (qn, kn, logf), ()

    return _rows(fn, name=name, s=s, tm=_tile(s, (512, 256, 128)),
                 ins=[("row", q, d, _c0), ("row", k, d, _c0), ("row", f, LANES, _c0), ("full", gq), ("full", gk), ("full", bf)],
                 outs=[(d, d, _c0, ACT_DTYPE), (d, d, _c0, ACT_DTYPE), (LANES, LANES, _c0, F32)])


def _fox_prep_bwd(q, k, f, gq, gk, bf, dqs, dkn, dlogf, *, name):
    s, d = q.shape
    scale = HEAD ** -0.5

    def fn(i, nrt, qv, kv, fv, gqv, gkv, bfv, dqv, dkv, dlf):
        outs, accs = [], []
        for xv, gv, dv, sc in ((qv, gqv, dqv, scale), (kv, gkv, dkv, 1.0)):
            xv, dv = xv.astype(F32), dv.astype(F32) * sc
            r = lax.rsqrt(_gmean(xv * xv, HEAD) + RMS_EPS)
            xh = xv * r
            g = dv * gv
            outs.append(r * (g - xh * _gmean(g * xh, HEAD)))
            accs.append(_sum_rows(dv * xh))
        z = fv + bfv
        df = dlf * _sigmoid(-z)
        outs.append(df)
        accs.append(_sum_rows(df))
        return outs, accs

    return _rows(fn, name=name, s=s, tm=_tile(s, (512, 256, 128)),
                 ins=[("row", q, d, _c0), ("row", k, d, _c0), ("row", f, LANES, _c0), ("full", gq), ("full", gk), ("full", bf),
                      ("row", dqs, d, _c0), ("row", dkn, d, _c0), ("row", dlogf, LANES, _c0)],
                 outs=[(d, d, _c0, ACT_DTYPE), (d, d, _c0, ACT_DTYPE), (LANES, LANES, _c0, ACT_DTYPE)],
                 accs=[(1, d, d, _c0), (1, d, d, _c0), (1, LANES, LANES, _c0)])


def _head_masks(shape):
    lane = lax.broadcasted_iota(jnp.int32, shape, len(shape) - 1)
    return lane < HEAD, lane >= HEAD


def _pick_lane(blk, idx):
    lane = lax.broadcasted_iota(jnp.int32, blk.shape, 1)
    return jnp.sum(jnp.where(lane == idx, blk, 0.0), axis=1, keepdims=True)


def _pick_row(blk, idx):
    sub = lax.broadcasted_iota(jnp.int32, blk.shape, 0)
    return jnp.sum(jnp.where(sub == idx, blk, 0.0), axis=0, keepdims=True)


def _fox_aug(qs, kn, cum, *, name):
    s, d = qs.shape
    hp = d // LANES

    def fn(i, nrt, qv, kv, cv):
        lane = lax.broadcasted_iota(jnp.int32, (qv.shape[0], LANES), 1)
        outs = [[], [], [], []]
        for p in range(hp):
            qt, kt = qv[:, p * LANES:(p + 1) * LANES], kv[:, p * LANES:(p + 1) * LANES]
            for h in range(2):
                mine = (lane < HEAD) if h == 0 else (lane >= HEAD)
                a0 = HEAD if h == 0 else 0
                c = cv[:, 2 * p + h:2 * p + h + 1]
                hi = c.astype(ACT_DTYPE).astype(F32)
                mid = (c - hi).astype(ACT_DTYPE).astype(F32)
                lo = (c - hi - mid).astype(ACT_DTYPE).astype(F32)
                ones = jnp.where((lane >= a0) & (lane < a0 + 3), 1.0, 0.0)
                kx = jnp.where(lane == a0, -hi, jnp.where(lane == a0 + 1, -mid, jnp.where(lane == a0 + 2, -lo, 0.0)))
                outs[h].append(jnp.where(mine, qt.astype(F32), ones))
                outs[2 + h].append(jnp.where(mine, kt.astype(F32), kx))
        return [jnp.concatenate(o, axis=1) for o in outs], ()

    return _rows(fn, name=name, s=s, tm=_tile(s, (512, 256, 128)), ins=[("row", qs, d, _c0), ("row", kn, d, _c0), ("row", cum, LANES, _c0)],
                 outs=[(d, d, _c0, ACT_DTYPE)] * 4)


def _tri_tables(nq, by_key):
    import numpy as np
    pairs = [(qi, kj) for kj in range(nq) for qi in range(kj, nq)] if by_key else [(qi, kj) for qi in range(nq) for kj in range(qi + 1)]
    return jnp.asarray(np.array([p[0] for p in pairs], np.int32)), jnp.asarray(np.array([p[1] for p in pairs], np.int32))


def _nt(a, b):
    return lax.dot_general(a, b, (((1,), (1,)), ((), ())), preferred_element_type=F32)


def _tn(a, b):
    return lax.dot_general(a, b, (((0,), (0,)), ((), ())), preferred_element_type=F32)


ATTN_BLOCKS = (1024, 512, 256, 128)


def _fox_dd(do, o, *, name):
    s, d = do.shape

    def fn(i, nrt, dov, ov):
        return (_reduce_heads(dov.astype(F32) * ov.astype(F32), d // HEAD, HEAD),), ()

    (dd,) = _rows(fn, name=name, s=s, tm=_tile(s, (512, 256, 128)), ins=[("row", do, d, _c0), ("row", o, d, _c0)],
                  outs=[(LANES, LANES, _c0, F32)])
    return dd


def _pair_rows(a, nh):
    s = a.shape[0]
    t = a[:, :nh].T.reshape(nh // 2, 2, s)
    return jnp.pad(t, ((0, 0), (0, SUBLANES - 2), (0, 0)))


def _rows01(r0, r1):
    sub = lax.broadcasted_iota(jnp.int32, (SUBLANES, r0.shape[1]), 0)
    return jnp.where(sub == 0, r0, jnp.where(sub == 1, r1, 0.0))


def _rider_parts(rider):
    if rider is None:
        return [], [], [], [], []
    return [ANY] * len(rider.arrays), [ANY] * len(rider.out_shapes), rider.out_shapes, rider.scratch, rider.arrays


def _fox_fwd_t(q_aug, k_aug, v, *, name, rider=None):
    s, d = v.shape
    bq = _tile(s, ATTN_BLOCKS)
    nq = s // bq
    hp = d // LANES
    qtab, ktab = _tri_tables(nq, by_key=False)

    def body(qt, kt, q0_ref, q1_ref, k0_ref, k1_ref, v_ref, o_ref, lse_ref, m0, m1, l0, l1, acc0, acc1):
        t = pl.program_id(1)
        qi, kj = qt[t], kt[t]
        ms, ls, accs = (m0, m1), (l0, l1), (acc0, acc1)

        @pl.when(kj == 0)
        def _():
            for h in range(2):
                ms[h][...] = jnp.full_like(ms[h], -jnp.inf)
                ls[h][...] = jnp.zeros_like(ls[h])
                accs[h][...] = jnp.zeros_like(accs[h])

        def update(diagonal):
            v2 = v_ref[...]
            qk = ((q0_ref, k0_ref), (q1_ref, k1_ref))
            sts = [_nt(qk[h][1][...], qk[h][0][...]) for h in range(2)]
            if diagonal:
                sts = [_diag_mask_t(st) for st in sts]
            m_prev = [ms[h][...] for h in range(2)]
            m_new = [jnp.maximum(m_prev[h], jnp.max(sts[h], axis=0, keepdims=True)) for h in range(2)]
            ps = [jnp.exp(sts[h] - m_new[h]) for h in range(2)]
            alpha = [jnp.exp(m_prev[h] - m_new[h]) for h in range(2)]
            for h in range(2):
                ls[h][...] = alpha[h] * ls[h][...] + jnp.sum(ps[h], axis=0, keepdims=True)
                accs[h][...] = alpha[h] * accs[h][...] + _tn(v2, ps[h].astype(MM_DTYPE))
                ms[h][...] = m_new[h]

        @pl.when(kj < qi)
        def _():
            update(False)

        @pl.when(kj == qi)
        def _():
            update(True)
            row = lax.broadcasted_iota(jnp.int32, (LANES, bq), 0)
            ot = jnp.where(row < HEAD, acc0[...] / l0[...], acc1[...] / l1[...])
            o_ref[...] = ot.T.astype(o_ref.dtype)
            lse_ref[...] = _rows01(m0[...] + jnp.log(l0[...]), m1[...] + jnp.log(l1[...]))

    blk = (bq, LANES)
    qmap = lambda p_, t, qt, kt: (qt[t], p_)
    kmap = lambda p_, t, qt, kt: (kt[t], p_)
    grid = (hp, qtab.shape[0])
    r_in, r_out, r_shapes, r_scratch, r_args = _rider_parts(rider)
    grid_spec = pltpu.PrefetchScalarGridSpec(
        num_scalar_prefetch=2, grid=grid,
        in_specs=[pl.BlockSpec(blk, qmap), pl.BlockSpec(blk, qmap), pl.BlockSpec(blk, kmap), pl.BlockSpec(blk, kmap), pl.BlockSpec(blk, kmap)] + r_in,
        out_specs=[pl.BlockSpec(blk, qmap), pl.BlockSpec((None, SUBLANES, bq), lambda p_, t, qt, kt: (p_, 0, qt[t]))] + r_out,
        scratch_shapes=[pltpu.VMEM((1, bq), F32)] * 4 + [pltpu.VMEM((LANES, bq), F32)] * 2 + r_scratch)
    outs = pl.pallas_call(
        _carry(body, rider, 2, 5, 2, grid), name=name, grid_spec=grid_spec,
        out_shape=[jax.ShapeDtypeStruct((s, d), ACT_DTYPE), jax.ShapeDtypeStruct((hp, SUBLANES, s), F32)] + r_shapes,
        compiler_params=_params(("arbitrary", "arbitrary") if rider else ("parallel", "arbitrary")),
    )(qtab, ktab, q_aug[0], q_aug[1], k_aug[0], k_aug[1], v, *r_args)
    return outs[0], outs[1], list(outs[2:])


def _diag_mask_t(st):
    key = lax.broadcasted_iota(jnp.int32, st.shape, 0)
    qry = lax.broadcasted_iota(jnp.int32, st.shape, 1)
    return jnp.where(qry >= key, st, -jnp.inf)


def _fox_bwd_t(q_aug, k_aug, v, lse, dd, do, *, name, rider=None):
    s, d = v.shape
    bq = _tile(s, ATTN_BLOCKS)
    nq = s // bq
    hp = d // LANES
    blk = (bq, LANES)
    qtab, ktab = _tri_tables(nq, by_key=True)
    n_steps = qtab.shape[0]

    def body(qt, kt, q0_ref, q1_ref, k0_ref, k1_ref, v_ref, lse_ref, dd_ref, do_ref,
             dq_ref, dk_ref, dv_ref, dcol_ref, drow_ref, dq_sc, rs_sc, dk_sc, dv_sc, cs_sc):
        t = pl.program_id(1)
        qi, kj = qt[t], kt[t]

        @pl.when(t == 0)
        def _():
            dq_sc[...] = jnp.zeros_like(dq_sc)
            rs_sc[...] = jnp.zeros_like(rs_sc)

        def update(diagonal):
            v2, do2 = v_ref[...], do_ref[...]
            masks = _head_masks(blk)
            row = lax.broadcasted_iota(jnp.int32, (LANES, bq), 0)
            off = pl.multiple_of(qi * bq, bq)
            for h, (q_ref, k_ref) in enumerate(((q0_ref, k0_ref), (q1_ref, k1_ref))):
                st = _nt(k_ref[...], q_ref[...])
                if diagonal:
                    st = _diag_mask_t(st)
                p = jnp.exp(st - lse_ref[h:h + 1, :])
                dp = _nt(v2, jnp.where(masks[h], do2, jnp.zeros_like(do2)))
                ds = p * (dp - dd_ref[h:h + 1, :])
                dsb = ds.astype(MM_DTYPE)
                dv_sc[h] += jnp.dot(p.astype(MM_DTYPE), do2, preferred_element_type=F32)
                dk_sc[h] += jnp.dot(dsb, q_ref[...], preferred_element_type=F32)
                cs_sc[h] += jnp.sum(ds, axis=1, keepdims=True)
                mine = (row < HEAD) if h == 0 else (row >= HEAD)
                dq_sc[:, pl.ds(off, bq)] += jnp.where(mine, _tn(k_ref[...], dsb), 0.0)
                rs_sc[h:h + 1, pl.ds(off, bq)] += jnp.sum(ds, axis=0, keepdims=True)

        @pl.when(qi == kj)
        def _():
            dk_sc[...] = jnp.zeros_like(dk_sc)
            dv_sc[...] = jnp.zeros_like(dv_sc)
            cs_sc[...] = jnp.zeros_like(cs_sc)
            update(True)

        @pl.when(qi > kj)
        def _():
            update(False)

        @pl.when(qi == nq - 1)
        def _():
            lo, _hi = _head_masks(blk)
            dk_ref[...] = jnp.where(lo, dk_sc[0], dk_sc[1]).astype(dk_ref.dtype)
            dv_ref[...] = jnp.where(lo, dv_sc[0], dv_sc[1]).astype(dv_ref.dtype)
            dcol_ref[...] = jnp.where(lo, cs_sc[0], cs_sc[1])

        @pl.when(t == n_steps - 1)
        def _():
            for c in range(nq):
                dq_ref[c * bq:(c + 1) * bq, :] = dq_sc[:, c * bq:(c + 1) * bq].T.astype(dq_ref.dtype)
            drow_ref[...] = rs_sc[...]

    qmap = lambda p_, t, qt, kt: (qt[t], p_)
    kmap = lambda p_, t, qt, kt: (kt[t], p_)
    rmap = lambda p_, t, qt, kt: (p_, 0, qt[t])
    grid = (hp, n_steps)
    r_in, r_out, r_shapes, r_scratch, r_args = _rider_parts(rider)
    outs = pl.pallas_call(
        _carry(body, rider, 2, 8, 5, grid), name=name,
        grid_spec=pltpu.PrefetchScalarGridSpec(
            num_scalar_prefetch=2, grid=grid,
            in_specs=[pl.BlockSpec(blk, qmap), pl.BlockSpec(blk, qmap), pl.BlockSpec(blk, kmap), pl.BlockSpec(blk, kmap), pl.BlockSpec(blk, kmap),
                      pl.BlockSpec((None, SUBLANES, bq), rmap), pl.BlockSpec((None, SUBLANES, bq), rmap), pl.BlockSpec(blk, qmap)] + r_in,
            out_specs=[pl.BlockSpec((s, LANES), lambda p_, t, qt, kt: (0, p_)), pl.BlockSpec(blk, kmap), pl.BlockSpec(blk, kmap),
                       pl.BlockSpec(blk, kmap), pl.BlockSpec((None, SUBLANES, s), lambda p_, t, qt, kt: (p_, 0, 0))] + r_out,
            scratch_shapes=[pltpu.VMEM((LANES, s), F32), pltpu.VMEM((SUBLANES, s), F32), pltpu.VMEM((2, bq, LANES), F32),
                            pltpu.VMEM((2, bq, LANES), F32), pltpu.VMEM((2, bq, 1), F32)] + r_scratch),
        out_shape=[jax.ShapeDtypeStruct((s, d), ACT_DTYPE), jax.ShapeDtypeStruct((s, d), ACT_DTYPE), jax.ShapeDtypeStruct((s, d), ACT_DTYPE),
                   jax.ShapeDtypeStruct((s, d), F32), jax.ShapeDtypeStruct((hp, SUBLANES, s), F32)] + r_shapes,
        compiler_params=_params(("arbitrary", "arbitrary") if rider else ("parallel", "arbitrary")),
    )(qtab, ktab, q_aug[0], q_aug[1], k_aug[0], k_aug[1], v, lse, dd, do, *r_args)
    return list(outs[:5]), list(outs[5:])


def _expand_heads(v, nh, hd):
    r = lax.broadcasted_iota(jnp.int32, (LANES, nh * hd), 0)
    c = lax.broadcasted_iota(jnp.int32, (LANES, nh * hd), 1) // hd
    e = jnp.where(r == c, 1.0, 0.0).astype(F32)
    return jnp.dot(v, e, precision=HI, preferred_element_type=F32)


def _reduce_heads(v, nh, hd):
    r = lax.broadcasted_iota(jnp.int32, (nh * hd, LANES), 0) // hd
    c = lax.broadcasted_iota(jnp.int32, (nh * hd, LANES), 1)
    e = jnp.where(r == c, 1.0, 0.0).astype(F32)
    return jnp.dot(v, e, precision=HI, preferred_element_type=F32)


def _ssd_prep(dt_raw, dt_bias, a_log, nh, *, name):
    s = dt_raw.shape[0]

    def fn(i, nrt, dtr, bsv, alv):
        dt = _softplus(dtr + bsv)
        acum = jnp.dot(_tri(SSM_CHUNK, False), dt * (-jnp.exp(alv)), precision=HI, preferred_element_type=F32)
        return (dt, acum, _expand_heads(dt, nh, HEAD), _expand_heads(acum, nh, HEAD)), ()

    w = nh * HEAD
    return _rows(fn, name=name, s=s, tm=SSM_CHUNK, ins=[("row", dt_raw, LANES, _c0), ("full", dt_bias), ("full", a_log)],
                 outs=[(LANES, LANES, _c0, F32), (LANES, LANES, _c0, F32), (w, w, _c0, F32), (w, w, _c0, F32)])


def _ssd_prep_bwd(dt_raw, dt_bias, a_log, ddtx, dacx, nh, *, name):
    s = dt_raw.shape[0]

    def fn(i, nrt, dtr, bsv, alv, ddx, dax):
        z = dtr + bsv
        dt = _softplus(z)
        a = -jnp.exp(alv)
        dda = jnp.dot(_tri(SSM_CHUNK, True), _reduce_heads(dax, nh, HEAD), precision=HI, preferred_element_type=F32)
        ddt = _reduce_heads(ddx, nh, HEAD) + dda * a
        dz = ddt * _sigmoid(z)
        lane = lax.broadcasted_iota(jnp.int32, dz.shape, 1)
        dz = jnp.where(lane < nh, dz, 0.0)
        return (dz,), (_sum_rows(dz), _sum_rows(dda * dt) * a)

    w = nh * HEAD
    return _rows(fn, name=name, s=s, tm=SSM_CHUNK,
                 ins=[("row", dt_raw, LANES, _c0), ("full", dt_bias), ("full", a_log), ("row", ddtx, w, _c0), ("row", dacx, w, _c0)],
                 outs=[(LANES, LANES, _c0, ACT_DTYPE)], accs=[(1, LANES, LANES, _c0), (1, LANES, LANES, _c0)])


def _ssd_decay(ac_blk, act_blk, head):
    col = _pick_lane(ac_blk, head)
    row = _pick_row(act_blk, head)
    r = lax.broadcasted_iota(jnp.int32, (SSM_CHUNK, SSM_CHUNK), 0)
    c = lax.broadcasted_iota(jnp.int32, (SSM_CHUNK, SSM_CHUNK), 1)
    return jnp.exp(jnp.where(r >= c, col - row, -jnp.inf))


def _group_masks(shape, hpg):
    lane = lax.broadcasted_iota(jnp.int32, shape, len(shape) - 1) // HEAD
    return [lane == k for k in range(hpg)]


def _ssd_scan_fwd(xs, bm, cm, dtx, acx, acum, acum_t, d_x, *, name):
    s, di = xs.shape
    ng = bm.shape[1] // SSM_STATE
    gw = di // ng
    hpg = gw // HEAD
    nc = s // SSM_CHUNK
    L = SSM_CHUNK
    nh_pad = acum_t.shape[0]

    def body(x_ref, b_ref, c_ref, dt_ref, ax_ref, ac_ref, act_ref, d_ref, y_ref, st_ref, state):
        g, c = pl.program_id(0), pl.program_id(1)

        @pl.when(c == 0)
        def _():
            state[...] = jnp.zeros_like(state)

        x4, bv, cv = x_ref[...].astype(F32), b_ref[...], c_ref[...]
        ax = ax_ref[...]
        tx = (x4 * dt_ref[...])
        cb = lax.dot_general(cv, bv, (((1,), (1,)), ((), ())), preferred_element_type=F32)
        masks = _group_masks((L, gw), hpg)
        y = jnp.zeros((L, gw), F32)
        txb = tx.astype(MM_DTYPE)
        for k in range(hpg):
            wk = (cb * _ssd_decay(ac_ref[...], act_ref[...], g * hpg + k)).astype(MM_DTYPE)
            y = y + jnp.where(masks[k], jnp.dot(wk, txb, preferred_element_type=F32), 0.0)
        prev = state[...]
        st_ref[...] = prev
        y = y + jnp.dot(cv, prev.astype(MM_DTYPE), preferred_element_type=F32) * jnp.exp(ax)
        y = y + d_ref[...] * x4
        y_ref[...] = y.astype(y_ref.dtype)
        a_last = ax[L - 1:L, :]
        sx = (tx * jnp.exp(a_last - ax)).astype(MM_DTYPE)
        state[...] = prev * jnp.exp(a_last) + lax.dot_general(bv, sx, (((0,), (0,)), ((), ())), preferred_element_type=F32)

    y, states = pl.pallas_call(
        body, name=name, grid=(ng, nc),
        in_specs=[pl.BlockSpec((L, gw), lambda g, c: (c, g)), pl.BlockSpec((L, SSM_STATE), lambda g, c: (c, g)),
                  pl.BlockSpec((L, SSM_STATE), lambda g, c: (c, g)), pl.BlockSpec((L, gw), lambda g, c: (c, g)),
                  pl.BlockSpec((L, gw), lambda g, c: (c, g)), pl.BlockSpec((L, LANES), lambda g, c: (c, 0)),
                  pl.BlockSpec((nh_pad, L), lambda g, c: (0, c)), pl.BlockSpec((1, gw), lambda g, c: (0, g))],
        out_specs=[pl.BlockSpec((L, gw), lambda g, c: (c, g)), pl.BlockSpec((None, None, SSM_STATE, gw), lambda g, c: (g, c, 0, 0))],
        out_shape=[jax.ShapeDtypeStruct((s, di), ACT_DTYPE), jax.ShapeDtypeStruct((ng, nc, SSM_STATE, gw), F32)],
        scratch_shapes=[pltpu.VMEM((SSM_STATE, gw), F32)],
        compiler_params=_params(("parallel", "arbitrary")),
    )(xs, bm, cm, dtx, acx, acum, acum_t, d_x)
    return y, states


def _ssd_scan_bwd(xs, bm, cm, dtx, acx, acum, acum_t, d_x, states, dy, *, name, rider=None):
    s, di = xs.shape
    ng = bm.shape[1] // SSM_STATE
    gw = di // ng
    hpg = gw // HEAD
    nc = s // SSM_CHUNK
    L = SSM_CHUNK
    nh_pad = acum_t.shape[0]

    def body(x_ref, b_ref, c_ref, dt_ref, ax_ref, ac_ref, act_ref, d_ref, st_ref, dy_ref,
             dx_ref, db_ref, dc_ref, ddt_ref, dax_ref, dd_ref, dstate):
        g, cc = pl.program_id(0), pl.program_id(1)

        @pl.when(cc == 0)
        def _():
            dstate[...] = jnp.zeros_like(dstate)
            dd_ref[...] = jnp.zeros_like(dd_ref)

        x4, bv, cv = x_ref[...].astype(F32), b_ref[...], c_ref[...]
        tv, ax, dyv = dt_ref[...], ax_ref[...], dy_ref[...].astype(F32)
        prev, dn = st_ref[...], dstate[...]
        dnb = dn.astype(MM_DTYPE)
        masks = _group_masks((L, gw), hpg)
        tx = x4 * tv
        txb = tx.astype(MM_DTYPE)
        e_ax = jnp.exp(ax)
        a_last = ax[L - 1:L, :]
        e_last = jnp.exp(a_last)
        ed = jnp.exp(a_last - ax)

        dx = d_ref[...] * dyv
        dd_ref[...] += _sum_rows(dyv * x4)
        dye = (dyv * e_ax).astype(MM_DTYPE)
        yo = jnp.dot(cv, prev.astype(MM_DTYPE), preferred_element_type=F32) * e_ax
        dc = lax.dot_general(dye, prev.astype(MM_DTYPE), (((1,), (1,)), ((), ())), preferred_element_type=F32)
        dprev = lax.dot_general(cv, dye, (((0,), (0,)), ((), ())), preferred_element_type=F32)
        dax = dyv * yo
        sx = tx * ed
        dsx = jnp.dot(bv, dnb, preferred_element_type=F32)
        db = lax.dot_general(sx.astype(MM_DTYPE), dnb, (((1,), (1,)), ((), ())), preferred_element_type=F32)
        dtx_ = dsx * ed
        dsx_sx = dsx * sx
        dax = dax - dsx_sx
        dlast = _sum_rows(dsx_sx) + _sum_rows(dn * prev) * e_last
        dprev = dprev + dn * e_last
        cb = lax.dot_general(cv, bv, (((1,), (1,)), ((), ())), preferred_element_type=F32)
        dcb = jnp.zeros((L, L), F32)
        lane = lax.broadcasted_iota(jnp.int32, (L, gw), 1)
        for k in range(hpg):
            dec = _ssd_decay(ac_ref[...], act_ref[...], g * hpg + k)
            wk = (cb * dec).astype(MM_DTYPE)
            dyk = jnp.where(masks[k], dyv, 0.0).astype(MM_DTYPE)
            dtx_ = dtx_ + jnp.where(masks[k], lax.dot_general(wk, dyk, (((0,), (0,)), ((), ())), preferred_element_type=F32), 0.0)
            dwk = lax.dot_general(dyk, txb, (((1,), (1,)), ((), ())), preferred_element_type=F32)
            dcb = dcb + dwk * dec
            mk = dwk * cb * dec
            da_k = jnp.sum(mk, axis=1, keepdims=True) - jnp.sum(mk.T, axis=1, keepdims=True)
            dax = dax + jnp.where(lane == k * HEAD, da_k, 0.0)
        dcbb = dcb.astype(MM_DTYPE)
        dc = dc + jnp.dot(dcbb, bv, preferred_element_type=F32)
        db = db + lax.dot_general(dcbb, cv, (((0,), (0,)), ((), ())), preferred_element_type=F32)
        sub = lax.broadcasted_iota(jnp.int32, (L, gw), 0)
        dax = dax + jnp.where(sub == L - 1, dlast, 0.0)
        dx_ref[...] = (dx + dtx_ * tv).astype(dx_ref.dtype)
        ddt_ref[...] = dtx_ * x4
        dax_ref[...] = dax
        db_ref[...] = db.astype(db_ref.dtype)
        dc_ref[...] = dc.astype(dc_ref.dtype)
        dstate[...] = dprev

    rev = lambda g, c: (nc - 1 - c, g)
    rev0 = lambda g, c: (nc - 1 - c, 0)
    grid = (ng, nc)
    r_in, r_out, r_shapes, r_scratch, r_args = _rider_parts(rider)
    outs = pl.pallas_call(
        _carry(body, rider, 0, 10, 6, grid), name=name, grid=grid,
        in_specs=[pl.BlockSpec((L, gw), rev), pl.BlockSpec((L, SSM_STATE), rev), pl.BlockSpec((L, SSM_STATE), rev),
                  pl.BlockSpec((L, gw), rev), pl.BlockSpec((L, gw), rev), pl.BlockSpec((L, LANES), rev0),
                  pl.BlockSpec((nh_pad, L), lambda g, c: (0, nc - 1 - c)), pl.BlockSpec((1, gw), lambda g, c: (0, g)),
                  pl.BlockSpec((None, None, SSM_STATE, gw), lambda g, c: (g, nc - 1 - c, 0, 0)), pl.BlockSpec((L, gw), rev)] + r_in,
        out_specs=[pl.BlockSpec((L, gw), rev), pl.BlockSpec((L, SSM_STATE), rev), pl.BlockSpec((L, SSM_STATE), rev),
                   pl.BlockSpec((L, gw), rev), pl.BlockSpec((L, gw), rev), pl.BlockSpec((1, gw), lambda g, c: (0, g))] + r_out,
        out_shape=[jax.ShapeDtypeStruct((s, di), ACT_DTYPE), jax.ShapeDtypeStruct(bm.shape, ACT_DTYPE), jax.ShapeDtypeStruct(cm.shape, ACT_DTYPE),
                   jax.ShapeDtypeStruct((s, di), F32), jax.ShapeDtypeStruct((s, di), F32), jax.ShapeDtypeStruct((1, di), F32)] + r_shapes,
        scratch_shapes=[pltpu.VMEM((SSM_STATE, gw), F32)] + r_scratch,
        compiler_params=_params(("arbitrary", "arbitrary") if rider else ("parallel", "arbitrary")),
    )(xs, bm, cm, dtx, acx, acum, acum_t, d_x, states, dy, *r_args)
    return list(outs[:6]), list(outs[6:])


def _ssd_gate(y, z, w, gs, *, name):
    s, d = y.shape

    def fn(i, nrt, yv, zv, wv):
        zv = zv.astype(F32)
        u = yv.astype(F32) * zv * _sigmoid(zv)
        return (u * lax.rsqrt(_gmean(u * u, gs) + RMS_EPS) * wv,), ()

    (o,) = _rows(fn, name=name, s=s, tm=_tile(s, (256, 128)), ins=[("row", y, d, _c0), ("row", z, d, _c0), ("full", w.reshape(1, d))],
                 outs=[(d, d, _c0, ACT_DTYPE)])
    return o


def _ssd_gate_bwd(y, z, w, do, gs, *, name):
    s, d = y.shape

    def fn(i, nrt, yv, zv, wv, dov):
        yv, zv, dov = yv.astype(F32), zv.astype(F32), dov.astype(F32)
        sg = _sigmoid(zv)
        sl = zv * sg
        u = yv * sl
        r = lax.rsqrt(_gmean(u * u, gs) + RMS_EPS)
        uh = u * r
        g = dov * wv
        du = r * (g - uh * _gmean(g * uh, gs))
        return (du * sl, du * yv * sg * (1.0 + zv * (1.0 - sg))), (_sum_rows(dov * uh),)

    dy, dz, dw = _rows(fn, name=name, s=s, tm=_tile(s, (256, 128)),
                       ins=[("row", y, d, _c0), ("row", z, d, _c0), ("full", w.reshape(1, d)), ("row", do, d, _c0)],
                       outs=[(d, d, _c0, ACT_DTYPE), (d, d, _c0, ACT_DTYPE)], accs=[(1, d, d, _c0)])
    return dy, dz, dw.reshape(d)


def _pad_lanes(w):
    return jnp.pad(w, ((0, 0), (0, LANES - w.shape[1])))


def _nt_sum(pairs, name):
    acc = None
    for a, b in pairs:
        acc = _mm(a, b, tb=True, add=acc, name=name)
    return acc


def _conv_mixer_fwd(h, w_in, layer, w_dw, tag):
    d, ns = h.shape[1], w_in.shape[3]
    inter = (3, _tile(math.gcd(d, ns), TILES))
    p = _mm(h, w_in, b_layer=layer, inter=inter, out_dtype=ACT_DTYPE, name=f"{tag}_in")
    return _gconv_fwd(p, w_dw, inter[1], name=f"{tag}_gate"), (h, w_in, layer, inter, p, w_dw)


def _conv_mixer_bwd(cache, do, tag):
    h, w_in, layer, inter, p, w_dw = cache
    dp, dw_dw = _gconv_bwd(p, w_dw, do, inter[1], name=f"{tag}_gate_bwd")
    dw_in = _mm(h, dp, ta=True, out_shard=w_in.shape[3], inter=inter, out_dtype=WIRE_DTYPE, name=f"{tag}_dw_in")
    dh = _mm(dp, w_in, tb=True, b_layer=layer, inter=inter, name=f"{tag}_dh")
    return dh, {"w_in": dw_in, "w_dw": dw_dw}


def _fox_mixer_fwd(h, w_in, b_f, q_gain, k_gain, tag, rider=None):
    d = h.shape[1]
    nh = d // HEAD
    ws = [w_in[:, k * d:(k + 1) * d] for k in range(3)] + [_pad_lanes(w_in[:, 3 * d:])]
    q, k, v = [_mm(h, w, out_dtype=ACT_DTYPE, name=f"{tag}_in") for w in ws[:3]]
    f = _mm(h, ws[3], name=f"{tag}_in_f")
    gq = jnp.tile(q_gain, nh).reshape(1, d)
    gk = jnp.tile(k_gain, nh).reshape(1, d)
    bf = _pad_lanes(b_f.reshape(1, nh))
    qs, kn, logf = _fox_prep(q, k, f, gq, gk, bf, name=f"{tag}_prep")
    cum = _cumsum_rows(logf, reverse=False, name=f"{tag}_cum")
    aug = _fox_aug(qs, kn, cum, name=f"{tag}_aug")
    q_aug, k_aug = aug[:2], aug[2:]
    o, lse, landed = _fox_fwd_t(q_aug, k_aug, v, name=f"{tag}_attn", rider=rider)
    return o, (h, ws, q, k, v, f, gq, gk, bf, q_aug, k_aug, o, lse), landed


def _fox_mixer_bwd(cache, do, tag, rider=None):
    h, ws, q, k, v, f, gq, gk, bf, q_aug, k_aug, o, lse = cache
    s, d = q.shape
    nh = d // HEAD
    dd = _pair_rows(_fox_dd(do, o, name=f"{tag}_attn_dd"), nh)
    (dqs, dkn, dv, dcol, drow), landed = _fox_bwd_t(q_aug, k_aug, v, lse, dd, do, name=f"{tag}_attn_bwd", rider=rider)
    dcum = _pad_lanes(drow[:, :2, :].reshape(nh, s).T - dcol[:, ::HEAD])
    dlogf = _cumsum_rows(dcum, reverse=True, name=f"{tag}_cum_bwd")
    dq, dk, df, dgq, dgk, dbf = _fox_prep_bwd(q, k, f, gq, gk, bf, dqs, dkn, dlogf, name=f"{tag}_prep_bwd")
    dps = (dq, dk, dv, df)
    dws = [_mm(h, dp, ta=True, name=f"{tag}_dw_in") for dp in dps]
    dw_in = jnp.concatenate(dws[:3] + [dws[3][:, :nh]], axis=1)
    dh = _nt_sum(list(zip(dps, ws)), f"{tag}_dh")
    return dh, {"w_in": dw_in, "b_f": dbf[0, :nh], "q_gain": dgq.reshape(nh, HEAD).sum(0), "k_gain": dgk.reshape(nh, HEAD).sum(0)}, landed


def _ssd_mixer_fwd(h, w_in, conv_w, conv_b, dt_bias, a_log, d_skip, norm_w, tag):
    di = norm_w.shape[0]
    nh = di // HEAD
    gn = (conv_w.shape[1] - di) // 2
    cuts = [0, di, 2 * di, 2 * di + gn, 2 * di + 2 * gn]
    ws = [w_in[:, cuts[k]:cuts[k + 1]] for k in range(4)] + [_pad_lanes(w_in[:, cuts[4]:])]
    z, xr, br, cr = [_mm(h, w, out_dtype=ACT_DTYPE, name=f"{tag}_in") for w in ws[:4]]
    dtr = _mm(h, ws[4], name=f"{tag}_in_dt")
    ccuts = [0, di, di + gn, di + 2 * gn]
    cws = [conv_w[:, ccuts[k]:ccuts[k + 1]] for k in range(3)]
    cbs = [conv_b[ccuts[k]:ccuts[k + 1]] for k in range(3)]
    xs, bm, cm = [_sconv_fwd(r, w, b, name=f"{tag}_conv") for r, w, b in zip((xr, br, cr), cws, cbs)]
    dtb = _pad_lanes(dt_bias.reshape(1, nh))
    alg = _pad_lanes(a_log.reshape(1, nh))
    _dt, acum, dtx, acx = _ssd_prep(dtr, dtb, alg, nh, name=f"{tag}_prep")
    acum_t = acum[:, :nh].T
    d_x = jnp.repeat(d_skip, HEAD).reshape(1, di)
    y, states = _ssd_scan_fwd(xs, bm, cm, dtx, acx, acum, acum_t, d_x, name=f"{tag}_scan")
    gs = di // (gn // SSM_STATE)
    o = _ssd_gate(y, z, norm_w, gs, name=f"{tag}_gate")
    return o, (h, ws, z, (xr, br, cr), dtr, cws, cbs, xs, bm, cm, dtb, alg, dtx, acx, acum, acum_t, d_x, states, y, norm_w, gs, nh)


def _ssd_mixer_bwd(cache, do, tag, rider=None):
    h, ws, z, raws, dtr, cws, cbs, xs, bm, cm, dtb, alg, dtx, acx, acum, acum_t, d_x, states, y, norm_w, gs, nh = cache
    dy, dz, dnorm = _ssd_gate_bwd(y, z, norm_w, do, gs, name=f"{tag}_gate_bwd")
    (dxs, dbm, dcm, ddtx, dacx, dd_x), landed = _ssd_scan_bwd(xs, bm, cm, dtx, acx, acum, acum_t, d_x, states, dy, name=f"{tag}_scan_bwd",
                                                             rider=rider)
    ddtr, ddtb, dalg = _ssd_prep_bwd(dtr, dtb, alg, ddtx, dacx, nh, name=f"{tag}_prep_bwd")
    conv = [_sconv_bwd(r, w, b, da, name=f"{tag}_conv_bwd") for r, w, b, da in zip(raws, cws, cbs, (dxs, dbm, dcm))]
    dps = (dz, conv[0][0], conv[1][0], conv[2][0], ddtr)
    dws = [_mm(h, dp, ta=True, name=f"{tag}_dw_in") for dp in dps]
    dw_in = jnp.concatenate(dws[:4] + [dws[4][:, :nh]], axis=1)
    dh = _nt_sum(list(zip(dps, ws)), f"{tag}_dh")
    return dh, {"w_in": dw_in, "conv_w": jnp.concatenate([c[1] for c in conv], axis=1), "conv_b": jnp.concatenate([c[2] for c in conv]),
                "dt_bias": ddtb[0, :nh], "a_log": dalg[0, :nh], "d": dd_x.reshape(nh, HEAD).sum(1), "norm_w": dnorm}, landed


def _rows_natural(cm, layer):
    return cm[:, layer].reshape(-1, cm.shape[3])


def _cols_natural(cm, layer):
    return jnp.moveaxis(cm[:, layer], 0, 1).reshape(cm.shape[2], -1)


def _cols_chip_major(g):
    return jnp.moveaxis(g.reshape(g.shape[0], N_CHIPS, -1), 1, 0).astype(WIRE_DTYPE)


MIXERS = ("conv", "fox", "ssd")


def _model_layer(name, l):
    return l if name.startswith("ffn") else 3 * l + MIXERS.index(name.split("_")[0])


def _piece(pieces, layer):
    for arr, start in pieces:
        if start <= layer < start + arr.shape[1]:
            return arr, layer - start
    raise KeyError(layer)


def _with_own(landed, shards, chip):
    return [lax.dynamic_update_slice(g, w[None], (chip, 0, 0, 0)) for g, w in zip(landed, shards)]


def _reduce_begin(gs, place):
    from_sibling = _swap_halves_list(gs, name="reduce_halves")
    return [_add_half(g, r, place, name="reduce_add_sibling") for g, r in zip(gs, from_sibling)]


def _reduce_end(by_chip, chip_sums, place):
    reds = [_sum_chips(b, s, place, name="reduce_sum_chips") for b, s in zip(by_chip, chip_sums)]
    return _join_halves_list(reds, name="reduce_share")


def _local_step(x, tgt, fw, cm, late, place):
    depth = fw["mix_norm"].shape[0]
    chip = place[0]
    cm = {n: list(p) for n, p in cm.items()}
    late = dict(late)

    def gather_rider(group):
        return _gather_ici_rider([shards for shards, _ in group.values()]) if group else None

    def land(group, landed):
        if group:
            full = _with_own(_forward_halves_list(landed, name="gather_forward"), [shards for shards, _ in group.values()], chip)
            for (n, (_, start)), arr in zip(group.items(), full):
                cm[n].append((arr, start))

    layers = []
    xc, y_prev = x, None
    for i in range(depth):
        kind, j = i % 3, i // 3
        tag = f"l{i}"
        xin, h = _resid_rms(xc, y_prev, fw["mix_norm"][i], name=f"{tag}_norm1")
        if kind == 0:
            w_in, jl = _piece(cm["conv_w_in"], j)
            o, mc = _conv_mixer_fwd(h, w_in, jl, fw["conv_w_dw"][j], tag + "_conv")
        elif kind == 1:
            group = late.pop(("attention", i), None)
            o, mc, landed = _fox_mixer_fwd(h, _cols_natural(*_piece(cm["fox_w_in"], j)), fw["fox_b_f"][j], fw["fox_q_gain"][j],
                                           fw["fox_k_gain"][j], tag + "_fox", rider=gather_rider(group))
            land(group, landed)
        else:
            o, mc = _ssd_mixer_fwd(h, _cols_natural(*_piece(cm["ssd_w_in"], j)), fw["ssd_conv_w"][j], fw["ssd_conv_b"][j], fw["ssd_dt_bias"][j],
                                   fw["ssd_a_log"][j], fw["ssd_d"][j], fw["ssd_norm_w"][j], tag + "_ssd")
        w_out = _rows_natural(*_piece(cm[MIXERS[kind] + "_w_out"], j))
        ym = _mm(o, w_out, name=f"{tag}_mix_out")
        x1, h2 = _resid_rms(xin, ym, fw["ffn_norm"][i], name=f"{tag}_norm2")
        w_gu, il = _piece(cm["ffn_w_gu"], i)
        w_down = _rows_natural(*_piece(cm["ffn_w_down"], i))
        inter = (2, _tile(math.gcd(w_down.shape[0], w_gu.shape[3]), TILES))
        group = late.pop(("ffn", i), None)
        gu = _mm(h2, w_gu, b_layer=il, inter=inter, out_dtype=ACT_DTYPE, name=f"{tag}_ffn_gu", rider=gather_rider(group))
        if group:
            gu, landed = gu
            land(group, landed)
        a = _swiglu_fwd(gu, inter[1], name=f"{tag}_swiglu")
        yf = _mm(a, w_down, name=f"{tag}_ffn_down")
        layers.append((xin, o, mc, w_out, x1, h2, w_gu, il, w_down, inter, gu, a))
        xc, y_prev = x1, yf
    loss, dx, dxb = _loss_head(xc, y_prev, tgt, name="loss_head")

    small = {k: [None] * v.shape[0] for k, v in fw.items()}
    reds = {}
    riding = None
    at_end = []
    for i in reversed(range(depth)):
        kind, j = i % 3, i // 3
        tag = f"l{i}"
        xin, o, mc, w_out, x1, h2, w_gu, il, w_down, inter, gu, a = layers[i]
        mine = [(("ffn_w_down", i), _mm(a, dxb, ta=True, out_dtype=WIRE_DTYPE, name=f"{tag}_dw_down").reshape(N_CHIPS, -1, w_down.shape[1]))]
        da = _mm(dxb, w_down, tb=True, out_dtype=ACT_DTYPE, name=f"{tag}_da")
        dgu = _swiglu_bwd(gu, da, inter[1], name=f"{tag}_swiglu_bwd")
        mine.append((("ffn_w_gu", i), _mm(h2, dgu, ta=True, out_shard=w_gu.shape[3], inter=inter, out_dtype=WIRE_DTYPE, name=f"{tag}_dw_gu")))
        rider = riding[2] if riding and kind == 0 else None
        dh2 = _mm(dgu, w_gu, tb=True, b_layer=il, inter=inter, name=f"{tag}_dh2", rider=rider)
        if rider is not None:
            dh2, landed = dh2
            reds.update(zip(riding[0], _reduce_end(landed, riding[1], place)))
            riding = None
        dx1, dx1b, small["ffn_norm"][i] = _rms_bwd(x1, fw["ffn_norm"][i], dh2, dx, name=f"{tag}_norm2_bwd")
        mine.append(((MIXERS[kind] + "_w_out", j),
                     _mm(o, dx1b, ta=True, out_dtype=WIRE_DTYPE, name=f"{tag}_dw_out").reshape(N_CHIPS, -1, w_out.shape[1])))
        do = _mm(dx1b, w_out, tb=True, out_dtype=ACT_DTYPE, name=f"{tag}_do")
        rider = riding[2] if riding else None
        if kind == 0:
            dh, mg = _conv_mixer_bwd(mc, do, tag + "_conv")
        elif kind == 1:
            dh, mg, landed = _fox_mixer_bwd(mc, do, tag + "_fox", rider=rider)
        else:
            dh, mg, landed = _ssd_mixer_bwd(mc, do, tag + "_ssd", rider=rider)
        if rider is not None:
            reds.update(zip(riding[0], _reduce_end(landed, riding[1], place)))
            riding = None
        for k, v in mg.items():
            if k == "w_in":
                mine.append(((f"{MIXERS[kind]}_w_in", j), v if kind == 0 else _cols_chip_major(v)))
            else:
                small[f"{MIXERS[kind]}_{k}"][j] = v
        dx, dxb, small["mix_norm"][i] = _rms_bwd(xin, fw["mix_norm"][i], dh, dx1, name=f"{tag}_norm1_bwd")
        if i > 0:
            chip_sums = _reduce_begin([g for _, g in mine], place)
            riding = ([k for k, _ in mine], chip_sums, _scatter_rider(chip_sums))
        else:
            at_end += mine
    assert riding is None
    chip_sums = _reduce_begin([g for _, g in at_end], place)
    by_chip = _run_rider(_scatter_rider(chip_sums), name="reduce_chips")
    reds.update(zip([k for k, _ in at_end], _reduce_end(by_chip, chip_sums, place)))
    return loss, dx, {k: jnp.stack(v) for k, v in small.items()}, reds


ANY = pl.BlockSpec(memory_space=pl.ANY)
VMEM_SPEC = pl.BlockSpec(memory_space=pltpu.VMEM)


def _place():
    return lax.axis_index("x"), lax.axis_index("y"), lax.axis_index("c")


def _remote(src, dst, send_sems, recv_sems, k, to):
    return pltpu.make_async_remote_copy(src_ref=src, dst_ref=dst, send_sem=send_sems.at[k], recv_sem=recv_sems.at[k],
                                        device_id=to, device_id_type=MESH)


def _half_of(ref, h, shape):
    layers, rows, _ = shape
    if layers % 2 == 0:
        return ref.at[pl.ds(h * (layers // 2), layers // 2)]
    return ref.at[:, pl.ds(pl.multiple_of(h * (rows // 2), 16), rows // 2)]


def _row_half(ref, h, rows):
    return ref.at[:, pl.ds(pl.multiple_of(h * (rows // 2), 16), rows // 2)]


def _gather_list(ws, *, name):
    n = len(ws)

    def body(*refs):
        w_refs, o_refs, send_sems, recv_sems = refs[:n], refs[n:2 * n], refs[2 * n], refs[2 * n + 1]
        x, y, c = _place()
        me, sibling, m = (x, y, c), (x, y, 1 - c), 2 * x + y
        chips = [(1 - x, y), (x, 1 - y), (1 - x, 1 - y)]
        first, passed = [], []
        for p, (w_ref, o_ref) in enumerate(zip(w_refs, o_refs)):
            for j, (px, py) in enumerate(chips):
                cp = _remote(_half_of(w_ref, c, ws[p].shape), _half_of(o_ref.at[m], c, ws[p].shape), send_sems, recv_sems, 6 * p + j, (px, py, c))
                cp.start()
                first.append(cp)
        for p, o_ref in enumerate(o_refs):
            for j, (px, py) in enumerate(chips):
                blk = _half_of(o_ref.at[2 * px + py], c, ws[p].shape)
                _remote(blk, blk, send_sems, recv_sems, 6 * p + j, me).wait_recv()
                fwd = _remote(blk, blk, send_sems, recv_sems, 6 * p + 3 + j, sibling)
                fwd.start()
                passed.append(fwd)
        for p, o_ref in enumerate(o_refs):
            for j, (px, py) in enumerate(chips):
                blk = _half_of(o_ref.at[2 * px + py], 1 - c, ws[p].shape)
                _remote(blk, blk, send_sems, recv_sems, 6 * p + 3 + j, me).wait_recv()
        for cp in first + passed:
            cp.wait_send()

    return pl.pallas_call(
        body, name=name, in_specs=[ANY] * n, out_specs=[ANY] * n,
        out_shape=[jax.ShapeDtypeStruct((N_CHIPS,) + w.shape, w.dtype) for w in ws],
        scratch_shapes=[pltpu.SemaphoreType.DMA((6 * n,)), pltpu.SemaphoreType.DMA((6 * n,))],
    )(*ws)


class _Rider:
    def __init__(self, arrays, out_shapes, n_sems, start, finish):
        self.arrays, self.out_shapes, self.n_sems, self.start, self.finish = list(arrays), list(out_shapes), n_sems, start, finish

    @property
    def scratch(self):
        return [pltpu.SemaphoreType.DMA((self.n_sems,)), pltpu.SemaphoreType.DMA((self.n_sems,))]


def _carry(body, rider, n_prefetch, n_in, n_out, grid):
    if rider is None:
        return body
    ri, ro = len(rider.arrays), len(rider.out_shapes)

    def wrapped(*refs):
        pre, rest = refs[:n_prefetch], refs[n_prefetch:]
        ins, r_in = rest[:n_in], rest[n_in:n_in + ri]
        outs, r_out = rest[n_in + ri:n_in + ri + n_out], rest[n_in + ri + n_out:n_in + ri + n_out + ro]
        scratch = rest[n_in + ri + n_out + ro:]
        first = functools.reduce(jnp.logical_and, [pl.program_id(a) == 0 for a in range(len(grid))])
        last = functools.reduce(jnp.logical_and, [pl.program_id(a) == g - 1 for a, g in enumerate(grid)])

        @pl.when(first)
        def _():
            rider.start(r_in, r_out, scratch[-2], scratch[-1])

        body(*pre, *ins, *outs, *scratch[:-2])

        @pl.when(last)
        def _():
            rider.finish(r_in, r_out, scratch[-2], scratch[-1])

    return wrapped


def _run_rider(rider, *, name):
    n = len(rider.arrays)

    def body(*refs):
        rider.start(refs[:n], refs[n:2 * n], refs[-2], refs[-1])
        rider.finish(refs[:n], refs[n:2 * n], refs[-2], refs[-1])

    return pl.pallas_call(body, name=name, in_specs=[ANY] * n, out_specs=[ANY] * len(rider.out_shapes), out_shape=rider.out_shapes,
                          scratch_shapes=rider.scratch)(*rider.arrays)


def _chips_of(x, y):
    return [(1 - x, y), (x, 1 - y), (1 - x, 1 - y)]


def _gather_ici_rider(ws):
    def copies(w_refs, o_refs, send_sems, recv_sems):
        x, y, c = _place()
        m = 2 * x + y
        return [_remote(_half_of(w_ref, c, ws[p].shape), _half_of(o_ref.at[m], c, ws[p].shape), send_sems, recv_sems, 3 * p + j, (px, py, c))
                for p, (w_ref, o_ref) in enumerate(zip(w_refs, o_refs)) for j, (px, py) in enumerate(_chips_of(x, y))]

    def start(w_refs, o_refs, send_sems, recv_sems):
        for cp in copies(w_refs, o_refs, send_sems, recv_sems):
            cp.start()

    def finish(w_refs, o_refs, send_sems, recv_sems):
        x, y, c = _place()
        for p, o_ref in enumerate(o_refs):
            for j, (px, py) in enumerate(_chips_of(x, y)):
                blk = _half_of(o_ref.at[2 * px + py], c, ws[p].shape)
                _remote(blk, blk, send_sems, recv_sems, 3 * p + j, (x, y, c)).wait_recv()
        for cp in copies(w_refs, o_refs, send_sems, recv_sems):
            cp.wait_send()

    return _Rider(ws, [jax.ShapeDtypeStruct((N_CHIPS,) + w.shape, w.dtype) for w in ws], 3 * len(ws), start, finish)


def _forward_halves_list(gathered, *, name):
    n = len(gathered)
    shapes = [g.shape[1:] for g in gathered]

    def body(*refs):
        o_refs, send_sems, recv_sems = refs[n:2 * n], refs[2 * n], refs[2 * n + 1]
        x, y, c = _place()
        cps = []
        for p, o_ref in enumerate(o_refs):
            for j, (px, py) in enumerate(_chips_of(x, y)):
                blk = _half_of(o_ref.at[2 * px + py], c, shapes[p])
                cp = _remote(blk, blk, send_sems, recv_sems, 3 * p + j, (x, y, 1 - c))
                cp.start()
                cps.append(cp)
        for p, o_ref in enumerate(o_refs):
            for j, (px, py) in enumerate(_chips_of(x, y)):
                blk = _half_of(o_ref.at[2 * px + py], 1 - c, shapes[p])
                _remote(blk, blk, send_sems, recv_sems, 3 * p + j, (x, y, c)).wait_recv()
        for cp in cps:
            cp.wait_send()

    return pl.pallas_call(
        body, name=name, in_specs=[ANY] * n, out_specs=[ANY] * n, out_shape=[jax.ShapeDtypeStruct(g.shape, g.dtype) for g in gathered],
        input_output_aliases={p: p for p in range(n)},
        scratch_shapes=[pltpu.SemaphoreType.DMA((3 * n,)), pltpu.SemaphoreType.DMA((3 * n,))],
    )(*gathered)


def _scatter_rider(sums):
    def copies(a_refs, o_refs, send_sems, recv_sems):
        x, y, c = _place()
        m = 2 * x + y
        return [_remote(a_ref.at[2 * px + py], o_ref.at[m], send_sems, recv_sems, 3 * p + j, (px, py, c))
                for p, (a_ref, o_ref) in enumerate(zip(a_refs, o_refs)) for j, (px, py) in enumerate(_chips_of(x, y))]

    def start(a_refs, o_refs, send_sems, recv_sems):
        for cp in copies(a_refs, o_refs, send_sems, recv_sems):
            cp.start()

    def finish(a_refs, o_refs, send_sems, recv_sems):
        x, y, c = _place()
        for p, o_ref in enumerate(o_refs):
            for j, (px, py) in enumerate(_chips_of(x, y)):
                blk = o_ref.at[2 * px + py]
                _remote(blk, blk, send_sems, recv_sems, 3 * p + j, (x, y, c)).wait_recv()
        for cp in copies(a_refs, o_refs, send_sems, recv_sems):
            cp.wait_send()

    return _Rider(sums, [jax.ShapeDtypeStruct(a.shape, a.dtype) for a in sums], 3 * len(sums), start, finish)


def _gather_small(v, *, name):
    r, w = v.shape

    def body(v_ref, o_ref, send_sems, recv_sems):
        x, y, c = _place()
        m = 2 * x + y
        chips = [(1 - x, y), (x, 1 - y), (1 - x, 1 - y)]
        o_ref[m] = v_ref[...]
        sends = [_remote(v_ref, o_ref.at[m], send_sems, recv_sems, j, (px, py, c)) for j, (px, py) in enumerate(chips)]
        for cp in sends:
            cp.start()
        for j, (px, py) in enumerate(chips):
            blk = o_ref.at[2 * px + py]
            _remote(blk, blk, send_sems, recv_sems, j, (x, y, c)).wait_recv()
        for cp in sends:
            cp.wait_send()

    return pl.pallas_call(
        body, name=name, in_specs=[VMEM_SPEC], out_specs=VMEM_SPEC, out_shape=jax.ShapeDtypeStruct((4, r, w), v.dtype),
        scratch_shapes=[pltpu.SemaphoreType.DMA((3,)), pltpu.SemaphoreType.DMA((3,))],
    )(v)


def _swap_halves_list(gs, *, name):
    n = len(gs)

    def body(*refs):
        g_refs, o_refs, send_sems, recv_sems = refs[:n], refs[n:2 * n], refs[2 * n], refs[2 * n + 1]
        x, y, c = _place()
        cps = [_remote(_row_half(g_ref, 1 - c, gs[p].shape[1]), o_ref, send_sems, recv_sems, p, (x, y, 1 - c))
               for p, (g_ref, o_ref) in enumerate(zip(g_refs, o_refs))]
        for cp in cps:
            cp.start()
        for cp in cps:
            cp.wait()

    return pl.pallas_call(
        body, name=name, in_specs=[ANY] * n, out_specs=[ANY] * n,
        out_shape=[jax.ShapeDtypeStruct((g.shape[0], g.shape[1] // 2, g.shape[2]), g.dtype) for g in gs],
        scratch_shapes=[pltpu.SemaphoreType.DMA((n,)), pltpu.SemaphoreType.DMA((n,))],
    )(*gs)


def _join_halves_list(reds, *, name):
    n = len(reds)

    def body(*refs):
        o_refs, send_sems, recv_sems = refs[n:2 * n], refs[2 * n], refs[2 * n + 1]
        x, y, c = _place()
        cps = []
        for p, o_ref in enumerate(o_refs):
            rh = reds[p].shape[0] // 2
            mine = o_ref.at[pl.ds(pl.multiple_of(c * rh, SUBLANES), rh)]
            cp = _remote(mine, mine, send_sems, recv_sems, p, (x, y, 1 - c))
            cp.start()
            cps.append(cp)
        for p, o_ref in enumerate(o_refs):
            rh = reds[p].shape[0] // 2
            other = o_ref.at[pl.ds(pl.multiple_of((1 - c) * rh, SUBLANES), rh)]
            _remote(other, other, send_sems, recv_sems, p, (x, y, c)).wait_recv()
        for cp in cps:
            cp.wait_send()

    return pl.pallas_call(
        body, name=name, in_specs=[ANY] * n, out_specs=[ANY] * n, out_shape=[jax.ShapeDtypeStruct(r.shape, r.dtype) for r in reds],
        input_output_aliases={p: p for p in range(n)},
        scratch_shapes=[pltpu.SemaphoreType.DMA((n,)), pltpu.SemaphoreType.DMA((n,))],
    )(*reds)


def _allreduce_small(v, *, name):
    r, w = v.shape

    def body(v_ref, o_ref, slots, send_sems, recv_sems):
        x, y, c = _place()
        me = 4 * x + 2 * y + c
        slots[me] = v_ref[...]
        peers = [((1 - x) if k & 4 else x, (1 - y) if k & 2 else y, (1 - c) if k & 1 else c) for k in range(1, 8)]
        sends = [_remote(v_ref, slots.at[me], send_sems, recv_sems, k, p) for k, p in enumerate(peers)]
        for cp in sends:
            cp.start()
        for k, (px, py, pc) in enumerate(peers):
            blk = slots.at[4 * px + 2 * py + pc]
            _remote(blk, blk, send_sems, recv_sems, k, (x, y, c)).wait_recv()
        for cp in sends:
            cp.wait_send()
        acc = slots[0]
        for k in range(1, 8):
            acc = acc + slots[k]
        o_ref[...] = acc

    return pl.pallas_call(
        body, name=name, in_specs=[VMEM_SPEC], out_specs=VMEM_SPEC, out_shape=jax.ShapeDtypeStruct(v.shape, v.dtype),
        scratch_shapes=[pltpu.VMEM((8, r, w), F32), pltpu.SemaphoreType.DMA((7,)), pltpu.SemaphoreType.DMA((7,))],
    )(v)


def _row_tile(r):
    return r if r <= 512 else _tile(r, (512, 256, 128, 64, 32, 16))


def _add_half(g, recv, place, *, name):
    n, r, w = g.shape
    tm = _row_tile(r // 2)
    nb = (r // 2) // tm

    def body(place_ref, g_ref, r_ref, o_ref):
        o_ref[...] = (g_ref[...].astype(F32) + r_ref[...].astype(F32)).astype(o_ref.dtype)

    return pl.pallas_call(
        body, name=name,
        grid_spec=pltpu.PrefetchScalarGridSpec(
            num_scalar_prefetch=1, grid=(n, nb),
            in_specs=[pl.BlockSpec((None, tm, w), lambda k, i, p: (k, p[1] * nb + i, 0)), pl.BlockSpec((None, tm, w), lambda k, i, p: (k, i, 0))],
            out_specs=pl.BlockSpec((None, tm, w), lambda k, i, p: (k, i, 0))),
        out_shape=jax.ShapeDtypeStruct(recv.shape, g.dtype), compiler_params=_params(("parallel", "parallel")),
    )(place, g, recv)


def _sum_chips(recv, own, place, *, name):
    n, r, w = recv.shape
    tm = _row_tile(r)
    nb = r // tm

    def body(place_ref, *refs):
        own_ref, o_ref = refs[n], refs[n + 1]
        acc = None
        for k in range(n):
            term = jnp.where(place_ref[0] == k, own_ref[...], refs[k][...]).astype(F32)
            acc = term if acc is None else acc + term
        o_ref[...] = acc

    recv_specs = [pl.BlockSpec((None, tm, w), lambda i, p, k=k: (jnp.where(p[0] == k, (k + 1) % n, k), i, 0)) for k in range(n)]
    return pl.pallas_call(
        body, name=name,
        grid_spec=pltpu.PrefetchScalarGridSpec(
            num_scalar_prefetch=1, grid=(nb,),
            in_specs=recv_specs + [pl.BlockSpec((None, tm, w), lambda i, p: (p[0], i, 0))],
            out_specs=pl.BlockSpec((tm, w), lambda i, p: (p[1] * nb + i, 0))),
        out_shape=jax.ShapeDtypeStruct((2 * r, w), F32), compiler_params=_params(("parallel",)),
    )(place, *([recv] * n), own)


def _adamw(w, g, m, v, *, name):
    shape = w.shape
    cols = shape[-1]
    rows = math.prod(shape[:-1])
    tm = _tile(rows, (256, 128, 64, 32, 16, 8))
    c1 = 1.0 - ADAM_B1 ** ADAM_STEP
    c2 = 1.0 - ADAM_B2 ** ADAM_STEP

    def fn(i, nrt, wv, gv, mv, vv):
        mn = ADAM_B1 * mv + (1.0 - ADAM_B1) * gv
        vn = ADAM_B2 * vv + (1.0 - ADAM_B2) * (gv * gv)
        delta = -ADAM_LR * ((mn / c1) / (jnp.sqrt(vn / c2) + ADAM_EPS) + ADAM_WD * wv)
        return (delta, mn, vn), ()

    outs = _rows(fn, name=name, s=rows, tm=tm, ins=[("row", t.reshape(rows, cols), cols, _c0) for t in (w, g, m, v)],
                 outs=[(cols, cols, _c0, F32)] * 3)
    return [o.reshape(shape) for o in outs]


WEIGHTS = ["mix_norm", "ffn_norm", "ffn_w_gu", "ffn_w_down", "conv_w_in", "conv_w_dw", "conv_w_out", "fox_w_in", "fox_b_f", "fox_q_gain",
           "fox_k_gain", "fox_w_out", "ssd_w_in", "ssd_conv_w", "ssd_conv_b", "ssd_dt_bias", "ssd_a_log", "ssd_d", "ssd_norm_w", "ssd_w_out"]
SHARD_AXIS = {"ffn_w_gu": 2, "ffn_w_down": 1, "conv_w_in": 2, "conv_w_dw": 2, "conv_w_out": 1, "fox_w_in": 2, "fox_w_out": 1, "ssd_w_in": 2,
              "ssd_conv_w": 2, "ssd_conv_b": 1, "ssd_norm_w": 1, "ssd_w_out": 1}
BIG = ["ffn_w_gu", "ffn_w_down", "conv_w_in", "conv_w_out", "fox_w_in", "fox_w_out", "ssd_w_in", "ssd_w_out"]
SMALL_SHARDED = ["conv_w_dw", "ssd_conv_w", "ssd_conv_b", "ssd_norm_w"]
N_CHIPS = 4


def _pack_flat(parts, pad_to):
    flat = [p.reshape(-1) for p in parts]
    offs, n = [], 0
    for f in flat:
        offs.append(n)
        n += f.shape[0]
    total = -(-n // pad_to) * pad_to
    if total > n:
        flat.append(jnp.zeros((total - n,), flat[0].dtype))
    return jnp.concatenate(flat).reshape(-1, LANES), offs


def kernel(x, mix_norm, ffn_norm, ffn_w_gu, ffn_w_down, conv_w_in, conv_w_dw, conv_w_out, fox_w_in, fox_b_f, fox_q_gain, fox_k_gain, fox_w_out, ssd_w_in, ssd_conv_w, ssd_conv_b, ssd_dt_bias, ssd_a_log, ssd_d, ssd_norm_w, ssd_w_out, loss_target, m_mix_norm, m_ffn_norm, m_ffn_w_gu, m_ffn_w_down, m_conv_w_in, m_conv_w_dw, m_conv_w_out, m_fox_w_in, m_fox_b_f, m_fox_q_gain, m_fox_k_gain, m_fox_w_out, m_ssd_w_in, m_ssd_conv_w, m_ssd_conv_b, m_ssd_dt_bias, m_ssd_a_log, m_ssd_d, m_ssd_norm_w, m_ssd_w_out, v_mix_norm, v_ffn_norm, v_ffn_w_gu, v_ffn_w_down, v_conv_w_in, v_conv_w_dw, v_conv_w_out, v_fox_w_in, v_fox_b_f, v_fox_q_gain, v_fox_k_gain, v_fox_w_out, v_ssd_w_in, v_ssd_conv_w, v_ssd_conv_b, v_ssd_dt_bias, v_ssd_a_log, v_ssd_d, v_ssd_norm_w, v_ssd_w_out):
    w = dict(zip(WEIGHTS, (mix_norm, ffn_norm, ffn_w_gu, ffn_w_down, conv_w_in, conv_w_dw, conv_w_out, fox_w_in, fox_b_f, fox_q_gain, fox_k_gain,
                           fox_w_out, ssd_w_in, ssd_conv_w, ssd_conv_b, ssd_dt_bias, ssd_a_log, ssd_d, ssd_norm_w, ssd_w_out)))
    m1 = dict(zip(WEIGHTS, (m_mix_norm, m_ffn_norm, m_ffn_w_gu, m_ffn_w_down, m_conv_w_in, m_conv_w_dw, m_conv_w_out, m_fox_w_in, m_fox_b_f,
                            m_fox_q_gain, m_fox_k_gain, m_fox_w_out, m_ssd_w_in, m_ssd_conv_w, m_ssd_conv_b, m_ssd_dt_bias, m_ssd_a_log, m_ssd_d,
                            m_ssd_norm_w, m_ssd_w_out)))
    m2 = dict(zip(WEIGHTS, (v_mix_norm, v_ffn_norm, v_ffn_w_gu, v_ffn_w_down, v_conv_w_in, v_conv_w_dw, v_conv_w_out, v_fox_w_in, v_fox_b_f,
                            v_fox_q_gain, v_fox_k_gain, v_fox_w_out, v_ssd_w_in, v_ssd_conv_w, v_ssd_conv_b, v_ssd_dt_bias, v_ssd_a_log, v_ssd_d,
                            v_ssd_norm_w, v_ssd_w_out)))
    cx, cy, cc = _place()
    chip = 2 * cx + cy

    place = jnp.stack([chip, cc]).astype(jnp.int32)
    depth = mix_norm.shape[0]
    attention = next((i for i in range(depth) if i % 3 == 1), depth)

    def carrier_of(layer):
        return None if layer == 0 else ("ffn", layer - 1) if layer <= attention else ("attention", attention)

    early, late, cm = {}, {}, {n: [] for n in BIG}
    for n in BIG:
        wb = w[n].astype(WIRE_DTYPE)
        keys = [carrier_of(_model_layer(n, l)) for l in range(wb.shape[0])]
        for key in dict.fromkeys(keys):
            first, count = keys.index(key), keys.count(key)
            if key is None:
                early[n] = wb[first:first + count]
            else:
                late.setdefault(key, {})[n] = (wb[first:first + count], first)
    gathered = _with_own(_gather_list(list(early.values()), name="gather_weights"), list(early.values()), chip)
    for n, g_ in zip(early, gathered):
        cm[n].append((g_, 0))
    sp, soffs = _pack_flat([w[n] for n in SMALL_SHARDED], SUBLANES * LANES)
    sgath = _gather_small(sp, name="gather_small").reshape(N_CHIPS, -1)
    full = {n: w[n] for n in WEIGHTS if n not in SHARD_AXIS}
    for n, off in zip(SMALL_SHARDED, soffs):
        full[n] = jnp.concatenate([sgath[j, off:off + w[n].size].reshape(w[n].shape) for j in range(N_CHIPS)], axis=SHARD_AXIS[n])

    loss, gx, grads, reds = _local_step(x[0], loss_target[0], full, cm, late, place)
    loss = lax.psum(loss, ("x", "y", "c"))

    small_names = [n for n in WEIGHTS if n not in BIG]
    sm, smoffs = _pack_flat([grads[n] for n in small_names], SUBLANES * LANES)
    sred = _allreduce_small(sm, name="allreduce_small").reshape(-1)

    g = {n: jnp.stack([reds[(n, l)] for l in range(w[n].shape[0])]).reshape(w[n].shape) for n in BIG}
    for n, off in zip(small_names, smoffs):
        fullg = sred[off:off + grads[n].size].reshape(grads[n].shape)
        if n in SHARD_AXIS:
            ax = SHARD_AXIS[n]
            fullg = lax.dynamic_slice_in_dim(fullg, chip * w[n].shape[ax], w[n].shape[ax], axis=ax)
        g[n] = fullg

    deltas, new_m, new_v = [], [], []
    for n in WEIGHTS:
        dl, mn, vn = _adamw(w[n], g[n], m1[n], m2[n], name=f"adamw_{n}")
        deltas.append(dl)
        new_m.append(mn)
        new_v.append(vn)
    return (loss, gx[None], *[g[n] for n in WEIGHTS], *deltas, *new_m, *new_v)
```

```python
import functools
import math

import jax
import jax.numpy as jnp
from jax import lax
from jax.experimental import pallas as pl
from jax.experimental.pallas import tpu as pltpu

F32 = jnp.float32
MM_DTYPE = jnp.bfloat16
ACT_DTYPE = jnp.bfloat16
WIRE_DTYPE = jnp.bfloat16

RMS_EPS = 1e-6
HEAD = 64
SSM_STATE = 128
SSM_CHUNK = 128
LANES = 128
SUBLANES = 8
VMEM_LIMIT = 48 * 1024 * 1024

ADAM_LR, ADAM_B1, ADAM_B2, ADAM_EPS, ADAM_WD, ADAM_STEP = 0.001, 0.9, 0.999, 1e-08, 0.01, 10

HI = lax.Precision.HIGHEST
MESH = pl.DeviceIdType.MESH


def _tile(dim, prefs):
    for p in prefs:
        if dim % p == 0:
            return p
    return dim


def _params(sem):
    return pltpu.CompilerParams(dimension_semantics=sem, vmem_limit_bytes=VMEM_LIMIT)


def _sigmoid(x):
    return 1.0 / (1.0 + jnp.exp(-x))


def _softplus(x):
    return jnp.maximum(x, 0.0) + jnp.log(1.0 + jnp.exp(-jnp.abs(x)))


TILES = (1024, 1408, 768, 512, 256, 128)
MIN_STEP_WORK = 1 << 30


def _mm(a, b, *, ta=False, tb=False, add=None, out_dtype=F32, name, b_layer=None, out_shard=None, inter=None, rider=None):
    ka, m = (a.shape[0], a.shape[1]) if ta else (a.shape[1], a.shape[0])
    if b_layer is None:
        kb, n = (b.shape[1], b.shape[0]) if tb else (b.shape[0], b.shape[1])
        ns = None
    else:
        ns = b.shape[3]
        kb, n = (b.shape[0] * ns, b.shape[2]) if tb else (b.shape[2], b.shape[0] * ns)
    assert ka == kb, (a.shape, b.shape, ta, tb)
    k = ka
    tm = _tile(m, (1408, 1024, 512, 256, 128) if ta else (512, 256, 128))
    tn = _tile(n, TILES)
    tk = _tile(k, TILES)
    if inter:
        segs, bw = inter
        if tb:
            tk = bw
        else:
            tn = bw
        tps = ((k if tb else n) // bw) // segs
        col = lambda q: ((q % segs) * tps + q // segs) * bw
    else:
        col = lambda q: q * (tk if tb else tn)
    def vmem(tm_, tk_):
        out_b = jnp.dtype(out_dtype).itemsize * 2 + (8 if add is not None else 0) + 4
        return 2 * tk_ * (tm_ * a.dtype.itemsize + tn * b.dtype.itemsize) + tm_ * tn * out_b

    if ta:
        while tk * 2 <= k and k % (tk * 2) == 0 and tm * tn * tk < MIN_STEP_WORK and vmem(tm, tk * 2) < VMEM_LIMIT * 3 // 4:
            tk *= 2
    else:
        while tm * 2 <= m and m % (tm * 2) == 0 and tm * tn * tk < MIN_STEP_WORK and vmem(tm * 2, tk) < VMEM_LIMIT * 3 // 4:
            tm *= 2
    nk = k // tk
    a_spec = pl.BlockSpec((tk, tm), lambda i, j, q: (q, i)) if ta else pl.BlockSpec((tm, tk), lambda i, j, q: (i, q))
    if b_layer is None:
        b_spec = pl.BlockSpec((tn, tk), lambda i, j, q: (j, q)) if tb else pl.BlockSpec((tk, tn), lambda i, j, q: (q, j))
    elif tb:
        b_spec = pl.BlockSpec((None, None, tn, tk), lambda i, j, q: (col(q) // ns, b_layer, j, (col(q) % ns) // tk))
    else:
        b_spec = pl.BlockSpec((None, None, tk, tn), lambda i, j, q: (col(j) // ns, b_layer, q, (col(j) % ns) // tn))
    if out_shard:
        assert ta and add is None
        o_spec = pl.BlockSpec((None, tm, tn), lambda i, j, q: (col(j) // out_shard, i, (col(j) % out_shard) // tn))
        o_shape = jax.ShapeDtypeStruct((N_CHIPS, m, out_shard), out_dtype)
    else:
        o_spec = pl.BlockSpec((tm, tn), lambda i, j, q: (i, j))
        o_shape = jax.ShapeDtypeStruct((m, n), out_dtype)
    dims = (((0 if ta else 1,), (1 if tb else 0,)), ((), ()))
    has_add = add is not None

    def body(*refs):
        a_ref, b_ref = refs[0], refs[1]
        o_ref = refs[2 + has_add]
        p = lax.dot_general(a_ref[...].astype(MM_DTYPE), b_ref[...].astype(MM_DTYPE), dims, preferred_element_type=F32)

        def finish(acc):
            if has_add:
                acc = acc + refs[2][...].astype(F32)
            o_ref[...] = acc.astype(out_dtype)

        if nk == 1:
            finish(p)
        else:
            acc_ref = refs[3 + has_add]
            q = pl.program_id(2)

            @pl.when(q == 0)
            def _():
                acc_ref[...] = p

            @pl.when(q > 0)
            def _():
                acc_ref[...] += p

            @pl.when(q == nk - 1)
            def _():
                finish(acc_ref[...])

    args = [a, b] + ([add] if has_add else [])
    in_specs = [a_spec, b_spec] + ([o_spec] if has_add else [])
    grid = (m // tm, n // tn, nk)
    r_in, r_out, r_shapes, r_scratch, r_args = _rider_parts(rider)
    outs = pl.pallas_call(
        _carry(body, rider, 0, len(args), 1, grid), name=name, grid=grid, in_specs=in_specs + r_in, out_specs=[o_spec] + r_out,
        out_shape=[o_shape] + r_shapes, scratch_shapes=([pltpu.VMEM((tm, tn), F32)] if nk > 1 else []) + r_scratch,
        compiler_params=_params(("arbitrary",) * 3 if rider else ("parallel", "parallel", "arbitrary")),
    )(*args, *r_args)
    return (outs[0], list(outs[1:])) if rider else outs[0]


def _rows(fn, *, name, s, tm, ncol=1, ins, outs, accs=()):
    nrt = s // tm
    hb = tm // SUBLANES
    in_specs, args = [], []
    for spec in ins:
        kind, arr = spec[0], spec[1]
        if kind == "full":
            in_specs.append(pl.BlockSpec(arr.shape, lambda j, i: (0, 0)))
        elif kind == "col":
            _, _, bw, cmap = spec
            in_specs.append(pl.BlockSpec((arr.shape[0], bw), lambda j, i, cmap=cmap: (0, cmap(j))))
        elif kind == "row":
            _, _, bw, cmap = spec
            in_specs.append(pl.BlockSpec((tm, bw), lambda j, i, cmap=cmap: (i, cmap(j))))
        elif kind == "prev":
            _, _, bw, cmap = spec
            in_specs.append(pl.BlockSpec((SUBLANES, bw), lambda j, i, cmap=cmap: (jnp.maximum(i * hb - 1, 0), cmap(j))))
        elif kind == "next":
            _, _, bw, cmap = spec
            in_specs.append(pl.BlockSpec((SUBLANES, bw), lambda j, i, cmap=cmap: (jnp.minimum((i + 1) * hb, s // SUBLANES - 1), cmap(j))))
        else:
            raise ValueError(kind)
        args.append(arr)
    out_specs, out_shape = [], []
    for w, bw, cmap, dt in outs:
        out_specs.append(pl.BlockSpec((tm, bw), lambda j, i, cmap=cmap: (i, cmap(j))))
        out_shape.append(jax.ShapeDtypeStruct((s, w), dt))
    for r, w, bw, cmap in accs:
        out_specs.append(pl.BlockSpec((r, bw), lambda j, i, cmap=cmap: (0, cmap(j))))
        out_shape.append(jax.ShapeDtypeStruct((r, w), F32))
    n_in, n_out, n_acc = len(ins), len(outs), len(accs)

    def body(*refs):
        i = pl.program_id(1)
        vals = [r[...] for r in refs[:n_in]]
        o_vals, a_vals = fn(i, nrt, *vals)
        assert len(o_vals) == n_out and len(a_vals) == n_acc
        for r, v in zip(refs[n_in:n_in + n_out], o_vals):
            r[...] = v.astype(r.dtype)
        for r, v in zip(refs[n_in + n_out:], a_vals):
            @pl.when(i == 0)
            def _(r=r, v=v):
                r[...] = v.astype(F32)

            @pl.when(i > 0)
            def _(r=r, v=v):
                r[...] += v.astype(F32)

    res = pl.pallas_call(
        body, name=name, grid=(ncol, nrt), in_specs=in_specs, out_specs=out_specs, out_shape=out_shape,
        compiler_params=_params(("parallel", "arbitrary" if accs else "parallel")),
    )(*args)
    return res


def _c0(j):
    return 0


def _cj(j):
    return j


def _gmean(v, gs):
    w = v.shape[-1]
    tile = max(gs, LANES)
    r = lax.broadcasted_iota(jnp.int32, (tile, tile), 0) // gs
    c = lax.broadcasted_iota(jnp.int32, (tile, tile), 1) // gs
    g = jnp.where(r == c, 1.0 / gs, 0.0).astype(F32)
    parts = [jnp.dot(v[:, t * tile:(t + 1) * tile], g, precision=HI, preferred_element_type=F32) for t in range(w // tile)]
    return parts[0] if len(parts) == 1 else jnp.concatenate(parts, axis=1)


def _sum_rows(v):
    return jnp.sum(v, axis=0, keepdims=True)


def _rms(x, w, *, name):
    s, d = x.shape

    def fn(i, nrt, xv, wv):
        r = lax.rsqrt(jnp.mean(xv * xv, axis=-1, keepdims=True) + RMS_EPS)
        return (xv * r * wv,), ()

    (h,) = _rows(fn, name=name, s=s, tm=_tile(s, (512, 256, 128)), ins=[("row", x, d, _c0), ("full", w.reshape(1, d))],
                 outs=[(d, d, _c0, ACT_DTYPE)])
    return h


def _rms_bwd(x, w, dh, dx_in, *, name):
    s, d = x.shape

    def fn(i, nrt, xv, wv, dhv, dxi):
        r = lax.rsqrt(jnp.mean(xv * xv, axis=-1, keepdims=True) + RMS_EPS)
        xh = xv * r
        g = dhv * wv
        dx = dxi + r * (g - xh * jnp.mean(g * xh, axis=-1, keepdims=True))
        return (dx, dx), (_sum_rows(dhv * xh),)

    dx, dxb, dw = _rows(fn, name=name, s=s, tm=_tile(s, (512, 256, 128)),
                        ins=[("row", x, d, _c0), ("full", w.reshape(1, d)), ("row", dh, d, _c0), ("row", dx_in, d, _c0)],
                        outs=[(d, d, _c0, F32), (d, d, _c0, MM_DTYPE)], accs=[(1, d, d, _c0)])
    return dx, dxb, dw.reshape(d)


def _ffn_up(h, w_gu, layer, bw, *, name, rider=None):
    m, k = h.shape
    ns = w_gu.shape[3]
    f = N_CHIPS * ns // 2
    tm = _tile(m, (512, 256, 128))

    def w_spec(first):
        return pl.BlockSpec((None, None, k, bw), lambda i, t: ((first + t * bw) // ns, layer, 0, ((first + t * bw) % ns) // bw))

    def body(h_ref, wg_ref, wu_ref, gu_ref, a_ref):
        hv = h_ref[...].astype(MM_DTYPE)
        g = jnp.dot(hv, wg_ref[...].astype(MM_DTYPE), preferred_element_type=F32)
        u = jnp.dot(hv, wu_ref[...].astype(MM_DTYPE), preferred_element_type=F32)
        gu_ref[...] = jnp.concatenate([g, u], axis=1).astype(gu_ref.dtype)
        a_ref[...] = (g * _sigmoid(g) * u).astype(a_ref.dtype)

    grid = (m // tm, f // bw)
    r_in, r_out, r_shapes, r_scratch, r_args = _rider_parts(rider)
    outs = pl.pallas_call(
        _carry(body, rider, 0, 3, 2, grid), name=name, grid=grid,
        in_specs=[pl.BlockSpec((tm, k), lambda i, t: (i, 0)), w_spec(0), w_spec(f)] + r_in,
        out_specs=[pl.BlockSpec((tm, 2 * bw), lambda i, t: (i, t)), pl.BlockSpec((tm, bw), lambda i, t: (i, t))] + r_out,
        out_shape=[jax.ShapeDtypeStruct((m, 2 * f), ACT_DTYPE), jax.ShapeDtypeStruct((m, f), ACT_DTYPE)] + r_shapes,
        scratch_shapes=r_scratch, compiler_params=_params(("arbitrary", "arbitrary") if rider else ("parallel", "parallel")),
    )(h, w_gu, w_gu, *r_args)
    return outs[0], outs[1], list(outs[2:])


def _ffn_back(dx, w_down, gu, bw, *, name):
    m, d = dx.shape
    f = w_down.shape[0]
    tm = _tile(m, (512, 256, 128))

    def body(dx_ref, w_ref, gu_ref, o_ref):
        da = _nt(dx_ref[...].astype(MM_DTYPE), w_ref[...].astype(MM_DTYPE))
        gv, uv = gu_ref[:, :bw].astype(F32), gu_ref[:, bw:].astype(F32)
        sg = _sigmoid(gv)
        o_ref[...] = jnp.concatenate([da * uv * sg * (1.0 + gv * (1.0 - sg)), da * gv * sg], axis=1).astype(o_ref.dtype)

    return pl.pallas_call(
        body, name=name, grid=(m // tm, f // bw),
        in_specs=[pl.BlockSpec((tm, d), lambda i, t: (i, 0)), pl.BlockSpec((bw, d), lambda i, t: (t, 0)), pl.BlockSpec((tm, 2 * bw), lambda i, t: (i, t))],
        out_specs=pl.BlockSpec((tm, 2 * bw), lambda i, t: (i, t)), out_shape=jax.ShapeDtypeStruct((m, 2 * f), ACT_DTYPE),
        compiler_params=_params(("parallel", "parallel")),
    )(dx, w_down, gu)


def _loss_head(x, tgt, *, name):
    s, d = x.shape

    def fn(i, nrt, xv, tv):
        diff = xv - tv
        part = 0.5 * jnp.sum(diff * diff) / d
        return (diff / d, diff / d), (jnp.full((1, LANES), part, F32),)

    dy, dyb, loss = _rows(fn, name=name, s=s, tm=_tile(s, (512, 256, 128)),
                          ins=[("row", x, d, _c0), ("row", tgt, d, _c0)],
                          outs=[(d, d, _c0, F32), (d, d, _c0, MM_DTYPE)], accs=[(1, LANES, LANES, _c0)])
    return loss[0, 0], dy, dyb


def _shift_down(ext, j, tm):
    src = pltpu.roll(ext, j, 0) if j else ext
    return src[SUBLANES:SUBLANES + tm]


def _shift_up(ext, j, tm):
    return ext[:tm] if j == 0 else pltpu.roll(ext, ext.shape[0] - j, 0)[:tm]


def _gconv_fwd(p, w, bw, *, name):
    s, d = p.shape[0], p.shape[1] // 3
    kw = w.shape[0]
    tm = _tile(s, (512, 256, 128))

    def fn(i, nrt, pv, pp, wv):
        pv, pp = pv.astype(F32), pp.astype(F32)
        cv = pv[:, bw:2 * bw] * pv[:, 2 * bw:]
        pcv = jnp.where(i == 0, 0.0, pp[:, bw:2 * bw] * pp[:, 2 * bw:])
        ext = jnp.concatenate([pcv, cv], axis=0)
        u = sum(wv[k:k + 1, :] * _shift_down(ext, kw - 1 - k, tm) for k in range(kw))
        return (pv[:, :bw] * u,), ()

    (o,) = _rows(fn, name=name, s=s, tm=tm, ncol=d // bw, ins=[("row", p, 3 * bw, _cj), ("prev", p, 3 * bw, _cj), ("col", w, bw, _cj)],
                 outs=[(d, bw, _cj, ACT_DTYPE)])
    return o


def _gconv_bwd(p, w, do, bw, *, name):
    s, d = do.shape
    kw = w.shape[0]
    tm = _tile(s, (512, 256, 128))

    def fn(i, nrt, pv, pp, pn, dov, ndo, wv):
        pv, pp, dov = pv.astype(F32), pp.astype(F32), dov.astype(F32)
        bv, cv_, vv = pv[:, :bw], pv[:, bw:2 * bw], pv[:, 2 * bw:]
        cv = cv_ * vv
        pcv = jnp.where(i == 0, 0.0, pp[:, bw:2 * bw] * pp[:, 2 * bw:])
        ext = jnp.concatenate([pcv, cv], axis=0)
        shifted = [_shift_down(ext, kw - 1 - k, tm) for k in range(kw)]
        u = sum(wv[k:k + 1, :] * shifted[k] for k in range(kw))
        db = dov * u
        du = dov * bv
        ndu = jnp.where(i == nrt - 1, 0.0, ndo.astype(F32) * pn[:, :bw].astype(F32))
        ext2 = jnp.concatenate([du, ndu], axis=0)
        dcv = sum(wv[k:k + 1, :] * _shift_up(ext2, kw - 1 - k, tm) for k in range(kw))
        dw = jnp.concatenate([_sum_rows(du * shifted[k]) for k in range(kw)], axis=0)
        return (jnp.concatenate([db, dcv * vv, dcv * cv_], axis=1),), (dw,)

    dp, dw = _rows(fn, name=name, s=s, tm=tm, ncol=d // bw,
                   ins=[("row", p, 3 * bw, _cj), ("prev", p, 3 * bw, _cj), ("next", p, 3 * bw, _cj), ("row", do, bw, _cj),
                        ("next", do, bw, _cj), ("col", w, bw, _cj)],
                   outs=[(3 * d, 3 * bw, _cj, ACT_DTYPE)], accs=[(kw, d, bw, _cj)])
    return dp, dw


def _sconv_fwd(x, w, bias, *, name):
    s, d = x.shape
    kw = w.shape[0]
    bw = _tile(d, (512, 256, 128))
    tm = _tile(s, (512, 256, 128))

    def fn(i, nrt, xv, px, wv, bsv):
        xv = xv.astype(F32)
        ext = jnp.concatenate([jnp.where(i == 0, 0.0, px.astype(F32)), xv], axis=0)
        pre = sum(wv[k:k + 1, :] * _shift_down(ext, kw - 1 - k, tm) for k in range(kw)) + bsv
        return (pre * _sigmoid(pre),), ()

    (o,) = _rows(fn, name=name, s=s, tm=tm, ncol=d // bw,
                 ins=[("row", x, bw, _cj), ("prev", x, bw, _cj), ("col", w, bw, _cj), ("col", bias.reshape(1, d), bw, _cj)],
                 outs=[(d, bw, _cj, ACT_DTYPE)])
    return o


def _sconv_bwd(x, w, bias, dact, *, name):
    s, d = x.shape
    kw = w.shape[0]
    bw = _tile(d, (512, 256, 128))
    tm = _tile(s, (512, 256, 128))

    def fn(i, nrt, xv, px, nx, dav, nda, wv, bsv):
        xv = xv.astype(F32)
        ext = jnp.concatenate([jnp.where(i == 0, 0.0, px.astype(F32)), xv, nx.astype(F32)], axis=0)
        rows_e = tm + SUBLANES
        pre_e = sum(wv[k:k + 1, :] * _shift_down(ext, kw - 1 - k, rows_e) for k in range(kw)) + bsv
        da_e = jnp.concatenate([dav.astype(F32), jnp.where(i == nrt - 1, 0.0, nda.astype(F32))], axis=0)
        sg = _sigmoid(pre_e)
        dpre_e = da_e * sg * (1.0 + pre_e * (1.0 - sg))
        dx = sum(wv[k:k + 1, :] * _shift_up(dpre_e, kw - 1 - k, tm) for k in range(kw))
        dpre = dpre_e[:tm]
        dw = jnp.concatenate([_sum_rows(dpre * _shift_down(ext, kw - 1 - k, tm)) for k in range(kw)], axis=0)
        return (dx,), (dw, _sum_rows(dpre))

    dx, dw, db = _rows(fn, name=name, s=s, tm=tm, ncol=d // bw,
                       ins=[("row", x, bw, _cj), ("prev", x, bw, _cj), ("next", x, bw, _cj), ("row", dact, bw, _cj),
                            ("next", dact, bw, _cj), ("col", w, bw, _cj), ("col", bias.reshape(1, d), bw, _cj)],
                       outs=[(d, bw, _cj, ACT_DTYPE)], accs=[(kw, d, bw, _cj), (1, d, bw, _cj)])
    return dx, dw, db.reshape(d)


def _tri(n, reverse):
    r = lax.broadcasted_iota(jnp.int32, (n, n), 0)
    c = lax.broadcasted_iota(jnp.int32, (n, n), 1)
    return jnp.where((c >= r) if reverse else (c <= r), 1.0, 0.0).astype(F32)


def _cumsum_rows(x, *, reverse, name):
    s, w = x.shape
    ch = _tile(s, (256, 128))
    n = s // ch

    def body(x_ref, o_ref, carry):
        i = pl.program_id(0)

        @pl.when(i == 0)
        def _():
            carry[...] = jnp.zeros_like(carry)

        out = jnp.dot(_tri(ch, reverse), x_ref[...], precision=HI, preferred_element_type=F32) + carry[...]
        o_ref[...] = out
        carry[...] = out[0:1, :] if reverse else out[ch - 1:ch, :]

    imap = (lambda i: (n - 1 - i, 0)) if reverse else (lambda i: (i, 0))
    return pl.pallas_call(
        body, name=name, grid=(n,), in_specs=[pl.BlockSpec((ch, w), imap)], out_specs=pl.BlockSpec((ch, w), imap),
        out_shape=jax.ShapeDtypeStruct((s, w), F32), scratch_shapes=[pltpu.VMEM((1, w), F32)],
        compiler_params=_params(("arbitrary",)),
    )(x)


def _fox_prep(q, k, f, gq, gk, bf, *, name):
    s, d = q.shape
    scale = HEAD ** -0.5

    def fn(i, nrt, qv, kv, fv, gqv, gkv, bfv):
        qv, kv = qv.astype(F32), kv.astype(F32)
        qn = qv * lax.rsqrt(_gmean(qv * qv, HEAD) + RMS_EPS) * gqv * scale
        kn = kv * lax.rsqrt(_gmean(kv * kv, HEAD) + RMS_EPS) * gkv
        z = fv + bfv
        logf = jnp.minimum(z, 0.0) - jnp.log(1.0 + jnp.exp(-jnp.abs(z)))
        return (qn, kn, logf), ()

    return _rows(fn, name=name, s=s, tm=_tile(s, (512, 256, 128)),
                 ins=[("row", q, d, _c0), ("row", k, d, _c0), ("row", f, LANES, _c0), ("full", gq), ("full", gk), ("full", bf)],
                 outs=[(d, d, _c0, ACT_DTYPE), (d, d, _c0, ACT_DTYPE), (LANES, LANES, _c0, F32)])


def _fox_prep_bwd(q, k, f, gq, gk, bf, dqs, dkn, dlogf, *, name):
    s, d = q.shape
    scale = HEAD ** -0.5

    def fn(i, nrt, qv, kv, fv, gqv, gkv, bfv, dqv, dkv, dlf):
        outs, accs = [], []
        for xv, gv, dv, sc in ((qv, gqv, dqv, scale), (kv, gkv, dkv, 1.0)):
            xv, dv = xv.astype(F32), dv.astype(F32) * sc
            r = lax.rsqrt(_gmean(xv * xv, HEAD) + RMS_EPS)
            xh = xv * r
            g = dv * gv
            outs.append(r * (g - xh * _gmean(g * xh, HEAD)))
            accs.append(_sum_rows(dv * xh))
        z = fv + bfv
        df = dlf * _sigmoid(-z)
        outs.append(df)
        accs.append(_sum_rows(df))
        return outs, accs

    return _rows(fn, name=name, s=s, tm=_tile(s, (512, 256, 128)),
                 ins=[("row", q, d, _c0), ("row", k, d, _c0), ("row", f, LANES, _c0), ("full", gq), ("full", gk), ("full", bf),
                      ("row", dqs, d, _c0), ("row", dkn, d, _c0), ("row", dlogf, LANES, _c0)],
                 outs=[(d, d, _c0, ACT_DTYPE), (d, d, _c0, ACT_DTYPE), (LANES, LANES, _c0, ACT_DTYPE)],
                 accs=[(1, d, d, _c0), (1, d, d, _c0), (1, LANES, LANES, _c0)])


def _head_masks(shape):
    lane = lax.broadcasted_iota(jnp.int32, shape, len(shape) - 1)
    return lane < HEAD, lane >= HEAD


def _pick_lane(blk, idx):
    lane = lax.broadcasted_iota(jnp.int32, blk.shape, 1)
    return jnp.sum(jnp.where(lane == idx, blk, 0.0), axis=1, keepdims=True)


def _pick_row(blk, idx):
    sub = lax.broadcasted_iota(jnp.int32, blk.shape, 0)
    return jnp.sum(jnp.where(sub == idx, blk, 0.0), axis=0, keepdims=True)


def _fox_aug(qs, kn, cum, *, name):
    s, d = qs.shape
    hp = d // LANES

    def fn(i, nrt, qv, kv, cv):
        lane = lax.broadcasted_iota(jnp.int32, (qv.shape[0], LANES), 1)
        outs = [[], [], [], []]
        for p in range(hp):
            qt, kt = qv[:, p * LANES:(p + 1) * LANES], kv[:, p * LANES:(p + 1) * LANES]
            for h in range(2):
                mine = (lane < HEAD) if h == 0 else (lane >= HEAD)
                a0 = HEAD if h == 0 else 0
                c = cv[:, 2 * p + h:2 * p + h + 1]
                hi = c.astype(ACT_DTYPE).astype(F32)
                mid = (c - hi).astype(ACT_DTYPE).astype(F32)
                lo = (c - hi - mid).astype(ACT_DTYPE).astype(F32)
                ones = jnp.where((lane >= a0) & (lane < a0 + 3), 1.0, 0.0)
                kx = jnp.where(lane == a0, -hi, jnp.where(lane == a0 + 1, -mid, jnp.where(lane == a0 + 2, -lo, 0.0)))
                outs[h].append(jnp.where(mine, qt.astype(F32), ones))
                outs[2 + h].append(jnp.where(mine, kt.astype(F32), kx))
        return [jnp.concatenate(o, axis=1) for o in outs], ()

    return _rows(fn, name=name, s=s, tm=_tile(s, (512, 256, 128)), ins=[("row", qs, d, _c0), ("row", kn, d, _c0), ("row", cum, LANES, _c0)],
                 outs=[(d, d, _c0, ACT_DTYPE)] * 4)


def _tri_tables(nq, by_key):
    import numpy as np
    pairs = [(qi, kj) for kj in range(nq) for qi in range(kj, nq)] if by_key else [(qi, kj) for qi in range(nq) for kj in range(qi + 1)]
    return jnp.asarray(np.array([p[0] for p in pairs], np.int32)), jnp.asarray(np.array([p[1] for p in pairs], np.int32))


def _nt(a, b):
    return lax.dot_general(a, b, (((1,), (1,)), ((), ())), preferred_element_type=F32)


def _tn(a, b):
    return lax.dot_general(a, b, (((0,), (0,)), ((), ())), preferred_element_type=F32)


ATTN_BLOCKS = (1024, 512, 256, 128)


def _fox_dd(do, o, *, name):
    s, d = do.shape

    def fn(i, nrt, dov, ov):
        return (_reduce_heads(dov.astype(F32) * ov.astype(F32), d // HEAD, HEAD),), ()

    (dd,) = _rows(fn, name=name, s=s, tm=_tile(s, (512, 256, 128)), ins=[("row", do, d, _c0), ("row", o, d, _c0)],
                  outs=[(LANES, LANES, _c0, F32)])
    return dd


def _pair_rows(a, nh):
    s = a.shape[0]
    t = a[:, :nh].T.reshape(nh // 2, 2, s)
    return jnp.pad(t, ((0, 0), (0, SUBLANES - 2), (0, 0)))


def _rows01(r0, r1):
    sub = lax.broadcasted_iota(jnp.int32, (SUBLANES, r0.shape[1]), 0)
    return jnp.where(sub == 0, r0, jnp.where(sub == 1, r1, 0.0))


def _rider_parts(rider):
    if rider is None:
        return [], [], [], [], []
    return [ANY] * len(rider.arrays), [ANY] * len(rider.out_shapes), rider.out_shapes, rider.scratch, rider.arrays


def _fox_fwd_t(q_aug, k_aug, v, *, name, rider=None):
    s, d = v.shape
    bq = _tile(s, ATTN_BLOCKS)
    nq = s // bq
    hp = d // LANES
    qtab, ktab = _tri_tables(nq, by_key=False)

    def body(qt, kt, q0_ref, q1_ref, k0_ref, k1_ref, v_ref, o_ref, lse_ref, m0, m1, l0, l1, acc0, acc1):
        t = pl.program_id(1)
        qi, kj = qt[t], kt[t]
        ms, ls, accs = (m0, m1), (l0, l1), (acc0, acc1)

        @pl.when(kj == 0)
        def _():
            for h in range(2):
                ms[h][...] = jnp.full_like(ms[h], -jnp.inf)
                ls[h][...] = jnp.zeros_like(ls[h])
                accs[h][...] = jnp.zeros_like(accs[h])

        def update(diagonal):
            v2 = v_ref[...]
            qk = ((q0_ref, k0_ref), (q1_ref, k1_ref))
            sts = [_nt(qk[h][1][...], qk[h][0][...]) for h in range(2)]
            if diagonal:
                sts = [_diag_mask_t(st) for st in sts]
            m_prev = [ms[h][...] for h in range(2)]
            m_new = [jnp.maximum(m_prev[h], jnp.max(sts[h], axis=0, keepdims=True)) for h in range(2)]
            ps = [jnp.exp(sts[h] - m_new[h]) for h in range(2)]
            alpha = [jnp.exp(m_prev[h] - m_new[h]) for h in range(2)]
            for h in range(2):
                ls[h][...] = alpha[h] * ls[h][...] + jnp.sum(ps[h], axis=0, keepdims=True)
                accs[h][...] = alpha[h] * accs[h][...] + _tn(v2, ps[h].astype(MM_DTYPE))
                ms[h][...] = m_new[h]

        @pl.when(kj < qi)
        def _():
            update(False)

        @pl.when(kj == qi)
        def _():
            update(True)
            row = lax.broadcasted_iota(jnp.int32, (LANES, bq), 0)
            ot = jnp.where(row < HEAD, acc0[...] / l0[...], acc1[...] / l1[...])
            o_ref[...] = ot.T.astype(o_ref.dtype)
            lse_ref[...] = _rows01(m0[...] + jnp.log(l0[...]), m1[...] + jnp.log(l1[...]))

    blk = (bq, LANES)
    qmap = lambda p_, t, qt, kt: (qt[t], p_)
    kmap = lambda p_, t, qt, kt: (kt[t], p_)
    grid = (hp, qtab.shape[0])
    r_in, r_out, r_shapes, r_scratch, r_args = _rider_parts(rider)
    grid_spec = pltpu.PrefetchScalarGridSpec(
        num_scalar_prefetch=2, grid=grid,
        in_specs=[pl.BlockSpec(blk, qmap), pl.BlockSpec(blk, qmap), pl.BlockSpec(blk, kmap), pl.BlockSpec(blk, kmap), pl.BlockSpec(blk, kmap)] + r_in,
        out_specs=[pl.BlockSpec(blk, qmap), pl.BlockSpec((None, SUBLANES, bq), lambda p_, t, qt, kt: (p_, 0, qt[t]))] + r_out,
        scratch_shapes=[pltpu.VMEM((1, bq), F32)] * 4 + [pltpu.VMEM((LANES, bq), F32)] * 2 + r_scratch)
    outs = pl.pallas_call(
        _carry(body, rider, 2, 5, 2, grid), name=name, grid_spec=grid_spec,
        out_shape=[jax.ShapeDtypeStruct((s, d), ACT_DTYPE), jax.ShapeDtypeStruct((hp, SUBLANES, s), F32)] + r_shapes,
        compiler_params=_params(("arbitrary", "arbitrary") if rider else ("parallel", "arbitrary")),
    )(qtab, ktab, q_aug[0], q_aug[1], k_aug[0], k_aug[1], v, *r_args)
    return outs[0], outs[1], list(outs[2:])


def _diag_mask_t(st):
    key = lax.broadcasted_iota(jnp.int32, st.shape, 0)
    qry = lax.broadcasted_iota(jnp.int32, st.shape, 1)
    return jnp.where(qry >= key, st, -jnp.inf)


def _fox_bwd_t(q_aug, k_aug, v, lse, dd, do, *, name, rider=None):
    s, d = v.shape
    bq = _tile(s, ATTN_BLOCKS)
    nq = s // bq
    hp = d // LANES
    blk = (bq, LANES)
    qtab, ktab = _tri_tables(nq, by_key=True)
    n_steps = qtab.shape[0]

    def body(qt, kt, q0_ref, q1_ref, k0_ref, k1_ref, v_ref, lse_ref, dd_ref, do_ref,
             dq_ref, dk_ref, dv_ref, dcol_ref, drow_ref, dq_sc, rs_sc, dk_sc, dv_sc, cs_sc):
        t = pl.program_id(1)
        qi, kj = qt[t], kt[t]

        @pl.when(t == 0)
        def _():
            dq_sc[...] = jnp.zeros_like(dq_sc)
            rs_sc[...] = jnp.zeros_like(rs_sc)

        def update(diagonal):
            v2, do2 = v_ref[...], do_ref[...]
            masks = _head_masks(blk)
            row = lax.broadcasted_iota(jnp.int32, (LANES, bq), 0)
            off = pl.multiple_of(qi * bq, bq)
            for h, (q_ref, k_ref) in enumerate(((q0_ref, k0_ref), (q1_ref, k1_ref))):
                st = _nt(k_ref[...], q_ref[...])
                if diagonal:
                    st = _diag_mask_t(st)
                p = jnp.exp(st - lse_ref[h:h + 1, :])
                dp = _nt(v2, jnp.where(masks[h], do2, jnp.zeros_like(do2)))
                ds = p * (dp - dd_ref[h:h + 1, :])
                dsb = ds.astype(MM_DTYPE)
                dv_sc[h] += jnp.dot(p.astype(MM_DTYPE), do2, preferred_element_type=F32)
                dk_sc[h] += jnp.dot(dsb, q_ref[...], preferred_element_type=F32)
                cs_sc[h] += jnp.sum(ds, axis=1, keepdims=True)
                mine = (row < HEAD) if h == 0 else (row >= HEAD)
                dq_sc[:, pl.ds(off, bq)] += jnp.where(mine, _tn(k_ref[...], dsb), 0.0)
                rs_sc[h:h + 1, pl.ds(off, bq)] += jnp.sum(ds, axis=0, keepdims=True)

        @pl.when(qi == kj)
        def _():
            dk_sc[...] = jnp.zeros_like(dk_sc)
            dv_sc[...] = jnp.zeros_like(dv_sc)
            cs_sc[...] = jnp.zeros_like(cs_sc)
            update(True)

        @pl.when(qi > kj)
        def _():
            update(False)

        @pl.when(qi == nq - 1)
        def _():
            lo, _hi = _head_masks(blk)
            dk_ref[...] = jnp.where(lo, dk_sc[0], dk_sc[1]).astype(dk_ref.dtype)
            dv_ref[...] = jnp.where(lo, dv_sc[0], dv_sc[1]).astype(dv_ref.dtype)
            dcol_ref[...] = jnp.where(lo, cs_sc[0], cs_sc[1])

        @pl.when(t == n_steps - 1)
        def _():
            for c in range(nq):
                dq_ref[c * bq:(c + 1) * bq, :] = dq_sc[:, c * bq:(c + 1) * bq].T.astype(dq_ref.dtype)
            drow_ref[...] = rs_sc[...]

    qmap = lambda p_, t, qt, kt: (qt[t], p_)
    kmap = lambda p_, t, qt, kt: (kt[t], p_)
    rmap = lambda p_, t, qt, kt: (p_, 0, qt[t])
    grid = (hp, n_steps)
    r_in, r_out, r_shapes, r_scratch, r_args = _rider_parts(rider)
    outs = pl.pallas_call(
        _carry(body, rider, 2, 8, 5, grid), name=name,
        grid_spec=pltpu.PrefetchScalarGridSpec(
            num_scalar_prefetch=2, grid=grid,
            in_specs=[pl.BlockSpec(blk, qmap), pl.BlockSpec(blk, qmap), pl.BlockSpec(blk, kmap), pl.BlockSpec(blk, kmap), pl.BlockSpec(blk, kmap),
                      pl.BlockSpec((None, SUBLANES, bq), rmap), pl.BlockSpec((None, SUBLANES, bq), rmap), pl.BlockSpec(blk, qmap)] + r_in,
            out_specs=[pl.BlockSpec((s, LANES), lambda p_, t, qt, kt: (0, p_)), pl.BlockSpec(blk, kmap), pl.BlockSpec(blk, kmap),
                       pl.BlockSpec(blk, kmap), pl.BlockSpec((None, SUBLANES, s), lambda p_, t, qt, kt: (p_, 0, 0))] + r_out,
            scratch_shapes=[pltpu.VMEM((LANES, s), F32), pltpu.VMEM((SUBLANES, s), F32), pltpu.VMEM((2, bq, LANES), F32),
                            pltpu.VMEM((2, bq, LANES), F32), pltpu.VMEM((2, bq, 1), F32)] + r_scratch),
        out_shape=[jax.ShapeDtypeStruct((s, d), ACT_DTYPE), jax.ShapeDtypeStruct((s, d), ACT_DTYPE), jax.ShapeDtypeStruct((s, d), ACT_DTYPE),
                   jax.ShapeDtypeStruct((s, d), F32), jax.ShapeDtypeStruct((hp, SUBLANES, s), F32)] + r_shapes,
        compiler_params=_params(("arbitrary", "arbitrary") if rider else ("parallel", "arbitrary")),
    )(qtab, ktab, q_aug[0], q_aug[1], k_aug[0], k_aug[1], v, lse, dd, do, *r_args)
    return list(outs[:5]), list(outs[5:])


def _expand_heads(v, nh, hd):
    r = lax.broadcasted_iota(jnp.int32, (LANES, nh * hd), 0)
    c = lax.broadcasted_iota(jnp.int32, (LANES, nh * hd), 1) // hd
    e = jnp.where(r == c, 1.0, 0.0).astype(F32)
    return jnp.dot(v, e, precision=HI, preferred_element_type=F32)


def _reduce_heads(v, nh, hd):
    r = lax.broadcasted_iota(jnp.int32, (nh * hd, LANES), 0) // hd
    c = lax.broadcasted_iota(jnp.int32, (nh * hd, LANES), 1)
    e = jnp.where(r == c, 1.0, 0.0).astype(F32)
    return jnp.dot(v, e, precision=HI, preferred_element_type=F32)


def _ssd_prep(dt_raw, dt_bias, a_log, nh, *, name):
    s = dt_raw.shape[0]

    def fn(i, nrt, dtr, bsv, alv):
        dt = _softplus(dtr + bsv)
        acum = jnp.dot(_tri(SSM_CHUNK, False), dt * (-jnp.exp(alv)), precision=HI, preferred_element_type=F32)
        return (dt, acum, _expand_heads(dt, nh, HEAD), _expand_heads(acum, nh, HEAD)), ()

    w = nh * HEAD
    return _rows(fn, name=name, s=s, tm=SSM_CHUNK, ins=[("row", dt_raw, LANES, _c0), ("full", dt_bias), ("full", a_log)],
                 outs=[(LANES, LANES, _c0, F32), (LANES, LANES, _c0, F32), (w, w, _c0, F32), (w, w, _c0, F32)])


def _ssd_prep_bwd(dt_raw, dt_bias, a_log, ddtx, dacx, nh, *, name):
    s = dt_raw.shape[0]

    def fn(i, nrt, dtr, bsv, alv, ddx, dax):
        z = dtr + bsv
        dt = _softplus(z)
        a = -jnp.exp(alv)
        dda = jnp.dot(_tri(SSM_CHUNK, True), _reduce_heads(dax, nh, HEAD), precision=HI, preferred_element_type=F32)
        ddt = _reduce_heads(ddx, nh, HEAD) + dda * a
        dz = ddt * _sigmoid(z)
        lane = lax.broadcasted_iota(jnp.int32, dz.shape, 1)
        dz = jnp.where(lane < nh, dz, 0.0)
        return (dz,), (_sum_rows(dz), _sum_rows(dda * dt) * a)

    w = nh * HEAD
    return _rows(fn, name=name, s=s, tm=SSM_CHUNK,
                 ins=[("row", dt_raw, LANES, _c0), ("full", dt_bias), ("full", a_log), ("row", ddtx, w, _c0), ("row", dacx, w, _c0)],
                 outs=[(LANES, LANES, _c0, ACT_DTYPE)], accs=[(1, LANES, LANES, _c0), (1, LANES, LANES, _c0)])


def _ssd_decay(ac_blk, act_blk, head):
    col = _pick_lane(ac_blk, head)
    row = _pick_row(act_blk, head)
    r = lax.broadcasted_iota(jnp.int32, (SSM_CHUNK, SSM_CHUNK), 0)
    c = lax.broadcasted_iota(jnp.int32, (SSM_CHUNK, SSM_CHUNK), 1)
    return jnp.exp(jnp.where(r >= c, col - row, -jnp.inf))


def _group_masks(shape, hpg):
    lane = lax.broadcasted_iota(jnp.int32, shape, len(shape) - 1) // HEAD
    return [lane == k for k in range(hpg)]


def _ssd_scan_fwd(xs, bm, cm, dtx, acx, acum, acum_t, d_x, *, name):
    s, di = xs.shape
    ng = bm.shape[1] // SSM_STATE
    gw = di // ng
    hpg = gw // HEAD
    nc = s // SSM_CHUNK
    L = SSM_CHUNK
    nh_pad = acum_t.shape[0]

    def body(x_ref, b_ref, c_ref, dt_ref, ax_ref, ac_ref, act_ref, d_ref, y_ref, st_ref, state):
        g, c = pl.program_id(0), pl.program_id(1)

        @pl.when(c == 0)
        def _():
            state[...] = jnp.zeros_like(state)

        x4, bv, cv = x_ref[...].astype(F32), b_ref[...], c_ref[...]
        ax = ax_ref[...]
        tx = (x4 * dt_ref[...])
        cb = lax.dot_general(cv, bv, (((1,), (1,)), ((), ())), preferred_element_type=F32)
        masks = _group_masks((L, gw), hpg)
        y = jnp.zeros((L, gw), F32)
        txb = tx.astype(MM_DTYPE)
        for k in range(hpg):
            wk = (cb * _ssd_decay(ac_ref[...], act_ref[...], g * hpg + k)).astype(MM_DTYPE)
            y = y + jnp.where(masks[k], jnp.dot(wk, txb, preferred_element_type=F32), 0.0)
        prev = state[...]
        st_ref[...] = prev
        y = y + jnp.dot(cv, prev.astype(MM_DTYPE), preferred_element_type=F32) * jnp.exp(ax)
        y = y + d_ref[...] * x4
        y_ref[...] = y.astype(y_ref.dtype)
        a_last = ax[L - 1:L, :]
        sx = (tx * jnp.exp(a_last - ax)).astype(MM_DTYPE)
        state[...] = prev * jnp.exp(a_last) + lax.dot_general(bv, sx, (((0,), (0,)), ((), ())), preferred_element_type=F32)

    y, states = pl.pallas_call(
        body, name=name, grid=(ng, nc),
        in_specs=[pl.BlockSpec((L, gw), lambda g, c: (c, g)), pl.BlockSpec((L, SSM_STATE), lambda g, c: (c, g)),
                  pl.BlockSpec((L, SSM_STATE), lambda g, c: (c, g)), pl.BlockSpec((L, gw), lambda g, c: (c, g)),
                  pl.BlockSpec((L, gw), lambda g, c: (c, g)), pl.BlockSpec((L, LANES), lambda g, c: (c, 0)),
                  pl.BlockSpec((nh_pad, L), lambda g, c: (0, c)), pl.BlockSpec((1, gw), lambda g, c: (0, g))],
        out_specs=[pl.BlockSpec((L, gw), lambda g, c: (c, g)), pl.BlockSpec((None, None, SSM_STATE, gw), lambda g, c: (g, c, 0, 0))],
        out_shape=[jax.ShapeDtypeStruct((s, di), ACT_DTYPE), jax.ShapeDtypeStruct((ng, nc, SSM_STATE, gw), F32)],
        scratch_shapes=[pltpu.VMEM((SSM_STATE, gw), F32)],
        compiler_params=_params(("parallel", "arbitrary")),
    )(xs, bm, cm, dtx, acx, acum, acum_t, d_x)
    return y, states


def _ssd_scan_bwd(xs, bm, cm, dtx, acx, acum, acum_t, d_x, states, dy, *, name, rider=None):
    s, di = xs.shape
    ng = bm.shape[1] // SSM_STATE
    gw = di // ng
    hpg = gw // HEAD
    nc = s // SSM_CHUNK
    L = SSM_CHUNK
    nh_pad = acum_t.shape[0]

    def body(x_ref, b_ref, c_ref, dt_ref, ax_ref, ac_ref, act_ref, d_ref, st_ref, dy_ref,
             dx_ref, db_ref, dc_ref, ddt_ref, dax_ref, dd_ref, dstate):
        g, cc = pl.program_id(0), pl.program_id(1)

        @pl.when(cc == 0)
        def _():
            dstate[...] = jnp.zeros_like(dstate)
            dd_ref[...] = jnp.zeros_like(dd_ref)

        x4, bv, cv = x_ref[...].astype(F32), b_ref[...], c_ref[...]
        tv, ax, dyv = dt_ref[...], ax_ref[...], dy_ref[...].astype(F32)
        prev, dn = st_ref[...], dstate[...]
        dnb = dn.astype(MM_DTYPE)
        masks = _group_masks((L, gw), hpg)
        tx = x4 * tv
        txb = tx.astype(MM_DTYPE)
        e_ax = jnp.exp(ax)
        a_last = ax[L - 1:L, :]
        e_last = jnp.exp(a_last)
        ed = jnp.exp(a_last - ax)

        dx = d_ref[...] * dyv
        dd_ref[...] += _sum_rows(dyv * x4)
        dye = (dyv * e_ax).astype(MM_DTYPE)
        yo = jnp.dot(cv, prev.astype(MM_DTYPE), preferred_element_type=F32) * e_ax
        dc = lax.dot_general(dye, prev.astype(MM_DTYPE), (((1,), (1,)), ((), ())), preferred_element_type=F32)
        dprev = lax.dot_general(cv, dye, (((0,), (0,)), ((), ())), preferred_element_type=F32)
        dax = dyv * yo
        sx = tx * ed
        dsx = jnp.dot(bv, dnb, preferred_element_type=F32)
        db = lax.dot_general(sx.astype(MM_DTYPE), dnb, (((1,), (1,)), ((), ())), preferred_element_type=F32)
        dtx_ = dsx * ed
        dsx_sx = dsx * sx
        dax = dax - dsx_sx
        dlast = _sum_rows(dsx_sx) + _sum_rows(dn * prev) * e_last
        dprev = dprev + dn * e_last
        cb = lax.dot_general(cv, bv, (((1,), (1,)), ((), ())), preferred_element_type=F32)
        dcb = jnp.zeros((L, L), F32)
        lane = lax.broadcasted_iota(jnp.int32, (L, gw), 1)
        for k in range(hpg):
            dec = _ssd_decay(ac_ref[...], act_ref[...], g * hpg + k)
            wk = (cb * dec).astype(MM_DTYPE)
            dyk = jnp.where(masks[k], dyv, 0.0).astype(MM_DTYPE)
            dtx_ = dtx_ + jnp.where(masks[k], lax.dot_general(wk, dyk, (((0,), (0,)), ((), ())), preferred_element_type=F32), 0.0)
            dwk = lax.dot_general(dyk, txb, (((1,), (1,)), ((), ())), preferred_element_type=F32)
            dcb = dcb + dwk * dec
            mk = dwk * cb * dec
            da_k = jnp.sum(mk, axis=1, keepdims=True) - jnp.sum(mk.T, axis=1, keepdims=True)
            dax = dax + jnp.where(lane == k * HEAD, da_k, 0.0)
        dcbb = dcb.astype(MM_DTYPE)
        dc = dc + jnp.dot(dcbb, bv, preferred_element_type=F32)
        db = db + lax.dot_general(dcbb, cv, (((0,), (0,)), ((), ())), preferred_element_type=F32)
        sub = lax.broadcasted_iota(jnp.int32, (L, gw), 0)
        dax = dax + jnp.where(sub == L - 1, dlast, 0.0)
        dx_ref[...] = (dx + dtx_ * tv).astype(dx_ref.dtype)
        ddt_ref[...] = dtx_ * x4
        dax_ref[...] = dax
        db_ref[...] = db.astype(db_ref.dtype)
        dc_ref[...] = dc.astype(dc_ref.dtype)
        dstate[...] = dprev

    rev = lambda g, c: (nc - 1 - c, g)
    rev0 = lambda g, c: (nc - 1 - c, 0)
    grid = (ng, nc)
    r_in, r_out, r_shapes, r_scratch, r_args = _rider_parts(rider)
    outs = pl.pallas_call(
        _carry(body, rider, 0, 10, 6, grid), name=name, grid=grid,
        in_specs=[pl.BlockSpec((L, gw), rev), pl.BlockSpec((L, SSM_STATE), rev), pl.BlockSpec((L, SSM_STATE), rev),
                  pl.BlockSpec((L, gw), rev), pl.BlockSpec((L, gw), rev), pl.BlockSpec((L, LANES), rev0),
                  pl.BlockSpec((nh_pad, L), lambda g, c: (0, nc - 1 - c)), pl.BlockSpec((1, gw), lambda g, c: (0, g)),
                  pl.BlockSpec((None, None, SSM_STATE, gw), lambda g, c: (g, nc - 1 - c, 0, 0)), pl.BlockSpec((L, gw), rev)] + r_in,
        out_specs=[pl.BlockSpec((L, gw), rev), pl.BlockSpec((L, SSM_STATE), rev), pl.BlockSpec((L, SSM_STATE), rev),
                   pl.BlockSpec((L, gw), rev), pl.BlockSpec((L, gw), rev), pl.BlockSpec((1, gw), lambda g, c: (0, g))] + r_out,
        out_shape=[jax.ShapeDtypeStruct((s, di), ACT_DTYPE), jax.ShapeDtypeStruct(bm.shape, ACT_DTYPE), jax.ShapeDtypeStruct(cm.shape, ACT_DTYPE),
                   jax.ShapeDtypeStruct((s, di), F32), jax.ShapeDtypeStruct((s, di), F32), jax.ShapeDtypeStruct((1, di), F32)] + r_shapes,
        scratch_shapes=[pltpu.VMEM((SSM_STATE, gw), F32)] + r_scratch,
        compiler_params=_params(("arbitrary", "arbitrary") if rider else ("parallel", "arbitrary")),
    )(xs, bm, cm, dtx, acx, acum, acum_t, d_x, states, dy, *r_args)
    return list(outs[:6]), list(outs[6:])


def _ssd_gate(y, z, w, gs, *, name):
    s, d = y.shape

    def fn(i, nrt, yv, zv, wv):
        zv = zv.astype(F32)
        u = yv.astype(F32) * zv * _sigmoid(zv)
        return (u * lax.rsqrt(_gmean(u * u, gs) + RMS_EPS) * wv,), ()

    (o,) = _rows(fn, name=name, s=s, tm=_tile(s, (256, 128)), ins=[("row", y, d, _c0), ("row", z, d, _c0), ("full", w.reshape(1, d))],
                 outs=[(d, d, _c0, ACT_DTYPE)])
    return o


def _ssd_gate_bwd(y, z, w, do, gs, *, name):
    s, d = y.shape

    def fn(i, nrt, yv, zv, wv, dov):
        yv, zv, dov = yv.astype(F32), zv.astype(F32), dov.astype(F32)
        sg = _sigmoid(zv)
        sl = zv * sg
        u = yv * sl
        r = lax.rsqrt(_gmean(u * u, gs) + RMS_EPS)
        uh = u * r
        g = dov * wv
        du = r * (g - uh * _gmean(g * uh, gs))
        return (du * sl, du * yv * sg * (1.0 + zv * (1.0 - sg))), (_sum_rows(dov * uh),)

    dy, dz, dw = _rows(fn, name=name, s=s, tm=_tile(s, (256, 128)),
                       ins=[("row", y, d, _c0), ("row", z, d, _c0), ("full", w.reshape(1, d)), ("row", do, d, _c0)],
                       outs=[(d, d, _c0, ACT_DTYPE), (d, d, _c0, ACT_DTYPE)], accs=[(1, d, d, _c0)])
    return dy, dz, dw.reshape(d)


def _pad_lanes(w):
    return jnp.pad(w, ((0, 0), (0, LANES - w.shape[1])))


def _nt_sum(pairs, name):
    acc = None
    for a, b in pairs:
        acc = _mm(a, b, tb=True, add=acc, name=name)
    return acc


def _conv_mixer_fwd(h, w_in, layer, w_dw, tag):
    d, ns = h.shape[1], w_in.shape[3]
    inter = (3, _tile(math.gcd(d, ns), TILES))
    p = _mm(h, w_in, b_layer=layer, inter=inter, out_dtype=ACT_DTYPE, name=f"{tag}_in")
    return _gconv_fwd(p, w_dw, inter[1], name=f"{tag}_gate"), (h, w_in, layer, inter, p, w_dw)


def _conv_mixer_bwd(cache, do, tag):
    h, w_in, layer, inter, p, w_dw = cache
    dp, dw_dw = _gconv_bwd(p, w_dw, do, inter[1], name=f"{tag}_gate_bwd")
    dw_in = _mm(h, dp, ta=True, out_shard=w_in.shape[3], inter=inter, out_dtype=WIRE_DTYPE, name=f"{tag}_dw_in")
    dh = _mm(dp, w_in, tb=True, b_layer=layer, inter=inter, name=f"{tag}_dh")
    return dh, {"w_in": dw_in, "w_dw": dw_dw}


def _fox_mixer_fwd(h, w_in, b_f, q_gain, k_gain, tag, rider=None):
    d = h.shape[1]
    nh = d // HEAD
    ws = [w_in[:, k * d:(k + 1) * d] for k in range(3)] + [_pad_lanes(w_in[:, 3 * d:])]
    q, k, v = [_mm(h, w, out_dtype=ACT_DTYPE, name=f"{tag}_in") for w in ws[:3]]
    f = _mm(h, ws[3], name=f"{tag}_in_f")
    gq = jnp.tile(q_gain, nh).reshape(1, d)
    gk = jnp.tile(k_gain, nh).reshape(1, d)
    bf = _pad_lanes(b_f.reshape(1, nh))
    qs, kn, logf = _fox_prep(q, k, f, gq, gk, bf, name=f"{tag}_prep")
    cum = _cumsum_rows(logf, reverse=False, name=f"{tag}_cum")
    aug = _fox_aug(qs, kn, cum, name=f"{tag}_aug")
    q_aug, k_aug = aug[:2], aug[2:]
    o, lse, landed = _fox_fwd_t(q_aug, k_aug, v, name=f"{tag}_attn", rider=rider)
    return o, (h, ws, q, k, v, f, gq, gk, bf, q_aug, k_aug, o, lse), landed


def _fox_mixer_bwd(cache, do, tag, rider=None):
    h, ws, q, k, v, f, gq, gk, bf, q_aug, k_aug, o, lse = cache
    s, d = q.shape
    nh = d // HEAD
    dd = _pair_rows(_fox_dd(do, o, name=f"{tag}_attn_dd"), nh)
    (dqs, dkn, dv, dcol, drow), landed = _fox_bwd_t(q_aug, k_aug, v, lse, dd, do, name=f"{tag}_attn_bwd", rider=rider)
    dcum = _pad_lanes(drow[:, :2, :].reshape(nh, s).T - dcol[:, ::HEAD])
    dlogf = _cumsum_rows(dcum, reverse=True, name=f"{tag}_cum_bwd")
    dq, dk, df, dgq, dgk, dbf = _fox_prep_bwd(q, k, f, gq, gk, bf, dqs, dkn, dlogf, name=f"{tag}_prep_bwd")
    dps = (dq, dk, dv, df)
    dws = [_mm(h, dp, ta=True, name=f"{tag}_dw_in") for dp in dps]
    dw_in = jnp.concatenate(dws[:3] + [dws[3][:, :nh]], axis=1)
    dh = _nt_sum(list(zip(dps, ws)), f"{tag}_dh")
    return dh, {"w_in": dw_in, "b_f": dbf[0, :nh], "q_gain": dgq.reshape(nh, HEAD).sum(0), "k_gain": dgk.reshape(nh, HEAD).sum(0)}, landed


def _ssd_mixer_fwd(h, w_in, conv_w, conv_b, dt_bias, a_log, d_skip, norm_w, tag):
    di = norm_w.shape[0]
    nh = di // HEAD
    gn = (conv_w.shape[1] - di) // 2
    cuts = [0, di, 2 * di, 2 * di + gn, 2 * di + 2 * gn]
    ws = [w_in[:, cuts[k]:cuts[k + 1]] for k in range(4)] + [_pad_lanes(w_in[:, cuts[4]:])]
    z, xr, br, cr = [_mm(h, w, out_dtype=ACT_DTYPE, name=f"{tag}_in") for w in ws[:4]]
    dtr = _mm(h, ws[4], name=f"{tag}_in_dt")
    ccuts = [0, di, di + gn, di + 2 * gn]
    cws = [conv_w[:, ccuts[k]:ccuts[k + 1]] for k in range(3)]
    cbs = [conv_b[ccuts[k]:ccuts[k + 1]] for k in range(3)]
    xs, bm, cm = [_sconv_fwd(r, w, b, name=f"{tag}_conv") for r, w, b in zip((xr, br, cr), cws, cbs)]
    dtb = _pad_lanes(dt_bias.reshape(1, nh))
    alg = _pad_lanes(a_log.reshape(1, nh))
    _dt, acum, dtx, acx = _ssd_prep(dtr, dtb, alg, nh, name=f"{tag}_prep")
    acum_t = acum[:, :nh].T
    d_x = jnp.repeat(d_skip, HEAD).reshape(1, di)
    y, states = _ssd_scan_fwd(xs, bm, cm, dtx, acx, acum, acum_t, d_x, name=f"{tag}_scan")
    gs = di // (gn // SSM_STATE)
    o = _ssd_gate(y, z, norm_w, gs, name=f"{tag}_gate")
    return o, (h, ws, z, (xr, br, cr), dtr, cws, cbs, xs, bm, cm, dtb, alg, dtx, acx, acum, acum_t, d_x, states, y, norm_w, gs, nh)


def _ssd_mixer_bwd(cache, do, tag, rider=None):
    h, ws, z, raws, dtr, cws, cbs, xs, bm, cm, dtb, alg, dtx, acx, acum, acum_t, d_x, states, y, norm_w, gs, nh = cache
    dy, dz, dnorm = _ssd_gate_bwd(y, z, norm_w, do, gs, name=f"{tag}_gate_bwd")
    (dxs, dbm, dcm, ddtx, dacx, dd_x), landed = _ssd_scan_bwd(xs, bm, cm, dtx, acx, acum, acum_t, d_x, states, dy, name=f"{tag}_scan_bwd",
                                                             rider=rider)
    ddtr, ddtb, dalg = _ssd_prep_bwd(dtr, dtb, alg, ddtx, dacx, nh, name=f"{tag}_prep_bwd")
    conv = [_sconv_bwd(r, w, b, da, name=f"{tag}_conv_bwd") for r, w, b, da in zip(raws, cws, cbs, (dxs, dbm, dcm))]
    dps = (dz, conv[0][0], conv[1][0], conv[2][0], ddtr)
    dws = [_mm(h, dp, ta=True, name=f"{tag}_dw_in") for dp in dps]
    dw_in = jnp.concatenate(dws[:4] + [dws[4][:, :nh]], axis=1)
    dh = _nt_sum(list(zip(dps, ws)), f"{tag}_dh")
    return dh, {"w_in": dw_in, "conv_w": jnp.concatenate([c[1] for c in conv], axis=1), "conv_b": jnp.concatenate([c[2] for c in conv]),
                "dt_bias": ddtb[0, :nh], "a_log": dalg[0, :nh], "d": dd_x.reshape(nh, HEAD).sum(1), "norm_w": dnorm}, landed


def _rows_natural(cm, layer):
    return cm[:, layer].reshape(-1, cm.shape[3])


def _cols_natural(cm, layer):
    return jnp.moveaxis(cm[:, layer], 0, 1).reshape(cm.shape[2], -1)


def _cols_chip_major(g):
    return jnp.moveaxis(g.reshape(g.shape[0], N_CHIPS, -1), 1, 0).astype(WIRE_DTYPE)


MIXERS = ("conv", "fox", "ssd")


def _model_layer(name, l):
    return l if name.startswith("ffn") else 3 * l + MIXERS.index(name.split("_")[0])


def _piece(pieces, layer):
    for arr, start in pieces:
        if start <= layer < start + arr.shape[1]:
            return arr, layer - start
    raise KeyError(layer)


def _with_own(landed, shards, chip):
    return [lax.dynamic_update_slice(g, w[None], (chip, 0, 0, 0)) for g, w in zip(landed, shards)]


def _reduce_begin(gs, place):
    from_sibling = _swap_halves_list(gs, name="reduce_halves")
    return [_add_half(g, r, place, name="reduce_add_sibling") for g, r in zip(gs, from_sibling)]


def _reduce_end(by_chip, chip_sums, place):
    reds = [_sum_chips(b, s, place, name="reduce_sum_chips") for b, s in zip(by_chip, chip_sums)]
    return _join_halves_list(reds, name="reduce_share")


def _local_step(x, tgt, fw, cm, late, place):
    depth = fw["mix_norm"].shape[0]
    chip = place[0]
    cm = {n: list(p) for n, p in cm.items()}
    late = dict(late)

    def gather_rider(group):
        return _gather_ici_rider([shards for shards, _ in group.values()]) if group else None

    def land(group, landed):
        if group:
            full = _with_own(_forward_halves_list(landed, name="gather_forward"), [shards for shards, _ in group.values()], chip)
            for (n, (_, start)), arr in zip(group.items(), full):
                cm[n].append((arr, start))

    layers = []
    xin = x
    for i in range(depth):
        kind, j = i % 3, i // 3
        tag = f"l{i}"
        h = _rms(xin, fw["mix_norm"][i], name=f"{tag}_norm1")
        if kind == 0:
            w_in, jl = _piece(cm["conv_w_in"], j)
            o, mc = _conv_mixer_fwd(h, w_in, jl, fw["conv_w_dw"][j], tag + "_conv")
        elif kind == 1:
            group = late.pop(("attention", i), None)
            o, mc, landed = _fox_mixer_fwd(h, _cols_natural(*_piece(cm["fox_w_in"], j)), fw["fox_b_f"][j], fw["fox_q_gain"][j],
                                           fw["fox_k_gain"][j], tag + "_fox", rider=gather_rider(group))
            land(group, landed)
        else:
            o, mc = _ssd_mixer_fwd(h, _cols_natural(*_piece(cm["ssd_w_in"], j)), fw["ssd_conv_w"][j], fw["ssd_conv_b"][j], fw["ssd_dt_bias"][j],
                                   fw["ssd_a_log"][j], fw["ssd_d"][j], fw["ssd_norm_w"][j], tag + "_ssd")
        w_out = _rows_natural(*_piece(cm[MIXERS[kind] + "_w_out"], j))
        x1 = _mm(o, w_out, add=xin, name=f"{tag}_mix_out")
        h2 = _rms(x1, fw["ffn_norm"][i], name=f"{tag}_norm2")
        w_gu, il = _piece(cm["ffn_w_gu"], i)
        w_down = _rows_natural(*_piece(cm["ffn_w_down"], i))
        inter = (2, _tile(math.gcd(w_down.shape[0], w_gu.shape[3]), TILES))
        group = late.pop(("ffn", i), None)
        gu, a, landed = _ffn_up(h2, w_gu, il, inter[1], name=f"{tag}_ffn_gu", rider=gather_rider(group))
        land(group, landed)
        layers.append((xin, o, mc, w_out, x1, h2, w_gu, il, w_down, inter, gu, a))
        xin = _mm(a, w_down, add=x1, name=f"{tag}_ffn_down")
    loss, dx, dxb = _loss_head(xin, tgt, name="loss_head")

    small = {k: [None] * v.shape[0] for k, v in fw.items()}
    reds = {}
    riding = None
    at_end = []
    for i in reversed(range(depth)):
        kind, j = i % 3, i // 3
        tag = f"l{i}"
        xin, o, mc, w_out, x1, h2, w_gu, il, w_down, inter, gu, a = layers[i]
        mine = [(("ffn_w_down", i), _mm(a, dxb, ta=True, out_dtype=WIRE_DTYPE, name=f"{tag}_dw_down").reshape(N_CHIPS, -1, w_down.shape[1]))]
        dgu = _ffn_back(dxb, w_down, gu, inter[1], name=f"{tag}_ffn_back")
        mine.append((("ffn_w_gu", i), _mm(h2, dgu, ta=True, out_shard=w_gu.shape[3], inter=inter, out_dtype=WIRE_DTYPE, name=f"{tag}_dw_gu")))
        rider = riding[2] if riding and kind == 0 else None
        dh2 = _mm(dgu, w_gu, tb=True, b_layer=il, inter=inter, name=f"{tag}_dh2", rider=rider)
        if rider is not None:
            dh2, landed = dh2
            reds.update(zip(riding[0], _reduce_end(landed, riding[1], place)))
            riding = None
        dx1, dx1b, small["ffn_norm"][i] = _rms_bwd(x1, fw["ffn_norm"][i], dh2, dx, name=f"{tag}_norm2_bwd")
        mine.append(((MIXERS[kind] + "_w_out", j),
                     _mm(o, dx1b, ta=True, out_dtype=WIRE_DTYPE, name=f"{tag}_dw_out").reshape(N_CHIPS, -1, w_out.shape[1])))
        do = _mm(dx1b, w_out, tb=True, out_dtype=ACT_DTYPE, name=f"{tag}_do")
        rider = riding[2] if riding else None
        if kind == 0:
            dh, mg = _conv_mixer_bwd(mc, do, tag + "_conv")
        elif kind == 1:
            dh, mg, landed = _fox_mixer_bwd(mc, do, tag + "_fox", rider=rider)
        else:
            dh, mg, landed = _ssd_mixer_bwd(mc, do, tag + "_ssd", rider=rider)
        if rider is not None:
            reds.update(zip(riding[0], _reduce_end(landed, riding[1], place)))
            riding = None
        for k, v in mg.items():
            if k == "w_in":
                mine.append(((f"{MIXERS[kind]}_w_in", j), v if kind == 0 else _cols_chip_major(v)))
            else:
                small[f"{MIXERS[kind]}_{k}"][j] = v
        dx, dxb, small["mix_norm"][i] = _rms_bwd(xin, fw["mix_norm"][i], dh, dx1, name=f"{tag}_norm1_bwd")
        if i > 0:
            chip_sums = _reduce_begin([g for _, g in mine], place)
            riding = ([k for k, _ in mine], chip_sums, _scatter_rider(chip_sums))
        else:
            at_end += mine
    assert riding is None
    chip_sums = _reduce_begin([g for _, g in at_end], place)
    by_chip = _run_rider(_scatter_rider(chip_sums), name="reduce_chips")
    reds.update(zip([k for k, _ in at_end], _reduce_end(by_chip, chip_sums, place)))
    return loss, dx, {k: jnp.stack(v) for k, v in small.items()}, reds


ANY = pl.BlockSpec(memory_space=pl.ANY)
VMEM_SPEC = pl.BlockSpec(memory_space=pltpu.VMEM)


def _place():
    return lax.axis_index("x"), lax.axis_index("y"), lax.axis_index("c")


def _remote(src, dst, send_sems, recv_sems, k, to):
    return pltpu.make_async_remote_copy(src_ref=src, dst_ref=dst, send_sem=send_sems.at[k], recv_sem=recv_sems.at[k],
                                        device_id=to, device_id_type=MESH)


def _half_of(ref, h, shape):
    layers, rows, _ = shape
    if layers % 2 == 0:
        return ref.at[pl.ds(h * (layers // 2), layers // 2)]
    return ref.at[:, pl.ds(pl.multiple_of(h * (rows // 2), 16), rows // 2)]


def _row_half(ref, h, rows):
    return ref.at[:, pl.ds(pl.multiple_of(h * (rows // 2), 16), rows // 2)]


def _gather_list(ws, *, name):
    n = len(ws)

    def body(*refs):
        w_refs, o_refs, send_sems, recv_sems = refs[:n], refs[n:2 * n], refs[2 * n], refs[2 * n + 1]
        x, y, c = _place()
        me, sibling, m = (x, y, c), (x, y, 1 - c), 2 * x + y
        chips = [(1 - x, y), (x, 1 - y), (1 - x, 1 - y)]
        first, passed = [], []
        for p, (w_ref, o_ref) in enumerate(zip(w_refs, o_refs)):
            for j, (px, py) in enumerate(chips):
                cp = _remote(_half_of(w_ref, c, ws[p].shape), _half_of(o_ref.at[m], c, ws[p].shape), send_sems, recv_sems, 6 * p + j, (px, py, c))
                cp.start()
                first.append(cp)
        for p, o_ref in enumerate(o_refs):
            for j, (px, py) in enumerate(chips):
                blk = _half_of(o_ref.at[2 * px + py], c, ws[p].shape)
                _remote(blk, blk, send_sems, recv_sems, 6 * p + j, me).wait_recv()
                fwd = _remote(blk, blk, send_sems, recv_sems, 6 * p + 3 + j, sibling)
                fwd.start()
                passed.append(fwd)
        for p, o_ref in enumerate(o_refs):
            for j, (px, py) in enumerate(chips):
                blk = _half_of(o_ref.at[2 * px + py], 1 - c, ws[p].shape)
                _remote(blk, blk, send_sems, recv_sems, 6 * p + 3 + j, me).wait_recv()
        for cp in first + passed:
            cp.wait_send()

    return pl.pallas_call(
        body, name=name, in_specs=[ANY] * n, out_specs=[ANY] * n,
        out_shape=[jax.ShapeDtypeStruct((N_CHIPS,) + w.shape, w.dtype) for w in ws],
        scratch_shapes=[pltpu.SemaphoreType.DMA((6 * n,)), pltpu.SemaphoreType.DMA((6 * n,))],
    )(*ws)


class _Rider:
    def __init__(self, arrays, out_shapes, n_sems, start, finish):
        self.arrays, self.out_shapes, self.n_sems, self.start, self.finish = list(arrays), list(out_shapes), n_sems, start, finish

    @property
    def scratch(self):
        return [pltpu.SemaphoreType.DMA((self.n_sems,)), pltpu.SemaphoreType.DMA((self.n_sems,))]


def _carry(body, rider, n_prefetch, n_in, n_out, grid):
    if rider is None:
        return body
    ri, ro = len(rider.arrays), len(rider.out_shapes)

    def wrapped(*refs):
        pre, rest = refs[:n_prefetch], refs[n_prefetch:]
        ins, r_in = rest[:n_in], rest[n_in:n_in + ri]
        outs, r_out = rest[n_in + ri:n_in + ri + n_out], rest[n_in + ri + n_out:n_in + ri + n_out + ro]
        scratch = rest[n_in + ri + n_out + ro:]
        first = functools.reduce(jnp.logical_and, [pl.program_id(a) == 0 for a in range(len(grid))])
        last = functools.reduce(jnp.logical_and, [pl.program_id(a) == g - 1 for a, g in enumerate(grid)])

        @pl.when(first)
        def _():
            rider.start(r_in, r_out, scratch[-2], scratch[-1])

        body(*pre, *ins, *outs, *scratch[:-2])

        @pl.when(last)
        def _():
            rider.finish(r_in, r_out, scratch[-2], scratch[-1])

    return wrapped


def _run_rider(rider, *, name):
    n = len(rider.arrays)

    def body(*refs):
        rider.start(refs[:n], refs[n:2 * n], refs[-2], refs[-1])
        rider.finish(refs[:n], refs[n:2 * n], refs[-2], refs[-1])

    return pl.pallas_call(body, name=name, in_specs=[ANY] * n, out_specs=[ANY] * len(rider.out_shapes), out_shape=rider.out_shapes,
                          scratch_shapes=rider.scratch)(*rider.arrays)


def _chips_of(x, y):
    return [(1 - x, y), (x, 1 - y), (1 - x, 1 - y)]


def _gather_ici_rider(ws):
    def copies(w_refs, o_refs, send_sems, recv_sems):
        x, y, c = _place()
        m = 2 * x + y
        return [_remote(_half_of(w_ref, c, ws[p].shape), _half_of(o_ref.at[m], c, ws[p].shape), send_sems, recv_sems, 3 * p + j, (px, py, c))
                for p, (w_ref, o_ref) in enumerate(zip(w_refs, o_refs)) for j, (px, py) in enumerate(_chips_of(x, y))]

    def start(w_refs, o_refs, send_sems, recv_sems):
        for cp in copies(w_refs, o_refs, send_sems, recv_sems):
            cp.start()

    def finish(w_refs, o_refs, send_sems, recv_sems):
        x, y, c = _place()
        for p, o_ref in enumerate(o_refs):
            for j, (px, py) in enumerate(_chips_of(x, y)):
                blk = _half_of(o_ref.at[2 * px + py], c, ws[p].shape)
                _remote(blk, blk, send_sems, recv_sems, 3 * p + j, (x, y, c)).wait_recv()
        for cp in copies(w_refs, o_refs, send_sems, recv_sems):
            cp.wait_send()

    return _Rider(ws, [jax.ShapeDtypeStruct((N_CHIPS,) + w.shape, w.dtype) for w in ws], 3 * len(ws), start, finish)


def _forward_halves_list(gathered, *, name):
    n = len(gathered)
    shapes = [g.shape[1:] for g in gathered]

    def body(*refs):
        o_refs, send_sems, recv_sems = refs[n:2 * n], refs[2 * n], refs[2 * n + 1]
        x, y, c = _place()
        cps = []
        for p, o_ref in enumerate(o_refs):
            for j, (px, py) in enumerate(_chips_of(x, y)):
                blk = _half_of(o_ref.at[2 * px + py], c, shapes[p])
                cp = _remote(blk, blk, send_sems, recv_sems, 3 * p + j, (x, y, 1 - c))
                cp.start()
                cps.append(cp)
        for p, o_ref in enumerate(o_refs):
            for j, (px, py) in enumerate(_chips_of(x, y)):
                blk = _half_of(o_ref.at[2 * px + py], 1 - c, shapes[p])
                _remote(blk, blk, send_sems, recv_sems, 3 * p + j, (x, y, c)).wait_recv()
        for cp in cps:
            cp.wait_send()

    return pl.pallas_call(
        body, name=name, in_specs=[ANY] * n, out_specs=[ANY] * n, out_shape=[jax.ShapeDtypeStruct(g.shape, g.dtype) for g in gathered],
        input_output_aliases={p: p for p in range(n)},
        scratch_shapes=[pltpu.SemaphoreType.DMA((3 * n,)), pltpu.SemaphoreType.DMA((3 * n,))],
    )(*gathered)


def _scatter_rider(sums):
    def copies(a_refs, o_refs, send_sems, recv_sems):
        x, y, c = _place()
        m = 2 * x + y
        return [_remote(a_ref.at[2 * px + py], o_ref.at[m], send_sems, recv_sems, 3 * p + j, (px, py, c))
                for p, (a_ref, o_ref) in enumerate(zip(a_refs, o_refs)) for j, (px, py) in enumerate(_chips_of(x, y))]

    def start(a_refs, o_refs, send_sems, recv_sems):
        for cp in copies(a_refs, o_refs, send_sems, recv_sems):
            cp.start()

    def finish(a_refs, o_refs, send_sems, recv_sems):
        x, y, c = _place()
        for p, o_ref in enumerate(o_refs):
            for j, (px, py) in enumerate(_chips_of(x, y)):
                blk = o_ref.at[2 * px + py]
                _remote(blk, blk, send_sems, recv_sems, 3 * p + j, (x, y, c)).wait_recv()
        for cp in copies(a_refs, o_refs, send_sems, recv_sems):
            cp.wait_send()

    return _Rider(sums, [jax.ShapeDtypeStruct(a.shape, a.dtype) for a in sums], 3 * len(sums), start, finish)


def _gather_small(v, *, name):
    r, w = v.shape

    def body(v_ref, o_ref, send_sems, recv_sems):
        x, y, c = _place()
        m = 2 * x + y
        chips = [(1 - x, y), (x, 1 - y), (1 - x, 1 - y)]
        o_ref[m] = v_ref[...]
        sends = [_remote(v_ref, o_ref.at[m], send_sems, recv_sems, j, (px, py, c)) for j, (px, py) in enumerate(chips)]
        for cp in sends:
            cp.start()
        for j, (px, py) in enumerate(chips):
            blk = o_ref.at[2 * px + py]
            _remote(blk, blk, send_sems, recv_sems, j, (x, y, c)).wait_recv()
        for cp in sends:
            cp.wait_send()

    return pl.pallas_call(
        body, name=name, in_specs=[VMEM_SPEC], out_specs=VMEM_SPEC, out_shape=jax.ShapeDtypeStruct((4, r, w), v.dtype),
        scratch_shapes=[pltpu.SemaphoreType.DMA((3,)), pltpu.SemaphoreType.DMA((3,))],
    )(v)


def _swap_halves_list(gs, *, name):
    n = len(gs)

    def body(*refs):
        g_refs, o_refs, send_sems, recv_sems = refs[:n], refs[n:2 * n], refs[2 * n], refs[2 * n + 1]
        x, y, c = _place()
        cps = [_remote(_row_half(g_ref, 1 - c, gs[p].shape[1]), o_ref, send_sems, recv_sems, p, (x, y, 1 - c))
               for p, (g_ref, o_ref) in enumerate(zip(g_refs, o_refs))]
        for cp in cps:
            cp.start()
        for cp in cps:
            cp.wait()

    return pl.pallas_call(
        body, name=name, in_specs=[ANY] * n, out_specs=[ANY] * n,
        out_shape=[jax.ShapeDtypeStruct((g.shape[0], g.shape[1] // 2, g.shape[2]), g.dtype) for g in gs],
        scratch_shapes=[pltpu.SemaphoreType.DMA((n,)), pltpu.SemaphoreType.DMA((n,))],
    )(*gs)


def _join_halves_list(reds, *, name):
    n = len(reds)

    def body(*refs):
        o_refs, send_sems, recv_sems = refs[n:2 * n], refs[2 * n], refs[2 * n + 1]
        x, y, c = _place()
        cps = []
        for p, o_ref in enumerate(o_refs):
            rh = reds[p].shape[0] // 2
            mine = o_ref.at[pl.ds(pl.multiple_of(c * rh, SUBLANES), rh)]
            cp = _remote(mine, mine, send_sems, recv_sems, p, (x, y, 1 - c))
            cp.start()
            cps.append(cp)
        for p, o_ref in enumerate(o_refs):
            rh = reds[p].shape[0] // 2
            other = o_ref.at[pl.ds(pl.multiple_of((1 - c) * rh, SUBLANES), rh)]
            _remote(other, other, send_sems, recv_sems, p, (x, y, c)).wait_recv()
        for cp in cps:
            cp.wait_send()

    return pl.pallas_call(
        body, name=name, in_specs=[ANY] * n, out_specs=[ANY] * n, out_shape=[jax.ShapeDtypeStruct(r.shape, r.dtype) for r in reds],
        input_output_aliases={p: p for p in range(n)},
        scratch_shapes=[pltpu.SemaphoreType.DMA((n,)), pltpu.SemaphoreType.DMA((n,))],
    )(*reds)


def _allreduce_small(v, *, name):
    r, w = v.shape

    def body(v_ref, o_ref, slots, send_sems, recv_sems):
        x, y, c = _place()
        me = 4 * x + 2 * y + c
        slots[me] = v_ref[...]
        peers = [((1 - x) if k & 4 else x, (1 - y) if k & 2 else y, (1 - c) if k & 1 else c) for k in range(1, 8)]
        sends = [_remote(v_ref, slots.at[me], send_sems, recv_sems, k, p) for k, p in enumerate(peers)]
        for cp in sends:
            cp.start()
        for k, (px, py, pc) in enumerate(peers):
            blk = slots.at[4 * px + 2 * py + pc]
            _remote(blk, blk, send_sems, recv_sems, k, (x, y, c)).wait_recv()
        for cp in sends:
            cp.wait_send()
        acc = slots[0]
        for k in range(1, 8):
            acc = acc + slots[k]
        o_ref[...] = acc

    return pl.pallas_call(
        body, name=name, in_specs=[VMEM_SPEC], out_specs=VMEM_SPEC, out_shape=jax.ShapeDtypeStruct(v.shape, v.dtype),
        scratch_shapes=[pltpu.VMEM((8, r, w), F32), pltpu.SemaphoreType.DMA((7,)), pltpu.SemaphoreType.DMA((7,))],
    )(v)


def _row_tile(r):
    return r if r <= 512 else _tile(r, (512, 256, 128, 64, 32, 16))


def _add_half(g, recv, place, *, name):
    n, r, w = g.shape
    tm = _row_tile(r // 2)
    nb = (r // 2) // tm

    def body(place_ref, g_ref, r_ref, o_ref):
        o_ref[...] = (g_ref[...].astype(F32) + r_ref[...].astype(F32)).astype(o_ref.dtype)

    return pl.pallas_call(
        body, name=name,
        grid_spec=pltpu.PrefetchScalarGridSpec(
            num_scalar_prefetch=1, grid=(n, nb),
            in_specs=[pl.BlockSpec((None, tm, w), lambda k, i, p: (k, p[1] * nb + i, 0)), pl.BlockSpec((None, tm, w), lambda k, i, p: (k, i, 0))],
            out_specs=pl.BlockSpec((None, tm, w), lambda k, i, p: (k, i, 0))),
        out_shape=jax.ShapeDtypeStruct(recv.shape, g.dtype), compiler_params=_params(("parallel", "parallel")),
    )(place, g, recv)


def _sum_chips(recv, own, place, *, name):
    n, r, w = recv.shape
    tm = _row_tile(r)
    nb = r // tm

    def body(place_ref, *refs):
        own_ref, o_ref = refs[n], refs[n + 1]
        acc = None
        for k in range(n):
            term = jnp.where(place_ref[0] == k, own_ref[...], refs[k][...]).astype(F32)
            acc = term if acc is None else acc + term
        o_ref[...] = acc

    recv_specs = [pl.BlockSpec((None, tm, w), lambda i, p, k=k: (jnp.where(p[0] == k, (k + 1) % n, k), i, 0)) for k in range(n)]
    return pl.pallas_call(
        body, name=name,
        grid_spec=pltpu.PrefetchScalarGridSpec(
            num_scalar_prefetch=1, grid=(nb,),
            in_specs=recv_specs + [pl.BlockSpec((None, tm, w), lambda i, p: (p[0], i, 0))],
            out_specs=pl.BlockSpec((tm, w), lambda i, p: (p[1] * nb + i, 0))),
        out_shape=jax.ShapeDtypeStruct((2 * r, w), F32), compiler_params=_params(("parallel",)),
    )(place, *([recv] * n), own)


def _adamw(w, g, m, v, *, name):
    shape = w.shape
    cols = shape[-1]
    rows = math.prod(shape[:-1])
    tm = _tile(rows, (256, 128, 64, 32, 16, 8))
    c1 = 1.0 - ADAM_B1 ** ADAM_STEP
    c2 = 1.0 - ADAM_B2 ** ADAM_STEP

    def fn(i, nrt, wv, gv, mv, vv):
        mn = ADAM_B1 * mv + (1.0 - ADAM_B1) * gv
        vn = ADAM_B2 * vv + (1.0 - ADAM_B2) * (gv * gv)
        delta = -ADAM_LR * ((mn / c1) / (jnp.sqrt(vn / c2) + ADAM_EPS) + ADAM_WD * wv)
        return (delta, mn, vn), ()

    outs = _rows(fn, name=name, s=rows, tm=tm, ins=[("row", t.reshape(rows, cols), cols, _c0) for t in (w, g, m, v)],
                 outs=[(cols, cols, _c0, F32)] * 3)
    return [o.reshape(shape) for o in outs]


WEIGHTS = ["mix_norm", "ffn_norm", "ffn_w_gu", "ffn_w_down", "conv_w_in", "conv_w_dw", "conv_w_out", "fox_w_in", "fox_b_f", "fox_q_gain",
           "fox_k_gain", "fox_w_out", "ssd_w_in", "ssd_conv_w", "ssd_conv_b", "ssd_dt_bias", "ssd_a_log", "ssd_d", "ssd_norm_w", "ssd_w_out"]
SHARD_AXIS = {"ffn_w_gu": 2, "ffn_w_down": 1, "conv_w_in": 2, "conv_w_dw": 2, "conv_w_out": 1, "fox_w_in": 2, "fox_w_out": 1, "ssd_w_in": 2,
              "ssd_conv_w": 2, "ssd_conv_b": 1, "ssd_norm_w": 1, "ssd_w_out": 1}
BIG = ["ffn_w_gu", "ffn_w_down", "conv_w_in", "conv_w_out", "fox_w_in", "fox_w_out", "ssd_w_in", "ssd_w_out"]
SMALL_SHARDED = ["conv_w_dw", "ssd_conv_w", "ssd_conv_b", "ssd_norm_w"]
N_CHIPS = 4


def _pack_flat(parts, pad_to):
    flat = [p.reshape(-1) for p in parts]
    offs, n = [], 0
    for f in flat:
        offs.append(n)
        n += f.shape[0]
    total = -(-n // pad_to) * pad_to
    if total > n:
        flat.append(jnp.zeros((total - n,), flat[0].dtype))
    return jnp.concatenate(flat).reshape(-1, LANES), offs


def kernel(x, mix_norm, ffn_norm, ffn_w_gu, ffn_w_down, conv_w_in, conv_w_dw, conv_w_out, fox_w_in, fox_b_f, fox_q_gain, fox_k_gain, fox_w_out, ssd_w_in, ssd_conv_w, ssd_conv_b, ssd_dt_bias, ssd_a_log, ssd_d, ssd_norm_w, ssd_w_out, loss_target, m_mix_norm, m_ffn_norm, m_ffn_w_gu, m_ffn_w_down, m_conv_w_in, m_conv_w_dw, m_conv_w_out, m_fox_w_in, m_fox_b_f, m_fox_q_gain, m_fox_k_gain, m_fox_w_out, m_ssd_w_in, m_ssd_conv_w, m_ssd_conv_b, m_ssd_dt_bias, m_ssd_a_log, m_ssd_d, m_ssd_norm_w, m_ssd_w_out, v_mix_norm, v_ffn_norm, v_ffn_w_gu, v_ffn_w_down, v_conv_w_in, v_conv_w_dw, v_conv_w_out, v_fox_w_in, v_fox_b_f, v_fox_q_gain, v_fox_k_gain, v_fox_w_out, v_ssd_w_in, v_ssd_conv_w, v_ssd_conv_b, v_ssd_dt_bias, v_ssd_a_log, v_ssd_d, v_ssd_norm_w, v_ssd_w_out):
    w = dict(zip(WEIGHTS, (mix_norm, ffn_norm, ffn_w_gu, ffn_w_down, conv_w_in, conv_w_dw, conv_w_out, fox_w_in, fox_b_f, fox_q_gain, fox_k_gain,
                           fox_w_out, ssd_w_in, ssd_conv_w, ssd_conv_b, ssd_dt_bias, ssd_a_log, ssd_d, ssd_norm_w, ssd_w_out)))
    m1 = dict(zip(WEIGHTS, (m_mix_norm, m_ffn_norm, m_ffn_w_gu, m_ffn_w_down, m_conv_w_in, m_conv_w_dw, m_conv_w_out, m_fox_w_in, m_fox_b_f,
                            m_fox_q_gain, m_fox_k_gain, m_fox_w_out, m_ssd_w_in, m_ssd_conv_w, m_ssd_conv_b, m_ssd_dt_bias, m_ssd_a_log, m_ssd_d,
                            m_ssd_norm_w, m_ssd_w_out)))
    m2 = dict(zip(WEIGHTS, (v_mix_norm, v_ffn_norm, v_ffn_w_gu, v_ffn_w_down, v_conv_w_in, v_conv_w_dw, v_conv_w_out, v_fox_w_in, v_fox_b_f,
                            v_fox_q_gain, v_fox_k_gain, v_fox_w_out, v_ssd_w_in, v_ssd_conv_w, v_ssd_conv_b, v_ssd_dt_bias, v_ssd_a_log, v_ssd_d,
                            v_ssd_norm_w, v_ssd_w_out)))
    cx, cy, cc = _place()
    chip = 2 * cx + cy

    place = jnp.stack([chip, cc]).astype(jnp.int32)
    depth = mix_norm.shape[0]
    attention = next((i for i in range(depth) if i % 3 == 1), depth)

    def carrier_of(layer):
        return None if layer == 0 else ("ffn", layer - 1) if layer <= attention else ("attention", attention)

    early, late, cm = {}, {}, {n: [] for n in BIG}
    for n in BIG:
        wb = w[n].astype(WIRE_DTYPE)
        keys = [carrier_of(_model_layer(n, l)) for l in range(wb.shape[0])]
        for key in dict.fromkeys(keys):
            first, count = keys.index(key), keys.count(key)
            if key is None:
                early[n] = wb[first:first + count]
            else:
                late.setdefault(key, {})[n] = (wb[first:first + count], first)
    gathered = _with_own(_gather_list(list(early.values()), name="gather_weights"), list(early.values()), chip)
    for n, g_ in zip(early, gathered):
        cm[n].append((g_, 0))
    sp, soffs = _pack_flat([w[n] for n in SMALL_SHARDED], SUBLANES * LANES)
    sgath = _gather_small(sp, name="gather_small").reshape(N_CHIPS, -1)
    full = {n: w[n] for n in WEIGHTS if n not in SHARD_AXIS}
    for n, off in zip(SMALL_SHARDED, soffs):
        full[n] = jnp.concatenate([sgath[j, off:off + w[n].size].reshape(w[n].shape) for j in range(N_CHIPS)], axis=SHARD_AXIS[n])

    loss, gx, grads, reds = _local_step(x[0], loss_target[0], full, cm, late, place)
    loss = lax.psum(loss, ("x", "y", "c"))

    small_names = [n for n in WEIGHTS if n not in BIG]
    sm, smoffs = _pack_flat([grads[n] for n in small_names], SUBLANES * LANES)
    sred = _allreduce_small(sm, name="allreduce_small").reshape(-1)

    g = {n: jnp.stack([reds[(n, l)] for l in range(w[n].shape[0])]).reshape(w[n].shape) for n in BIG}
    for n, off in zip(small_names, smoffs):
        fullg = sred[off:off + grads[n].size].reshape(grads[n].shape)
        if n in SHARD_AXIS:
            ax = SHARD_AXIS[n]
            fullg = lax.dynamic_slice_in_dim(fullg, chip * w[n].shape[ax], w[n].shape[ax], axis=ax)
        g[n] = fullg

    deltas, new_m, new_v = [], [], []
    for n in WEIGHTS:
        dl, mn, vn = _adamw(w[n], g[n], m1[n], m2[n], name=f"adamw_{n}")
        deltas.append(dl)
        new_m.append(mn)
        new_v.append(vn)
    return (loss, gx[None], *[g[n] for n in WEIGHTS], *deltas, *new_m, *new_v)
```

```python
import functools
import math

import jax
import jax.numpy as jnp
from jax import lax
from jax.experimental import pallas as pl
from jax.experimental.pallas import tpu as pltpu

F32 = jnp.float32
MM_DTYPE = jnp.bfloat16
ACT_DTYPE = jnp.bfloat16
WIRE_DTYPE = jnp.bfloat16

RMS_EPS = 1e-6
HEAD = 64
SSM_STATE = 128
SSM_CHUNK = 128
LANES = 128
SUBLANES = 8
VMEM_LIMIT = 48 * 1024 * 1024

ADAM_LR, ADAM_B1, ADAM_B2, ADAM_EPS, ADAM_WD, ADAM_STEP = 0.001, 0.9, 0.999, 1e-08, 0.01, 10

HI = lax.Precision.HIGHEST
MESH = pl.DeviceIdType.MESH


def _tile(dim, prefs):
    for p in prefs:
        if dim % p == 0:
            return p
    return dim


def _params(sem):
    return pltpu.CompilerParams(dimension_semantics=sem, vmem_limit_bytes=VMEM_LIMIT)


def _sigmoid(x):
    return 1.0 / (1.0 + jnp.exp(-x))


def _softplus(x):
    return jnp.maximum(x, 0.0) + jnp.log(1.0 + jnp.exp(-jnp.abs(x)))


TILES = (1024, 1408, 768, 512, 256, 128)
MIN_STEP_WORK = 1 << 30


def _mm(a, b, *, ta=False, tb=False, add=None, out_dtype=F32, name, b_layer=None, out_shard=None, inter=None, rider=None):
    ka, m = (a.shape[0], a.shape[1]) if ta else (a.shape[1], a.shape[0])
    if b_layer is None:
        kb, n = (b.shape[1], b.shape[0]) if tb else (b.shape[0], b.shape[1])
        ns = None
    else:
        ns = b.shape[3]
        kb, n = (b.shape[0] * ns, b.shape[2]) if tb else (b.shape[2], b.shape[0] * ns)
    assert ka == kb, (a.shape, b.shape, ta, tb)
    k = ka
    tm = _tile(m, (1408, 1024, 512, 256, 128) if ta else (512, 256, 128))
    tn = _tile(n, TILES)
    tk = _tile(k, TILES)
    if inter:
        segs, bw = inter
        if tb:
            tk = bw
        else:
            tn = bw
        tps = ((k if tb else n) // bw) // segs
        col = lambda q: ((q % segs) * tps + q // segs) * bw
    else:
        col = lambda q: q * (tk if tb else tn)
    def vmem(tm_, tk_):
        out_b = jnp.dtype(out_dtype).itemsize * 2 + (8 if add is not None else 0) + 4
        return 2 * tk_ * (tm_ * a.dtype.itemsize + tn * b.dtype.itemsize) + tm_ * tn * out_b

    if ta:
        while tk * 2 <= k and k % (tk * 2) == 0 and tm * tn * tk < MIN_STEP_WORK and vmem(tm, tk * 2) < VMEM_LIMIT * 3 // 4:
            tk *= 2
    else:
        while tm * 2 <= m and m % (tm * 2) == 0 and tm * tn * tk < MIN_STEP_WORK and vmem(tm * 2, tk) < VMEM_LIMIT * 3 // 4:
            tm *= 2
    nk = k // tk
    a_spec = pl.BlockSpec((tk, tm), lambda i, j, q: (q, i)) if ta else pl.BlockSpec((tm, tk), lambda i, j, q: (i, q))
    if b_layer is None:
        b_spec = pl.BlockSpec((tn, tk), lambda i, j, q: (j, q)) if tb else pl.BlockSpec((tk, tn), lambda i, j, q: (q, j))
    elif tb:
        b_spec = pl.BlockSpec((None, None, tn, tk), lambda i, j, q: (col(q) // ns, b_layer, j, (col(q) % ns) // tk))
    else:
        b_spec = pl.BlockSpec((None, None, tk, tn), lambda i, j, q: (col(j) // ns, b_layer, q, (col(j) % ns) // tn))
    if out_shard:
        assert ta and add is None
        o_spec = pl.BlockSpec((None, tm, tn), lambda i, j, q: (col(j) // out_shard, i, (col(j) % out_shard) // tn))
        o_shape = jax.ShapeDtypeStruct((N_CHIPS, m, out_shard), out_dtype)
    else:
        o_spec = pl.BlockSpec((tm, tn), lambda i, j, q: (i, j))
        o_shape = jax.ShapeDtypeStruct((m, n), out_dtype)
    dims = (((0 if ta else 1,), (1 if tb else 0,)), ((), ()))
    has_add = add is not None

    def body(*refs):
        a_ref, b_ref = refs[0], refs[1]
        o_ref = refs[2 + has_add]
        p = lax.dot_general(a_ref[...].astype(MM_DTYPE), b_ref[...].astype(MM_DTYPE), dims, preferred_element_type=F32)

        def finish(acc):
            if has_add:
                acc = acc + refs[2][...].astype(F32)
            o_ref[...] = acc.astype(out_dtype)

        if nk == 1:
            finish(p)
        else:
            acc_ref = refs[3 + has_add]
            q = pl.program_id(2)

            @pl.when(q == 0)
            def _():
                acc_ref[...] = p

            @pl.when(q > 0)
            def _():
                acc_ref[...] += p

            @pl.when(q == nk - 1)
            def _():
                finish(acc_ref[...])

    args = [a, b] + ([add] if has_add else [])
    in_specs = [a_spec, b_spec] + ([o_spec] if has_add else [])
    grid = (m // tm, n // tn, nk)
    r_in, r_out, r_shapes, r_scratch, r_args = _rider_parts(rider)
    outs = pl.pallas_call(
        _carry(body, rider, 0, len(args), 1, grid), name=name, grid=grid, in_specs=in_specs + r_in, out_specs=[o_spec] + r_out,
        out_shape=[o_shape] + r_shapes, scratch_shapes=([pltpu.VMEM((tm, tn), F32)] if nk > 1 else []) + r_scratch,
        compiler_params=_params(("arbitrary",) * 3 if rider else ("parallel", "parallel", "arbitrary")),
    )(*args, *r_args)
    return (outs[0], list(outs[1:])) if rider else outs[0]


def _rows(fn, *, name, s, tm, ncol=1, ins, outs, accs=()):
    nrt = s // tm
    hb = tm // SUBLANES
    in_specs, args = [], []
    for spec in ins:
        kind, arr = spec[0], spec[1]
        if kind == "full":
            in_specs.append(pl.BlockSpec(arr.shape, lambda j, i: (0, 0)))
        elif kind == "col":
            _, _, bw, cmap = spec
            in_specs.append(pl.BlockSpec((arr.shape[0], bw), lambda j, i, cmap=cmap: (0, cmap(j))))
        elif kind == "row":
            _, _, bw, cmap = spec
            in_specs.append(pl.BlockSpec((tm, bw), lambda j, i, cmap=cmap: (i, cmap(j))))
        elif kind == "prev":
            _, _, bw, cmap = spec
            in_specs.append(pl.BlockSpec((SUBLANES, bw), lambda j, i, cmap=cmap: (jnp.maximum(i * hb - 1, 0), cmap(j))))
        elif kind == "next":
            _, _, bw, cmap = spec
            in_specs.append(pl.BlockSpec((SUBLANES, bw), lambda j, i, cmap=cmap: (jnp.minimum((i + 1) * hb, s // SUBLANES - 1), cmap(j))))
        else:
            raise ValueError(kind)
        args.append(arr)
    out_specs, out_shape = [], []
    for w, bw, cmap, dt in outs:
        out_specs.append(pl.BlockSpec((tm, bw), lambda j, i, cmap=cmap: (i, cmap(j))))
        out_shape.append(jax.ShapeDtypeStruct((s, w), dt))
    for r, w, bw, cmap in accs:
        out_specs.append(pl.BlockSpec((r, bw), lambda j, i, cmap=cmap: (0, cmap(j))))
        out_shape.append(jax.ShapeDtypeStruct((r, w), F32))
    n_in, n_out, n_acc = len(ins), len(outs), len(accs)

    def body(*refs):
        i = pl.program_id(1)
        vals = [r[...] for r in refs[:n_in]]
        o_vals, a_vals = fn(i, nrt, *vals)
        assert len(o_vals) == n_out and len(a_vals) == n_acc
        for r, v in zip(refs[n_in:n_in + n_out], o_vals):
            r[...] = v.astype(r.dtype)
        for r, v in zip(refs[n_in + n_out:], a_vals):
            @pl.when(i == 0)
            def _(r=r, v=v):
                r[...] = v.astype(F32)

            @pl.when(i > 0)
            def _(r=r, v=v):
                r[...] += v.astype(F32)

    res = pl.pallas_call(
        body, name=name, grid=(ncol, nrt), in_specs=in_specs, out_specs=out_specs, out_shape=out_shape,
        compiler_params=_params(("parallel", "arbitrary" if accs else "parallel")),
    )(*args)
    return res


def _c0(j):
    return 0


def _cj(j):
    return j


def _gmean(v, gs):
    w = v.shape[-1]
    tile = max(gs, LANES)
    r = lax.broadcasted_iota(jnp.int32, (tile, tile), 0) // gs
    c = lax.broadcasted_iota(jnp.int32, (tile, tile), 1) // gs
    g = jnp.where(r == c, 1.0 / gs, 0.0).astype(F32)
    parts = [jnp.dot(v[:, t * tile:(t + 1) * tile], g, precision=HI, preferred_element_type=F32) for t in range(w // tile)]
    return parts[0] if len(parts) == 1 else jnp.concatenate(parts, axis=1)


def _sum_rows(v):
    return jnp.sum(v, axis=0, keepdims=True)


def _rms(x, w, *, name):
    s, d = x.shape

    def fn(i, nrt, xv, wv):
        r = lax.rsqrt(jnp.mean(xv * xv, axis=-1, keepdims=True) + RMS_EPS)
        return (xv * r * wv,), ()

    (h,) = _rows(fn, name=name, s=s, tm=_tile(s, (512, 256, 128)), ins=[("row", x, d, _c0), ("full", w.reshape(1, d))],
                 outs=[(d, d, _c0, ACT_DTYPE)])
    return h


def _rms_bwd(x, w, dh, dx_in, *, name):
    s, d = x.shape

    def fn(i, nrt, xv, wv, dhv, dxi):
        r = lax.rsqrt(jnp.mean(xv * xv, axis=-1, keepdims=True) + RMS_EPS)
        xh = xv * r
        g = dhv * wv
        dx = dxi + r * (g - xh * jnp.mean(g * xh, axis=-1, keepdims=True))
        return (dx, dx), (_sum_rows(dhv * xh),)

    dx, dxb, dw = _rows(fn, name=name, s=s, tm=_tile(s, (512, 256, 128)),
                        ins=[("row", x, d, _c0), ("full", w.reshape(1, d)), ("row", dh, d, _c0), ("row", dx_in, d, _c0)],
                        outs=[(d, d, _c0, F32), (d, d, _c0, MM_DTYPE)], accs=[(1, d, d, _c0)])
    return dx, dxb, dw.reshape(d)


def _ffn_up(h, w_gu, layer, bw, *, name, rider=None):
    m, k = h.shape
    ns = w_gu.shape[3]
    f = N_CHIPS * ns // 2
    tm = _tile(m, (512, 256, 128))

    def w_spec(first):
        return pl.BlockSpec((None, None, k, bw), lambda t, i: ((first + t * bw) // ns, layer, 0, ((first + t * bw) % ns) // bw))

    def body(h_ref, wg_ref, wu_ref, gu_ref, a_ref):
        hv = h_ref[...].astype(MM_DTYPE)
        g = jnp.dot(hv, wg_ref[...].astype(MM_DTYPE), preferred_element_type=F32)
        u = jnp.dot(hv, wu_ref[...].astype(MM_DTYPE), preferred_element_type=F32)
        gu_ref[...] = jnp.concatenate([g, u], axis=1).astype(gu_ref.dtype)
        a_ref[...] = (g * _sigmoid(g) * u).astype(a_ref.dtype)

    grid = (f // bw, m // tm)
    r_in, r_out, r_shapes, r_scratch, r_args = _rider_parts(rider)
    outs = pl.pallas_call(
        _carry(body, rider, 0, 3, 2, grid), name=name, grid=grid,
        in_specs=[pl.BlockSpec((tm, k), lambda t, i: (i, 0)), w_spec(0), w_spec(f)] + r_in,
        out_specs=[pl.BlockSpec((tm, 2 * bw), lambda t, i: (i, t)), pl.BlockSpec((tm, bw), lambda t, i: (i, t))] + r_out,
        out_shape=[jax.ShapeDtypeStruct((m, 2 * f), ACT_DTYPE), jax.ShapeDtypeStruct((m, f), ACT_DTYPE)] + r_shapes,
        scratch_shapes=r_scratch, compiler_params=_params(("arbitrary", "arbitrary") if rider else ("parallel", "parallel")),
    )(h, w_gu, w_gu, *r_args)
    return outs[0], outs[1], list(outs[2:])


def _ffn_back(dx, w_down, gu, bw, *, name):
    m, d = dx.shape
    f = w_down.shape[0]
    tm = _tile(m, (512, 256, 128))

    def body(dx_ref, w_ref, gu_ref, o_ref):
        da = _nt(dx_ref[...].astype(MM_DTYPE), w_ref[...].astype(MM_DTYPE))
        gv, uv = gu_ref[:, :bw].astype(F32), gu_ref[:, bw:].astype(F32)
        sg = _sigmoid(gv)
        o_ref[...] = jnp.concatenate([da * uv * sg * (1.0 + gv * (1.0 - sg)), da * gv * sg], axis=1).astype(o_ref.dtype)

    return pl.pallas_call(
        body, name=name, grid=(f // bw, m // tm),
        in_specs=[pl.BlockSpec((tm, d), lambda t, i: (i, 0)), pl.BlockSpec((bw, d), lambda t, i: (t, 0)), pl.BlockSpec((tm, 2 * bw), lambda t, i: (i, t))],
        out_specs=pl.BlockSpec((tm, 2 * bw), lambda t, i: (i, t)), out_shape=jax.ShapeDtypeStruct((m, 2 * f), ACT_DTYPE),
        compiler_params=_params(("parallel", "parallel")),
    )(dx, w_down, gu)


def _loss_head(x, tgt, *, name):
    s, d = x.shape

    def fn(i, nrt, xv, tv):
        diff = xv - tv
        part = 0.5 * jnp.sum(diff * diff) / d
        return (diff / d, diff / d), (jnp.full((1, LANES), part, F32),)

    dy, dyb, loss = _rows(fn, name=name, s=s, tm=_tile(s, (512, 256, 128)),
                          ins=[("row", x, d, _c0), ("row", tgt, d, _c0)],
                          outs=[(d, d, _c0, F32), (d, d, _c0, MM_DTYPE)], accs=[(1, LANES, LANES, _c0)])
    return loss[0, 0], dy, dyb


def _shift_down(ext, j, tm):
    src = pltpu.roll(ext, j, 0) if j else ext
    return src[SUBLANES:SUBLANES + tm]


def _shift_up(ext, j, tm):
    return ext[:tm] if j == 0 else pltpu.roll(ext, ext.shape[0] - j, 0)[:tm]


def _gconv_fwd(p, w, bw, *, name):
    s, d = p.shape[0], p.shape[1] // 3
    kw = w.shape[0]
    tm = _tile(s, (512, 256, 128))

    def fn(i, nrt, pv, pp, wv):
        pv, pp = pv.astype(F32), pp.astype(F32)
        cv = pv[:, bw:2 * bw] * pv[:, 2 * bw:]
        pcv = jnp.where(i == 0, 0.0, pp[:, bw:2 * bw] * pp[:, 2 * bw:])
        ext = jnp.concatenate([pcv, cv], axis=0)
        u = sum(wv[k:k + 1, :] * _shift_down(ext, kw - 1 - k, tm) for k in range(kw))
        return (pv[:, :bw] * u,), ()

    (o,) = _rows(fn, name=name, s=s, tm=tm, ncol=d // bw, ins=[("row", p, 3 * bw, _cj), ("prev", p, 3 * bw, _cj), ("col", w, bw, _cj)],
                 outs=[(d, bw, _cj, ACT_DTYPE)])
    return o


def _gconv_bwd(p, w, do, bw, *, name):
    s, d = do.shape
    kw = w.shape[0]
    tm = _tile(s, (512, 256, 128))

    def fn(i, nrt, pv, pp, pn, dov, ndo, wv):
        pv, pp, dov = pv.astype(F32), pp.astype(F32), dov.astype(F32)
        bv, cv_, vv = pv[:, :bw], pv[:, bw:2 * bw], pv[:, 2 * bw:]
        cv = cv_ * vv
        pcv = jnp.where(i == 0, 0.0, pp[:, bw:2 * bw] * pp[:, 2 * bw:])
        ext = jnp.concatenate([pcv, cv], axis=0)
        shifted = [_shift_down(ext, kw - 1 - k, tm) for k in range(kw)]
        u = sum(wv[k:k + 1, :] * shifted[k] for k in range(kw))
        db = dov * u
        du = dov * bv
        ndu = jnp.where(i == nrt - 1, 0.0, ndo.astype(F32) * pn[:, :bw].astype(F32))
        ext2 = jnp.concatenate([du, ndu], axis=0)
        dcv = sum(wv[k:k + 1, :] * _shift_up(ext2, kw - 1 - k, tm) for k in range(kw))
        dw = jnp.concatenate([_sum_rows(du * shifted[k]) for k in range(kw)], axis=0)
        return (jnp.concatenate([db, dcv * vv, dcv * cv_], axis=1),), (dw,)

    dp, dw = _rows(fn, name=name, s=s, tm=tm, ncol=d // bw,
                   ins=[("row", p, 3 * bw, _cj), ("prev", p, 3 * bw, _cj), ("next", p, 3 * bw, _cj), ("row", do, bw, _cj),
                        ("next", do, bw, _cj), ("col", w, bw, _cj)],
                   outs=[(3 * d, 3 * bw, _cj, ACT_DTYPE)], accs=[(kw, d, bw, _cj)])
    return dp, dw


def _sconv_fwd(x, w, bias, *, name):
    s, d = x.shape
    kw = w.shape[0]
    bw = _tile(d, (512, 256, 128))
    tm = _tile(s, (512, 256, 128))

    def fn(i, nrt, xv, px, wv, bsv):
        xv = xv.astype(F32)
        ext = jnp.concatenate([jnp.where(i == 0, 0.0, px.astype(F32)), xv], axis=0)
        pre = sum(wv[k:k + 1, :] * _shift_down(ext, kw - 1 - k, tm) for k in range(kw)) + bsv
        return (pre * _sigmoid(pre),), ()

    (o,) = _rows(fn, name=name, s=s, tm=tm, ncol=d // bw,
                 ins=[("row", x, bw, _cj), ("prev", x, bw, _cj), ("col", w, bw, _cj), ("col", bias.reshape(1, d), bw, _cj)],
                 outs=[(d, bw, _cj, ACT_DTYPE)])
    return o


def _sconv_bwd(x, w, bias, dact, *, name):
    s, d = x.shape
    kw = w.shape[0]
    bw = _tile(d, (512, 256, 128))
    tm = _tile(s, (512, 256, 128))

    def fn(i, nrt, xv, px, nx, dav, nda, wv, bsv):
        xv = xv.astype(F32)
        ext = jnp.concatenate([jnp.where(i == 0, 0.0, px.astype(F32)), xv, nx.astype(F32)], axis=0)
        rows_e = tm + SUBLANES
        pre_e = sum(wv[k:k + 1, :] * _shift_down(ext, kw - 1 - k, rows_e) for k in range(kw)) + bsv
        da_e = jnp.concatenate([dav.astype(F32), jnp.where(i == nrt - 1, 0.0, nda.astype(F32))], axis=0)
        sg = _sigmoid(pre_e)
        dpre_e = da_e * sg * (1.0 + pre_e * (1.0 - sg))
        dx = sum(wv[k:k + 1, :] * _shift_up(dpre_e, kw - 1 - k, tm) for k in range(kw))
        dpre = dpre_e[:tm]
        dw = jnp.concatenate([_sum_rows(dpre * _shift_down(ext, kw - 1 - k, tm)) for k in range(kw)], axis=0)
        return (dx,), (dw, _sum_rows(dpre))

    dx, dw, db = _rows(fn, name=name, s=s, tm=tm, ncol=d // bw,
                       ins=[("row", x, bw, _cj), ("prev", x, bw, _cj), ("next", x, bw, _cj), ("row", dact, bw, _cj),
                            ("next", dact, bw, _cj), ("col", w, bw, _cj), ("col", bias.reshape(1, d), bw, _cj)],
                       outs=[(d, bw, _cj, ACT_DTYPE)], accs=[(kw, d, bw, _cj), (1, d, bw, _cj)])
    return dx, dw, db.reshape(d)


def _tri(n, reverse):
    r = lax.broadcasted_iota(jnp.int32, (n, n), 0)
    c = lax.broadcasted_iota(jnp.int32, (n, n), 1)
    return jnp.where((c >= r) if reverse else (c <= r), 1.0, 0.0).astype(F32)


def _cumsum_rows(x, *, reverse, name):
    s, w = x.shape
    ch = _tile(s, (256, 128))
    n = s // ch

    def body(x_ref, o_ref, carry):
        i = pl.program_id(0)

        @pl.when(i == 0)
        def _():
            carry[...] = jnp.zeros_like(carry)

        out = jnp.dot(_tri(ch, reverse), x_ref[...], precision=HI, preferred_element_type=F32) + carry[...]
        o_ref[...] = out
        carry[...] = out[0:1, :] if reverse else out[ch - 1:ch, :]

    imap = (lambda i: (n - 1 - i, 0)) if reverse else (lambda i: (i, 0))
    return pl.pallas_call(
        body, name=name, grid=(n,), in_specs=[pl.BlockSpec((ch, w), imap)], out_specs=pl.BlockSpec((ch, w), imap),
        out_shape=jax.ShapeDtypeStruct((s, w), F32), scratch_shapes=[pltpu.VMEM((1, w), F32)],
        compiler_params=_params(("arbitrary",)),
    )(x)


def _fox_prep(q, k, f, gq, gk, bf, *, name):
    s, d = q.shape
    scale = HEAD ** -0.5

    def fn(i, nrt, qv, kv, fv, gqv, gkv, bfv):
        qv, kv = qv.astype(F32), kv.astype(F32)
        qn = qv * lax.rsqrt(_gmean(qv * qv, HEAD) + RMS_EPS) * gqv * scale
        kn = kv * lax.rsqrt(_gmean(kv * kv, HEAD) + RMS_EPS) * gkv
        z = fv + bfv
        logf = jnp.minimum(z, 0.0) - jnp.log(1.0 + jnp.exp(-jnp.abs(z)))
        return (qn, kn, logf), ()

    return _rows(fn, name=name, s=s, tm=_tile(s, (512, 256, 128)),
                 ins=[("row", q, d, _c0), ("row", k, d, _c0), ("row", f, LANES, _c0), ("full", gq), ("full", gk), ("full", bf)],
                 outs=[(d, d, _c0, ACT_DTYPE), (d, d, _c0, ACT_DTYPE), (LANES, LANES, _c0, F32)])


def _fox_prep_bwd(q, k, f, gq, gk, bf, dqs, dkn, dlogf, *, name):
    s, d = q.shape
    scale = HEAD ** -0.5

    def fn(i, nrt, qv, kv, fv, gqv, gkv, bfv, dqv, dkv, dlf):
        outs, accs = [], []
        for xv, gv, dv, sc in ((qv, gqv, dqv, scale), (kv, gkv, dkv, 1.0)):
            xv, dv = xv.astype(F32), dv.astype(F32) * sc
            r = lax.rsqrt(_gmean(xv * xv, HEAD) + RMS_EPS)
            xh = xv * r
            g = dv * gv
            outs.append(r * (g - xh * _gmean(g * xh, HEAD)))
            accs.append(_sum_rows(dv * xh))
        z = fv + bfv
        df = dlf * _sigmoid(-z)
        outs.append(df)
        accs.append(_sum_rows(df))
        return outs, accs

    return _rows(fn, name=name, s=s, tm=_tile(s, (512, 256, 128)),
                 ins=[("row", q, d, _c0), ("row", k, d, _c0), ("row", f, LANES, _c0), ("full", gq), ("full", gk), ("full", bf),
                      ("row", dqs, d, _c0), ("row", dkn, d, _c0), ("row", dlogf, LANES, _c0)],
                 outs=[(d, d, _c0, ACT_DTYPE), (d, d, _c0, ACT_DTYPE), (LANES, LANES, _c0, ACT_DTYPE)],
                 accs=[(1, d, d, _c0), (1, d, d, _c0), (1, LANES, LANES, _c0)])


def _head_masks(shape):
    lane = lax.broadcasted_iota(jnp.int32, shape, len(shape) - 1)
    return lane < HEAD, lane >= HEAD


def _pick_lane(blk, idx):
    lane = lax.broadcasted_iota(jnp.int32, blk.shape, 1)
    return jnp.sum(jnp.where(lane == idx, blk, 0.0), axis=1, keepdims=True)


def _pick_row(blk, idx):
    sub = lax.broadcasted_iota(jnp.int32, blk.shape, 0)
    return jnp.sum(jnp.where(sub == idx, blk, 0.0), axis=0, keepdims=True)


def _fox_aug(qs, kn, cum, *, name):
    s, d = qs.shape
    hp = d // LANES

    def fn(i, nrt, qv, kv, cv):
        lane = lax.broadcasted_iota(jnp.int32, (qv.shape[0], LANES), 1)
        outs = [[], [], [], []]
        for p in range(hp):
            qt, kt = qv[:, p * LANES:(p + 1) * LANES], kv[:, p * LANES:(p + 1) * LANES]
            for h in range(2):
                mine = (lane < HEAD) if h == 0 else (lane >= HEAD)
                a0 = HEAD if h == 0 else 0
                c = cv[:, 2 * p + h:2 * p + h + 1]
                hi = c.astype(ACT_DTYPE).astype(F32)
                mid = (c - hi).astype(ACT_DTYPE).astype(F32)
                lo = (c - hi - mid).astype(ACT_DTYPE).astype(F32)
                ones = jnp.where((lane >= a0) & (lane < a0 + 3), 1.0, 0.0)
                kx = jnp.where(lane == a0, -hi, jnp.where(lane == a0 + 1, -mid, jnp.where(lane == a0 + 2, -lo, 0.0)))
                outs[h].append(jnp.where(mine, qt.astype(F32), ones))
                outs[2 + h].append(jnp.where(mine, kt.astype(F32), kx))
        return [jnp.concatenate(o, axis=1) for o in outs], ()

    return _rows(fn, name=name, s=s, tm=_tile(s, (512, 256, 128)), ins=[("row", qs, d, _c0), ("row", kn, d, _c0), ("row", cum, LANES, _c0)],
                 outs=[(d, d, _c0, ACT_DTYPE)] * 4)


def _tri_tables(nq, by_key):
    import numpy as np
    pairs = [(qi, kj) for kj in range(nq) for qi in range(kj, nq)] if by_key else [(qi, kj) for qi in range(nq) for kj in range(qi + 1)]
    return jnp.asarray(np.array([p[0] for p in pairs], np.int32)), jnp.asarray(np.array([p[1] for p in pairs], np.int32))


def _nt(a, b):
    return lax.dot_general(a, b, (((1,), (1,)), ((), ())), preferred_element_type=F32)


def _tn(a, b):
    return lax.dot_general(a, b, (((0,), (0,)), ((), ())), preferred_element_type=F32)


ATTN_BLOCKS = (1024, 512, 256, 128)


def _fox_dd(do, o, *, name):
    s, d = do.shape

    def fn(i, nrt, dov, ov):
        return (_reduce_heads(dov.astype(F32) * ov.astype(F32), d // HEAD, HEAD),), ()

    (dd,) = _rows(fn, name=name, s=s, tm=_tile(s, (512, 256, 128)), ins=[("row", do, d, _c0), ("row", o, d, _c0)],
                  outs=[(LANES, LANES, _c0, F32)])
    return dd


def _pair_rows(a, nh):
    s = a.shape[0]
    t = a[:, :nh].T.reshape(nh // 2, 2, s)
    return jnp.pad(t, ((0, 0), (0, SUBLANES - 2), (0, 0)))


def _rows01(r0, r1):
    sub = lax.broadcasted_iota(jnp.int32, (SUBLANES, r0.shape[1]), 0)
    return jnp.where(sub == 0, r0, jnp.where(sub == 1, r1, 0.0))


def _rider_parts(rider):
    if rider is None:
        return [], [], [], [], []
    return [ANY] * len(rider.arrays), [ANY] * len(rider.out_shapes), rider.out_shapes, rider.scratch, rider.arrays


def _fox_fwd_t(q_aug, k_aug, v, *, name, rider=None):
    s, d = v.shape
    bq = _tile(s, ATTN_BLOCKS)
    nq = s // bq
    hp = d // LANES
    qtab, ktab = _tri_tables(nq, by_key=False)

    def body(qt, kt, q0_ref, q1_ref, k0_ref, k1_ref, v_ref, o_ref, lse_ref, m0, m1, l0, l1, acc0, acc1):
        t = pl.program_id(1)
        qi, kj = qt[t], kt[t]
        ms, ls, accs = (m0, m1), (l0, l1), (acc0, acc1)

        @pl.when(kj == 0)
        def _():
            for h in range(2):
                ms[h][...] = jnp.full_like(ms[h], -jnp.inf)
                ls[h][...] = jnp.zeros_like(ls[h])
                accs[h][...] = jnp.zeros_like(accs[h])

        def update(diagonal):
            v2 = v_ref[...]
            qk = ((q0_ref, k0_ref), (q1_ref, k1_ref))
            sts = [_nt(qk[h][1][...], qk[h][0][...]) for h in range(2)]
            if diagonal:
                sts = [_diag_mask_t(st) for st in sts]
            m_prev = [ms[h][...] for h in range(2)]
            m_new = [jnp.maximum(m_prev[h], jnp.max(sts[h], axis=0, keepdims=True)) for h in range(2)]
            ps = [jnp.exp(sts[h] - m_new[h]) for h in range(2)]
            alpha = [jnp.exp(m_prev[h] - m_new[h]) for h in range(2)]
            for h in range(2):
                ls[h][...] = alpha[h] * ls[h][...] + jnp.sum(ps[h], axis=0, keepdims=True)
                accs[h][...] = alpha[h] * accs[h][...] + _tn(v2, ps[h].astype(MM_DTYPE))
                ms[h][...] = m_new[h]

        @pl.when(kj < qi)
        def _():
            update(False)

        @pl.when(kj == qi)
        def _():
            update(True)
            row = lax.broadcasted_iota(jnp.int32, (LANES, bq), 0)
            ot = jnp.where(row < HEAD, acc0[...] / l0[...], acc1[...] / l1[...])
            o_ref[...] = ot.T.astype(o_ref.dtype)
            lse_ref[...] = _rows01(m0[...] + jnp.log(l0[...]), m1[...] + jnp.log(l1[...]))

    blk = (bq, LANES)
    qmap = lambda p_, t, qt, kt: (qt[t], p_)
    kmap = lambda p_, t, qt, kt: (kt[t], p_)
    grid = (hp, qtab.shape[0])
    r_in, r_out, r_shapes, r_scratch, r_args = _rider_parts(rider)
    grid_spec = pltpu.PrefetchScalarGridSpec(
        num_scalar_prefetch=2, grid=grid,
        in_specs=[pl.BlockSpec(blk, qmap), pl.BlockSpec(blk, qmap), pl.BlockSpec(blk, kmap), pl.BlockSpec(blk, kmap), pl.BlockSpec(blk, kmap)] + r_in,
        out_specs=[pl.BlockSpec(blk, qmap), pl.BlockSpec((None, SUBLANES, bq), lambda p_, t, qt, kt: (p_, 0, qt[t]))] + r_out,
        scratch_shapes=[pltpu.VMEM((1, bq), F32)] * 4 + [pltpu.VMEM((LANES, bq), F32)] * 2 + r_scratch)
    outs = pl.pallas_call(
        _carry(body, rider, 2, 5, 2, grid), name=name, grid_spec=grid_spec,
        out_shape=[jax.ShapeDtypeStruct((s, d), ACT_DTYPE), jax.ShapeDtypeStruct((hp, SUBLANES, s), F32)] + r_shapes,
        compiler_params=_params(("arbitrary", "arbitrary") if rider else ("parallel", "arbitrary")),
    )(qtab, ktab, q_aug[0], q_aug[1], k_aug[0], k_aug[1], v, *r_args)
    return outs[0], outs[1], list(outs[2:])


def _diag_mask_t(st):
    key = lax.broadcasted_iota(jnp.int32, st.shape, 0)
    qry = lax.broadcasted_iota(jnp.int32, st.shape, 1)
    return jnp.where(qry >= key, st, -jnp.inf)


def _fox_bwd_t(q_aug, k_aug, v, lse, dd, do, *, name, rider=None):
    s, d = v.shape
    bq = _tile(s, ATTN_BLOCKS)
    nq = s // bq
    hp = d // LANES
    blk = (bq, LANES)
    qtab, ktab = _tri_tables(nq, by_key=True)
    n_steps = qtab.shape[0]

    def body(qt, kt, q0_ref, q1_ref, k0_ref, k1_ref, v_ref, lse_ref, dd_ref, do_ref,
             dq_ref, dk_ref, dv_ref, dcol_ref, drow_ref, dq_sc, rs_sc, dk_sc, dv_sc, cs_sc):
        t = pl.program_id(1)
        qi, kj = qt[t], kt[t]

        @pl.when(t == 0)
        def _():
            dq_sc[...] = jnp.zeros_like(dq_sc)
            rs_sc[...] = jnp.zeros_like(rs_sc)

        def update(diagonal):
            v2, do2 = v_ref[...], do_ref[...]
            masks = _head_masks(blk)
            row = lax.broadcasted_iota(jnp.int32, (LANES, bq), 0)
            off = pl.multiple_of(qi * bq, bq)
            for h, (q_ref, k_ref) in enumerate(((q0_ref, k0_ref), (q1_ref, k1_ref))):
                st = _nt(k_ref[...], q_ref[...])
                if diagonal:
                    st = _diag_mask_t(st)
                p = jnp.exp(st - lse_ref[h:h + 1, :])
                dp = _nt(v2, jnp.where(masks[h], do2, jnp.zeros_like(do2)))
                ds = p * (dp - dd_ref[h:h + 1, :])
                dsb = ds.astype(MM_DTYPE)
                dv_sc[h] += jnp.dot(p.astype(MM_DTYPE), do2, preferred_element_type=F32)
                dk_sc[h] += jnp.dot(dsb, q_ref[...], preferred_element_type=F32)
                cs_sc[h] += jnp.sum(ds, axis=1, keepdims=True)
                mine = (row < HEAD) if h == 0 else (row >= HEAD)
                dq_sc[:, pl.ds(off, bq)] += jnp.where(mine, _tn(k_ref[...], dsb), 0.0)
                rs_sc[h:h + 1, pl.ds(off, bq)] += jnp.sum(ds, axis=0, keepdims=True)

        @pl.when(qi == kj)
        def _():
            dk_sc[...] = jnp.zeros_like(dk_sc)
            dv_sc[...] = jnp.zeros_like(dv_sc)
            cs_sc[...] = jnp.zeros_like(cs_sc)
            update(True)

        @pl.when(qi > kj)
        def _():
            update(False)

        @pl.when(qi == nq - 1)
        def _():
            lo, _hi = _head_masks(blk)
            dk_ref[...] = jnp.where(lo, dk_sc[0], dk_sc[1]).astype(dk_ref.dtype)
            dv_ref[...] = jnp.where(lo, dv_sc[0], dv_sc[1]).astype(dv_ref.dtype)
            dcol_ref[...] = jnp.where(lo, cs_sc[0], cs_sc[1])

        @pl.when(t == n_steps - 1)
        def _():
            for c in range(nq):
                dq_ref[c * bq:(c + 1) * bq, :] = dq_sc[:, c * bq:(c + 1) * bq].T.astype(dq_ref.dtype)
            drow_ref[...] = rs_sc[...]

    qmap = lambda p_, t, qt, kt: (qt[t], p_)
    kmap = lambda p_, t, qt, kt: (kt[t], p_)
    rmap = lambda p_, t, qt, kt: (p_, 0, qt[t])
    grid = (hp, n_steps)
    r_in, r_out, r_shapes, r_scratch, r_args = _rider_parts(rider)
    outs = pl.pallas_call(
        _carry(body, rider, 2, 8, 5, grid), name=name,
        grid_spec=pltpu.PrefetchScalarGridSpec(
            num_scalar_prefetch=2, grid=grid,
            in_specs=[pl.BlockSpec(blk, qmap), pl.BlockSpec(blk, qmap), pl.BlockSpec(blk, kmap), pl.BlockSpec(blk, kmap), pl.BlockSpec(blk, kmap),
                      pl.BlockSpec((None, SUBLANES, bq), rmap), pl.BlockSpec((None, SUBLANES, bq), rmap), pl.BlockSpec(blk, qmap)] + r_in,
            out_specs=[pl.BlockSpec((s, LANES), lambda p_, t, qt, kt: (0, p_)), pl.BlockSpec(blk, kmap), pl.BlockSpec(blk, kmap),
                       pl.BlockSpec(blk, kmap), pl.BlockSpec((None, SUBLANES, s), lambda p_, t, qt, kt: (p_, 0, 0))] + r_out,
            scratch_shapes=[pltpu.VMEM((LANES, s), F32), pltpu.VMEM((SUBLANES, s), F32), pltpu.VMEM((2, bq, LANES), F32),
                            pltpu.VMEM((2, bq, LANES), F32), pltpu.VMEM((2, bq, 1), F32)] + r_scratch),
        out_shape=[jax.ShapeDtypeStruct((s, d), ACT_DTYPE), jax.ShapeDtypeStruct((s, d), ACT_DTYPE), jax.ShapeDtypeStruct((s, d), ACT_DTYPE),
                   jax.ShapeDtypeStruct((s, d), F32), jax.ShapeDtypeStruct((hp, SUBLANES, s), F32)] + r_shapes,
        compiler_params=_params(("arbitrary", "arbitrary") if rider else ("parallel", "arbitrary")),
    )(qtab, ktab, q_aug[0], q_aug[1], k_aug[0], k_aug[1], v, lse, dd, do, *r_args)
    return list(outs[:5]), list(outs[5:])


def _expand_heads(v, nh, hd):
    r = lax.broadcasted_iota(jnp.int32, (LANES, nh * hd), 0)
    c = lax.broadcasted_iota(jnp.int32, (LANES, nh * hd), 1) // hd
    e = jnp.where(r == c, 1.0, 0.0).astype(F32)
    return jnp.dot(v, e, precision=HI, preferred_element_type=F32)


def _reduce_heads(v, nh, hd):
    r = lax.broadcasted_iota(jnp.int32, (nh * hd, LANES), 0) // hd
    c = lax.broadcasted_iota(jnp.int32, (nh * hd, LANES), 1)
    e = jnp.where(r == c, 1.0, 0.0).astype(F32)
    return jnp.dot(v, e, precision=HI, preferred_element_type=F32)


def _ssd_prep(dt_raw, dt_bias, a_log, nh, *, name):
    s = dt_raw.shape[0]

    def fn(i, nrt, dtr, bsv, alv):
        dt = _softplus(dtr + bsv)
        acum = jnp.dot(_tri(SSM_CHUNK, False), dt * (-jnp.exp(alv)), precision=HI, preferred_element_type=F32)
        return (dt, acum, _expand_heads(dt, nh, HEAD), _expand_heads(acum, nh, HEAD)), ()

    w = nh * HEAD
    return _rows(fn, name=name, s=s, tm=SSM_CHUNK, ins=[("row", dt_raw, LANES, _c0), ("full", dt_bias), ("full", a_log)],
                 outs=[(LANES, LANES, _c0, F32), (LANES, LANES, _c0, F32), (w, w, _c0, F32), (w, w, _c0, F32)])


def _ssd_prep_bwd(dt_raw, dt_bias, a_log, ddtx, dacx, nh, *, name):
    s = dt_raw.shape[0]

    def fn(i, nrt, dtr, bsv, alv, ddx, dax):
        z = dtr + bsv
        dt = _softplus(z)
        a = -jnp.exp(alv)
        dda = jnp.dot(_tri(SSM_CHUNK, True), _reduce_heads(dax, nh, HEAD), precision=HI, preferred_element_type=F32)
        ddt = _reduce_heads(ddx, nh, HEAD) + dda * a
        dz = ddt * _sigmoid(z)
        lane = lax.broadcasted_iota(jnp.int32, dz.shape, 1)
        dz = jnp.where(lane < nh, dz, 0.0)
        return (dz,), (_sum_rows(dz), _sum_rows(dda * dt) * a)

    w = nh * HEAD
    return _rows(fn, name=name, s=s, tm=SSM_CHUNK,
                 ins=[("row", dt_raw, LANES, _c0), ("full", dt_bias), ("full", a_log), ("row", ddtx, w, _c0), ("row", dacx, w, _c0)],
                 outs=[(LANES, LANES, _c0, ACT_DTYPE)], accs=[(1, LANES, LANES, _c0), (1, LANES, LANES, _c0)])


def _ssd_decay(ac_blk, act_blk, head):
    col = _pick_lane(ac_blk, head)
    row = _pick_row(act_blk, head)
    r = lax.broadcasted_iota(jnp.int32, (SSM_CHUNK, SSM_CHUNK), 0)
    c = lax.broadcasted_iota(jnp.int32, (SSM_CHUNK, SSM_CHUNK), 1)
    return jnp.exp(jnp.where(r >= c, col - row, -jnp.inf))


def _group_masks(shape, hpg):
    lane = lax.broadcasted_iota(jnp.int32, shape, len(shape) - 1) // HEAD
    return [lane == k for k in range(hpg)]


def _ssd_scan_fwd(xs, bm, cm, dtx, acx, acum, acum_t, d_x, *, name):
    s, di = xs.shape
    ng = bm.shape[1] // SSM_STATE
    gw = di // ng
    hpg = gw // HEAD
    nc = s // SSM_CHUNK
    L = SSM_CHUNK
    nh_pad = acum_t.shape[0]

    gp = 2 if ng % 2 == 0 else 1
    N = SSM_STATE

    def body(x_ref, b_ref, c_ref, dt_ref, ax_ref, ac_ref, act_ref, d_ref, y_ref, st_ref, state):
        g2, c = pl.program_id(0), pl.program_id(1)

        @pl.when(c == 0)
        def _():
            state[...] = jnp.zeros_like(state)

        masks = _group_masks((L, gw), hpg)
        for gi in range(gp):
            g = g2 * gp + gi
            lanes, st_lanes = slice(gi * gw, (gi + 1) * gw), slice(gi * N, (gi + 1) * N)
            x4, bv, cv = x_ref[:, lanes].astype(F32), b_ref[:, st_lanes], c_ref[:, st_lanes]
            ax = ax_ref[:, lanes]
            tx = x4 * dt_ref[:, lanes]
            cb = lax.dot_general(cv, bv, (((1,), (1,)), ((), ())), preferred_element_type=F32)
            y = jnp.zeros((L, gw), F32)
            txb = tx.astype(MM_DTYPE)
            for k in range(hpg):
                wk = (cb * _ssd_decay(ac_ref[...], act_ref[...], g * hpg + k)).astype(MM_DTYPE)
                y = y + jnp.where(masks[k], jnp.dot(wk, txb, preferred_element_type=F32), 0.0)
            prev = state[gi]
            st_ref[gi] = prev
            y = y + jnp.dot(cv, prev.astype(MM_DTYPE), preferred_element_type=F32) * jnp.exp(ax)
            y = y + d_ref[:, lanes] * x4
            y_ref[:, lanes] = y.astype(y_ref.dtype)
            a_last = ax[L - 1:L, :]
            sx = (tx * jnp.exp(a_last - ax)).astype(MM_DTYPE)
            state[gi] = prev * jnp.exp(a_last) + lax.dot_general(bv, sx, (((0,), (0,)), ((), ())), preferred_element_type=F32)

    y, states = pl.pallas_call(
        body, name=name, grid=(ng // gp, nc),
        in_specs=[pl.BlockSpec((L, gp * gw), lambda g, c: (c, g)), pl.BlockSpec((L, gp * N), lambda g, c: (c, g)),
                  pl.BlockSpec((L, gp * N), lambda g, c: (c, g)), pl.BlockSpec((L, gp * gw), lambda g, c: (c, g)),
                  pl.BlockSpec((L, gp * gw), lambda g, c: (c, g)), pl.BlockSpec((L, LANES), lambda g, c: (c, 0)),
                  pl.BlockSpec((nh_pad, L), lambda g, c: (0, c)), pl.BlockSpec((1, gp * gw), lambda g, c: (0, g))],
        out_specs=[pl.BlockSpec((L, gp * gw), lambda g, c: (c, g)), pl.BlockSpec((gp, None, N, gw), lambda g, c: (g, c, 0, 0))],
        out_shape=[jax.ShapeDtypeStruct((s, di), ACT_DTYPE), jax.ShapeDtypeStruct((ng, nc, N, gw), F32)],
        scratch_shapes=[pltpu.VMEM((gp, N, gw), F32)],
        compiler_params=_params(("parallel", "arbitrary")),
    )(xs, bm, cm, dtx, acx, acum, acum_t, d_x)
    return y, states


def _ssd_scan_bwd(xs, bm, cm, dtx, acx, acum, acum_t, d_x, states, dy, *, name, rider=None):
    s, di = xs.shape
    ng = bm.shape[1] // SSM_STATE
    gw = di // ng
    hpg = gw // HEAD
    nc = s // SSM_CHUNK
    L = SSM_CHUNK
    nh_pad = acum_t.shape[0]

    gp = 2 if ng % 2 == 0 else 1
    N = SSM_STATE

    def body(x_ref, b_ref, c_ref, dt_ref, ax_ref, ac_ref, act_ref, d_ref, st_ref, dy_ref,
             dx_ref, db_ref, dc_ref, ddt_ref, dax_ref, dd_ref, dstate):
        g2, cc = pl.program_id(0), pl.program_id(1)

        @pl.when(cc == 0)
        def _():
            dstate[...] = jnp.zeros_like(dstate)
            dd_ref[...] = jnp.zeros_like(dd_ref)

        for gi in range(gp):
            one_group(g2 * gp + gi, gi, slice(gi * gw, (gi + 1) * gw), slice(gi * N, (gi + 1) * N), x_ref, b_ref, c_ref, dt_ref, ax_ref, ac_ref,
                      act_ref, d_ref, st_ref, dy_ref, dx_ref, db_ref, dc_ref, ddt_ref, dax_ref, dd_ref, dstate)

    def one_group(g, gi, lanes, st_lanes, x_ref, b_ref, c_ref, dt_ref, ax_ref, ac_ref, act_ref, d_ref, st_ref, dy_ref,
                  dx_ref, db_ref, dc_ref, ddt_ref, dax_ref, dd_ref, dstate):
        x4, bv, cv = x_ref[:, lanes].astype(F32), b_ref[:, st_lanes], c_ref[:, st_lanes]
        tv, ax, dyv = dt_ref[:, lanes], ax_ref[:, lanes], dy_ref[:, lanes].astype(F32)
        prev, dn = st_ref[gi], dstate[gi]
        dnb = dn.astype(MM_DTYPE)
        masks = _group_masks((L, gw), hpg)
        tx = x4 * tv
        txb = tx.astype(MM_DTYPE)
        e_ax = jnp.exp(ax)
        a_last = ax[L - 1:L, :]
        e_last = jnp.exp(a_last)
        ed = jnp.exp(a_last - ax)

        dx = d_ref[:, lanes] * dyv
        dd_ref[:, lanes] += _sum_rows(dyv * x4)
        dye = (dyv * e_ax).astype(MM_DTYPE)
        yo = jnp.dot(cv, prev.astype(MM_DTYPE), preferred_element_type=F32) * e_ax
        dc = lax.dot_general(dye, prev.astype(MM_DTYPE), (((1,), (1,)), ((), ())), preferred_element_type=F32)
        dprev = lax.dot_general(cv, dye, (((0,), (0,)), ((), ())), preferred_element_type=F32)
        dax = dyv * yo
        sx = tx * ed
        dsx = jnp.dot(bv, dnb, preferred_element_type=F32)
        db = lax.dot_general(sx.astype(MM_DTYPE), dnb, (((1,), (1,)), ((), ())), preferred_element_type=F32)
        dtx_ = dsx * ed
        dsx_sx = dsx * sx
        dax = dax - dsx_sx
        dlast = _sum_rows(dsx_sx) + _sum_rows(dn * prev) * e_last
        dprev = dprev + dn * e_last
        cb = lax.dot_general(cv, bv, (((1,), (1,)), ((), ())), preferred_element_type=F32)
        dcb = jnp.zeros((L, L), F32)
        lane = lax.broadcasted_iota(jnp.int32, (L, gw), 1)
        for k in range(hpg):
            dec = _ssd_decay(ac_ref[...], act_ref[...], g * hpg + k)
            wk = (cb * dec).astype(MM_DTYPE)
            dyk = jnp.where(masks[k], dyv, 0.0).astype(MM_DTYPE)
            dtx_ = dtx_ + jnp.where(masks[k], lax.dot_general(wk, dyk, (((0,), (0,)), ((), ())), preferred_element_type=F32), 0.0)
            dwk = lax.dot_general(dyk, txb, (((1,), (1,)), ((), ())), preferred_element_type=F32)
            dcb = dcb + dwk * dec
            mk = dwk * cb * dec
            da_k = jnp.sum(mk, axis=1, keepdims=True) - jnp.sum(mk.T, axis=1, keepdims=True)
            dax = dax + jnp.where(lane == k * HEAD, da_k, 0.0)
        dcbb = dcb.astype(MM_DTYPE)
        dc = dc + jnp.dot(dcbb, bv, preferred_element_type=F32)
        db = db + lax.dot_general(dcbb, cv, (((0,), (0,)), ((), ())), preferred_element_type=F32)
        sub = lax.broadcasted_iota(jnp.int32, (L, gw), 0)
        dax = dax + jnp.where(sub == L - 1, dlast, 0.0)
        dx_ref[:, lanes] = (dx + dtx_ * tv).astype(dx_ref.dtype)
        ddt_ref[:, lanes] = dtx_ * x4
        dax_ref[:, lanes] = dax
        db_ref[:, st_lanes] = db.astype(db_ref.dtype)
        dc_ref[:, st_lanes] = dc.astype(dc_ref.dtype)
        dstate[gi] = dprev

    rev = lambda g, c: (nc - 1 - c, g)
    rev0 = lambda g, c: (nc - 1 - c, 0)
    grid = (ng // gp, nc)
    r_in, r_out, r_shapes, r_scratch, r_args = _rider_parts(rider)
    outs = pl.pallas_call(
        _carry(body, rider, 0, 10, 6, grid), name=name, grid=grid,
        in_specs=[pl.BlockSpec((L, gp * gw), rev), pl.BlockSpec((L, gp * N), rev), pl.BlockSpec((L, gp * N), rev),
                  pl.BlockSpec((L, gp * gw), rev), pl.BlockSpec((L, gp * gw), rev), pl.BlockSpec((L, LANES), rev0),
                  pl.BlockSpec((nh_pad, L), lambda g, c: (0, nc - 1 - c)), pl.BlockSpec((1, gp * gw), lambda g, c: (0, g)),
                  pl.BlockSpec((gp, None, N, gw), lambda g, c: (g, nc - 1 - c, 0, 0)), pl.BlockSpec((L, gp * gw), rev)] + r_in,
        out_specs=[pl.BlockSpec((L, gp * gw), rev), pl.BlockSpec((L, gp * N), rev), pl.BlockSpec((L, gp * N), rev),
                   pl.BlockSpec((L, gp * gw), rev), pl.BlockSpec((L, gp * gw), rev), pl.BlockSpec((1, gp * gw), lambda g, c: (0, g))] + r_out,
        out_shape=[jax.ShapeDtypeStruct((s, di), ACT_DTYPE), jax.ShapeDtypeStruct(bm.shape, ACT_DTYPE), jax.ShapeDtypeStruct(cm.shape, ACT_DTYPE),
                   jax.ShapeDtypeStruct((s, di), F32), jax.ShapeDtypeStruct((s, di), F32), jax.ShapeDtypeStruct((1, di), F32)] + r_shapes,
        scratch_shapes=[pltpu.VMEM((gp, N, gw), F32)] + r_scratch,
        compiler_params=_params(("arbitrary", "arbitrary") if rider else ("parallel", "arbitrary")),
    )(xs, bm, cm, dtx, acx, acum, acum_t, d_x, states, dy, *r_args)
    return list(outs[:6]), list(outs[6:])


def _ssd_gate(y, z, w, gs, *, name):
    s, d = y.shape

    def fn(i, nrt, yv, zv, wv):
        zv = zv.astype(F32)
        u = yv.astype(F32) * zv * _sigmoid(zv)
        return (u * lax.rsqrt(_gmean(u * u, gs) + RMS_EPS) * wv,), ()

    (o,) = _rows(fn, name=name, s=s, tm=_tile(s, (256, 128)), ins=[("row", y, d, _c0), ("row", z, d, _c0), ("full", w.reshape(1, d))],
                 outs=[(d, d, _c0, ACT_DTYPE)])
    return o


def _ssd_gate_bwd(y, z, w, do, gs, *, name):
    s, d = y.shape

    def fn(i, nrt, yv, zv, wv, dov):
        yv, zv, dov = yv.astype(F32), zv.astype(F32), dov.astype(F32)
        sg = _sigmoid(zv)
        sl = zv * sg
        u = yv * sl
        r = lax.rsqrt(_gmean(u * u, gs) + RMS_EPS)
        uh = u * r
        g = dov * wv
        du = r * (g - uh * _gmean(g * uh, gs))
        return (du * sl, du * yv * sg * (1.0 + zv * (1.0 - sg))), (_sum_rows(dov * uh),)

    dy, dz, dw = _rows(fn, name=name, s=s, tm=_tile(s, (256, 128)),
                       ins=[("row", y, d, _c0), ("row", z, d, _c0), ("full", w.reshape(1, d)), ("row", do, d, _c0)],
                       outs=[(d, d, _c0, ACT_DTYPE), (d, d, _c0, ACT_DTYPE)], accs=[(1, d, d, _c0)])
    return dy, dz, dw.reshape(d)


def _pad_lanes(w):
    return jnp.pad(w, ((0, 0), (0, LANES - w.shape[1])))


def _nt_sum(pairs, name):
    acc = None
    for a, b in pairs:
        acc = _mm(a, b, tb=True, add=acc, name=name)
    return acc


def _conv_mixer_fwd(h, w_in, layer, w_dw, tag):
    d, ns = h.shape[1], w_in.shape[3]
    inter = (3, _tile(math.gcd(d, ns), TILES))
    p = _mm(h, w_in, b_layer=layer, inter=inter, out_dtype=ACT_DTYPE, name=f"{tag}_in")
    return _gconv_fwd(p, w_dw, inter[1], name=f"{tag}_gate"), (h, w_in, layer, inter, p, w_dw)


def _conv_mixer_bwd(cache, do, tag):
    h, w_in, layer, inter, p, w_dw = cache
    dp, dw_dw = _gconv_bwd(p, w_dw, do, inter[1], name=f"{tag}_gate_bwd")
    dw_in = _mm(h, dp, ta=True, out_shard=w_in.shape[3], inter=inter, out_dtype=WIRE_DTYPE, name=f"{tag}_dw_in")
    dh = _mm(dp, w_in, tb=True, b_layer=layer, inter=inter, name=f"{tag}_dh")
    return dh, {"w_in": dw_in, "w_dw": dw_dw}


def _fox_mixer_fwd(h, w_in, b_f, q_gain, k_gain, tag, rider=None):
    d = h.shape[1]
    nh = d // HEAD
    ws = [w_in[:, k * d:(k + 1) * d] for k in range(3)] + [_pad_lanes(w_in[:, 3 * d:])]
    q, k, v = [_mm(h, w, out_dtype=ACT_DTYPE, name=f"{tag}_in") for w in ws[:3]]
    f = _mm(h, ws[3], name=f"{tag}_in_f")
    gq = jnp.tile(q_gain, nh).reshape(1, d)
    gk = jnp.tile(k_gain, nh).reshape(1, d)
    bf = _pad_lanes(b_f.reshape(1, nh))
    qs, kn, logf = _fox_prep(q, k, f, gq, gk, bf, name=f"{tag}_prep")
    cum = _cumsum_rows(logf, reverse=False, name=f"{tag}_cum")
    aug = _fox_aug(qs, kn, cum, name=f"{tag}_aug")
    q_aug, k_aug = aug[:2], aug[2:]
    o, lse, landed = _fox_fwd_t(q_aug, k_aug, v, name=f"{tag}_attn", rider=rider)
    return o, (h, ws, q, k, v, f, gq, gk, bf, q_aug, k_aug, o, lse), landed


def _fox_mixer_bwd(cache, do, tag, rider=None):
    h, ws, q, k, v, f, gq, gk, bf, q_aug, k_aug, o, lse = cache
    s, d = q.shape
    nh = d // HEAD
    dd = _pair_rows(_fox_dd(do, o, name=f"{tag}_attn_dd"), nh)
    (dqs, dkn, dv, dcol, drow), landed = _fox_bwd_t(q_aug, k_aug, v, lse, dd, do, name=f"{tag}_attn_bwd", rider=rider)
    dcum = _pad_lanes(drow[:, :2, :].reshape(nh, s).T - dcol[:, ::HEAD])
    dlogf = _cumsum_rows(dcum, reverse=True, name=f"{tag}_cum_bwd")
    dq, dk, df, dgq, dgk, dbf = _fox_prep_bwd(q, k, f, gq, gk, bf, dqs, dkn, dlogf, name=f"{tag}_prep_bwd")
    dps = (dq, dk, dv, df)
    dws = [_mm(h, dp, ta=True, name=f"{tag}_dw_in") for dp in dps]
    dw_in = jnp.concatenate(dws[:3] + [dws[3][:, :nh]], axis=1)
    dh = _nt_sum(list(zip(dps, ws)), f"{tag}_dh")
    return dh, {"w_in": dw_in, "b_f": dbf[0, :nh], "q_gain": dgq.reshape(nh, HEAD).sum(0), "k_gain": dgk.reshape(nh, HEAD).sum(0)}, landed


def _ssd_mixer_fwd(h, w_in, conv_w, conv_b, dt_bias, a_log, d_skip, norm_w, tag):
    di = norm_w.shape[0]
    nh = di // HEAD
    gn = (conv_w.shape[1] - di) // 2
    cuts = [0, di, 2 * di, 2 * di + gn, 2 * di + 2 * gn]
    ws = [w_in[:, cuts[k]:cuts[k + 1]] for k in range(4)] + [_pad_lanes(w_in[:, cuts[4]:])]
    z, xr, br, cr = [_mm(h, w, out_dtype=ACT_DTYPE, name=f"{tag}_in") for w in ws[:4]]
    dtr = _mm(h, ws[4], name=f"{tag}_in_dt")
    ccuts = [0, di, di + gn, di + 2 * gn]
    cws = [conv_w[:, ccuts[k]:ccuts[k + 1]] for k in range(3)]
    cbs = [conv_b[ccuts[k]:ccuts[k + 1]] for k in range(3)]
    xs, bm, cm = [_sconv_fwd(r, w, b, name=f"{tag}_conv") for r, w, b in zip((xr, br, cr), cws, cbs)]
    dtb = _pad_lanes(dt_bias.reshape(1, nh))
    alg = _pad_lanes(a_log.reshape(1, nh))
    _dt, acum, dtx, acx = _ssd_prep(dtr, dtb, alg, nh, name=f"{tag}_prep")
    acum_t = acum[:, :nh].T
    d_x = jnp.repeat(d_skip, HEAD).reshape(1, di)
    y, states = _ssd_scan_fwd(xs, bm, cm, dtx, acx, acum, acum_t, d_x, name=f"{tag}_scan")
    gs = di // (gn // SSM_STATE)
    o = _ssd_gate(y, z, norm_w, gs, name=f"{tag}_gate")
    return o, (h, ws, z, (xr, br, cr), dtr, cws, cbs, xs, bm, cm, dtb, alg, dtx, acx, acum, acum_t, d_x, states, y, norm_w, gs, nh)


def _ssd_mixer_bwd(cache, do, tag, rider=None):
    h, ws, z, raws, dtr, cws, cbs, xs, bm, cm, dtb, alg, dtx, acx, acum, acum_t, d_x, states, y, norm_w, gs, nh = cache
    dy, dz, dnorm = _ssd_gate_bwd(y, z, norm_w, do, gs, name=f"{tag}_gate_bwd")
    (dxs, dbm, dcm, ddtx, dacx, dd_x), landed = _ssd_scan_bwd(xs, bm, cm, dtx, acx, acum, acum_t, d_x, states, dy, name=f"{tag}_scan_bwd",
                                                             rider=rider)
    ddtr, ddtb, dalg = _ssd_prep_bwd(dtr, dtb, alg, ddtx, dacx, nh, name=f"{tag}_prep_bwd")
    conv = [_sconv_bwd(r, w, b, da, name=f"{tag}_conv_bwd") for r, w, b, da in zip(raws, cws, cbs, (dxs, dbm, dcm))]
    dps = (dz, conv[0][0], conv[1][0], conv[2][0], ddtr)
    dws = [_mm(h, dp, ta=True, name=f"{tag}_dw_in") for dp in dps]
    dw_in = jnp.concatenate(dws[:4] + [dws[4][:, :nh]], axis=1)
    dh = _nt_sum(list(zip(dps, ws)), f"{tag}_dh")
    return dh, {"w_in": dw_in, "conv_w": jnp.concatenate([c[1] for c in conv], axis=1), "conv_b": jnp.concatenate([c[2] for c in conv]),
                "dt_bias": ddtb[0, :nh], "a_log": dalg[0, :nh], "d": dd_x.reshape(nh, HEAD).sum(1), "norm_w": dnorm}, landed


def _rows_natural(cm, layer):
    return cm[:, layer].reshape(-1, cm.shape[3])


def _cols_natural(cm, layer):
    return jnp.moveaxis(cm[:, layer], 0, 1).reshape(cm.shape[2], -1)


def _cols_chip_major(g):
    return jnp.moveaxis(g.reshape(g.shape[0], N_CHIPS, -1), 1, 0).astype(WIRE_DTYPE)


MIXERS = ("conv", "fox", "ssd")


def _model_layer(name, l):
    return l if name.startswith("ffn") else 3 * l + MIXERS.index(name.split("_")[0])


def _piece(pieces, layer):
    for arr, start in pieces:
        if start <= layer < start + arr.shape[1]:
            return arr, layer - start
    raise KeyError(layer)


def _with_own(landed, shards, chip):
    return [lax.dynamic_update_slice(g, w[None], (chip, 0, 0, 0)) for g, w in zip(landed, shards)]


def _reduce_begin(gs, place):
    from_sibling = _swap_halves_list(gs, name="reduce_halves")
    return [_add_half(g, r, place, name="reduce_add_sibling") for g, r in zip(gs, from_sibling)]


def _reduce_end(by_chip, chip_sums, place):
    reds = [_sum_chips(b, s, place, name="reduce_sum_chips") for b, s in zip(by_chip, chip_sums)]
    return _join_halves_list(reds, name="reduce_share")


def _local_step(x, tgt, fw, cm, late, place):
    depth = fw["mix_norm"].shape[0]
    chip = place[0]
    cm = {n: list(p) for n, p in cm.items()}
    late = dict(late)

    def gather_rider(group):
        return _gather_ici_rider([shards for shards, _ in group.values()]) if group else None

    def land(group, landed):
        if group:
            full = _with_own(_forward_halves_list(landed, name="gather_forward"), [shards for shards, _ in group.values()], chip)
            for (n, (_, start)), arr in zip(group.items(), full):
                cm[n].append((arr, start))

    layers = []
    xin = x
    for i in range(depth):
        kind, j = i % 3, i // 3
        tag = f"l{i}"
        h = _rms(xin, fw["mix_norm"][i], name=f"{tag}_norm1")
        if kind == 0:
            w_in, jl = _piece(cm["conv_w_in"], j)
            o, mc = _conv_mixer_fwd(h, w_in, jl, fw["conv_w_dw"][j], tag + "_conv")
        elif kind == 1:
            group = late.pop(("attention", i), None)
            o, mc, landed = _fox_mixer_fwd(h, _cols_natural(*_piece(cm["fox_w_in"], j)), fw["fox_b_f"][j], fw["fox_q_gain"][j],
                                           fw["fox_k_gain"][j], tag + "_fox", rider=gather_rider(group))
            land(group, landed)
        else:
            o, mc = _ssd_mixer_fwd(h, _cols_natural(*_piece(cm["ssd_w_in"], j)), fw["ssd_conv_w"][j], fw["ssd_conv_b"][j], fw["ssd_dt_bias"][j],
                                   fw["ssd_a_log"][j], fw["ssd_d"][j], fw["ssd_norm_w"][j], tag + "_ssd")
        w_out = _rows_natural(*_piece(cm[MIXERS[kind] + "_w_out"], j))
        x1 = _mm(o, w_out, add=xin, name=f"{tag}_mix_out")
        h2 = _rms(x1, fw["ffn_norm"][i], name=f"{tag}_norm2")
        w_gu, il = _piece(cm["ffn_w_gu"], i)
        w_down = _rows_natural(*_piece(cm["ffn_w_down"], i))
        inter = (2, _tile(math.gcd(w_down.shape[0], w_gu.shape[3]), TILES))
        group = late.pop(("ffn", i), None)
        gu, a, landed = _ffn_up(h2, w_gu, il, inter[1], name=f"{tag}_ffn_gu", rider=gather_rider(group))
        land(group, landed)
        layers.append((xin, o, mc, w_out, x1, h2, w_gu, il, w_down, inter, gu, a))
        xin = _mm(a, w_down, add=x1, name=f"{tag}_ffn_down")
    loss, dx, dxb = _loss_head(xin, tgt, name="loss_head")

    small = {k: [None] * v.shape[0] for k, v in fw.items()}
    reds = {}
    riding = None
    at_end = []
    for i in reversed(range(depth)):
        kind, j = i % 3, i // 3
        tag = f"l{i}"
        xin, o, mc, w_out, x1, h2, w_gu, il, w_down, inter, gu, a = layers[i]
        mine = [(("ffn_w_down", i), _mm(a, dxb, ta=True, out_dtype=WIRE_DTYPE, name=f"{tag}_dw_down").reshape(N_CHIPS, -1, w_down.shape[1]))]
        dgu = _ffn_back(dxb, w_down, gu, inter[1], name=f"{tag}_ffn_back")
        mine.append((("ffn_w_gu", i), _mm(h2, dgu, ta=True, out_shard=w_gu.shape[3], inter=inter, out_dtype=WIRE_DTYPE, name=f"{tag}_dw_gu")))
        rider = riding[2] if riding and kind == 0 else None
        dh2 = _mm(dgu, w_gu, tb=True, b_layer=il, inter=inter, name=f"{tag}_dh2", rider=rider)
        if rider is not None:
            dh2, landed = dh2
            reds.update(zip(riding[0], _reduce_end(landed, riding[1], place)))
            riding = None
        dx1, dx1b, small["ffn_norm"][i] = _rms_bwd(x1, fw["ffn_norm"][i], dh2, dx, name=f"{tag}_norm2_bwd")
        mine.append(((MIXERS[kind] + "_w_out", j),
                     _mm(o, dx1b, ta=True, out_dtype=WIRE_DTYPE, name=f"{tag}_dw_out").reshape(N_CHIPS, -1, w_out.shape[1])))
        do = _mm(dx1b, w_out, tb=True, out_dtype=ACT_DTYPE, name=f"{tag}_do")
        rider = riding[2] if riding else None
        if kind == 0:
            dh, mg = _conv_mixer_bwd(mc, do, tag + "_conv")
        elif kind == 1:
            dh, mg, landed = _fox_mixer_bwd(mc, do, tag + "_fox", rider=rider)
        else:
            dh, mg, landed = _ssd_mixer_bwd(mc, do, tag + "_ssd", rider=rider)
        if rider is not None:
            reds.update(zip(riding[0], _reduce_end(landed, riding[1], place)))
            riding = None
        for k, v in mg.items():
            if k == "w_in":
                mine.append(((f"{MIXERS[kind]}_w_in", j), v if kind == 0 else _cols_chip_major(v)))
            else:
                small[f"{MIXERS[kind]}_{k}"][j] = v
        dx, dxb, small["mix_norm"][i] = _rms_bwd(xin, fw["mix_norm"][i], dh, dx1, name=f"{tag}_norm1_bwd")
        if i > 0:
            chip_sums = _reduce_begin([g for _, g in mine], place)
            riding = ([k for k, _ in mine], chip_sums, _scatter_rider(chip_sums))
        else:
            at_end += mine
    assert riding is None
    chip_sums = _reduce_begin([g for _, g in at_end], place)
    by_chip = _run_rider(_scatter_rider(chip_sums), name="reduce_chips")
    reds.update(zip([k for k, _ in at_end], _reduce_end(by_chip, chip_sums, place)))
    return loss, dx, {k: jnp.stack(v) for k, v in small.items()}, reds


ANY = pl.BlockSpec(memory_space=pl.ANY)
VMEM_SPEC = pl.BlockSpec(memory_space=pltpu.VMEM)


def _place():
    return lax.axis_index("x"), lax.axis_index("y"), lax.axis_index("c")


def _remote(src, dst, send_sems, recv_sems, k, to):
    return pltpu.make_async_remote_copy(src_ref=src, dst_ref=dst, send_sem=send_sems.at[k], recv_sem=recv_sems.at[k],
                                        device_id=to, device_id_type=MESH)


def _half_of(ref, h, shape):
    layers, rows, _ = shape
    if layers % 2 == 0:
        return ref.at[pl.ds(h * (layers // 2), layers // 2)]
    return ref.at[:, pl.ds(pl.multiple_of(h * (rows // 2), 16), rows // 2)]


def _row_half(ref, h, rows):
    return ref.at[:, pl.ds(pl.multiple_of(h * (rows // 2), 16), rows // 2)]


def _gather_list(ws, *, name):
    n = len(ws)

    def body(*refs):
        w_refs, o_refs, send_sems, recv_sems = refs[:n], refs[n:2 * n], refs[2 * n], refs[2 * n + 1]
        x, y, c = _place()
        me, sibling, m = (x, y, c), (x, y, 1 - c), 2 * x + y
        chips = [(1 - x, y), (x, 1 - y), (1 - x, 1 - y)]
        first, passed = [], []
        for p, (w_ref, o_ref) in enumerate(zip(w_refs, o_refs)):
            for j, (px, py) in enumerate(chips):
                cp = _remote(_half_of(w_ref, c, ws[p].shape), _half_of(o_ref.at[m], c, ws[p].shape), send_sems, recv_sems, 6 * p + j, (px, py, c))
                cp.start()
                first.append(cp)
        for p, o_ref in enumerate(o_refs):
            for j, (px, py) in enumerate(chips):
                blk = _half_of(o_ref.at[2 * px + py], c, ws[p].shape)
                _remote(blk, blk, send_sems, recv_sems, 6 * p + j, me).wait_recv()
                fwd = _remote(blk, blk, send_sems, recv_sems, 6 * p + 3 + j, sibling)
                fwd.start()
                passed.append(fwd)
        for p, o_ref in enumerate(o_refs):
            for j, (px, py) in enumerate(chips):
                blk = _half_of(o_ref.at[2 * px + py], 1 - c, ws[p].shape)
                _remote(blk, blk, send_sems, recv_sems, 6 * p + 3 + j, me).wait_recv()
        for cp in first + passed:
            cp.wait_send()

    return pl.pallas_call(
        body, name=name, in_specs=[ANY] * n, out_specs=[ANY] * n,
        out_shape=[jax.ShapeDtypeStruct((N_CHIPS,) + w.shape, w.dtype) for w in ws],
        scratch_shapes=[pltpu.SemaphoreType.DMA((6 * n,)), pltpu.SemaphoreType.DMA((6 * n,))],
    )(*ws)


class _Rider:
    def __init__(self, arrays, out_shapes, n_sems, start, finish):
        self.arrays, self.out_shapes, self.n_sems, self.start, self.finish = list(arrays), list(out_shapes), n_sems, start, finish

    @property
    def scratch(self):
        return [pltpu.SemaphoreType.DMA((self.n_sems,)), pltpu.SemaphoreType.DMA((self.n_sems,))]


def _carry(body, rider, n_prefetch, n_in, n_out, grid):
    if rider is None:
        return body
    ri, ro = len(rider.arrays), len(rider.out_shapes)

    def wrapped(*refs):
        pre, rest = refs[:n_prefetch], refs[n_prefetch:]
        ins, r_in = rest[:n_in], rest[n_in:n_in + ri]
        outs, r_out = rest[n_in + ri:n_in + ri + n_out], rest[n_in + ri + n_out:n_in + ri + n_out + ro]
        scratch = rest[n_in + ri + n_out + ro:]
        first = functools.reduce(jnp.logical_and, [pl.program_id(a) == 0 for a in range(len(grid))])
        last = functools.reduce(jnp.logical_and, [pl.program_id(a) == g - 1 for a, g in enumerate(grid)])

        @pl.when(first)
        def _():
            rider.start(r_in, r_out, scratch[-2], scratch[-1])

        body(*pre, *ins, *outs, *scratch[:-2])

        @pl.when(last)
        def _():
            rider.finish(r_in, r_out, scratch[-2], scratch[-1])

    return wrapped


def _run_rider(rider, *, name):
    n = len(rider.arrays)

    def body(*refs):
        rider.start(refs[:n], refs[n:2 * n], refs[-2], refs[-1])
        rider.finish(refs[:n], refs[n:2 * n], refs[-2], refs[-1])

    return pl.pallas_call(body, name=name, in_specs=[ANY] * n, out_specs=[ANY] * len(rider.out_shapes), out_shape=rider.out_shapes,
                          scratch_shapes=rider.scratch)(*rider.arrays)


def _chips_of(x, y):
    return [(1 - x, y), (x, 1 - y), (1 - x, 1 - y)]


def _gather_ici_rider(ws):
    def copies(w_refs, o_refs, send_sems, recv_sems):
        x, y, c = _place()
        m = 2 * x + y
        return [_remote(_half_of(w_ref, c, ws[p].shape), _half_of(o_ref.at[m], c, ws[p].shape), send_sems, recv_sems, 3 * p + j, (px, py, c))
                for p, (w_ref, o_ref) in enumerate(zip(w_refs, o_refs)) for j, (px, py) in enumerate(_chips_of(x, y))]

    def start(w_refs, o_refs, send_sems, recv_sems):
        for cp in copies(w_refs, o_refs, send_sems, recv_sems):
            cp.start()

    def finish(w_refs, o_refs, send_sems, recv_sems):
        x, y, c = _place()
        for p, o_ref in enumerate(o_refs):
            for j, (px, py) in enumerate(_chips_of(x, y)):
                blk = _half_of(o_ref.at[2 * px + py], c, ws[p].shape)
                _remote(blk, blk, send_sems, recv_sems, 3 * p + j, (x, y, c)).wait_recv()
        for cp in copies(w_refs, o_refs, send_sems, recv_sems):
            cp.wait_send()

    return _Rider(ws, [jax.ShapeDtypeStruct((N_CHIPS,) + w.shape, w.dtype) for w in ws], 3 * len(ws), start, finish)


def _forward_halves_list(gathered, *, name):
    n = len(gathered)
    shapes = [g.shape[1:] for g in gathered]

    def body(*refs):
        o_refs, send_sems, recv_sems = refs[n:2 * n], refs[2 * n], refs[2 * n + 1]
        x, y, c = _place()
        cps = []
        for p, o_ref in enumerate(o_refs):
            for j, (px, py) in enumerate(_chips_of(x, y)):
                blk = _half_of(o_ref.at[2 * px + py], c, shapes[p])
                cp = _remote(blk, blk, send_sems, recv_sems, 3 * p + j, (x, y, 1 - c))
                cp.start()
                cps.append(cp)
        for p, o_ref in enumerate(o_refs):
            for j, (px, py) in enumerate(_chips_of(x, y)):
                blk = _half_of(o_ref.at[2 * px + py], 1 - c, shapes[p])
                _remote(blk, blk, send_sems, recv_sems, 3 * p + j, (x, y, c)).wait_recv()
        for cp in cps:
            cp.wait_send()

    return pl.pallas_call(
        body, name=name, in_specs=[ANY] * n, out_specs=[ANY] * n, out_shape=[jax.ShapeDtypeStruct(g.shape, g.dtype) for g in gathered],
        input_output_aliases={p: p for p in range(n)},
        scratch_shapes=[pltpu.SemaphoreType.DMA((3 * n,)), pltpu.SemaphoreType.DMA((3 * n,))],
    )(*gathered)


def _scatter_rider(sums):
    def copies(a_refs, o_refs, send_sems, recv_sems):
        x, y, c = _place()
        m = 2 * x + y
        return [_remote(a_ref.at[2 * px + py], o_ref.at[m], send_sems, recv_sems, 3 * p + j, (px, py, c))
                for p, (a_ref, o_ref) in enumerate(zip(a_refs, o_refs)) for j, (px, py) in enumerate(_chips_of(x, y))]

    def start(a_refs, o_refs, send_sems, recv_sems):
        for cp in copies(a_refs, o_refs, send_sems, recv_sems):
            cp.start()

    def finish(a_refs, o_refs, send_sems, recv_sems):
        x, y, c = _place()
        for p, o_ref in enumerate(o_refs):
            for j, (px, py) in enumerate(_chips_of(x, y)):
                blk = o_ref.at[2 * px + py]
                _remote(blk, blk, send_sems, recv_sems, 3 * p + j, (x, y, c)).wait_recv()
        for cp in copies(a_refs, o_refs, send_sems, recv_sems):
            cp.wait_send()

    return _Rider(sums, [jax.ShapeDtypeStruct(a.shape, a.dtype) for a in sums], 3 * len(sums), start, finish)


def _gather_small(v, *, name):
    r, w = v.shape

    def body(v_ref, o_ref, send_sems, recv_sems):
        x, y, c = _place()
        m = 2 * x + y
        chips = [(1 - x, y), (x, 1 - y), (1 - x, 1 - y)]
        o_ref[m] = v_ref[...]
        sends = [_remote(v_ref, o_ref.at[m], send_sems, recv_sems, j, (px, py, c)) for j, (px, py) in enumerate(chips)]
        for cp in sends:
            cp.start()
        for j, (px, py) in enumerate(chips):
            blk = o_ref.at[2 * px + py]
            _remote(blk, blk, send_sems, recv_sems, j, (x, y, c)).wait_recv()
        for cp in sends:
            cp.wait_send()

    return pl.pallas_call(
        body, name=name, in_specs=[VMEM_SPEC], out_specs=VMEM_SPEC, out_shape=jax.ShapeDtypeStruct((4, r, w), v.dtype),
        scratch_shapes=[pltpu.SemaphoreType.DMA((3,)), pltpu.SemaphoreType.DMA((3,))],
    )(v)


def _swap_halves_list(gs, *, name):
    n = len(gs)

    def body(*refs):
        g_refs, o_refs, send_sems, recv_sems = refs[:n], refs[n:2 * n], refs[2 * n], refs[2 * n + 1]
        x, y, c = _place()
        cps = [_remote(_row_half(g_ref, 1 - c, gs[p].shape[1]), o_ref, send_sems, recv_sems, p, (x, y, 1 - c))
               for p, (g_ref, o_ref) in enumerate(zip(g_refs, o_refs))]
        for cp in cps:
            cp.start()
        for cp in cps:
            cp.wait()

    return pl.pallas_call(
        body, name=name, in_specs=[ANY] * n, out_specs=[ANY] * n,
        out_shape=[jax.ShapeDtypeStruct((g.shape[0], g.shape[1] // 2, g.shape[2]), g.dtype) for g in gs],
        scratch_shapes=[pltpu.SemaphoreType.DMA((n,)), pltpu.SemaphoreType.DMA((n,))],
    )(*gs)


def _join_halves_list(reds, *, name):
    n = len(reds)

    def body(*refs):
        o_refs, send_sems, recv_sems = refs[n:2 * n], refs[2 * n], refs[2 * n + 1]
        x, y, c = _place()
        cps = []
        for p, o_ref in enumerate(o_refs):
            rh = reds[p].shape[0] // 2
            mine = o_ref.at[pl.ds(pl.multiple_of(c * rh, SUBLANES), rh)]
            cp = _remote(mine, mine, send_sems, recv_sems, p, (x, y, 1 - c))
            cp.start()
            cps.append(cp)
        for p, o_ref in enumerate(o_refs):
            rh = reds[p].shape[0] // 2
            other = o_ref.at[pl.ds(pl.multiple_of((1 - c) * rh, SUBLANES), rh)]
            _remote(other, other, send_sems, recv_sems, p, (x, y, c)).wait_recv()
        for cp in cps:
            cp.wait_send()

    return pl.pallas_call(
        body, name=name, in_specs=[ANY] * n, out_specs=[ANY] * n, out_shape=[jax.ShapeDtypeStruct(r.shape, r.dtype) for r in reds],
        input_output_aliases={p: p for p in range(n)},
        scratch_shapes=[pltpu.SemaphoreType.DMA((n,)), pltpu.SemaphoreType.DMA((n,))],
    )(*reds)


def _allreduce_small(v, *, name):
    r, w = v.shape

    def body(v_ref, o_ref, slots, send_sems, recv_sems):
        x, y, c = _place()
        me = 4 * x + 2 * y + c
        slots[me] = v_ref[...]
        peers = [((1 - x) if k & 4 else x, (1 - y) if k & 2 else y, (1 - c) if k & 1 else c) for k in range(1, 8)]
        sends = [_remote(v_ref, slots.at[me], send_sems, recv_sems, k, p) for k, p in enumerate(peers)]
        for cp in sends:
            cp.start()
        for k, (px, py, pc) in enumerate(peers):
            blk = slots.at[4 * px + 2 * py + pc]
            _remote(blk, blk, send_sems, recv_sems, k, (x, y, c)).wait_recv()
        for cp in sends:
            cp.wait_send()
        acc = slots[0]
        for k in range(1, 8):
            acc = acc + slots[k]
        o_ref[...] = acc

    return pl.pallas_call(
        body, name=name, in_specs=[VMEM_SPEC], out_specs=VMEM_SPEC, out_shape=jax.ShapeDtypeStruct(v.shape, v.dtype),
        scratch_shapes=[pltpu.VMEM((8, r, w), F32), pltpu.SemaphoreType.DMA((7,)), pltpu.SemaphoreType.DMA((7,))],
    )(v)


def _row_tile(r):
    return r if r <= 512 else _tile(r, (512, 256, 128, 64, 32, 16))


def _add_half(g, recv, place, *, name):
    n, r, w = g.shape
    tm = _row_tile(r // 2)
    nb = (r // 2) // tm

    def body(place_ref, g_ref, r_ref, o_ref):
        o_ref[...] = (g_ref[...].astype(F32) + r_ref[...].astype(F32)).astype(o_ref.dtype)

    return pl.pallas_call(
        body, name=name,
        grid_spec=pltpu.PrefetchScalarGridSpec(
            num_scalar_prefetch=1, grid=(n, nb),
            in_specs=[pl.BlockSpec((None, tm, w), lambda k, i, p: (k, p[1] * nb + i, 0)), pl.BlockSpec((None, tm, w), lambda k, i, p: (k, i, 0))],
            out_specs=pl.BlockSpec((None, tm, w), lambda k, i, p: (k, i, 0))),
        out_shape=jax.ShapeDtypeStruct(recv.shape, g.dtype), compiler_params=_params(("parallel", "parallel")),
    )(place, g, recv)


def _sum_chips(recv, own, place, *, name):
    n, r, w = recv.shape
    tm = _row_tile(r)
    nb = r // tm

    def body(place_ref, *refs):
        own_ref, o_ref = refs[n], refs[n + 1]
        acc = None
        for k in range(n):
            term = jnp.where(place_ref[0] == k, own_ref[...], refs[k][...]).astype(F32)
            acc = term if acc is None else acc + term
        o_ref[...] = acc

    recv_specs = [pl.BlockSpec((None, tm, w), lambda i, p, k=k: (jnp.where(p[0] == k, (k + 1) % n, k), i, 0)) for k in range(n)]
    return pl.pallas_call(
        body, name=name,
        grid_spec=pltpu.PrefetchScalarGridSpec(
            num_scalar_prefetch=1, grid=(nb,),
            in_specs=recv_specs + [pl.BlockSpec((None, tm, w), lambda i, p: (p[0], i, 0))],
            out_specs=pl.BlockSpec((tm, w), lambda i, p: (p[1] * nb + i, 0))),
        out_shape=jax.ShapeDtypeStruct((2 * r, w), F32), compiler_params=_params(("parallel",)),
    )(place, *([recv] * n), own)


def _adamw(w, g, m, v, *, name):
    shape = w.shape
    cols = shape[-1]
    rows = math.prod(shape[:-1])
    tm = _tile(rows, (256, 128, 64, 32, 16, 8))
    c1 = 1.0 - ADAM_B1 ** ADAM_STEP
    c2 = 1.0 - ADAM_B2 ** ADAM_STEP

    def fn(i, nrt, wv, gv, mv, vv):
        mn = ADAM_B1 * mv + (1.0 - ADAM_B1) * gv
        vn = ADAM_B2 * vv + (1.0 - ADAM_B2) * (gv * gv)
        delta = -ADAM_LR * ((mn / c1) / (jnp.sqrt(vn / c2) + ADAM_EPS) + ADAM_WD * wv)
        return (delta, mn, vn), ()

    outs = _rows(fn, name=name, s=rows, tm=tm, ins=[("row", t.reshape(rows, cols), cols, _c0) for t in (w, g, m, v)],
                 outs=[(cols, cols, _c0, F32)] * 3)
    return [o.reshape(shape) for o in outs]


WEIGHTS = ["mix_norm", "ffn_norm", "ffn_w_gu", "ffn_w_down", "conv_w_in", "conv_w_dw", "conv_w_out", "fox_w_in", "fox_b_f", "fox_q_gain",
           "fox_k_gain", "fox_w_out", "ssd_w_in", "ssd_conv_w", "ssd_conv_b", "ssd_dt_bias", "ssd_a_log", "ssd_d", "ssd_norm_w", "ssd_w_out"]
SHARD_AXIS = {"ffn_w_gu": 2, "ffn_w_down": 1, "conv_w_in": 2, "conv_w_dw": 2, "conv_w_out": 1, "fox_w_in": 2, "fox_w_out": 1, "ssd_w_in": 2,
              "ssd_conv_w": 2, "ssd_conv_b": 1, "ssd_norm_w": 1, "ssd_w_out": 1}
BIG = ["ffn_w_gu", "ffn_w_down", "conv_w_in", "conv_w_out", "fox_w_in", "fox_w_out", "ssd_w_in", "ssd_w_out"]
SMALL_SHARDED = ["conv_w_dw", "ssd_conv_w", "ssd_conv_b", "ssd_norm_w"]
N_CHIPS = 4


def _pack_flat(parts, pad_to):
    flat = [p.reshape(-1) for p in parts]
    offs, n = [], 0
    for f in flat:
        offs.append(n)
        n += f.shape[0]
    total = -(-n // pad_to) * pad_to
    if total > n:
        flat.append(jnp.zeros((total - n,), flat[0].dtype))
    return jnp.concatenate(flat).reshape(-1, LANES), offs


def kernel(x, mix_norm, ffn_norm, ffn_w_gu, ffn_w_down, conv_w_in, conv_w_dw, conv_w_out, fox_w_in, fox_b_f, fox_q_gain, fox_k_gain, fox_w_out, ssd_w_in, ssd_conv_w, ssd_conv_b, ssd_dt_bias, ssd_a_log, ssd_d, ssd_norm_w, ssd_w_out, loss_target, m_mix_norm, m_ffn_norm, m_ffn_w_gu, m_ffn_w_down, m_conv_w_in, m_conv_w_dw, m_conv_w_out, m_fox_w_in, m_fox_b_f, m_fox_q_gain, m_fox_k_gain, m_fox_w_out, m_ssd_w_in, m_ssd_conv_w, m_ssd_conv_b, m_ssd_dt_bias, m_ssd_a_log, m_ssd_d, m_ssd_norm_w, m_ssd_w_out, v_mix_norm, v_ffn_norm, v_ffn_w_gu, v_ffn_w_down, v_conv_w_in, v_conv_w_dw, v_conv_w_out, v_fox_w_in, v_fox_b_f, v_fox_q_gain, v_fox_k_gain, v_fox_w_out, v_ssd_w_in, v_ssd_conv_w, v_ssd_conv_b, v_ssd_dt_bias, v_ssd_a_log, v_ssd_d, v_ssd_norm_w, v_ssd_w_out):
    w = dict(zip(WEIGHTS, (mix_norm, ffn_norm, ffn_w_gu, ffn_w_down, conv_w_in, conv_w_dw, conv_w_out, fox_w_in, fox_b_f, fox_q_gain, fox_k_gain,
                           fox_w_out, ssd_w_in, ssd_conv_w, ssd_conv_b, ssd_dt_bias, ssd_a_log, ssd_d, ssd_norm_w, ssd_w_out)))
    m1 = dict(zip(WEIGHTS, (m_mix_norm, m_ffn_norm, m_ffn_w_gu, m_ffn_w_down, m_conv_w_in, m_conv_w_dw, m_conv_w_out, m_fox_w_in, m_fox_b_f,
                            m_fox_q_gain, m_fox_k_gain, m_fox_w_out, m_ssd_w_in, m_ssd_conv_w, m_ssd_conv_b, m_ssd_dt_bias, m_ssd_a_log, m_ssd_d,
                            m_ssd_norm_w, m_ssd_w_out)))
    m2 = dict(zip(WEIGHTS, (v_mix_norm, v_ffn_norm, v_ffn_w_gu, v_ffn_w_down, v_conv_w_in, v_conv_w_dw, v_conv_w_out, v_fox_w_in, v_fox_b_f,
                            v_fox_q_gain, v_fox_k_gain, v_fox_w_out, v_ssd_w_in, v_ssd_conv_w, v_ssd_conv_b, v_ssd_dt_bias, v_ssd_a_log, v_ssd_d,
                            v_ssd_norm_w, v_ssd_w_out)))
    cx, cy, cc = _place()
    chip = 2 * cx + cy

    place = jnp.stack([chip, cc]).astype(jnp.int32)
    depth = mix_norm.shape[0]
    attention = next((i for i in range(depth) if i % 3 == 1), depth)

    def carrier_of(layer):
        return None if layer == 0 else ("ffn", layer - 1) if layer <= attention else ("attention", attention)

    early, late, cm = {}, {}, {n: [] for n in BIG}
    for n in BIG:
        wb = w[n].astype(WIRE_DTYPE)
        keys = [carrier_of(_model_layer(n, l)) for l in range(wb.shape[0])]
        for key in dict.fromkeys(keys):
            first, count = keys.index(key), keys.count(key)
            if key is None:
                early[n] = wb[first:first + count]
            else:
                late.setdefault(key, {})[n] = (wb[first:first + count], first)
    gathered = _with_own(_gather_list(list(early.values()), name="gather_weights"), list(early.values()), chip)
    for n, g_ in zip(early, gathered):
        cm[n].append((g_, 0))
    sp, soffs = _pack_flat([w[n] for n in SMALL_SHARDED], SUBLANES * LANES)
    sgath = _gather_small(sp, name="gather_small").reshape(N_CHIPS, -1)
    full = {n: w[n] for n in WEIGHTS if n not in SHARD_AXIS}
    for n, off in zip(SMALL_SHARDED, soffs):
        full[n] = jnp.concatenate([sgath[j, off:off + w[n].size].reshape(w[n].shape) for j in range(N_CHIPS)], axis=SHARD_AXIS[n])

    loss, gx, grads, reds = _local_step(x[0], loss_target[0], full, cm, late, place)
    loss = lax.psum(loss, ("x", "y", "c"))

    small_names = [n for n in WEIGHTS if n not in BIG]
    sm, smoffs = _pack_flat([grads[n] for n in small_names], SUBLANES * LANES)
    sred = _allreduce_small(sm, name="allreduce_small").reshape(-1)

    g = {n: jnp.stack([reds[(n, l)] for l in range(w[n].shape[0])]).reshape(w[n].shape) for n in BIG}
    for n, off in zip(small_names, smoffs):
        fullg = sred[off:off + grads[n].size].reshape(grads[n].shape)
        if n in SHARD_AXIS:
            ax = SHARD_AXIS[n]
            fullg = lax.dynamic_slice_in_dim(fullg, chip * w[n].shape[ax], w[n].shape[ax], axis=ax)
        g[n] = fullg

    deltas, new_m, new_v = [], [], []
    for n in WEIGHTS:
        dl, mn, vn = _adamw(w[n], g[n], m1[n], m2[n], name=f"adamw_{n}")
        deltas.append(dl)
        new_m.append(mn)
        new_v.append(vn)
    return (loss, gx[None], *[g[n] for n in WEIGHTS], *deltas, *new_m, *new_v)
```

```python
import functools
import math

import jax
import jax.numpy as jnp
from jax import lax
from jax.experimental import pallas as pl
from jax.experimental.pallas import tpu as pltpu

F32 = jnp.float32
MM_DTYPE = jnp.bfloat16
ACT_DTYPE = jnp.bfloat16
WIRE_DTYPE = jnp.bfloat16

RMS_EPS = 1e-6
HEAD = 64
SSM_STATE = 128
SSM_CHUNK = 128
LANES = 128
SUBLANES = 8
VMEM_LIMIT = 48 * 1024 * 1024

ADAM_LR, ADAM_B1, ADAM_B2, ADAM_EPS, ADAM_WD, ADAM_STEP = 0.001, 0.9, 0.999, 1e-08, 0.01, 10

HI = lax.Precision.HIGHEST
MESH = pl.DeviceIdType.MESH


def _tile(dim, prefs):
    for p in prefs:
        if dim % p == 0:
            return p
    return dim


def _params(sem):
    return pltpu.CompilerParams(dimension_semantics=sem, vmem_limit_bytes=VMEM_LIMIT)


def _sigmoid(x):
    return 1.0 / (1.0 + jnp.exp(-x))


def _softplus(x):
    return jnp.maximum(x, 0.0) + jnp.log(1.0 + jnp.exp(-jnp.abs(x)))


TILES = (1024, 1408, 768, 512, 256, 128)
MIN_STEP_WORK = 1 << 30


def _mm(a, b, *, ta=False, tb=False, add=None, out_dtype=F32, name, b_layer=None, out_shard=None, inter=None, rider=None):
    ka, m = (a.shape[0], a.shape[1]) if ta else (a.shape[1], a.shape[0])
    if b_layer is None:
        kb, n = (b.shape[1], b.shape[0]) if tb else (b.shape[0], b.shape[1])
        ns = None
    else:
        ns = b.shape[3]
        kb, n = (b.shape[0] * ns, b.shape[2]) if tb else (b.shape[2], b.shape[0] * ns)
    assert ka == kb, (a.shape, b.shape, ta, tb)
    k = ka
    tm = _tile(m, (1408, 1024, 512, 256, 128) if ta else (512, 256, 128))
    tn = _tile(n, TILES)
    tk = _tile(k, TILES)
    if inter:
        segs, bw = inter
        if tb:
            tk = bw
        else:
            tn = bw
        tps = ((k if tb else n) // bw) // segs
        col = lambda q: ((q % segs) * tps + q // segs) * bw
    else:
        col = lambda q: q * (tk if tb else tn)
    def vmem(tm_, tk_):
        out_b = jnp.dtype(out_dtype).itemsize * 2 + (8 if add is not None else 0) + 4
        return 2 * tk_ * (tm_ * a.dtype.itemsize + tn * b.dtype.itemsize) + tm_ * tn * out_b

    if ta:
        while tk * 2 <= k and k % (tk * 2) == 0 and tm * tn * tk < MIN_STEP_WORK and vmem(tm, tk * 2) < VMEM_LIMIT * 3 // 4:
            tk *= 2
    else:
        while tm * 2 <= m and m % (tm * 2) == 0 and tm * tn * tk < MIN_STEP_WORK and vmem(tm * 2, tk) < VMEM_LIMIT * 3 // 4:
            tm *= 2
    nk = k // tk
    a_spec = pl.BlockSpec((tk, tm), lambda i, j, q: (q, i)) if ta else pl.BlockSpec((tm, tk), lambda i, j, q: (i, q))
    if b_layer is None:
        b_spec = pl.BlockSpec((tn, tk), lambda i, j, q: (j, q)) if tb else pl.BlockSpec((tk, tn), lambda i, j, q: (q, j))
    elif tb:
        b_spec = pl.BlockSpec((None, None, tn, tk), lambda i, j, q: (col(q) // ns, b_layer, j, (col(q) % ns) // tk))
    else:
        b_spec = pl.BlockSpec((None, None, tk, tn), lambda i, j, q: (col(j) // ns, b_layer, q, (col(j) % ns) // tn))
    if out_shard:
        assert ta and add is None
        o_spec = pl.BlockSpec((None, tm, tn), lambda i, j, q: (col(j) // out_shard, i, (col(j) % out_shard) // tn))
        o_shape = jax.ShapeDtypeStruct((N_CHIPS, m, out_shard), out_dtype)
    else:
        o_spec = pl.BlockSpec((tm, tn), lambda i, j, q: (i, j))
        o_shape = jax.ShapeDtypeStruct((m, n), out_dtype)
    dims = (((0 if ta else 1,), (1 if tb else 0,)), ((), ()))
    has_add = add is not None

    def body(*refs):
        a_ref, b_ref = refs[0], refs[1]
        o_ref = refs[2 + has_add]
        p = lax.dot_general(a_ref[...].astype(MM_DTYPE), b_ref[...].astype(MM_DTYPE), dims, preferred_element_type=F32)

        def finish(acc):
            if has_add:
                acc = acc + refs[2][...].astype(F32)
            o_ref[...] = acc.astype(out_dtype)

        if nk == 1:
            finish(p)
        else:
            acc_ref = refs[3 + has_add]
            q = pl.program_id(2)

            @pl.when(q == 0)
            def _():
                acc_ref[...] = p

            @pl.when(q > 0)
            def _():
                acc_ref[...] += p

            @pl.when(q == nk - 1)
            def _():
                finish(acc_ref[...])

    args = [a, b] + ([add] if has_add else [])
    in_specs = [a_spec, b_spec] + ([o_spec] if has_add else [])
    grid = (m // tm, n // tn, nk)
    r_in, r_out, r_shapes, r_scratch, r_args = _rider_parts(rider)
    outs = pl.pallas_call(
        _carry(body, rider, 0, len(args), 1, grid), name=name, grid=grid, in_specs=in_specs + r_in, out_specs=[o_spec] + r_out,
        out_shape=[o_shape] + r_shapes, scratch_shapes=([pltpu.VMEM((tm, tn), F32)] if nk > 1 else []) + r_scratch,
        compiler_params=_params(("arbitrary",) * 3 if rider else ("parallel", "parallel", "arbitrary")),
    )(*args, *r_args)
    return (outs[0], list(outs[1:])) if rider else outs[0]


def _rows(fn, *, name, s, tm, ncol=1, ins, outs, accs=()):
    nrt = s // tm
    hb = tm // SUBLANES
    in_specs, args = [], []
    for spec in ins:
        kind, arr = spec[0], spec[1]
        if kind == "full":
            in_specs.append(pl.BlockSpec(arr.shape, lambda j, i: (0, 0)))
        elif kind == "col":
            _, _, bw, cmap = spec
            in_specs.append(pl.BlockSpec((arr.shape[0], bw), lambda j, i, cmap=cmap: (0, cmap(j))))
        elif kind == "row":
            _, _, bw, cmap = spec
            in_specs.append(pl.BlockSpec((tm, bw), lambda j, i, cmap=cmap: (i, cmap(j))))
        elif kind == "prev":
            _, _, bw, cmap = spec
            in_specs.append(pl.BlockSpec((SUBLANES, bw), lambda j, i, cmap=cmap: (jnp.maximum(i * hb - 1, 0), cmap(j))))
        elif kind == "next":
            _, _, bw, cmap = spec
            in_specs.append(pl.BlockSpec((SUBLANES, bw), lambda j, i, cmap=cmap: (jnp.minimum((i + 1) * hb, s // SUBLANES - 1), cmap(j))))
        else:
            raise ValueError(kind)
        args.append(arr)
    out_specs, out_shape = [], []
    for w, bw, cmap, dt in outs:
        out_specs.append(pl.BlockSpec((tm, bw), lambda j, i, cmap=cmap: (i, cmap(j))))
        out_shape.append(jax.ShapeDtypeStruct((s, w), dt))
    for r, w, bw, cmap in accs:
        out_specs.append(pl.BlockSpec((r, bw), lambda j, i, cmap=cmap: (0, cmap(j))))
        out_shape.append(jax.ShapeDtypeStruct((r, w), F32))
    n_in, n_out, n_acc = len(ins), len(outs), len(accs)

    def body(*refs):
        i = pl.program_id(1)
        vals = [r[...] for r in refs[:n_in]]
        o_vals, a_vals = fn(i, nrt, *vals)
        assert len(o_vals) == n_out and len(a_vals) == n_acc
        for r, v in zip(refs[n_in:n_in + n_out], o_vals):
            r[...] = v.astype(r.dtype)
        for r, v in zip(refs[n_in + n_out:], a_vals):
            @pl.when(i == 0)
            def _(r=r, v=v):
                r[...] = v.astype(F32)

            @pl.when(i > 0)
            def _(r=r, v=v):
                r[...] += v.astype(F32)

    res = pl.pallas_call(
        body, name=name, grid=(ncol, nrt), in_specs=in_specs, out_specs=out_specs, out_shape=out_shape,
        compiler_params=_params(("parallel", "arbitrary" if accs else "parallel")),
    )(*args)
    return res


def _c0(j):
    return 0


def _cj(j):
    return j


def _gmean(v, gs):
    w = v.shape[-1]
    tile = max(gs, LANES)
    r = lax.broadcasted_iota(jnp.int32, (tile, tile), 0) // gs
    c = lax.broadcasted_iota(jnp.int32, (tile, tile), 1) // gs
    g = jnp.where(r == c, 1.0 / gs, 0.0).astype(F32)
    parts = [jnp.dot(v[:, t * tile:(t + 1) * tile], g, precision=HI, preferred_element_type=F32) for t in range(w // tile)]
    return parts[0] if len(parts) == 1 else jnp.concatenate(parts, axis=1)


def _sum_rows(v):
    return jnp.sum(v, axis=0, keepdims=True)


def _rms(x, w, *, name):
    s, d = x.shape

    def fn(i, nrt, xv, wv):
        r = lax.rsqrt(jnp.mean(xv * xv, axis=-1, keepdims=True) + RMS_EPS)
        return (xv * r * wv,), ()

    (h,) = _rows(fn, name=name, s=s, tm=_tile(s, (512, 256, 128)), ins=[("row", x, d, _c0), ("full", w.reshape(1, d))],
                 outs=[(d, d, _c0, ACT_DTYPE)])
    return h


def _rms_bwd(x, w, dh, dx_in, *, name):
    s, d = x.shape

    def fn(i, nrt, xv, wv, dhv, dxi):
        r = lax.rsqrt(jnp.mean(xv * xv, axis=-1, keepdims=True) + RMS_EPS)
        xh = xv * r
        g = dhv * wv
        dx = dxi + r * (g - xh * jnp.mean(g * xh, axis=-1, keepdims=True))
        return (dx, dx), (_sum_rows(dhv * xh),)

    dx, dxb, dw = _rows(fn, name=name, s=s, tm=_tile(s, (512, 256, 128)),
                        ins=[("row", x, d, _c0), ("full", w.reshape(1, d)), ("row", dh, d, _c0), ("row", dx_in, d, _c0)],
                        outs=[(d, d, _c0, F32), (d, d, _c0, MM_DTYPE)], accs=[(1, d, d, _c0)])
    return dx, dxb, dw.reshape(d)


def _ffn_up(h, w_gu, layer, bw, *, name, rider=None):
    m, k = h.shape
    ns = w_gu.shape[3]
    f = N_CHIPS * ns // 2
    tm = _tile(m, (512, 256, 128))

    def w_spec(first):
        return pl.BlockSpec((None, None, k, bw), lambda t, i: ((first + t * bw) // ns, layer, 0, ((first + t * bw) % ns) // bw))

    def body(h_ref, wg_ref, wu_ref, gu_ref, a_ref):
        hv = h_ref[...].astype(MM_DTYPE)
        g = jnp.dot(hv, wg_ref[...].astype(MM_DTYPE), preferred_element_type=F32)
        u = jnp.dot(hv, wu_ref[...].astype(MM_DTYPE), preferred_element_type=F32)
        gu_ref[...] = jnp.concatenate([g, u], axis=1).astype(gu_ref.dtype)
        a_ref[...] = (g * _sigmoid(g) * u).astype(a_ref.dtype)

    grid = (f // bw, m // tm)
    r_in, r_out, r_shapes, r_scratch, r_args = _rider_parts(rider)
    outs = pl.pallas_call(
        _carry(body, rider, 0, 3, 2, grid), name=name, grid=grid,
        in_specs=[pl.BlockSpec((tm, k), lambda t, i: (i, 0)), w_spec(0), w_spec(f)] + r_in,
        out_specs=[pl.BlockSpec((tm, 2 * bw), lambda t, i: (i, t)), pl.BlockSpec((tm, bw), lambda t, i: (i, t))] + r_out,
        out_shape=[jax.ShapeDtypeStruct((m, 2 * f), ACT_DTYPE), jax.ShapeDtypeStruct((m, f), ACT_DTYPE)] + r_shapes,
        scratch_shapes=r_scratch, compiler_params=_params(("arbitrary", "arbitrary") if rider else ("parallel", "parallel")),
    )(h, w_gu, w_gu, *r_args)
    return outs[0], outs[1], list(outs[2:])


def _ffn_back(dx, w_down, gu, bw, *, name):
    m, d = dx.shape
    f = w_down.shape[0]
    tm = _tile(m, (512, 256, 128))

    rc = _tile(tm, (256, 128))

    def body(dx_ref, w_ref, gu_ref, o_ref):
        wv = w_ref[...].astype(MM_DTYPE)
        for r in range(tm // rc):
            rows = slice(r * rc, (r + 1) * rc)
            da = _nt(dx_ref[rows, :].astype(MM_DTYPE), wv)
            gv, uv = gu_ref[rows, :bw].astype(F32), gu_ref[rows, bw:].astype(F32)
            sg = _sigmoid(gv)
            o_ref[rows, :bw] = (da * uv * sg * (1.0 + gv * (1.0 - sg))).astype(o_ref.dtype)
            o_ref[rows, bw:] = (da * gv * sg).astype(o_ref.dtype)

    return pl.pallas_call(
        body, name=name, grid=(f // bw, m // tm),
        in_specs=[pl.BlockSpec((tm, d), lambda t, i: (i, 0)), pl.BlockSpec((bw, d), lambda t, i: (t, 0)), pl.BlockSpec((tm, 2 * bw), lambda t, i: (i, t))],
        out_specs=pl.BlockSpec((tm, 2 * bw), lambda t, i: (i, t)), out_shape=jax.ShapeDtypeStruct((m, 2 * f), ACT_DTYPE),
        compiler_params=_params(("parallel", "parallel")),
    )(dx, w_down, gu)


def _loss_head(x, tgt, *, name):
    s, d = x.shape

    def fn(i, nrt, xv, tv):
        diff = xv - tv
        part = 0.5 * jnp.sum(diff * diff) / d
        return (diff / d, diff / d), (jnp.full((1, LANES), part, F32),)

    dy, dyb, loss = _rows(fn, name=name, s=s, tm=_tile(s, (512, 256, 128)),
                          ins=[("row", x, d, _c0), ("row", tgt, d, _c0)],
                          outs=[(d, d, _c0, F32), (d, d, _c0, MM_DTYPE)], accs=[(1, LANES, LANES, _c0)])
    return loss[0, 0], dy, dyb


def _shift_down(ext, j, tm):
    src = pltpu.roll(ext, j, 0) if j else ext
    return src[SUBLANES:SUBLANES + tm]


def _shift_up(ext, j, tm):
    return ext[:tm] if j == 0 else pltpu.roll(ext, ext.shape[0] - j, 0)[:tm]


def _gconv_fwd(p, w, bw, *, name):
    s, d = p.shape[0], p.shape[1] // 3
    kw = w.shape[0]
    tm = _tile(s, (512, 256, 128))

    def fn(i, nrt, pv, pp, wv):
        pv, pp = pv.astype(F32), pp.astype(F32)
        cv = pv[:, bw:2 * bw] * pv[:, 2 * bw:]
        pcv = jnp.where(i == 0, 0.0, pp[:, bw:2 * bw] * pp[:, 2 * bw:])
        ext = jnp.concatenate([pcv, cv], axis=0)
        u = sum(wv[k:k + 1, :] * _shift_down(ext, kw - 1 - k, tm) for k in range(kw))
        return (pv[:, :bw] * u,), ()

    (o,) = _rows(fn, name=name, s=s, tm=tm, ncol=d // bw, ins=[("row", p, 3 * bw, _cj), ("prev", p, 3 * bw, _cj), ("col", w, bw, _cj)],
                 outs=[(d, bw, _cj, ACT_DTYPE)])
    return o


def _gconv_bwd(p, w, do, bw, *, name):
    s, d = do.shape
    kw = w.shape[0]
    tm = _tile(s, (512, 256, 128))

    def fn(i, nrt, pv, pp, pn, dov, ndo, wv):
        pv, pp, dov = pv.astype(F32), pp.astype(F32), dov.astype(F32)
        bv, cv_, vv = pv[:, :bw], pv[:, bw:2 * bw], pv[:, 2 * bw:]
        cv = cv_ * vv
        pcv = jnp.where(i == 0, 0.0, pp[:, bw:2 * bw] * pp[:, 2 * bw:])
        ext = jnp.concatenate([pcv, cv], axis=0)
        shifted = [_shift_down(ext, kw - 1 - k, tm) for k in range(kw)]
        u = sum(wv[k:k + 1, :] * shifted[k] for k in range(kw))
        db = dov * u
        du = dov * bv
        ndu = jnp.where(i == nrt - 1, 0.0, ndo.astype(F32) * pn[:, :bw].astype(F32))
        ext2 = jnp.concatenate([du, ndu], axis=0)
        dcv = sum(wv[k:k + 1, :] * _shift_up(ext2, kw - 1 - k, tm) for k in range(kw))
        dw = jnp.concatenate([_sum_rows(du * shifted[k]) for k in range(kw)], axis=0)
        return (jnp.concatenate([db, dcv * vv, dcv * cv_], axis=1),), (dw,)

    dp, dw = _rows(fn, name=name, s=s, tm=tm, ncol=d // bw,
                   ins=[("row", p, 3 * bw, _cj), ("prev", p, 3 * bw, _cj), ("next", p, 3 * bw, _cj), ("row", do, bw, _cj),
                        ("next", do, bw, _cj), ("col", w, bw, _cj)],
                   outs=[(3 * d, 3 * bw, _cj, ACT_DTYPE)], accs=[(kw, d, bw, _cj)])
    return dp, dw


def _sconv_fwd(x, w, bias, *, name):
    s, d = x.shape
    kw = w.shape[0]
    bw = _tile(d, (512, 256, 128))
    tm = _tile(s, (512, 256, 128))

    def fn(i, nrt, xv, px, wv, bsv):
        xv = xv.astype(F32)
        ext = jnp.concatenate([jnp.where(i == 0, 0.0, px.astype(F32)), xv], axis=0)
        pre = sum(wv[k:k + 1, :] * _shift_down(ext, kw - 1 - k, tm) for k in range(kw)) + bsv
        return (pre * _sigmoid(pre),), ()

    (o,) = _rows(fn, name=name, s=s, tm=tm, ncol=d // bw,
                 ins=[("row", x, bw, _cj), ("prev", x, bw, _cj), ("col", w, bw, _cj), ("col", bias.reshape(1, d), bw, _cj)],
                 outs=[(d, bw, _cj, ACT_DTYPE)])
    return o


def _sconv_bwd(x, w, bias, dact, *, name):
    s, d = x.shape
    kw = w.shape[0]
    bw = _tile(d, (512, 256, 128))
    tm = _tile(s, (512, 256, 128))

    def fn(i, nrt, xv, px, nx, dav, nda, wv, bsv):
        xv = xv.astype(F32)
        ext = jnp.concatenate([jnp.where(i == 0, 0.0, px.astype(F32)), xv, nx.astype(F32)], axis=0)
        rows_e = tm + SUBLANES
        pre_e = sum(wv[k:k + 1, :] * _shift_down(ext, kw - 1 - k, rows_e) for k in range(kw)) + bsv
        da_e = jnp.concatenate([dav.astype(F32), jnp.where(i == nrt - 1, 0.0, nda.astype(F32))], axis=0)
        sg = _sigmoid(pre_e)
        dpre_e = da_e * sg * (1.0 + pre_e * (1.0 - sg))
        dx = sum(wv[k:k + 1, :] * _shift_up(dpre_e, kw - 1 - k, tm) for k in range(kw))
        dpre = dpre_e[:tm]
        dw = jnp.concatenate([_sum_rows(dpre * _shift_down(ext, kw - 1 - k, tm)) for k in range(kw)], axis=0)
        return (dx,), (dw, _sum_rows(dpre))

    dx, dw, db = _rows(fn, name=name, s=s, tm=tm, ncol=d // bw,
                       ins=[("row", x, bw, _cj), ("prev", x, bw, _cj), ("next", x, bw, _cj), ("row", dact, bw, _cj),
                            ("next", dact, bw, _cj), ("col", w, bw, _cj), ("col", bias.reshape(1, d), bw, _cj)],
                       outs=[(d, bw, _cj, ACT_DTYPE)], accs=[(kw, d, bw, _cj), (1, d, bw, _cj)])
    return dx, dw, db.reshape(d)


def _tri(n, reverse):
    r = lax.broadcasted_iota(jnp.int32, (n, n), 0)
    c = lax.broadcasted_iota(jnp.int32, (n, n), 1)
    return jnp.where((c >= r) if reverse else (c <= r), 1.0, 0.0).astype(F32)


def _cumsum_rows(x, *, reverse, name):
    s, w = x.shape
    ch = _tile(s, (256, 128))
    n = s // ch

    def body(x_ref, o_ref, carry):
        i = pl.program_id(0)

        @pl.when(i == 0)
        def _():
            carry[...] = jnp.zeros_like(carry)

        out = jnp.dot(_tri(ch, reverse), x_ref[...], precision=HI, preferred_element_type=F32) + carry[...]
        o_ref[...] = out
        carry[...] = out[0:1, :] if reverse else out[ch - 1:ch, :]

    imap = (lambda i: (n - 1 - i, 0)) if reverse else (lambda i: (i, 0))
    return pl.pallas_call(
        body, name=name, grid=(n,), in_specs=[pl.BlockSpec((ch, w), imap)], out_specs=pl.BlockSpec((ch, w), imap),
        out_shape=jax.ShapeDtypeStruct((s, w), F32), scratch_shapes=[pltpu.VMEM((1, w), F32)],
        compiler_params=_params(("arbitrary",)),
    )(x)


def _fox_prep(q, k, f, gq, gk, bf, *, name):
    s, d = q.shape
    scale = HEAD ** -0.5

    def fn(i, nrt, qv, kv, fv, gqv, gkv, bfv):
        qv, kv = qv.astype(F32), kv.astype(F32)
        qn = qv * lax.rsqrt(_gmean(qv * qv, HEAD) + RMS_EPS) * gqv * scale
        kn = kv * lax.rsqrt(_gmean(kv * kv, HEAD) + RMS_EPS) * gkv
        z = fv + bfv
        logf = jnp.minimum(z, 0.0) - jnp.log(1.0 + jnp.exp(-jnp.abs(z)))
        return (qn, kn, logf), ()

    return _rows(fn, name=name, s=s, tm=_tile(s, (512, 256, 128)),
                 ins=[("row", q, d, _c0), ("row", k, d, _c0), ("row", f, LANES, _c0), ("full", gq), ("full", gk), ("full", bf)],
                 outs=[(d, d, _c0, ACT_DTYPE), (d, d, _c0, ACT_DTYPE), (LANES, LANES, _c0, F32)])


def _fox_prep_bwd(q, k, f, gq, gk, bf, dqs, dkn, dlogf, *, name):
    s, d = q.shape
    scale = HEAD ** -0.5

    def fn(i, nrt, qv, kv, fv, gqv, gkv, bfv, dqv, dkv, dlf):
        outs, accs = [], []
        for xv, gv, dv, sc in ((qv, gqv, dqv, scale), (kv, gkv, dkv, 1.0)):
            xv, dv = xv.astype(F32), dv.astype(F32) * sc
            r = lax.rsqrt(_gmean(xv * xv, HEAD) + RMS_EPS)
            xh = xv * r
            g = dv * gv
            outs.append(r * (g - xh * _gmean(g * xh, HEAD)))
            accs.append(_sum_rows(dv * xh))
        z = fv + bfv
        df = dlf * _sigmoid(-z)
        outs.append(df)
        accs.append(_sum_rows(df))
        return outs, accs

    return _rows(fn, name=name, s=s, tm=_tile(s, (512, 256, 128)),
                 ins=[("row", q, d, _c0), ("row", k, d, _c0), ("row", f, LANES, _c0), ("full", gq), ("full", gk), ("full", bf),
                      ("row", dqs, d, _c0), ("row", dkn, d, _c0), ("row", dlogf, LANES, _c0)],
                 outs=[(d, d, _c0, ACT_DTYPE), (d, d, _c0, ACT_DTYPE), (LANES, LANES, _c0, ACT_DTYPE)],
                 accs=[(1, d, d, _c0), (1, d, d, _c0), (1, LANES, LANES, _c0)])


def _head_masks(shape):
    lane = lax.broadcasted_iota(jnp.int32, shape, len(shape) - 1)
    return lane < HEAD, lane >= HEAD


def _pick_lane(blk, idx):
    lane = lax.broadcasted_iota(jnp.int32, blk.shape, 1)
    return jnp.sum(jnp.where(lane == idx, blk, 0.0), axis=1, keepdims=True)


def _pick_row(blk, idx):
    sub = lax.broadcasted_iota(jnp.int32, blk.shape, 0)
    return jnp.sum(jnp.where(sub == idx, blk, 0.0), axis=0, keepdims=True)


def _fox_aug(qs, kn, cum, *, name):
    s, d = qs.shape
    hp = d // LANES

    def fn(i, nrt, qv, kv, cv):
        lane = lax.broadcasted_iota(jnp.int32, (qv.shape[0], LANES), 1)
        outs = [[], [], [], []]
        for p in range(hp):
            qt, kt = qv[:, p * LANES:(p + 1) * LANES], kv[:, p * LANES:(p + 1) * LANES]
            for h in range(2):
                mine = (lane < HEAD) if h == 0 else (lane >= HEAD)
                a0 = HEAD if h == 0 else 0
                c = cv[:, 2 * p + h:2 * p + h + 1]
                hi = c.astype(ACT_DTYPE).astype(F32)
                mid = (c - hi).astype(ACT_DTYPE).astype(F32)
                lo = (c - hi - mid).astype(ACT_DTYPE).astype(F32)
                ones = jnp.where((lane >= a0) & (lane < a0 + 3), 1.0, 0.0)
                kx = jnp.where(lane == a0, -hi, jnp.where(lane == a0 + 1, -mid, jnp.where(lane == a0 + 2, -lo, 0.0)))
                outs[h].append(jnp.where(mine, qt.astype(F32), ones))
                outs[2 + h].append(jnp.where(mine, kt.astype(F32), kx))
        return [jnp.concatenate(o, axis=1) for o in outs], ()

    return _rows(fn, name=name, s=s, tm=_tile(s, (512, 256, 128)), ins=[("row", qs, d, _c0), ("row", kn, d, _c0), ("row", cum, LANES, _c0)],
                 outs=[(d, d, _c0, ACT_DTYPE)] * 4)


def _tri_tables(nq, by_key):
    import numpy as np
    pairs = [(qi, kj) for kj in range(nq) for qi in range(kj, nq)] if by_key else [(qi, kj) for qi in range(nq) for kj in range(qi + 1)]
    return jnp.asarray(np.array([p[0] for p in pairs], np.int32)), jnp.asarray(np.array([p[1] for p in pairs], np.int32))


def _nt(a, b):
    return lax.dot_general(a, b, (((1,), (1,)), ((), ())), preferred_element_type=F32)


def _tn(a, b):
    return lax.dot_general(a, b, (((0,), (0,)), ((), ())), preferred_element_type=F32)


ATTN_BLOCKS = (1024, 512, 256, 128)


def _fox_dd(do, o, *, name):
    s, d = do.shape

    def fn(i, nrt, dov, ov):
        return (_reduce_heads(dov.astype(F32) * ov.astype(F32), d // HEAD, HEAD),), ()

    (dd,) = _rows(fn, name=name, s=s, tm=_tile(s, (512, 256, 128)), ins=[("row", do, d, _c0), ("row", o, d, _c0)],
                  outs=[(LANES, LANES, _c0, F32)])
    return dd


def _pair_rows(a, nh):
    s = a.shape[0]
    t = a[:, :nh].T.reshape(nh // 2, 2, s)
    return jnp.pad(t, ((0, 0), (0, SUBLANES - 2), (0, 0)))


def _rows01(r0, r1):
    sub = lax.broadcasted_iota(jnp.int32, (SUBLANES, r0.shape[1]), 0)
    return jnp.where(sub == 0, r0, jnp.where(sub == 1, r1, 0.0))


def _rider_parts(rider):
    if rider is None:
        return [], [], [], [], []
    return [ANY] * len(rider.arrays), [ANY] * len(rider.out_shapes), rider.out_shapes, rider.scratch, rider.arrays


def _fox_fwd_t(q_aug, k_aug, v, *, name, rider=None):
    s, d = v.shape
    bq = _tile(s, ATTN_BLOCKS)
    nq = s // bq
    hp = d // LANES
    qtab, ktab = _tri_tables(nq, by_key=False)

    def body(qt, kt, q0_ref, q1_ref, k0_ref, k1_ref, v_ref, o_ref, lse_ref, m0, m1, l0, l1, acc0, acc1):
        t = pl.program_id(1)
        qi, kj = qt[t], kt[t]
        ms, ls, accs = (m0, m1), (l0, l1), (acc0, acc1)

        @pl.when(kj == 0)
        def _():
            for h in range(2):
                ms[h][...] = jnp.full_like(ms[h], -jnp.inf)
                ls[h][...] = jnp.zeros_like(ls[h])
                accs[h][...] = jnp.zeros_like(accs[h])

        def update(diagonal):
            v2 = v_ref[...]
            qk = ((q0_ref, k0_ref), (q1_ref, k1_ref))
            sts = [_nt(qk[h][1][...], qk[h][0][...]) for h in range(2)]
            if diagonal:
                sts = [_diag_mask_t(st) for st in sts]
            m_prev = [ms[h][...] for h in range(2)]
            m_new = [jnp.maximum(m_prev[h], jnp.max(sts[h], axis=0, keepdims=True)) for h in range(2)]
            ps = [jnp.exp(sts[h] - m_new[h]) for h in range(2)]
            alpha = [jnp.exp(m_prev[h] - m_new[h]) for h in range(2)]
            for h in range(2):
                ls[h][...] = alpha[h] * ls[h][...] + jnp.sum(ps[h], axis=0, keepdims=True)
                accs[h][...] = alpha[h] * accs[h][...] + _tn(v2, ps[h].astype(MM_DTYPE))
                ms[h][...] = m_new[h]

        @pl.when(kj < qi)
        def _():
            update(False)

        @pl.when(kj == qi)
        def _():
            update(True)
            row = lax.broadcasted_iota(jnp.int32, (LANES, bq), 0)
            ot = jnp.where(row < HEAD, acc0[...] / l0[...], acc1[...] / l1[...])
            o_ref[...] = ot.T.astype(o_ref.dtype)
            lse_ref[...] = _rows01(m0[...] + jnp.log(l0[...]), m1[...] + jnp.log(l1[...]))

    blk = (bq, LANES)
    qmap = lambda p_, t, qt, kt: (qt[t], p_)
    kmap = lambda p_, t, qt, kt: (kt[t], p_)
    grid = (hp, qtab.shape[0])
    r_in, r_out, r_shapes, r_scratch, r_args = _rider_parts(rider)
    grid_spec = pltpu.PrefetchScalarGridSpec(
        num_scalar_prefetch=2, grid=grid,
        in_specs=[pl.BlockSpec(blk, qmap), pl.BlockSpec(blk, qmap), pl.BlockSpec(blk, kmap), pl.BlockSpec(blk, kmap), pl.BlockSpec(blk, kmap)] + r_in,
        out_specs=[pl.BlockSpec(blk, qmap), pl.BlockSpec((None, SUBLANES, bq), lambda p_, t, qt, kt: (p_, 0, qt[t]))] + r_out,
        scratch_shapes=[pltpu.VMEM((1, bq), F32)] * 4 + [pltpu.VMEM((LANES, bq), F32)] * 2 + r_scratch)
    outs = pl.pallas_call(
        _carry(body, rider, 2, 5, 2, grid), name=name, grid_spec=grid_spec,
        out_shape=[jax.ShapeDtypeStruct((s, d), ACT_DTYPE), jax.ShapeDtypeStruct((hp, SUBLANES, s), F32)] + r_shapes,
        compiler_params=_params(("arbitrary", "arbitrary") if rider else ("parallel", "arbitrary")),
    )(qtab, ktab, q_aug[0], q_aug[1], k_aug[0], k_aug[1], v, *r_args)
    return outs[0], outs[1], list(outs[2:])


def _diag_mask_t(st):
    key = lax.broadcasted_iota(jnp.int32, st.shape, 0)
    qry = lax.broadcasted_iota(jnp.int32, st.shape, 1)
    return jnp.where(qry >= key, st, -jnp.inf)


def _fox_bwd_t(q_aug, k_aug, v, lse, dd, do, *, name, rider=None):
    s, d = v.shape
    bq = _tile(s, ATTN_BLOCKS)
    nq = s // bq
    hp = d // LANES
    blk = (bq, LANES)
    qtab, ktab = _tri_tables(nq, by_key=True)
    n_steps = qtab.shape[0]

    def body(qt, kt, q0_ref, q1_ref, k0_ref, k1_ref, v_ref, lse_ref, dd_ref, do_ref,
             dq_ref, dk_ref, dv_ref, dcol_ref, drow_ref, dq_sc, rs_sc, dk_sc, dv_sc, cs_sc):
        t = pl.program_id(1)
        qi, kj = qt[t], kt[t]

        @pl.when(t == 0)
        def _():
            dq_sc[...] = jnp.zeros_like(dq_sc)
            rs_sc[...] = jnp.zeros_like(rs_sc)

        def update(diagonal):
            v2, do2 = v_ref[...], do_ref[...]
            masks = _head_masks(blk)
            row = lax.broadcasted_iota(jnp.int32, (LANES, bq), 0)
            off = pl.multiple_of(qi * bq, bq)
            for h, (q_ref, k_ref) in enumerate(((q0_ref, k0_ref), (q1_ref, k1_ref))):
                st = _nt(k_ref[...], q_ref[...])
                if diagonal:
                    st = _diag_mask_t(st)
                p = jnp.exp(st - lse_ref[h:h + 1, :])
                dp = _nt(v2, jnp.where(masks[h], do2, jnp.zeros_like(do2)))
                ds = p * (dp - dd_ref[h:h + 1, :])
                dsb = ds.astype(MM_DTYPE)
                dv_sc[h] += jnp.dot(p.astype(MM_DTYPE), do2, preferred_element_type=F32)
                dk_sc[h] += jnp.dot(dsb, q_ref[...], preferred_element_type=F32)
                cs_sc[h] += jnp.sum(ds, axis=1, keepdims=True)
                mine = (row < HEAD) if h == 0 else (row >= HEAD)
                dq_sc[:, pl.ds(off, bq)] += jnp.where(mine, _tn(k_ref[...], dsb), 0.0)
                rs_sc[h:h + 1, pl.ds(off, bq)] += jnp.sum(ds, axis=0, keepdims=True)

        @pl.when(qi == kj)
        def _():
            dk_sc[...] = jnp.zeros_like(dk_sc)
            dv_sc[...] = jnp.zeros_like(dv_sc)
            cs_sc[...] = jnp.zeros_like(cs_sc)
            update(True)

        @pl.when(qi > kj)
        def _():
            update(False)

        @pl.when(qi == nq - 1)
        def _():
            lo, _hi = _head_masks(blk)
            dk_ref[...] = jnp.where(lo, dk_sc[0], dk_sc[1]).astype(dk_ref.dtype)
            dv_ref[...] = jnp.where(lo, dv_sc[0], dv_sc[1]).astype(dv_ref.dtype)
            dcol_ref[...] = jnp.where(lo, cs_sc[0], cs_sc[1])

        @pl.when(t == n_steps - 1)
        def _():
            for c in range(nq):
                dq_ref[c * bq:(c + 1) * bq, :] = dq_sc[:, c * bq:(c + 1) * bq].T.astype(dq_ref.dtype)
            drow_ref[...] = rs_sc[...]

    qmap = lambda p_, t, qt, kt: (qt[t], p_)
    kmap = lambda p_, t, qt, kt: (kt[t], p_)
    rmap = lambda p_, t, qt, kt: (p_, 0, qt[t])
    grid = (hp, n_steps)
    r_in, r_out, r_shapes, r_scratch, r_args = _rider_parts(rider)
    outs = pl.pallas_call(
        _carry(body, rider, 2, 8, 5, grid), name=name,
        grid_spec=pltpu.PrefetchScalarGridSpec(
            num_scalar_prefetch=2, grid=grid,
            in_specs=[pl.BlockSpec(blk, qmap), pl.BlockSpec(blk, qmap), pl.BlockSpec(blk, kmap), pl.BlockSpec(blk, kmap), pl.BlockSpec(blk, kmap),
                      pl.BlockSpec((None, SUBLANES, bq), rmap), pl.BlockSpec((None, SUBLANES, bq), rmap), pl.BlockSpec(blk, qmap)] + r_in,
            out_specs=[pl.BlockSpec((s, LANES), lambda p_, t, qt, kt: (0, p_)), pl.BlockSpec(blk, kmap), pl.BlockSpec(blk, kmap),
                       pl.BlockSpec(blk, kmap), pl.BlockSpec((None, SUBLANES, s), lambda p_, t, qt, kt: (p_, 0, 0))] + r_out,
            scratch_shapes=[pltpu.VMEM((LANES, s), F32), pltpu.VMEM((SUBLANES, s), F32), pltpu.VMEM((2, bq, LANES), F32),
                            pltpu.VMEM((2, bq, LANES), F32), pltpu.VMEM((2, bq, 1), F32)] + r_scratch),
        out_shape=[jax.ShapeDtypeStruct((s, d), ACT_DTYPE), jax.ShapeDtypeStruct((s, d), ACT_DTYPE), jax.ShapeDtypeStruct((s, d), ACT_DTYPE),
                   jax.ShapeDtypeStruct((s, d), F32), jax.ShapeDtypeStruct((hp, SUBLANES, s), F32)] + r_shapes,
        compiler_params=_params(("arbitrary", "arbitrary") if rider else ("parallel", "arbitrary")),
    )(qtab, ktab, q_aug[0], q_aug[1], k_aug[0], k_aug[1], v, lse, dd, do, *r_args)
    return list(outs[:5]), list(outs[5:])


def _expand_heads(v, nh, hd):
    r = lax.broadcasted_iota(jnp.int32, (LANES, nh * hd), 0)
    c = lax.broadcasted_iota(jnp.int32, (LANES, nh * hd), 1) // hd
    e = jnp.where(r == c, 1.0, 0.0).astype(F32)
    return jnp.dot(v, e, precision=HI, preferred_element_type=F32)


def _reduce_heads(v, nh, hd):
    r = lax.broadcasted_iota(jnp.int32, (nh * hd, LANES), 0) // hd
    c = lax.broadcasted_iota(jnp.int32, (nh * hd, LANES), 1)
    e = jnp.where(r == c, 1.0, 0.0).astype(F32)
    return jnp.dot(v, e, precision=HI, preferred_element_type=F32)


def _ssd_prep(dt_raw, dt_bias, a_log, nh, *, name):
    s = dt_raw.shape[0]

    def fn(i, nrt, dtr, bsv, alv):
        dt = _softplus(dtr + bsv)
        acum = jnp.dot(_tri(SSM_CHUNK, False), dt * (-jnp.exp(alv)), precision=HI, preferred_element_type=F32)
        return (dt, acum, _expand_heads(dt, nh, HEAD), _expand_heads(acum, nh, HEAD)), ()

    w = nh * HEAD
    return _rows(fn, name=name, s=s, tm=SSM_CHUNK, ins=[("row", dt_raw, LANES, _c0), ("full", dt_bias), ("full", a_log)],
                 outs=[(LANES, LANES, _c0, F32), (LANES, LANES, _c0, F32), (w, w, _c0, F32), (w, w, _c0, F32)])


def _ssd_prep_bwd(dt_raw, dt_bias, a_log, ddtx, dacx, nh, *, name):
    s = dt_raw.shape[0]

    def fn(i, nrt, dtr, bsv, alv, ddx, dax):
        z = dtr + bsv
        dt = _softplus(z)
        a = -jnp.exp(alv)
        dda = jnp.dot(_tri(SSM_CHUNK, True), _reduce_heads(dax, nh, HEAD), precision=HI, preferred_element_type=F32)
        ddt = _reduce_heads(ddx, nh, HEAD) + dda * a
        dz = ddt * _sigmoid(z)
        lane = lax.broadcasted_iota(jnp.int32, dz.shape, 1)
        dz = jnp.where(lane < nh, dz, 0.0)
        return (dz,), (_sum_rows(dz), _sum_rows(dda * dt) * a)

    w = nh * HEAD
    return _rows(fn, name=name, s=s, tm=SSM_CHUNK,
                 ins=[("row", dt_raw, LANES, _c0), ("full", dt_bias), ("full", a_log), ("row", ddtx, w, _c0), ("row", dacx, w, _c0)],
                 outs=[(LANES, LANES, _c0, ACT_DTYPE)], accs=[(1, LANES, LANES, _c0), (1, LANES, LANES, _c0)])


def _ssd_decay(ac_blk, act_blk, head):
    col = _pick_lane(ac_blk, head)
    row = _pick_row(act_blk, head)
    r = lax.broadcasted_iota(jnp.int32, (SSM_CHUNK, SSM_CHUNK), 0)
    c = lax.broadcasted_iota(jnp.int32, (SSM_CHUNK, SSM_CHUNK), 1)
    return jnp.exp(jnp.where(r >= c, col - row, -jnp.inf))


def _group_masks(shape, hpg):
    lane = lax.broadcasted_iota(jnp.int32, shape, len(shape) - 1) // HEAD
    return [lane == k for k in range(hpg)]


def _ssd_scan_fwd(xs, bm, cm, dtx, acx, acum, acum_t, d_x, *, name):
    s, di = xs.shape
    ng = bm.shape[1] // SSM_STATE
    gw = di // ng
    hpg = gw // HEAD
    nc = s // SSM_CHUNK
    L = SSM_CHUNK
    nh_pad = acum_t.shape[0]

    gp = next(n for n in (4, 2, 1) if ng % n == 0)
    N = SSM_STATE

    def body(x_ref, b_ref, c_ref, dt_ref, ax_ref, ac_ref, act_ref, d_ref, y_ref, st_ref, state):
        g2, c = pl.program_id(0), pl.program_id(1)

        @pl.when(c == 0)
        def _():
            state[...] = jnp.zeros_like(state)

        masks = _group_masks((L, gw), hpg)
        for gi in range(gp):
            g = g2 * gp + gi
            lanes, st_lanes = slice(gi * gw, (gi + 1) * gw), slice(gi * N, (gi + 1) * N)
            x4, bv, cv = x_ref[:, lanes].astype(F32), b_ref[:, st_lanes], c_ref[:, st_lanes]
            ax = ax_ref[:, lanes]
            tx = x4 * dt_ref[:, lanes]
            cb = lax.dot_general(cv, bv, (((1,), (1,)), ((), ())), preferred_element_type=F32)
            y = jnp.zeros((L, gw), F32)
            txb = tx.astype(MM_DTYPE)
            for k in range(hpg):
                wk = (cb * _ssd_decay(ac_ref[...], act_ref[...], g * hpg + k)).astype(MM_DTYPE)
                y = y + jnp.where(masks[k], jnp.dot(wk, txb, preferred_element_type=F32), 0.0)
            prev = state[gi]
            st_ref[gi] = prev
            y = y + jnp.dot(cv, prev.astype(MM_DTYPE), preferred_element_type=F32) * jnp.exp(ax)
            y = y + d_ref[:, lanes] * x4
            y_ref[:, lanes] = y.astype(y_ref.dtype)
            a_last = ax[L - 1:L, :]
            sx = (tx * jnp.exp(a_last - ax)).astype(MM_DTYPE)
            state[gi] = prev * jnp.exp(a_last) + lax.dot_general(bv, sx, (((0,), (0,)), ((), ())), preferred_element_type=F32)

    y, states = pl.pallas_call(
        body, name=name, grid=(ng // gp, nc),
        in_specs=[pl.BlockSpec((L, gp * gw), lambda g, c: (c, g)), pl.BlockSpec((L, gp * N), lambda g, c: (c, g)),
                  pl.BlockSpec((L, gp * N), lambda g, c: (c, g)), pl.BlockSpec((L, gp * gw), lambda g, c: (c, g)),
                  pl.BlockSpec((L, gp * gw), lambda g, c: (c, g)), pl.BlockSpec((L, LANES), lambda g, c: (c, 0)),
                  pl.BlockSpec((nh_pad, L), lambda g, c: (0, c)), pl.BlockSpec((1, gp * gw), lambda g, c: (0, g))],
        out_specs=[pl.BlockSpec((L, gp * gw), lambda g, c: (c, g)), pl.BlockSpec((gp, None, N, gw), lambda g, c: (g, c, 0, 0))],
        out_shape=[jax.ShapeDtypeStruct((s, di), ACT_DTYPE), jax.ShapeDtypeStruct((ng, nc, N, gw), F32)],
        scratch_shapes=[pltpu.VMEM((gp, N, gw), F32)],
        compiler_params=_params(("parallel", "arbitrary")),
    )(xs, bm, cm, dtx, acx, acum, acum_t, d_x)
    return y, states


def _ssd_scan_bwd(xs, bm, cm, dtx, acx, acum, acum_t, d_x, states, dy, *, name, rider=None):
    s, di = xs.shape
    ng = bm.shape[1] // SSM_STATE
    gw = di // ng
    hpg = gw // HEAD
    nc = s // SSM_CHUNK
    L = SSM_CHUNK
    nh_pad = acum_t.shape[0]

    gp = next(n for n in (4, 2, 1) if ng % n == 0)
    N = SSM_STATE

    def body(x_ref, b_ref, c_ref, dt_ref, ax_ref, ac_ref, act_ref, d_ref, st_ref, dy_ref,
             dx_ref, db_ref, dc_ref, ddt_ref, dax_ref, dd_ref, dstate):
        g2, cc = pl.program_id(0), pl.program_id(1)

        @pl.when(cc == 0)
        def _():
            dstate[...] = jnp.zeros_like(dstate)
            dd_ref[...] = jnp.zeros_like(dd_ref)

        for gi in range(gp):
            one_group(g2 * gp + gi, gi, slice(gi * gw, (gi + 1) * gw), slice(gi * N, (gi + 1) * N), x_ref, b_ref, c_ref, dt_ref, ax_ref, ac_ref,
                      act_ref, d_ref, st_ref, dy_ref, dx_ref, db_ref, dc_ref, ddt_ref, dax_ref, dd_ref, dstate)

    def one_group(g, gi, lanes, st_lanes, x_ref, b_ref, c_ref, dt_ref, ax_ref, ac_ref, act_ref, d_ref, st_ref, dy_ref,
                  dx_ref, db_ref, dc_ref, ddt_ref, dax_ref, dd_ref, dstate):
        x4, bv, cv = x_ref[:, lanes].astype(F32), b_ref[:, st_lanes], c_ref[:, st_lanes]
        tv, ax, dyv = dt_ref[:, lanes], ax_ref[:, lanes], dy_ref[:, lanes].astype(F32)
        prev, dn = st_ref[gi], dstate[gi]
        dnb = dn.astype(MM_DTYPE)
        masks = _group_masks((L, gw), hpg)
        tx = x4 * tv
        txb = tx.astype(MM_DTYPE)
        e_ax = jnp.exp(ax)
        a_last = ax[L - 1:L, :]
        e_last = jnp.exp(a_last)
        ed = jnp.exp(a_last - ax)

        dx = d_ref[:, lanes] * dyv
        dd_ref[:, lanes] += _sum_rows(dyv * x4)
        dye = (dyv * e_ax).astype(MM_DTYPE)
        yo = jnp.dot(cv, prev.astype(MM_DTYPE), preferred_element_type=F32) * e_ax
        dc = lax.dot_general(dye, prev.astype(MM_DTYPE), (((1,), (1,)), ((), ())), preferred_element_type=F32)
        dprev = lax.dot_general(cv, dye, (((0,), (0,)), ((), ())), preferred_element_type=F32)
        dax = dyv * yo
        sx = tx * ed
        dsx = jnp.dot(bv, dnb, preferred_element_type=F32)
        db = lax.dot_general(sx.astype(MM_DTYPE), dnb, (((1,), (1,)), ((), ())), preferred_element_type=F32)
        dtx_ = dsx * ed
        dsx_sx = dsx * sx
        dax = dax - dsx_sx
        dlast = _sum_rows(dsx_sx) + _sum_rows(dn * prev) * e_last
        dprev = dprev + dn * e_last
        cb = lax.dot_general(cv, bv, (((1,), (1,)), ((), ())), preferred_element_type=F32)
        dcb = jnp.zeros((L, L), F32)
        lane = lax.broadcasted_iota(jnp.int32, (L, gw), 1)
        for k in range(hpg):
            dec = _ssd_decay(ac_ref[...], act_ref[...], g * hpg + k)
            wk = (cb * dec).astype(MM_DTYPE)
            dyk = jnp.where(masks[k], dyv, 0.0).astype(MM_DTYPE)
            dtx_ = dtx_ + jnp.where(masks[k], lax.dot_general(wk, dyk, (((0,), (0,)), ((), ())), preferred_element_type=F32), 0.0)
            dwk = lax.dot_general(dyk, txb, (((1,), (1,)), ((), ())), preferred_element_type=F32)
            dcb = dcb + dwk * dec
            mk = dwk * cb * dec
            da_k = jnp.sum(mk, axis=1, keepdims=True) - jnp.sum(mk.T, axis=1, keepdims=True)
            dax = dax + jnp.where(lane == k * HEAD, da_k, 0.0)
        dcbb = dcb.astype(MM_DTYPE)
        dc = dc + jnp.dot(dcbb, bv, preferred_element_type=F32)
        db = db + lax.dot_general(dcbb, cv, (((0,), (0,)), ((), ())), preferred_element_type=F32)
        sub = lax.broadcasted_iota(jnp.int32, (L, gw), 0)
        dax = dax + jnp.where(sub == L - 1, dlast, 0.0)
        dx_ref[:, lanes] = (dx + dtx_ * tv).astype(dx_ref.dtype)
        ddt_ref[:, lanes] = dtx_ * x4
        dax_ref[:, lanes] = dax
        db_ref[:, st_lanes] = db.astype(db_ref.dtype)
        dc_ref[:, st_lanes] = dc.astype(dc_ref.dtype)
        dstate[gi] = dprev

    rev = lambda g, c: (nc - 1 - c, g)
    rev0 = lambda g, c: (nc - 1 - c, 0)
    grid = (ng // gp, nc)
    r_in, r_out, r_shapes, r_scratch, r_args = _rider_parts(rider)
    outs = pl.pallas_call(
        _carry(body, rider, 0, 10, 6, grid), name=name, grid=grid,
        in_specs=[pl.BlockSpec((L, gp * gw), rev), pl.BlockSpec((L, gp * N), rev), pl.BlockSpec((L, gp * N), rev),
                  pl.BlockSpec((L, gp * gw), rev), pl.BlockSpec((L, gp * gw), rev), pl.BlockSpec((L, LANES), rev0),
                  pl.BlockSpec((nh_pad, L), lambda g, c: (0, nc - 1 - c)), pl.BlockSpec((1, gp * gw), lambda g, c: (0, g)),
                  pl.BlockSpec((gp, None, N, gw), lambda g, c: (g, nc - 1 - c, 0, 0)), pl.BlockSpec((L, gp * gw), rev)] + r_in,
        out_specs=[pl.BlockSpec((L, gp * gw), rev), pl.BlockSpec((L, gp * N), rev), pl.BlockSpec((L, gp * N), rev),
                   pl.BlockSpec((L, gp * gw), rev), pl.BlockSpec((L, gp * gw), rev), pl.BlockSpec((1, gp * gw), lambda g, c: (0, g))] + r_out,
        out_shape=[jax.ShapeDtypeStruct((s, di), ACT_DTYPE), jax.ShapeDtypeStruct(bm.shape, ACT_DTYPE), jax.ShapeDtypeStruct(cm.shape, ACT_DTYPE),
                   jax.ShapeDtypeStruct((s, di), F32), jax.ShapeDtypeStruct((s, di), F32), jax.ShapeDtypeStruct((1, di), F32)] + r_shapes,
        scratch_shapes=[pltpu.VMEM((gp, N, gw), F32)] + r_scratch,
        compiler_params=_params(("arbitrary", "arbitrary") if rider else ("parallel", "arbitrary")),
    )(xs, bm, cm, dtx, acx, acum, acum_t, d_x, states, dy, *r_args)
    return list(outs[:6]), list(outs[6:])


def _ssd_gate(y, z, w, gs, *, name):
    s, d = y.shape

    def fn(i, nrt, yv, zv, wv):
        zv = zv.astype(F32)
        u = yv.astype(F32) * zv * _sigmoid(zv)
        return (u * lax.rsqrt(_gmean(u * u, gs) + RMS_EPS) * wv,), ()

    (o,) = _rows(fn, name=name, s=s, tm=_tile(s, (256, 128)), ins=[("row", y, d, _c0), ("row", z, d, _c0), ("full", w.reshape(1, d))],
                 outs=[(d, d, _c0, ACT_DTYPE)])
    return o


def _ssd_gate_bwd(y, z, w, do, gs, *, name):
    s, d = y.shape

    def fn(i, nrt, yv, zv, wv, dov):
        yv, zv, dov = yv.astype(F32), zv.astype(F32), dov.astype(F32)
        sg = _sigmoid(zv)
        sl = zv * sg
        u = yv * sl
        r = lax.rsqrt(_gmean(u * u, gs) + RMS_EPS)
        uh = u * r
        g = dov * wv
        du = r * (g - uh * _gmean(g * uh, gs))
        return (du * sl, du * yv * sg * (1.0 + zv * (1.0 - sg))), (_sum_rows(dov * uh),)

    dy, dz, dw = _rows(fn, name=name, s=s, tm=_tile(s, (256, 128)),
                       ins=[("row", y, d, _c0), ("row", z, d, _c0), ("full", w.reshape(1, d)), ("row", do, d, _c0)],
                       outs=[(d, d, _c0, ACT_DTYPE), (d, d, _c0, ACT_DTYPE)], accs=[(1, d, d, _c0)])
    return dy, dz, dw.reshape(d)


def _pad_lanes(w):
    return jnp.pad(w, ((0, 0), (0, LANES - w.shape[1])))


def _nt_sum(pairs, name):
    acc = None
    for a, b in pairs:
        acc = _mm(a, b, tb=True, add=acc, name=name)
    return acc


def _conv_mixer_fwd(h, w_in, layer, w_dw, tag):
    d, ns = h.shape[1], w_in.shape[3]
    inter = (3, _tile(math.gcd(d, ns), TILES))
    p = _mm(h, w_in, b_layer=layer, inter=inter, out_dtype=ACT_DTYPE, name=f"{tag}_in")
    return _gconv_fwd(p, w_dw, inter[1], name=f"{tag}_gate"), (h, w_in, layer, inter, p, w_dw)


def _conv_mixer_bwd(cache, do, tag):
    h, w_in, layer, inter, p, w_dw = cache
    dp, dw_dw = _gconv_bwd(p, w_dw, do, inter[1], name=f"{tag}_gate_bwd")
    dw_in = _mm(h, dp, ta=True, out_shard=w_in.shape[3], inter=inter, out_dtype=WIRE_DTYPE, name=f"{tag}_dw_in")
    dh = _mm(dp, w_in, tb=True, b_layer=layer, inter=inter, name=f"{tag}_dh")
    return dh, {"w_in": dw_in, "w_dw": dw_dw}


def _fox_mixer_fwd(h, w_in, b_f, q_gain, k_gain, tag, rider=None):
    d = h.shape[1]
    nh = d // HEAD
    ws = [w_in[:, k * d:(k + 1) * d] for k in range(3)] + [_pad_lanes(w_in[:, 3 * d:])]
    q, k, v = [_mm(h, w, out_dtype=ACT_DTYPE, name=f"{tag}_in") for w in ws[:3]]
    f = _mm(h, ws[3], name=f"{tag}_in_f")
    gq = jnp.tile(q_gain, nh).reshape(1, d)
    gk = jnp.tile(k_gain, nh).reshape(1, d)
    bf = _pad_lanes(b_f.reshape(1, nh))
    qs, kn, logf = _fox_prep(q, k, f, gq, gk, bf, name=f"{tag}_prep")
    cum = _cumsum_rows(logf, reverse=False, name=f"{tag}_cum")
    aug = _fox_aug(qs, kn, cum, name=f"{tag}_aug")
    q_aug, k_aug = aug[:2], aug[2:]
    o, lse, landed = _fox_fwd_t(q_aug, k_aug, v, name=f"{tag}_attn", rider=rider)
    return o, (h, ws, q, k, v, f, gq, gk, bf, q_aug, k_aug, o, lse), landed


def _fox_mixer_bwd(cache, do, tag, rider=None):
    h, ws, q, k, v, f, gq, gk, bf, q_aug, k_aug, o, lse = cache
    s, d = q.shape
    nh = d // HEAD
    dd = _pair_rows(_fox_dd(do, o, name=f"{tag}_attn_dd"), nh)
    (dqs, dkn, dv, dcol, drow), landed = _fox_bwd_t(q_aug, k_aug, v, lse, dd, do, name=f"{tag}_attn_bwd", rider=rider)
    dcum = _pad_lanes(drow[:, :2, :].reshape(nh, s).T - dcol[:, ::HEAD])
    dlogf = _cumsum_rows(dcum, reverse=True, name=f"{tag}_cum_bwd")
    dq, dk, df, dgq, dgk, dbf = _fox_prep_bwd(q, k, f, gq, gk, bf, dqs, dkn, dlogf, name=f"{tag}_prep_bwd")
    dps = (dq, dk, dv, df)
    dws = [_mm(h, dp, ta=True, name=f"{tag}_dw_in") for dp in dps]
    dw_in = jnp.concatenate(dws[:3] + [dws[3][:, :nh]], axis=1)
    dh = _nt_sum(list(zip(dps, ws)), f"{tag}_dh")
    return dh, {"w_in": dw_in, "b_f": dbf[0, :nh], "q_gain": dgq.reshape(nh, HEAD).sum(0), "k_gain": dgk.reshape(nh, HEAD).sum(0)}, landed


def _ssd_mixer_fwd(h, w_in, conv_w, conv_b, dt_bias, a_log, d_skip, norm_w, tag):
    di = norm_w.shape[0]
    nh = di // HEAD
    gn = (conv_w.shape[1] - di) // 2
    cuts = [0, di, 2 * di, 2 * di + gn, 2 * di + 2 * gn]
    ws = [w_in[:, cuts[k]:cuts[k + 1]] for k in range(4)] + [_pad_lanes(w_in[:, cuts[4]:])]
    z, xr, br, cr = [_mm(h, w, out_dtype=ACT_DTYPE, name=f"{tag}_in") for w in ws[:4]]
    dtr = _mm(h, ws[4], name=f"{tag}_in_dt")
    ccuts = [0, di, di + gn, di + 2 * gn]
    cws = [conv_w[:, ccuts[k]:ccuts[k + 1]] for k in range(3)]
    cbs = [conv_b[ccuts[k]:ccuts[k + 1]] for k in range(3)]
    xs, bm, cm = [_sconv_fwd(r, w, b, name=f"{tag}_conv") for r, w, b in zip((xr, br, cr), cws, cbs)]
    dtb = _pad_lanes(dt_bias.reshape(1, nh))
    alg = _pad_lanes(a_log.reshape(1, nh))
    _dt, acum, dtx, acx = _ssd_prep(dtr, dtb, alg, nh, name=f"{tag}_prep")
    acum_t = acum[:, :nh].T
    d_x = jnp.repeat(d_skip, HEAD).reshape(1, di)
    y, states = _ssd_scan_fwd(xs, bm, cm, dtx, acx, acum, acum_t, d_x, name=f"{tag}_scan")
    gs = di // (gn // SSM_STATE)
    o = _ssd_gate(y, z, norm_w, gs, name=f"{tag}_gate")
    return o, (h, ws, z, (xr, br, cr), dtr, cws, cbs, xs, bm, cm, dtb, alg, dtx, acx, acum, acum_t, d_x, states, y, norm_w, gs, nh)


def _ssd_mixer_bwd(cache, do, tag, rider=None):
    h, ws, z, raws, dtr, cws, cbs, xs, bm, cm, dtb, alg, dtx, acx, acum, acum_t, d_x, states, y, norm_w, gs, nh = cache
    dy, dz, dnorm = _ssd_gate_bwd(y, z, norm_w, do, gs, name=f"{tag}_gate_bwd")
    (dxs, dbm, dcm, ddtx, dacx, dd_x), landed = _ssd_scan_bwd(xs, bm, cm, dtx, acx, acum, acum_t, d_x, states, dy, name=f"{tag}_scan_bwd",
                                                             rider=rider)
    ddtr, ddtb, dalg = _ssd_prep_bwd(dtr, dtb, alg, ddtx, dacx, nh, name=f"{tag}_prep_bwd")
    conv = [_sconv_bwd(r, w, b, da, name=f"{tag}_conv_bwd") for r, w, b, da in zip(raws, cws, cbs, (dxs, dbm, dcm))]
    dps = (dz, conv[0][0], conv[1][0], conv[2][0], ddtr)
    dws = [_mm(h, dp, ta=True, name=f"{tag}_dw_in") for dp in dps]
    dw_in = jnp.concatenate(dws[:4] + [dws[4][:, :nh]], axis=1)
    dh = _nt_sum(list(zip(dps, ws)), f"{tag}_dh")
    return dh, {"w_in": dw_in, "conv_w": jnp.concatenate([c[1] for c in conv], axis=1), "conv_b": jnp.concatenate([c[2] for c in conv]),
                "dt_bias": ddtb[0, :nh], "a_log": dalg[0, :nh], "d": dd_x.reshape(nh, HEAD).sum(1), "norm_w": dnorm}, landed


def _rows_natural(cm, layer):
    return cm[:, layer].reshape(-1, cm.shape[3])


def _cols_natural(cm, layer):
    return jnp.moveaxis(cm[:, layer], 0, 1).reshape(cm.shape[2], -1)


def _cols_chip_major(g):
    return jnp.moveaxis(g.reshape(g.shape[0], N_CHIPS, -1), 1, 0).astype(WIRE_DTYPE)


MIXERS = ("conv", "fox", "ssd")


def _model_layer(name, l):
    return l if name.startswith("ffn") else 3 * l + MIXERS.index(name.split("_")[0])


def _piece(pieces, layer):
    for arr, start in pieces:
        if start <= layer < start + arr.shape[1]:
            return arr, layer - start
    raise KeyError(layer)


def _with_own(landed, shards, chip):
    return [lax.dynamic_update_slice(g, w[None], (chip, 0, 0, 0)) for g, w in zip(landed, shards)]


def _reduce_begin(gs, place):
    from_sibling = _swap_halves_list(gs, name="reduce_halves")
    return [_add_half(g, r, place, name="reduce_add_sibling") for g, r in zip(gs, from_sibling)]


def _reduce_end(by_chip, chip_sums, place):
    reds = [_sum_chips(b, s, place, name="reduce_sum_chips") for b, s in zip(by_chip, chip_sums)]
    return _join_halves_list(reds, name="reduce_share")


def _local_step(x, tgt, fw, cm, late, place):
    depth = fw["mix_norm"].shape[0]
    chip = place[0]
    cm = {n: list(p) for n, p in cm.items()}
    late = dict(late)

    def gather_rider(group):
        return _gather_ici_rider([shards for shards, _ in group.values()]) if group else None

    def land(group, landed):
        if group:
            full = _with_own(_forward_halves_list(landed, name="gather_forward"), [shards for shards, _ in group.values()], chip)
            for (n, (_, start)), arr in zip(group.items(), full):
                cm[n].append((arr, start))

    layers = []
    xin = x
    for i in range(depth):
        kind, j = i % 3, i // 3
        tag = f"l{i}"
        h = _rms(xin, fw["mix_norm"][i], name=f"{tag}_norm1")
        if kind == 0:
            w_in, jl = _piece(cm["conv_w_in"], j)
            o, mc = _conv_mixer_fwd(h, w_in, jl, fw["conv_w_dw"][j], tag + "_conv")
        elif kind == 1:
            group = late.pop(("attention", i), None)
            o, mc, landed = _fox_mixer_fwd(h, _cols_natural(*_piece(cm["fox_w_in"], j)), fw["fox_b_f"][j], fw["fox_q_gain"][j],
                                           fw["fox_k_gain"][j], tag + "_fox", rider=gather_rider(group))
            land(group, landed)
        else:
            o, mc = _ssd_mixer_fwd(h, _cols_natural(*_piece(cm["ssd_w_in"], j)), fw["ssd_conv_w"][j], fw["ssd_conv_b"][j], fw["ssd_dt_bias"][j],
                                   fw["ssd_a_log"][j], fw["ssd_d"][j], fw["ssd_norm_w"][j], tag + "_ssd")
        w_out = _rows_natural(*_piece(cm[MIXERS[kind] + "_w_out"], j))
        x1 = _mm(o, w_out, add=xin, name=f"{tag}_mix_out")
        h2 = _rms(x1, fw["ffn_norm"][i], name=f"{tag}_norm2")
        w_gu, il = _piece(cm["ffn_w_gu"], i)
        w_down = _rows_natural(*_piece(cm["ffn_w_down"], i))
        inter = (2, _tile(math.gcd(w_down.shape[0], w_gu.shape[3]), TILES))
        group = late.pop(("ffn", i), None)
        gu, a, landed = _ffn_up(h2, w_gu, il, inter[1], name=f"{tag}_ffn_gu", rider=gather_rider(group))
        land(group, landed)
        layers.append((xin, o, mc, w_out, x1, h2, w_gu, il, w_down, inter, gu, a))
        xin = _mm(a, w_down, add=x1, name=f"{tag}_ffn_down")
    loss, dx, dxb = _loss_head(xin, tgt, name="loss_head")

    small = {k: [None] * v.shape[0] for k, v in fw.items()}
    reds = {}
    riding = None
    at_end = []
    for i in reversed(range(depth)):
        kind, j = i % 3, i // 3
        tag = f"l{i}"
        xin, o, mc, w_out, x1, h2, w_gu, il, w_down, inter, gu, a = layers[i]
        mine = [(("ffn_w_down", i), _mm(a, dxb, ta=True, out_dtype=WIRE_DTYPE, name=f"{tag}_dw_down").reshape(N_CHIPS, -1, w_down.shape[1]))]
        dgu = _ffn_back(dxb, w_down, gu, inter[1], name=f"{tag}_ffn_back")
        mine.append((("ffn_w_gu", i), _mm(h2, dgu, ta=True, out_shard=w_gu.shape[3], inter=inter, out_dtype=WIRE_DTYPE, name=f"{tag}_dw_gu")))
        rider = riding[2] if riding and kind == 0 else None
        dh2 = _mm(dgu, w_gu, tb=True, b_layer=il, inter=inter, name=f"{tag}_dh2", rider=rider)
        if rider is not None:
            dh2, landed = dh2
            reds.update(zip(riding[0], _reduce_end(landed, riding[1], place)))
            riding = None
        dx1, dx1b, small["ffn_norm"][i] = _rms_bwd(x1, fw["ffn_norm"][i], dh2, dx, name=f"{tag}_norm2_bwd")
        mine.append(((MIXERS[kind] + "_w_out", j),
                     _mm(o, dx1b, ta=True, out_dtype=WIRE_DTYPE, name=f"{tag}_dw_out").reshape(N_CHIPS, -1, w_out.shape[1])))
        do = _mm(dx1b, w_out, tb=True, out_dtype=ACT_DTYPE, name=f"{tag}_do")
        rider = riding[2] if riding else None
        if kind == 0:
            dh, mg = _conv_mixer_bwd(mc, do, tag + "_conv")
        elif kind == 1:
            dh, mg, landed = _fox_mixer_bwd(mc, do, tag + "_fox", rider=rider)
        else:
            dh, mg, landed = _ssd_mixer_bwd(mc, do, tag + "_ssd", rider=rider)
        if rider is not None:
            reds.update(zip(riding[0], _reduce_end(landed, riding[1], place)))
            riding = None
        for k, v in mg.items():
            if k == "w_in":
                mine.append(((f"{MIXERS[kind]}_w_in", j), v if kind == 0 else _cols_chip_major(v)))
            else:
                small[f"{MIXERS[kind]}_{k}"][j] = v
        dx, dxb, small["mix_norm"][i] = _rms_bwd(xin, fw["mix_norm"][i], dh, dx1, name=f"{tag}_norm1_bwd")
        if i > 0:
            chip_sums = _reduce_begin([g for _, g in mine], place)
            riding = ([k for k, _ in mine], chip_sums, _scatter_rider(chip_sums))
        else:
            at_end += mine
    assert riding is None
    chip_sums = _reduce_begin([g for _, g in at_end], place)
    by_chip = _run_rider(_scatter_rider(chip_sums), name="reduce_chips")
    reds.update(zip([k for k, _ in at_end], _reduce_end(by_chip, chip_sums, place)))
    return loss, dx, {k: jnp.stack(v) for k, v in small.items()}, reds


ANY = pl.BlockSpec(memory_space=pl.ANY)
VMEM_SPEC = pl.BlockSpec(memory_space=pltpu.VMEM)


def _place():
    return lax.axis_index("x"), lax.axis_index("y"), lax.axis_index("c")


def _remote(src, dst, send_sems, recv_sems, k, to):
    return pltpu.make_async_remote_copy(src_ref=src, dst_ref=dst, send_sem=send_sems.at[k], recv_sem=recv_sems.at[k],
                                        device_id=to, device_id_type=MESH)


def _half_of(ref, h, shape):
    layers, rows, _ = shape
    if layers % 2 == 0:
        return ref.at[pl.ds(h * (layers // 2), layers // 2)]
    return ref.at[:, pl.ds(pl.multiple_of(h * (rows // 2), 16), rows // 2)]


def _row_half(ref, h, rows):
    return ref.at[:, pl.ds(pl.multiple_of(h * (rows // 2), 16), rows // 2)]


def _gather_list(ws, *, name):
    n = len(ws)

    def body(*refs):
        w_refs, o_refs, send_sems, recv_sems = refs[:n], refs[n:2 * n], refs[2 * n], refs[2 * n + 1]
        x, y, c = _place()
        me, sibling, m = (x, y, c), (x, y, 1 - c), 2 * x + y
        chips = [(1 - x, y), (x, 1 - y), (1 - x, 1 - y)]
        first, passed = [], []
        for p, (w_ref, o_ref) in enumerate(zip(w_refs, o_refs)):
            for j, (px, py) in enumerate(chips):
                cp = _remote(_half_of(w_ref, c, ws[p].shape), _half_of(o_ref.at[m], c, ws[p].shape), send_sems, recv_sems, 6 * p + j, (px, py, c))
                cp.start()
                first.append(cp)
        for p, o_ref in enumerate(o_refs):
            for j, (px, py) in enumerate(chips):
                blk = _half_of(o_ref.at[2 * px + py], c, ws[p].shape)
                _remote(blk, blk, send_sems, recv_sems, 6 * p + j, me).wait_recv()
                fwd = _remote(blk, blk, send_sems, recv_sems, 6 * p + 3 + j, sibling)
                fwd.start()
                passed.append(fwd)
        for p, o_ref in enumerate(o_refs):
            for j, (px, py) in enumerate(chips):
                blk = _half_of(o_ref.at[2 * px + py], 1 - c, ws[p].shape)
                _remote(blk, blk, send_sems, recv_sems, 6 * p + 3 + j, me).wait_recv()
        for cp in first + passed:
            cp.wait_send()

    return pl.pallas_call(
        body, name=name, in_specs=[ANY] * n, out_specs=[ANY] * n,
        out_shape=[jax.ShapeDtypeStruct((N_CHIPS,) + w.shape, w.dtype) for w in ws],
        scratch_shapes=[pltpu.SemaphoreType.DMA((6 * n,)), pltpu.SemaphoreType.DMA((6 * n,))],
    )(*ws)


class _Rider:
    def __init__(self, arrays, out_shapes, n_sems, start, finish):
        self.arrays, self.out_shapes, self.n_sems, self.start, self.finish = list(arrays), list(out_shapes), n_sems, start, finish

    @property
    def scratch(self):
        return [pltpu.SemaphoreType.DMA((self.n_sems,)), pltpu.SemaphoreType.DMA((self.n_sems,))]


def _carry(body, rider, n_prefetch, n_in, n_out, grid):
    if rider is None:
        return body
    ri, ro = len(rider.arrays), len(rider.out_shapes)

    def wrapped(*refs):
        pre, rest = refs[:n_prefetch], refs[n_prefetch:]
        ins, r_in = rest[:n_in], rest[n_in:n_in + ri]
        outs, r_out = rest[n_in + ri:n_in + ri + n_out], rest[n_in + ri + n_out:n_in + ri + n_out + ro]
        scratch = rest[n_in + ri + n_out + ro:]
        first = functools.reduce(jnp.logical_and, [pl.program_id(a) == 0 for a in range(len(grid))])
        last = functools.reduce(jnp.logical_and, [pl.program_id(a) == g - 1 for a, g in enumerate(grid)])

        @pl.when(first)
        def _():
            rider.start(r_in, r_out, scratch[-2], scratch[-1])

        body(*pre, *ins, *outs, *scratch[:-2])

        @pl.when(last)
        def _():
            rider.finish(r_in, r_out, scratch[-2], scratch[-1])

    return wrapped


def _run_rider(rider, *, name):
    n = len(rider.arrays)

    def body(*refs):
        rider.start(refs[:n], refs[n:2 * n], refs[-2], refs[-1])
        rider.finish(refs[:n], refs[n:2 * n], refs[-2], refs[-1])

    return pl.pallas_call(body, name=name, in_specs=[ANY] * n, out_specs=[ANY] * len(rider.out_shapes), out_shape=rider.out_shapes,
                          scratch_shapes=rider.scratch)(*rider.arrays)


def _chips_of(x, y):
    return [(1 - x, y), (x, 1 - y), (1 - x, 1 - y)]


def _gather_ici_rider(ws):
    def copies(w_refs, o_refs, send_sems, recv_sems):
        x, y, c = _place()
        m = 2 * x + y
        return [_remote(_half_of(w_ref, c, ws[p].shape), _half_of(o_ref.at[m], c, ws[p].shape), send_sems, recv_sems, 3 * p + j, (px, py, c))
                for p, (w_ref, o_ref) in enumerate(zip(w_refs, o_refs)) for j, (px, py) in enumerate(_chips_of(x, y))]

    def start(w_refs, o_refs, send_sems, recv_sems):
        for cp in copies(w_refs, o_refs, send_sems, recv_sems):
            cp.start()

    def finish(w_refs, o_refs, send_sems, recv_sems):
        x, y, c = _place()
        for p, o_ref in enumerate(o_refs):
            for j, (px, py) in enumerate(_chips_of(x, y)):
                blk = _half_of(o_ref.at[2 * px + py], c, ws[p].shape)
                _remote(blk, blk, send_sems, recv_sems, 3 * p + j, (x, y, c)).wait_recv()
        for cp in copies(w_refs, o_refs, send_sems, recv_sems):
            cp.wait_send()

    return _Rider(ws, [jax.ShapeDtypeStruct((N_CHIPS,) + w.shape, w.dtype) for w in ws], 3 * len(ws), start, finish)


def _forward_halves_list(gathered, *, name):
    n = len(gathered)
    shapes = [g.shape[1:] for g in gathered]

    def body(*refs):
        o_refs, send_sems, recv_sems = refs[n:2 * n], refs[2 * n], refs[2 * n + 1]
        x, y, c = _place()
        cps = []
        for p, o_ref in enumerate(o_refs):
            for j, (px, py) in enumerate(_chips_of(x, y)):
                blk = _half_of(o_ref.at[2 * px + py], c, shapes[p])
                cp = _remote(blk, blk, send_sems, recv_sems, 3 * p + j, (x, y, 1 - c))
                cp.start()
                cps.append(cp)
        for p, o_ref in enumerate(o_refs):
            for j, (px, py) in enumerate(_chips_of(x, y)):
                blk = _half_of(o_ref.at[2 * px + py], 1 - c, shapes[p])
                _remote(blk, blk, send_sems, recv_sems, 3 * p + j, (x, y, c)).wait_recv()
        for cp in cps:
            cp.wait_send()

    return pl.pallas_call(
        body, name=name, in_specs=[ANY] * n, out_specs=[ANY] * n, out_shape=[jax.ShapeDtypeStruct(g.shape, g.dtype) for g in gathered],
        input_output_aliases={p: p for p in range(n)},
        scratch_shapes=[pltpu.SemaphoreType.DMA((3 * n,)), pltpu.SemaphoreType.DMA((3 * n,))],
    )(*gathered)


def _scatter_rider(sums):
    def copies(a_refs, o_refs, send_sems, recv_sems):
        x, y, c = _place()
        m = 2 * x + y
        return [_remote(a_ref.at[2 * px + py], o_ref.at[m], send_sems, recv_sems, 3 * p + j, (px, py, c))
                for p, (a_ref, o_ref) in enumerate(zip(a_refs, o_refs)) for j, (px, py) in enumerate(_chips_of(x, y))]

    def start(a_refs, o_refs, send_sems, recv_sems):
        for cp in copies(a_refs, o_refs, send_sems, recv_sems):
            cp.start()

    def finish(a_refs, o_refs, send_sems, recv_sems):
        x, y, c = _place()
        for p, o_ref in enumerate(o_refs):
            for j, (px, py) in enumerate(_chips_of(x, y)):
                blk = o_ref.at[2 * px + py]
                _remote(blk, blk, send_sems, recv_sems, 3 * p + j, (x, y, c)).wait_recv()
        for cp in copies(a_refs, o_refs, send_sems, recv_sems):
            cp.wait_send()

    return _Rider(sums, [jax.ShapeDtypeStruct(a.shape, a.dtype) for a in sums], 3 * len(sums), start, finish)


def _gather_small(v, *, name):
    r, w = v.shape

    def body(v_ref, o_ref, send_sems, recv_sems):
        x, y, c = _place()
        m = 2 * x + y
        chips = [(1 - x, y), (x, 1 - y), (1 - x, 1 - y)]
        o_ref[m] = v_ref[...]
        sends = [_remote(v_ref, o_ref.at[m], send_sems, recv_sems, j, (px, py, c)) for j, (px, py) in enumerate(chips)]
        for cp in sends:
            cp.start()
        for j, (px, py) in enumerate(chips):
            blk = o_ref.at[2 * px + py]
            _remote(blk, blk, send_sems, recv_sems, j, (x, y, c)).wait_recv()
        for cp in sends:
            cp.wait_send()

    return pl.pallas_call(
        body, name=name, in_specs=[VMEM_SPEC], out_specs=VMEM_SPEC, out_shape=jax.ShapeDtypeStruct((4, r, w), v.dtype),
        scratch_shapes=[pltpu.SemaphoreType.DMA((3,)), pltpu.SemaphoreType.DMA((3,))],
    )(v)


def _swap_halves_list(gs, *, name):
    n = len(gs)

    def body(*refs):
        g_refs, o_refs, send_sems, recv_sems = refs[:n], refs[n:2 * n], refs[2 * n], refs[2 * n + 1]
        x, y, c = _place()
        cps = [_remote(_row_half(g_ref, 1 - c, gs[p].shape[1]), o_ref, send_sems, recv_sems, p, (x, y, 1 - c))
               for p, (g_ref, o_ref) in enumerate(zip(g_refs, o_refs))]
        for cp in cps:
            cp.start()
        for cp in cps:
            cp.wait()

    return pl.pallas_call(
        body, name=name, in_specs=[ANY] * n, out_specs=[ANY] * n,
        out_shape=[jax.ShapeDtypeStruct((g.shape[0], g.shape[1] // 2, g.shape[2]), g.dtype) for g in gs],
        scratch_shapes=[pltpu.SemaphoreType.DMA((n,)), pltpu.SemaphoreType.DMA((n,))],
    )(*gs)


def _join_halves_list(reds, *, name):
    n = len(reds)

    def body(*refs):
        o_refs, send_sems, recv_sems = refs[n:2 * n], refs[2 * n], refs[2 * n + 1]
        x, y, c = _place()
        cps = []
        for p, o_ref in enumerate(o_refs):
            rh = reds[p].shape[0] // 2
            mine = o_ref.at[pl.ds(pl.multiple_of(c * rh, SUBLANES), rh)]
            cp = _remote(mine, mine, send_sems, recv_sems, p, (x, y, 1 - c))
            cp.start()
            cps.append(cp)
        for p, o_ref in enumerate(o_refs):
            rh = reds[p].shape[0] // 2
            other = o_ref.at[pl.ds(pl.multiple_of((1 - c) * rh, SUBLANES), rh)]
            _remote(other, other, send_sems, recv_sems, p, (x, y, c)).wait_recv()
        for cp in cps:
            cp.wait_send()

    return pl.pallas_call(
        body, name=name, in_specs=[ANY] * n, out_specs=[ANY] * n, out_shape=[jax.ShapeDtypeStruct(r.shape, r.dtype) for r in reds],
        input_output_aliases={p: p for p in range(n)},
        scratch_shapes=[pltpu.SemaphoreType.DMA((n,)), pltpu.SemaphoreType.DMA((n,))],
    )(*reds)


def _allreduce_small(v, *, name):
    r, w = v.shape

    def body(v_ref, o_ref, slots, send_sems, recv_sems):
        x, y, c = _place()
        me = 4 * x + 2 * y + c
        slots[me] = v_ref[...]
        peers = [((1 - x) if k & 4 else x, (1 - y) if k & 2 else y, (1 - c) if k & 1 else c) for k in range(1, 8)]
        sends = [_remote(v_ref, slots.at[me], send_sems, recv_sems, k, p) for k, p in enumerate(peers)]
        for cp in sends:
            cp.start()
        for k, (px, py, pc) in enumerate(peers):
            blk = slots.at[4 * px + 2 * py + pc]
            _remote(blk, blk, send_sems, recv_sems, k, (x, y, c)).wait_recv()
        for cp in sends:
            cp.wait_send()
        acc = slots[0]
        for k in range(1, 8):
            acc = acc + slots[k]
        o_ref[...] = acc

    return pl.pallas_call(
        body, name=name, in_specs=[VMEM_SPEC], out_specs=VMEM_SPEC, out_shape=jax.ShapeDtypeStruct(v.shape, v.dtype),
        scratch_shapes=[pltpu.VMEM((8, r, w), F32), pltpu.SemaphoreType.DMA((7,)), pltpu.SemaphoreType.DMA((7,))],
    )(v)


def _row_tile(r):
    return r if r <= 512 else _tile(r, (512, 256, 128, 64, 32, 16))


def _add_half(g, recv, place, *, name):
    n, r, w = g.shape
    tm = _row_tile(r // 2)
    nb = (r // 2) // tm

    def body(place_ref, g_ref, r_ref, o_ref):
        o_ref[...] = (g_ref[...].astype(F32) + r_ref[...].astype(F32)).astype(o_ref.dtype)

    return pl.pallas_call(
        body, name=name,
        grid_spec=pltpu.PrefetchScalarGridSpec(
            num_scalar_prefetch=1, grid=(n, nb),
            in_specs=[pl.BlockSpec((None, tm, w), lambda k, i, p: (k, p[1] * nb + i, 0)), pl.BlockSpec((None, tm, w), lambda k, i, p: (k, i, 0))],
            out_specs=pl.BlockSpec((None, tm, w), lambda k, i, p: (k, i, 0))),
        out_shape=jax.ShapeDtypeStruct(recv.shape, g.dtype), compiler_params=_params(("parallel", "parallel")),
    )(place, g, recv)


def _sum_chips(recv, own, place, *, name):
    n, r, w = recv.shape
    tm = _row_tile(r)
    nb = r // tm

    def body(place_ref, *refs):
        own_ref, o_ref = refs[n], refs[n + 1]
        acc = None
        for k in range(n):
            term = jnp.where(place_ref[0] == k, own_ref[...], refs[k][...]).astype(F32)
            acc = term if acc is None else acc + term
        o_ref[...] = acc

    recv_specs = [pl.BlockSpec((None, tm, w), lambda i, p, k=k: (jnp.where(p[0] == k, (k + 1) % n, k), i, 0)) for k in range(n)]
    return pl.pallas_call(
        body, name=name,
        grid_spec=pltpu.PrefetchScalarGridSpec(
            num_scalar_prefetch=1, grid=(nb,),
            in_specs=recv_specs + [pl.BlockSpec((None, tm, w), lambda i, p: (p[0], i, 0))],
            out_specs=pl.BlockSpec((tm, w), lambda i, p: (p[1] * nb + i, 0))),
        out_shape=jax.ShapeDtypeStruct((2 * r, w), F32), compiler_params=_params(("parallel",)),
    )(place, *([recv] * n), own)


def _adamw(w, g, m, v, *, name):
    shape = w.shape
    cols = shape[-1]
    rows = math.prod(shape[:-1])
    tm = _tile(rows, (256, 128, 64, 32, 16, 8))
    c1 = 1.0 - ADAM_B1 ** ADAM_STEP
    c2 = 1.0 - ADAM_B2 ** ADAM_STEP

    def fn(i, nrt, wv, gv, mv, vv):
        mn = ADAM_B1 * mv + (1.0 - ADAM_B1) * gv
        vn = ADAM_B2 * vv + (1.0 - ADAM_B2) * (gv * gv)
        delta = -ADAM_LR * ((mn / c1) / (jnp.sqrt(vn / c2) + ADAM_EPS) + ADAM_WD * wv)
        return (delta, mn, vn), ()

    outs = _rows(fn, name=name, s=rows, tm=tm, ins=[("row", t.reshape(rows, cols), cols, _c0) for t in (w, g, m, v)],
                 outs=[(cols, cols, _c0, F32)] * 3)
    return [o.reshape(shape) for o in outs]


WEIGHTS = ["mix_norm", "ffn_norm", "ffn_w_gu", "ffn_w_down", "conv_w_in", "conv_w_dw", "conv_w_out", "fox_w_in", "fox_b_f", "fox_q_gain",
           "fox_k_gain", "fox_w_out", "ssd_w_in", "ssd_conv_w", "ssd_conv_b", "ssd_dt_bias", "ssd_a_log", "ssd_d", "ssd_norm_w", "ssd_w_out"]
SHARD_AXIS = {"ffn_w_gu": 2, "ffn_w_down": 1, "conv_w_in": 2, "conv_w_dw": 2, "conv_w_out": 1, "fox_w_in": 2, "fox_w_out": 1, "ssd_w_in": 2,
              "ssd_conv_w": 2, "ssd_conv_b": 1, "ssd_norm_w": 1, "ssd_w_out": 1}
BIG = ["ffn_w_gu", "ffn_w_down", "conv_w_in", "conv_w_out", "fox_w_in", "fox_w_out", "ssd_w_in", "ssd_w_out"]
SMALL_SHARDED = ["conv_w_dw", "ssd_conv_w", "ssd_conv_b", "ssd_norm_w"]
N_CHIPS = 4


def _pack_flat(parts, pad_to):
    flat = [p.reshape(-1) for p in parts]
    offs, n = [], 0
    for f in flat:
        offs.append(n)
        n += f.shape[0]
    total = -(-n // pad_to) * pad_to
    if total > n:
        flat.append(jnp.zeros((total - n,), flat[0].dtype))
    return jnp.concatenate(flat).reshape(-1, LANES), offs


def kernel(x, mix_norm, ffn_norm, ffn_w_gu, ffn_w_down, conv_w_in, conv_w_dw, conv_w_out, fox_w_in, fox_b_f, fox_q_gain, fox_k_gain, fox_w_out, ssd_w_in, ssd_conv_w, ssd_conv_b, ssd_dt_bias, ssd_a_log, ssd_d, ssd_norm_w, ssd_w_out, loss_target, m_mix_norm, m_ffn_norm, m_ffn_w_gu, m_ffn_w_down, m_conv_w_in, m_conv_w_dw, m_conv_w_out, m_fox_w_in, m_fox_b_f, m_fox_q_gain, m_fox_k_gain, m_fox_w_out, m_ssd_w_in, m_ssd_conv_w, m_ssd_conv_b, m_ssd_dt_bias, m_ssd_a_log, m_ssd_d, m_ssd_norm_w, m_ssd_w_out, v_mix_norm, v_ffn_norm, v_ffn_w_gu, v_ffn_w_down, v_conv_w_in, v_conv_w_dw, v_conv_w_out, v_fox_w_in, v_fox_b_f, v_fox_q_gain, v_fox_k_gain, v_fox_w_out, v_ssd_w_in, v_ssd_conv_w, v_ssd_conv_b, v_ssd_dt_bias, v_ssd_a_log, v_ssd_d, v_ssd_norm_w, v_ssd_w_out):
    w = dict(zip(WEIGHTS, (mix_norm, ffn_norm, ffn_w_gu, ffn_w_down, conv_w_in, conv_w_dw, conv_w_out, fox_w_in, fox_b_f, fox_q_gain, fox_k_gain,
                           fox_w_out, ssd_w_in, ssd_conv_w, ssd_conv_b, ssd_dt_bias, ssd_a_log, ssd_d, ssd_norm_w, ssd_w_out)))
    m1 = dict(zip(WEIGHTS, (m_mix_norm, m_ffn_norm, m_ffn_w_gu, m_ffn_w_down, m_conv_w_in, m_conv_w_dw, m_conv_w_out, m_fox_w_in, m_fox_b_f,
                            m_fox_q_gain, m_fox_k_gain, m_fox_w_out, m_ssd_w_in, m_ssd_conv_w, m_ssd_conv_b, m_ssd_dt_bias, m_ssd_a_log, m_ssd_d,
                            m_ssd_norm_w, m_ssd_w_out)))
    m2 = dict(zip(WEIGHTS, (v_mix_norm, v_ffn_norm, v_ffn_w_gu, v_ffn_w_down, v_conv_w_in, v_conv_w_dw, v_conv_w_out, v_fox_w_in, v_fox_b_f,
                            v_fox_q_gain, v_fox_k_gain, v_fox_w_out, v_ssd_w_in, v_ssd_conv_w, v_ssd_conv_b, v_ssd_dt_bias, v_ssd_a_log, v_ssd_d,
                            v_ssd_norm_w, v_ssd_w_out)))
    cx, cy, cc = _place()
    chip = 2 * cx + cy

    place = jnp.stack([chip, cc]).astype(jnp.int32)
    depth = mix_norm.shape[0]
    attention = next((i for i in range(depth) if i % 3 == 1), depth)

    def carrier_of(layer):
        return None if layer == 0 else ("ffn", layer - 1) if layer <= attention else ("attention", attention)

    early, late, cm = {}, {}, {n: [] for n in BIG}
    for n in BIG:
        wb = w[n].astype(WIRE_DTYPE)
        keys = [carrier_of(_model_layer(n, l)) for l in range(wb.shape[0])]
        for key in dict.fromkeys(keys):
            first, count = keys.index(key), keys.count(key)
            if key is None:
                early[n] = wb[first:first + count]
            else:
                late.setdefault(key, {})[n] = (wb[first:first + count], first)
    gathered = _with_own(_gather_list(list(early.values()), name="gather_weights"), list(early.values()), chip)
    for n, g_ in zip(early, gathered):
        cm[n].append((g_, 0))
    sp, soffs = _pack_flat([w[n] for n in SMALL_SHARDED], SUBLANES * LANES)
    sgath = _gather_small(sp, name="gather_small").reshape(N_CHIPS, -1)
    full = {n: w[n] for n in WEIGHTS if n not in SHARD_AXIS}
    for n, off in zip(SMALL_SHARDED, soffs):
        full[n] = jnp.concatenate([sgath[j, off:off + w[n].size].reshape(w[n].shape) for j in range(N_CHIPS)], axis=SHARD_AXIS[n])

    loss, gx, grads, reds = _local_step(x[0], loss_target[0], full, cm, late, place)
    loss = lax.psum(loss, ("x", "y", "c"))

    small_names = [n for n in WEIGHTS if n not in BIG]
    sm, smoffs = _pack_flat([grads[n] for n in small_names], SUBLANES * LANES)
    sred = _allreduce_small(sm, name="allreduce_small").reshape(-1)

    g = {n: jnp.stack([reds[(n, l)] for l in range(w[n].shape[0])]).reshape(w[n].shape) for n in BIG}
    for n, off in zip(small_names, smoffs):
        fullg = sred[off:off + grads[n].size].reshape(grads[n].shape)
        if n in SHARD_AXIS:
            ax = SHARD_AXIS[n]
            fullg = lax.dynamic_slice_in_dim(fullg, chip * w[n].shape[ax], w[n].shape[ax], axis=ax)
        g[n] = fullg

    deltas, new_m, new_v = [], [], []
    for n in WEIGHTS:
        dl, mn, vn = _adamw(w[n], g[n], m1[n], m2[n], name=f"adamw_{n}")
        deltas.append(dl)
        new_m.append(mn)
        new_v.append(vn)
    return (loss, gx[None], *[g[n] for n in WEIGHTS], *deltas, *new_m, *new_v)
```

```python
import functools
import math

import jax
import jax.numpy as jnp
from jax import lax
from jax.experimental import pallas as pl
from jax.experimental.pallas import tpu as pltpu

F32 = jnp.float32
MM_DTYPE = jnp.bfloat16
ACT_DTYPE = jnp.bfloat16
WIRE_DTYPE = jnp.bfloat16

RMS_EPS = 1e-6
HEAD = 64
SSM_STATE = 128
SSM_CHUNK = 128
LANES = 128
SUBLANES = 8
VMEM_LIMIT = 48 * 1024 * 1024

ADAM_LR, ADAM_B1, ADAM_B2, ADAM_EPS, ADAM_WD, ADAM_STEP = 0.001, 0.9, 0.999, 1e-08, 0.01, 10

HI = lax.Precision.HIGHEST
MESH = pl.DeviceIdType.MESH


def _tile(dim, prefs):
    for p in prefs:
        if dim % p == 0:
            return p
    return dim


def _params(sem):
    return pltpu.CompilerParams(dimension_semantics=sem, vmem_limit_bytes=VMEM_LIMIT)


def _sigmoid(x):
    return 1.0 / (1.0 + jnp.exp(-x))


def _softplus(x):
    return jnp.maximum(x, 0.0) + jnp.log(1.0 + jnp.exp(-jnp.abs(x)))


TILES = (1024, 1408, 768, 512, 256, 128)
MIN_STEP_WORK = 1 << 30


def _mm(a, b, *, ta=False, tb=False, add=None, out_dtype=F32, name, b_layer=None, out_shard=None, inter=None, rider=None):
    ka, m = (a.shape[0], a.shape[1]) if ta else (a.shape[1], a.shape[0])
    if b_layer is None:
        kb, n = (b.shape[1], b.shape[0]) if tb else (b.shape[0], b.shape[1])
        ns = None
    else:
        ns = b.shape[3]
        kb, n = (b.shape[0] * ns, b.shape[2]) if tb else (b.shape[2], b.shape[0] * ns)
    assert ka == kb, (a.shape, b.shape, ta, tb)
    k = ka
    tm = _tile(m, (1408, 1024, 512, 256, 128) if ta else (512, 256, 128))
    tn = _tile(n, TILES)
    tk = _tile(k, TILES)
    if inter:
        segs, bw = inter
        if tb:
            tk = bw
        else:
            tn = bw
        tps = ((k if tb else n) // bw) // segs
        col = lambda q: ((q % segs) * tps + q // segs) * bw
    else:
        col = lambda q: q * (tk if tb else tn)
    def vmem(tm_, tk_):
        out_b = jnp.dtype(out_dtype).itemsize * 2 + (8 if add is not None else 0) + 4
        return 2 * tk_ * (tm_ * a.dtype.itemsize + tn * b.dtype.itemsize) + tm_ * tn * out_b

    if ta:
        while tk * 2 <= k and k % (tk * 2) == 0 and tm * tn * tk < MIN_STEP_WORK and vmem(tm, tk * 2) < VMEM_LIMIT * 3 // 4:
            tk *= 2
    else:
        while tm * 2 <= m and m % (tm * 2) == 0 and tm * tn * tk < MIN_STEP_WORK and vmem(tm * 2, tk) < VMEM_LIMIT * 3 // 4:
            tm *= 2
    nk = k // tk
    a_spec = pl.BlockSpec((tk, tm), lambda i, j, q: (q, i)) if ta else pl.BlockSpec((tm, tk), lambda i, j, q: (i, q))
    if b_layer is None:
        b_spec = pl.BlockSpec((tn, tk), lambda i, j, q: (j, q)) if tb else pl.BlockSpec((tk, tn), lambda i, j, q: (q, j))
    elif tb:
        b_spec = pl.BlockSpec((None, None, tn, tk), lambda i, j, q: (col(q) // ns, b_layer, j, (col(q) % ns) // tk))
    else:
        b_spec = pl.BlockSpec((None, None, tk, tn), lambda i, j, q: (col(j) // ns, b_layer, q, (col(j) % ns) // tn))
    if out_shard:
        assert ta and add is None
        o_spec = pl.BlockSpec((None, tm, tn), lambda i, j, q: (col(j) // out_shard, i, (col(j) % out_shard) // tn))
        o_shape = jax.ShapeDtypeStruct((N_CHIPS, m, out_shard), out_dtype)
    else:
        o_spec = pl.BlockSpec((tm, tn), lambda i, j, q: (i, j))
        o_shape = jax.ShapeDtypeStruct((m, n), out_dtype)
    dims = (((0 if ta else 1,), (1 if tb else 0,)), ((), ()))
    has_add = add is not None

    def body(*refs):
        a_ref, b_ref = refs[0], refs[1]
        o_ref = refs[2 + has_add]
        p = lax.dot_general(a_ref[...].astype(MM_DTYPE), b_ref[...].astype(MM_DTYPE), dims, preferred_element_type=F32)

        def finish(acc):
            if has_add:
                acc = acc + refs[2][...].astype(F32)
            o_ref[...] = acc.astype(out_dtype)

        if nk == 1:
            finish(p)
        else:
            acc_ref = refs[3 + has_add]
            q = pl.program_id(2)

            @pl.when(q == 0)
            def _():
                acc_ref[...] = p

            @pl.when(q > 0)
            def _():
                acc_ref[...] += p

            @pl.when(q == nk - 1)
            def _():
                finish(acc_ref[...])

    args = [a, b] + ([add] if has_add else [])
    in_specs = [a_spec, b_spec] + ([o_spec] if has_add else [])
    grid = (m // tm, n // tn, nk)
    r_in, r_out, r_shapes, r_scratch, r_args = _rider_parts(rider)
    outs = pl.pallas_call(
        _carry(body, rider, 0, len(args), 1, grid), name=name, grid=grid, in_specs=in_specs + r_in, out_specs=[o_spec] + r_out,
        out_shape=[o_shape] + r_shapes, scratch_shapes=([pltpu.VMEM((tm, tn), F32)] if nk > 1 else []) + r_scratch,
        compiler_params=_params(("arbitrary",) * 3 if rider else ("parallel", "parallel", "arbitrary")),
    )(*args, *r_args)
    return (outs[0], list(outs[1:])) if rider else outs[0]


def _rows(fn, *, name, s, tm, ncol=1, ins, outs, accs=()):
    nrt = s // tm
    hb = tm // SUBLANES
    in_specs, args = [], []
    for spec in ins:
        kind, arr = spec[0], spec[1]
        if kind == "full":
            in_specs.append(pl.BlockSpec(arr.shape, lambda j, i: (0, 0)))
        elif kind == "col":
            _, _, bw, cmap = spec
            in_specs.append(pl.BlockSpec((arr.shape[0], bw), lambda j, i, cmap=cmap: (0, cmap(j))))
        elif kind == "row":
            _, _, bw, cmap = spec
            in_specs.append(pl.BlockSpec((tm, bw), lambda j, i, cmap=cmap: (i, cmap(j))))
        elif kind == "prev":
            _, _, bw, cmap = spec
            in_specs.append(pl.BlockSpec((SUBLANES, bw), lambda j, i, cmap=cmap: (jnp.maximum(i * hb - 1, 0), cmap(j))))
        elif kind == "next":
            _, _, bw, cmap = spec
            in_specs.append(pl.BlockSpec((SUBLANES, bw), lambda j, i, cmap=cmap: (jnp.minimum((i + 1) * hb, s // SUBLANES - 1), cmap(j))))
        else:
            raise ValueError(kind)
        args.append(arr)
    out_specs, out_shape = [], []
    for w, bw, cmap, dt in outs:
        out_specs.append(pl.BlockSpec((tm, bw), lambda j, i, cmap=cmap: (i, cmap(j))))
        out_shape.append(jax.ShapeDtypeStruct((s, w), dt))
    for r, w, bw, cmap in accs:
        out_specs.append(pl.BlockSpec((r, bw), lambda j, i, cmap=cmap: (0, cmap(j))))
        out_shape.append(jax.ShapeDtypeStruct((r, w), F32))
    n_in, n_out, n_acc = len(ins), len(outs), len(accs)

    def body(*refs):
        i = pl.program_id(1)
        vals = [r[...] for r in refs[:n_in]]
        o_vals, a_vals = fn(i, nrt, *vals)
        assert len(o_vals) == n_out and len(a_vals) == n_acc
        for r, v in zip(refs[n_in:n_in + n_out], o_vals):
            r[...] = v.astype(r.dtype)
        for r, v in zip(refs[n_in + n_out:], a_vals):
            @pl.when(i == 0)
            def _(r=r, v=v):
                r[...] = v.astype(F32)

            @pl.when(i > 0)
            def _(r=r, v=v):
                r[...] += v.astype(F32)

    res = pl.pallas_call(
        body, name=name, grid=(ncol, nrt), in_specs=in_specs, out_specs=out_specs, out_shape=out_shape,
        compiler_params=_params(("parallel", "arbitrary" if accs else "parallel")),
    )(*args)
    return res


def _c0(j):
    return 0


def _cj(j):
    return j


def _gmean(v, gs):
    w = v.shape[-1]
    tile = max(gs, LANES)
    r = lax.broadcasted_iota(jnp.int32, (tile, tile), 0) // gs
    c = lax.broadcasted_iota(jnp.int32, (tile, tile), 1) // gs
    g = jnp.where(r == c, 1.0 / gs, 0.0).astype(F32)
    parts = [jnp.dot(v[:, t * tile:(t + 1) * tile], g, precision=HI, preferred_element_type=F32) for t in range(w // tile)]
    return parts[0] if len(parts) == 1 else jnp.concatenate(parts, axis=1)


def _sum_rows(v):
    return jnp.sum(v, axis=0, keepdims=True)


def _rms(x, w, *, name):
    s, d = x.shape

    def fn(i, nrt, xv, wv):
        r = lax.rsqrt(jnp.mean(xv * xv, axis=-1, keepdims=True) + RMS_EPS)
        return (xv * r * wv,), ()

    (h,) = _rows(fn, name=name, s=s, tm=_tile(s, (512, 256, 128)), ins=[("row", x, d, _c0), ("full", w.reshape(1, d))],
                 outs=[(d, d, _c0, ACT_DTYPE)])
    return h


def _rms_bwd(x, w, dh, dx_in, *, name):
    s, d = x.shape

    def fn(i, nrt, xv, wv, dhv, dxi):
        r = lax.rsqrt(jnp.mean(xv * xv, axis=-1, keepdims=True) + RMS_EPS)
        xh = xv * r
        g = dhv * wv
        dx = dxi + r * (g - xh * jnp.mean(g * xh, axis=-1, keepdims=True))
        return (dx, dx), (_sum_rows(dhv * xh),)

    dx, dxb, dw = _rows(fn, name=name, s=s, tm=_tile(s, (512, 256, 128)),
                        ins=[("row", x, d, _c0), ("full", w.reshape(1, d)), ("row", dh, d, _c0), ("row", dx_in, d, _c0)],
                        outs=[(d, d, _c0, F32), (d, d, _c0, MM_DTYPE)], accs=[(1, d, d, _c0)])
    return dx, dxb, dw.reshape(d)


def _ffn_up(h, w_gu, layer, bw, *, name, rider=None):
    m, k = h.shape
    ns = w_gu.shape[3]
    f = N_CHIPS * ns // 2
    tm = _tile(m, (512, 256, 128))

    def w_spec(first):
        return pl.BlockSpec((None, None, k, bw), lambda t, i: ((first + t * bw) // ns, layer, 0, ((first + t * bw) % ns) // bw))

    def body(h_ref, wg_ref, wu_ref, gu_ref, a_ref):
        hv = h_ref[...].astype(MM_DTYPE)
        g = jnp.dot(hv, wg_ref[...].astype(MM_DTYPE), preferred_element_type=F32)
        u = jnp.dot(hv, wu_ref[...].astype(MM_DTYPE), preferred_element_type=F32)
        gu_ref[...] = jnp.concatenate([g, u], axis=1).astype(gu_ref.dtype)
        a_ref[...] = (g * _sigmoid(g) * u).astype(a_ref.dtype)

    grid = (f // bw, m // tm)
    r_in, r_out, r_shapes, r_scratch, r_args = _rider_parts(rider)
    outs = pl.pallas_call(
        _carry(body, rider, 0, 3, 2, grid), name=name, grid=grid,
        in_specs=[pl.BlockSpec((tm, k), lambda t, i: (i, 0)), w_spec(0), w_spec(f)] + r_in,
        out_specs=[pl.BlockSpec((tm, 2 * bw), lambda t, i: (i, t)), pl.BlockSpec((tm, bw), lambda t, i: (i, t))] + r_out,
        out_shape=[jax.ShapeDtypeStruct((m, 2 * f), ACT_DTYPE), jax.ShapeDtypeStruct((m, f), ACT_DTYPE)] + r_shapes,
        scratch_shapes=r_scratch, compiler_params=_params(("arbitrary", "arbitrary") if rider else ("parallel", "parallel")),
    )(h, w_gu, w_gu, *r_args)
    return outs[0], outs[1], list(outs[2:])


def _ffn_back(dx, w_down, gu, bw, *, name):
    m, d = dx.shape
    f = w_down.shape[0]
    tm = _tile(m, (512, 256, 128))

    rc = _tile(tm, (256, 128))

    def body(dx_ref, w_ref, gu_ref, o_ref):
        wv = w_ref[...].astype(MM_DTYPE)
        for r in range(tm // rc):
            rows = slice(r * rc, (r + 1) * rc)
            da = _nt(dx_ref[rows, :].astype(MM_DTYPE), wv)
            gv, uv = gu_ref[rows, :bw].astype(F32), gu_ref[rows, bw:].astype(F32)
            sg = _sigmoid(gv)
            o_ref[rows, :bw] = (da * uv * sg * (1.0 + gv * (1.0 - sg))).astype(o_ref.dtype)
            o_ref[rows, bw:] = (da * gv * sg).astype(o_ref.dtype)

    return pl.pallas_call(
        body, name=name, grid=(f // bw, m // tm),
        in_specs=[pl.BlockSpec((tm, d), lambda t, i: (i, 0)), pl.BlockSpec((bw, d), lambda t, i: (t, 0)), pl.BlockSpec((tm, 2 * bw), lambda t, i: (i, t))],
        out_specs=pl.BlockSpec((tm, 2 * bw), lambda t, i: (i, t)), out_shape=jax.ShapeDtypeStruct((m, 2 * f), ACT_DTYPE),
        compiler_params=_params(("parallel", "parallel")),
    )(dx, w_down, gu)


def _loss_head(x, tgt, *, name):
    s, d = x.shape

    def fn(i, nrt, xv, tv):
        diff = xv - tv
        part = 0.5 * jnp.sum(diff * diff) / d
        return (diff / d, diff / d), (jnp.full((1, LANES), part, F32),)

    dy, dyb, loss = _rows(fn, name=name, s=s, tm=_tile(s, (512, 256, 128)),
                          ins=[("row", x, d, _c0), ("row", tgt, d, _c0)],
                          outs=[(d, d, _c0, F32), (d, d, _c0, MM_DTYPE)], accs=[(1, LANES, LANES, _c0)])
    return loss[0, 0], dy, dyb


def _shift_down(ext, j, tm):
    src = pltpu.roll(ext, j, 0) if j else ext
    return src[SUBLANES:SUBLANES + tm]


def _shift_up(ext, j, tm):
    return ext[:tm] if j == 0 else pltpu.roll(ext, ext.shape[0] - j, 0)[:tm]


def _gconv_fwd(p, w, bw, *, name):
    s, d = p.shape[0], p.shape[1] // 3
    kw = w.shape[0]
    tm = _tile(s, (512, 256, 128))

    def fn(i, nrt, pv, pp, wv):
        pv, pp = pv.astype(F32), pp.astype(F32)
        cv = pv[:, bw:2 * bw] * pv[:, 2 * bw:]
        pcv = jnp.where(i == 0, 0.0, pp[:, bw:2 * bw] * pp[:, 2 * bw:])
        ext = jnp.concatenate([pcv, cv], axis=0)
        u = sum(wv[k:k + 1, :] * _shift_down(ext, kw - 1 - k, tm) for k in range(kw))
        return (pv[:, :bw] * u,), ()

    (o,) = _rows(fn, name=name, s=s, tm=tm, ncol=d // bw, ins=[("row", p, 3 * bw, _cj), ("prev", p, 3 * bw, _cj), ("col", w, bw, _cj)],
                 outs=[(d, bw, _cj, ACT_DTYPE)])
    return o


def _gconv_bwd(p, w, do, bw, *, name):
    s, d = do.shape
    kw = w.shape[0]
    tm = _tile(s, (512, 256, 128))

    def fn(i, nrt, pv, pp, pn, dov, ndo, wv):
        pv, pp, dov = pv.astype(F32), pp.astype(F32), dov.astype(F32)
        bv, cv_, vv = pv[:, :bw], pv[:, bw:2 * bw], pv[:, 2 * bw:]
        cv = cv_ * vv
        pcv = jnp.where(i == 0, 0.0, pp[:, bw:2 * bw] * pp[:, 2 * bw:])
        ext = jnp.concatenate([pcv, cv], axis=0)
        shifted = [_shift_down(ext, kw - 1 - k, tm) for k in range(kw)]
        u = sum(wv[k:k + 1, :] * shifted[k] for k in range(kw))
        db = dov * u
        du = dov * bv
        ndu = jnp.where(i == nrt - 1, 0.0, ndo.astype(F32) * pn[:, :bw].astype(F32))
        ext2 = jnp.concatenate([du, ndu], axis=0)
        dcv = sum(wv[k:k + 1, :] * _shift_up(ext2, kw - 1 - k, tm) for k in range(kw))
        dw = jnp.concatenate([_sum_rows(du * shifted[k]) for k in range(kw)], axis=0)
        return (jnp.concatenate([db, dcv * vv, dcv * cv_], axis=1),), (dw,)

    dp, dw = _rows(fn, name=name, s=s, tm=tm, ncol=d // bw,
                   ins=[("row", p, 3 * bw, _cj), ("prev", p, 3 * bw, _cj), ("next", p, 3 * bw, _cj), ("row", do, bw, _cj),
                        ("next", do, bw, _cj), ("col", w, bw, _cj)],
                   outs=[(3 * d, 3 * bw, _cj, ACT_DTYPE)], accs=[(kw, d, bw, _cj)])
    return dp, dw


def _sconv_fwd(x, w, bias, *, name):
    s, d = x.shape
    kw = w.shape[0]
    bw = _tile(d, (512, 256, 128))
    tm = _tile(s, (512, 256, 128))

    def fn(i, nrt, xv, px, wv, bsv):
        xv = xv.astype(F32)
        ext = jnp.concatenate([jnp.where(i == 0, 0.0, px.astype(F32)), xv], axis=0)
        pre = sum(wv[k:k + 1, :] * _shift_down(ext, kw - 1 - k, tm) for k in range(kw)) + bsv
        return (pre * _sigmoid(pre),), ()

    (o,) = _rows(fn, name=name, s=s, tm=tm, ncol=d // bw,
                 ins=[("row", x, bw, _cj), ("prev", x, bw, _cj), ("col", w, bw, _cj), ("col", bias.reshape(1, d), bw, _cj)],
                 outs=[(d, bw, _cj, ACT_DTYPE)])
    return o


def _sconv_bwd(x, w, bias, dact, *, name):
    s, d = x.shape
    kw = w.shape[0]
    bw = _tile(d, (512, 256, 128))
    tm = _tile(s, (512, 256, 128))

    def fn(i, nrt, xv, px, nx, dav, nda, wv, bsv):
        xv = xv.astype(F32)
        ext = jnp.concatenate([jnp.where(i == 0, 0.0, px.astype(F32)), xv, nx.astype(F32)], axis=0)
        rows_e = tm + SUBLANES
        pre_e = sum(wv[k:k + 1, :] * _shift_down(ext, kw - 1 - k, rows_e) for k in range(kw)) + bsv
        da_e = jnp.concatenate([dav.astype(F32), jnp.where(i == nrt - 1, 0.0, nda.astype(F32))], axis=0)
        sg = _sigmoid(pre_e)
        dpre_e = da_e * sg * (1.0 + pre_e * (1.0 - sg))
        dx = sum(wv[k:k + 1, :] * _shift_up(dpre_e, kw - 1 - k, tm) for k in range(kw))
        dpre = dpre_e[:tm]
        dw = jnp.concatenate([_sum_rows(dpre * _shift_down(ext, kw - 1 - k, tm)) for k in range(kw)], axis=0)
        return (dx,), (dw, _sum_rows(dpre))

    dx, dw, db = _rows(fn, name=name, s=s, tm=tm, ncol=d // bw,
                       ins=[("row", x, bw, _cj), ("prev", x, bw, _cj), ("next", x, bw, _cj), ("row", dact, bw, _cj),
                            ("next", dact, bw, _cj), ("col", w, bw, _cj), ("col", bias.reshape(1, d), bw, _cj)],
                       outs=[(d, bw, _cj, ACT_DTYPE)], accs=[(kw, d, bw, _cj), (1, d, bw, _cj)])
    return dx, dw, db.reshape(d)


def _tri(n, reverse):
    r = lax.broadcasted_iota(jnp.int32, (n, n), 0)
    c = lax.broadcasted_iota(jnp.int32, (n, n), 1)
    return jnp.where((c >= r) if reverse else (c <= r), 1.0, 0.0).astype(F32)


def _cumsum_rows(x, *, reverse, name):
    s, w = x.shape
    ch = _tile(s, (256, 128))
    n = s // ch

    def body(x_ref, o_ref, carry):
        i = pl.program_id(0)

        @pl.when(i == 0)
        def _():
            carry[...] = jnp.zeros_like(carry)

        out = jnp.dot(_tri(ch, reverse), x_ref[...], precision=HI, preferred_element_type=F32) + carry[...]
        o_ref[...] = out
        carry[...] = out[0:1, :] if reverse else out[ch - 1:ch, :]

    imap = (lambda i: (n - 1 - i, 0)) if reverse else (lambda i: (i, 0))
    return pl.pallas_call(
        body, name=name, grid=(n,), in_specs=[pl.BlockSpec((ch, w), imap)], out_specs=pl.BlockSpec((ch, w), imap),
        out_shape=jax.ShapeDtypeStruct((s, w), F32), scratch_shapes=[pltpu.VMEM((1, w), F32)],
        compiler_params=_params(("arbitrary",)),
    )(x)


def _fox_prep(q, k, f, gq, gk, bf, *, name):
    s, d = q.shape
    scale = HEAD ** -0.5

    def fn(i, nrt, qv, kv, fv, gqv, gkv, bfv):
        qv, kv = qv.astype(F32), kv.astype(F32)
        qn = qv * lax.rsqrt(_gmean(qv * qv, HEAD) + RMS_EPS) * gqv * scale
        kn = kv * lax.rsqrt(_gmean(kv * kv, HEAD) + RMS_EPS) * gkv
        z = fv + bfv
        logf = jnp.minimum(z, 0.0) - jnp.log(1.0 + jnp.exp(-jnp.abs(z)))
        return (qn, kn, logf), ()

    return _rows(fn, name=name, s=s, tm=_tile(s, (512, 256, 128)),
                 ins=[("row", q, d, _c0), ("row", k, d, _c0), ("row", f, LANES, _c0), ("full", gq), ("full", gk), ("full", bf)],
                 outs=[(d, d, _c0, ACT_DTYPE), (d, d, _c0, ACT_DTYPE), (LANES, LANES, _c0, F32)])


def _fox_prep_bwd(q, k, f, gq, gk, bf, dqs, dkn, dlogf, *, name):
    s, d = q.shape
    scale = HEAD ** -0.5

    def fn(i, nrt, qv, kv, fv, gqv, gkv, bfv, dqv, dkv, dlf):
        outs, accs = [], []
        for xv, gv, dv, sc in ((qv, gqv, dqv, scale), (kv, gkv, dkv, 1.0)):
            xv, dv = xv.astype(F32), dv.astype(F32) * sc
            r = lax.rsqrt(_gmean(xv * xv, HEAD) + RMS_EPS)
            xh = xv * r
            g = dv * gv
            outs.append(r * (g - xh * _gmean(g * xh, HEAD)))
            accs.append(_sum_rows(dv * xh))
        z = fv + bfv
        df = dlf * _sigmoid(-z)
        outs.append(df)
        accs.append(_sum_rows(df))
        return outs, accs

    return _rows(fn, name=name, s=s, tm=_tile(s, (512, 256, 128)),
                 ins=[("row", q, d, _c0), ("row", k, d, _c0), ("row", f, LANES, _c0), ("full", gq), ("full", gk), ("full", bf),
                      ("row", dqs, d, _c0), ("row", dkn, d, _c0), ("row", dlogf, LANES, _c0)],
                 outs=[(d, d, _c0, ACT_DTYPE), (d, d, _c0, ACT_DTYPE), (LANES, LANES, _c0, ACT_DTYPE)],
                 accs=[(1, d, d, _c0), (1, d, d, _c0), (1, LANES, LANES, _c0)])


def _head_masks(shape):
    lane = lax.broadcasted_iota(jnp.int32, shape, len(shape) - 1)
    return lane < HEAD, lane >= HEAD


def _pick_lane(blk, idx):
    lane = lax.broadcasted_iota(jnp.int32, blk.shape, 1)
    return jnp.sum(jnp.where(lane == idx, blk, 0.0), axis=1, keepdims=True)


def _pick_row(blk, idx):
    sub = lax.broadcasted_iota(jnp.int32, blk.shape, 0)
    return jnp.sum(jnp.where(sub == idx, blk, 0.0), axis=0, keepdims=True)


def _fox_aug(qs, kn, cum, *, name):
    s, d = qs.shape
    hp = d // LANES

    def fn(i, nrt, qv, kv, cv):
        lane = lax.broadcasted_iota(jnp.int32, (qv.shape[0], LANES), 1)
        outs = [[], [], [], []]
        for p in range(hp):
            qt, kt = qv[:, p * LANES:(p + 1) * LANES], kv[:, p * LANES:(p + 1) * LANES]
            for h in range(2):
                mine = (lane < HEAD) if h == 0 else (lane >= HEAD)
                a0 = HEAD if h == 0 else 0
                c = cv[:, 2 * p + h:2 * p + h + 1]
                hi = c.astype(ACT_DTYPE).astype(F32)
                mid = (c - hi).astype(ACT_DTYPE).astype(F32)
                lo = (c - hi - mid).astype(ACT_DTYPE).astype(F32)
                ones = jnp.where((lane >= a0) & (lane < a0 + 3), 1.0, 0.0)
                kx = jnp.where(lane == a0, -hi, jnp.where(lane == a0 + 1, -mid, jnp.where(lane == a0 + 2, -lo, 0.0)))
                outs[h].append(jnp.where(mine, qt.astype(F32), ones))
                outs[2 + h].append(jnp.where(mine, kt.astype(F32), kx))
        return [jnp.concatenate(o, axis=1) for o in outs], ()

    return _rows(fn, name=name, s=s, tm=_tile(s, (512, 256, 128)), ins=[("row", qs, d, _c0), ("row", kn, d, _c0), ("row", cum, LANES, _c0)],
                 outs=[(d, d, _c0, ACT_DTYPE)] * 4)


def _tri_tables(nq, by_key):
    import numpy as np
    pairs = [(qi, kj) for kj in range(nq) for qi in range(kj, nq)] if by_key else [(qi, kj) for qi in range(nq) for kj in range(qi + 1)]
    return jnp.asarray(np.array([p[0] for p in pairs], np.int32)), jnp.asarray(np.array([p[1] for p in pairs], np.int32))


def _nt(a, b):
    return lax.dot_general(a, b, (((1,), (1,)), ((), ())), preferred_element_type=F32)


def _tn(a, b):
    return lax.dot_general(a, b, (((0,), (0,)), ((), ())), preferred_element_type=F32)


ATTN_BLOCKS = (1024, 512, 256, 128)


def _fox_dd(do, o, *, name):
    s, d = do.shape

    def fn(i, nrt, dov, ov):
        return (_reduce_heads(dov.astype(F32) * ov.astype(F32), d // HEAD, HEAD),), ()

    (dd,) = _rows(fn, name=name, s=s, tm=_tile(s, (512, 256, 128)), ins=[("row", do, d, _c0), ("row", o, d, _c0)],
                  outs=[(LANES, LANES, _c0, F32)])
    return dd


def _pair_rows(a, nh):
    s = a.shape[0]
    t = a[:, :nh].T.reshape(nh // 2, 2, s)
    return jnp.pad(t, ((0, 0), (0, SUBLANES - 2), (0, 0)))


def _rows01(r0, r1):
    sub = lax.broadcasted_iota(jnp.int32, (SUBLANES, r0.shape[1]), 0)
    return jnp.where(sub == 0, r0, jnp.where(sub == 1, r1, 0.0))


def _rider_parts(rider):
    if rider is None:
        return [], [], [], [], []
    return [ANY] * len(rider.arrays), [ANY] * len(rider.out_shapes), rider.out_shapes, rider.scratch, rider.arrays


def _fox_fwd_t(q_aug, k_aug, v, *, name, rider=None):
    s, d = v.shape
    bq = _tile(s, ATTN_BLOCKS)
    nq = s // bq
    hp = d // LANES
    qtab, ktab = _tri_tables(nq, by_key=False)

    def body(qt, kt, q0_ref, q1_ref, k0_ref, k1_ref, v_ref, o_ref, lse_ref, m0, m1, l0, l1, acc0, acc1):
        t = pl.program_id(1)
        qi, kj = qt[t], kt[t]
        ms, ls, accs = (m0, m1), (l0, l1), (acc0, acc1)

        @pl.when(kj == 0)
        def _():
            for h in range(2):
                ms[h][...] = jnp.full_like(ms[h], -jnp.inf)
                ls[h][...] = jnp.zeros_like(ls[h])
                accs[h][...] = jnp.zeros_like(accs[h])

        def update(diagonal):
            v2 = v_ref[...]
            qk = ((q0_ref, k0_ref), (q1_ref, k1_ref))
            sts = [_nt(qk[h][1][...], qk[h][0][...]) for h in range(2)]
            if diagonal:
                sts = [_diag_mask_t(st) for st in sts]
            m_prev = [ms[h][...] for h in range(2)]
            m_new = [jnp.maximum(m_prev[h], jnp.max(sts[h], axis=0, keepdims=True)) for h in range(2)]
            ps = [jnp.exp(sts[h] - m_new[h]) for h in range(2)]
            alpha = [jnp.exp(m_prev[h] - m_new[h]) for h in range(2)]
            for h in range(2):
                ls[h][...] = alpha[h] * ls[h][...] + jnp.sum(ps[h], axis=0, keepdims=True)
                accs[h][...] = alpha[h] * accs[h][...] + _tn(v2, ps[h].astype(MM_DTYPE))
                ms[h][...] = m_new[h]

        @pl.when(kj < qi)
        def _():
            update(False)

        @pl.when(kj == qi)
        def _():
            update(True)
            row = lax.broadcasted_iota(jnp.int32, (LANES, bq), 0)
            ot = jnp.where(row < HEAD, acc0[...] / l0[...], acc1[...] / l1[...])
            o_ref[...] = ot.T.astype(o_ref.dtype)
            lse_ref[...] = _rows01(m0[...] + jnp.log(l0[...]), m1[...] + jnp.log(l1[...]))

    blk = (bq, LANES)
    qmap = lambda p_, t, qt, kt: (qt[t], p_)
    kmap = lambda p_, t, qt, kt: (kt[t], p_)
    grid = (hp, qtab.shape[0])
    r_in, r_out, r_shapes, r_scratch, r_args = _rider_parts(rider)
    grid_spec = pltpu.PrefetchScalarGridSpec(
        num_scalar_prefetch=2, grid=grid,
        in_specs=[pl.BlockSpec(blk, qmap), pl.BlockSpec(blk, qmap), pl.BlockSpec(blk, kmap), pl.BlockSpec(blk, kmap), pl.BlockSpec(blk, kmap)] + r_in,
        out_specs=[pl.BlockSpec(blk, qmap), pl.BlockSpec((None, SUBLANES, bq), lambda p_, t, qt, kt: (p_, 0, qt[t]))] + r_out,
        scratch_shapes=[pltpu.VMEM((1, bq), F32)] * 4 + [pltpu.VMEM((LANES, bq), F32)] * 2 + r_scratch)
    outs = pl.pallas_call(
        _carry(body, rider, 2, 5, 2, grid), name=name, grid_spec=grid_spec,
        out_shape=[jax.ShapeDtypeStruct((s, d), ACT_DTYPE), jax.ShapeDtypeStruct((hp, SUBLANES, s), F32)] + r_shapes,
        compiler_params=_params(("arbitrary", "arbitrary") if rider else ("parallel", "arbitrary")),
    )(qtab, ktab, q_aug[0], q_aug[1], k_aug[0], k_aug[1], v, *r_args)
    return outs[0], outs[1], list(outs[2:])


def _diag_mask_t(st):
    key = lax.broadcasted_iota(jnp.int32, st.shape, 0)
    qry = lax.broadcasted_iota(jnp.int32, st.shape, 1)
    return jnp.where(qry >= key, st, -jnp.inf)


def _fox_bwd_t(q_aug, k_aug, v, lse, dd, do, *, name, rider=None):
    s, d = v.shape
    bq = _tile(s, ATTN_BLOCKS)
    nq = s // bq
    hp = d // LANES
    blk = (bq, LANES)
    qtab, ktab = _tri_tables(nq, by_key=True)
    n_steps = qtab.shape[0]

    def body(qt, kt, q0_ref, q1_ref, k0_ref, k1_ref, v_ref, lse_ref, dd_ref, do_ref,
             dq_ref, dk_ref, dv_ref, dcol_ref, drow_ref, dq_sc, rs_sc, dk_sc, dv_sc, cs_sc):
        t = pl.program_id(1)
        qi, kj = qt[t], kt[t]

        @pl.when(t == 0)
        def _():
            dq_sc[...] = jnp.zeros_like(dq_sc)
            rs_sc[...] = jnp.zeros_like(rs_sc)

        def update(diagonal):
            v2, do2 = v_ref[...], do_ref[...]
            masks = _head_masks(blk)
            row = lax.broadcasted_iota(jnp.int32, (LANES, bq), 0)
            off = pl.multiple_of(qi * bq, bq)
            for h, (q_ref, k_ref) in enumerate(((q0_ref, k0_ref), (q1_ref, k1_ref))):
                st = _nt(k_ref[...], q_ref[...])
                if diagonal:
                    st = _diag_mask_t(st)
                p = jnp.exp(st - lse_ref[h:h + 1, :])
                dp = _nt(v2, jnp.where(masks[h], do2, jnp.zeros_like(do2)))
                ds = p * (dp - dd_ref[h:h + 1, :])
                dsb = ds.astype(MM_DTYPE)
                dv_sc[h] += jnp.dot(p.astype(MM_DTYPE), do2, preferred_element_type=F32)
                dk_sc[h] += jnp.dot(dsb, q_ref[...], preferred_element_type=F32)
                cs_sc[h] += jnp.sum(ds, axis=1, keepdims=True)
                mine = (row < HEAD) if h == 0 else (row >= HEAD)
                dq_sc[:, pl.ds(off, bq)] += jnp.where(mine, _tn(k_ref[...], dsb), 0.0)
                rs_sc[h:h + 1, pl.ds(off, bq)] += jnp.sum(ds, axis=0, keepdims=True)

        @pl.when(qi == kj)
        def _():
            dk_sc[...] = jnp.zeros_like(dk_sc)
            dv_sc[...] = jnp.zeros_like(dv_sc)
            cs_sc[...] = jnp.zeros_like(cs_sc)
            update(True)

        @pl.when(qi > kj)
        def _():
            update(False)

        @pl.when(qi == nq - 1)
        def _():
            lo, _hi = _head_masks(blk)
            dk_ref[...] = jnp.where(lo, dk_sc[0], dk_sc[1]).astype(dk_ref.dtype)
            dv_ref[...] = jnp.where(lo, dv_sc[0], dv_sc[1]).astype(dv_ref.dtype)
            dcol_ref[...] = jnp.where(lo, cs_sc[0], cs_sc[1])

        @pl.when(t == n_steps - 1)
        def _():
            for c in range(nq):
                dq_ref[c * bq:(c + 1) * bq, :] = dq_sc[:, c * bq:(c + 1) * bq].T.astype(dq_ref.dtype)
            drow_ref[...] = rs_sc[...]

    qmap = lambda p_, t, qt, kt: (qt[t], p_)
    kmap = lambda p_, t, qt, kt: (kt[t], p_)
    rmap = lambda p_, t, qt, kt: (p_, 0, qt[t])
    grid = (hp, n_steps)
    r_in, r_out, r_shapes, r_scratch, r_args = _rider_parts(rider)
    outs = pl.pallas_call(
        _carry(body, rider, 2, 8, 5, grid), name=name,
        grid_spec=pltpu.PrefetchScalarGridSpec(
            num_scalar_prefetch=2, grid=grid,
            in_specs=[pl.BlockSpec(blk, qmap), pl.BlockSpec(blk, qmap), pl.BlockSpec(blk, kmap), pl.BlockSpec(blk, kmap), pl.BlockSpec(blk, kmap),
                      pl.BlockSpec((None, SUBLANES, bq), rmap), pl.BlockSpec((None, SUBLANES, bq), rmap), pl.BlockSpec(blk, qmap)] + r_in,
            out_specs=[pl.BlockSpec((s, LANES), lambda p_, t, qt, kt: (0, p_)), pl.BlockSpec(blk, kmap), pl.BlockSpec(blk, kmap),
                       pl.BlockSpec(blk, kmap), pl.BlockSpec((None, SUBLANES, s), lambda p_, t, qt, kt: (p_, 0, 0))] + r_out,
            scratch_shapes=[pltpu.VMEM((LANES, s), F32), pltpu.VMEM((SUBLANES, s), F32), pltpu.VMEM((2, bq, LANES), F32),
                            pltpu.VMEM((2, bq, LANES), F32), pltpu.VMEM((2, bq, 1), F32)] + r_scratch),
        out_shape=[jax.ShapeDtypeStruct((s, d), ACT_DTYPE), jax.ShapeDtypeStruct((s, d), ACT_DTYPE), jax.ShapeDtypeStruct((s, d), ACT_DTYPE),
                   jax.ShapeDtypeStruct((s, d), F32), jax.ShapeDtypeStruct((hp, SUBLANES, s), F32)] + r_shapes,
        compiler_params=_params(("arbitrary", "arbitrary") if rider else ("parallel", "arbitrary")),
    )(qtab, ktab, q_aug[0], q_aug[1], k_aug[0], k_aug[1], v, lse, dd, do, *r_args)
    return list(outs[:5]), list(outs[5:])


def _expand_heads(v, nh, hd):
    r = lax.broadcasted_iota(jnp.int32, (LANES, nh * hd), 0)
    c = lax.broadcasted_iota(jnp.int32, (LANES, nh * hd), 1) // hd
    e = jnp.where(r == c, 1.0, 0.0).astype(F32)
    return jnp.dot(v, e, precision=HI, preferred_element_type=F32)


def _reduce_heads(v, nh, hd):
    r = lax.broadcasted_iota(jnp.int32, (nh * hd, LANES), 0) // hd
    c = lax.broadcasted_iota(jnp.int32, (nh * hd, LANES), 1)
    e = jnp.where(r == c, 1.0, 0.0).astype(F32)
    return jnp.dot(v, e, precision=HI, preferred_element_type=F32)


def _ssd_prep(dt_raw, dt_bias, a_log, nh, *, name):
    s = dt_raw.shape[0]

    def fn(i, nrt, dtr, bsv, alv):
        dt = _softplus(dtr + bsv)
        acum = jnp.dot(_tri(SSM_CHUNK, False), dt * (-jnp.exp(alv)), precision=HI, preferred_element_type=F32)
        return (dt, acum, _expand_heads(dt, nh, HEAD), _expand_heads(acum, nh, HEAD)), ()

    w = nh * HEAD
    return _rows(fn, name=name, s=s, tm=SSM_CHUNK, ins=[("row", dt_raw, LANES, _c0), ("full", dt_bias), ("full", a_log)],
                 outs=[(LANES, LANES, _c0, F32), (LANES, LANES, _c0, F32), (w, w, _c0, F32), (w, w, _c0, F32)])


def _ssd_prep_bwd(dt_raw, dt_bias, a_log, ddtx, dacx, nh, *, name):
    s = dt_raw.shape[0]

    def fn(i, nrt, dtr, bsv, alv, ddx, dax):
        z = dtr + bsv
        dt = _softplus(z)
        a = -jnp.exp(alv)
        dda = jnp.dot(_tri(SSM_CHUNK, True), _reduce_heads(dax, nh, HEAD), precision=HI, preferred_element_type=F32)
        ddt = _reduce_heads(ddx, nh, HEAD) + dda * a
        dz = ddt * _sigmoid(z)
        lane = lax.broadcasted_iota(jnp.int32, dz.shape, 1)
        dz = jnp.where(lane < nh, dz, 0.0)
        return (dz,), (_sum_rows(dz), _sum_rows(dda * dt) * a)

    w = nh * HEAD
    return _rows(fn, name=name, s=s, tm=SSM_CHUNK,
                 ins=[("row", dt_raw, LANES, _c0), ("full", dt_bias), ("full", a_log), ("row", ddtx, w, _c0), ("row", dacx, w, _c0)],
                 outs=[(LANES, LANES, _c0, ACT_DTYPE)], accs=[(1, LANES, LANES, _c0), (1, LANES, LANES, _c0)])


def _ssd_decay(ac_blk, act_blk, head):
    col = _pick_lane(ac_blk, head)
    row = _pick_row(act_blk, head)
    r = lax.broadcasted_iota(jnp.int32, (SSM_CHUNK, SSM_CHUNK), 0)
    c = lax.broadcasted_iota(jnp.int32, (SSM_CHUNK, SSM_CHUNK), 1)
    return jnp.exp(jnp.where(r >= c, col - row, -jnp.inf))


def _group_masks(shape, hpg):
    lane = lax.broadcasted_iota(jnp.int32, shape, len(shape) - 1) // HEAD
    return [lane == k for k in range(hpg)]


def _ssd_scan_fwd(xs, bm, cm, dtx, acx, acum, acum_t, d_x, *, name):
    s, di = xs.shape
    ng = bm.shape[1] // SSM_STATE
    gw = di // ng
    hpg = gw // HEAD
    nc = s // SSM_CHUNK
    L = SSM_CHUNK
    nh_pad = acum_t.shape[0]

    gp = next(n for n in (8, 4, 2, 1) if ng % n == 0)
    N = SSM_STATE

    def body(x_ref, b_ref, c_ref, dt_ref, ax_ref, ac_ref, act_ref, d_ref, y_ref, st_ref, state):
        g2, c = pl.program_id(0), pl.program_id(1)

        @pl.when(c == 0)
        def _():
            state[...] = jnp.zeros_like(state)

        masks = _group_masks((L, gw), hpg)
        for gi in range(gp):
            g = g2 * gp + gi
            lanes, st_lanes = slice(gi * gw, (gi + 1) * gw), slice(gi * N, (gi + 1) * N)
            x4, bv, cv = x_ref[:, lanes].astype(F32), b_ref[:, st_lanes], c_ref[:, st_lanes]
            ax = ax_ref[:, lanes]
            tx = x4 * dt_ref[:, lanes]
            cb = lax.dot_general(cv, bv, (((1,), (1,)), ((), ())), preferred_element_type=F32)
            y = jnp.zeros((L, gw), F32)
            txb = tx.astype(MM_DTYPE)
            for k in range(hpg):
                wk = (cb * _ssd_decay(ac_ref[...], act_ref[...], g * hpg + k)).astype(MM_DTYPE)
                y = y + jnp.where(masks[k], jnp.dot(wk, txb, preferred_element_type=F32), 0.0)
            prev = state[gi]
            st_ref[gi] = prev
            y = y + jnp.dot(cv, prev.astype(MM_DTYPE), preferred_element_type=F32) * jnp.exp(ax)
            y = y + d_ref[:, lanes] * x4
            y_ref[:, lanes] = y.astype(y_ref.dtype)
            a_last = ax[L - 1:L, :]
            sx = (tx * jnp.exp(a_last - ax)).astype(MM_DTYPE)
            state[gi] = prev * jnp.exp(a_last) + lax.dot_general(bv, sx, (((0,), (0,)), ((), ())), preferred_element_type=F32)

    y, states = pl.pallas_call(
        body, name=name, grid=(ng // gp, nc),
        in_specs=[pl.BlockSpec((L, gp * gw), lambda g, c: (c, g)), pl.BlockSpec((L, gp * N), lambda g, c: (c, g)),
                  pl.BlockSpec((L, gp * N), lambda g, c: (c, g)), pl.BlockSpec((L, gp * gw), lambda g, c: (c, g)),
                  pl.BlockSpec((L, gp * gw), lambda g, c: (c, g)), pl.BlockSpec((L, LANES), lambda g, c: (c, 0)),
                  pl.BlockSpec((nh_pad, L), lambda g, c: (0, c)), pl.BlockSpec((1, gp * gw), lambda g, c: (0, g))],
        out_specs=[pl.BlockSpec((L, gp * gw), lambda g, c: (c, g)), pl.BlockSpec((gp, None, N, gw), lambda g, c: (g, c, 0, 0))],
        out_shape=[jax.ShapeDtypeStruct((s, di), ACT_DTYPE), jax.ShapeDtypeStruct((ng, nc, N, gw), F32)],
        scratch_shapes=[pltpu.VMEM((gp, N, gw), F32)],
        compiler_params=_params(("parallel", "arbitrary")),
    )(xs, bm, cm, dtx, acx, acum, acum_t, d_x)
    return y, states


def _ssd_scan_bwd(xs, bm, cm, dtx, acx, acum, acum_t, d_x, states, dy, *, name, rider=None):
    s, di = xs.shape
    ng = bm.shape[1] // SSM_STATE
    gw = di // ng
    hpg = gw // HEAD
    nc = s // SSM_CHUNK
    L = SSM_CHUNK
    nh_pad = acum_t.shape[0]

    gp = next(n for n in (8, 4, 2, 1) if ng % n == 0)
    N = SSM_STATE

    def body(x_ref, b_ref, c_ref, dt_ref, ax_ref, ac_ref, act_ref, d_ref, st_ref, dy_ref,
             dx_ref, db_ref, dc_ref, ddt_ref, dax_ref, dd_ref, dstate):
        g2, cc = pl.program_id(0), pl.program_id(1)

        @pl.when(cc == 0)
        def _():
            dstate[...] = jnp.zeros_like(dstate)
            dd_ref[...] = jnp.zeros_like(dd_ref)

        for gi in range(gp):
            one_group(g2 * gp + gi, gi, slice(gi * gw, (gi + 1) * gw), slice(gi * N, (gi + 1) * N), x_ref, b_ref, c_ref, dt_ref, ax_ref, ac_ref,
                      act_ref, d_ref, st_ref, dy_ref, dx_ref, db_ref, dc_ref, ddt_ref, dax_ref, dd_ref, dstate)

    def one_group(g, gi, lanes, st_lanes, x_ref, b_ref, c_ref, dt_ref, ax_ref, ac_ref, act_ref, d_ref, st_ref, dy_ref,
                  dx_ref, db_ref, dc_ref, ddt_ref, dax_ref, dd_ref, dstate):
        x4, bv, cv = x_ref[:, lanes].astype(F32), b_ref[:, st_lanes], c_ref[:, st_lanes]
        tv, ax, dyv = dt_ref[:, lanes], ax_ref[:, lanes], dy_ref[:, lanes].astype(F32)
        prev, dn = st_ref[gi], dstate[gi]
        dnb = dn.astype(MM_DTYPE)
        masks = _group_masks((L, gw), hpg)
        tx = x4 * tv
        txb = tx.astype(MM_DTYPE)
        e_ax = jnp.exp(ax)
        a_last = ax[L - 1:L, :]
        e_last = jnp.exp(a_last)
        ed = jnp.exp(a_last - ax)

        dx = d_ref[:, lanes] * dyv
        dd_ref[:, lanes] += _sum_rows(dyv * x4)
        dye = (dyv * e_ax).astype(MM_DTYPE)
        yo = jnp.dot(cv, prev.astype(MM_DTYPE), preferred_element_type=F32) * e_ax
        dc = lax.dot_general(dye, prev.astype(MM_DTYPE), (((1,), (1,)), ((), ())), preferred_element_type=F32)
        dprev = lax.dot_general(cv, dye, (((0,), (0,)), ((), ())), preferred_element_type=F32)
        dax = dyv * yo
        sx = tx * ed
        dsx = jnp.dot(bv, dnb, preferred_element_type=F32)
        db = lax.dot_general(sx.astype(MM_DTYPE), dnb, (((1,), (1,)), ((), ())), preferred_element_type=F32)
        dtx_ = dsx * ed
        dsx_sx = dsx * sx
        dax = dax - dsx_sx
        dlast = _sum_rows(dsx_sx) + _sum_rows(dn * prev) * e_last
        dprev = dprev + dn * e_last
        cb = lax.dot_general(cv, bv, (((1,), (1,)), ((), ())), preferred_element_type=F32)
        dcb = jnp.zeros((L, L), F32)
        lane = lax.broadcasted_iota(jnp.int32, (L, gw), 1)
        for k in range(hpg):
            dec = _ssd_decay(ac_ref[...], act_ref[...], g * hpg + k)
            wk = (cb * dec).astype(MM_DTYPE)
            dyk = jnp.where(masks[k], dyv, 0.0).astype(MM_DTYPE)
            dtx_ = dtx_ + jnp.where(masks[k], lax.dot_general(wk, dyk, (((0,), (0,)), ((), ())), preferred_element_type=F32), 0.0)
            dwk = lax.dot_general(dyk, txb, (((1,), (1,)), ((), ())), preferred_element_type=F32)
            dcb = dcb + dwk * dec
            mk = dwk * cb * dec
            da_k = jnp.sum(mk, axis=1, keepdims=True) - jnp.sum(mk.T, axis=1, keepdims=True)
            dax = dax + jnp.where(lane == k * HEAD, da_k, 0.0)
        dcbb = dcb.astype(MM_DTYPE)
        dc = dc + jnp.dot(dcbb, bv, preferred_element_type=F32)
        db = db + lax.dot_general(dcbb, cv, (((0,), (0,)), ((), ())), preferred_element_type=F32)
        sub = lax.broadcasted_iota(jnp.int32, (L, gw), 0)
        dax = dax + jnp.where(sub == L - 1, dlast, 0.0)
        dx_ref[:, lanes] = (dx + dtx_ * tv).astype(dx_ref.dtype)
        ddt_ref[:, lanes] = dtx_ * x4
        dax_ref[:, lanes] = dax
        db_ref[:, st_lanes] = db.astype(db_ref.dtype)
        dc_ref[:, st_lanes] = dc.astype(dc_ref.dtype)
        dstate[gi] = dprev

    rev = lambda g, c: (nc - 1 - c, g)
    rev0 = lambda g, c: (nc - 1 - c, 0)
    grid = (ng // gp, nc)
    r_in, r_out, r_shapes, r_scratch, r_args = _rider_parts(rider)
    outs = pl.pallas_call(
        _carry(body, rider, 0, 10, 6, grid), name=name, grid=grid,
        in_specs=[pl.BlockSpec((L, gp * gw), rev), pl.BlockSpec((L, gp * N), rev), pl.BlockSpec((L, gp * N), rev),
                  pl.BlockSpec((L, gp * gw), rev), pl.BlockSpec((L, gp * gw), rev), pl.BlockSpec((L, LANES), rev0),
                  pl.BlockSpec((nh_pad, L), lambda g, c: (0, nc - 1 - c)), pl.BlockSpec((1, gp * gw), lambda g, c: (0, g)),
                  pl.BlockSpec((gp, None, N, gw), lambda g, c: (g, nc - 1 - c, 0, 0)), pl.BlockSpec((L, gp * gw), rev)] + r_in,
        out_specs=[pl.BlockSpec((L, gp * gw), rev), pl.BlockSpec((L, gp * N), rev), pl.BlockSpec((L, gp * N), rev),
                   pl.BlockSpec((L, gp * gw), rev), pl.BlockSpec((L, gp * gw), rev), pl.BlockSpec((1, gp * gw), lambda g, c: (0, g))] + r_out,
        out_shape=[jax.ShapeDtypeStruct((s, di), ACT_DTYPE), jax.ShapeDtypeStruct(bm.shape, ACT_DTYPE), jax.ShapeDtypeStruct(cm.shape, ACT_DTYPE),
                   jax.ShapeDtypeStruct((s, di), F32), jax.ShapeDtypeStruct((s, di), F32), jax.ShapeDtypeStruct((1, di), F32)] + r_shapes,
        scratch_shapes=[pltpu.VMEM((gp, N, gw), F32)] + r_scratch,
        compiler_params=_params(("arbitrary", "arbitrary") if rider else ("parallel", "arbitrary")),
    )(xs, bm, cm, dtx, acx, acum, acum_t, d_x, states, dy, *r_args)
    return list(outs[:6]), list(outs[6:])


def _ssd_gate(y, z, w, gs, *, name):
    s, d = y.shape

    def fn(i, nrt, yv, zv, wv):
        zv = zv.astype(F32)
        u = yv.astype(F32) * zv * _sigmoid(zv)
        return (u * lax.rsqrt(_gmean(u * u, gs) + RMS_EPS) * wv,), ()

    (o,) = _rows(fn, name=name, s=s, tm=_tile(s, (256, 128)), ins=[("row", y, d, _c0), ("row", z, d, _c0), ("full", w.reshape(1, d))],
                 outs=[(d, d, _c0, ACT_DTYPE)])
    return o


def _ssd_gate_bwd(y, z, w, do, gs, *, name):
    s, d = y.shape

    def fn(i, nrt, yv, zv, wv, dov):
        yv, zv, dov = yv.astype(F32), zv.astype(F32), dov.astype(F32)
        sg = _sigmoid(zv)
        sl = zv * sg
        u = yv * sl
        r = lax.rsqrt(_gmean(u * u, gs) + RMS_EPS)
        uh = u * r
        g = dov * wv
        du = r * (g - uh * _gmean(g * uh, gs))
        return (du * sl, du * yv * sg * (1.0 + zv * (1.0 - sg))), (_sum_rows(dov * uh),)

    dy, dz, dw = _rows(fn, name=name, s=s, tm=_tile(s, (256, 128)),
                       ins=[("row", y, d, _c0), ("row", z, d, _c0), ("full", w.reshape(1, d)), ("row", do, d, _c0)],
                       outs=[(d, d, _c0, ACT_DTYPE), (d, d, _c0, ACT_DTYPE)], accs=[(1, d, d, _c0)])
    return dy, dz, dw.reshape(d)


def _pad_lanes(w):
    return jnp.pad(w, ((0, 0), (0, LANES - w.shape[1])))


def _nt_sum(pairs, name):
    acc = None
    for a, b in pairs:
        acc = _mm(a, b, tb=True, add=acc, name=name)
    return acc


def _conv_mixer_fwd(h, w_in, layer, w_dw, tag):
    d, ns = h.shape[1], w_in.shape[3]
    inter = (3, _tile(math.gcd(d, ns), TILES))
    p = _mm(h, w_in, b_layer=layer, inter=inter, out_dtype=ACT_DTYPE, name=f"{tag}_in")
    return _gconv_fwd(p, w_dw, inter[1], name=f"{tag}_gate"), (h, w_in, layer, inter, p, w_dw)


def _conv_mixer_bwd(cache, do, tag):
    h, w_in, layer, inter, p, w_dw = cache
    dp, dw_dw = _gconv_bwd(p, w_dw, do, inter[1], name=f"{tag}_gate_bwd")
    dw_in = _mm(h, dp, ta=True, out_shard=w_in.shape[3], inter=inter, out_dtype=WIRE_DTYPE, name=f"{tag}_dw_in")
    dh = _mm(dp, w_in, tb=True, b_layer=layer, inter=inter, name=f"{tag}_dh")
    return dh, {"w_in": dw_in, "w_dw": dw_dw}


def _fox_mixer_fwd(h, w_in, b_f, q_gain, k_gain, tag, rider=None):
    d = h.shape[1]
    nh = d // HEAD
    ws = [w_in[:, k * d:(k + 1) * d] for k in range(3)] + [_pad_lanes(w_in[:, 3 * d:])]
    q, k, v = [_mm(h, w, out_dtype=ACT_DTYPE, name=f"{tag}_in") for w in ws[:3]]
    f = _mm(h, ws[3], name=f"{tag}_in_f")
    gq = jnp.tile(q_gain, nh).reshape(1, d)
    gk = jnp.tile(k_gain, nh).reshape(1, d)
    bf = _pad_lanes(b_f.reshape(1, nh))
    qs, kn, logf = _fox_prep(q, k, f, gq, gk, bf, name=f"{tag}_prep")
    cum = _cumsum_rows(logf, reverse=False, name=f"{tag}_cum")
    aug = _fox_aug(qs, kn, cum, name=f"{tag}_aug")
    q_aug, k_aug = aug[:2], aug[2:]
    o, lse, landed = _fox_fwd_t(q_aug, k_aug, v, name=f"{tag}_attn", rider=rider)
    return o, (h, ws, q, k, v, f, gq, gk, bf, q_aug, k_aug, o, lse), landed


def _fox_mixer_bwd(cache, do, tag, rider=None):
    h, ws, q, k, v, f, gq, gk, bf, q_aug, k_aug, o, lse = cache
    s, d = q.shape
    nh = d // HEAD
    dd = _pair_rows(_fox_dd(do, o, name=f"{tag}_attn_dd"), nh)
    (dqs, dkn, dv, dcol, drow), landed = _fox_bwd_t(q_aug, k_aug, v, lse, dd, do, name=f"{tag}_attn_bwd", rider=rider)
    dcum = _pad_lanes(drow[:, :2, :].reshape(nh, s).T - dcol[:, ::HEAD])
    dlogf = _cumsum_rows(dcum, reverse=True, name=f"{tag}_cum_bwd")
    dq, dk, df, dgq, dgk, dbf = _fox_prep_bwd(q, k, f, gq, gk, bf, dqs, dkn, dlogf, name=f"{tag}_prep_bwd")
    dps = (dq, dk, dv, df)
    dws = [_mm(h, dp, ta=True, name=f"{tag}_dw_in") for dp in dps]
    dw_in = jnp.concatenate(dws[:3] + [dws[3][:, :nh]], axis=1)
    dh = _nt_sum(list(zip(dps, ws)), f"{tag}_dh")
    return dh, {"w_in": dw_in, "b_f": dbf[0, :nh], "q_gain": dgq.reshape(nh, HEAD).sum(0), "k_gain": dgk.reshape(nh, HEAD).sum(0)}, landed


def _ssd_mixer_fwd(h, w_in, conv_w, conv_b, dt_bias, a_log, d_skip, norm_w, tag):
    di = norm_w.shape[0]
    nh = di // HEAD
    gn = (conv_w.shape[1] - di) // 2
    cuts = [0, di, 2 * di, 2 * di + gn, 2 * di + 2 * gn]
    ws = [w_in[:, cuts[k]:cuts[k + 1]] for k in range(4)] + [_pad_lanes(w_in[:, cuts[4]:])]
    z, xr, br, cr = [_mm(h, w, out_dtype=ACT_DTYPE, name=f"{tag}_in") for w in ws[:4]]
    dtr = _mm(h, ws[4], name=f"{tag}_in_dt")
    ccuts = [0, di, di + gn, di + 2 * gn]
    cws = [conv_w[:, ccuts[k]:ccuts[k + 1]] for k in range(3)]
    cbs = [conv_b[ccuts[k]:ccuts[k + 1]] for k in range(3)]
    xs, bm, cm = [_sconv_fwd(r, w, b, name=f"{tag}_conv") for r, w, b in zip((xr, br, cr), cws, cbs)]
    dtb = _pad_lanes(dt_bias.reshape(1, nh))
    alg = _pad_lanes(a_log.reshape(1, nh))
    _dt, acum, dtx, acx = _ssd_prep(dtr, dtb, alg, nh, name=f"{tag}_prep")
    acum_t = acum[:, :nh].T
    d_x = jnp.repeat(d_skip, HEAD).reshape(1, di)
    y, states = _ssd_scan_fwd(xs, bm, cm, dtx, acx, acum, acum_t, d_x, name=f"{tag}_scan")
    gs = di // (gn // SSM_STATE)
    o = _ssd_gate(y, z, norm_w, gs, name=f"{tag}_gate")
    return o, (h, ws, z, (xr, br, cr), dtr, cws, cbs, xs, bm, cm, dtb, alg, dtx, acx, acum, acum_t, d_x, states, y, norm_w, gs, nh)


def _ssd_mixer_bwd(cache, do, tag, rider=None):
    h, ws, z, raws, dtr, cws, cbs, xs, bm, cm, dtb, alg, dtx, acx, acum, acum_t, d_x, states, y, norm_w, gs, nh = cache
    dy, dz, dnorm = _ssd_gate_bwd(y, z, norm_w, do, gs, name=f"{tag}_gate_bwd")
    (dxs, dbm, dcm, ddtx, dacx, dd_x), landed = _ssd_scan_bwd(xs, bm, cm, dtx, acx, acum, acum_t, d_x, states, dy, name=f"{tag}_scan_bwd",
                                                             rider=rider)
    ddtr, ddtb, dalg = _ssd_prep_bwd(dtr, dtb, alg, ddtx, dacx, nh, name=f"{tag}_prep_bwd")
    conv = [_sconv_bwd(r, w, b, da, name=f"{tag}_conv_bwd") for r, w, b, da in zip(raws, cws, cbs, (dxs, dbm, dcm))]
    dps = (dz, conv[0][0], conv[1][0], conv[2][0], ddtr)
    dws = [_mm(h, dp, ta=True, name=f"{tag}_dw_in") for dp in dps]
    dw_in = jnp.concatenate(dws[:4] + [dws[4][:, :nh]], axis=1)
    dh = _nt_sum(list(zip(dps, ws)), f"{tag}_dh")
    return dh, {"w_in": dw_in, "conv_w": jnp.concatenate([c[1] for c in conv], axis=1), "conv_b": jnp.concatenate([c[2] for c in conv]),
                "dt_bias": ddtb[0, :nh], "a_log": dalg[0, :nh], "d": dd_x.reshape(nh, HEAD).sum(1), "norm_w": dnorm}, landed


def _rows_natural(cm, layer):
    return cm[:, layer].reshape(-1, cm.shape[3])


def _cols_natural(cm, layer):
    return jnp.moveaxis(cm[:, layer], 0, 1).reshape(cm.shape[2], -1)


def _cols_chip_major(g):
    return jnp.moveaxis(g.reshape(g.shape[0], N_CHIPS, -1), 1, 0).astype(WIRE_DTYPE)


MIXERS = ("conv", "fox", "ssd")


def _model_layer(name, l):
    return l if name.startswith("ffn") else 3 * l + MIXERS.index(name.split("_")[0])


def _piece(pieces, layer):
    for arr, start in pieces:
        if start <= layer < start + arr.shape[1]:
            return arr, layer - start
    raise KeyError(layer)


def _with_own(landed, shards, chip):
    return [lax.dynamic_update_slice(g, w[None], (chip, 0, 0, 0)) for g, w in zip(landed, shards)]


def _reduce_begin(gs, place):
    from_sibling = _swap_halves_list(gs, name="reduce_halves")
    return [_add_half(g, r, place, name="reduce_add_sibling") for g, r in zip(gs, from_sibling)]


def _reduce_end(by_chip, chip_sums, place):
    reds = [_sum_chips(b, s, place, name="reduce_sum_chips") for b, s in zip(by_chip, chip_sums)]
    return _join_halves_list(reds, name="reduce_share")


def _local_step(x, tgt, fw, cm, late, place):
    depth = fw["mix_norm"].shape[0]
    chip = place[0]
    cm = {n: list(p) for n, p in cm.items()}
    late = dict(late)

    def gather_rider(group):
        return _gather_ici_rider([shards for shards, _ in group.values()]) if group else None

    def land(group, landed):
        if group:
            full = _with_own(_forward_halves_list(landed, name="gather_forward"), [shards for shards, _ in group.values()], chip)
            for (n, (_, start)), arr in zip(group.items(), full):
                cm[n].append((arr, start))

    layers = []
    xin = x
    for i in range(depth):
        kind, j = i % 3, i // 3
        tag = f"l{i}"
        h = _rms(xin, fw["mix_norm"][i], name=f"{tag}_norm1")
        if kind == 0:
            w_in, jl = _piece(cm["conv_w_in"], j)
            o, mc = _conv_mixer_fwd(h, w_in, jl, fw["conv_w_dw"][j], tag + "_conv")
        elif kind == 1:
            group = late.pop(("attention", i), None)
            o, mc, landed = _fox_mixer_fwd(h, _cols_natural(*_piece(cm["fox_w_in"], j)), fw["fox_b_f"][j], fw["fox_q_gain"][j],
                                           fw["fox_k_gain"][j], tag + "_fox", rider=gather_rider(group))
            land(group, landed)
        else:
            o, mc = _ssd_mixer_fwd(h, _cols_natural(*_piece(cm["ssd_w_in"], j)), fw["ssd_conv_w"][j], fw["ssd_conv_b"][j], fw["ssd_dt_bias"][j],
                                   fw["ssd_a_log"][j], fw["ssd_d"][j], fw["ssd_norm_w"][j], tag + "_ssd")
        w_out = _rows_natural(*_piece(cm[MIXERS[kind] + "_w_out"], j))
        x1 = _mm(o, w_out, add=xin, name=f"{tag}_mix_out")
        h2 = _rms(x1, fw["ffn_norm"][i], name=f"{tag}_norm2")
        w_gu, il = _piece(cm["ffn_w_gu"], i)
        w_down = _rows_natural(*_piece(cm["ffn_w_down"], i))
        inter = (2, _tile(math.gcd(w_down.shape[0], w_gu.shape[3]), TILES))
        group = late.pop(("ffn", i), None)
        gu, a, landed = _ffn_up(h2, w_gu, il, inter[1], name=f"{tag}_ffn_gu", rider=gather_rider(group))
        land(group, landed)
        layers.append((xin, o, mc, w_out, x1, h2, w_gu, il, w_down, inter, gu, a))
        xin = _mm(a, w_down, add=x1, name=f"{tag}_ffn_down")
    loss, dx, dxb = _loss_head(xin, tgt, name="loss_head")

    small = {k: [None] * v.shape[0] for k, v in fw.items()}
    reds = {}
    riding = None
    at_end = []
    for i in reversed(range(depth)):
        kind, j = i % 3, i // 3
        tag = f"l{i}"
        xin, o, mc, w_out, x1, h2, w_gu, il, w_down, inter, gu, a = layers[i]
        mine = [(("ffn_w_down", i), _mm(a, dxb, ta=True, out_dtype=WIRE_DTYPE, name=f"{tag}_dw_down").reshape(N_CHIPS, -1, w_down.shape[1]))]
        dgu = _ffn_back(dxb, w_down, gu, inter[1], name=f"{tag}_ffn_back")
        mine.append((("ffn_w_gu", i), _mm(h2, dgu, ta=True, out_shard=w_gu.shape[3], inter=inter, out_dtype=WIRE_DTYPE, name=f"{tag}_dw_gu")))
        rider = riding[2] if riding and kind == 0 else None
        dh2 = _mm(dgu, w_gu, tb=True, b_layer=il, inter=inter, name=f"{tag}_dh2", rider=rider)
        if rider is not None:
            dh2, landed = dh2
            reds.update(zip(riding[0], _reduce_end(landed, riding[1], place)))
            riding = None
        dx1, dx1b, small["ffn_norm"][i] = _rms_bwd(x1, fw["ffn_norm"][i], dh2, dx, name=f"{tag}_norm2_bwd")
        mine.append(((MIXERS[kind] + "_w_out", j),
                     _mm(o, dx1b, ta=True, out_dtype=WIRE_DTYPE, name=f"{tag}_dw_out").reshape(N_CHIPS, -1, w_out.shape[1])))
        do = _mm(dx1b, w_out, tb=True, out_dtype=ACT_DTYPE, name=f"{tag}_do")
        rider = riding[2] if riding else None
        if kind == 0:
            dh, mg = _conv_mixer_bwd(mc, do, tag + "_conv")
        elif kind == 1:
            dh, mg, landed = _fox_mixer_bwd(mc, do, tag + "_fox", rider=rider)
        else:
            dh, mg, landed = _ssd_mixer_bwd(mc, do, tag + "_ssd", rider=rider)
        if rider is not None:
            reds.update(zip(riding[0], _reduce_end(landed, riding[1], place)))
            riding = None
        for k, v in mg.items():
            if k == "w_in":
                mine.append(((f"{MIXERS[kind]}_w_in", j), v if kind == 0 else _cols_chip_major(v)))
            else:
                small[f"{MIXERS[kind]}_{k}"][j] = v
        dx, dxb, small["mix_norm"][i] = _rms_bwd(xin, fw["mix_norm"][i], dh, dx1, name=f"{tag}_norm1_bwd")
        if i > 0:
            chip_sums = _reduce_begin([g for _, g in mine], place)
            riding = ([k for k, _ in mine], chip_sums, _scatter_rider(chip_sums))
        else:
            at_end += mine
    assert riding is None
    chip_sums = _reduce_begin([g for _, g in at_end], place)
    by_chip = _run_rider(_scatter_rider(chip_sums), name="reduce_chips")
    reds.update(zip([k for k, _ in at_end], _reduce_end(by_chip, chip_sums, place)))
    return loss, dx, {k: jnp.stack(v) for k, v in small.items()}, reds


ANY = pl.BlockSpec(memory_space=pl.ANY)
VMEM_SPEC = pl.BlockSpec(memory_space=pltpu.VMEM)


def _place():
    return lax.axis_index("x"), lax.axis_index("y"), lax.axis_index("c")


def _remote(src, dst, send_sems, recv_sems, k, to):
    return pltpu.make_async_remote_copy(src_ref=src, dst_ref=dst, send_sem=send_sems.at[k], recv_sem=recv_sems.at[k],
                                        device_id=to, device_id_type=MESH)


def _half_of(ref, h, shape):
    layers, rows, _ = shape
    if layers % 2 == 0:
        return ref.at[pl.ds(h * (layers // 2), layers // 2)]
    return ref.at[:, pl.ds(pl.multiple_of(h * (rows // 2), 16), rows // 2)]


def _row_half(ref, h, rows):
    return ref.at[:, pl.ds(pl.multiple_of(h * (rows // 2), 16), rows // 2)]


def _gather_list(ws, *, name):
    n = len(ws)

    def body(*refs):
        w_refs, o_refs, send_sems, recv_sems = refs[:n], refs[n:2 * n], refs[2 * n], refs[2 * n + 1]
        x, y, c = _place()
        me, sibling, m = (x, y, c), (x, y, 1 - c), 2 * x + y
        chips = [(1 - x, y), (x, 1 - y), (1 - x, 1 - y)]
        first, passed = [], []
        for p, (w_ref, o_ref) in enumerate(zip(w_refs, o_refs)):
            for j, (px, py) in enumerate(chips):
                cp = _remote(_half_of(w_ref, c, ws[p].shape), _half_of(o_ref.at[m], c, ws[p].shape), send_sems, recv_sems, 6 * p + j, (px, py, c))
                cp.start()
                first.append(cp)
        for p, o_ref in enumerate(o_refs):
            for j, (px, py) in enumerate(chips):
                blk = _half_of(o_ref.at[2 * px + py], c, ws[p].shape)
                _remote(blk, blk, send_sems, recv_sems, 6 * p + j, me).wait_recv()
                fwd = _remote(blk, blk, send_sems, recv_sems, 6 * p + 3 + j, sibling)
                fwd.start()
                passed.append(fwd)
        for p, o_ref in enumerate(o_refs):
            for j, (px, py) in enumerate(chips):
                blk = _half_of(o_ref.at[2 * px + py], 1 - c, ws[p].shape)
                _remote(blk, blk, send_sems, recv_sems, 6 * p + 3 + j, me).wait_recv()
        for cp in first + passed:
            cp.wait_send()

    return pl.pallas_call(
        body, name=name, in_specs=[ANY] * n, out_specs=[ANY] * n,
        out_shape=[jax.ShapeDtypeStruct((N_CHIPS,) + w.shape, w.dtype) for w in ws],
        scratch_shapes=[pltpu.SemaphoreType.DMA((6 * n,)), pltpu.SemaphoreType.DMA((6 * n,))],
    )(*ws)


class _Rider:
    def __init__(self, arrays, out_shapes, n_sems, start, finish):
        self.arrays, self.out_shapes, self.n_sems, self.start, self.finish = list(arrays), list(out_shapes), n_sems, start, finish

    @property
    def scratch(self):
        return [pltpu.SemaphoreType.DMA((self.n_sems,)), pltpu.SemaphoreType.DMA((self.n_sems,))]


def _carry(body, rider, n_prefetch, n_in, n_out, grid):
    if rider is None:
        return body
    ri, ro = len(rider.arrays), len(rider.out_shapes)

    def wrapped(*refs):
        pre, rest = refs[:n_prefetch], refs[n_prefetch:]
        ins, r_in = rest[:n_in], rest[n_in:n_in + ri]
        outs, r_out = rest[n_in + ri:n_in + ri + n_out], rest[n_in + ri + n_out:n_in + ri + n_out + ro]
        scratch = rest[n_in + ri + n_out + ro:]
        first = functools.reduce(jnp.logical_and, [pl.program_id(a) == 0 for a in range(len(grid))])
        last = functools.reduce(jnp.logical_and, [pl.program_id(a) == g - 1 for a, g in enumerate(grid)])

        @pl.when(first)
        def _():
            rider.start(r_in, r_out, scratch[-2], scratch[-1])

        body(*pre, *ins, *outs, *scratch[:-2])

        @pl.when(last)
        def _():
            rider.finish(r_in, r_out, scratch[-2], scratch[-1])

    return wrapped


def _run_rider(rider, *, name):
    n = len(rider.arrays)

    def body(*refs):
        rider.start(refs[:n], refs[n:2 * n], refs[-2], refs[-1])
        rider.finish(refs[:n], refs[n:2 * n], refs[-2], refs[-1])

    return pl.pallas_call(body, name=name, in_specs=[ANY] * n, out_specs=[ANY] * len(rider.out_shapes), out_shape=rider.out_shapes,
                          scratch_shapes=rider.scratch)(*rider.arrays)


def _chips_of(x, y):
    return [(1 - x, y), (x, 1 - y), (1 - x, 1 - y)]


def _gather_ici_rider(ws):
    def copies(w_refs, o_refs, send_sems, recv_sems):
        x, y, c = _place()
        m = 2 * x + y
        return [_remote(_half_of(w_ref, c, ws[p].shape), _half_of(o_ref.at[m], c, ws[p].shape), send_sems, recv_sems, 3 * p + j, (px, py, c))
                for p, (w_ref, o_ref) in enumerate(zip(w_refs, o_refs)) for j, (px, py) in enumerate(_chips_of(x, y))]

    def start(w_refs, o_refs, send_sems, recv_sems):
        for cp in copies(w_refs, o_refs, send_sems, recv_sems):
            cp.start()

    def finish(w_refs, o_refs, send_sems, recv_sems):
        x, y, c = _place()
        for p, o_ref in enumerate(o_refs):
            for j, (px, py) in enumerate(_chips_of(x, y)):
                blk = _half_of(o_ref.at[2 * px + py], c, ws[p].shape)
                _remote(blk, blk, send_sems, recv_sems, 3 * p + j, (x, y, c)).wait_recv()
        for cp in copies(w_refs, o_refs, send_sems, recv_sems):
            cp.wait_send()

    return _Rider(ws, [jax.ShapeDtypeStruct((N_CHIPS,) + w.shape, w.dtype) for w in ws], 3 * len(ws), start, finish)


def _forward_halves_list(gathered, *, name):
    n = len(gathered)
    shapes = [g.shape[1:] for g in gathered]

    def body(*refs):
        o_refs, send_sems, recv_sems = refs[n:2 * n], refs[2 * n], refs[2 * n + 1]
        x, y, c = _place()
        cps = []
        for p, o_ref in enumerate(o_refs):
            for j, (px, py) in enumerate(_chips_of(x, y)):
                blk = _half_of(o_ref.at[2 * px + py], c, shapes[p])
                cp = _remote(blk, blk, send_sems, recv_sems, 3 * p + j, (x, y, 1 - c))
                cp.start()
                cps.append(cp)
        for p, o_ref in enumerate(o_refs):
            for j, (px, py) in enumerate(_chips_of(x, y)):
                blk = _half_of(o_ref.at[2 * px + py], 1 - c, shapes[p])
                _remote(blk, blk, send_sems, recv_sems, 3 * p + j, (x, y, c)).wait_recv()
        for cp in cps:
            cp.wait_send()

    return pl.pallas_call(
        body, name=name, in_specs=[ANY] * n, out_specs=[ANY] * n, out_shape=[jax.ShapeDtypeStruct(g.shape, g.dtype) for g in gathered],
        input_output_aliases={p: p for p in range(n)},
        scratch_shapes=[pltpu.SemaphoreType.DMA((3 * n,)), pltpu.SemaphoreType.DMA((3 * n,))],
    )(*gathered)


def _scatter_rider(sums):
    def copies(a_refs, o_refs, send_sems, recv_sems):
        x, y, c = _place()
        m = 2 * x + y
        return [_remote(a_ref.at[2 * px + py], o_ref.at[m], send_sems, recv_sems, 3 * p + j, (px, py, c))
                for p, (a_ref, o_ref) in enumerate(zip(a_refs, o_refs)) for j, (px, py) in enumerate(_chips_of(x, y))]

    def start(a_refs, o_refs, send_sems, recv_sems):
        for cp in copies(a_refs, o_refs, send_sems, recv_sems):
            cp.start()

    def finish(a_refs, o_refs, send_sems, recv_sems):
        x, y, c = _place()
        for p, o_ref in enumerate(o_refs):
            for j, (px, py) in enumerate(_chips_of(x, y)):
                blk = o_ref.at[2 * px + py]
                _remote(blk, blk, send_sems, recv_sems, 3 * p + j, (x, y, c)).wait_recv()
        for cp in copies(a_refs, o_refs, send_sems, recv_sems):
            cp.wait_send()

    return _Rider(sums, [jax.ShapeDtypeStruct(a.shape, a.dtype) for a in sums], 3 * len(sums), start, finish)


def _gather_small(v, *, name):
    r, w = v.shape

    def body(v_ref, o_ref, send_sems, recv_sems):
        x, y, c = _place()
        m = 2 * x + y
        chips = [(1 - x, y), (x, 1 - y), (1 - x, 1 - y)]
        o_ref[m] = v_ref[...]
        sends = [_remote(v_ref, o_ref.at[m], send_sems, recv_sems, j, (px, py, c)) for j, (px, py) in enumerate(chips)]
        for cp in sends:
            cp.start()
        for j, (px, py) in enumerate(chips):
            blk = o_ref.at[2 * px + py]
            _remote(blk, blk, send_sems, recv_sems, j, (x, y, c)).wait_recv()
        for cp in sends:
            cp.wait_send()

    return pl.pallas_call(
        body, name=name, in_specs=[VMEM_SPEC], out_specs=VMEM_SPEC, out_shape=jax.ShapeDtypeStruct((4, r, w), v.dtype),
        scratch_shapes=[pltpu.SemaphoreType.DMA((3,)), pltpu.SemaphoreType.DMA((3,))],
    )(v)


def _swap_halves_list(gs, *, name):
    n = len(gs)

    def body(*refs):
        g_refs, o_refs, send_sems, recv_sems = refs[:n], refs[n:2 * n], refs[2 * n], refs[2 * n + 1]
        x, y, c = _place()
        cps = [_remote(_row_half(g_ref, 1 - c, gs[p].shape[1]), o_ref, send_sems, recv_sems, p, (x, y, 1 - c))
               for p, (g_ref, o_ref) in enumerate(zip(g_refs, o_refs))]
        for cp in cps:
            cp.start()
        for cp in cps:
            cp.wait()

    return pl.pallas_call(
        body, name=name, in_specs=[ANY] * n, out_specs=[ANY] * n,
        out_shape=[jax.ShapeDtypeStruct((g.shape[0], g.shape[1] // 2, g.shape[2]), g.dtype) for g in gs],
        scratch_shapes=[pltpu.SemaphoreType.DMA((n,)), pltpu.SemaphoreType.DMA((n,))],
    )(*gs)


def _join_halves_list(reds, *, name):
    n = len(reds)

    def body(*refs):
        o_refs, send_sems, recv_sems = refs[n:2 * n], refs[2 * n], refs[2 * n + 1]
        x, y, c = _place()
        cps = []
        for p, o_ref in enumerate(o_refs):
            rh = reds[p].shape[0] // 2
            mine = o_ref.at[pl.ds(pl.multiple_of(c * rh, SUBLANES), rh)]
            cp = _remote(mine, mine, send_sems, recv_sems, p, (x, y, 1 - c))
            cp.start()
            cps.append(cp)
        for p, o_ref in enumerate(o_refs):
            rh = reds[p].shape[0] // 2
            other = o_ref.at[pl.ds(pl.multiple_of((1 - c) * rh, SUBLANES), rh)]
            _remote(other, other, send_sems, recv_sems, p, (x, y, c)).wait_recv()
        for cp in cps:
            cp.wait_send()

    return pl.pallas_call(
        body, name=name, in_specs=[ANY] * n, out_specs=[ANY] * n, out_shape=[jax.ShapeDtypeStruct(r.shape, r.dtype) for r in reds],
        input_output_aliases={p: p for p in range(n)},
        scratch_shapes=[pltpu.SemaphoreType.DMA((n,)), pltpu.SemaphoreType.DMA((n,))],
    )(*reds)


def _allreduce_small(v, *, name):
    r, w = v.shape

    def body(v_ref, o_ref, slots, send_sems, recv_sems):
        x, y, c = _place()
        me = 4 * x + 2 * y + c
        slots[me] = v_ref[...]
        peers = [((1 - x) if k & 4 else x, (1 - y) if k & 2 else y, (1 - c) if k & 1 else c) for k in range(1, 8)]
        sends = [_remote(v_ref, slots.at[me], send_sems, recv_sems, k, p) for k, p in enumerate(peers)]
        for cp in sends:
            cp.start()
        for k, (px, py, pc) in enumerate(peers):
            blk = slots.at[4 * px + 2 * py + pc]
            _remote(blk, blk, send_sems, recv_sems, k, (x, y, c)).wait_recv()
        for cp in sends:
            cp.wait_send()
        acc = slots[0]
        for k in range(1, 8):
            acc = acc + slots[k]
        o_ref[...] = acc

    return pl.pallas_call(
        body, name=name, in_specs=[VMEM_SPEC], out_specs=VMEM_SPEC, out_shape=jax.ShapeDtypeStruct(v.shape, v.dtype),
        scratch_shapes=[pltpu.VMEM((8, r, w), F32), pltpu.SemaphoreType.DMA((7,)), pltpu.SemaphoreType.DMA((7,))],
    )(v)


def _row_tile(r):
    return r if r <= 512 else _tile(r, (512, 256, 128, 64, 32, 16))


def _add_half(g, recv, place, *, name):
    n, r, w = g.shape
    tm = _row_tile(r // 2)
    nb = (r // 2) // tm

    def body(place_ref, g_ref, r_ref, o_ref):
        o_ref[...] = (g_ref[...].astype(F32) + r_ref[...].astype(F32)).astype(o_ref.dtype)

    return pl.pallas_call(
        body, name=name,
        grid_spec=pltpu.PrefetchScalarGridSpec(
            num_scalar_prefetch=1, grid=(n, nb),
            in_specs=[pl.BlockSpec((None, tm, w), lambda k, i, p: (k, p[1] * nb + i, 0)), pl.BlockSpec((None, tm, w), lambda k, i, p: (k, i, 0))],
            out_specs=pl.BlockSpec((None, tm, w), lambda k, i, p: (k, i, 0))),
        out_shape=jax.ShapeDtypeStruct(recv.shape, g.dtype), compiler_params=_params(("parallel", "parallel")),
    )(place, g, recv)


def _sum_chips(recv, own, place, *, name):
    n, r, w = recv.shape
    tm = _row_tile(r)
    nb = r // tm

    def body(place_ref, *refs):
        own_ref, o_ref = refs[n], refs[n + 1]
        acc = None
        for k in range(n):
            term = jnp.where(place_ref[0] == k, own_ref[...], refs[k][...]).astype(F32)
            acc = term if acc is None else acc + term
        o_ref[...] = acc

    recv_specs = [pl.BlockSpec((None, tm, w), lambda i, p, k=k: (jnp.where(p[0] == k, (k + 1) % n, k), i, 0)) for k in range(n)]
    return pl.pallas_call(
        body, name=name,
        grid_spec=pltpu.PrefetchScalarGridSpec(
            num_scalar_prefetch=1, grid=(nb,),
            in_specs=recv_specs + [pl.BlockSpec((None, tm, w), lambda i, p: (p[0], i, 0))],
            out_specs=pl.BlockSpec((tm, w), lambda i, p: (p[1] * nb + i, 0))),
        out_shape=jax.ShapeDtypeStruct((2 * r, w), F32), compiler_params=_params(("parallel",)),
    )(place, *([recv] * n), own)


def _adamw(w, g, m, v, *, name):
    shape = w.shape
    cols = shape[-1]
    rows = math.prod(shape[:-1])
    tm = _tile(rows, (256, 128, 64, 32, 16, 8))
    c1 = 1.0 - ADAM_B1 ** ADAM_STEP
    c2 = 1.0 - ADAM_B2 ** ADAM_STEP

    def fn(i, nrt, wv, gv, mv, vv):
        mn = ADAM_B1 * mv + (1.0 - ADAM_B1) * gv
        vn = ADAM_B2 * vv + (1.0 - ADAM_B2) * (gv * gv)
        delta = -ADAM_LR * ((mn / c1) / (jnp.sqrt(vn / c2) + ADAM_EPS) + ADAM_WD * wv)
        return (delta, mn, vn), ()

    outs = _rows(fn, name=name, s=rows, tm=tm, ins=[("row", t.reshape(rows, cols), cols, _c0) for t in (w, g, m, v)],
                 outs=[(cols, cols, _c0, F32)] * 3)
    return [o.reshape(shape) for o in outs]


WEIGHTS = ["mix_norm", "ffn_norm", "ffn_w_gu", "ffn_w_down", "conv_w_in", "conv_w_dw", "conv_w_out", "fox_w_in", "fox_b_f", "fox_q_gain",
           "fox_k_gain", "fox_w_out", "ssd_w_in", "ssd_conv_w", "ssd_conv_b", "ssd_dt_bias", "ssd_a_log", "ssd_d", "ssd_norm_w", "ssd_w_out"]
SHARD_AXIS = {"ffn_w_gu": 2, "ffn_w_down": 1, "conv_w_in": 2, "conv_w_dw": 2, "conv_w_out": 1, "fox_w_in": 2, "fox_w_out": 1, "ssd_w_in": 2,
              "ssd_conv_w": 2, "ssd_conv_b": 1, "ssd_norm_w": 1, "ssd_w_out": 1}
BIG = ["ffn_w_gu", "ffn_w_down", "conv_w_in", "conv_w_out", "fox_w_in", "fox_w_out", "ssd_w_in", "ssd_w_out"]
SMALL_SHARDED = ["conv_w_dw", "ssd_conv_w", "ssd_conv_b", "ssd_norm_w"]
N_CHIPS = 4


def _pack_flat(parts, pad_to):
    flat = [p.reshape(-1) for p in parts]
    offs, n = [], 0
    for f in flat:
        offs.append(n)
        n += f.shape[0]
    total = -(-n // pad_to) * pad_to
    if total > n:
        flat.append(jnp.zeros((total - n,), flat[0].dtype))
    return jnp.concatenate(flat).reshape(-1, LANES), offs


def kernel(x, mix_norm, ffn_norm, ffn_w_gu, ffn_w_down, conv_w_in, conv_w_dw, conv_w_out, fox_w_in, fox_b_f, fox_q_gain, fox_k_gain, fox_w_out, ssd_w_in, ssd_conv_w, ssd_conv_b, ssd_dt_bias, ssd_a_log, ssd_d, ssd_norm_w, ssd_w_out, loss_target, m_mix_norm, m_ffn_norm, m_ffn_w_gu, m_ffn_w_down, m_conv_w_in, m_conv_w_dw, m_conv_w_out, m_fox_w_in, m_fox_b_f, m_fox_q_gain, m_fox_k_gain, m_fox_w_out, m_ssd_w_in, m_ssd_conv_w, m_ssd_conv_b, m_ssd_dt_bias, m_ssd_a_log, m_ssd_d, m_ssd_norm_w, m_ssd_w_out, v_mix_norm, v_ffn_norm, v_ffn_w_gu, v_ffn_w_down, v_conv_w_in, v_conv_w_dw, v_conv_w_out, v_fox_w_in, v_fox_b_f, v_fox_q_gain, v_fox_k_gain, v_fox_w_out, v_ssd_w_in, v_ssd_conv_w, v_ssd_conv_b, v_ssd_dt_bias, v_ssd_a_log, v_ssd_d, v_ssd_norm_w, v_ssd_w_out):
    w = dict(zip(WEIGHTS, (mix_norm, ffn_norm, ffn_w_gu, ffn_w_down, conv_w_in, conv_w_dw, conv_w_out, fox_w_in, fox_b_f, fox_q_gain, fox_k_gain,
                           fox_w_out, ssd_w_in, ssd_conv_w, ssd_conv_b, ssd_dt_bias, ssd_a_log, ssd_d, ssd_norm_w, ssd_w_out)))
    m1 = dict(zip(WEIGHTS, (m_mix_norm, m_ffn_norm, m_ffn_w_gu, m_ffn_w_down, m_conv_w_in, m_conv_w_dw, m_conv_w_out, m_fox_w_in, m_fox_b_f,
                            m_fox_q_gain, m_fox_k_gain, m_fox_w_out, m_ssd_w_in, m_ssd_conv_w, m_ssd_conv_b, m_ssd_dt_bias, m_ssd_a_log, m_ssd_d,
                            m_ssd_norm_w, m_ssd_w_out)))
    m2 = dict(zip(WEIGHTS, (v_mix_norm, v_ffn_norm, v_ffn_w_gu, v_ffn_w_down, v_conv_w_in, v_conv_w_dw, v_conv_w_out, v_fox_w_in, v_fox_b_f,
                            v_fox_q_gain, v_fox_k_gain, v_fox_w_out, v_ssd_w_in, v_ssd_conv_w, v_ssd_conv_b, v_ssd_dt_bias, v_ssd_a_log, v_ssd_d,
                            v_ssd_norm_w, v_ssd_w_out)))
    cx, cy, cc = _place()
    chip = 2 * cx + cy

    place = jnp.stack([chip, cc]).astype(jnp.int32)
    depth = mix_norm.shape[0]
    attention = next((i for i in range(depth) if i % 3 == 1), depth)

    def carrier_of(layer):
        return None if layer == 0 else ("ffn", layer - 1) if layer <= attention else ("attention", attention)

    early, late, cm = {}, {}, {n: [] for n in BIG}
    for n in BIG:
        wb = w[n].astype(WIRE_DTYPE)
        keys = [carrier_of(_model_layer(n, l)) for l in range(wb.shape[0])]
        for key in dict.fromkeys(keys):
            first, count = keys.index(key), keys.count(key)
            if key is None:
                early[n] = wb[first:first + count]
            else:
                late.setdefault(key, {})[n] = (wb[first:first + count], first)
    gathered = _with_own(_gather_list(list(early.values()), name="gather_weights"), list(early.values()), chip)
    for n, g_ in zip(early, gathered):
        cm[n].append((g_, 0))
    sp, soffs = _pack_flat([w[n] for n in SMALL_SHARDED], SUBLANES * LANES)
    sgath = _gather_small(sp, name="gather_small").reshape(N_CHIPS, -1)
    full = {n: w[n] for n in WEIGHTS if n not in SHARD_AXIS}
    for n, off in zip(SMALL_SHARDED, soffs):
        full[n] = jnp.concatenate([sgath[j, off:off + w[n].size].reshape(w[n].shape) for j in range(N_CHIPS)], axis=SHARD_AXIS[n])

    loss, gx, grads, reds = _local_step(x[0], loss_target[0], full, cm, late, place)
    loss = lax.psum(loss, ("x", "y", "c"))

    small_names = [n for n in WEIGHTS if n not in BIG]
    sm, smoffs = _pack_flat([grads[n] for n in small_names], SUBLANES * LANES)
    sred = _allreduce_small(sm, name="allreduce_small").reshape(-1)

    g = {n: jnp.stack([reds[(n, l)] for l in range(w[n].shape[0])]).reshape(w[n].shape) for n in BIG}
    for n, off in zip(small_names, smoffs):
        fullg = sred[off:off + grads[n].size].reshape(grads[n].shape)
        if n in SHARD_AXIS:
            ax = SHARD_AXIS[n]
            fullg = lax.dynamic_slice_in_dim(fullg, chip * w[n].shape[ax], w[n].shape[ax], axis=ax)
        g[n] = fullg

    deltas, new_m, new_v = [], [], []
    for n in WEIGHTS:
        dl, mn, vn = _adamw(w[n], g[n], m1[n], m2[n], name=f"adamw_{n}")
        deltas.append(dl)
        new_m.append(mn)
        new_v.append(vn)
    return (loss, gx[None], *[g[n] for n in WEIGHTS], *deltas, *new_m, *new_v)
```

```python
import functools
import math

import jax
import jax.numpy as jnp
from jax import lax
from jax.experimental import pallas as pl
from jax.experimental.pallas import tpu as pltpu

F32 = jnp.float32
MM_DTYPE = jnp.bfloat16
ACT_DTYPE = jnp.bfloat16
WIRE_DTYPE = jnp.bfloat16

RMS_EPS = 1e-6
HEAD = 64
SSM_STATE = 128
SSM_CHUNK = 128
LANES = 128
SUBLANES = 8
VMEM_LIMIT = 48 * 1024 * 1024

ADAM_LR, ADAM_B1, ADAM_B2, ADAM_EPS, ADAM_WD, ADAM_STEP = 0.001, 0.9, 0.999, 1e-08, 0.01, 10

HI = lax.Precision.HIGHEST
MESH = pl.DeviceIdType.MESH


def _tile(dim, prefs):
    for p in prefs:
        if dim % p == 0:
            return p
    return dim


def _params(sem):
    return pltpu.CompilerParams(dimension_semantics=sem, vmem_limit_bytes=VMEM_LIMIT)


def _sigmoid(x):
    return 1.0 / (1.0 + jnp.exp(-x))


def _softplus(x):
    return jnp.maximum(x, 0.0) + jnp.log(1.0 + jnp.exp(-jnp.abs(x)))


TILES = (1024, 1408, 768, 512, 256, 128)
MIN_STEP_WORK = 1 << 30


def _mm(a, b, *, ta=False, tb=False, add=None, out_dtype=F32, name, b_layer=None, out_shard=None, inter=None, rider=None):
    ka, m = (a.shape[0], a.shape[1]) if ta else (a.shape[1], a.shape[0])
    if b_layer is None:
        kb, n = (b.shape[1], b.shape[0]) if tb else (b.shape[0], b.shape[1])
        ns = None
    else:
        ns = b.shape[3]
        kb, n = (b.shape[0] * ns, b.shape[2]) if tb else (b.shape[2], b.shape[0] * ns)
    assert ka == kb, (a.shape, b.shape, ta, tb)
    k = ka
    tm = _tile(m, (1408, 1024, 512, 256, 128) if ta else (512, 256, 128))
    tn = _tile(n, TILES)
    tk = _tile(k, TILES)
    if inter:
        segs, bw = inter
        if tb:
            tk = bw
        else:
            tn = bw
        tps = ((k if tb else n) // bw) // segs
        col = lambda q: ((q % segs) * tps + q // segs) * bw
    else:
        col = lambda q: q * (tk if tb else tn)
    def vmem(tm_, tk_):
        out_b = jnp.dtype(out_dtype).itemsize * 2 + (8 if add is not None else 0) + 4
        return 2 * tk_ * (tm_ * a.dtype.itemsize + tn * b.dtype.itemsize) + tm_ * tn * out_b

    if ta:
        while tk * 2 <= k and k % (tk * 2) == 0 and tm * tn * tk < MIN_STEP_WORK and vmem(tm, tk * 2) < VMEM_LIMIT * 3 // 4:
            tk *= 2
    else:
        while tm * 2 <= m and m % (tm * 2) == 0 and tm * tn * tk < MIN_STEP_WORK and vmem(tm * 2, tk) < VMEM_LIMIT * 3 // 4:
            tm *= 2
    nk = k // tk
    a_spec = pl.BlockSpec((tk, tm), lambda i, j, q: (q, i)) if ta else pl.BlockSpec((tm, tk), lambda i, j, q: (i, q))
    if b_layer is None:
        b_spec = pl.BlockSpec((tn, tk), lambda i, j, q: (j, q)) if tb else pl.BlockSpec((tk, tn), lambda i, j, q: (q, j))
    elif tb:
        b_spec = pl.BlockSpec((None, None, tn, tk), lambda i, j, q: (col(q) // ns, b_layer, j, (col(q) % ns) // tk))
    else:
        b_spec = pl.BlockSpec((None, None, tk, tn), lambda i, j, q: (col(j) // ns, b_layer, q, (col(j) % ns) // tn))
    if out_shard:
        assert ta and add is None
        o_spec = pl.BlockSpec((None, tm, tn), lambda i, j, q: (col(j) // out_shard, i, (col(j) % out_shard) // tn))
        o_shape = jax.ShapeDtypeStruct((N_CHIPS, m, out_shard), out_dtype)
    else:
        o_spec = pl.BlockSpec((tm, tn), lambda i, j, q: (i, j))
        o_shape = jax.ShapeDtypeStruct((m, n), out_dtype)
    dims = (((0 if ta else 1,), (1 if tb else 0,)), ((), ()))
    has_add = add is not None

    def body(*refs):
        a_ref, b_ref = refs[0], refs[1]
        o_ref = refs[2 + has_add]
        p = lax.dot_general(a_ref[...].astype(MM_DTYPE), b_ref[...].astype(MM_DTYPE), dims, preferred_element_type=F32)

        def finish(acc):
            if has_add:
                acc = acc + refs[2][...].astype(F32)
            o_ref[...] = acc.astype(out_dtype)

        if nk == 1:
            finish(p)
        else:
            acc_ref = refs[3 + has_add]
            q = pl.program_id(2)

            @pl.when(q == 0)
            def _():
                acc_ref[...] = p

            @pl.when(q > 0)
            def _():
                acc_ref[...] += p

            @pl.when(q == nk - 1)
            def _():
                finish(acc_ref[...])

    args = [a, b] + ([add] if has_add else [])
    in_specs = [a_spec, b_spec] + ([o_spec] if has_add else [])
    grid = (m // tm, n // tn, nk)
    r_in, r_out, r_shapes, r_scratch, r_args = _rider_parts(rider)
    outs = pl.pallas_call(
        _carry(body, rider, 0, len(args), 1, grid), name=name, grid=grid, in_specs=in_specs + r_in, out_specs=[o_spec] + r_out,
        out_shape=[o_shape] + r_shapes, scratch_shapes=([pltpu.VMEM((tm, tn), F32)] if nk > 1 else []) + r_scratch,
        compiler_params=_params(("arbitrary",) * 3 if rider else ("parallel", "parallel", "arbitrary")),
    )(*args, *r_args)
    return (outs[0], list(outs[1:])) if rider else outs[0]


def _rows(fn, *, name, s, tm, ncol=1, ins, outs, accs=()):
    nrt = s // tm
    hb = tm // SUBLANES
    in_specs, args = [], []
    for spec in ins:
        kind, arr = spec[0], spec[1]
        if kind == "full":
            in_specs.append(pl.BlockSpec(arr.shape, lambda j, i: (0, 0)))
        elif kind == "col":
            _, _, bw, cmap = spec
            in_specs.append(pl.BlockSpec((arr.shape[0], bw), lambda j, i, cmap=cmap: (0, cmap(j))))
        elif kind == "row":
            _, _, bw, cmap = spec
            in_specs.append(pl.BlockSpec((tm, bw), lambda j, i, cmap=cmap: (i, cmap(j))))
        elif kind == "prev":
            _, _, bw, cmap = spec
            in_specs.append(pl.BlockSpec((SUBLANES, bw), lambda j, i, cmap=cmap: (jnp.maximum(i * hb - 1, 0), cmap(j))))
        elif kind == "next":
            _, _, bw, cmap = spec
            in_specs.append(pl.BlockSpec((SUBLANES, bw), lambda j, i, cmap=cmap: (jnp.minimum((i + 1) * hb, s // SUBLANES - 1), cmap(j))))
        else:
            raise ValueError(kind)
        args.append(arr)
    out_specs, out_shape = [], []
    for w, bw, cmap, dt in outs:
        out_specs.append(pl.BlockSpec((tm, bw), lambda j, i, cmap=cmap: (i, cmap(j))))
        out_shape.append(jax.ShapeDtypeStruct((s, w), dt))
    for r, w, bw, cmap in accs:
        out_specs.append(pl.BlockSpec((r, bw), lambda j, i, cmap=cmap: (0, cmap(j))))
        out_shape.append(jax.ShapeDtypeStruct((r, w), F32))
    n_in, n_out, n_acc = len(ins), len(outs), len(accs)

    def body(*refs):
        i = pl.program_id(1)
        vals = [r[...] for r in refs[:n_in]]
        o_vals, a_vals = fn(i, nrt, *vals)
        assert len(o_vals) == n_out and len(a_vals) == n_acc
        for r, v in zip(refs[n_in:n_in + n_out], o_vals):
            r[...] = v.astype(r.dtype)
        for r, v in zip(refs[n_in + n_out:], a_vals):
            @pl.when(i == 0)
            def _(r=r, v=v):
                r[...] = v.astype(F32)

            @pl.when(i > 0)
            def _(r=r, v=v):
                r[...] += v.astype(F32)

    res = pl.pallas_call(
        body, name=name, grid=(ncol, nrt), in_specs=in_specs, out_specs=out_specs, out_shape=out_shape,
        compiler_params=_params(("parallel", "arbitrary" if accs else "parallel")),
    )(*args)
    return res


def _c0(j):
    return 0


def _cj(j):
    return j


def _gmean(v, gs):
    w = v.shape[-1]
    tile = max(gs, LANES)
    r = lax.broadcasted_iota(jnp.int32, (tile, tile), 0) // gs
    c = lax.broadcasted_iota(jnp.int32, (tile, tile), 1) // gs
    g = jnp.where(r == c, 1.0 / gs, 0.0).astype(F32)
    parts = [jnp.dot(v[:, t * tile:(t + 1) * tile], g, precision=HI, preferred_element_type=F32) for t in range(w // tile)]
    return parts[0] if len(parts) == 1 else jnp.concatenate(parts, axis=1)


def _sum_rows(v):
    return jnp.sum(v, axis=0, keepdims=True)


def _rms(x, w, *, name):
    s, d = x.shape

    def fn(i, nrt, xv, wv):
        r = lax.rsqrt(jnp.mean(xv * xv, axis=-1, keepdims=True) + RMS_EPS)
        return (xv * r * wv,), ()

    (h,) = _rows(fn, name=name, s=s, tm=_tile(s, (512, 256, 128)), ins=[("row", x, d, _c0), ("full", w.reshape(1, d))],
                 outs=[(d, d, _c0, ACT_DTYPE)])
    return h


def _rms_bwd(x, w, dh, dx_in, *, name):
    s, d = x.shape

    def fn(i, nrt, xv, wv, dhv, dxi):
        r = lax.rsqrt(jnp.mean(xv * xv, axis=-1, keepdims=True) + RMS_EPS)
        xh = xv * r
        g = dhv * wv
        dx = dxi + r * (g - xh * jnp.mean(g * xh, axis=-1, keepdims=True))
        return (dx, dx), (_sum_rows(dhv * xh),)

    dx, dxb, dw = _rows(fn, name=name, s=s, tm=_tile(s, (512, 256, 128)),
                        ins=[("row", x, d, _c0), ("full", w.reshape(1, d)), ("row", dh, d, _c0), ("row", dx_in, d, _c0)],
                        outs=[(d, d, _c0, F32), (d, d, _c0, MM_DTYPE)], accs=[(1, d, d, _c0)])
    return dx, dxb, dw.reshape(d)


def _ffn_up(h, w_gu, layer, bw, *, name, rider=None):
    m, k = h.shape
    ns = w_gu.shape[3]
    f = N_CHIPS * ns // 2
    tm = _tile(m, (512, 256, 128))

    def w_spec(first):
        return pl.BlockSpec((None, None, k, bw), lambda t, i: ((first + t * bw) // ns, layer, 0, ((first + t * bw) % ns) // bw))

    def body(h_ref, wg_ref, wu_ref, gu_ref, a_ref):
        hv = h_ref[...].astype(MM_DTYPE)
        g = jnp.dot(hv, wg_ref[...].astype(MM_DTYPE), preferred_element_type=F32)
        u = jnp.dot(hv, wu_ref[...].astype(MM_DTYPE), preferred_element_type=F32)
        gu_ref[...] = jnp.concatenate([g, u], axis=1).astype(gu_ref.dtype)
        a_ref[...] = (g * _sigmoid(g) * u).astype(a_ref.dtype)

    grid = (f // bw, m // tm)
    r_in, r_out, r_shapes, r_scratch, r_args = _rider_parts(rider)
    outs = pl.pallas_call(
        _carry(body, rider, 0, 3, 2, grid), name=name, grid=grid,
        in_specs=[pl.BlockSpec((tm, k), lambda t, i: (i, 0)), w_spec(0), w_spec(f)] + r_in,
        out_specs=[pl.BlockSpec((tm, 2 * bw), lambda t, i: (i, t)), pl.BlockSpec((tm, bw), lambda t, i: (i, t))] + r_out,
        out_shape=[jax.ShapeDtypeStruct((m, 2 * f), ACT_DTYPE), jax.ShapeDtypeStruct((m, f), ACT_DTYPE)] + r_shapes,
        scratch_shapes=r_scratch, compiler_params=_params(("arbitrary", "arbitrary") if rider else ("parallel", "parallel")),
    )(h, w_gu, w_gu, *r_args)
    return outs[0], outs[1], list(outs[2:])


def _ffn_back(dx, w_down, gu, bw, *, name):
    m, d = dx.shape
    f = w_down.shape[0]
    tm = _tile(m, (512, 256, 128))

    rc = _tile(tm, (256, 128))

    def body(dx_ref, w_ref, gu_ref, o_ref):
        wv = w_ref[...].astype(MM_DTYPE)
        for r in range(tm // rc):
            rows = slice(r * rc, (r + 1) * rc)
            da = _nt(dx_ref[rows, :].astype(MM_DTYPE), wv)
            gv, uv = gu_ref[rows, :bw].astype(F32), gu_ref[rows, bw:].astype(F32)
            sg = _sigmoid(gv)
            o_ref[rows, :bw] = (da * uv * sg * (1.0 + gv * (1.0 - sg))).astype(o_ref.dtype)
            o_ref[rows, bw:] = (da * gv * sg).astype(o_ref.dtype)

    return pl.pallas_call(
        body, name=name, grid=(f // bw, m // tm),
        in_specs=[pl.BlockSpec((tm, d), lambda t, i: (i, 0)), pl.BlockSpec((bw, d), lambda t, i: (t, 0)), pl.BlockSpec((tm, 2 * bw), lambda t, i: (i, t))],
        out_specs=pl.BlockSpec((tm, 2 * bw), lambda t, i: (i, t)), out_shape=jax.ShapeDtypeStruct((m, 2 * f), ACT_DTYPE),
        compiler_params=_params(("parallel", "parallel")),
    )(dx, w_down, gu)


def _loss_head(x, tgt, *, name):
    s, d = x.shape

    def fn(i, nrt, xv, tv):
        diff = xv - tv
        part = 0.5 * jnp.sum(diff * diff) / d
        return (diff / d, diff / d), (jnp.full((1, LANES), part, F32),)

    dy, dyb, loss = _rows(fn, name=name, s=s, tm=_tile(s, (512, 256, 128)),
                          ins=[("row", x, d, _c0), ("row", tgt, d, _c0)],
                          outs=[(d, d, _c0, F32), (d, d, _c0, MM_DTYPE)], accs=[(1, LANES, LANES, _c0)])
    return loss[0, 0], dy, dyb


def _shift_down(ext, j, tm):
    src = pltpu.roll(ext, j, 0) if j else ext
    return src[SUBLANES:SUBLANES + tm]


def _shift_up(ext, j, tm):
    return ext[:tm] if j == 0 else pltpu.roll(ext, ext.shape[0] - j, 0)[:tm]


def _gconv_fwd(p, w, bw, *, name):
    s, d = p.shape[0], p.shape[1] // 3
    kw = w.shape[0]
    tm = _tile(s, (512, 256, 128))

    def fn(i, nrt, pv, pp, wv):
        pv, pp = pv.astype(F32), pp.astype(F32)
        cv = pv[:, bw:2 * bw] * pv[:, 2 * bw:]
        pcv = jnp.where(i == 0, 0.0, pp[:, bw:2 * bw] * pp[:, 2 * bw:])
        ext = jnp.concatenate([pcv, cv], axis=0)
        u = sum(wv[k:k + 1, :] * _shift_down(ext, kw - 1 - k, tm) for k in range(kw))
        return (pv[:, :bw] * u,), ()

    (o,) = _rows(fn, name=name, s=s, tm=tm, ncol=d // bw, ins=[("row", p, 3 * bw, _cj), ("prev", p, 3 * bw, _cj), ("col", w, bw, _cj)],
                 outs=[(d, bw, _cj, ACT_DTYPE)])
    return o


def _gconv_bwd(p, w, do, bw, *, name):
    s, d = do.shape
    kw = w.shape[0]
    tm = _tile(s, (512, 256, 128))

    def fn(i, nrt, pv, pp, pn, dov, ndo, wv):
        pv, pp, dov = pv.astype(F32), pp.astype(F32), dov.astype(F32)
        bv, cv_, vv = pv[:, :bw], pv[:, bw:2 * bw], pv[:, 2 * bw:]
        cv = cv_ * vv
        pcv = jnp.where(i == 0, 0.0, pp[:, bw:2 * bw] * pp[:, 2 * bw:])
        ext = jnp.concatenate([pcv, cv], axis=0)
        shifted = [_shift_down(ext, kw - 1 - k, tm) for k in range(kw)]
        u = sum(wv[k:k + 1, :] * shifted[k] for k in range(kw))
        db = dov * u
        du = dov * bv
        ndu = jnp.where(i == nrt - 1, 0.0, ndo.astype(F32) * pn[:, :bw].astype(F32))
        ext2 = jnp.concatenate([du, ndu], axis=0)
        dcv = sum(wv[k:k + 1, :] * _shift_up(ext2, kw - 1 - k, tm) for k in range(kw))
        dw = jnp.concatenate([_sum_rows(du * shifted[k]) for k in range(kw)], axis=0)
        return (jnp.concatenate([db, dcv * vv, dcv * cv_], axis=1),), (dw,)

    dp, dw = _rows(fn, name=name, s=s, tm=tm, ncol=d // bw,
                   ins=[("row", p, 3 * bw, _cj), ("prev", p, 3 * bw, _cj), ("next", p, 3 * bw, _cj), ("row", do, bw, _cj),
                        ("next", do, bw, _cj), ("col", w, bw, _cj)],
                   outs=[(3 * d, 3 * bw, _cj, ACT_DTYPE)], accs=[(kw, d, bw, _cj)])
    return dp, dw


def _sconv_fwd(x, w, bias, *, name):
    s, d = x.shape
    kw = w.shape[0]
    bw = _tile(d, (512, 256, 128))
    tm = _tile(s, (512, 256, 128))

    def fn(i, nrt, xv, px, wv, bsv):
        xv = xv.astype(F32)
        ext = jnp.concatenate([jnp.where(i == 0, 0.0, px.astype(F32)), xv], axis=0)
        pre = sum(wv[k:k + 1, :] * _shift_down(ext, kw - 1 - k, tm) for k in range(kw)) + bsv
        return (pre * _sigmoid(pre),), ()

    (o,) = _rows(fn, name=name, s=s, tm=tm, ncol=d // bw,
                 ins=[("row", x, bw, _cj), ("prev", x, bw, _cj), ("col", w, bw, _cj), ("col", bias.reshape(1, d), bw, _cj)],
                 outs=[(d, bw, _cj, ACT_DTYPE)])
    return o


def _sconv_bwd(x, w, bias, dact, *, name):
    s, d = x.shape
    kw = w.shape[0]
    bw = _tile(d, (512, 256, 128))
    tm = _tile(s, (512, 256, 128))

    def fn(i, nrt, xv, px, nx, dav, nda, wv, bsv):
        xv = xv.astype(F32)
        ext = jnp.concatenate([jnp.where(i == 0, 0.0, px.astype(F32)), xv, nx.astype(F32)], axis=0)
        rows_e = tm + SUBLANES
        pre_e = sum(wv[k:k + 1, :] * _shift_down(ext, kw - 1 - k, rows_e) for k in range(kw)) + bsv
        da_e = jnp.concatenate([dav.astype(F32), jnp.where(i == nrt - 1, 0.0, nda.astype(F32))], axis=0)
        sg = _sigmoid(pre_e)
        dpre_e = da_e * sg * (1.0 + pre_e * (1.0 - sg))
        dx = sum(wv[k:k + 1, :] * _shift_up(dpre_e, kw - 1 - k, tm) for k in range(kw))
        dpre = dpre_e[:tm]
        dw = jnp.concatenate([_sum_rows(dpre * _shift_down(ext, kw - 1 - k, tm)) for k in range(kw)], axis=0)
        return (dx,), (dw, _sum_rows(dpre))

    dx, dw, db = _rows(fn, name=name, s=s, tm=tm, ncol=d // bw,
                       ins=[("row", x, bw, _cj), ("prev", x, bw, _cj), ("next", x, bw, _cj), ("row", dact, bw, _cj),
                            ("next", dact, bw, _cj), ("col", w, bw, _cj), ("col", bias.reshape(1, d), bw, _cj)],
                       outs=[(d, bw, _cj, ACT_DTYPE)], accs=[(kw, d, bw, _cj), (1, d, bw, _cj)])
    return dx, dw, db.reshape(d)


def _tri(n, reverse):
    r = lax.broadcasted_iota(jnp.int32, (n, n), 0)
    c = lax.broadcasted_iota(jnp.int32, (n, n), 1)
    return jnp.where((c >= r) if reverse else (c <= r), 1.0, 0.0).astype(F32)


def _cumsum_rows(x, *, reverse, name):
    s, w = x.shape
    ch = _tile(s, (256, 128))
    n = s // ch

    def body(x_ref, o_ref, carry):
        i = pl.program_id(0)

        @pl.when(i == 0)
        def _():
            carry[...] = jnp.zeros_like(carry)

        out = jnp.dot(_tri(ch, reverse), x_ref[...], precision=HI, preferred_element_type=F32) + carry[...]
        o_ref[...] = out
        carry[...] = out[0:1, :] if reverse else out[ch - 1:ch, :]

    imap = (lambda i: (n - 1 - i, 0)) if reverse else (lambda i: (i, 0))
    return pl.pallas_call(
        body, name=name, grid=(n,), in_specs=[pl.BlockSpec((ch, w), imap)], out_specs=pl.BlockSpec((ch, w), imap),
        out_shape=jax.ShapeDtypeStruct((s, w), F32), scratch_shapes=[pltpu.VMEM((1, w), F32)],
        compiler_params=_params(("arbitrary",)),
    )(x)


def _fox_prep(q, k, f, gq, gk, bf, *, name):
    s, d = q.shape
    scale = HEAD ** -0.5

    def fn(i, nrt, qv, kv, fv, gqv, gkv, bfv):
        qv, kv = qv.astype(F32), kv.astype(F32)
        qn = qv * lax.rsqrt(_gmean(qv * qv, HEAD) + RMS_EPS) * gqv * scale
        kn = kv * lax.rsqrt(_gmean(kv * kv, HEAD) + RMS_EPS) * gkv
        z = fv + bfv
        logf = jnp.minimum(z, 0.0) - jnp.log(1.0 + jnp.exp(-jnp.abs(z)))
        return (qn, kn, logf), ()

    return _rows(fn, name=name, s=s, tm=_tile(s, (512, 256, 128)),
                 ins=[("row", q, d, _c0), ("row", k, d, _c0), ("row", f, LANES, _c0), ("full", gq), ("full", gk), ("full", bf)],
                 outs=[(d, d, _c0, ACT_DTYPE), (d, d, _c0, ACT_DTYPE), (LANES, LANES, _c0, F32)])


def _fox_prep_bwd(q, k, f, gq, gk, bf, dqs, dkn, dlogf, *, name):
    s, d = q.shape
    scale = HEAD ** -0.5

    def fn(i, nrt, qv, kv, fv, gqv, gkv, bfv, dqv, dkv, dlf):
        outs, accs = [], []
        for xv, gv, dv, sc in ((qv, gqv, dqv, scale), (kv, gkv, dkv, 1.0)):
            xv, dv = xv.astype(F32), dv.astype(F32) * sc
            r = lax.rsqrt(_gmean(xv * xv, HEAD) + RMS_EPS)
            xh = xv * r
            g = dv * gv
            outs.append(r * (g - xh * _gmean(g * xh, HEAD)))
            accs.append(_sum_rows(dv * xh))
        z = fv + bfv
        df = dlf * _sigmoid(-z)
        outs.append(df)
        accs.append(_sum_rows(df))
        return outs, accs

    return _rows(fn, name=name, s=s, tm=_tile(s, (512, 256, 128)),
                 ins=[("row", q, d, _c0), ("row", k, d, _c0), ("row", f, LANES, _c0), ("full", gq), ("full", gk), ("full", bf),
                      ("row", dqs, d, _c0), ("row", dkn, d, _c0), ("row", dlogf, LANES, _c0)],
                 outs=[(d, d, _c0, ACT_DTYPE), (d, d, _c0, ACT_DTYPE), (LANES, LANES, _c0, ACT_DTYPE)],
                 accs=[(1, d, d, _c0), (1, d, d, _c0), (1, LANES, LANES, _c0)])


def _head_masks(shape):
    lane = lax.broadcasted_iota(jnp.int32, shape, len(shape) - 1)
    return lane < HEAD, lane >= HEAD


def _pick_lane(blk, idx):
    lane = lax.broadcasted_iota(jnp.int32, blk.shape, 1)
    return jnp.sum(jnp.where(lane == idx, blk, 0.0), axis=1, keepdims=True)


def _pick_row(blk, idx):
    sub = lax.broadcasted_iota(jnp.int32, blk.shape, 0)
    return jnp.sum(jnp.where(sub == idx, blk, 0.0), axis=0, keepdims=True)


def _fox_aug(qs, kn, cum, *, name):
    s, d = qs.shape
    hp = d // LANES

    def fn(i, nrt, qv, kv, cv):
        lane = lax.broadcasted_iota(jnp.int32, (qv.shape[0], LANES), 1)
        outs = [[], [], [], []]
        for p in range(hp):
            qt, kt = qv[:, p * LANES:(p + 1) * LANES], kv[:, p * LANES:(p + 1) * LANES]
            for h in range(2):
                mine = (lane < HEAD) if h == 0 else (lane >= HEAD)
                a0 = HEAD if h == 0 else 0
                c = cv[:, 2 * p + h:2 * p + h + 1]
                hi = c.astype(ACT_DTYPE).astype(F32)
                mid = (c - hi).astype(ACT_DTYPE).astype(F32)
                lo = (c - hi - mid).astype(ACT_DTYPE).astype(F32)
                ones = jnp.where((lane >= a0) & (lane < a0 + 3), 1.0, 0.0)
                kx = jnp.where(lane == a0, -hi, jnp.where(lane == a0 + 1, -mid, jnp.where(lane == a0 + 2, -lo, 0.0)))
                outs[h].append(jnp.where(mine, qt.astype(F32), ones))
                outs[2 + h].append(jnp.where(mine, kt.astype(F32), kx))
        return [jnp.concatenate(o, axis=1) for o in outs], ()

    return _rows(fn, name=name, s=s, tm=_tile(s, (512, 256, 128)), ins=[("row", qs, d, _c0), ("row", kn, d, _c0), ("row", cum, LANES, _c0)],
                 outs=[(d, d, _c0, ACT_DTYPE)] * 4)


def _tri_tables(nq, by_key):
    import numpy as np
    pairs = [(qi, kj) for kj in range(nq) for qi in range(kj, nq)] if by_key else [(qi, kj) for qi in range(nq) for kj in range(qi + 1)]
    return jnp.asarray(np.array([p[0] for p in pairs], np.int32)), jnp.asarray(np.array([p[1] for p in pairs], np.int32))


def _nt(a, b):
    return lax.dot_general(a, b, (((1,), (1,)), ((), ())), preferred_element_type=F32)


def _tn(a, b):
    return lax.dot_general(a, b, (((0,), (0,)), ((), ())), preferred_element_type=F32)


ATTN_BLOCKS = (1024, 512, 256, 128)


def _fox_dd(do, o, *, name):
    s, d = do.shape

    def fn(i, nrt, dov, ov):
        return (_reduce_heads(dov.astype(F32) * ov.astype(F32), d // HEAD, HEAD),), ()

    (dd,) = _rows(fn, name=name, s=s, tm=_tile(s, (512, 256, 128)), ins=[("row", do, d, _c0), ("row", o, d, _c0)],
                  outs=[(LANES, LANES, _c0, F32)])
    return dd


def _pair_rows(a, nh):
    s = a.shape[0]
    t = a[:, :nh].T.reshape(nh // 2, 2, s)
    return jnp.pad(t, ((0, 0), (0, SUBLANES - 2), (0, 0)))


def _rows01(r0, r1):
    sub = lax.broadcasted_iota(jnp.int32, (SUBLANES, r0.shape[1]), 0)
    return jnp.where(sub == 0, r0, jnp.where(sub == 1, r1, 0.0))


def _rider_parts(rider):
    if rider is None:
        return [], [], [], [], []
    return [ANY] * len(rider.arrays), [ANY] * len(rider.out_shapes), rider.out_shapes, rider.scratch, rider.arrays


def _fox_fwd_t(q_aug, k_aug, v, *, name, rider=None):
    s, d = v.shape
    bq = _tile(s, ATTN_BLOCKS)
    nq = s // bq
    hp = d // LANES
    qtab, ktab = _tri_tables(nq, by_key=False)

    def body(qt, kt, q0_ref, q1_ref, k0_ref, k1_ref, v_ref, o_ref, lse_ref, m0, m1, l0, l1, acc0, acc1):
        t = pl.program_id(1)
        qi, kj = qt[t], kt[t]
        ms, ls, accs = (m0, m1), (l0, l1), (acc0, acc1)

        @pl.when(kj == 0)
        def _():
            for h in range(2):
                ms[h][...] = jnp.full_like(ms[h], -jnp.inf)
                ls[h][...] = jnp.zeros_like(ls[h])
                accs[h][...] = jnp.zeros_like(accs[h])

        def update(diagonal):
            v2 = v_ref[...]
            qk = ((q0_ref, k0_ref), (q1_ref, k1_ref))
            sts = [_nt(qk[h][1][...], qk[h][0][...]) for h in range(2)]
            if diagonal:
                sts = [_diag_mask_t(st) for st in sts]
            m_prev = [ms[h][...] for h in range(2)]
            m_new = [jnp.maximum(m_prev[h], jnp.max(sts[h], axis=0, keepdims=True)) for h in range(2)]
            ps = [jnp.exp(sts[h] - m_new[h]) for h in range(2)]
            alpha = [jnp.exp(m_prev[h] - m_new[h]) for h in range(2)]
            for h in range(2):
                ls[h][...] = alpha[h] * ls[h][...] + jnp.sum(ps[h], axis=0, keepdims=True)
                accs[h][...] = alpha[h] * accs[h][...] + _tn(v2, ps[h].astype(MM_DTYPE))
                ms[h][...] = m_new[h]

        @pl.when(kj < qi)
        def _():
            update(False)

        @pl.when(kj == qi)
        def _():
            update(True)
            row = lax.broadcasted_iota(jnp.int32, (LANES, bq), 0)
            ot = jnp.where(row < HEAD, acc0[...] / l0[...], acc1[...] / l1[...])
            o_ref[...] = ot.T.astype(o_ref.dtype)
            lse_ref[...] = _rows01(m0[...] + jnp.log(l0[...]), m1[...] + jnp.log(l1[...]))

    blk = (bq, LANES)
    qmap = lambda p_, t, qt, kt: (qt[t], p_)
    kmap = lambda p_, t, qt, kt: (kt[t], p_)
    grid = (hp, qtab.shape[0])
    r_in, r_out, r_shapes, r_scratch, r_args = _rider_parts(rider)
    grid_spec = pltpu.PrefetchScalarGridSpec(
        num_scalar_prefetch=2, grid=grid,
        in_specs=[pl.BlockSpec(blk, qmap), pl.BlockSpec(blk, qmap), pl.BlockSpec(blk, kmap), pl.BlockSpec(blk, kmap), pl.BlockSpec(blk, kmap)] + r_in,
        out_specs=[pl.BlockSpec(blk, qmap), pl.BlockSpec((None, SUBLANES, bq), lambda p_, t, qt, kt: (p_, 0, qt[t]))] + r_out,
        scratch_shapes=[pltpu.VMEM((1, bq), F32)] * 4 + [pltpu.VMEM((LANES, bq), F32)] * 2 + r_scratch)
    outs = pl.pallas_call(
        _carry(body, rider, 2, 5, 2, grid), name=name, grid_spec=grid_spec,
        out_shape=[jax.ShapeDtypeStruct((s, d), ACT_DTYPE), jax.ShapeDtypeStruct((hp, SUBLANES, s), F32)] + r_shapes,
        compiler_params=_params(("arbitrary", "arbitrary") if rider else ("parallel", "arbitrary")),
    )(qtab, ktab, q_aug[0], q_aug[1], k_aug[0], k_aug[1], v, *r_args)
    return outs[0], outs[1], list(outs[2:])


def _diag_mask_t(st):
    key = lax.broadcasted_iota(jnp.int32, st.shape, 0)
    qry = lax.broadcasted_iota(jnp.int32, st.shape, 1)
    return jnp.where(qry >= key, st, -jnp.inf)


def _fox_bwd_t(q_aug, k_aug, v, lse, dd, do, *, name, rider=None):
    s, d = v.shape
    bq = _tile(s, ATTN_BLOCKS)
    nq = s // bq
    hp = d // LANES
    blk = (bq, LANES)
    qtab, ktab = _tri_tables(nq, by_key=True)
    n_steps = qtab.shape[0]

    def body(qt, kt, q0_ref, q1_ref, k0_ref, k1_ref, v_ref, lse_ref, dd_ref, do_ref,
             dq_ref, dk_ref, dv_ref, dcol_ref, drow_ref, dq_sc, rs_sc, dk_sc, dv_sc, cs_sc):
        t = pl.program_id(1)
        qi, kj = qt[t], kt[t]

        @pl.when(t == 0)
        def _():
            dq_sc[...] = jnp.zeros_like(dq_sc)
            rs_sc[...] = jnp.zeros_like(rs_sc)

        def update(diagonal):
            v2, do2 = v_ref[...], do_ref[...]
            masks = _head_masks(blk)
            row = lax.broadcasted_iota(jnp.int32, (LANES, bq), 0)
            off = pl.multiple_of(qi * bq, bq)
            for h, (q_ref, k_ref) in enumerate(((q0_ref, k0_ref), (q1_ref, k1_ref))):
                st = _nt(k_ref[...], q_ref[...])
                if diagonal:
                    st = _diag_mask_t(st)
                p = jnp.exp(st - lse_ref[h:h + 1, :])
                dp = _nt(v2, jnp.where(masks[h], do2, jnp.zeros_like(do2)))
                ds = p * (dp - dd_ref[h:h + 1, :])
                dsb = ds.astype(MM_DTYPE)
                dv_sc[h] += jnp.dot(p.astype(MM_DTYPE), do2, preferred_element_type=F32)
                dk_sc[h] += jnp.dot(dsb, q_ref[...], preferred_element_type=F32)
                cs_sc[h] += jnp.sum(ds, axis=1, keepdims=True)
                mine = (row < HEAD) if h == 0 else (row >= HEAD)
                dq_sc[:, pl.ds(off, bq)] += jnp.where(mine, _tn(k_ref[...], dsb), 0.0)
                rs_sc[h:h + 1, pl.ds(off, bq)] += jnp.sum(ds, axis=0, keepdims=True)

        @pl.when(qi == kj)
        def _():
            dk_sc[...] = jnp.zeros_like(dk_sc)
            dv_sc[...] = jnp.zeros_like(dv_sc)
            cs_sc[...] = jnp.zeros_like(cs_sc)
            update(True)

        @pl.when(qi > kj)
        def _():
            update(False)

        @pl.when(qi == nq - 1)
        def _():
            lo, _hi = _head_masks(blk)
            dk_ref[...] = jnp.where(lo, dk_sc[0], dk_sc[1]).astype(dk_ref.dtype)
            dv_ref[...] = jnp.where(lo, dv_sc[0], dv_sc[1]).astype(dv_ref.dtype)
            dcol_ref[...] = jnp.where(lo, cs_sc[0], cs_sc[1])

        @pl.when(t == n_steps - 1)
        def _():
            for c in range(nq):
                dq_ref[c * bq:(c + 1) * bq, :] = dq_sc[:, c * bq:(c + 1) * bq].T.astype(dq_ref.dtype)
            drow_ref[...] = rs_sc[...]

    qmap = lambda p_, t, qt, kt: (qt[t], p_)
    kmap = lambda p_, t, qt, kt: (kt[t], p_)
    rmap = lambda p_, t, qt, kt: (p_, 0, qt[t])
    grid = (hp, n_steps)
    r_in, r_out, r_shapes, r_scratch, r_args = _rider_parts(rider)
    outs = pl.pallas_call(
        _carry(body, rider, 2, 8, 5, grid), name=name,
        grid_spec=pltpu.PrefetchScalarGridSpec(
            num_scalar_prefetch=2, grid=grid,
            in_specs=[pl.BlockSpec(blk, qmap), pl.BlockSpec(blk, qmap), pl.BlockSpec(blk, kmap), pl.BlockSpec(blk, kmap), pl.BlockSpec(blk, kmap),
                      pl.BlockSpec((None, SUBLANES, bq), rmap), pl.BlockSpec((None, SUBLANES, bq), rmap), pl.BlockSpec(blk, qmap)] + r_in,
            out_specs=[pl.BlockSpec((s, LANES), lambda p_, t, qt, kt: (0, p_)), pl.BlockSpec(blk, kmap), pl.BlockSpec(blk, kmap),
                       pl.BlockSpec(blk, kmap), pl.BlockSpec((None, SUBLANES, s), lambda p_, t, qt, kt: (p_, 0, 0))] + r_out,
            scratch_shapes=[pltpu.VMEM((LANES, s), F32), pltpu.VMEM((SUBLANES, s), F32), pltpu.VMEM((2, bq, LANES), F32),
                            pltpu.VMEM((2, bq, LANES), F32), pltpu.VMEM((2, bq, 1), F32)] + r_scratch),
        out_shape=[jax.ShapeDtypeStruct((s, d), ACT_DTYPE), jax.ShapeDtypeStruct((s, d), ACT_DTYPE), jax.ShapeDtypeStruct((s, d), ACT_DTYPE),
                   jax.ShapeDtypeStruct((s, d), F32), jax.ShapeDtypeStruct((hp, SUBLANES, s), F32)] + r_shapes,
        compiler_params=_params(("arbitrary", "arbitrary") if rider else ("parallel", "arbitrary")),
    )(qtab, ktab, q_aug[0], q_aug[1], k_aug[0], k_aug[1], v, lse, dd, do, *r_args)
    return list(outs[:5]), list(outs[5:])


def _expand_heads(v, nh, hd):
    r = lax.broadcasted_iota(jnp.int32, (LANES, nh * hd), 0)
    c = lax.broadcasted_iota(jnp.int32, (LANES, nh * hd), 1) // hd
    e = jnp.where(r == c, 1.0, 0.0).astype(F32)
    return jnp.dot(v, e, precision=HI, preferred_element_type=F32)


def _reduce_heads(v, nh, hd):
    r = lax.broadcasted_iota(jnp.int32, (nh * hd, LANES), 0) // hd
    c = lax.broadcasted_iota(jnp.int32, (nh * hd, LANES), 1)
    e = jnp.where(r == c, 1.0, 0.0).astype(F32)
    return jnp.dot(v, e, precision=HI, preferred_element_type=F32)


def _ssd_prep(dt_raw, dt_bias, a_log, nh, *, name):
    s = dt_raw.shape[0]

    def fn(i, nrt, dtr, bsv, alv):
        dt = _softplus(dtr + bsv)
        acum = jnp.dot(_tri(SSM_CHUNK, False), dt * (-jnp.exp(alv)), precision=HI, preferred_element_type=F32)
        return (dt, acum, _expand_heads(dt, nh, HEAD), _expand_heads(acum, nh, HEAD)), ()

    w = nh * HEAD
    return _rows(fn, name=name, s=s, tm=SSM_CHUNK, ins=[("row", dt_raw, LANES, _c0), ("full", dt_bias), ("full", a_log)],
                 outs=[(LANES, LANES, _c0, F32), (LANES, LANES, _c0, F32), (w, w, _c0, F32), (w, w, _c0, F32)])


def _ssd_prep_bwd(dt_raw, dt_bias, a_log, ddtx, dacx, nh, *, name):
    s = dt_raw.shape[0]

    def fn(i, nrt, dtr, bsv, alv, ddx, dax):
        z = dtr + bsv
        dt = _softplus(z)
        a = -jnp.exp(alv)
        dda = jnp.dot(_tri(SSM_CHUNK, True), _reduce_heads(dax, nh, HEAD), precision=HI, preferred_element_type=F32)
        ddt = _reduce_heads(ddx, nh, HEAD) + dda * a
        dz = ddt * _sigmoid(z)
        lane = lax.broadcasted_iota(jnp.int32, dz.shape, 1)
        dz = jnp.where(lane < nh, dz, 0.0)
        return (dz,), (_sum_rows(dz), _sum_rows(dda * dt) * a)

    w = nh * HEAD
    return _rows(fn, name=name, s=s, tm=SSM_CHUNK,
                 ins=[("row", dt_raw, LANES, _c0), ("full", dt_bias), ("full", a_log), ("row", ddtx, w, _c0), ("row", dacx, w, _c0)],
                 outs=[(LANES, LANES, _c0, ACT_DTYPE)], accs=[(1, LANES, LANES, _c0), (1, LANES, LANES, _c0)])


def _ssd_decay(ac_blk, act_blk, head):
    col = _pick_lane(ac_blk, head)
    row = _pick_row(act_blk, head)
    r = lax.broadcasted_iota(jnp.int32, (SSM_CHUNK, SSM_CHUNK), 0)
    c = lax.broadcasted_iota(jnp.int32, (SSM_CHUNK, SSM_CHUNK), 1)
    return jnp.exp(jnp.where(r >= c, col - row, -jnp.inf))


def _group_masks(shape, hpg):
    lane = lax.broadcasted_iota(jnp.int32, shape, len(shape) - 1) // HEAD
    return [lane == k for k in range(hpg)]


def _ssd_scan_fwd(xs, bm, cm, dtx, acx, acum, acum_t, d_x, *, name):
    s, di = xs.shape
    ng = bm.shape[1] // SSM_STATE
    gw = di // ng
    hpg = gw // HEAD
    nc = s // SSM_CHUNK
    L = SSM_CHUNK
    nh_pad = acum_t.shape[0]

    gp = next(n for n in (8, 4, 2, 1) if ng % n == 0)
    N = SSM_STATE

    def body(x_ref, b_ref, c_ref, dt_ref, ax_ref, ac_ref, act_ref, d_ref, y_ref, st_ref, state):
        g2, c = pl.program_id(0), pl.program_id(1)

        @pl.when(c == 0)
        def _():
            state[...] = jnp.zeros_like(state)

        masks = _group_masks((L, gw), hpg)
        for gi in range(gp):
            g = g2 * gp + gi
            lanes, st_lanes = slice(gi * gw, (gi + 1) * gw), slice(gi * N, (gi + 1) * N)
            x4, bv, cv = x_ref[:, lanes].astype(F32), b_ref[:, st_lanes], c_ref[:, st_lanes]
            ax = ax_ref[:, lanes]
            tx = x4 * dt_ref[:, lanes]
            cb = lax.dot_general(cv, bv, (((1,), (1,)), ((), ())), preferred_element_type=F32)
            y = jnp.zeros((L, gw), F32)
            txb = tx.astype(MM_DTYPE)
            for k in range(hpg):
                wk = (cb * _ssd_decay(ac_ref[...], act_ref[...], g * hpg + k)).astype(MM_DTYPE)
                y = y + jnp.where(masks[k], jnp.dot(wk, txb, preferred_element_type=F32), 0.0)
            prev = state[gi]
            st_ref[gi] = prev
            y = y + jnp.dot(cv, prev.astype(MM_DTYPE), preferred_element_type=F32) * jnp.exp(ax)
            y = y + d_ref[:, lanes] * x4
            y_ref[:, lanes] = y.astype(y_ref.dtype)
            a_last = ax[L - 1:L, :]
            sx = (tx * jnp.exp(a_last - ax)).astype(MM_DTYPE)
            state[gi] = prev * jnp.exp(a_last) + lax.dot_general(bv, sx, (((0,), (0,)), ((), ())), preferred_element_type=F32)

    y, states = pl.pallas_call(
        body, name=name, grid=(ng // gp, nc),
        in_specs=[pl.BlockSpec((L, gp * gw), lambda g, c: (c, g)), pl.BlockSpec((L, gp * N), lambda g, c: (c, g)),
                  pl.BlockSpec((L, gp * N), lambda g, c: (c, g)), pl.BlockSpec((L, gp * gw), lambda g, c: (c, g)),
                  pl.BlockSpec((L, gp * gw), lambda g, c: (c, g)), pl.BlockSpec((L, LANES), lambda g, c: (c, 0)),
                  pl.BlockSpec((nh_pad, L), lambda g, c: (0, c)), pl.BlockSpec((1, gp * gw), lambda g, c: (0, g))],
        out_specs=[pl.BlockSpec((L, gp * gw), lambda g, c: (c, g)), pl.BlockSpec((gp, None, N, gw), lambda g, c: (g, c, 0, 0))],
        out_shape=[jax.ShapeDtypeStruct((s, di), ACT_DTYPE), jax.ShapeDtypeStruct((ng, nc, N, gw), F32)],
        scratch_shapes=[pltpu.VMEM((gp, N, gw), F32)],
        compiler_params=_params(("parallel", "arbitrary")),
    )(xs, bm, cm, dtx, acx, acum, acum_t, d_x)
    return y, states


def _ssd_scan_bwd(xs, bm, cm, dtx, acx, acum, acum_t, d_x, states, dy, *, name, rider=None):
    s, di = xs.shape
    ng = bm.shape[1] // SSM_STATE
    gw = di // ng
    hpg = gw // HEAD
    nc = s // SSM_CHUNK
    L = SSM_CHUNK
    nh_pad = acum_t.shape[0]

    gp = next(n for n in (8, 4, 2, 1) if ng % n == 0)
    N = SSM_STATE

    def body(x_ref, b_ref, c_ref, dt_ref, ax_ref, ac_ref, act_ref, d_ref, st_ref, dy_ref,
             dx_ref, db_ref, dc_ref, ddt_ref, dax_ref, dd_ref, dstate):
        g2, cc = pl.program_id(0), pl.program_id(1)

        @pl.when(cc == 0)
        def _():
            dstate[...] = jnp.zeros_like(dstate)
            dd_ref[...] = jnp.zeros_like(dd_ref)

        for gi in range(gp):
            one_group(g2 * gp + gi, gi, slice(gi * gw, (gi + 1) * gw), slice(gi * N, (gi + 1) * N), x_ref, b_ref, c_ref, dt_ref, ax_ref, ac_ref,
                      act_ref, d_ref, st_ref, dy_ref, dx_ref, db_ref, dc_ref, ddt_ref, dax_ref, dd_ref, dstate)

    def one_group(g, gi, lanes, st_lanes, x_ref, b_ref, c_ref, dt_ref, ax_ref, ac_ref, act_ref, d_ref, st_ref, dy_ref,
                  dx_ref, db_ref, dc_ref, ddt_ref, dax_ref, dd_ref, dstate):
        x4, bv, cv = x_ref[:, lanes].astype(F32), b_ref[:, st_lanes], c_ref[:, st_lanes]
        tv, ax, dyv = dt_ref[:, lanes], ax_ref[:, lanes], dy_ref[:, lanes].astype(F32)
        prev, dn = st_ref[gi], dstate[gi]
        dnb = dn.astype(MM_DTYPE)
        masks = _group_masks((L, gw), hpg)
        tx = x4 * tv
        txb = tx.astype(MM_DTYPE)
        e_ax = jnp.exp(ax)
        a_last = ax[L - 1:L, :]
        e_last = jnp.exp(a_last)
        ed = jnp.exp(a_last - ax)

        dx = d_ref[:, lanes] * dyv
        dd_ref[:, lanes] += _sum_rows(dyv * x4)
        dye = (dyv * e_ax).astype(MM_DTYPE)
        yo = jnp.dot(cv, prev.astype(MM_DTYPE), preferred_element_type=F32) * e_ax
        dc = lax.dot_general(dye, prev.astype(MM_DTYPE), (((1,), (1,)), ((), ())), preferred_element_type=F32)
        dprev = lax.dot_general(cv, dye, (((0,), (0,)), ((), ())), preferred_element_type=F32)
        dax = dyv * yo
        sx = tx * ed
        dsx = jnp.dot(bv, dnb, preferred_element_type=F32)
        db = lax.dot_general(sx.astype(MM_DTYPE), dnb, (((1,), (1,)), ((), ())), preferred_element_type=F32)
        dtx_ = dsx * ed
        dsx_sx = dsx * sx
        dax = dax - dsx_sx
        dlast = _sum_rows(dsx_sx) + _sum_rows(dn * prev) * e_last
        dprev = dprev + dn * e_last
        cb = lax.dot_general(cv, bv, (((1,), (1,)), ((), ())), preferred_element_type=F32)
        dcb = jnp.zeros((L, L), F32)
        lane = lax.broadcasted_iota(jnp.int32, (L, gw), 1)
        for k in range(hpg):
            dec = _ssd_decay(ac_ref[...], act_ref[...], g * hpg + k)
            wk = (cb * dec).astype(MM_DTYPE)
            dyk = jnp.where(masks[k], dyv, 0.0).astype(MM_DTYPE)
            dtx_ = dtx_ + jnp.where(masks[k], lax.dot_general(wk, dyk, (((0,), (0,)), ((), ())), preferred_element_type=F32), 0.0)
            dwk = lax.dot_general(dyk, txb, (((1,), (1,)), ((), ())), preferred_element_type=F32)
            dcb = dcb + dwk * dec
            mk = dwk * cb * dec
            da_k = jnp.sum(mk, axis=1, keepdims=True) - jnp.sum(mk.T, axis=1, keepdims=True)
            dax = dax + jnp.where(lane == k * HEAD, da_k, 0.0)
        dcbb = dcb.astype(MM_DTYPE)
        dc = dc + jnp.dot(dcbb, bv, preferred_element_type=F32)
        db = db + lax.dot_general(dcbb, cv, (((0,), (0,)), ((), ())), preferred_element_type=F32)
        sub = lax.broadcasted_iota(jnp.int32, (L, gw), 0)
        dax = dax + jnp.where(sub == L - 1, dlast, 0.0)
        dx_ref[:, lanes] = (dx + dtx_ * tv).astype(dx_ref.dtype)
        ddt_ref[:, lanes] = dtx_ * x4
        dax_ref[:, lanes] = dax
        db_ref[:, st_lanes] = db.astype(db_ref.dtype)
        dc_ref[:, st_lanes] = dc.astype(dc_ref.dtype)
        dstate[gi] = dprev

    rev = lambda g, c: (nc - 1 - c, g)
    rev0 = lambda g, c: (nc - 1 - c, 0)
    grid = (ng // gp, nc)
    r_in, r_out, r_shapes, r_scratch, r_args = _rider_parts(rider)
    outs = pl.pallas_call(
        _carry(body, rider, 0, 10, 6, grid), name=name, grid=grid,
        in_specs=[pl.BlockSpec((L, gp * gw), rev), pl.BlockSpec((L, gp * N), rev), pl.BlockSpec((L, gp * N), rev),
                  pl.BlockSpec((L, gp * gw), rev), pl.BlockSpec((L, gp * gw), rev), pl.BlockSpec((L, LANES), rev0),
                  pl.BlockSpec((nh_pad, L), lambda g, c: (0, nc - 1 - c)), pl.BlockSpec((1, gp * gw), lambda g, c: (0, g)),
                  pl.BlockSpec((gp, None, N, gw), lambda g, c: (g, nc - 1 - c, 0, 0)), pl.BlockSpec((L, gp * gw), rev)] + r_in,
        out_specs=[pl.BlockSpec((L, gp * gw), rev), pl.BlockSpec((L, gp * N), rev), pl.BlockSpec((L, gp * N), rev),
                   pl.BlockSpec((L, gp * gw), rev), pl.BlockSpec((L, gp * gw), rev), pl.BlockSpec((1, gp * gw), lambda g, c: (0, g))] + r_out,
        out_shape=[jax.ShapeDtypeStruct((s, di), ACT_DTYPE), jax.ShapeDtypeStruct(bm.shape, ACT_DTYPE), jax.ShapeDtypeStruct(cm.shape, ACT_DTYPE),
                   jax.ShapeDtypeStruct((s, di), F32), jax.ShapeDtypeStruct((s, di), F32), jax.ShapeDtypeStruct((1, di), F32)] + r_shapes,
        scratch_shapes=[pltpu.VMEM((gp, N, gw), F32)] + r_scratch,
        compiler_params=_params(("arbitrary", "arbitrary") if rider else ("parallel", "arbitrary")),
    )(xs, bm, cm, dtx, acx, acum, acum_t, d_x, states, dy, *r_args)
    return list(outs[:6]), list(outs[6:])


def _ssd_gate(y, z, w, gs, *, name):
    s, d = y.shape

    def fn(i, nrt, yv, zv, wv):
        zv = zv.astype(F32)
        u = yv.astype(F32) * zv * _sigmoid(zv)
        return (u * lax.rsqrt(_gmean(u * u, gs) + RMS_EPS) * wv,), ()

    (o,) = _rows(fn, name=name, s=s, tm=_tile(s, (256, 128)), ins=[("row", y, d, _c0), ("row", z, d, _c0), ("full", w.reshape(1, d))],
                 outs=[(d, d, _c0, ACT_DTYPE)])
    return o


def _ssd_gate_bwd(y, z, w, do, gs, *, name):
    s, d = y.shape

    def fn(i, nrt, yv, zv, wv, dov):
        yv, zv, dov = yv.astype(F32), zv.astype(F32), dov.astype(F32)
        sg = _sigmoid(zv)
        sl = zv * sg
        u = yv * sl
        r = lax.rsqrt(_gmean(u * u, gs) + RMS_EPS)
        uh = u * r
        g = dov * wv
        du = r * (g - uh * _gmean(g * uh, gs))
        return (du * sl, du * yv * sg * (1.0 + zv * (1.0 - sg))), (_sum_rows(dov * uh),)

    dy, dz, dw = _rows(fn, name=name, s=s, tm=_tile(s, (256, 128)),
                       ins=[("row", y, d, _c0), ("row", z, d, _c0), ("full", w.reshape(1, d)), ("row", do, d, _c0)],
                       outs=[(d, d, _c0, ACT_DTYPE), (d, d, _c0, ACT_DTYPE)], accs=[(1, d, d, _c0)])
    return dy, dz, dw.reshape(d)


def _pad_lanes(w):
    return jnp.pad(w, ((0, 0), (0, LANES - w.shape[1])))


def _nt_sum(pairs, name):
    acc = None
    for a, b in pairs:
        acc = _mm(a, b, tb=True, add=acc, name=name)
    return acc


def _conv_mixer_fwd(h, w_in, layer, w_dw, tag, rider=None):
    d, ns = h.shape[1], w_in.shape[3]
    inter = (3, _tile(math.gcd(d, ns), TILES))
    p = _mm(h, w_in, b_layer=layer, inter=inter, out_dtype=ACT_DTYPE, name=f"{tag}_in", rider=rider)
    p, landed = p if rider else (p, [])
    return _gconv_fwd(p, w_dw, inter[1], name=f"{tag}_gate"), (h, w_in, layer, inter, p, w_dw), landed


def _conv_mixer_bwd(cache, do, tag, rider=None):
    h, w_in, layer, inter, p, w_dw = cache
    dp, dw_dw = _gconv_bwd(p, w_dw, do, inter[1], name=f"{tag}_gate_bwd")
    dw_in = _mm(h, dp, ta=True, out_shard=w_in.shape[3], inter=inter, out_dtype=WIRE_DTYPE, name=f"{tag}_dw_in")
    dh = _mm(dp, w_in, tb=True, b_layer=layer, inter=inter, name=f"{tag}_dh", rider=rider)
    dh, landed = dh if rider else (dh, [])
    return dh, {"w_in": dw_in, "w_dw": dw_dw}, landed


def _fox_mixer_fwd(h, w_in, b_f, q_gain, k_gain, tag, rider=None):
    d = h.shape[1]
    nh = d // HEAD
    ws = [w_in[:, k * d:(k + 1) * d] for k in range(3)] + [_pad_lanes(w_in[:, 3 * d:])]
    q, k, v = [_mm(h, w, out_dtype=ACT_DTYPE, name=f"{tag}_in") for w in ws[:3]]
    f = _mm(h, ws[3], name=f"{tag}_in_f")
    gq = jnp.tile(q_gain, nh).reshape(1, d)
    gk = jnp.tile(k_gain, nh).reshape(1, d)
    bf = _pad_lanes(b_f.reshape(1, nh))
    qs, kn, logf = _fox_prep(q, k, f, gq, gk, bf, name=f"{tag}_prep")
    cum = _cumsum_rows(logf, reverse=False, name=f"{tag}_cum")
    aug = _fox_aug(qs, kn, cum, name=f"{tag}_aug")
    q_aug, k_aug = aug[:2], aug[2:]
    o, lse, landed = _fox_fwd_t(q_aug, k_aug, v, name=f"{tag}_attn", rider=rider)
    return o, (h, ws, q, k, v, f, gq, gk, bf, q_aug, k_aug, o, lse), landed


def _fox_mixer_bwd(cache, do, tag, rider=None):
    h, ws, q, k, v, f, gq, gk, bf, q_aug, k_aug, o, lse = cache
    s, d = q.shape
    nh = d // HEAD
    dd = _pair_rows(_fox_dd(do, o, name=f"{tag}_attn_dd"), nh)
    (dqs, dkn, dv, dcol, drow), landed = _fox_bwd_t(q_aug, k_aug, v, lse, dd, do, name=f"{tag}_attn_bwd", rider=rider)
    dcum = _pad_lanes(drow[:, :2, :].reshape(nh, s).T - dcol[:, ::HEAD])
    dlogf = _cumsum_rows(dcum, reverse=True, name=f"{tag}_cum_bwd")
    dq, dk, df, dgq, dgk, dbf = _fox_prep_bwd(q, k, f, gq, gk, bf, dqs, dkn, dlogf, name=f"{tag}_prep_bwd")
    dps = (dq, dk, dv, df)
    dws = [_mm(h, dp, ta=True, name=f"{tag}_dw_in") for dp in dps]
    dw_in = jnp.concatenate(dws[:3] + [dws[3][:, :nh]], axis=1)
    dh = _nt_sum(list(zip(dps, ws)), f"{tag}_dh")
    return dh, {"w_in": dw_in, "b_f": dbf[0, :nh], "q_gain": dgq.reshape(nh, HEAD).sum(0), "k_gain": dgk.reshape(nh, HEAD).sum(0)}, landed


def _ssd_mixer_fwd(h, w_in, conv_w, conv_b, dt_bias, a_log, d_skip, norm_w, tag):
    di = norm_w.shape[0]
    nh = di // HEAD
    gn = (conv_w.shape[1] - di) // 2
    cuts = [0, di, 2 * di, 2 * di + gn, 2 * di + 2 * gn]
    ws = [w_in[:, cuts[k]:cuts[k + 1]] for k in range(4)] + [_pad_lanes(w_in[:, cuts[4]:])]
    z, xr, br, cr = [_mm(h, w, out_dtype=ACT_DTYPE, name=f"{tag}_in") for w in ws[:4]]
    dtr = _mm(h, ws[4], name=f"{tag}_in_dt")
    ccuts = [0, di, di + gn, di + 2 * gn]
    cws = [conv_w[:, ccuts[k]:ccuts[k + 1]] for k in range(3)]
    cbs = [conv_b[ccuts[k]:ccuts[k + 1]] for k in range(3)]
    xs, bm, cm = [_sconv_fwd(r, w, b, name=f"{tag}_conv") for r, w, b in zip((xr, br, cr), cws, cbs)]
    dtb = _pad_lanes(dt_bias.reshape(1, nh))
    alg = _pad_lanes(a_log.reshape(1, nh))
    _dt, acum, dtx, acx = _ssd_prep(dtr, dtb, alg, nh, name=f"{tag}_prep")
    acum_t = acum[:, :nh].T
    d_x = jnp.repeat(d_skip, HEAD).reshape(1, di)
    y, states = _ssd_scan_fwd(xs, bm, cm, dtx, acx, acum, acum_t, d_x, name=f"{tag}_scan")
    gs = di // (gn // SSM_STATE)
    o = _ssd_gate(y, z, norm_w, gs, name=f"{tag}_gate")
    return o, (h, ws, z, (xr, br, cr), dtr, cws, cbs, xs, bm, cm, dtb, alg, dtx, acx, acum, acum_t, d_x, states, y, norm_w, gs, nh)


def _ssd_mixer_bwd(cache, do, tag, rider=None):
    h, ws, z, raws, dtr, cws, cbs, xs, bm, cm, dtb, alg, dtx, acx, acum, acum_t, d_x, states, y, norm_w, gs, nh = cache
    dy, dz, dnorm = _ssd_gate_bwd(y, z, norm_w, do, gs, name=f"{tag}_gate_bwd")
    (dxs, dbm, dcm, ddtx, dacx, dd_x), landed = _ssd_scan_bwd(xs, bm, cm, dtx, acx, acum, acum_t, d_x, states, dy, name=f"{tag}_scan_bwd",
                                                             rider=rider)
    ddtr, ddtb, dalg = _ssd_prep_bwd(dtr, dtb, alg, ddtx, dacx, nh, name=f"{tag}_prep_bwd")
    conv = [_sconv_bwd(r, w, b, da, name=f"{tag}_conv_bwd") for r, w, b, da in zip(raws, cws, cbs, (dxs, dbm, dcm))]
    dps = (dz, conv[0][0], conv[1][0], conv[2][0], ddtr)
    dws = [_mm(h, dp, ta=True, name=f"{tag}_dw_in") for dp in dps]
    dw_in = jnp.concatenate(dws[:4] + [dws[4][:, :nh]], axis=1)
    dh = _nt_sum(list(zip(dps, ws)), f"{tag}_dh")
    return dh, {"w_in": dw_in, "conv_w": jnp.concatenate([c[1] for c in conv], axis=1), "conv_b": jnp.concatenate([c[2] for c in conv]),
                "dt_bias": ddtb[0, :nh], "a_log": dalg[0, :nh], "d": dd_x.reshape(nh, HEAD).sum(1), "norm_w": dnorm}, landed


def _rows_natural(cm, layer):
    return cm[:, layer].reshape(-1, cm.shape[3])


def _cols_natural(cm, layer):
    return jnp.moveaxis(cm[:, layer], 0, 1).reshape(cm.shape[2], -1)


def _cols_chip_major(g):
    return jnp.moveaxis(g.reshape(g.shape[0], N_CHIPS, -1), 1, 0).astype(WIRE_DTYPE)


MIXERS = ("conv", "fox", "ssd")


def _model_layer(name, l):
    return l if name.startswith("ffn") else 3 * l + MIXERS.index(name.split("_")[0])


def _piece(pieces, layer):
    for arr, start in pieces:
        if start <= layer < start + arr.shape[1]:
            return arr, layer - start
    raise KeyError(layer)


def _with_own(landed, shards, chip):
    return [lax.dynamic_update_slice(g, w[None], (chip, 0, 0, 0)) for g, w in zip(landed, shards)]


def _reduce_begin(gs, place):
    from_sibling = _swap_halves_list(gs, name="reduce_halves")
    return [_add_half(g, r, place, name="reduce_add_sibling") for g, r in zip(gs, from_sibling)]


def _reduce_end(by_chip, chip_sums, place):
    reds = [_sum_chips(b, s, place, name="reduce_sum_chips") for b, s in zip(by_chip, chip_sums)]
    return _join_halves_list(reds, name="reduce_share")


def _local_step(x, tgt, fw, cm, late, place):
    depth = fw["mix_norm"].shape[0]
    chip = place[0]
    cm = {n: list(p) for n, p in cm.items()}
    late = dict(late)

    def gather_rider(group):
        return _gather_ici_rider([shards for shards, _ in group.values()]) if group else None

    def land(group, landed):
        if group:
            full = _with_own(_forward_halves_list(landed, name="gather_forward"), [shards for shards, _ in group.values()], chip)
            for (n, (_, start)), arr in zip(group.items(), full):
                cm[n].append((arr, start))

    layers = []
    xin = x
    for i in range(depth):
        kind, j = i % 3, i // 3
        tag = f"l{i}"
        h = _rms(xin, fw["mix_norm"][i], name=f"{tag}_norm1")
        if kind == 0:
            w_in, jl = _piece(cm["conv_w_in"], j)
            group = late.pop(("mixer", i), None)
            o, mc, landed = _conv_mixer_fwd(h, w_in, jl, fw["conv_w_dw"][j], tag + "_conv", rider=gather_rider(group))
            land(group, landed)
        elif kind == 1:
            group = late.pop(("attention", i), None)
            o, mc, landed = _fox_mixer_fwd(h, _cols_natural(*_piece(cm["fox_w_in"], j)), fw["fox_b_f"][j], fw["fox_q_gain"][j],
                                           fw["fox_k_gain"][j], tag + "_fox", rider=gather_rider(group))
            land(group, landed)
        else:
            o, mc = _ssd_mixer_fwd(h, _cols_natural(*_piece(cm["ssd_w_in"], j)), fw["ssd_conv_w"][j], fw["ssd_conv_b"][j], fw["ssd_dt_bias"][j],
                                   fw["ssd_a_log"][j], fw["ssd_d"][j], fw["ssd_norm_w"][j], tag + "_ssd")
        w_out = _rows_natural(*_piece(cm[MIXERS[kind] + "_w_out"], j))
        x1 = _mm(o, w_out, add=xin, name=f"{tag}_mix_out")
        h2 = _rms(x1, fw["ffn_norm"][i], name=f"{tag}_norm2")
        w_gu, il = _piece(cm["ffn_w_gu"], i)
        w_down = _rows_natural(*_piece(cm["ffn_w_down"], i))
        inter = (2, _tile(math.gcd(w_down.shape[0], w_gu.shape[3]), TILES))
        group = late.pop(("ffn", i), None)
        gu, a, landed = _ffn_up(h2, w_gu, il, inter[1], name=f"{tag}_ffn_gu", rider=gather_rider(group))
        land(group, landed)
        layers.append((xin, o, mc, w_out, x1, h2, w_gu, il, w_down, inter, gu, a))
        xin = _mm(a, w_down, add=x1, name=f"{tag}_ffn_down")
    loss, dx, dxb = _loss_head(xin, tgt, name="loss_head")

    small = {k: [None] * v.shape[0] for k, v in fw.items()}
    reds = {}
    riding = None
    at_end = []
    for i in reversed(range(depth)):
        kind, j = i % 3, i // 3
        tag = f"l{i}"
        xin, o, mc, w_out, x1, h2, w_gu, il, w_down, inter, gu, a = layers[i]
        mine = [(("ffn_w_down", i), _mm(a, dxb, ta=True, out_dtype=WIRE_DTYPE, name=f"{tag}_dw_down").reshape(N_CHIPS, -1, w_down.shape[1]))]
        dgu = _ffn_back(dxb, w_down, gu, inter[1], name=f"{tag}_ffn_back")
        mine.append((("ffn_w_gu", i), _mm(h2, dgu, ta=True, out_shard=w_gu.shape[3], inter=inter, out_dtype=WIRE_DTYPE, name=f"{tag}_dw_gu")))
        rider = riding[2] if riding and kind == 0 else None
        dh2 = _mm(dgu, w_gu, tb=True, b_layer=il, inter=inter, name=f"{tag}_dh2", rider=rider)
        if rider is not None:
            dh2, landed = dh2
            reds.update(zip(riding[0], _reduce_end(landed, riding[1], place)))
            riding = None
        dx1, dx1b, small["ffn_norm"][i] = _rms_bwd(x1, fw["ffn_norm"][i], dh2, dx, name=f"{tag}_norm2_bwd")
        mine.append(((MIXERS[kind] + "_w_out", j),
                     _mm(o, dx1b, ta=True, out_dtype=WIRE_DTYPE, name=f"{tag}_dw_out").reshape(N_CHIPS, -1, w_out.shape[1])))
        do = _mm(dx1b, w_out, tb=True, out_dtype=ACT_DTYPE, name=f"{tag}_do")
        rider = riding[2] if riding else None
        if kind == 0:
            if i == 0:
                sums0 = _reduce_begin([g for _, g in mine], place)
                dh, mg, landed = _conv_mixer_bwd(mc, do, tag + "_conv", rider=_scatter_rider(sums0))
                reds.update(zip([k for k, _ in mine], _reduce_end(landed, sums0, place)))
                mine = []
            else:
                dh, mg, _ = _conv_mixer_bwd(mc, do, tag + "_conv")
        elif kind == 1:
            dh, mg, landed = _fox_mixer_bwd(mc, do, tag + "_fox", rider=rider)
        else:
            dh, mg, landed = _ssd_mixer_bwd(mc, do, tag + "_ssd", rider=rider)
        if rider is not None:
            reds.update(zip(riding[0], _reduce_end(landed, riding[1], place)))
            riding = None
        for k, v in mg.items():
            if k == "w_in":
                mine.append(((f"{MIXERS[kind]}_w_in", j), v if kind == 0 else _cols_chip_major(v)))
            else:
                small[f"{MIXERS[kind]}_{k}"][j] = v
        dx, dxb, small["mix_norm"][i] = _rms_bwd(xin, fw["mix_norm"][i], dh, dx1, name=f"{tag}_norm1_bwd")
        if i > 0:
            chip_sums = _reduce_begin([g for _, g in mine], place)
            riding = ([k for k, _ in mine], chip_sums, _scatter_rider(chip_sums))
        else:
            at_end += mine
    assert riding is None
    chip_sums = _reduce_begin([g for _, g in at_end], place)
    by_chip = _run_rider(_scatter_rider(chip_sums), name="reduce_chips")
    reds.update(zip([k for k, _ in at_end], _reduce_end(by_chip, chip_sums, place)))
    return loss, dx, {k: jnp.stack(v) for k, v in small.items()}, reds


ANY = pl.BlockSpec(memory_space=pl.ANY)
VMEM_SPEC = pl.BlockSpec(memory_space=pltpu.VMEM)


def _place():
    return lax.axis_index("x"), lax.axis_index("y"), lax.axis_index("c")


def _remote(src, dst, send_sems, recv_sems, k, to):
    return pltpu.make_async_remote_copy(src_ref=src, dst_ref=dst, send_sem=send_sems.at[k], recv_sem=recv_sems.at[k],
                                        device_id=to, device_id_type=MESH)


def _half_of(ref, h, shape):
    layers, rows, _ = shape
    if layers % 2 == 0:
        return ref.at[pl.ds(h * (layers // 2), layers // 2)]
    return ref.at[:, pl.ds(pl.multiple_of(h * (rows // 2), 16), rows // 2)]


def _row_half(ref, h, rows):
    return ref.at[:, pl.ds(pl.multiple_of(h * (rows // 2), 16), rows // 2)]


def _gather_list(ws, *, name):
    n = len(ws)

    def body(*refs):
        w_refs, o_refs, send_sems, recv_sems = refs[:n], refs[n:2 * n], refs[2 * n], refs[2 * n + 1]
        x, y, c = _place()
        me, sibling, m = (x, y, c), (x, y, 1 - c), 2 * x + y
        chips = [(1 - x, y), (x, 1 - y), (1 - x, 1 - y)]
        first, passed = [], []
        for p, (w_ref, o_ref) in enumerate(zip(w_refs, o_refs)):
            for j, (px, py) in enumerate(chips):
                cp = _remote(_half_of(w_ref, c, ws[p].shape), _half_of(o_ref.at[m], c, ws[p].shape), send_sems, recv_sems, 6 * p + j, (px, py, c))
                cp.start()
                first.append(cp)
        for p, o_ref in enumerate(o_refs):
            for j, (px, py) in enumerate(chips):
                blk = _half_of(o_ref.at[2 * px + py], c, ws[p].shape)
                _remote(blk, blk, send_sems, recv_sems, 6 * p + j, me).wait_recv()
                fwd = _remote(blk, blk, send_sems, recv_sems, 6 * p + 3 + j, sibling)
                fwd.start()
                passed.append(fwd)
        for p, o_ref in enumerate(o_refs):
            for j, (px, py) in enumerate(chips):
                blk = _half_of(o_ref.at[2 * px + py], 1 - c, ws[p].shape)
                _remote(blk, blk, send_sems, recv_sems, 6 * p + 3 + j, me).wait_recv()
        for cp in first + passed:
            cp.wait_send()

    return pl.pallas_call(
        body, name=name, in_specs=[ANY] * n, out_specs=[ANY] * n,
        out_shape=[jax.ShapeDtypeStruct((N_CHIPS,) + w.shape, w.dtype) for w in ws],
        scratch_shapes=[pltpu.SemaphoreType.DMA((6 * n,)), pltpu.SemaphoreType.DMA((6 * n,))],
    )(*ws)


class _Rider:
    def __init__(self, arrays, out_shapes, n_sems, start, finish):
        self.arrays, self.out_shapes, self.n_sems, self.start, self.finish = list(arrays), list(out_shapes), n_sems, start, finish

    @property
    def scratch(self):
        return [pltpu.SemaphoreType.DMA((self.n_sems,)), pltpu.SemaphoreType.DMA((self.n_sems,))]


def _carry(body, rider, n_prefetch, n_in, n_out, grid):
    if rider is None:
        return body
    ri, ro = len(rider.arrays), len(rider.out_shapes)

    def wrapped(*refs):
        pre, rest = refs[:n_prefetch], refs[n_prefetch:]
        ins, r_in = rest[:n_in], rest[n_in:n_in + ri]
        outs, r_out = rest[n_in + ri:n_in + ri + n_out], rest[n_in + ri + n_out:n_in + ri + n_out + ro]
        scratch = rest[n_in + ri + n_out + ro:]
        first = functools.reduce(jnp.logical_and, [pl.program_id(a) == 0 for a in range(len(grid))])
        last = functools.reduce(jnp.logical_and, [pl.program_id(a) == g - 1 for a, g in enumerate(grid)])

        @pl.when(first)
        def _():
            rider.start(r_in, r_out, scratch[-2], scratch[-1])

        body(*pre, *ins, *outs, *scratch[:-2])

        @pl.when(last)
        def _():
            rider.finish(r_in, r_out, scratch[-2], scratch[-1])

    return wrapped


def _run_rider(rider, *, name):
    n = len(rider.arrays)

    def body(*refs):
        rider.start(refs[:n], refs[n:2 * n], refs[-2], refs[-1])
        rider.finish(refs[:n], refs[n:2 * n], refs[-2], refs[-1])

    return pl.pallas_call(body, name=name, in_specs=[ANY] * n, out_specs=[ANY] * len(rider.out_shapes), out_shape=rider.out_shapes,
                          scratch_shapes=rider.scratch)(*rider.arrays)


def _chips_of(x, y):
    return [(1 - x, y), (x, 1 - y), (1 - x, 1 - y)]


def _gather_ici_rider(ws):
    def copies(w_refs, o_refs, send_sems, recv_sems):
        x, y, c = _place()
        m = 2 * x + y
        return [_remote(_half_of(w_ref, c, ws[p].shape), _half_of(o_ref.at[m], c, ws[p].shape), send_sems, recv_sems, 3 * p + j, (px, py, c))
                for p, (w_ref, o_ref) in enumerate(zip(w_refs, o_refs)) for j, (px, py) in enumerate(_chips_of(x, y))]

    def start(w_refs, o_refs, send_sems, recv_sems):
        for cp in copies(w_refs, o_refs, send_sems, recv_sems):
            cp.start()

    def finish(w_refs, o_refs, send_sems, recv_sems):
        x, y, c = _place()
        for p, o_ref in enumerate(o_refs):
            for j, (px, py) in enumerate(_chips_of(x, y)):
                blk = _half_of(o_ref.at[2 * px + py], c, ws[p].shape)
                _remote(blk, blk, send_sems, recv_sems, 3 * p + j, (x, y, c)).wait_recv()
        for cp in copies(w_refs, o_refs, send_sems, recv_sems):
            cp.wait_send()

    return _Rider(ws, [jax.ShapeDtypeStruct((N_CHIPS,) + w.shape, w.dtype) for w in ws], 3 * len(ws), start, finish)


def _forward_halves_list(gathered, *, name):
    n = len(gathered)
    shapes = [g.shape[1:] for g in gathered]

    def body(*refs):
        o_refs, send_sems, recv_sems = refs[n:2 * n], refs[2 * n], refs[2 * n + 1]
        x, y, c = _place()
        cps = []
        for p, o_ref in enumerate(o_refs):
            for j, (px, py) in enumerate(_chips_of(x, y)):
                blk = _half_of(o_ref.at[2 * px + py], c, shapes[p])
                cp = _remote(blk, blk, send_sems, recv_sems, 3 * p + j, (x, y, 1 - c))
                cp.start()
                cps.append(cp)
        for p, o_ref in enumerate(o_refs):
            for j, (px, py) in enumerate(_chips_of(x, y)):
                blk = _half_of(o_ref.at[2 * px + py], 1 - c, shapes[p])
                _remote(blk, blk, send_sems, recv_sems, 3 * p + j, (x, y, c)).wait_recv()
        for cp in cps:
            cp.wait_send()

    return pl.pallas_call(
        body, name=name, in_specs=[ANY] * n, out_specs=[ANY] * n, out_shape=[jax.ShapeDtypeStruct(g.shape, g.dtype) for g in gathered],
        input_output_aliases={p: p for p in range(n)},
        scratch_shapes=[pltpu.SemaphoreType.DMA((3 * n,)), pltpu.SemaphoreType.DMA((3 * n,))],
    )(*gathered)


def _scatter_rider(sums):
    def copies(a_refs, o_refs, send_sems, recv_sems):
        x, y, c = _place()
        m = 2 * x + y
        return [_remote(a_ref.at[2 * px + py], o_ref.at[m], send_sems, recv_sems, 3 * p + j, (px, py, c))
                for p, (a_ref, o_ref) in enumerate(zip(a_refs, o_refs)) for j, (px, py) in enumerate(_chips_of(x, y))]

    def start(a_refs, o_refs, send_sems, recv_sems):
        for cp in copies(a_refs, o_refs, send_sems, recv_sems):
            cp.start()

    def finish(a_refs, o_refs, send_sems, recv_sems):
        x, y, c = _place()
        for p, o_ref in enumerate(o_refs):
            for j, (px, py) in enumerate(_chips_of(x, y)):
                blk = o_ref.at[2 * px + py]
                _remote(blk, blk, send_sems, recv_sems, 3 * p + j, (x, y, c)).wait_recv()
        for cp in copies(a_refs, o_refs, send_sems, recv_sems):
            cp.wait_send()

    return _Rider(sums, [jax.ShapeDtypeStruct(a.shape, a.dtype) for a in sums], 3 * len(sums), start, finish)


def _gather_small(v, *, name):
    r, w = v.shape

    def body(v_ref, o_ref, send_sems, recv_sems):
        x, y, c = _place()
        m = 2 * x + y
        chips = [(1 - x, y), (x, 1 - y), (1 - x, 1 - y)]
        o_ref[m] = v_ref[...]
        sends = [_remote(v_ref, o_ref.at[m], send_sems, recv_sems, j, (px, py, c)) for j, (px, py) in enumerate(chips)]
        for cp in sends:
            cp.start()
        for j, (px, py) in enumerate(chips):
            blk = o_ref.at[2 * px + py]
            _remote(blk, blk, send_sems, recv_sems, j, (x, y, c)).wait_recv()
        for cp in sends:
            cp.wait_send()

    return pl.pallas_call(
        body, name=name, in_specs=[VMEM_SPEC], out_specs=VMEM_SPEC, out_shape=jax.ShapeDtypeStruct((4, r, w), v.dtype),
        scratch_shapes=[pltpu.SemaphoreType.DMA((3,)), pltpu.SemaphoreType.DMA((3,))],
    )(v)


def _swap_halves_list(gs, *, name):
    n = len(gs)

    def body(*refs):
        g_refs, o_refs, send_sems, recv_sems = refs[:n], refs[n:2 * n], refs[2 * n], refs[2 * n + 1]
        x, y, c = _place()
        cps = [_remote(_row_half(g_ref, 1 - c, gs[p].shape[1]), o_ref, send_sems, recv_sems, p, (x, y, 1 - c))
               for p, (g_ref, o_ref) in enumerate(zip(g_refs, o_refs))]
        for cp in cps:
            cp.start()
        for cp in cps:
            cp.wait()

    return pl.pallas_call(
        body, name=name, in_specs=[ANY] * n, out_specs=[ANY] * n,
        out_shape=[jax.ShapeDtypeStruct((g.shape[0], g.shape[1] // 2, g.shape[2]), g.dtype) for g in gs],
        scratch_shapes=[pltpu.SemaphoreType.DMA((n,)), pltpu.SemaphoreType.DMA((n,))],
    )(*gs)


def _join_halves_list(reds, *, name):
    n = len(reds)

    def body(*refs):
        o_refs, send_sems, recv_sems = refs[n:2 * n], refs[2 * n], refs[2 * n + 1]
        x, y, c = _place()
        cps = []
        for p, o_ref in enumerate(o_refs):
            rh = reds[p].shape[0] // 2
            mine = o_ref.at[pl.ds(pl.multiple_of(c * rh, SUBLANES), rh)]
            cp = _remote(mine, mine, send_sems, recv_sems, p, (x, y, 1 - c))
            cp.start()
            cps.append(cp)
        for p, o_ref in enumerate(o_refs):
            rh = reds[p].shape[0] // 2
            other = o_ref.at[pl.ds(pl.multiple_of((1 - c) * rh, SUBLANES), rh)]
            _remote(other, other, send_sems, recv_sems, p, (x, y, c)).wait_recv()
        for cp in cps:
            cp.wait_send()

    return pl.pallas_call(
        body, name=name, in_specs=[ANY] * n, out_specs=[ANY] * n, out_shape=[jax.ShapeDtypeStruct(r.shape, r.dtype) for r in reds],
        input_output_aliases={p: p for p in range(n)},
        scratch_shapes=[pltpu.SemaphoreType.DMA((n,)), pltpu.SemaphoreType.DMA((n,))],
    )(*reds)


def _allreduce_small(v, *, name):
    r, w = v.shape

    def body(v_ref, o_ref, slots, send_sems, recv_sems):
        x, y, c = _place()
        me = 4 * x + 2 * y + c
        slots[me] = v_ref[...]
        peers = [((1 - x) if k & 4 else x, (1 - y) if k & 2 else y, (1 - c) if k & 1 else c) for k in range(1, 8)]
        sends = [_remote(v_ref, slots.at[me], send_sems, recv_sems, k, p) for k, p in enumerate(peers)]
        for cp in sends:
            cp.start()
        for k, (px, py, pc) in enumerate(peers):
            blk = slots.at[4 * px + 2 * py + pc]
            _remote(blk, blk, send_sems, recv_sems, k, (x, y, c)).wait_recv()
        for cp in sends:
            cp.wait_send()
        acc = slots[0]
        for k in range(1, 8):
            acc = acc + slots[k]
        o_ref[...] = acc

    return pl.pallas_call(
        body, name=name, in_specs=[VMEM_SPEC], out_specs=VMEM_SPEC, out_shape=jax.ShapeDtypeStruct(v.shape, v.dtype),
        scratch_shapes=[pltpu.VMEM((8, r, w), F32), pltpu.SemaphoreType.DMA((7,)), pltpu.SemaphoreType.DMA((7,))],
    )(v)


def _row_tile(r):
    return r if r <= 512 else _tile(r, (512, 256, 128, 64, 32, 16))


def _add_half(g, recv, place, *, name):
    n, r, w = g.shape
    tm = _row_tile(r // 2)
    nb = (r // 2) // tm

    def body(place_ref, g_ref, r_ref, o_ref):
        o_ref[...] = (g_ref[...].astype(F32) + r_ref[...].astype(F32)).astype(o_ref.dtype)

    return pl.pallas_call(
        body, name=name,
        grid_spec=pltpu.PrefetchScalarGridSpec(
            num_scalar_prefetch=1, grid=(n, nb),
            in_specs=[pl.BlockSpec((None, tm, w), lambda k, i, p: (k, p[1] * nb + i, 0)), pl.BlockSpec((None, tm, w), lambda k, i, p: (k, i, 0))],
            out_specs=pl.BlockSpec((None, tm, w), lambda k, i, p: (k, i, 0))),
        out_shape=jax.ShapeDtypeStruct(recv.shape, g.dtype), compiler_params=_params(("parallel", "parallel")),
    )(place, g, recv)


def _sum_chips(recv, own, place, *, name):
    n, r, w = recv.shape
    tm = _row_tile(r)
    nb = r // tm

    def body(place_ref, *refs):
        own_ref, o_ref = refs[n], refs[n + 1]
        acc = None
        for k in range(n):
            term = jnp.where(place_ref[0] == k, own_ref[...], refs[k][...]).astype(F32)
            acc = term if acc is None else acc + term
        o_ref[...] = acc

    recv_specs = [pl.BlockSpec((None, tm, w), lambda i, p, k=k: (jnp.where(p[0] == k, (k + 1) % n, k), i, 0)) for k in range(n)]
    return pl.pallas_call(
        body, name=name,
        grid_spec=pltpu.PrefetchScalarGridSpec(
            num_scalar_prefetch=1, grid=(nb,),
            in_specs=recv_specs + [pl.BlockSpec((None, tm, w), lambda i, p: (p[0], i, 0))],
            out_specs=pl.BlockSpec((tm, w), lambda i, p: (p[1] * nb + i, 0))),
        out_shape=jax.ShapeDtypeStruct((2 * r, w), F32), compiler_params=_params(("parallel",)),
    )(place, *([recv] * n), own)


def _adamw(w, g, m, v, *, name):
    shape = w.shape
    cols = shape[-1]
    rows = math.prod(shape[:-1])
    tm = _tile(rows, (256, 128, 64, 32, 16, 8))
    c1 = 1.0 - ADAM_B1 ** ADAM_STEP
    c2 = 1.0 - ADAM_B2 ** ADAM_STEP

    def fn(i, nrt, wv, gv, mv, vv):
        mn = ADAM_B1 * mv + (1.0 - ADAM_B1) * gv
        vn = ADAM_B2 * vv + (1.0 - ADAM_B2) * (gv * gv)
        delta = -ADAM_LR * ((mn / c1) / (jnp.sqrt(vn / c2) + ADAM_EPS) + ADAM_WD * wv)
        return (delta, mn, vn), ()

    outs = _rows(fn, name=name, s=rows, tm=tm, ins=[("row", t.reshape(rows, cols), cols, _c0) for t in (w, g, m, v)],
                 outs=[(cols, cols, _c0, F32)] * 3)
    return [o.reshape(shape) for o in outs]


WEIGHTS = ["mix_norm", "ffn_norm", "ffn_w_gu", "ffn_w_down", "conv_w_in", "conv_w_dw", "conv_w_out", "fox_w_in", "fox_b_f", "fox_q_gain",
           "fox_k_gain", "fox_w_out", "ssd_w_in", "ssd_conv_w", "ssd_conv_b", "ssd_dt_bias", "ssd_a_log", "ssd_d", "ssd_norm_w", "ssd_w_out"]
SHARD_AXIS = {"ffn_w_gu": 2, "ffn_w_down": 1, "conv_w_in": 2, "conv_w_dw": 2, "conv_w_out": 1, "fox_w_in": 2, "fox_w_out": 1, "ssd_w_in": 2,
              "ssd_conv_w": 2, "ssd_conv_b": 1, "ssd_norm_w": 1, "ssd_w_out": 1}
BIG = ["ffn_w_gu", "ffn_w_down", "conv_w_in", "conv_w_out", "fox_w_in", "fox_w_out", "ssd_w_in", "ssd_w_out"]
SMALL_SHARDED = ["conv_w_dw", "ssd_conv_w", "ssd_conv_b", "ssd_norm_w"]
N_CHIPS = 4


def _pack_flat(parts, pad_to):
    flat = [p.reshape(-1) for p in parts]
    offs, n = [], 0
    for f in flat:
        offs.append(n)
        n += f.shape[0]
    total = -(-n // pad_to) * pad_to
    if total > n:
        flat.append(jnp.zeros((total - n,), flat[0].dtype))
    return jnp.concatenate(flat).reshape(-1, LANES), offs


def kernel(x, mix_norm, ffn_norm, ffn_w_gu, ffn_w_down, conv_w_in, conv_w_dw, conv_w_out, fox_w_in, fox_b_f, fox_q_gain, fox_k_gain, fox_w_out, ssd_w_in, ssd_conv_w, ssd_conv_b, ssd_dt_bias, ssd_a_log, ssd_d, ssd_norm_w, ssd_w_out, loss_target, m_mix_norm, m_ffn_norm, m_ffn_w_gu, m_ffn_w_down, m_conv_w_in, m_conv_w_dw, m_conv_w_out, m_fox_w_in, m_fox_b_f, m_fox_q_gain, m_fox_k_gain, m_fox_w_out, m_ssd_w_in, m_ssd_conv_w, m_ssd_conv_b, m_ssd_dt_bias, m_ssd_a_log, m_ssd_d, m_ssd_norm_w, m_ssd_w_out, v_mix_norm, v_ffn_norm, v_ffn_w_gu, v_ffn_w_down, v_conv_w_in, v_conv_w_dw, v_conv_w_out, v_fox_w_in, v_fox_b_f, v_fox_q_gain, v_fox_k_gain, v_fox_w_out, v_ssd_w_in, v_ssd_conv_w, v_ssd_conv_b, v_ssd_dt_bias, v_ssd_a_log, v_ssd_d, v_ssd_norm_w, v_ssd_w_out):
    w = dict(zip(WEIGHTS, (mix_norm, ffn_norm, ffn_w_gu, ffn_w_down, conv_w_in, conv_w_dw, conv_w_out, fox_w_in, fox_b_f, fox_q_gain, fox_k_gain,
                           fox_w_out, ssd_w_in, ssd_conv_w, ssd_conv_b, ssd_dt_bias, ssd_a_log, ssd_d, ssd_norm_w, ssd_w_out)))
    m1 = dict(zip(WEIGHTS, (m_mix_norm, m_ffn_norm, m_ffn_w_gu, m_ffn_w_down, m_conv_w_in, m_conv_w_dw, m_conv_w_out, m_fox_w_in, m_fox_b_f,
                            m_fox_q_gain, m_fox_k_gain, m_fox_w_out, m_ssd_w_in, m_ssd_conv_w, m_ssd_conv_b, m_ssd_dt_bias, m_ssd_a_log, m_ssd_d,
                            m_ssd_norm_w, m_ssd_w_out)))
    m2 = dict(zip(WEIGHTS, (v_mix_norm, v_ffn_norm, v_ffn_w_gu, v_ffn_w_down, v_conv_w_in, v_conv_w_dw, v_conv_w_out, v_fox_w_in, v_fox_b_f,
                            v_fox_q_gain, v_fox_k_gain, v_fox_w_out, v_ssd_w_in, v_ssd_conv_w, v_ssd_conv_b, v_ssd_dt_bias, v_ssd_a_log, v_ssd_d,
                            v_ssd_norm_w, v_ssd_w_out)))
    cx, cy, cc = _place()
    chip = 2 * cx + cy

    place = jnp.stack([chip, cc]).astype(jnp.int32)
    depth = mix_norm.shape[0]
    attention = next((i for i in range(depth) if i % 3 == 1), depth)

    def carrier_of(n, layer):
        if layer == 0:
            return ("mixer", 0) if n.startswith("ffn") else None
        return ("ffn", layer - 1) if layer <= attention else ("attention", attention)

    early, late, cm = {}, {}, {n: [] for n in BIG}
    for n in BIG:
        wb = w[n].astype(WIRE_DTYPE)
        keys = [carrier_of(n, _model_layer(n, l)) for l in range(wb.shape[0])]
        for key in dict.fromkeys(keys):
            first, count = keys.index(key), keys.count(key)
            if key is None:
                early[n] = wb[first:first + count]
            else:
                late.setdefault(key, {})[n] = (wb[first:first + count], first)
    gathered = _with_own(_gather_list(list(early.values()), name="gather_weights"), list(early.values()), chip)
    for n, g_ in zip(early, gathered):
        cm[n].append((g_, 0))
    sp, soffs = _pack_flat([w[n] for n in SMALL_SHARDED], SUBLANES * LANES)
    sgath = _gather_small(sp, name="gather_small").reshape(N_CHIPS, -1)
    full = {n: w[n] for n in WEIGHTS if n not in SHARD_AXIS}
    for n, off in zip(SMALL_SHARDED, soffs):
        full[n] = jnp.concatenate([sgath[j, off:off + w[n].size].reshape(w[n].shape) for j in range(N_CHIPS)], axis=SHARD_AXIS[n])

    loss, gx, grads, reds = _local_step(x[0], loss_target[0], full, cm, late, place)
    loss = lax.psum(loss, ("x", "y", "c"))

    small_names = [n for n in WEIGHTS if n not in BIG]
    sm, smoffs = _pack_flat([grads[n] for n in small_names], SUBLANES * LANES)
    sred = _allreduce_small(sm, name="allreduce_small").reshape(-1)

    g = {n: jnp.stack([reds[(n, l)] for l in range(w[n].shape[0])]).reshape(w[n].shape) for n in BIG}
    for n, off in zip(small_names, smoffs):
        fullg = sred[off:off + grads[n].size].reshape(grads[n].shape)
        if n in SHARD_AXIS:
            ax = SHARD_AXIS[n]
            fullg = lax.dynamic_slice_in_dim(fullg, chip * w[n].shape[ax], w[n].shape[ax], axis=ax)
        g[n] = fullg

    deltas, new_m, new_v = [], [], []
    for n in WEIGHTS:
        dl, mn, vn = _adamw(w[n], g[n], m1[n], m2[n], name=f"adamw_{n}")
        deltas.append(dl)
        new_m.append(mn)
        new_v.append(vn)
    return (loss, gx[None], *[g[n] for n in WEIGHTS], *deltas, *new_m, *new_v)
```
